```python
import jax, jax.numpy as jnp
from jax import lax
import numpy as np

D_MODEL = 1024
BATCH = 8
SEQ = 4096
DEPTH = 4

N_A_LAYERS = DEPTH // 2
N_B_LAYERS = DEPTH - N_A_LAYERS
POOL_WINDOWS = (2, 4, 8, 16)
N_POOL_GROUPS = len(POOL_WINDOWS)
POOL_GROUP = D_MODEL // N_POOL_GROUPS
N_HEADS = 8
QK_NOPE = 128
QK_ROPE = 64
V_HEAD = 128
QK_HEAD = QK_NOPE + QK_ROPE
Q_RANK = 3 * D_MODEL // 8
KV_RANK = D_MODEL // 4
ROPE_THETA = 10000.0
Q_BLOCK = 128
D_FF = ((8 * D_MODEL // 3 + 127) // 128) * 128
CONV_WIDTH = 3
EPS = 1e-6
N_MOD = 6
MAX_POS_OFFSET = 1024

kernel_name = "yoco_pool_mla_adaln_convglu"


def rmsnorm(x, g):
    x32 = x.astype(jnp.float32)
    y = x32 * lax.rsqrt(jnp.mean(x32 * x32, axis=-1, keepdims=True) + EPS)
    return y.astype(x.dtype) * g


def modulate(h, shift, scale):
    return h * (1 + scale[:, None, :]) + shift[:, None, :]


def rope_tables(positions):
    inv = 1.0 / (ROPE_THETA ** (jnp.arange(0, QK_ROPE, 2, dtype=jnp.float32) / QK_ROPE))
    ang = positions.astype(jnp.float32)[..., None] * inv
    return jnp.cos(ang), jnp.sin(ang)


def apply_rope(x, cos, sin):
    x32 = x.astype(jnp.float32)
    x1, x2 = jnp.split(x32, 2, axis=-1)
    out = jnp.concatenate([x1 * cos - x2 * sin, x2 * cos + x1 * sin], axis=-1)
    return out.astype(x.dtype)


def trailing_mean_minus_self(h, w):
    s = h.shape[1]
    h32 = h.astype(jnp.float32)
    cs = jnp.cumsum(h32, axis=1)
    cs_lag = jnp.pad(cs, ((0, 0), (w, 0), (0, 0)))[:, :s]
    count = jnp.minimum(jnp.arange(1, s + 1, dtype=jnp.float32), float(w))
    mean = (cs - cs_lag) / count[None, :, None]
    return (mean - h32).astype(h.dtype)


def pool_mixer(h, w_g, b_g, scale):
    bsz, s, d = h.shape
    hg = h.reshape(bsz, s, N_POOL_GROUPS, POOL_GROUP)
    pooled = jnp.stack([trailing_mean_minus_self(hg[:, :, g], POOL_WINDOWS[g])
                        for g in range(N_POOL_GROUPS)], axis=2)
    y = jnp.einsum('bsgc,gcd->bsgd', pooled, w_g).reshape(bsz, s, d) + b_g
    return y * scale


def conv_glu_ffn(h, w_up, conv_w, conv_b, w_down):
    s = h.shape[1]
    a, v = jnp.split(h @ w_up, 2, axis=-1)
    ap = jnp.pad(a, ((0, 0), (CONV_WIDTH - 1, 0), (0, 0)))
    a = sum(ap[:, k:k + s] * conv_w[k] for k in range(CONV_WIDTH)) + conv_b
    return (jax.nn.gelu(a, approximate=False) * v) @ w_down


def shared_kv(x, kv_in_g, w_dkv, ckv_norm_g, w_uk, w_uv, cos, sin):
    bsz, s, _ = x.shape
    kv = rmsnorm(x, kv_in_g) @ w_dkv
    c_kv = rmsnorm(kv[..., :KV_RANK], ckv_norm_g)
    k_rope = apply_rope(kv[..., KV_RANK:], cos, sin)
    k_nope = (c_kv @ w_uk).reshape(bsz, s, N_HEADS, QK_NOPE)
    k = jnp.concatenate([k_nope, jnp.broadcast_to(k_rope[:, :, None, :],
                                                  (bsz, s, N_HEADS, QK_ROPE))], axis=-1)
    v = (c_kv @ w_uv).reshape(bsz, s, N_HEADS, V_HEAD)
    return k, v


def causal_block_attention(q, k, v):
    s = q.shape[1]
    q = q * (QK_HEAD ** -0.5)
    outs = []
    for i in range(s // Q_BLOCK):
        q0 = i * Q_BLOCK
        k_end = q0 + Q_BLOCK
        sc = jnp.einsum('bqhd,bkhd->bhqk', q[:, q0:k_end], k[:, :k_end],
                        preferred_element_type=jnp.float32)
        mask = jnp.arange(k_end)[None, :] <= jnp.arange(q0, k_end)[:, None]
        sc = jnp.where(mask, sc, jnp.finfo(jnp.float32).min)
        p = jax.nn.softmax(sc, axis=-1).astype(v.dtype)
        outs.append(jnp.einsum('bhqk,bkhd->bqhd', p, v[:, :k_end]))
    return jnp.concatenate(outs, axis=1)


def mla_mixer(h, k, v, w_dq, q_norm_g, w_uq, w_o, cos, sin):
    bsz, s, _ = h.shape
    c_q = rmsnorm(h @ w_dq, q_norm_g)
    q = (c_q @ w_uq).reshape(bsz, s, N_HEADS, QK_HEAD)
    q = jnp.concatenate([q[..., :QK_NOPE],
                         apply_rope(q[..., QK_NOPE:], cos[:, :, None], sin[:, :, None])], axis=-1)
    o = causal_block_attention(q, k, v)
    return o.reshape(bsz, s, N_HEADS * V_HEAD) @ w_o


def _fwd_setup_inputs(seed: int = 0) -> dict:
    key = jax.random.key(seed)
    ks = jax.random.split(key, 26)
    f32 = jnp.float32
    nrm = lambda k, shape, s: jax.random.normal(k, shape, f32) * s
    d, f = D_MODEL, D_FF
    positions = (jnp.arange(SEQ, dtype=jnp.int32)[None, :]
                 + jax.random.randint(ks[2], (BATCH, 1), 0, MAX_POS_OFFSET, dtype=jnp.int32))
    return {
        "x": nrm(ks[0], (BATCH, SEQ, d), 1.0),
        "c": nrm(ks[1], (BATCH, d), 1.0),
        "positions": positions,
        "mod_w": nrm(ks[3], (DEPTH, d, N_MOD * d), d ** -0.5),
        "mod_b": nrm(ks[4], (DEPTH, N_MOD * d), 0.01),
        "norm1_g": 1.0 + nrm(ks[5], (DEPTH, d), 0.02),
        "norm2_g": 1.0 + nrm(ks[6], (DEPTH, d), 0.02),
        "pool_w": nrm(ks[7], (N_A_LAYERS, N_POOL_GROUPS, POOL_GROUP, POOL_GROUP), POOL_GROUP ** -0.5),
        "pool_b": nrm(ks[8], (N_A_LAYERS, d), 0.01),
        "pool_scale": 1.0 + nrm(ks[9], (N_A_LAYERS, d), 0.1),
        "kv_in_g": 1.0 + nrm(ks[10], (d,), 0.02),
        "w_dkv": nrm(ks[11], (d, KV_RANK + QK_ROPE), d ** -0.5),
        "ckv_norm_g": 1.0 + nrm(ks[12], (KV_RANK,), 0.02),
        "w_uk": nrm(ks[13], (KV_RANK, N_HEADS * QK_NOPE), KV_RANK ** -0.5),
        "w_uv": nrm(ks[14], (KV_RANK, N_HEADS * V_HEAD), KV_RANK ** -0.5),
        "w_dq": nrm(ks[15], (N_B_LAYERS, d, Q_RANK), d ** -0.5),
        "q_norm_g": 1.0 + nrm(ks[16], (N_B_LAYERS, Q_RANK), 0.02),
        "w_uq": nrm(ks[17], (N_B_LAYERS, Q_RANK, N_HEADS * QK_HEAD), Q_RANK ** -0.5),
        "w_o": nrm(ks[18], (N_B_LAYERS, N_HEADS * V_HEAD, d), (N_HEADS * V_HEAD) ** -0.5),
        "w_up": nrm(ks[19], (DEPTH, d, 2 * f), d ** -0.5),
        "conv_w": nrm(ks[20], (DEPTH, CONV_WIDTH, f), CONV_WIDTH ** -0.5),
        "conv_b": nrm(ks[21], (DEPTH, f), 0.01),
        "w_down": nrm(ks[22], (DEPTH, f, d), f ** -0.5),
        "final_g": 1.0 + nrm(ks[23], (d,), 0.02),
    }


def _fwd_reference(x, c, positions, mod_w, mod_b, norm1_g, norm2_g, pool_w, pool_b, pool_scale,
              kv_in_g, w_dkv, ckv_norm_g, w_uk, w_uv, w_dq, q_norm_g, w_uq, w_o,
              w_up, conv_w, conv_b, w_down, final_g):
    cos, sin = rope_tables(positions)
    mods = jnp.einsum('bd,lde->lbe', jax.nn.silu(c), mod_w) + mod_b[:, None, :]
    k = v = None
    for l in range(DEPTH):
        sh1, sc1, g1, sh2, sc2, g2 = jnp.split(mods[l], N_MOD, axis=-1)
        h = modulate(rmsnorm(x, norm1_g[l]), sh1, sc1)
        if l < N_A_LAYERS:
            y = pool_mixer(h, pool_w[l], pool_b[l], pool_scale[l])
        else:
            j = l - N_A_LAYERS
            y = mla_mixer(h, k, v, w_dq[j], q_norm_g[j], w_uq[j], w_o[j], cos, sin)
        x = x + g1[:, None, :] * y
        h = modulate(rmsnorm(x, norm2_g[l]), sh2, sc2)
        x = x + g2[:, None, :] * conv_glu_ffn(h, w_up[l], conv_w[l], conv_b[l], w_down[l])
        if l == N_A_LAYERS - 1:
            k, v = shared_kv(x, kv_in_g, w_dkv, ckv_norm_g, w_uk, w_uv, cos, sin)
    return rmsnorm(x, final_g)


import jax as _jax
import jax.numpy as _jnp

TWIN_FORMAT = 'train_step'
FWD_PARAMS = ['x', 'c', 'positions', 'mod_w', 'mod_b', 'norm1_g', 'norm2_g', 'pool_w', 'pool_b', 'pool_scale', 'kv_in_g', 'w_dkv', 'ckv_norm_g', 'w_uk', 'w_uv', 'w_dq', 'q_norm_g', 'w_uq', 'w_o', 'w_up', 'conv_w', 'conv_b', 'w_down', 'final_g']
TWIN_WEIGHTS = ['mod_w', 'mod_b', 'norm1_g', 'norm2_g', 'pool_w', 'pool_b', 'pool_scale', 'kv_in_g', 'w_dkv', 'ckv_norm_g', 'w_uk', 'w_uv', 'w_dq', 'q_norm_g', 'w_uq', 'w_o', 'w_up', 'conv_w', 'conv_b', 'w_down', 'final_g']
TWIN_DIFF_INPUT = 'x'
TWIN_INPUTS = ['x', 'c', 'positions', 'mod_w', 'mod_b', 'norm1_g', 'norm2_g', 'pool_w', 'pool_b', 'pool_scale', 'kv_in_g', 'w_dkv', 'ckv_norm_g', 'w_uk', 'w_uv', 'w_dq', 'q_norm_g', 'w_uq', 'w_o', 'w_up', 'conv_w', 'conv_b', 'w_down', 'final_g', 'loss_target', 'm_mod_w', 'm_mod_b', 'm_norm1_g', 'm_norm2_g', 'm_pool_w', 'm_pool_b', 'm_pool_scale', 'm_kv_in_g', 'm_w_dkv', 'm_ckv_norm_g', 'm_w_uk', 'm_w_uv', 'm_w_dq', 'm_q_norm_g', 'm_w_uq', 'm_w_o', 'm_w_up', 'm_conv_w', 'm_conv_b', 'm_w_down', 'm_final_g', 'v_mod_w', 'v_mod_b', 'v_norm1_g', 'v_norm2_g', 'v_pool_w', 'v_pool_b', 'v_pool_scale', 'v_kv_in_g', 'v_w_dkv', 'v_ckv_norm_g', 'v_w_uk', 'v_w_uv', 'v_w_dq', 'v_q_norm_g', 'v_w_uq', 'v_w_o', 'v_w_up', 'v_conv_w', 'v_conv_b', 'v_w_down', 'v_final_g']
TWIN_OUTPUTS = ['loss', 'grad_x', 'grad_mod_w', 'grad_mod_b', 'grad_norm1_g', 'grad_norm2_g', 'grad_pool_w', 'grad_pool_b', 'grad_pool_scale', 'grad_kv_in_g', 'grad_w_dkv', 'grad_ckv_norm_g', 'grad_w_uk', 'grad_w_uv', 'grad_w_dq', 'grad_q_norm_g', 'grad_w_uq', 'grad_w_o', 'grad_w_up', 'grad_conv_w', 'grad_conv_b', 'grad_w_down', 'grad_final_g', 'delta_mod_w', 'delta_mod_b', 'delta_norm1_g', 'delta_norm2_g', 'delta_pool_w', 'delta_pool_b', 'delta_pool_scale', 'delta_kv_in_g', 'delta_w_dkv', 'delta_ckv_norm_g', 'delta_w_uk', 'delta_w_uv', 'delta_w_dq', 'delta_q_norm_g', 'delta_w_uq', 'delta_w_o', 'delta_w_up', 'delta_conv_w', 'delta_conv_b', 'delta_w_down', 'delta_final_g', 'new_m_mod_w', 'new_m_mod_b', 'new_m_norm1_g', 'new_m_norm2_g', 'new_m_pool_w', 'new_m_pool_b', 'new_m_pool_scale', 'new_m_kv_in_g', 'new_m_w_dkv', 'new_m_ckv_norm_g', 'new_m_w_uk', 'new_m_w_uv', 'new_m_w_dq', 'new_m_q_norm_g', 'new_m_w_uq', 'new_m_w_o', 'new_m_w_up', 'new_m_conv_w', 'new_m_conv_b', 'new_m_w_down', 'new_m_final_g', 'new_v_mod_w', 'new_v_mod_b', 'new_v_norm1_g', 'new_v_norm2_g', 'new_v_pool_w', 'new_v_pool_b', 'new_v_pool_scale', 'new_v_kv_in_g', 'new_v_w_dkv', 'new_v_ckv_norm_g', 'new_v_w_uk', 'new_v_w_uv', 'new_v_w_dq', 'new_v_q_norm_g', 'new_v_w_uq', 'new_v_w_o', 'new_v_w_up', 'new_v_conv_w', 'new_v_conv_b', 'new_v_w_down', 'new_v_final_g']
TWIN_LEAF_KINDS = {'loss': 'loss', 'grad_x': 'grad_x', 'grad_mod_w': 'grad_w', 'grad_mod_b': 'grad_w', 'grad_norm1_g': 'grad_w', 'grad_norm2_g': 'grad_w', 'grad_pool_w': 'grad_w', 'grad_pool_b': 'grad_w', 'grad_pool_scale': 'grad_w', 'grad_kv_in_g': 'grad_w', 'grad_w_dkv': 'grad_w', 'grad_ckv_norm_g': 'grad_w', 'grad_w_uk': 'grad_w', 'grad_w_uv': 'grad_w', 'grad_w_dq': 'grad_w', 'grad_q_norm_g': 'grad_w', 'grad_w_uq': 'grad_w', 'grad_w_o': 'grad_w', 'grad_w_up': 'grad_w', 'grad_conv_w': 'grad_w', 'grad_conv_b': 'grad_w', 'grad_w_down': 'grad_w', 'grad_final_g': 'grad_w', 'delta_mod_w': 'delta_w', 'delta_mod_b': 'delta_w', 'delta_norm1_g': 'delta_w', 'delta_norm2_g': 'delta_w', 'delta_pool_w': 'delta_w', 'delta_pool_b': 'delta_w', 'delta_pool_scale': 'delta_w', 'delta_kv_in_g': 'delta_w', 'delta_w_dkv': 'delta_w', 'delta_ckv_norm_g': 'delta_w', 'delta_w_uk': 'delta_w', 'delta_w_uv': 'delta_w', 'delta_w_dq': 'delta_w', 'delta_q_norm_g': 'delta_w', 'delta_w_uq': 'delta_w', 'delta_w_o': 'delta_w', 'delta_w_up': 'delta_w', 'delta_conv_w': 'delta_w', 'delta_conv_b': 'delta_w', 'delta_w_down': 'delta_w', 'delta_final_g': 'delta_w', 'new_m_mod_w': 'new_m', 'new_m_mod_b': 'new_m', 'new_m_norm1_g': 'new_m', 'new_m_norm2_g': 'new_m', 'new_m_pool_w': 'new_m', 'new_m_pool_b': 'new_m', 'new_m_pool_scale': 'new_m', 'new_m_kv_in_g': 'new_m', 'new_m_w_dkv': 'new_m', 'new_m_ckv_norm_g': 'new_m', 'new_m_w_uk': 'new_m', 'new_m_w_uv': 'new_m', 'new_m_w_dq': 'new_m', 'new_m_q_norm_g': 'new_m', 'new_m_w_uq': 'new_m', 'new_m_w_o': 'new_m', 'new_m_w_up': 'new_m', 'new_m_conv_w': 'new_m', 'new_m_conv_b': 'new_m', 'new_m_w_down': 'new_m', 'new_m_final_g': 'new_m', 'new_v_mod_w': 'new_v', 'new_v_mod_b': 'new_v', 'new_v_norm1_g': 'new_v', 'new_v_norm2_g': 'new_v', 'new_v_pool_w': 'new_v', 'new_v_pool_b': 'new_v', 'new_v_pool_scale': 'new_v', 'new_v_kv_in_g': 'new_v', 'new_v_w_dkv': 'new_v', 'new_v_ckv_norm_g': 'new_v', 'new_v_w_uk': 'new_v', 'new_v_w_uv': 'new_v', 'new_v_w_dq': 'new_v', 'new_v_q_norm_g': 'new_v', 'new_v_w_uq': 'new_v', 'new_v_w_o': 'new_v', 'new_v_w_up': 'new_v', 'new_v_conv_w': 'new_v', 'new_v_conv_b': 'new_v', 'new_v_w_down': 'new_v', 'new_v_final_g': 'new_v'}


def _forward(args):
    return _fwd_reference(*[args[k] for k in FWD_PARAMS])


def _output_shape():
    out = _jax.eval_shape(lambda: _forward(_fwd_setup_inputs(0)))
    return out.shape, out.dtype

N_MICROBATCH = 1
ADAM_LR = 0.001
ADAM_B1 = 0.9
ADAM_B2 = 0.999
ADAM_EPS = 1e-08
ADAM_WD = 0.01
ADAM_STEP = 10
PER_EXAMPLE_BATCH_AXIS = {'x': 0, 'c': 0, 'positions': 0, 'loss_target': 0}
SHARED_INPUTS = []
_WEIGHT_DTYPES = {'mod_w': _jnp.float32, 'mod_b': _jnp.float32, 'norm1_g': _jnp.float32, 'norm2_g': _jnp.float32, 'pool_w': _jnp.float32, 'pool_b': _jnp.float32, 'pool_scale': _jnp.float32, 'kv_in_g': _jnp.float32, 'w_dkv': _jnp.float32, 'ckv_norm_g': _jnp.float32, 'w_uk': _jnp.float32, 'w_uv': _jnp.float32, 'w_dq': _jnp.float32, 'q_norm_g': _jnp.float32, 'w_uq': _jnp.float32, 'w_o': _jnp.float32, 'w_up': _jnp.float32, 'conv_w': _jnp.float32, 'conv_b': _jnp.float32, 'w_down': _jnp.float32, 'final_g': _jnp.float32}
MOMENT_SCALE = {'mod_w': 6.165490e-02, 'mod_b': 9.957571e-02, 'norm1_g': 7.453140e-02, 'norm2_g': 1.061671e-01, 'pool_w': 1.048439e-01, 'pool_b': 1.012188e-01, 'pool_scale': 1.417796e-01, 'kv_in_g': 2.964222e-02, 'w_dkv': 5.317209e-02, 'ckv_norm_g': 5.962641e-02, 'w_uk': 1.359236e-02, 'w_uv': 2.538144e-02, 'w_dq': 1.894897e-02, 'q_norm_g': 1.849604e-02, 'w_uq': 9.369141e-03, 'w_o': 1.803635e-02, 'w_up': 5.193049e-02, 'conv_w': 5.278784e-02, 'conv_b': 3.874786e-02, 'w_down': 8.562155e-02, 'final_g': 3.275438e+01}


def _to_microbatches(a, axis):
    t = _jnp.moveaxis(a, axis, 0)
    t = t.reshape((N_MICROBATCH, t.shape[0] // N_MICROBATCH) + t.shape[1:])
    return _jnp.moveaxis(t, 1, axis + 1)


def setup_inputs(seed: int = 0) -> dict:
    inp = _fwd_setup_inputs(seed)
    key = _jax.random.fold_in(_jax.random.key(seed), 7919)
    shape, _ = _output_shape()
    out = dict(inp)
    out["loss_target"] = _jax.random.normal(_jax.random.fold_in(key, 0), shape, _jnp.float32)
    for i, name in enumerate(TWIN_WEIGHTS):
        w = inp[name].astype(_jnp.float32)
        if MOMENT_SCALE is None:
            s = _jnp.sqrt(_jnp.mean(_jnp.square(w)) + 1e-30)
        else:
            s = MOMENT_SCALE[name]
        km, kv = _jax.random.split(_jax.random.fold_in(key, i + 1))
        out[name] = w
        out["m_" + name] = s * _jax.random.normal(km, w.shape, _jnp.float32)
        out["v_" + name] = (s * s) * _jax.random.uniform(kv, w.shape, _jnp.float32, 0.5, 1.5)
    if N_MICROBATCH > 1:
        for name, axis in PER_EXAMPLE_BATCH_AXIS.items():
            out[name] = _to_microbatches(out[name], axis)
    return {'x': out['x'], 'c': out['c'], 'positions': out['positions'], 'mod_w': out['mod_w'], 'mod_b': out['mod_b'], 'norm1_g': out['norm1_g'], 'norm2_g': out['norm2_g'], 'pool_w': out['pool_w'], 'pool_b': out['pool_b'], 'pool_scale': out['pool_scale'], 'kv_in_g': out['kv_in_g'], 'w_dkv': out['w_dkv'], 'ckv_norm_g': out['ckv_norm_g'], 'w_uk': out['w_uk'], 'w_uv': out['w_uv'], 'w_dq': out['w_dq'], 'q_norm_g': out['q_norm_g'], 'w_uq': out['w_uq'], 'w_o': out['w_o'], 'w_up': out['w_up'], 'conv_w': out['conv_w'], 'conv_b': out['conv_b'], 'w_down': out['w_down'], 'final_g': out['final_g'], 'loss_target': out['loss_target'], 'm_mod_w': out['m_mod_w'], 'm_mod_b': out['m_mod_b'], 'm_norm1_g': out['m_norm1_g'], 'm_norm2_g': out['m_norm2_g'], 'm_pool_w': out['m_pool_w'], 'm_pool_b': out['m_pool_b'], 'm_pool_scale': out['m_pool_scale'], 'm_kv_in_g': out['m_kv_in_g'], 'm_w_dkv': out['m_w_dkv'], 'm_ckv_norm_g': out['m_ckv_norm_g'], 'm_w_uk': out['m_w_uk'], 'm_w_uv': out['m_w_uv'], 'm_w_dq': out['m_w_dq'], 'm_q_norm_g': out['m_q_norm_g'], 'm_w_uq': out['m_w_uq'], 'm_w_o': out['m_w_o'], 'm_w_up': out['m_w_up'], 'm_conv_w': out['m_conv_w'], 'm_conv_b': out['m_conv_b'], 'm_w_down': out['m_w_down'], 'm_final_g': out['m_final_g'], 'v_mod_w': out['v_mod_w'], 'v_mod_b': out['v_mod_b'], 'v_norm1_g': out['v_norm1_g'], 'v_norm2_g': out['v_norm2_g'], 'v_pool_w': out['v_pool_w'], 'v_pool_b': out['v_pool_b'], 'v_pool_scale': out['v_pool_scale'], 'v_kv_in_g': out['v_kv_in_g'], 'v_w_dkv': out['v_w_dkv'], 'v_ckv_norm_g': out['v_ckv_norm_g'], 'v_w_uk': out['v_w_uk'], 'v_w_uv': out['v_w_uv'], 'v_w_dq': out['v_w_dq'], 'v_q_norm_g': out['v_q_norm_g'], 'v_w_uq': out['v_w_uq'], 'v_w_o': out['v_w_o'], 'v_w_up': out['v_w_up'], 'v_conv_w': out['v_conv_w'], 'v_conv_b': out['v_conv_b'], 'v_w_down': out['v_w_down'], 'v_final_g': out['v_final_g']}


def _loss(weights, diff, rest, loss_target):
    with _jax.named_scope("forward"):
        args = {**rest, TWIN_DIFF_INPUT: diff, **{k: w.astype(_WEIGHT_DTYPES[k]) for k, w in weights.items()}}
        y = _forward(args)
    with _jax.named_scope("loss_head"):
        err = _jnp.square(y.astype(_jnp.float32) - loss_target)
        return 0.5 * _jnp.sum(_jnp.mean(err, axis=-1)) if err.ndim else 0.5 * err


def _adamw(w, g, m, v):
    m = ADAM_B1 * m + (1.0 - ADAM_B1) * g
    v = ADAM_B2 * v + (1.0 - ADAM_B2) * _jnp.square(g)
    m_hat = m / (1.0 - ADAM_B1 ** ADAM_STEP)
    v_hat = v / (1.0 - ADAM_B2 ** ADAM_STEP)
    delta = -ADAM_LR * (m_hat / (_jnp.sqrt(v_hat) + ADAM_EPS) + ADAM_WD * w)
    return delta, m, v


def reference(x, c, positions, mod_w, mod_b, norm1_g, norm2_g, pool_w, pool_b, pool_scale, kv_in_g, w_dkv, ckv_norm_g, w_uk, w_uv, w_dq, q_norm_g, w_uq, w_o, w_up, conv_w, conv_b, w_down, final_g, loss_target, m_mod_w, m_mod_b, m_norm1_g, m_norm2_g, m_pool_w, m_pool_b, m_pool_scale, m_kv_in_g, m_w_dkv, m_ckv_norm_g, m_w_uk, m_w_uv, m_w_dq, m_q_norm_g, m_w_uq, m_w_o, m_w_up, m_conv_w, m_conv_b, m_w_down, m_final_g, v_mod_w, v_mod_b, v_norm1_g, v_norm2_g, v_pool_w, v_pool_b, v_pool_scale, v_kv_in_g, v_w_dkv, v_ckv_norm_g, v_w_uk, v_w_uv, v_w_dq, v_q_norm_g, v_w_uq, v_w_o, v_w_up, v_conv_w, v_conv_b, v_w_down, v_final_g):
    given = dict(x=x, c=c, positions=positions, mod_w=mod_w, mod_b=mod_b, norm1_g=norm1_g, norm2_g=norm2_g, pool_w=pool_w, pool_b=pool_b, pool_scale=pool_scale, kv_in_g=kv_in_g, w_dkv=w_dkv, ckv_norm_g=ckv_norm_g, w_uk=w_uk, w_uv=w_uv, w_dq=w_dq, q_norm_g=q_norm_g, w_uq=w_uq, w_o=w_o, w_up=w_up, conv_w=conv_w, conv_b=conv_b, w_down=w_down, final_g=final_g, loss_target=loss_target, m_mod_w=m_mod_w, m_mod_b=m_mod_b, m_norm1_g=m_norm1_g, m_norm2_g=m_norm2_g, m_pool_w=m_pool_w, m_pool_b=m_pool_b, m_pool_scale=m_pool_scale, m_kv_in_g=m_kv_in_g, m_w_dkv=m_w_dkv, m_ckv_norm_g=m_ckv_norm_g, m_w_uk=m_w_uk, m_w_uv=m_w_uv, m_w_dq=m_w_dq, m_q_norm_g=m_q_norm_g, m_w_uq=m_w_uq, m_w_o=m_w_o, m_w_up=m_w_up, m_conv_w=m_conv_w, m_conv_b=m_conv_b, m_w_down=m_w_down, m_final_g=m_final_g, v_mod_w=v_mod_w, v_mod_b=v_mod_b, v_norm1_g=v_norm1_g, v_norm2_g=v_norm2_g, v_pool_w=v_pool_w, v_pool_b=v_pool_b, v_pool_scale=v_pool_scale, v_kv_in_g=v_kv_in_g, v_w_dkv=v_w_dkv, v_ckv_norm_g=v_ckv_norm_g, v_w_uk=v_w_uk, v_w_uv=v_w_uv, v_w_dq=v_w_dq, v_q_norm_g=v_q_norm_g, v_w_uq=v_w_uq, v_w_o=v_w_o, v_w_up=v_w_up, v_conv_w=v_conv_w, v_conv_b=v_conv_b, v_w_down=v_w_down, v_final_g=v_final_g)
    weights = {n: given[n] for n in TWIN_WEIGHTS}
    shared = {n: given[n] for n in SHARED_INPUTS}
    per_example = {n: given[n] for n in ['x', 'c', 'positions']}
    grad_fn = _jax.value_and_grad(_loss, argnums=(0, 1))

    def one_microbatch(ex, loss_target):
        ex = dict(ex)
        diff = ex.pop(TWIN_DIFF_INPUT)
        return grad_fn(weights, diff, {**shared, **ex}, loss_target)

    if N_MICROBATCH == 1:
        loss, (grad_w, grad_x) = one_microbatch(per_example, given["loss_target"])
    else:
        def body(carry, xs):
            loss_sum, grad_sum = carry
            l_k, (gw_k, gx_k) = one_microbatch(xs[0], xs[1])
            with _jax.named_scope("update"):
                return (loss_sum + l_k, _jax.tree.map(_jnp.add, grad_sum, gw_k)), gx_k

        init = (_jnp.zeros((), _jnp.float32), _jax.tree.map(_jnp.zeros_like, weights))
        (loss, grad_w), grad_x = _jax.lax.scan(body, init, (per_example, given["loss_target"]))
    with _jax.named_scope("update"):
        delta_w, new_m, new_v = {}, {}, {}
        for n in TWIN_WEIGHTS:
            delta_w[n], new_m[n], new_v[n] = _adamw(weights[n], grad_w[n], given["m_" + n], given["v_" + n])
    return (loss, grad_x, *[grad_w[n] for n in TWIN_WEIGHTS], *[delta_w[n] for n in TWIN_WEIGHTS],
            *[new_m[n] for n in TWIN_WEIGHTS], *[new_v[n] for n in TWIN_WEIGHTS])
```

```python
import functools
import math

import jax
import jax.numpy as jnp
from jax import lax
from jax.experimental import pallas as pl
from jax.experimental.pallas import tpu as pltpu

F32 = jnp.float32
BF16 = jnp.bfloat16
MESH = pl.DeviceIdType.MESH

DEPTH = 4
N_A = 2
POOL_WINDOWS = (2, 4, 8, 16)
N_GROUPS = 4
N_HEADS = 8
QK_NOPE = 128
QK_ROPE = 64
V_HEAD = 128
QK_HEAD = QK_NOPE + QK_ROPE
HEAD_PAD = 256
KV_RANK = 256
ROPE_THETA = 10000.0
EPS = 1e-6
ADAM_LR = 0.001
ADAM_B1 = 0.9
ADAM_B2 = 0.999
ADAM_EPS = 1e-08
ADAM_WD = 0.01
ADAM_STEP = 10

N_CHIPS = 4
N_DEV = 8
LANES = 128
PACK_COLS = 1024
VMEM_LIMIT = 56 * 1024 * 1024
GLU_TILE = 256
ATT_BLOCK = 256


def _cparams(*sem):
    return pltpu.CompilerParams(dimension_semantics=sem if sem else None, vmem_limit_bytes=VMEM_LIMIT)


def _pick(n, target, mult):
    best = None
    d = mult
    while d <= min(n, target):
        if n % d == 0:
            best = d
        d += mult
    return n if best is None else best


def _row(v):
    return v.reshape(1, -1).astype(F32)


_DIMS = {"nn": (((1,), (0,)), ((), ())), "nt": (((1,), (1,)), ((), ())), "tn": (((0,), (0,)), ((), ()))}


def _mm_body(mode, nk, has_bias, has_res):
    def body(*refs):
        a_ref, b_ref = refs[0], refs[1]
        pos = 2
        bias_ref = res_ref = cs_ref = None
        if has_bias:
            bias_ref = refs[pos]
            pos += 1
        if has_res:
            res_ref, cs_ref = refs[pos], refs[pos + 1]
            pos += 2
        o_ref = refs[pos]
        pos += 1
        o2_ref = None
        if has_res:
            o2_ref = refs[pos]
            pos += 1
        acc_ref = refs[pos]
        k = pl.program_id(2)

        @pl.when(k == 0)
        def _():
            acc_ref[...] = jnp.zeros_like(acc_ref)

        acc_ref[...] += lax.dot_general(a_ref[...].astype(BF16), b_ref[...].astype(BF16), _DIMS[mode],
                                        preferred_element_type=F32)

        @pl.when(k == nk - 1)
        def _():
            y = acc_ref[...]
            if has_bias:
                y = y + bias_ref[...]
            o_ref[...] = y.astype(o_ref.dtype)
            if has_res:
                o2_ref[...] = res_ref[...] + cs_ref[...] * y

    return body


def mm(a, b, mode, out_dtype, name, *, tm=1024, tn=512, tk=1024, bias=None, res=None, colscale=None):
    if mode == "nn":
        (M, K), N = a.shape, b.shape[1]
    elif mode == "nt":
        (M, K), N = a.shape, b.shape[0]
    else:
        (K, M), N = a.shape, b.shape[1]
    tm = _pick(M, tm, LANES if mode == "tn" else 8)
    tn = _pick(N, tn, LANES)
    tk = _pick(K, tk, LANES) if mode != "tn" else _pick(K, tk, 8)
    nk = K // tk
    a_spec = {"nn": pl.BlockSpec((tm, tk), lambda i, j, k: (i, k)),
              "nt": pl.BlockSpec((tm, tk), lambda i, j, k: (i, k)),
              "tn": pl.BlockSpec((tk, tm), lambda i, j, k: (k, i))}[mode]
    b_spec = {"nn": pl.BlockSpec((tk, tn), lambda i, j, k: (k, j)),
              "nt": pl.BlockSpec((tn, tk), lambda i, j, k: (j, k)),
              "tn": pl.BlockSpec((tk, tn), lambda i, j, k: (k, j))}[mode]
    o_spec = pl.BlockSpec((tm, tn), lambda i, j, k: (i, j))
    v_spec = pl.BlockSpec((1, tn), lambda i, j, k: (0, j))
    in_specs, args = [a_spec, b_spec], [a, b]
    if bias is not None:
        in_specs.append(v_spec)
        args.append(_row(bias))
    out_shape = [jax.ShapeDtypeStruct((M, N), out_dtype)]
    out_specs = [o_spec]
    if res is not None:
        in_specs += [o_spec, v_spec]
        args += [res, _row(colscale)]
        out_shape.append(jax.ShapeDtypeStruct((M, N), F32))
        out_specs.append(o_spec)
    outs = pl.pallas_call(
        _mm_body(mode, nk, bias is not None, res is not None),
        grid=(M // tm, N // tn, nk),
        in_specs=in_specs, out_specs=out_specs, out_shape=out_shape,
        scratch_shapes=[pltpu.VMEM((tm, tn), F32)],
        compiler_params=_cparams("parallel", "parallel", "arbitrary"),
        name=name,
    )(*args)
    return outs if res is not None else outs[0]


def gmm(a, w, mode, out_dtype, name, *, bias=None, res=None, colscale=None, tr=512):
    S_ = a.shape[0]
    G = N_GROUPS
    C = a.shape[1] // G
    tr = _pick(S_, tr, 8)
    nr = S_ // tr
    if mode == "tn":
        def body(a_ref, b_ref, o_ref, acc_ref):
            i = pl.program_id(1)

            @pl.when(i == 0)
            def _():
                acc_ref[...] = jnp.zeros_like(acc_ref)

            acc_ref[...] += lax.dot_general(a_ref[...].astype(BF16), b_ref[...].astype(BF16), _DIMS["tn"],
                                            preferred_element_type=F32)

            @pl.when(i == nr - 1)
            def _():
                o_ref[...] = acc_ref[...].astype(o_ref.dtype)

        blk = pl.BlockSpec((tr, C), lambda g, i: (i, g))
        return pl.pallas_call(
            body, grid=(G, nr), in_specs=[blk, blk],
            out_specs=pl.BlockSpec((None, C, C), lambda g, i: (g, 0, 0)),
            out_shape=jax.ShapeDtypeStruct((G, C, C), out_dtype),
            scratch_shapes=[pltpu.VMEM((C, C), F32)],
            compiler_params=_cparams("parallel", "arbitrary"), name=name,
        )(a, w)

    has_bias, has_res = bias is not None, res is not None

    def body(*refs):
        a_ref, w_ref = refs[0], refs[1]
        pos = 2
        if has_bias:
            bias_ref = refs[pos]
            pos += 1
        if has_res:
            res_ref, cs_ref = refs[pos], refs[pos + 1]
            pos += 2
        o_ref = refs[pos]
        y = lax.dot_general(a_ref[...].astype(BF16), w_ref[...].astype(BF16), _DIMS[mode],
                            preferred_element_type=F32)
        if has_bias:
            y = y + bias_ref[...]
        o_ref[...] = y.astype(o_ref.dtype)
        if has_res:
            refs[pos + 1][...] = res_ref[...] + cs_ref[...] * y

    blk = pl.BlockSpec((tr, C), lambda i, g: (i, g))
    vec = pl.BlockSpec((1, C), lambda i, g: (0, g))
    in_specs = [blk, pl.BlockSpec((None, C, C), lambda i, g: (g, 0, 0))]
    args = [a, w]
    if has_bias:
        in_specs.append(vec)
        args.append(_row(bias))
    out_shape = [jax.ShapeDtypeStruct(a.shape, out_dtype)]
    out_specs = [blk]
    if has_res:
        in_specs += [blk, vec]
        args += [res, _row(colscale)]
        out_shape.append(jax.ShapeDtypeStruct(a.shape, F32))
        out_specs.append(blk)
    outs = pl.pallas_call(
        body, grid=(nr, G), in_specs=in_specs, out_specs=out_specs, out_shape=out_shape,
        compiler_params=_cparams("parallel", "parallel"), name=name,
    )(*args)
    return outs if has_res else outs[0]


def norm_fwd(x, g, sc, sh, out_dtype, name, tr=512):
    S_, Dn = x.shape
    tr = _pick(S_, tr, 8)

    def body(x_ref, g_ref, sc_ref, sh_ref, o_ref):
        xv = x_ref[...]
        r = lax.rsqrt(jnp.mean(xv * xv, axis=-1, keepdims=True) + EPS)
        o_ref[...] = (((xv * r) * g_ref[...]) * (1.0 + sc_ref[...]) + sh_ref[...]).astype(o_ref.dtype)

    blk = pl.BlockSpec((tr, Dn), lambda i: (i, 0))
    vec = pl.BlockSpec((1, Dn), lambda i: (0, 0))
    return pl.pallas_call(
        body, grid=(S_ // tr,), in_specs=[blk, vec, vec, vec], out_specs=blk,
        out_shape=jax.ShapeDtypeStruct((S_, Dn), out_dtype),
        compiler_params=_cparams("parallel"), name=name,
    )(x, _row(g), _row(sc), _row(sh))


def norm_bwd(x, g, sc, dh, dres, name, tr=512):
    S_, Dn = x.shape
    tr = _pick(S_, tr, 8)
    has_res = dres is not None

    def body(*refs):
        x_ref, g_ref, sc_ref, dh_ref = refs[:4]
        pos = 4
        if has_res:
            dres_ref = refs[pos]
            pos += 1
        dx_ref, s1_ref, s2_ref = refs[pos:pos + 3]
        i = pl.program_id(0)

        @pl.when(i == 0)
        def _():
            s1_ref[...] = jnp.zeros_like(s1_ref)
            s2_ref[...] = jnp.zeros_like(s2_ref)

        xv = x_ref[...]
        r = lax.rsqrt(jnp.mean(xv * xv, axis=-1, keepdims=True) + EPS)
        n = xv * r
        dhv = dh_ref[...].astype(F32)
        dn = dhv * (g_ref[...] * (1.0 + sc_ref[...]))
        dx = r * (dn - n * jnp.mean(dn * n, axis=-1, keepdims=True))
        if has_res:
            dx = dx + dres_ref[...]
        dx_ref[...] = dx
        s1_ref[...] += jnp.sum(dhv, axis=0, keepdims=True)
        s2_ref[...] += jnp.sum(dhv * n, axis=0, keepdims=True)

    blk = pl.BlockSpec((tr, Dn), lambda i: (i, 0))
    vec = pl.BlockSpec((1, Dn), lambda i: (0, 0))
    in_specs, args = [blk, vec, vec, blk], [x, _row(g), _row(sc), dh]
    if has_res:
        in_specs.append(blk)
        args.append(dres)
    vshape = jax.ShapeDtypeStruct((1, Dn), F32)
    return pl.pallas_call(
        body, grid=(S_ // tr,), in_specs=in_specs, out_specs=[blk, vec, vec],
        out_shape=[jax.ShapeDtypeStruct((S_, Dn), F32), vshape, vshape],
        compiler_params=_cparams("arbitrary"), name=name,
    )(*args)


def gate_bwd(dx, y, colscale, name, tr=512):
    S_, Dn = dx.shape
    tr = _pick(S_, tr, 8)

    def body(dx_ref, y_ref, cs_ref, d_ref, a_ref, c_ref):
        i = pl.program_id(0)

        @pl.when(i == 0)
        def _():
            a_ref[...] = jnp.zeros_like(a_ref)
            c_ref[...] = jnp.zeros_like(c_ref)

        dxv = dx_ref[...]
        d_ref[...] = (dxv * cs_ref[...]).astype(d_ref.dtype)
        a_ref[...] += jnp.sum(dxv * y_ref[...].astype(F32), axis=0, keepdims=True)
        c_ref[...] += jnp.sum(dxv, axis=0, keepdims=True)

    blk = pl.BlockSpec((tr, Dn), lambda i: (i, 0))
    vec = pl.BlockSpec((1, Dn), lambda i: (0, 0))
    vshape = jax.ShapeDtypeStruct((1, Dn), F32)
    return pl.pallas_call(
        body, grid=(S_ // tr,), in_specs=[blk, blk, vec], out_specs=[blk, vec, vec],
        out_shape=[jax.ShapeDtypeStruct((S_, Dn), BF16), vshape, vshape],
        compiler_params=_cparams("arbitrary"), name=name,
    )(dx, y, _row(colscale))


def loss_head(x, g, target, name, tr=512):
    S_, Dn = x.shape
    tr = _pick(S_, tr, 8)

    def body(x_ref, g_ref, t_ref, dx_ref, dg_ref, loss_ref):
        i = pl.program_id(0)

        @pl.when(i == 0)
        def _():
            dg_ref[...] = jnp.zeros_like(dg_ref)
            loss_ref[...] = jnp.zeros_like(loss_ref)

        xv = x_ref[...]
        r = lax.rsqrt(jnp.mean(xv * xv, axis=-1, keepdims=True) + EPS)
        n = xv * r
        e = n * g_ref[...] - t_ref[...]
        loss_ref[...] += 0.5 * jnp.sum(jnp.mean(e * e, axis=-1, keepdims=True), axis=0, keepdims=True)
        dy = e * (1.0 / Dn)
        dg_ref[...] += jnp.sum(dy * n, axis=0, keepdims=True)
        dn = dy * g_ref[...]
        dx_ref[...] = r * (dn - n * jnp.mean(dn * n, axis=-1, keepdims=True))

    blk = pl.BlockSpec((tr, Dn), lambda i: (i, 0))
    vec = pl.BlockSpec((1, Dn), lambda i: (0, 0))
    one = pl.BlockSpec((1, 1), lambda i: (0, 0))
    return pl.pallas_call(
        body, grid=(S_ // tr,), in_specs=[blk, vec, blk], out_specs=[blk, vec, one],
        out_shape=[jax.ShapeDtypeStruct((S_, Dn), F32), jax.ShapeDtypeStruct((1, Dn), F32),
                   jax.ShapeDtypeStruct((1, 1), F32)],
        compiler_params=_cparams("arbitrary"), name=name,
    )(x, _row(g), target)


POOL_HALO = 16
POOL_CHUNK = 512


def _rows(ref, lo, hi, n_rows):
    parts = []
    if lo < 0:
        parts.append(jnp.zeros((-lo, ref.shape[1]), F32))
    parts.append(ref[max(lo, 0):min(hi, n_rows), :].astype(F32))
    if hi > n_rows:
        parts.append(jnp.zeros((hi - n_rows, ref.shape[1]), F32))
    return parts[0] if len(parts) == 1 else jnp.concatenate(parts, axis=0)


def _window_sum(e, w, back):
    n = e.shape[0]
    s, width = e, 1
    while width < w:
        s = s + pltpu.roll(s, width if back else n - width, 0)
        width *= 2
    return s


def _pool_call(h, out_dtype, name, backward):
    S_, Dn = h.shape
    C = Dn // N_GROUPS
    ch = _pick(S_, POOL_CHUNK, 8)

    def body(h_ref, o_ref):
        g = pl.program_id(0)
        for gi, w in enumerate(POOL_WINDOWS):
            @pl.when(g == gi)
            def _(w=w):
                for r0 in range(0, S_, ch):
                    t = (r0 + lax.broadcasted_iota(jnp.int32, (ch, C), 0)).astype(F32)
                    cnt = jnp.minimum(t + 1.0, float(w))
                    if not backward:
                        ext = _rows(h_ref, r0 - POOL_HALO, r0 + ch, S_)
                        cur = ext[POOL_HALO:]
                        mean = _window_sum(ext, w, True)[POOL_HALO:] / cnt
                        o_ref[r0:r0 + ch, :] = (mean - cur).astype(o_ref.dtype)
                    else:
                        ext = _rows(h_ref, r0, r0 + ch + POOL_HALO, S_)
                        text = (r0 + lax.broadcasted_iota(jnp.int32, (ch + POOL_HALO, C), 0)).astype(F32)
                        e = ext / jnp.minimum(text + 1.0, float(w))
                        o_ref[r0:r0 + ch, :] = (_window_sum(e, w, False)[:ch] - ext[:ch]).astype(o_ref.dtype)

    blk = pl.BlockSpec((S_, C), lambda g: (0, g))
    return pl.pallas_call(
        body, grid=(N_GROUPS,), in_specs=[blk], out_specs=blk,
        out_shape=jax.ShapeDtypeStruct((S_, Dn), out_dtype),
        compiler_params=_cparams("parallel"), name=name,
    )(h)


GLU_CHUNK = 512
_SQRT_HALF = 0.7071067811865476
_INV_SQRT_2PI = 0.3989422804014327


def _gelu(a):
    return 0.5 * a * (1.0 + lax.erf(a * _SQRT_HALF))


def _gelu_grad(a):
    return 0.5 * (1.0 + lax.erf(a * _SQRT_HALF)) + a * (_INV_SQRT_2PI * jnp.exp(-0.5 * a * a))


def glu_fwd(u, conv_w, conv_b, name):
    S_, F2 = u.shape
    Fh = F2 // 2
    tf = GLU_TILE
    ch = _pick(S_, GLU_CHUNK, 8)

    def body(u_ref, cw_ref, cb_ref, z_ref):
        cw0, cw1, cw2 = cw_ref[0:1, :], cw_ref[1:2, :], cw_ref[2:3, :]
        cb = cb_ref[...]
        a_ref = u_ref.at[:, 0:tf]
        for r0 in range(0, S_, ch):
            ext = _rows(a_ref, r0 - 8, r0 + ch, S_)
            a0 = ext[8:]
            a1 = pltpu.roll(ext, 1, 0)[8:]
            a2 = pltpu.roll(ext, 2, 0)[8:]
            ac = a2 * cw0 + a1 * cw1 + a0 * cw2 + cb
            z_ref[r0:r0 + ch, :] = (_gelu(ac) * u_ref[r0:r0 + ch, tf:2 * tf].astype(F32)).astype(z_ref.dtype)

    return pl.pallas_call(
        body, grid=(Fh // tf,),
        in_specs=[pl.BlockSpec((S_, 2 * tf), lambda j: (0, j)), pl.BlockSpec((3, tf), lambda j: (0, j)),
                  pl.BlockSpec((1, tf), lambda j: (0, j))],
        out_specs=pl.BlockSpec((S_, tf), lambda j: (0, j)),
        out_shape=jax.ShapeDtypeStruct((S_, Fh), BF16),
        compiler_params=_cparams("parallel"), name=name,
    )(u, conv_w, _row(conv_b))


def glu_bwd(u, dz, conv_w, conv_b, name):
    S_, F2 = u.shape
    Fh = F2 // 2
    tf = GLU_TILE
    ch = _pick(S_, GLU_CHUNK, 8)

    def body(u_ref, dz_ref, cw_ref, cb_ref, du_ref, dcw_ref, dcb_ref):
        cw0, cw1, cw2 = cw_ref[0:1, :], cw_ref[1:2, :], cw_ref[2:3, :]
        cb = cb_ref[...]
        a_ref = u_ref.at[:, 0:tf]
        v_ref = u_ref.at[:, tf:2 * tf]
        acc = [jnp.zeros((1, tf), F32) for _ in range(4)]
        n = ch + 8
        for r0 in range(0, S_, ch):
            ext = _rows(a_ref, r0 - 8, r0 + n, S_)
            a0 = ext[8:]
            a1 = pltpu.roll(ext, 1, 0)[8:]
            a2 = pltpu.roll(ext, 2, 0)[8:]
            ac = a2 * cw0 + a1 * cw1 + a0 * cw2 + cb
            vv = _rows(v_ref, r0, r0 + n, S_)
            dzv = _rows(dz_ref, r0, r0 + n, S_)
            gl = _gelu(ac)
            dac = dzv * vv * _gelu_grad(ac)
            da = (dac * cw2 + pltpu.roll(dac, n - 1, 0) * cw1 + pltpu.roll(dac, n - 2, 0) * cw0)[:ch]
            du_ref[r0:r0 + ch, 0:tf] = da.astype(du_ref.dtype)
            du_ref[r0:r0 + ch, tf:2 * tf] = (dzv[:ch] * gl[:ch]).astype(du_ref.dtype)
            dc = dac[:ch]
            acc[0] = acc[0] + jnp.sum(dc * a2[:ch], axis=0, keepdims=True)
            acc[1] = acc[1] + jnp.sum(dc * a1[:ch], axis=0, keepdims=True)
            acc[2] = acc[2] + jnp.sum(dc * a0[:ch], axis=0, keepdims=True)
            acc[3] = acc[3] + jnp.sum(dc, axis=0, keepdims=True)
        dcw_ref[0:1, :] = acc[0]
        dcw_ref[1:2, :] = acc[1]
        dcw_ref[2:3, :] = acc[2]
        dcb_ref[...] = acc[3]

    return pl.pallas_call(
        body, grid=(Fh // tf,),
        in_specs=[pl.BlockSpec((S_, 2 * tf), lambda j: (0, j)), pl.BlockSpec((S_, tf), lambda j: (0, j)),
                  pl.BlockSpec((3, tf), lambda j: (0, j)), pl.BlockSpec((1, tf), lambda j: (0, j))],
        out_specs=[pl.BlockSpec((S_, 2 * tf), lambda j: (0, j)), pl.BlockSpec((3, tf), lambda j: (0, j)),
                   pl.BlockSpec((1, tf), lambda j: (0, j))],
        out_shape=[jax.ShapeDtypeStruct((S_, F2), BF16), jax.ShapeDtypeStruct((3, Fh), F32),
                   jax.ShapeDtypeStruct((1, Fh), F32)],
        compiler_params=_cparams("parallel"), name=name,
    )(u, dz, conv_w, _row(conv_b))


def rope_tables(pos, inv, name, tr=512):
    S_ = pos.shape[0]
    tr = _pick(S_, tr, 8)

    def body(p_ref, inv_ref, c_ref, s1_ref, s2_ref):
        ang = p_ref[...] * inv_ref[...]
        lane = lax.broadcasted_iota(jnp.int32, ang.shape, 1)
        half = QK_ROPE // 2
        cosv, sinv = jnp.cos(ang), jnp.sin(ang)
        c_ref[...] = jnp.where(lane < QK_ROPE, cosv, 0.0)
        s1_ref[...] = jnp.where(lane < half, -sinv, 0.0)
        s2_ref[...] = jnp.where((lane >= half) & (lane < QK_ROPE), sinv, 0.0)

    blk = pl.BlockSpec((tr, LANES), lambda i: (i, 0))
    shp = jax.ShapeDtypeStruct((S_, LANES), F32)
    return pl.pallas_call(
        body, grid=(S_ // tr,),
        in_specs=[pl.BlockSpec((tr, 1), lambda i: (i, 0)), pl.BlockSpec((1, LANES), lambda i: (0, 0))],
        out_specs=[blk, blk, blk], out_shape=[shp, shp, shp],
        compiler_params=_cparams("parallel"), name=name,
    )(pos, inv)


_HALF = QK_ROPE // 2


def _rope(t, c, s1, s2):
    return t * c + pltpu.roll(t, LANES - _HALF, 1) * s1 + pltpu.roll(t, _HALF, 1) * s2


def _rope_t(d, c, s1, s2):
    return d * c + pltpu.roll(d * s1, _HALF, 1) + pltpu.roll(d * s2, LANES - _HALF, 1)


def q_prep(q, tabs, scale, backward, name, tr=512):
    S_, W = q.shape
    tr = _pick(S_, tr, 8)

    def body(q_ref, c_ref, s1_ref, s2_ref, o_ref):
        o_ref[:, 0:LANES] = (q_ref[:, 0:LANES].astype(F32) * scale).astype(o_ref.dtype)
        t = q_ref[:, LANES:2 * LANES].astype(F32)
        fn = _rope_t if backward else _rope
        o_ref[:, LANES:2 * LANES] = (fn(t, c_ref[...], s1_ref[...], s2_ref[...]) * scale).astype(o_ref.dtype)

    blk = pl.BlockSpec((tr, HEAD_PAD), lambda i, h: (i, h))
    tab = pl.BlockSpec((tr, LANES), lambda i, h: (i, 0))
    return pl.pallas_call(
        body, grid=(S_ // tr, W // HEAD_PAD), in_specs=[blk, tab, tab, tab], out_specs=blk,
        out_shape=jax.ShapeDtypeStruct((S_, W), BF16),
        compiler_params=_cparams("parallel", "parallel"), name=name,
    )(q, *tabs)


def k_prep(knv, kv_ext, tabs, name, tr=512):
    S_ = knv.shape[0]
    tr = _pick(S_, tr, 8)

    def body(kn_ref, t_ref, c_ref, s1_ref, s2_ref, o_ref):
        o_ref[:, 0:LANES] = kn_ref[...].astype(o_ref.dtype)
        o_ref[:, LANES:2 * LANES] = _rope(t_ref[...], c_ref[...], s1_ref[...], s2_ref[...]).astype(o_ref.dtype)

    tab = pl.BlockSpec((tr, LANES), lambda i, h: (i, 0))
    return pl.pallas_call(
        body, grid=(S_ // tr, N_HEADS),
        in_specs=[pl.BlockSpec((tr, LANES), lambda i, h: (i, h)),
                  pl.BlockSpec((tr, LANES), lambda i, h: (i, KV_RANK // LANES)), tab, tab, tab],
        out_specs=pl.BlockSpec((tr, HEAD_PAD), lambda i, h: (i, h)),
        out_shape=jax.ShapeDtypeStruct((S_, N_HEADS * HEAD_PAD), BF16),
        compiler_params=_cparams("parallel", "parallel"), name=name,
    )(knv, kv_ext, *tabs)


def k_prep_bwd(dk_a, dk_b, dv_a, dv_b, tabs, name, tr=256):
    S_ = dk_a.shape[0]
    tr = _pick(S_, tr, 8)
    HV = N_HEADS * V_HEAD

    def body(ka_ref, kb_ref, va_ref, vb_ref, c_ref, s1_ref, s2_ref, o_ref, t_ref):
        dr = jnp.zeros((tr, LANES), F32)
        for h in range(N_HEADS):
            lo = h * HEAD_PAD
            o_ref[:, h * LANES:(h + 1) * LANES] = (ka_ref[:, lo:lo + LANES] + kb_ref[:, lo:lo + LANES]).astype(o_ref.dtype)
            dr = dr + ka_ref[:, lo + LANES:lo + 2 * LANES] + kb_ref[:, lo + LANES:lo + 2 * LANES]
        o_ref[:, HV:2 * HV] = (va_ref[...] + vb_ref[...]).astype(o_ref.dtype)
        t_ref[...] = _rope_t(dr, c_ref[...], s1_ref[...], s2_ref[...])

    kblk = pl.BlockSpec((tr, N_HEADS * HEAD_PAD), lambda i: (i, 0))
    vblk = pl.BlockSpec((tr, HV), lambda i: (i, 0))
    tab = pl.BlockSpec((tr, LANES), lambda i: (i, 0))
    return pl.pallas_call(
        body, grid=(S_ // tr,), in_specs=[kblk, kblk, vblk, vblk, tab, tab, tab],
        out_specs=[pl.BlockSpec((tr, 2 * HV), lambda i: (i, 0)), tab],
        out_shape=[jax.ShapeDtypeStruct((S_, 2 * HV), BF16), jax.ShapeDtypeStruct((S_, LANES), F32)],
        compiler_params=_cparams("parallel"), name=name,
    )(dk_a, dk_b, dv_a, dv_b, *tabs)


_NEG = -1e30


def attn_fwd(q, k, knv, name):
    S_ = q.shape[0]
    T = _pick(S_, ATT_BLOCK, 8)

    def body(q_ref, k_ref, v_ref, o_ref, lse_ref):
        i = pl.program_id(1)
        qv = q_ref[...]

        def step(j, carry, masked):
            m, l, acc = carry
            start = pl.multiple_of(j * T, T)
            s = lax.dot_general(qv, k_ref[pl.ds(start, T), :], _DIMS["nt"], preferred_element_type=F32)
            if masked:
                rowi = lax.broadcasted_iota(jnp.int32, (T, T), 0)
                coli = lax.broadcasted_iota(jnp.int32, (T, T), 1)
                s = jnp.where(coli <= rowi, s, _NEG)
            m_new = jnp.maximum(m, jnp.max(s, axis=-1, keepdims=True))
            alpha = jnp.exp(m - m_new)
            p = jnp.exp(s - m_new)
            l = alpha * l + jnp.sum(p, axis=-1, keepdims=True)
            acc = alpha * acc + lax.dot_general(p.astype(BF16), v_ref[pl.ds(start, T), :], _DIMS["nn"],
                                                preferred_element_type=F32)
            return m_new, l, acc

        init = (jnp.full((T, 1), _NEG, F32), jnp.zeros((T, 1), F32), jnp.zeros((T, V_HEAD), F32))
        carry = lax.fori_loop(0, i, functools.partial(step, masked=False), init)
        m, l, acc = step(i, carry, True)
        o_ref[...] = (acc / l).astype(o_ref.dtype)
        lse_ref[...] = m + jnp.log(l)

    return pl.pallas_call(
        body, grid=(N_HEADS, S_ // T),
        in_specs=[pl.BlockSpec((T, HEAD_PAD), lambda h, i: (i, h)),
                  pl.BlockSpec((S_, HEAD_PAD), lambda h, i: (0, h)),
                  pl.BlockSpec((S_, V_HEAD), lambda h, i: (0, N_HEADS + h))],
        out_specs=[pl.BlockSpec((T, V_HEAD), lambda h, i: (i, h)),
                   pl.BlockSpec((None, T, 1), lambda h, i: (h, i, 0))],
        out_shape=[jax.ShapeDtypeStruct((S_, N_HEADS * V_HEAD), BF16), jax.ShapeDtypeStruct((N_HEADS, S_, 1), F32)],
        compiler_params=_cparams("parallel", "parallel"), name=name,
    )(q, k, knv)


def attn_delta(o, do, name, tr=512):
    S_ = o.shape[0]
    tr = _pick(S_, tr, 8)

    def body(o_ref, do_ref, d_ref):
        d_ref[...] = jnp.sum(o_ref[...].astype(F32) * do_ref[...].astype(F32), axis=-1, keepdims=True)

    blk = pl.BlockSpec((tr, V_HEAD), lambda i, h: (i, h))
    return pl.pallas_call(
        body, grid=(S_ // tr, N_HEADS), in_specs=[blk, blk],
        out_specs=pl.BlockSpec((None, tr, 1), lambda i, h: (h, i, 0)),
        out_shape=jax.ShapeDtypeStruct((N_HEADS, S_, 1), F32),
        compiler_params=_cparams("parallel", "parallel"), name=name,
    )(o, do)


def attn_bwd(q, k, knv, do, lse_row, delta_row, name):
    S_ = q.shape[0]
    T = _pick(S_, ATT_BLOCK, LANES)
    nb = S_ // T

    def body(q_ref, do_ref, lse_ref, dl_ref, k_ref, v_ref, dq_ref, dk_ref, dv_ref):
        j = pl.program_id(1)

        @pl.when(j == 0)
        def _():
            dq_ref[...] = jnp.zeros_like(dq_ref)

        kv = k_ref[...]
        vv = v_ref[...]

        def step(i, carry, masked):
            dk, dv = carry
            start = pl.multiple_of(i * T, T)
            qv = q_ref[pl.ds(start, T), :]
            dov = do_ref[pl.ds(start, T), :]
            st = lax.dot_general(kv, qv, _DIMS["nt"], preferred_element_type=F32)
            pt = jnp.exp(st - lse_ref[:, pl.ds(start, T)])
            if masked:
                keyi = lax.broadcasted_iota(jnp.int32, (T, T), 0)
                qryi = lax.broadcasted_iota(jnp.int32, (T, T), 1)
                pt = jnp.where(keyi <= qryi, pt, 0.0)
            dpt = lax.dot_general(vv, dov, _DIMS["nt"], preferred_element_type=F32)
            dst = (pt * (dpt - dl_ref[:, pl.ds(start, T)])).astype(BF16)
            dv = dv + lax.dot_general(pt.astype(BF16), dov, _DIMS["nn"], preferred_element_type=F32)
            dk = dk + lax.dot_general(dst, qv, _DIMS["nn"], preferred_element_type=F32)
            dq_ref[pl.ds(start, T), :] += lax.dot_general(dst, kv, _DIMS["tn"], preferred_element_type=F32)
            return dk, dv

        carry = step(j, (jnp.zeros((T, HEAD_PAD), F32), jnp.zeros((T, V_HEAD), F32)), True)
        dk, dv = lax.fori_loop(j + 1, nb, functools.partial(step, masked=False), carry)
        dk_ref[...] = dk
        dv_ref[...] = dv

    return pl.pallas_call(
        body, grid=(N_HEADS, nb),
        in_specs=[pl.BlockSpec((S_, HEAD_PAD), lambda h, j: (0, h)),
                  pl.BlockSpec((S_, V_HEAD), lambda h, j: (0, h)),
                  pl.BlockSpec((None, 1, S_), lambda h, j: (h, 0, 0)),
                  pl.BlockSpec((None, 1, S_), lambda h, j: (h, 0, 0)),
                  pl.BlockSpec((T, HEAD_PAD), lambda h, j: (j, h)),
                  pl.BlockSpec((T, V_HEAD), lambda h, j: (j, N_HEADS + h))],
        out_specs=[pl.BlockSpec((S_, HEAD_PAD), lambda h, j: (0, h)),
                   pl.BlockSpec((T, HEAD_PAD), lambda h, j: (j, h)),
                   pl.BlockSpec((T, V_HEAD), lambda h, j: (j, h))],
        out_shape=[jax.ShapeDtypeStruct((S_, N_HEADS * HEAD_PAD), F32),
                   jax.ShapeDtypeStruct((S_, N_HEADS * HEAD_PAD), F32),
                   jax.ShapeDtypeStruct((S_, N_HEADS * V_HEAD), F32)],
        compiler_params=_cparams("parallel", "arbitrary"), name=name,
    )(q, do, lse_row, delta_row, k, knv)


def mods_fwd(c_all, mod_w, mod_b, name, tn=512):
    L, Dn, E = mod_w.shape
    R = c_all.shape[0]
    tn = _pick(E, tn, LANES)

    def body(c_ref, w_ref, b_ref, o_ref):
        cv = c_ref[...]
        sc = (cv / (1.0 + jnp.exp(-cv))).astype(BF16)
        o_ref[...] = lax.dot_general(sc, w_ref[...].astype(BF16), _DIMS["nn"], preferred_element_type=F32) + b_ref[...]

    return pl.pallas_call(
        body, grid=(L, E // tn),
        in_specs=[pl.BlockSpec((R, Dn), lambda l, j: (0, 0)), pl.BlockSpec((None, Dn, tn), lambda l, j: (l, 0, j)),
                  pl.BlockSpec((None, 1, tn), lambda l, j: (l, 0, j))],
        out_specs=pl.BlockSpec((None, R, tn), lambda l, j: (l, 0, j)),
        out_shape=jax.ShapeDtypeStruct((L, R, E), F32),
        compiler_params=_cparams("parallel", "parallel"), name=name,
    )(c_all, mod_w, mod_b.reshape(L, 1, E))


def _adam_math(w, g, m, v):
    m = ADAM_B1 * m + (1.0 - ADAM_B1) * g
    v = ADAM_B2 * v + (1.0 - ADAM_B2) * (g * g)
    m_hat = m / (1.0 - ADAM_B1 ** ADAM_STEP)
    v_hat = v / (1.0 - ADAM_B2 ** ADAM_STEP)
    delta = -ADAM_LR * (m_hat / (jnp.sqrt(v_hat) + ADAM_EPS) + ADAM_WD * w)
    return delta, m, v


def _as2d(a):
    return a.reshape(-1, a.shape[-1]) if a.ndim != 2 else a


def adamw(w, g, m, v, name):
    shape = w.shape
    w2, g2, m2, v2 = _as2d(w), _as2d(g), _as2d(m), _as2d(v)
    R, C = w2.shape
    tr = _pick(R, max(8, (1 << 18) // C // 8 * 8), 8)

    def body(w_ref, g_ref, m_ref, v_ref, d_ref, mo_ref, vo_ref):
        d, mn, vn = _adam_math(w_ref[...], g_ref[...], m_ref[...], v_ref[...])
        d_ref[...] = d
        mo_ref[...] = mn
        vo_ref[...] = vn

    blk = pl.BlockSpec((tr, C), lambda i: (i, 0))
    shp = jax.ShapeDtypeStruct((R, C), F32)
    outs = pl.pallas_call(
        body, grid=(R // tr,), in_specs=[blk] * 4, out_specs=[blk] * 3, out_shape=[shp] * 3,
        compiler_params=_cparams("parallel"), name=name,
    )(w2, g2, m2, v2)
    return tuple(o.reshape(shape) for o in outs)


def adamw_sum(parts, w, m, v, name):
    P, R, C = parts.shape

    def body(p_ref, w_ref, m_ref, v_ref, g_ref, d_ref, mo_ref, vo_ref):
        g = p_ref[0]
        for k in range(1, P):
            g = g + p_ref[k]
        d, mn, vn = _adam_math(w_ref[...], g, m_ref[...], v_ref[...])
        g_ref[...] = g
        d_ref[...] = d
        mo_ref[...] = mn
        vo_ref[...] = vn

    shp = jax.ShapeDtypeStruct((R, C), F32)
    return pl.pallas_call(body, out_shape=[shp] * 4, compiler_params=_cparams(), name=name)(parts, w, m, v)


def adamw_modw(c_col, dm, w, m, v, name, tr=256, tn=512):
    L, Dn, E = w.shape
    B = c_col.shape[0]
    tr = _pick(Dn, tr, 8)
    tn = _pick(E, tn, LANES)

    def body(c_ref, dm_ref, w_ref, m_ref, v_ref, g_ref, d_ref, mo_ref, vo_ref):
        g = jnp.zeros((tr, tn), F32)
        for b in range(B):
            cv = c_ref[b]
            g = g + (cv / (1.0 + jnp.exp(-cv))) * dm_ref[b:b + 1, :]
        d, mn, vn = _adam_math(w_ref[...], g, m_ref[...], v_ref[...])
        g_ref[...] = g
        d_ref[...] = d
        mo_ref[...] = mn
        vo_ref[...] = vn

    blk = pl.BlockSpec((None, tr, tn), lambda l, i, j: (l, i, j))
    shp = jax.ShapeDtypeStruct((L, Dn, E), F32)
    return pl.pallas_call(
        body, grid=(L, Dn // tr, E // tn),
        in_specs=[pl.BlockSpec((B, tr, 1), lambda l, i, j: (0, i, 0)),
                  pl.BlockSpec((None, B, tn), lambda l, i, j: (l, 0, j)), blk, blk, blk],
        out_specs=[blk] * 4, out_shape=[shp] * 4,
        compiler_params=_cparams("parallel", "parallel", "parallel"), name=name,
    )(c_col, dm, w, m, v)


def add_round(a, b, name, tr=512):
    R, C = a.shape
    tr = _pick(R, tr, 16)

    def body(a_ref, b_ref, o_ref):
        o_ref[...] = (a_ref[...] + b_ref[...].astype(F32)).astype(BF16)

    blk = pl.BlockSpec((tr, C), lambda i: (i, 0))
    return pl.pallas_call(
        body, grid=(R // tr,), in_specs=[blk, blk], out_specs=blk, out_shape=jax.ShapeDtypeStruct((R, C), BF16),
        compiler_params=_cparams("parallel"), name=name,
    )(a, b)


def sum_parts(parts, name, tr=512):
    P, R, C = parts.shape
    tr = _pick(R, tr, 16)

    def body(p_ref, o_ref):
        s = p_ref[0].astype(F32)
        for k in range(1, P):
            s = s + p_ref[k].astype(F32)
        o_ref[...] = s

    return pl.pallas_call(
        body, grid=(R // tr,), in_specs=[pl.BlockSpec((P, tr, C), lambda i: (0, i, 0))],
        out_specs=pl.BlockSpec((tr, C), lambda i: (i, 0)), out_shape=jax.ShapeDtypeStruct((R, C), F32),
        compiler_params=_cparams("parallel"), name=name,
    )(parts)


_ANY = pl.BlockSpec(memory_space=pl.ANY)


def _place():
    return lax.axis_index("x"), lax.axis_index("y"), lax.axis_index("c")


def _flip(v, bit):
    return 1 - v if bit else v


def chip_gather(buf, name):
    def body(in_ref, out_ref, send_sems, recv_sems, local_sem):
        x, y, c = _place()
        me = 2 * x + y
        mine = pltpu.make_async_copy(in_ref, out_ref.at[me], local_sem)
        mine.start()
        sends = []
        for k in range(1, N_CHIPS):
            px, py = _flip(x, k >> 1), _flip(y, k & 1)
            cp = pltpu.make_async_remote_copy(src_ref=in_ref, dst_ref=out_ref.at[me], send_sem=send_sems.at[k - 1],
                                              recv_sem=recv_sems.at[k - 1], device_id=(px, py, c), device_id_type=MESH)
            cp.start()
            sends.append(cp)
        for k in range(1, N_CHIPS):
            px, py = _flip(x, k >> 1), _flip(y, k & 1)
            pltpu.make_async_remote_copy(src_ref=in_ref, dst_ref=out_ref.at[2 * px + py], send_sem=send_sems.at[k - 1],
                                         recv_sem=recv_sems.at[k - 1], device_id=(px, py, c),
                                         device_id_type=MESH).wait_recv()
        for cp in sends:
            cp.wait_send()
        mine.wait()

    return pl.pallas_call(
        body, in_specs=[_ANY], out_specs=_ANY,
        out_shape=jax.ShapeDtypeStruct((N_CHIPS,) + buf.shape, buf.dtype),
        scratch_shapes=[pltpu.SemaphoreType.DMA((N_CHIPS - 1,)), pltpu.SemaphoreType.DMA((N_CHIPS - 1,)),
                        pltpu.SemaphoreType.DMA],
        name=name,
    )(buf)


def chip_all_to_all(buf, name):
    def body(in_ref, out_ref, send_sems, recv_sems, local_sem):
        x, y, c = _place()
        me = 2 * x + y
        mine = pltpu.make_async_copy(in_ref.at[me], out_ref.at[me], local_sem)
        mine.start()
        sends = []
        for k in range(1, N_CHIPS):
            px, py = _flip(x, k >> 1), _flip(y, k & 1)
            cp = pltpu.make_async_remote_copy(src_ref=in_ref.at[2 * px + py], dst_ref=out_ref.at[me],
                                              send_sem=send_sems.at[k - 1], recv_sem=recv_sems.at[k - 1],
                                              device_id=(px, py, c), device_id_type=MESH)
            cp.start()
            sends.append(cp)
        for k in range(1, N_CHIPS):
            px, py = _flip(x, k >> 1), _flip(y, k & 1)
            pltpu.make_async_remote_copy(src_ref=in_ref.at[me], dst_ref=out_ref.at[2 * px + py],
                                         send_sem=send_sems.at[k - 1], recv_sem=recv_sems.at[k - 1],
                                         device_id=(px, py, c), device_id_type=MESH).wait_recv()
        for cp in sends:
            cp.wait_send()
        mine.wait()

    return pl.pallas_call(
        body, in_specs=[_ANY], out_specs=_ANY, out_shape=jax.ShapeDtypeStruct(buf.shape, buf.dtype),
        scratch_shapes=[pltpu.SemaphoreType.DMA((N_CHIPS - 1,)), pltpu.SemaphoreType.DMA((N_CHIPS - 1,)),
                        pltpu.SemaphoreType.DMA],
        name=name,
    )(buf)


def core_gather(buf, name):
    def body(in_ref, out_ref, send_sem, recv_sem, local_sem):
        x, y, c = _place()
        mine = pltpu.make_async_copy(in_ref, out_ref.at[c], local_sem)
        mine.start()
        cp = pltpu.make_async_remote_copy(src_ref=in_ref, dst_ref=out_ref.at[c], send_sem=send_sem, recv_sem=recv_sem,
                                          device_id=(x, y, 1 - c), device_id_type=MESH)
        cp.start()
        pltpu.make_async_remote_copy(src_ref=in_ref, dst_ref=out_ref.at[1 - c], send_sem=send_sem, recv_sem=recv_sem,
                                     device_id=(x, y, 1 - c), device_id_type=MESH).wait_recv()
        cp.wait_send()
        mine.wait()

    return pl.pallas_call(
        body, in_specs=[_ANY], out_specs=_ANY, out_shape=jax.ShapeDtypeStruct((2,) + buf.shape, buf.dtype),
        scratch_shapes=[pltpu.SemaphoreType.DMA, pltpu.SemaphoreType.DMA, pltpu.SemaphoreType.DMA],
        name=name,
    )(buf)


def core_swap(buf, name):
    def body(in_ref, out_ref, send_sem, recv_sem):
        x, y, c = _place()
        cp = pltpu.make_async_remote_copy(src_ref=in_ref, dst_ref=out_ref, send_sem=send_sem, recv_sem=recv_sem,
                                          device_id=(x, y, 1 - c), device_id_type=MESH)
        cp.start()
        cp.wait()

    return pl.pallas_call(
        body, in_specs=[_ANY], out_specs=_ANY, out_shape=jax.ShapeDtypeStruct(buf.shape, buf.dtype),
        scratch_shapes=[pltpu.SemaphoreType.DMA, pltpu.SemaphoreType.DMA],
        name=name,
    )(buf)


def device_gather(buf, name):
    def body(in_ref, out_ref, send_sems, recv_sems, local_sem):
        x, y, c = _place()
        me = 4 * x + 2 * y + c
        mine = pltpu.make_async_copy(in_ref, out_ref.at[me], local_sem)
        mine.start()
        sends = []
        for k in range(1, N_DEV):
            peer = (_flip(x, (k >> 2) & 1), _flip(y, (k >> 1) & 1), _flip(c, k & 1))
            cp = pltpu.make_async_remote_copy(src_ref=in_ref, dst_ref=out_ref.at[me], send_sem=send_sems.at[k - 1],
                                              recv_sem=recv_sems.at[k - 1], device_id=peer, device_id_type=MESH)
            cp.start()
            sends.append(cp)
        for k in range(1, N_DEV):
            peer = (_flip(x, (k >> 2) & 1), _flip(y, (k >> 1) & 1), _flip(c, k & 1))
            pltpu.make_async_remote_copy(src_ref=in_ref, dst_ref=out_ref.at[4 * peer[0] + 2 * peer[1] + peer[2]],
                                         send_sem=send_sems.at[k - 1], recv_sem=recv_sems.at[k - 1], device_id=peer,
                                         device_id_type=MESH).wait_recv()
        for cp in sends:
            cp.wait_send()
        mine.wait()

    return pl.pallas_call(
        body, in_specs=[_ANY], out_specs=_ANY, out_shape=jax.ShapeDtypeStruct((N_DEV,) + buf.shape, buf.dtype),
        scratch_shapes=[pltpu.SemaphoreType.DMA((N_DEV - 1,)), pltpu.SemaphoreType.DMA((N_DEV - 1,)),
                        pltpu.SemaphoreType.DMA],
        name=name,
    )(buf)


WEIGHT_ORDER = ["mod_w", "mod_b", "norm1_g", "norm2_g", "pool_w", "pool_b", "pool_scale", "kv_in_g", "w_dkv",
                "ckv_norm_g", "w_uk", "w_uv", "w_dq", "q_norm_g", "w_uq", "w_o", "w_up", "conv_w", "conv_b", "w_down",
                "final_g"]
EXCHANGED = {"w_up": 2, "w_down": 1, "w_o": 1, "w_uq": 2, "w_dq": 1, "pool_w": 2, "w_dkv": 0, "w_uk": 1, "w_uv": 1,
             "conv_w": 2, "pool_b": 1, "pool_scale": 1}
KEPT_F32 = ("conv_w", "pool_b", "pool_scale")
REPLICATED = ["mod_b", "norm1_g", "norm2_g", "kv_in_g", "ckv_norm_g", "q_norm_g", "conv_b", "final_g"]
PACK_ALIGN = 2 * 16 * PACK_COLS


def _padded(n, align):
    return -(-n // align) * align


def _flat_pad(parts, total):
    flat = jnp.concatenate(parts, axis=-1)
    pad = total - flat.shape[-1]
    if pad:
        flat = jnp.concatenate([flat, jnp.zeros(flat.shape[:-1] + (pad,), flat.dtype)], axis=-1)
    return flat


def _split_shards(full, axis):
    shp = full.shape
    t = full.reshape(shp[:axis] + (N_CHIPS, shp[axis] // N_CHIPS) + shp[axis + 1:])
    return jnp.moveaxis(t, axis, 0).reshape(N_CHIPS, -1)


def _join_shards(rows, shard_shape, axis):
    t = jnp.moveaxis(rows.reshape((N_CHIPS,) + tuple(shard_shape)), 0, axis)
    return t.reshape(tuple(shard_shape[:axis]) + (N_CHIPS * shard_shape[axis],) + tuple(shard_shape[axis + 1:]))


def _index(a, i, axis=0):
    return lax.dynamic_index_in_dim(a, i, axis, keepdims=False)


def kernel(x, c, positions, mod_w, mod_b, norm1_g, norm2_g, pool_w, pool_b, pool_scale, kv_in_g, w_dkv, ckv_norm_g, w_uk, w_uv, w_dq, q_norm_g, w_uq, w_o, w_up, conv_w, conv_b, w_down, final_g, loss_target, m_mod_w, m_mod_b, m_norm1_g, m_norm2_g, m_pool_w, m_pool_b, m_pool_scale, m_kv_in_g, m_w_dkv, m_ckv_norm_g, m_w_uk, m_w_uv, m_w_dq, m_q_norm_g, m_w_uq, m_w_o, m_w_up, m_conv_w, m_conv_b, m_w_down, m_final_g, v_mod_w, v_mod_b, v_norm1_g, v_norm2_g, v_pool_w, v_pool_b, v_pool_scale, v_kv_in_g, v_w_dkv, v_ckv_norm_g, v_w_uk, v_w_uv, v_w_dq, v_q_norm_g, v_w_uq, v_w_o, v_w_up, v_conv_w, v_conv_b, v_w_down, v_final_g):
    given = dict(locals())
    W = {n: given[n] for n in WEIGHT_ORDER}
    M1 = {n: given["m_" + n] for n in WEIGHT_ORDER}
    V2 = {n: given["v_" + n] for n in WEIGHT_ORDER}
    xi, yi, ci = lax.axis_index("x"), lax.axis_index("y"), lax.axis_index("c")
    chip = 2 * xi + yi
    dev = 4 * xi + 2 * yi + ci
    x0 = x[0]
    S_, D = x0.shape
    Fh = conv_b.shape[1]
    E = mod_b.shape[1]
    Es = E // N_CHIPS
    zD = jnp.zeros((D,), F32)

    sizes = {n: math.prod(W[n].shape) * (2 if n in KEPT_F32 else 1) for n in EXCHANGED}
    T = _padded(sum(sizes.values()), PACK_ALIGN)
    R = T // PACK_COLS
    own = _flat_pad([lax.bitcast_convert_type(W[n], BF16).reshape(-1) if n in KEPT_F32 else W[n].astype(BF16).reshape(-1)
                     for n in EXCHANGED], T)
    own_half = _index(own.reshape(2, R // 2, PACK_COLS), ci)
    halves = core_gather(chip_gather(own_half, "gather_w_chips"), "gather_w_cores")
    rows = jnp.swapaxes(halves, 0, 1).reshape(N_CHIPS, T)
    full = {}
    off = 0
    for n, axis in EXCHANGED.items():
        seg = rows[:, off:off + sizes[n]]
        off += sizes[n]
        if n in KEPT_F32:
            seg = lax.bitcast_convert_type(seg.reshape(N_CHIPS, -1, 2), F32)
        full[n] = _join_shards(seg, W[n].shape, axis)

    n_ffn_tiles = Fh // GLU_TILE
    w_up_p = full["w_up"].reshape(DEPTH, D, 2, n_ffn_tiles, GLU_TILE).swapaxes(2, 3).reshape(DEPTH, D, 2 * Fh)
    n_mla = DEPTH - N_A
    q_rank = full["w_uq"].shape[1]
    wq = full["w_uq"].reshape(n_mla, q_rank, N_HEADS, QK_HEAD)
    w_uq_ext = jnp.concatenate([wq, jnp.zeros((n_mla, q_rank, N_HEADS, HEAD_PAD - QK_HEAD), BF16)],
                               axis=3).reshape(n_mla, q_rank, N_HEADS * HEAD_PAD)
    kv_w = KV_RANK + QK_ROPE
    w_dkv_ext = jnp.concatenate([full["w_dkv"], jnp.zeros((D, KV_RANK + LANES - kv_w), BF16)], axis=1)
    w_ukv = jnp.concatenate([full["w_uk"], full["w_uv"]], axis=1)

    c_all = device_gather(c, "gather_c").reshape(N_DEV, D)
    c_pad = jnp.concatenate([c_all, jnp.zeros((16 - N_DEV, D), F32)], axis=0)
    mod_b_mine = lax.dynamic_slice_in_dim(mod_b, chip * Es, Es, axis=1)
    mods_part = mods_fwd(c_pad, mod_w, mod_b_mine, "mods_fwd")
    mods_all = chip_gather(mods_part, "gather_mods")
    mods = jnp.swapaxes(_index(mods_all, dev, axis=2), 0, 1).reshape(DEPTH, E)
    mod = [[mods[l, k * D:(k + 1) * D] for k in range(6)] for l in range(DEPTH)]

    half = QK_ROPE // 2
    inv = 1.0 / (ROPE_THETA ** (jnp.arange(0, QK_ROPE, 2, dtype=F32) / QK_ROPE))
    inv_row = jnp.concatenate([inv, inv, jnp.zeros((LANES - 2 * half,), F32)]).reshape(1, LANES)
    tabs = rope_tables(positions[0].astype(F32).reshape(S_, 1), inv_row, "rope_tables")
    att_scale = QK_HEAD ** -0.5

    saved = []
    xcur = x0
    kv_saved = None
    K = knv = None
    for l in range(DEPTH):
        sh1, sc1, g1, sh2, sc2, g2 = mod[l]
        st = {"xin": xcur}
        if l < N_A:
            h1 = norm_fwd(xcur, norm1_g[l], sc1, sh1, F32, f"norm1_fwd{l}")
            st["pooled"] = _pool_call(h1, BF16, f"pool_fwd{l}", False)
            st["cs"] = g1 * full["pool_scale"][l]
            st["ypre"], xmid = gmm(st["pooled"], full["pool_w"][l], "nn", F32, f"pool_mm{l}", bias=full["pool_b"][l],
                                   res=xcur, colscale=st["cs"])
        else:
            j = l - N_A
            st["h1"] = norm_fwd(xcur, norm1_g[l], sc1, sh1, BF16, f"norm1_fwd{l}")
            st["ql"] = mm(st["h1"], full["w_dq"][j], "nn", F32, f"dq_mm{l}")
            st["cq"] = norm_fwd(st["ql"], q_norm_g[j], jnp.zeros_like(q_norm_g[j]), jnp.zeros_like(q_norm_g[j]), BF16,
                                f"qnorm_fwd{l}")
            qe = mm(st["cq"], w_uq_ext[j], "nn", F32, f"uq_mm{l}")
            st["Q"] = q_prep(qe, tabs, att_scale, False, f"q_prep{l}")
            st["o"], lse = attn_fwd(st["Q"], K, knv, f"attn_fwd{l}")
            st["lse"] = lse.reshape(N_HEADS, 1, S_)
            st["y"], xmid = mm(st["o"], full["w_o"][j], "nn", F32, f"wo_mm{l}", res=xcur, colscale=g1)
        st["xmid"] = xmid
        st["h2"] = norm_fwd(xmid, norm2_g[l], sc2, sh2, BF16, f"norm2_fwd{l}")
        st["u"] = mm(st["h2"], w_up_p[l], "nn", F32, f"up_mm{l}")
        st["z"] = glu_fwd(st["u"], full["conv_w"][l], conv_b[l], f"glu_fwd{l}")
        st["f"], xcur = mm(st["z"], full["w_down"][l], "nn", F32, f"down_mm{l}", tk=1408, res=xmid, colscale=g2)
        saved.append(st)
        if l == N_A - 1:
            xn = norm_fwd(xcur, kv_in_g, zD, zD, BF16, "kvin_fwd")
            kv_ext = mm(xn, w_dkv_ext, "nn", F32, "dkv_mm")
            lat = kv_ext[:, :KV_RANK]
            zk = jnp.zeros((KV_RANK,), F32)
            ckv = norm_fwd(lat, ckv_norm_g, zk, zk, BF16, "ckv_fwd")
            knv = mm(ckv, w_ukv, "nn", BF16, "ukv_mm")
            K = k_prep(knv, kv_ext, tabs, "k_prep")
            kv_saved = {"x": xcur, "xn": xn, "lat": lat, "ckv": ckv}

    dx, d_final_g, loss_part = loss_head(xcur, final_g, loss_target[0], "loss_head")
    loss = lax.psum(loss_part[0, 0], ("x", "y", "c"))

    G = {}
    dmods = [None] * DEPTH
    d_norm1 = [None] * DEPTH
    d_norm2 = [None] * DEPTH
    d_conv_b = [None] * DEPTH
    d_qnorm = [None] * n_mla
    dkv_acc = []
    for l in reversed(range(DEPTH)):
        sh1, sc1, g1, sh2, sc2, g2 = mod[l]
        st = saved[l]
        df, a2, _ = gate_bwd(dx, st["f"], g2, f"gate2_bwd{l}")
        dz = mm(df, full["w_down"][l], "nt", BF16, f"down_dx{l}")
        G[("w_down", l)] = mm(st["z"], df, "tn", F32, f"down_dw{l}")
        du, dcw, dcb = glu_bwd(st["u"], dz, full["conv_w"][l], conv_b[l], f"glu_bwd{l}")
        G[("conv_w", l)] = dcw
        d_conv_b[l] = dcb[0]
        dh2 = mm(du, w_up_p[l], "nt", BF16, f"up_dx{l}", tk=1408)
        G[("w_up", l)] = mm(st["h2"], du, "tn", F32, f"up_dw{l}").reshape(D, n_ffn_tiles, 2, GLU_TILE).swapaxes(
            1, 2).reshape(D, 2 * Fh)
        dxmid, s1, s2 = norm_bwd(st["xmid"], norm2_g[l], sc2, dh2, dx, f"norm2_bwd{l}")
        dsh2, dsc2, d_norm2[l] = s1[0], s2[0] * norm2_g[l], s2[0] * (1.0 + sc2)
        if l < N_A:
            dyp, a1, csum = gate_bwd(dxmid, st["ypre"], st["cs"], f"gate1_bwd{l}")
            dg1 = full["pool_scale"][l] * a1[0]
            G[("pool_scale", l)] = g1 * a1[0]
            G[("pool_b", l)] = st["cs"] * csum[0]
            dpooled = gmm(dyp, full["pool_w"][l], "nt", F32, f"pool_dx{l}")
            G[("pool_w", l)] = gmm(st["pooled"], dyp, "tn", F32, f"pool_dw{l}")
            dh1 = _pool_call(dpooled, F32, f"pool_bwd{l}", True)
        else:
            j = l - N_A
            dy, a1, _ = gate_bwd(dxmid, st["y"], g1, f"gate1_bwd{l}")
            dg1 = a1[0]
            do = mm(dy, full["w_o"][j], "nt", BF16, f"wo_dx{l}")
            G[("w_o", j)] = mm(st["o"], dy, "tn", F32, f"wo_dw{l}")
            delta = attn_delta(st["o"], do, f"attn_delta{l}").reshape(N_HEADS, 1, S_)
            dQ, dK, dV = attn_bwd(st["Q"], K, knv, do, st["lse"], delta, f"attn_bwd{l}")
            dkv_acc.append((dK, dV))
            dqe = q_prep(dQ, tabs, att_scale, True, f"q_prep_bwd{l}")
            dcq = mm(dqe, w_uq_ext[j], "nt", F32, f"uq_dx{l}")
            G[("w_uq", j)] = mm(st["cq"], dqe, "tn", F32, f"uq_dw{l}").reshape(q_rank, N_HEADS, HEAD_PAD)[
                :, :, :QK_HEAD].reshape(q_rank, N_HEADS * QK_HEAD)
            zq = jnp.zeros_like(q_norm_g[j])
            dql, _, s2q = norm_bwd(st["ql"], q_norm_g[j], zq, dcq, None, f"qnorm_bwd{l}")
            d_qnorm[j] = s2q[0]
            dh1 = mm(dql, full["w_dq"][j], "nt", BF16, f"dq_dx{l}")
            G[("w_dq", j)] = mm(st["h1"], dql, "tn", F32, f"dq_dw{l}")
        dx, s1, s2 = norm_bwd(st["xin"], norm1_g[l], sc1, dh1, dxmid, f"norm1_bwd{l}")
        dsh1, dsc1, d_norm1[l] = s1[0], s2[0] * norm1_g[l], s2[0] * (1.0 + sc1)
        dmods[l] = jnp.concatenate([dsh1, dsc1, dg1, dsh2, dsc2, a2[0]])
        if l == N_A:
            (dk_a, dv_a), (dk_b, dv_b) = dkv_acc
            dknv, d_tk = k_prep_bwd(dk_a, dk_b, dv_a, dv_b, tabs, "k_prep_bwd")
            dckv = mm(dknv, w_ukv, "nt", F32, "ukv_dx")
            d_ukv = mm(kv_saved["ckv"], dknv, "tn", F32, "ukv_dw")
            G[("w_uk", 0)], G[("w_uv", 0)] = d_ukv[:, :N_HEADS * QK_NOPE], d_ukv[:, N_HEADS * QK_NOPE:]
            zk = jnp.zeros((KV_RANK,), F32)
            dlat, _, s2c = norm_bwd(kv_saved["lat"], ckv_norm_g, zk, dckv, None, "ckv_bwd")
            d_ckv_g = s2c[0]
            dkv_ext = jnp.concatenate([dlat, d_tk], axis=1)
            dxn = mm(dkv_ext, w_dkv_ext, "nt", BF16, "dkv_dx")
            G[("w_dkv", 0)] = mm(kv_saved["xn"], dkv_ext, "tn", F32, "dkv_dw")[:, :kv_w]
            dx, _, s2k = norm_bwd(kv_saved["x"], kv_in_g, zD, dxn, dx, "kvin_bwd")
            d_kvin_g = s2k[0]

    def stacked(n):
        k = W[n].shape[0] if W[n].ndim > 2 or n in ("pool_b", "pool_scale") else None
        return G[(n, 0)] if k is None else jnp.stack([G[(n, i)] for i in range(k)])

    gsizes = {n: math.prod(W[n].shape) for n in EXCHANGED}
    Tg = _padded(sum(gsizes.values()), PACK_ALIGN)
    Rg = Tg // PACK_COLS
    gflat = _flat_pad([_split_shards(stacked(n), axis) for n, axis in EXCHANGED.items()], Tg)
    gflat = gflat.reshape(N_CHIPS, 2, Rg // 2, PACK_COLS)
    keep = _index(gflat, ci, axis=1).reshape(N_CHIPS * Rg // 2, PACK_COLS)
    give = _index(gflat, 1 - ci, axis=1).astype(BF16).reshape(N_CHIPS * Rg // 2, PACK_COLS)
    chip_sum = add_round(keep, core_swap(give, "reduce_cores"), "reduce_cores_add")
    got = chip_all_to_all(chip_sum.reshape(N_CHIPS, Rg // 2, PACK_COLS), "reduce_chips")
    red = core_gather(sum_parts(got, "reduce_chips_add"), "reduce_gather").reshape(Tg)

    grads, deltas, new_m, new_v = {}, {}, {}, {}
    off = 0
    for n in EXCHANGED:
        grads[n] = red[off:off + gsizes[n]].reshape(W[n].shape)
        off += gsizes[n]
        deltas[n], new_m[n], new_v[n] = adamw(W[n], grads[n], M1[n], V2[n], f"adamw_{n}")

    small = {"mod_b": jnp.stack(dmods), "norm1_g": jnp.stack(d_norm1), "norm2_g": jnp.stack(d_norm2),
             "kv_in_g": d_kvin_g, "ckv_norm_g": d_ckv_g, "q_norm_g": jnp.stack(d_qnorm),
             "conv_b": jnp.stack(d_conv_b), "final_g": d_final_g[0]}
    ssizes = {n: math.prod(W[n].shape) for n in REPLICATED}
    Ts = _padded(sum(ssizes.values()), 8 * PACK_COLS)

    def pack_small(d):
        return _flat_pad([d[n].reshape(-1) for n in REPLICATED], Ts).reshape(Ts // PACK_COLS, PACK_COLS)

    parts = device_gather(pack_small(small), "gather_small")
    outs = adamw_sum(parts, pack_small(W), pack_small(M1), pack_small(V2), "adamw_small")
    off = 0
    for n in REPLICATED:
        for dst, o in zip((grads, deltas, new_m, new_v), outs):
            dst[n] = o.reshape(-1)[off:off + ssizes[n]].reshape(W[n].shape)
        off += ssizes[n]

    dm_all = parts.reshape(N_DEV, -1)[:, :DEPTH * E].reshape(N_DEV, DEPTH, E)
    dm_mine = jnp.swapaxes(lax.dynamic_slice_in_dim(dm_all, chip * Es, Es, axis=2), 0, 1)
    grads["mod_w"], deltas["mod_w"], new_m["mod_w"], new_v["mod_w"] = adamw_modw(
        c_all.reshape(N_DEV, D, 1), dm_mine, mod_w, m_mod_w, v_mod_w, "adamw_mod_w")

    return (loss, dx.reshape(x.shape), *[grads[n] for n in WEIGHT_ORDER], *[deltas[n] for n in WEIGHT_ORDER],
            *[new_m[n] for n in WEIGHT_ORDER], *[new_v[n] for n in WEIGHT_ORDER])
```

```python
import functools
import math

import jax
import jax.numpy as jnp
from jax import lax
from jax.experimental import pallas as pl
from jax.experimental.pallas import tpu as pltpu

F32 = jnp.float32
BF16 = jnp.bfloat16
MESH = pl.DeviceIdType.MESH

DEPTH = 4
N_A = 2
POOL_WINDOWS = (2, 4, 8, 16)
N_GROUPS = 4
N_HEADS = 8
QK_NOPE = 128
QK_ROPE = 64
V_HEAD = 128
QK_HEAD = QK_NOPE + QK_ROPE
HEAD_PAD = 256
KV_RANK = 256
ROPE_THETA = 10000.0
EPS = 1e-6
ADAM_LR = 0.001
ADAM_B1 = 0.9
ADAM_B2 = 0.999
ADAM_EPS = 1e-08
ADAM_WD = 0.01
ADAM_STEP = 10

N_CHIPS = 4
N_DEV = 8
LANES = 128
PACK_COLS = 1024
VMEM_LIMIT = 56 * 1024 * 1024
GLU_TILE = 256
ATT_BLOCK = 256
ATT_Q_BLOCK = 256
ATT_K_BLOCK = 512
ATT_HEADS_PER_STEP = 2


def _cparams(*sem):
    return pltpu.CompilerParams(dimension_semantics=sem if sem else None, vmem_limit_bytes=VMEM_LIMIT)


def _pick(n, target, mult):
    best = None
    d = mult
    while d <= min(n, target):
        if n % d == 0:
            best = d
        d += mult
    return n if best is None else best


def _row(v):
    return v.reshape(1, -1).astype(F32)


_DIMS = {"nn": (((1,), (0,)), ((), ())), "nt": (((1,), (1,)), ((), ())), "tn": (((0,), (0,)), ((), ()))}


def _mm_body(mode, nk, has_bias, has_res):
    def body(*refs):
        a_ref, b_ref = refs[0], refs[1]
        pos = 2
        bias_ref = res_ref = cs_ref = None
        if has_bias:
            bias_ref = refs[pos]
            pos += 1
        if has_res:
            res_ref, cs_ref = refs[pos], refs[pos + 1]
            pos += 2
        o_ref = refs[pos]
        pos += 1
        o2_ref = None
        if has_res:
            o2_ref = refs[pos]
            pos += 1
        acc_ref = refs[pos]
        k = pl.program_id(2)

        @pl.when(k == 0)
        def _():
            acc_ref[...] = jnp.zeros_like(acc_ref)

        acc_ref[...] += lax.dot_general(a_ref[...].astype(BF16), b_ref[...].astype(BF16), _DIMS[mode],
                                        preferred_element_type=F32)

        @pl.when(k == nk - 1)
        def _():
            y = acc_ref[...]
            if has_bias:
                y = y + bias_ref[...]
            o_ref[...] = y.astype(o_ref.dtype)
            if has_res:
                o2_ref[...] = res_ref[...] + cs_ref[...] * y

    return body


def mm(a, b, mode, out_dtype, name, *, tm=1024, tn=512, tk=1024, bias=None, res=None, colscale=None, layer=None):
    bshape = b.shape if layer is None else b.shape[1:]
    if mode == "nn":
        (M, K), N = a.shape, bshape[1]
    elif mode == "nt":
        (M, K), N = a.shape, bshape[0]
    else:
        (K, M), N = a.shape, bshape[1]
    tm = _pick(M, tm, LANES if mode == "tn" else 8)
    tn = _pick(N, tn, LANES)
    tk = _pick(K, tk, LANES) if mode != "tn" else _pick(K, tk, 8)
    nk = K // tk
    a_spec = {"nn": pl.BlockSpec((tm, tk), lambda i, j, k: (i, k)),
              "nt": pl.BlockSpec((tm, tk), lambda i, j, k: (i, k)),
              "tn": pl.BlockSpec((tk, tm), lambda i, j, k: (k, i))}[mode]
    b_blk, b_map = {"nn": ((tk, tn), lambda i, j, k: (k, j)),
                    "nt": ((tn, tk), lambda i, j, k: (j, k)),
                    "tn": ((tk, tn), lambda i, j, k: (k, j))}[mode]
    if layer is None:
        b_spec = pl.BlockSpec(b_blk, b_map)
    else:
        b_spec = pl.BlockSpec((None,) + b_blk, lambda i, j, k: (layer,) + b_map(i, j, k))
    o_spec = pl.BlockSpec((tm, tn), lambda i, j, k: (i, j))
    v_spec = pl.BlockSpec((1, tn), lambda i, j, k: (0, j))
    in_specs, args = [a_spec, b_spec], [a, b]
    if bias is not None:
        in_specs.append(v_spec)
        args.append(_row(bias))
    out_shape = [jax.ShapeDtypeStruct((M, N), out_dtype)]
    out_specs = [o_spec]
    if res is not None:
        in_specs += [o_spec, v_spec]
        args += [res, _row(colscale)]
        out_shape.append(jax.ShapeDtypeStruct((M, N), F32))
        out_specs.append(o_spec)
    outs = pl.pallas_call(
        _mm_body(mode, nk, bias is not None, res is not None),
        grid=(M // tm, N // tn, nk),
        in_specs=in_specs, out_specs=out_specs, out_shape=out_shape,
        scratch_shapes=[pltpu.VMEM((tm, tn), F32)],
        compiler_params=_cparams("parallel", "parallel", "arbitrary"),
        name=name,
    )(*args)
    return outs if res is not None else outs[0]


def gmm(a, w, mode, out_dtype, name, *, bias=None, res=None, colscale=None, tr=512):
    S_ = a.shape[0]
    G = N_GROUPS
    C = a.shape[1] // G
    tr = _pick(S_, tr, 8)
    nr = S_ // tr
    if mode == "tn":
        def body(a_ref, b_ref, o_ref, acc_ref):
            i = pl.program_id(1)

            @pl.when(i == 0)
            def _():
                acc_ref[...] = jnp.zeros_like(acc_ref)

            acc_ref[...] += lax.dot_general(a_ref[...].astype(BF16), b_ref[...].astype(BF16), _DIMS["tn"],
                                            preferred_element_type=F32)

            @pl.when(i == nr - 1)
            def _():
                o_ref[...] = acc_ref[...].astype(o_ref.dtype)

        blk = pl.BlockSpec((tr, C), lambda g, i: (i, g))
        return pl.pallas_call(
            body, grid=(G, nr), in_specs=[blk, blk],
            out_specs=pl.BlockSpec((None, C, C), lambda g, i: (g, 0, 0)),
            out_shape=jax.ShapeDtypeStruct((G, C, C), out_dtype),
            scratch_shapes=[pltpu.VMEM((C, C), F32)],
            compiler_params=_cparams("parallel", "arbitrary"), name=name,
        )(a, w)

    has_bias, has_res = bias is not None, res is not None

    def body(*refs):
        a_ref, w_ref = refs[0], refs[1]
        pos = 2
        if has_bias:
            bias_ref = refs[pos]
            pos += 1
        if has_res:
            res_ref, cs_ref = refs[pos], refs[pos + 1]
            pos += 2
        o_ref = refs[pos]
        y = lax.dot_general(a_ref[...].astype(BF16), w_ref[...].astype(BF16), _DIMS[mode],
                            preferred_element_type=F32)
        if has_bias:
            y = y + bias_ref[...]
        o_ref[...] = y.astype(o_ref.dtype)
        if has_res:
            refs[pos + 1][...] = res_ref[...] + cs_ref[...] * y

    blk = pl.BlockSpec((tr, C), lambda i, g: (i, g))
    vec = pl.BlockSpec((1, C), lambda i, g: (0, g))
    in_specs = [blk, pl.BlockSpec((None, C, C), lambda i, g: (g, 0, 0))]
    args = [a, w]
    if has_bias:
        in_specs.append(vec)
        args.append(_row(bias))
    out_shape = [jax.ShapeDtypeStruct(a.shape, out_dtype)]
    out_specs = [blk]
    if has_res:
        in_specs += [blk, vec]
        args += [res, _row(colscale)]
        out_shape.append(jax.ShapeDtypeStruct(a.shape, F32))
        out_specs.append(blk)
    outs = pl.pallas_call(
        body, grid=(nr, G), in_specs=in_specs, out_specs=out_specs, out_shape=out_shape,
        compiler_params=_cparams("parallel", "parallel"), name=name,
    )(*args)
    return outs if has_res else outs[0]


def norm_fwd(x, g, sc, sh, out_dtype, name, tr=512):
    S_, Dn = x.shape
    tr = _pick(S_, tr, 8)

    def body(x_ref, g_ref, sc_ref, sh_ref, o_ref):
        xv = x_ref[...]
        r = lax.rsqrt(jnp.mean(xv * xv, axis=-1, keepdims=True) + EPS)
        o_ref[...] = (((xv * r) * g_ref[...]) * (1.0 + sc_ref[...]) + sh_ref[...]).astype(o_ref.dtype)

    blk = pl.BlockSpec((tr, Dn), lambda i: (i, 0))
    vec = pl.BlockSpec((1, Dn), lambda i: (0, 0))
    return pl.pallas_call(
        body, grid=(S_ // tr,), in_specs=[blk, vec, vec, vec], out_specs=blk,
        out_shape=jax.ShapeDtypeStruct((S_, Dn), out_dtype),
        compiler_params=_cparams("parallel"), name=name,
    )(x, _row(g), _row(sc), _row(sh))


def norm_bwd(x, g, sc, dh, dres, name, tr=512):
    S_, Dn = x.shape
    tr = _pick(S_, tr, 8)
    has_res = dres is not None

    def body(*refs):
        x_ref, g_ref, sc_ref, dh_ref = refs[:4]
        pos = 4
        if has_res:
            dres_ref = refs[pos]
            pos += 1
        dx_ref, s1_ref, s2_ref = refs[pos:pos + 3]
        i = pl.program_id(0)

        @pl.when(i == 0)
        def _():
            s1_ref[...] = jnp.zeros_like(s1_ref)
            s2_ref[...] = jnp.zeros_like(s2_ref)

        xv = x_ref[...]
        r = lax.rsqrt(jnp.mean(xv * xv, axis=-1, keepdims=True) + EPS)
        n = xv * r
        dhv = dh_ref[...].astype(F32)
        dn = dhv * (g_ref[...] * (1.0 + sc_ref[...]))
        dx = r * (dn - n * jnp.mean(dn * n, axis=-1, keepdims=True))
        if has_res:
            dx = dx + dres_ref[...]
        dx_ref[...] = dx
        s1_ref[...] += jnp.sum(dhv, axis=0, keepdims=True)
        s2_ref[...] += jnp.sum(dhv * n, axis=0, keepdims=True)

    blk = pl.BlockSpec((tr, Dn), lambda i: (i, 0))
    vec = pl.BlockSpec((1, Dn), lambda i: (0, 0))
    in_specs, args = [blk, vec, vec, blk], [x, _row(g), _row(sc), dh]
    if has_res:
        in_specs.append(blk)
        args.append(dres)
    vshape = jax.ShapeDtypeStruct((1, Dn), F32)
    return pl.pallas_call(
        body, grid=(S_ // tr,), in_specs=in_specs, out_specs=[blk, vec, vec],
        out_shape=[jax.ShapeDtypeStruct((S_, Dn), F32), vshape, vshape],
        compiler_params=_cparams("arbitrary"), name=name,
    )(*args)


def gate_bwd(dx, y, colscale, name, tr=512):
    S_, Dn = dx.shape
    tr = _pick(S_, tr, 8)

    def body(dx_ref, y_ref, cs_ref, d_ref, a_ref, c_ref):
        i = pl.program_id(0)

        @pl.when(i == 0)
        def _():
            a_ref[...] = jnp.zeros_like(a_ref)
            c_ref[...] = jnp.zeros_like(c_ref)

        dxv = dx_ref[...]
        d_ref[...] = (dxv * cs_ref[...]).astype(d_ref.dtype)
        a_ref[...] += jnp.sum(dxv * y_ref[...].astype(F32), axis=0, keepdims=True)
        c_ref[...] += jnp.sum(dxv, axis=0, keepdims=True)

    blk = pl.BlockSpec((tr, Dn), lambda i: (i, 0))
    vec = pl.BlockSpec((1, Dn), lambda i: (0, 0))
    vshape = jax.ShapeDtypeStruct((1, Dn), F32)
    return pl.pallas_call(
        body, grid=(S_ // tr,), in_specs=[blk, blk, vec], out_specs=[blk, vec, vec],
        out_shape=[jax.ShapeDtypeStruct((S_, Dn), BF16), vshape, vshape],
        compiler_params=_cparams("arbitrary"), name=name,
    )(dx, y, _row(colscale))


def loss_head(x, g, target, name, tr=512):
    S_, Dn = x.shape
    tr = _pick(S_, tr, 8)

    def body(x_ref, g_ref, t_ref, dx_ref, dg_ref, loss_ref):
        i = pl.program_id(0)

        @pl.when(i == 0)
        def _():
            dg_ref[...] = jnp.zeros_like(dg_ref)
            loss_ref[...] = jnp.zeros_like(loss_ref)

        xv = x_ref[...]
        r = lax.rsqrt(jnp.mean(xv * xv, axis=-1, keepdims=True) + EPS)
        n = xv * r
        e = n * g_ref[...] - t_ref[...]
        loss_ref[...] += 0.5 * jnp.sum(jnp.mean(e * e, axis=-1, keepdims=True), axis=0, keepdims=True)
        dy = e * (1.0 / Dn)
        dg_ref[...] += jnp.sum(dy * n, axis=0, keepdims=True)
        dn = dy * g_ref[...]
        dx_ref[...] = r * (dn - n * jnp.mean(dn * n, axis=-1, keepdims=True))

    blk = pl.BlockSpec((tr, Dn), lambda i: (i, 0))
    vec = pl.BlockSpec((1, Dn), lambda i: (0, 0))
    one = pl.BlockSpec((1, 1), lambda i: (0, 0))
    return pl.pallas_call(
        body, grid=(S_ // tr,), in_specs=[blk, vec, blk], out_specs=[blk, vec, one],
        out_shape=[jax.ShapeDtypeStruct((S_, Dn), F32), jax.ShapeDtypeStruct((1, Dn), F32),
                   jax.ShapeDtypeStruct((1, 1), F32)],
        compiler_params=_cparams("arbitrary"), name=name,
    )(x, _row(g), target)


POOL_HALO = 16
POOL_CHUNK = 512


def _rows(ref, lo, hi, n_rows):
    parts = []
    if lo < 0:
        parts.append(jnp.zeros((-lo, ref.shape[1]), F32))
    parts.append(ref[max(lo, 0):min(hi, n_rows), :].astype(F32))
    if hi > n_rows:
        parts.append(jnp.zeros((hi - n_rows, ref.shape[1]), F32))
    return parts[0] if len(parts) == 1 else jnp.concatenate(parts, axis=0)


def _window_sum(e, w, back):
    n = e.shape[0]
    s, width = e, 1
    while width < w:
        s = s + pltpu.roll(s, width if back else n - width, 0)
        width *= 2
    return s


def _pool_call(h, out_dtype, name, backward):
    S_, Dn = h.shape
    C = Dn // N_GROUPS
    ch = _pick(S_, POOL_CHUNK, 8)

    def body(h_ref, o_ref):
        g = pl.program_id(0)
        for gi, w in enumerate(POOL_WINDOWS):
            @pl.when(g == gi)
            def _(w=w):
                for r0 in range(0, S_, ch):
                    t = (r0 + lax.broadcasted_iota(jnp.int32, (ch, C), 0)).astype(F32)
                    cnt = jnp.minimum(t + 1.0, float(w))
                    if not backward:
                        ext = _rows(h_ref, r0 - POOL_HALO, r0 + ch, S_)
                        cur = ext[POOL_HALO:]
                        mean = _window_sum(ext, w, True)[POOL_HALO:] / cnt
                        o_ref[r0:r0 + ch, :] = (mean - cur).astype(o_ref.dtype)
                    else:
                        ext = _rows(h_ref, r0, r0 + ch + POOL_HALO, S_)
                        text = (r0 + lax.broadcasted_iota(jnp.int32, (ch + POOL_HALO, C), 0)).astype(F32)
                        e = ext / jnp.minimum(text + 1.0, float(w))
                        o_ref[r0:r0 + ch, :] = (_window_sum(e, w, False)[:ch] - ext[:ch]).astype(o_ref.dtype)

    blk = pl.BlockSpec((S_, C), lambda g: (0, g))
    return pl.pallas_call(
        body, grid=(N_GROUPS,), in_specs=[blk], out_specs=blk,
        out_shape=jax.ShapeDtypeStruct((S_, Dn), out_dtype),
        compiler_params=_cparams("parallel"), name=name,
    )(h)


GLU_CHUNK = 512
_SQRT_HALF = 0.7071067811865476
_INV_SQRT_2PI = 0.3989422804014327


def _gelu(a):
    return 0.5 * a * (1.0 + lax.erf(a * _SQRT_HALF))


def _gelu_grad(a):
    return 0.5 * (1.0 + lax.erf(a * _SQRT_HALF)) + a * (_INV_SQRT_2PI * jnp.exp(-0.5 * a * a))


def glu_fwd(u, conv_w, conv_b, name):
    S_, F2 = u.shape
    Fh = F2 // 2
    tf = GLU_TILE
    ch = _pick(S_, GLU_CHUNK, 8)

    def body(u_ref, cw_ref, cb_ref, z_ref):
        cw0, cw1, cw2 = cw_ref[0:1, :], cw_ref[1:2, :], cw_ref[2:3, :]
        cb = cb_ref[...]
        a_ref = u_ref.at[:, 0:tf]
        for r0 in range(0, S_, ch):
            ext = _rows(a_ref, r0 - 8, r0 + ch, S_)
            a0 = ext[8:]
            a1 = pltpu.roll(ext, 1, 0)[8:]
            a2 = pltpu.roll(ext, 2, 0)[8:]
            ac = a2 * cw0 + a1 * cw1 + a0 * cw2 + cb
            z_ref[r0:r0 + ch, :] = (_gelu(ac) * u_ref[r0:r0 + ch, tf:2 * tf].astype(F32)).astype(z_ref.dtype)

    return pl.pallas_call(
        body, grid=(Fh // tf,),
        in_specs=[pl.BlockSpec((S_, 2 * tf), lambda j: (0, j)), pl.BlockSpec((3, tf), lambda j: (0, j)),
                  pl.BlockSpec((1, tf), lambda j: (0, j))],
        out_specs=pl.BlockSpec((S_, tf), lambda j: (0, j)),
        out_shape=jax.ShapeDtypeStruct((S_, Fh), BF16),
        compiler_params=_cparams("parallel"), name=name,
    )(u, conv_w, _row(conv_b))


def glu_bwd(u, dz, conv_w, conv_b, name):
    S_, F2 = u.shape
    Fh = F2 // 2
    tf = GLU_TILE
    ch = _pick(S_, GLU_CHUNK, 8)

    def body(u_ref, dz_ref, cw_ref, cb_ref, du_ref, dcw_ref, dcb_ref):
        cw0, cw1, cw2 = cw_ref[0:1, :], cw_ref[1:2, :], cw_ref[2:3, :]
        cb = cb_ref[...]
        a_ref = u_ref.at[:, 0:tf]
        v_ref = u_ref.at[:, tf:2 * tf]
        acc = [jnp.zeros((1, tf), F32) for _ in range(4)]
        n = ch + 8
        for r0 in range(0, S_, ch):
            ext = _rows(a_ref, r0 - 8, r0 + n, S_)
            a0 = ext[8:]
            a1 = pltpu.roll(ext, 1, 0)[8:]
            a2 = pltpu.roll(ext, 2, 0)[8:]
            ac = a2 * cw0 + a1 * cw1 + a0 * cw2 + cb
            vv = _rows(v_ref, r0, r0 + n, S_)
            dzv = _rows(dz_ref, r0, r0 + n, S_)
            gl = _gelu(ac)
            dac = dzv * vv * _gelu_grad(ac)
            da = (dac * cw2 + pltpu.roll(dac, n - 1, 0) * cw1 + pltpu.roll(dac, n - 2, 0) * cw0)[:ch]
            du_ref[r0:r0 + ch, 0:tf] = da.astype(du_ref.dtype)
            du_ref[r0:r0 + ch, tf:2 * tf] = (dzv[:ch] * gl[:ch]).astype(du_ref.dtype)
            dc = dac[:ch]
            acc[0] = acc[0] + jnp.sum(dc * a2[:ch], axis=0, keepdims=True)
            acc[1] = acc[1] + jnp.sum(dc * a1[:ch], axis=0, keepdims=True)
            acc[2] = acc[2] + jnp.sum(dc * a0[:ch], axis=0, keepdims=True)
            acc[3] = acc[3] + jnp.sum(dc, axis=0, keepdims=True)
        dcw_ref[0:1, :] = acc[0]
        dcw_ref[1:2, :] = acc[1]
        dcw_ref[2:3, :] = acc[2]
        dcb_ref[...] = acc[3]

    return pl.pallas_call(
        body, grid=(Fh // tf,),
        in_specs=[pl.BlockSpec((S_, 2 * tf), lambda j: (0, j)), pl.BlockSpec((S_, tf), lambda j: (0, j)),
                  pl.BlockSpec((3, tf), lambda j: (0, j)), pl.BlockSpec((1, tf), lambda j: (0, j))],
        out_specs=[pl.BlockSpec((S_, 2 * tf), lambda j: (0, j)), pl.BlockSpec((3, tf), lambda j: (0, j)),
                   pl.BlockSpec((1, tf), lambda j: (0, j))],
        out_shape=[jax.ShapeDtypeStruct((S_, F2), BF16), jax.ShapeDtypeStruct((3, Fh), F32),
                   jax.ShapeDtypeStruct((1, Fh), F32)],
        compiler_params=_cparams("parallel"), name=name,
    )(u, dz, conv_w, _row(conv_b))


def rope_tables(pos, inv, name, tr=512):
    S_ = pos.shape[0]
    tr = _pick(S_, tr, 8)

    def body(p_ref, inv_ref, c_ref, s1_ref, s2_ref):
        ang = p_ref[...] * inv_ref[...]
        lane = lax.broadcasted_iota(jnp.int32, ang.shape, 1)
        half = QK_ROPE // 2
        cosv, sinv = jnp.cos(ang), jnp.sin(ang)
        c_ref[...] = jnp.where(lane < QK_ROPE, cosv, 0.0)
        s1_ref[...] = jnp.where(lane < half, -sinv, 0.0)
        s2_ref[...] = jnp.where((lane >= half) & (lane < QK_ROPE), sinv, 0.0)

    blk = pl.BlockSpec((tr, LANES), lambda i: (i, 0))
    shp = jax.ShapeDtypeStruct((S_, LANES), F32)
    return pl.pallas_call(
        body, grid=(S_ // tr,),
        in_specs=[pl.BlockSpec((tr, 1), lambda i: (i, 0)), pl.BlockSpec((1, LANES), lambda i: (0, 0))],
        out_specs=[blk, blk, blk], out_shape=[shp, shp, shp],
        compiler_params=_cparams("parallel"), name=name,
    )(pos, inv)


_HALF = QK_ROPE // 2


def _rope(t, c, s1, s2):
    return t * c + pltpu.roll(t, LANES - _HALF, 1) * s1 + pltpu.roll(t, _HALF, 1) * s2


def _rope_t(d, c, s1, s2):
    return d * c + pltpu.roll(d * s1, _HALF, 1) + pltpu.roll(d * s2, LANES - _HALF, 1)


def q_prep(q, tabs, scale, backward, name, tr=512):
    S_, W = q.shape
    tr = _pick(S_, tr, 8)

    def body(q_ref, c_ref, s1_ref, s2_ref, o_ref):
        o_ref[:, 0:LANES] = (q_ref[:, 0:LANES].astype(F32) * scale).astype(o_ref.dtype)
        t = q_ref[:, LANES:2 * LANES].astype(F32)
        fn = _rope_t if backward else _rope
        o_ref[:, LANES:2 * LANES] = (fn(t, c_ref[...], s1_ref[...], s2_ref[...]) * scale).astype(o_ref.dtype)

    blk = pl.BlockSpec((tr, HEAD_PAD), lambda i, h: (i, h))
    tab = pl.BlockSpec((tr, LANES), lambda i, h: (i, 0))
    return pl.pallas_call(
        body, grid=(S_ // tr, W // HEAD_PAD), in_specs=[blk, tab, tab, tab], out_specs=blk,
        out_shape=jax.ShapeDtypeStruct((S_, W), BF16),
        compiler_params=_cparams("parallel", "parallel"), name=name,
    )(q, *tabs)


def k_prep(knv, kv_ext, tabs, name, tr=512):
    S_ = knv.shape[0]
    tr = _pick(S_, tr, 8)

    def body(kn_ref, v_ref, t_ref, c_ref, s1_ref, s2_ref, o_ref, vx_ref):
        o_ref[:, 0:LANES] = kn_ref[...].astype(o_ref.dtype)
        o_ref[:, LANES:2 * LANES] = _rope(t_ref[...], c_ref[...], s1_ref[...], s2_ref[...]).astype(o_ref.dtype)
        vx_ref[:, 0:V_HEAD] = v_ref[...].astype(vx_ref.dtype)
        vx_ref[:, V_HEAD:HEAD_PAD] = jnp.ones((tr, HEAD_PAD - V_HEAD), vx_ref.dtype)

    tab = pl.BlockSpec((tr, LANES), lambda i, h: (i, 0))
    head = pl.BlockSpec((tr, HEAD_PAD), lambda i, h: (i, h))
    shp = jax.ShapeDtypeStruct((S_, N_HEADS * HEAD_PAD), BF16)
    return pl.pallas_call(
        body, grid=(S_ // tr, N_HEADS),
        in_specs=[pl.BlockSpec((tr, LANES), lambda i, h: (i, h)),
                  pl.BlockSpec((tr, V_HEAD), lambda i, h: (i, N_HEADS + h)),
                  pl.BlockSpec((tr, LANES), lambda i, h: (i, KV_RANK // LANES)), tab, tab, tab],
        out_specs=[head, head], out_shape=[shp, shp],
        compiler_params=_cparams("parallel", "parallel"), name=name,
    )(knv, knv, kv_ext, *tabs)


def k_prep_bwd(dk_a, dk_b, dv_a, dv_b, tabs, name, tr=256):
    S_ = dk_a.shape[0]
    tr = _pick(S_, tr, 8)
    HV = N_HEADS * V_HEAD

    def body(ka_ref, kb_ref, va_ref, vb_ref, c_ref, s1_ref, s2_ref, o_ref, t_ref):
        dr = jnp.zeros((tr, LANES), F32)
        for h in range(N_HEADS):
            lo = h * HEAD_PAD
            o_ref[:, h * LANES:(h + 1) * LANES] = (ka_ref[:, lo:lo + LANES] + kb_ref[:, lo:lo + LANES]).astype(o_ref.dtype)
            dr = dr + ka_ref[:, lo + LANES:lo + 2 * LANES] + kb_ref[:, lo + LANES:lo + 2 * LANES]
        o_ref[:, HV:2 * HV] = (va_ref[...] + vb_ref[...]).astype(o_ref.dtype)
        t_ref[...] = _rope_t(dr, c_ref[...], s1_ref[...], s2_ref[...])

    kblk = pl.BlockSpec((tr, N_HEADS * HEAD_PAD), lambda i: (i, 0))
    vblk = pl.BlockSpec((tr, HV), lambda i: (i, 0))
    tab = pl.BlockSpec((tr, LANES), lambda i: (i, 0))
    return pl.pallas_call(
        body, grid=(S_ // tr,), in_specs=[kblk, kblk, vblk, vblk, tab, tab, tab],
        out_specs=[pl.BlockSpec((tr, 2 * HV), lambda i: (i, 0)), tab],
        out_shape=[jax.ShapeDtypeStruct((S_, 2 * HV), BF16), jax.ShapeDtypeStruct((S_, LANES), F32)],
        compiler_params=_cparams("parallel"), name=name,
    )(dk_a, dk_b, dv_a, dv_b, *tabs)


_NEG = -1e30


def attn_fwd(q, k, vx, name):
    S_ = q.shape[0]
    TQ = _pick(S_, ATT_Q_BLOCK, 8)
    TK = _pick(S_, ATT_K_BLOCK, TQ)
    HP = ATT_HEADS_PER_STEP
    W = HP * HEAD_PAD
    ratio = TK // TQ

    def body(q_ref, k_ref, v_ref, o_ref, lse_ref):
        i = pl.program_id(1)
        qs = [q_ref[:, h * HEAD_PAD:(h + 1) * HEAD_PAD] for h in range(HP)]

        def step(j, carry, masked):
            start = pl.multiple_of(j * TK, TK)
            out = []
            for h in range(HP):
                m, acc = carry[h]
                cols = slice(h * HEAD_PAD, (h + 1) * HEAD_PAD)
                s = lax.dot_general(qs[h], k_ref[pl.ds(start, TK), cols], _DIMS["nt"], preferred_element_type=F32)
                if masked:
                    rowi = i * TQ + lax.broadcasted_iota(jnp.int32, (TQ, TK), 0)
                    coli = j * TK + lax.broadcasted_iota(jnp.int32, (TQ, TK), 1)
                    s = jnp.where(coli <= rowi, s, _NEG)
                m_new = jnp.maximum(m, jnp.max(s, axis=-1, keepdims=True))
                alpha = jnp.exp(m - m_new)
                p = jnp.exp(s - m_new).astype(BF16)
                acc = alpha * acc + lax.dot_general(p, v_ref[pl.ds(start, TK), cols], _DIMS["nn"],
                                                    preferred_element_type=F32)
                out.append((m_new, acc))
            return tuple(out)

        init = tuple((jnp.full((TQ, 1), _NEG, F32), jnp.zeros((TQ, HEAD_PAD), F32)) for _ in range(HP))
        last = i // ratio
        carry = step(last, lax.fori_loop(0, last, functools.partial(step, masked=False), init), True)
        for h in range(HP):
            m, acc = carry[h]
            l = acc[:, V_HEAD:]
            o_ref[:, h * V_HEAD:(h + 1) * V_HEAD] = (acc[:, :V_HEAD] / l).astype(o_ref.dtype)
            lse_ref[h] = m + jnp.log(jnp.max(l, axis=-1, keepdims=True))

    return pl.pallas_call(
        body, grid=(N_HEADS // HP, S_ // TQ),
        in_specs=[pl.BlockSpec((TQ, W), lambda g, i: (i, g)),
                  pl.BlockSpec((S_, W), lambda g, i: (0, g)),
                  pl.BlockSpec((S_, W), lambda g, i: (0, g))],
        out_specs=[pl.BlockSpec((TQ, HP * V_HEAD), lambda g, i: (i, g)),
                   pl.BlockSpec((HP, TQ, 1), lambda g, i: (g, i, 0))],
        out_shape=[jax.ShapeDtypeStruct((S_, N_HEADS * V_HEAD), BF16), jax.ShapeDtypeStruct((N_HEADS, S_, 1), F32)],
        compiler_params=_cparams("parallel", "parallel"), name=name,
    )(q, k, vx)


def attn_delta(o, do, name, tr=512):
    S_ = o.shape[0]
    tr = _pick(S_, tr, 8)

    def body(o_ref, do_ref, d_ref):
        d_ref[...] = jnp.sum(o_ref[...].astype(F32) * do_ref[...].astype(F32), axis=-1, keepdims=True)

    blk = pl.BlockSpec((tr, V_HEAD), lambda i, h: (i, h))
    return pl.pallas_call(
        body, grid=(S_ // tr, N_HEADS), in_specs=[blk, blk],
        out_specs=pl.BlockSpec((None, tr, 1), lambda i, h: (h, i, 0)),
        out_shape=jax.ShapeDtypeStruct((N_HEADS, S_, 1), F32),
        compiler_params=_cparams("parallel", "parallel"), name=name,
    )(o, do)


def attn_bwd(q, k, knv, do, lse_row, delta_row, name):
    S_ = q.shape[0]
    T = _pick(S_, ATT_BLOCK, LANES)
    nb = S_ // T

    def body(q_ref, do_ref, lse_ref, dl_ref, k_ref, v_ref, dq_ref, dk_ref, dv_ref):
        j = pl.program_id(1)

        @pl.when(j == 0)
        def _():
            dq_ref[...] = jnp.zeros_like(dq_ref)

        kv = k_ref[...]
        vv = v_ref[...]

        def step(i, carry, masked):
            dk, dv = carry
            start = pl.multiple_of(i * T, T)
            qv = q_ref[pl.ds(start, T), :]
            dov = do_ref[pl.ds(start, T), :]
            st = lax.dot_general(kv, qv, _DIMS["nt"], preferred_element_type=F32)
            pt = jnp.exp(st - lse_ref[:, pl.ds(start, T)])
            if masked:
                keyi = lax.broadcasted_iota(jnp.int32, (T, T), 0)
                qryi = lax.broadcasted_iota(jnp.int32, (T, T), 1)
                pt = jnp.where(keyi <= qryi, pt, 0.0)
            dpt = lax.dot_general(vv, dov, _DIMS["nt"], preferred_element_type=F32)
            dst = (pt * (dpt - dl_ref[:, pl.ds(start, T)])).astype(BF16)
            dv = dv + lax.dot_general(pt.astype(BF16), dov, _DIMS["nn"], preferred_element_type=F32)
            dk = dk + lax.dot_general(dst, qv, _DIMS["nn"], preferred_element_type=F32)
            dq_ref[pl.ds(start, T), :] += lax.dot_general(dst, kv, _DIMS["tn"], preferred_element_type=F32)
            return dk, dv

        carry = step(j, (jnp.zeros((T, HEAD_PAD), F32), jnp.zeros((T, V_HEAD), F32)), True)
        dk, dv = lax.fori_loop(j + 1, nb, functools.partial(step, masked=False), carry)
        dk_ref[...] = dk
        dv_ref[...] = dv

    return pl.pallas_call(
        body, grid=(N_HEADS, nb),
        in_specs=[pl.BlockSpec((S_, HEAD_PAD), lambda h, j: (0, h)),
                  pl.BlockSpec((S_, V_HEAD), lambda h, j: (0, h)),
                  pl.BlockSpec((None, 1, S_), lambda h, j: (h, 0, 0)),
                  pl.BlockSpec((None, 1, S_), lambda h, j: (h, 0, 0)),
                  pl.BlockSpec((T, HEAD_PAD), lambda h, j: (j, h)),
                  pl.BlockSpec((T, V_HEAD), lambda h, j: (j, N_HEADS + h))],
        out_specs=[pl.BlockSpec((S_, HEAD_PAD), lambda h, j: (0, h)),
                   pl.BlockSpec((T, HEAD_PAD), lambda h, j: (j, h)),
                   pl.BlockSpec((T, V_HEAD), lambda h, j: (j, h))],
        out_shape=[jax.ShapeDtypeStruct((S_, N_HEADS * HEAD_PAD), F32),
                   jax.ShapeDtypeStruct((S_, N_HEADS * HEAD_PAD), F32),
                   jax.ShapeDtypeStruct((S_, N_HEADS * V_HEAD), F32)],
        compiler_params=_cparams("parallel", "arbitrary"), name=name,
    )(q, do, lse_row, delta_row, k, knv)


def mods_fwd(c_all, mod_w, mod_b, name, tn=512):
    L, Dn, E = mod_w.shape
    R = c_all.shape[0]
    tn = _pick(E, tn, LANES)

    def body(c_ref, w_ref, b_ref, o_ref):
        cv = c_ref[...]
        sc = (cv / (1.0 + jnp.exp(-cv))).astype(BF16)
        o_ref[...] = lax.dot_general(sc, w_ref[...].astype(BF16), _DIMS["nn"], preferred_element_type=F32) + b_ref[...]

    return pl.pallas_call(
        body, grid=(L, E // tn),
        in_specs=[pl.BlockSpec((R, Dn), lambda l, j: (0, 0)), pl.BlockSpec((None, Dn, tn), lambda l, j: (l, 0, j)),
                  pl.BlockSpec((None, 1, tn), lambda l, j: (l, 0, j))],
        out_specs=pl.BlockSpec((None, R, tn), lambda l, j: (l, 0, j)),
        out_shape=jax.ShapeDtypeStruct((L, R, E), F32),
        compiler_params=_cparams("parallel", "parallel"), name=name,
    )(c_all, mod_w, mod_b.reshape(L, 1, E))


def _adam_math(w, g, m, v):
    m = ADAM_B1 * m + (1.0 - ADAM_B1) * g
    v = ADAM_B2 * v + (1.0 - ADAM_B2) * (g * g)
    m_hat = m / (1.0 - ADAM_B1 ** ADAM_STEP)
    v_hat = v / (1.0 - ADAM_B2 ** ADAM_STEP)
    delta = -ADAM_LR * (m_hat / (jnp.sqrt(v_hat) + ADAM_EPS) + ADAM_WD * w)
    return delta, m, v


def _as2d(a):
    return a.reshape(-1, a.shape[-1]) if a.ndim != 2 else a


def adamw(w, g, m, v, name):
    shape = w.shape
    w2, g2, m2, v2 = _as2d(w), _as2d(g), _as2d(m), _as2d(v)
    R, C = w2.shape
    tr = _pick(R, max(8, (1 << 18) // C // 8 * 8), 8)

    def body(w_ref, g_ref, m_ref, v_ref, d_ref, mo_ref, vo_ref):
        d, mn, vn = _adam_math(w_ref[...], g_ref[...], m_ref[...], v_ref[...])
        d_ref[...] = d
        mo_ref[...] = mn
        vo_ref[...] = vn

    blk = pl.BlockSpec((tr, C), lambda i: (i, 0))
    shp = jax.ShapeDtypeStruct((R, C), F32)
    outs = pl.pallas_call(
        body, grid=(R // tr,), in_specs=[blk] * 4, out_specs=[blk] * 3, out_shape=[shp] * 3,
        compiler_params=_cparams("parallel"), name=name,
    )(w2, g2, m2, v2)
    return tuple(o.reshape(shape) for o in outs)


def adamw_sum(parts, w, m, v, name):
    P, R, C = parts.shape

    def body(p_ref, w_ref, m_ref, v_ref, g_ref, d_ref, mo_ref, vo_ref):
        g = p_ref[0]
        for k in range(1, P):
            g = g + p_ref[k]
        d, mn, vn = _adam_math(w_ref[...], g, m_ref[...], v_ref[...])
        g_ref[...] = g
        d_ref[...] = d
        mo_ref[...] = mn
        vo_ref[...] = vn

    shp = jax.ShapeDtypeStruct((R, C), F32)
    return pl.pallas_call(body, out_shape=[shp] * 4, compiler_params=_cparams(), name=name)(parts, w, m, v)


def adamw_modw(c_col, dm, w, m, v, name, tr=256, tn=512):
    L, Dn, E = w.shape
    B = c_col.shape[0]
    tr = _pick(Dn, tr, 8)
    tn = _pick(E, tn, LANES)

    def body(c_ref, dm_ref, w_ref, m_ref, v_ref, g_ref, d_ref, mo_ref, vo_ref):
        g = jnp.zeros((tr, tn), F32)
        for b in range(B):
            cv = c_ref[b]
            g = g + (cv / (1.0 + jnp.exp(-cv))) * dm_ref[b:b + 1, :]
        d, mn, vn = _adam_math(w_ref[...], g, m_ref[...], v_ref[...])
        g_ref[...] = g
        d_ref[...] = d
        mo_ref[...] = mn
        vo_ref[...] = vn

    blk = pl.BlockSpec((None, tr, tn), lambda l, i, j: (l, i, j))
    shp = jax.ShapeDtypeStruct((L, Dn, E), F32)
    return pl.pallas_call(
        body, grid=(L, Dn // tr, E // tn),
        in_specs=[pl.BlockSpec((B, tr, 1), lambda l, i, j: (0, i, 0)),
                  pl.BlockSpec((None, B, tn), lambda l, i, j: (l, 0, j)), blk, blk, blk],
        out_specs=[blk] * 4, out_shape=[shp] * 4,
        compiler_params=_cparams("parallel", "parallel", "parallel"), name=name,
    )(c_col, dm, w, m, v)


def add_round(a, b, name, tr=512):
    R, C = a.shape
    tr = _pick(R, tr, 16)

    def body(a_ref, b_ref, o_ref):
        o_ref[...] = (a_ref[...] + b_ref[...].astype(F32)).astype(BF16)

    blk = pl.BlockSpec((tr, C), lambda i: (i, 0))
    return pl.pallas_call(
        body, grid=(R // tr,), in_specs=[blk, blk], out_specs=blk, out_shape=jax.ShapeDtypeStruct((R, C), BF16),
        compiler_params=_cparams("parallel"), name=name,
    )(a, b)


def sum_parts(parts, name, tr=512):
    P, R, C = parts.shape
    tr = _pick(R, tr, 16)

    def body(p_ref, o_ref):
        s = p_ref[0].astype(F32)
        for k in range(1, P):
            s = s + p_ref[k].astype(F32)
        o_ref[...] = s

    return pl.pallas_call(
        body, grid=(R // tr,), in_specs=[pl.BlockSpec((P, tr, C), lambda i: (0, i, 0))],
        out_specs=pl.BlockSpec((tr, C), lambda i: (i, 0)), out_shape=jax.ShapeDtypeStruct((R, C), F32),
        compiler_params=_cparams("parallel"), name=name,
    )(parts)


_ANY = pl.BlockSpec(memory_space=pl.ANY)


def _place():
    return lax.axis_index("x"), lax.axis_index("y"), lax.axis_index("c")


def _flip(v, bit):
    return 1 - v if bit else v


def chip_gather(buf, name):
    def body(in_ref, out_ref, send_sems, recv_sems):
        x, y, c = _place()
        me = 2 * x + y
        sends = []
        for k in range(1, N_CHIPS):
            px, py = _flip(x, k >> 1), _flip(y, k & 1)
            cp = pltpu.make_async_remote_copy(src_ref=in_ref, dst_ref=out_ref.at[me], send_sem=send_sems.at[k - 1],
                                              recv_sem=recv_sems.at[k - 1], device_id=(px, py, c), device_id_type=MESH)
            cp.start()
            sends.append(cp)
        for k in range(1, N_CHIPS):
            px, py = _flip(x, k >> 1), _flip(y, k & 1)
            pltpu.make_async_remote_copy(src_ref=in_ref, dst_ref=out_ref.at[2 * px + py], send_sem=send_sems.at[k - 1],
                                         recv_sem=recv_sems.at[k - 1], device_id=(px, py, c),
                                         device_id_type=MESH).wait_recv()
        for cp in sends:
            cp.wait_send()

    out = pl.pallas_call(
        body, in_specs=[_ANY], out_specs=_ANY,
        out_shape=jax.ShapeDtypeStruct((N_CHIPS,) + buf.shape, buf.dtype),
        scratch_shapes=[pltpu.SemaphoreType.DMA((N_CHIPS - 1,)), pltpu.SemaphoreType.DMA((N_CHIPS - 1,))],
        name=name,
    )(buf)
    return lax.dynamic_update_index_in_dim(out, buf, 2 * lax.axis_index("x") + lax.axis_index("y"), 0)


def chip_all_to_all(buf, name):
    def body(in_ref, out_ref, send_sems, recv_sems):
        x, y, c = _place()
        me = 2 * x + y
        sends = []
        for k in range(1, N_CHIPS):
            px, py = _flip(x, k >> 1), _flip(y, k & 1)
            cp = pltpu.make_async_remote_copy(src_ref=in_ref.at[2 * px + py], dst_ref=out_ref.at[me],
                                              send_sem=send_sems.at[k - 1], recv_sem=recv_sems.at[k - 1],
                                              device_id=(px, py, c), device_id_type=MESH)
            cp.start()
            sends.append(cp)
        for k in range(1, N_CHIPS):
            px, py = _flip(x, k >> 1), _flip(y, k & 1)
            pltpu.make_async_remote_copy(src_ref=in_ref.at[me], dst_ref=out_ref.at[2 * px + py],
                                         send_sem=send_sems.at[k - 1], recv_sem=recv_sems.at[k - 1],
                                         device_id=(px, py, c), device_id_type=MESH).wait_recv()
        for cp in sends:
            cp.wait_send()

    out = pl.pallas_call(
        body, in_specs=[_ANY], out_specs=_ANY, out_shape=jax.ShapeDtypeStruct(buf.shape, buf.dtype),
        scratch_shapes=[pltpu.SemaphoreType.DMA((N_CHIPS - 1,)), pltpu.SemaphoreType.DMA((N_CHIPS - 1,))],
        name=name,
    )(buf)
    me = 2 * lax.axis_index("x") + lax.axis_index("y")
    return lax.dynamic_update_index_in_dim(out, _index(buf, me), me, 0)


def core_gather(buf, name):
    def body(in_ref, out_ref, send_sem, recv_sem):
        x, y, c = _place()
        cp = pltpu.make_async_remote_copy(src_ref=in_ref, dst_ref=out_ref.at[c], send_sem=send_sem, recv_sem=recv_sem,
                                          device_id=(x, y, 1 - c), device_id_type=MESH)
        cp.start()
        pltpu.make_async_remote_copy(src_ref=in_ref, dst_ref=out_ref.at[1 - c], send_sem=send_sem, recv_sem=recv_sem,
                                     device_id=(x, y, 1 - c), device_id_type=MESH).wait_recv()
        cp.wait_send()

    out = pl.pallas_call(
        body, in_specs=[_ANY], out_specs=_ANY, out_shape=jax.ShapeDtypeStruct((2,) + buf.shape, buf.dtype),
        scratch_shapes=[pltpu.SemaphoreType.DMA, pltpu.SemaphoreType.DMA],
        name=name,
    )(buf)
    return lax.dynamic_update_index_in_dim(out, buf, lax.axis_index("c"), 0)


def core_swap(buf, name):
    def body(in_ref, out_ref, send_sem, recv_sem):
        x, y, c = _place()
        cp = pltpu.make_async_remote_copy(src_ref=in_ref, dst_ref=out_ref, send_sem=send_sem, recv_sem=recv_sem,
                                          device_id=(x, y, 1 - c), device_id_type=MESH)
        cp.start()
        cp.wait()

    return pl.pallas_call(
        body, in_specs=[_ANY], out_specs=_ANY, out_shape=jax.ShapeDtypeStruct(buf.shape, buf.dtype),
        scratch_shapes=[pltpu.SemaphoreType.DMA, pltpu.SemaphoreType.DMA],
        name=name,
    )(buf)


def device_gather(buf, name):
    def body(in_ref, out_ref, send_sems, recv_sems, local_sem):
        x, y, c = _place()
        me = 4 * x + 2 * y + c
        mine = pltpu.make_async_copy(in_ref, out_ref.at[me], local_sem)
        mine.start()
        sends = []
        for k in range(1, N_DEV):
            peer = (_flip(x, (k >> 2) & 1), _flip(y, (k >> 1) & 1), _flip(c, k & 1))
            cp = pltpu.make_async_remote_copy(src_ref=in_ref, dst_ref=out_ref.at[me], send_sem=send_sems.at[k - 1],
                                              recv_sem=recv_sems.at[k - 1], device_id=peer, device_id_type=MESH)
            cp.start()
            sends.append(cp)
        for k in range(1, N_DEV):
            peer = (_flip(x, (k >> 2) & 1), _flip(y, (k >> 1) & 1), _flip(c, k & 1))
            pltpu.make_async_remote_copy(src_ref=in_ref, dst_ref=out_ref.at[4 * peer[0] + 2 * peer[1] + peer[2]],
                                         send_sem=send_sems.at[k - 1], recv_sem=recv_sems.at[k - 1], device_id=peer,
                                         device_id_type=MESH).wait_recv()
        for cp in sends:
            cp.wait_send()
        mine.wait()

    return pl.pallas_call(
        body, in_specs=[_ANY], out_specs=_ANY, out_shape=jax.ShapeDtypeStruct((N_DEV,) + buf.shape, buf.dtype),
        scratch_shapes=[pltpu.SemaphoreType.DMA((N_DEV - 1,)), pltpu.SemaphoreType.DMA((N_DEV - 1,)),
                        pltpu.SemaphoreType.DMA],
        name=name,
    )(buf)


WEIGHT_ORDER = ["mod_w", "mod_b", "norm1_g", "norm2_g", "pool_w", "pool_b", "pool_scale", "kv_in_g", "w_dkv",
                "ckv_norm_g", "w_uk", "w_uv", "w_dq", "q_norm_g", "w_uq", "w_o", "w_up", "conv_w", "conv_b", "w_down",
                "final_g"]
EXCHANGED = {"w_up": 2, "w_down": 1, "w_o": 1, "w_uq": 2, "w_dq": 1, "pool_w": 2, "w_dkv": 0, "w_uk": 1, "w_uv": 1,
             "conv_w": 2, "pool_b": 1, "pool_scale": 1}
KEPT_F32 = ("conv_w", "pool_b", "pool_scale")
REPLICATED = ["mod_b", "norm1_g", "norm2_g", "kv_in_g", "ckv_norm_g", "q_norm_g", "conv_b", "final_g"]
PACK_ALIGN = 2 * 16 * PACK_COLS


def _padded(n, align):
    return -(-n // align) * align


def _flat_pad(parts, total):
    flat = jnp.concatenate(parts, axis=-1)
    pad = total - flat.shape[-1]
    if pad:
        flat = jnp.concatenate([flat, jnp.zeros(flat.shape[:-1] + (pad,), flat.dtype)], axis=-1)
    return flat


def _split_shards(full, axis):
    shp = full.shape
    t = full.reshape(shp[:axis] + (N_CHIPS, shp[axis] // N_CHIPS) + shp[axis + 1:])
    return jnp.moveaxis(t, axis, 0).reshape(N_CHIPS, -1)


def _join_shards(rows, shard_shape, axis):
    t = jnp.moveaxis(rows.reshape((N_CHIPS,) + tuple(shard_shape)), 0, axis)
    return t.reshape(tuple(shard_shape[:axis]) + (N_CHIPS * shard_shape[axis],) + tuple(shard_shape[axis + 1:]))


def _index(a, i, axis=0):
    return lax.dynamic_index_in_dim(a, i, axis, keepdims=False)


def kernel(x, c, positions, mod_w, mod_b, norm1_g, norm2_g, pool_w, pool_b, pool_scale, kv_in_g, w_dkv, ckv_norm_g, w_uk, w_uv, w_dq, q_norm_g, w_uq, w_o, w_up, conv_w, conv_b, w_down, final_g, loss_target, m_mod_w, m_mod_b, m_norm1_g, m_norm2_g, m_pool_w, m_pool_b, m_pool_scale, m_kv_in_g, m_w_dkv, m_ckv_norm_g, m_w_uk, m_w_uv, m_w_dq, m_q_norm_g, m_w_uq, m_w_o, m_w_up, m_conv_w, m_conv_b, m_w_down, m_final_g, v_mod_w, v_mod_b, v_norm1_g, v_norm2_g, v_pool_w, v_pool_b, v_pool_scale, v_kv_in_g, v_w_dkv, v_ckv_norm_g, v_w_uk, v_w_uv, v_w_dq, v_q_norm_g, v_w_uq, v_w_o, v_w_up, v_conv_w, v_conv_b, v_w_down, v_final_g):
    given = dict(locals())
    W = {n: given[n] for n in WEIGHT_ORDER}
    M1 = {n: given["m_" + n] for n in WEIGHT_ORDER}
    V2 = {n: given["v_" + n] for n in WEIGHT_ORDER}
    xi, yi, ci = lax.axis_index("x"), lax.axis_index("y"), lax.axis_index("c")
    chip = 2 * xi + yi
    dev = 4 * xi + 2 * yi + ci
    x0 = x[0]
    S_, D = x0.shape
    Fh = conv_b.shape[1]
    E = mod_b.shape[1]
    Es = E // N_CHIPS
    zD = jnp.zeros((D,), F32)

    sizes = {n: math.prod(W[n].shape) * (2 if n in KEPT_F32 else 1) for n in EXCHANGED}
    T = _padded(sum(sizes.values()), PACK_ALIGN)
    R = T // PACK_COLS
    own = _flat_pad([lax.bitcast_convert_type(W[n], BF16).reshape(-1) if n in KEPT_F32 else W[n].astype(BF16).reshape(-1)
                     for n in EXCHANGED], T)
    own_half = _index(own.reshape(2, R // 2, PACK_COLS), ci)
    halves = core_gather(chip_gather(own_half, "gather_w_chips"), "gather_w_cores")
    rows = jnp.swapaxes(halves, 0, 1).reshape(N_CHIPS, T)
    full = {}
    off = 0
    for n, axis in EXCHANGED.items():
        seg = rows[:, off:off + sizes[n]]
        off += sizes[n]
        if n in KEPT_F32:
            seg = lax.bitcast_convert_type(seg.reshape(N_CHIPS, -1, 2), F32)
        full[n] = _join_shards(seg, W[n].shape, axis)

    n_ffn_tiles = Fh // GLU_TILE
    w_up_p = full["w_up"].reshape(DEPTH, D, 2, n_ffn_tiles, GLU_TILE).swapaxes(2, 3).reshape(DEPTH, D, 2 * Fh)
    n_mla = DEPTH - N_A
    q_rank = full["w_uq"].shape[1]
    wq = full["w_uq"].reshape(n_mla, q_rank, N_HEADS, QK_HEAD)
    w_uq_ext = jnp.concatenate([wq, jnp.zeros((n_mla, q_rank, N_HEADS, HEAD_PAD - QK_HEAD), BF16)],
                               axis=3).reshape(n_mla, q_rank, N_HEADS * HEAD_PAD)
    kv_w = KV_RANK + QK_ROPE
    w_dkv_ext = jnp.concatenate([full["w_dkv"], jnp.zeros((D, KV_RANK + LANES - kv_w), BF16)], axis=1)
    w_ukv = jnp.concatenate([full["w_uk"], full["w_uv"]], axis=1)

    c_all = device_gather(c, "gather_c").reshape(N_DEV, D)
    c_pad = jnp.concatenate([c_all, jnp.zeros((16 - N_DEV, D), F32)], axis=0)
    mod_b_mine = lax.dynamic_slice_in_dim(mod_b, chip * Es, Es, axis=1)
    mods_part = mods_fwd(c_pad, mod_w, mod_b_mine, "mods_fwd")
    mods_all = chip_gather(mods_part, "gather_mods")
    mods = jnp.swapaxes(_index(mods_all, dev, axis=2), 0, 1).reshape(DEPTH, E)
    mod = [[mods[l, k * D:(k + 1) * D] for k in range(6)] for l in range(DEPTH)]

    half = QK_ROPE // 2
    inv = 1.0 / (ROPE_THETA ** (jnp.arange(0, QK_ROPE, 2, dtype=F32) / QK_ROPE))
    inv_row = jnp.concatenate([inv, inv, jnp.zeros((LANES - 2 * half,), F32)]).reshape(1, LANES)
    tabs = rope_tables(positions[0].astype(F32).reshape(S_, 1), inv_row, "rope_tables")
    att_scale = QK_HEAD ** -0.5

    saved = []
    xcur = x0
    kv_saved = None
    K = VX = knv = None
    for l in range(DEPTH):
        sh1, sc1, g1, sh2, sc2, g2 = mod[l]
        st = {"xin": xcur}
        if l < N_A:
            h1 = norm_fwd(xcur, norm1_g[l], sc1, sh1, F32, f"norm1_fwd{l}")
            st["pooled"] = _pool_call(h1, BF16, f"pool_fwd{l}", False)
            st["cs"] = g1 * full["pool_scale"][l]
            st["ypre"], xmid = gmm(st["pooled"], full["pool_w"][l], "nn", F32, f"pool_mm{l}", bias=full["pool_b"][l],
                                   res=xcur, colscale=st["cs"])
        else:
            j = l - N_A
            st["h1"] = norm_fwd(xcur, norm1_g[l], sc1, sh1, BF16, f"norm1_fwd{l}")
            st["ql"] = mm(st["h1"], full["w_dq"], "nn", F32, f"dq_mm{l}", layer=j)
            st["cq"] = norm_fwd(st["ql"], q_norm_g[j], jnp.zeros_like(q_norm_g[j]), jnp.zeros_like(q_norm_g[j]), BF16,
                                f"qnorm_fwd{l}")
            qe = mm(st["cq"], w_uq_ext, "nn", F32, f"uq_mm{l}", layer=j)
            st["Q"] = q_prep(qe, tabs, att_scale, False, f"q_prep{l}")
            st["o"], lse = attn_fwd(st["Q"], K, VX, f"attn_fwd{l}")
            st["lse"] = lse.reshape(N_HEADS, 1, S_)
            st["y"], xmid = mm(st["o"], full["w_o"], "nn", F32, f"wo_mm{l}", res=xcur, colscale=g1, layer=j)
        st["xmid"] = xmid
        st["h2"] = norm_fwd(xmid, norm2_g[l], sc2, sh2, BF16, f"norm2_fwd{l}")
        st["u"] = mm(st["h2"], w_up_p, "nn", F32, f"up_mm{l}", layer=l)
        st["z"] = glu_fwd(st["u"], full["conv_w"][l], conv_b[l], f"glu_fwd{l}")
        st["f"], xcur = mm(st["z"], full["w_down"], "nn", F32, f"down_mm{l}", tk=1408, res=xmid, colscale=g2, layer=l)
        saved.append(st)
        if l == N_A - 1:
            xn = norm_fwd(xcur, kv_in_g, zD, zD, BF16, "kvin_fwd")
            kv_ext = mm(xn, w_dkv_ext, "nn", F32, "dkv_mm")
            lat = kv_ext[:, :KV_RANK]
            zk = jnp.zeros((KV_RANK,), F32)
            ckv = norm_fwd(lat, ckv_norm_g, zk, zk, BF16, "ckv_fwd")
            knv = mm(ckv, w_ukv, "nn", BF16, "ukv_mm")
            K, VX = k_prep(knv, kv_ext, tabs, "k_prep")
            kv_saved = {"x": xcur, "xn": xn, "lat": lat, "ckv": ckv}

    dx, d_final_g, loss_part = loss_head(xcur, final_g, loss_target[0], "loss_head")
    loss = lax.psum(loss_part[0, 0], ("x", "y", "c"))

    G = {}
    dmods = [None] * DEPTH
    d_norm1 = [None] * DEPTH
    d_norm2 = [None] * DEPTH
    d_conv_b = [None] * DEPTH
    d_qnorm = [None] * n_mla
    dkv_acc = []
    for l in reversed(range(DEPTH)):
        sh1, sc1, g1, sh2, sc2, g2 = mod[l]
        st = saved[l]
        df, a2, _ = gate_bwd(dx, st["f"], g2, f"gate2_bwd{l}")
        dz = mm(df, full["w_down"], "nt", BF16, f"down_dx{l}", layer=l)
        G[("w_down", l)] = mm(st["z"], df, "tn", F32, f"down_dw{l}")
        du, dcw, dcb = glu_bwd(st["u"], dz, full["conv_w"][l], conv_b[l], f"glu_bwd{l}")
        G[("conv_w", l)] = dcw
        d_conv_b[l] = dcb[0]
        dh2 = mm(du, w_up_p, "nt", BF16, f"up_dx{l}", tk=1408, layer=l)
        G[("w_up", l)] = mm(st["h2"], du, "tn", F32, f"up_dw{l}").reshape(D, n_ffn_tiles, 2, GLU_TILE).swapaxes(
            1, 2).reshape(D, 2 * Fh)
        dxmid, s1, s2 = norm_bwd(st["xmid"], norm2_g[l], sc2, dh2, dx, f"norm2_bwd{l}")
        dsh2, dsc2, d_norm2[l] = s1[0], s2[0] * norm2_g[l], s2[0] * (1.0 + sc2)
        if l < N_A:
            dyp, a1, csum = gate_bwd(dxmid, st["ypre"], st["cs"], f"gate1_bwd{l}")
            dg1 = full["pool_scale"][l] * a1[0]
            G[("pool_scale", l)] = g1 * a1[0]
            G[("pool_b", l)] = st["cs"] * csum[0]
            dpooled = gmm(dyp, full["pool_w"][l], "nt", F32, f"pool_dx{l}")
            G[("pool_w", l)] = gmm(st["pooled"], dyp, "tn", F32, f"pool_dw{l}")
            dh1 = _pool_call(dpooled, F32, f"pool_bwd{l}", True)
        else:
            j = l - N_A
            dy, a1, _ = gate_bwd(dxmid, st["y"], g1, f"gate1_bwd{l}")
            dg1 = a1[0]
            do = mm(dy, full["w_o"], "nt", BF16, f"wo_dx{l}", layer=j)
            G[("w_o", j)] = mm(st["o"], dy, "tn", F32, f"wo_dw{l}")
            delta = attn_delta(st["o"], do, f"attn_delta{l}").reshape(N_HEADS, 1, S_)
            dQ, dK, dV = attn_bwd(st["Q"], K, knv, do, st["lse"], delta, f"attn_bwd{l}")
            dkv_acc.append((dK, dV))
            dqe = q_prep(dQ, tabs, att_scale, True, f"q_prep_bwd{l}")
            dcq = mm(dqe, w_uq_ext, "nt", F32, f"uq_dx{l}", layer=j)
            G[("w_uq", j)] = mm(st["cq"], dqe, "tn", F32, f"uq_dw{l}").reshape(q_rank, N_HEADS, HEAD_PAD)[
                :, :, :QK_HEAD].reshape(q_rank, N_HEADS * QK_HEAD)
            zq = jnp.zeros_like(q_norm_g[j])
            dql, _, s2q = norm_bwd(st["ql"], q_norm_g[j], zq, dcq, None, f"qnorm_bwd{l}")
            d_qnorm[j] = s2q[0]
            dh1 = mm(dql, full["w_dq"], "nt", BF16, f"dq_dx{l}", layer=j)
            G[("w_dq", j)] = mm(st["h1"], dql, "tn", F32, f"dq_dw{l}")
        dx, s1, s2 = norm_bwd(st["xin"], norm1_g[l], sc1, dh1, dxmid, f"norm1_bwd{l}")
        dsh1, dsc1, d_norm1[l] = s1[0], s2[0] * norm1_g[l], s2[0] * (1.0 + sc1)
        dmods[l] = jnp.concatenate([dsh1, dsc1, dg1, dsh2, dsc2, a2[0]])
        if l == N_A:
            (dk_a, dv_a), (dk_b, dv_b) = dkv_acc
            dknv, d_tk = k_prep_bwd(dk_a, dk_b, dv_a, dv_b, tabs, "k_prep_bwd")
            dckv = mm(dknv, w_ukv, "nt", F32, "ukv_dx")
            d_ukv = mm(kv_saved["ckv"], dknv, "tn", F32, "ukv_dw")
            G[("w_uk", 0)], G[("w_uv", 0)] = d_ukv[:, :N_HEADS * QK_NOPE], d_ukv[:, N_HEADS * QK_NOPE:]
            zk = jnp.zeros((KV_RANK,), F32)
            dlat, _, s2c = norm_bwd(kv_saved["lat"], ckv_norm_g, zk, dckv, None, "ckv_bwd")
            d_ckv_g = s2c[0]
            dkv_ext = jnp.concatenate([dlat, d_tk], axis=1)
            dxn = mm(dkv_ext, w_dkv_ext, "nt", BF16, "dkv_dx")
            G[("w_dkv", 0)] = mm(kv_saved["xn"], dkv_ext, "tn", F32, "dkv_dw")[:, :kv_w]
            dx, _, s2k = norm_bwd(kv_saved["x"], kv_in_g, zD, dxn, dx, "kvin_bwd")
            d_kvin_g = s2k[0]

    def stacked(n):
        k = W[n].shape[0] if W[n].ndim > 2 or n in ("pool_b", "pool_scale") else None
        return G[(n, 0)] if k is None else jnp.stack([G[(n, i)] for i in range(k)])

    gsizes = {n: math.prod(W[n].shape) for n in EXCHANGED}
    Tg = _padded(sum(gsizes.values()), PACK_ALIGN)
    Rg = Tg // PACK_COLS
    gflat = _flat_pad([_split_shards(stacked(n), axis) for n, axis in EXCHANGED.items()], Tg)
    gflat = gflat.reshape(N_CHIPS, 2, Rg // 2, PACK_COLS)
    keep = _index(gflat, ci, axis=1).reshape(N_CHIPS * Rg // 2, PACK_COLS)
    give = _index(gflat, 1 - ci, axis=1).astype(BF16).reshape(N_CHIPS * Rg // 2, PACK_COLS)
    chip_sum = add_round(keep, core_swap(give, "reduce_cores"), "reduce_cores_add")
    got = chip_all_to_all(chip_sum.reshape(N_CHIPS, Rg // 2, PACK_COLS), "reduce_chips")
    red = core_gather(sum_parts(got, "reduce_chips_add"), "reduce_gather").reshape(Tg)

    grads, deltas, new_m, new_v = {}, {}, {}, {}
    off = 0
    for n in EXCHANGED:
        grads[n] = red[off:off + gsizes[n]].reshape(W[n].shape)
        off += gsizes[n]
        deltas[n], new_m[n], new_v[n] = adamw(W[n], grads[n], M1[n], V2[n], f"adamw_{n}")

    small = {"mod_b": jnp.stack(dmods), "norm1_g": jnp.stack(d_norm1), "norm2_g": jnp.stack(d_norm2),
             "kv_in_g": d_kvin_g, "ckv_norm_g": d_ckv_g, "q_norm_g": jnp.stack(d_qnorm),
             "conv_b": jnp.stack(d_conv_b), "final_g": d_final_g[0]}
    ssizes = {n: math.prod(W[n].shape) for n in REPLICATED}
    Ts = _padded(sum(ssizes.values()), 8 * PACK_COLS)

    def pack_small(d):
        return _flat_pad([d[n].reshape(-1) for n in REPLICATED], Ts).reshape(Ts // PACK_COLS, PACK_COLS)

    parts = device_gather(pack_small(small), "gather_small")
    outs = adamw_sum(parts, pack_small(W), pack_small(M1), pack_small(V2), "adamw_small")
    off = 0
    for n in REPLICATED:
        for dst, o in zip((grads, deltas, new_m, new_v), outs):
            dst[n] = o.reshape(-1)[off:off + ssizes[n]].reshape(W[n].shape)
        off += ssizes[n]

    dm_all = parts.reshape(N_DEV, -1)[:, :DEPTH * E].reshape(N_DEV, DEPTH, E)
    dm_mine = jnp.swapaxes(lax.dynamic_slice_in_dim(dm_all, chip * Es, Es, axis=2), 0, 1)
    grads["mod_w"], deltas["mod_w"], new_m["mod_w"], new_v["mod_w"] = adamw_modw(
        c_all.reshape(N_DEV, D, 1), dm_mine, mod_w, m_mod_w, v_mod_w, "adamw_mod_w")

    return (loss, dx.reshape(x.shape), *[grads[n] for n in WEIGHT_ORDER], *[deltas[n] for n in WEIGHT_ORDER],
            *[new_m[n] for n in WEIGHT_ORDER], *[new_v[n] for n in WEIGHT_ORDER])
```

```python
import functools
import math

import jax
import jax.numpy as jnp
from jax import lax
from jax.experimental import pallas as pl
from jax.experimental.pallas import tpu as pltpu

F32 = jnp.float32
BF16 = jnp.bfloat16
MESH = pl.DeviceIdType.MESH

DEPTH = 4
N_A = 2
POOL_WINDOWS = (2, 4, 8, 16)
N_GROUPS = 4
N_HEADS = 8
QK_NOPE = 128
QK_ROPE = 64
V_HEAD = 128
QK_HEAD = QK_NOPE + QK_ROPE
HEAD_PAD = 256
KV_RANK = 256
ROPE_THETA = 10000.0
EPS = 1e-6
ADAM_LR = 0.001
ADAM_B1 = 0.9
ADAM_B2 = 0.999
ADAM_EPS = 1e-08
ADAM_WD = 0.01
ADAM_STEP = 10

N_CHIPS = 4
N_DEV = 8
LANES = 128
PACK_COLS = 1024
VMEM_LIMIT = 56 * 1024 * 1024
GLU_TILE = 256
ATT_BWD_K_BLOCK = 256
ATT_BWD_Q_BLOCK = 512
ATT_Q_BLOCK = 256
ATT_K_BLOCK = 512
ATT_HEADS_PER_STEP = 2


def _cparams(*sem):
    return pltpu.CompilerParams(dimension_semantics=sem if sem else None, vmem_limit_bytes=VMEM_LIMIT)


def _pick(n, target, mult):
    best = None
    d = mult
    while d <= min(n, target):
        if n % d == 0:
            best = d
        d += mult
    return n if best is None else best


def _row(v):
    return v.reshape(1, -1).astype(F32)


_DIMS = {"nn": (((1,), (0,)), ((), ())), "nt": (((1,), (1,)), ((), ())), "tn": (((0,), (0,)), ((), ()))}


def _mm_body(mode, nk, has_bias, has_res):
    def body(*refs):
        a_ref, b_ref = refs[0], refs[1]
        pos = 2
        bias_ref = res_ref = cs_ref = None
        if has_bias:
            bias_ref = refs[pos]
            pos += 1
        if has_res:
            res_ref, cs_ref = refs[pos], refs[pos + 1]
            pos += 2
        o_ref = refs[pos]
        pos += 1
        o2_ref = None
        if has_res:
            o2_ref = refs[pos]
            pos += 1
        acc_ref = refs[pos]
        k = pl.program_id(2)

        @pl.when(k == 0)
        def _():
            acc_ref[...] = jnp.zeros_like(acc_ref)

        acc_ref[...] += lax.dot_general(a_ref[...].astype(BF16), b_ref[...].astype(BF16), _DIMS[mode],
                                        preferred_element_type=F32)

        @pl.when(k == nk - 1)
        def _():
            y = acc_ref[...]
            if has_bias:
                y = y + bias_ref[...]
            o_ref[...] = y.astype(o_ref.dtype)
            if has_res:
                o2_ref[...] = res_ref[...] + cs_ref[...] * y

    return body


def mm(a, b, mode, out_dtype, name, *, tm=1024, tn=512, tk=1024, bias=None, res=None, colscale=None, layer=None):
    bshape = b.shape if layer is None else b.shape[1:]
    if mode == "nn":
        (M, K), N = a.shape, bshape[1]
    elif mode == "nt":
        (M, K), N = a.shape, bshape[0]
    else:
        (K, M), N = a.shape, bshape[1]
    tm = _pick(M, tm, LANES if mode == "tn" else 8)
    tn = _pick(N, tn, LANES)
    tk = _pick(K, tk, LANES) if mode != "tn" else _pick(K, tk, 8)
    nk = K // tk
    a_spec = {"nn": pl.BlockSpec((tm, tk), lambda i, j, k: (i, k)),
              "nt": pl.BlockSpec((tm, tk), lambda i, j, k: (i, k)),
              "tn": pl.BlockSpec((tk, tm), lambda i, j, k: (k, i))}[mode]
    b_blk, b_map = {"nn": ((tk, tn), lambda i, j, k: (k, j)),
                    "nt": ((tn, tk), lambda i, j, k: (j, k)),
                    "tn": ((tk, tn), lambda i, j, k: (k, j))}[mode]
    if layer is None:
        b_spec = pl.BlockSpec(b_blk, b_map)
    else:
        b_spec = pl.BlockSpec((None,) + b_blk, lambda i, j, k: (layer,) + b_map(i, j, k))
    o_spec = pl.BlockSpec((tm, tn), lambda i, j, k: (i, j))
    v_spec = pl.BlockSpec((1, tn), lambda i, j, k: (0, j))
    in_specs, args = [a_spec, b_spec], [a, b]
    if bias is not None:
        in_specs.append(v_spec)
        args.append(_row(bias))
    out_shape = [jax.ShapeDtypeStruct((M, N), out_dtype)]
    out_specs = [o_spec]
    if res is not None:
        in_specs += [o_spec, v_spec]
        args += [res, _row(colscale)]
        out_shape.append(jax.ShapeDtypeStruct((M, N), F32))
        out_specs.append(o_spec)
    outs = pl.pallas_call(
        _mm_body(mode, nk, bias is not None, res is not None),
        grid=(M // tm, N // tn, nk),
        in_specs=in_specs, out_specs=out_specs, out_shape=out_shape,
        scratch_shapes=[pltpu.VMEM((tm, tn), F32)],
        compiler_params=_cparams("parallel", "parallel", "arbitrary"),
        name=name,
    )(*args)
    return outs if res is not None else outs[0]


def gmm(a, w, mode, out_dtype, name, *, bias=None, res=None, colscale=None, tr=512):
    S_ = a.shape[0]
    G = N_GROUPS
    C = a.shape[1] // G
    tr = _pick(S_, tr, 8)
    nr = S_ // tr
    if mode == "tn":
        def body(a_ref, b_ref, o_ref, acc_ref):
            i = pl.program_id(1)

            @pl.when(i == 0)
            def _():
                acc_ref[...] = jnp.zeros_like(acc_ref)

            acc_ref[...] += lax.dot_general(a_ref[...].astype(BF16), b_ref[...].astype(BF16), _DIMS["tn"],
                                            preferred_element_type=F32)

            @pl.when(i == nr - 1)
            def _():
                o_ref[...] = acc_ref[...].astype(o_ref.dtype)

        blk = pl.BlockSpec((tr, C), lambda g, i: (i, g))
        return pl.pallas_call(
            body, grid=(G, nr), in_specs=[blk, blk],
            out_specs=pl.BlockSpec((None, C, C), lambda g, i: (g, 0, 0)),
            out_shape=jax.ShapeDtypeStruct((G, C, C), out_dtype),
            scratch_shapes=[pltpu.VMEM((C, C), F32)],
            compiler_params=_cparams("parallel", "arbitrary"), name=name,
        )(a, w)

    has_bias, has_res = bias is not None, res is not None

    def body(*refs):
        a_ref, w_ref = refs[0], refs[1]
        pos = 2
        if has_bias:
            bias_ref = refs[pos]
            pos += 1
        if has_res:
            res_ref, cs_ref = refs[pos], refs[pos + 1]
            pos += 2
        o_ref = refs[pos]
        y = lax.dot_general(a_ref[...].astype(BF16), w_ref[...].astype(BF16), _DIMS[mode],
                            preferred_element_type=F32)
        if has_bias:
            y = y + bias_ref[...]
        o_ref[...] = y.astype(o_ref.dtype)
        if has_res:
            refs[pos + 1][...] = res_ref[...] + cs_ref[...] * y

    blk = pl.BlockSpec((tr, C), lambda i, g: (i, g))
    vec = pl.BlockSpec((1, C), lambda i, g: (0, g))
    in_specs = [blk, pl.BlockSpec((None, C, C), lambda i, g: (g, 0, 0))]
    args = [a, w]
    if has_bias:
        in_specs.append(vec)
        args.append(_row(bias))
    out_shape = [jax.ShapeDtypeStruct(a.shape, out_dtype)]
    out_specs = [blk]
    if has_res:
        in_specs += [blk, vec]
        args += [res, _row(colscale)]
        out_shape.append(jax.ShapeDtypeStruct(a.shape, F32))
        out_specs.append(blk)
    outs = pl.pallas_call(
        body, grid=(nr, G), in_specs=in_specs, out_specs=out_specs, out_shape=out_shape,
        compiler_params=_cparams("parallel", "parallel"), name=name,
    )(*args)
    return outs if has_res else outs[0]


def norm_fwd(x, g, sc, sh, out_dtype, name, tr=512):
    S_, Dn = x.shape
    tr = _pick(S_, tr, 8)

    def body(x_ref, g_ref, sc_ref, sh_ref, o_ref):
        xv = x_ref[...]
        r = lax.rsqrt(jnp.mean(xv * xv, axis=-1, keepdims=True) + EPS)
        o_ref[...] = (((xv * r) * g_ref[...]) * (1.0 + sc_ref[...]) + sh_ref[...]).astype(o_ref.dtype)

    blk = pl.BlockSpec((tr, Dn), lambda i: (i, 0))
    vec = pl.BlockSpec((1, Dn), lambda i: (0, 0))
    return pl.pallas_call(
        body, grid=(S_ // tr,), in_specs=[blk, vec, vec, vec], out_specs=blk,
        out_shape=jax.ShapeDtypeStruct((S_, Dn), out_dtype),
        compiler_params=_cparams("parallel"), name=name,
    )(x, _row(g), _row(sc), _row(sh))


def norm_bwd(x, g, sc, dh, dres, name, tr=512):
    S_, Dn = x.shape
    tr = _pick(S_, tr, 8)
    has_res = dres is not None

    def body(*refs):
        x_ref, g_ref, sc_ref, dh_ref = refs[:4]
        pos = 4
        if has_res:
            dres_ref = refs[pos]
            pos += 1
        dx_ref, s1_ref, s2_ref = refs[pos:pos + 3]
        i = pl.program_id(0)

        @pl.when(i == 0)
        def _():
            s1_ref[...] = jnp.zeros_like(s1_ref)
            s2_ref[...] = jnp.zeros_like(s2_ref)

        xv = x_ref[...]
        r = lax.rsqrt(jnp.mean(xv * xv, axis=-1, keepdims=True) + EPS)
        n = xv * r
        dhv = dh_ref[...].astype(F32)
        dn = dhv * (g_ref[...] * (1.0 + sc_ref[...]))
        dx = r * (dn - n * jnp.mean(dn * n, axis=-1, keepdims=True))
        if has_res:
            dx = dx + dres_ref[...]
        dx_ref[...] = dx
        s1_ref[...] += jnp.sum(dhv, axis=0, keepdims=True)
        s2_ref[...] += jnp.sum(dhv * n, axis=0, keepdims=True)

    blk = pl.BlockSpec((tr, Dn), lambda i: (i, 0))
    vec = pl.BlockSpec((1, Dn), lambda i: (0, 0))
    in_specs, args = [blk, vec, vec, blk], [x, _row(g), _row(sc), dh]
    if has_res:
        in_specs.append(blk)
        args.append(dres)
    vshape = jax.ShapeDtypeStruct((1, Dn), F32)
    return pl.pallas_call(
        body, grid=(S_ // tr,), in_specs=in_specs, out_specs=[blk, vec, vec],
        out_shape=[jax.ShapeDtypeStruct((S_, Dn), F32), vshape, vshape],
        compiler_params=_cparams("arbitrary"), name=name,
    )(*args)


def gate_bwd(dx, y, colscale, name, tr=512):
    S_, Dn = dx.shape
    tr = _pick(S_, tr, 8)

    def body(dx_ref, y_ref, cs_ref, d_ref, a_ref, c_ref):
        i = pl.program_id(0)

        @pl.when(i == 0)
        def _():
            a_ref[...] = jnp.zeros_like(a_ref)
            c_ref[...] = jnp.zeros_like(c_ref)

        dxv = dx_ref[...]
        d_ref[...] = (dxv * cs_ref[...]).astype(d_ref.dtype)
        a_ref[...] += jnp.sum(dxv * y_ref[...].astype(F32), axis=0, keepdims=True)
        c_ref[...] += jnp.sum(dxv, axis=0, keepdims=True)

    blk = pl.BlockSpec((tr, Dn), lambda i: (i, 0))
    vec = pl.BlockSpec((1, Dn), lambda i: (0, 0))
    vshape = jax.ShapeDtypeStruct((1, Dn), F32)
    return pl.pallas_call(
        body, grid=(S_ // tr,), in_specs=[blk, blk, vec], out_specs=[blk, vec, vec],
        out_shape=[jax.ShapeDtypeStruct((S_, Dn), BF16), vshape, vshape],
        compiler_params=_cparams("arbitrary"), name=name,
    )(dx, y, _row(colscale))


def loss_head(x, g, target, name, tr=512):
    S_, Dn = x.shape
    tr = _pick(S_, tr, 8)

    def body(x_ref, g_ref, t_ref, dx_ref, dg_ref, loss_ref):
        i = pl.program_id(0)

        @pl.when(i == 0)
        def _():
            dg_ref[...] = jnp.zeros_like(dg_ref)
            loss_ref[...] = jnp.zeros_like(loss_ref)

        xv = x_ref[...]
        r = lax.rsqrt(jnp.mean(xv * xv, axis=-1, keepdims=True) + EPS)
        n = xv * r
        e = n * g_ref[...] - t_ref[...]
        loss_ref[...] += 0.5 * jnp.sum(jnp.mean(e * e, axis=-1, keepdims=True), axis=0, keepdims=True)
        dy = e * (1.0 / Dn)
        dg_ref[...] += jnp.sum(dy * n, axis=0, keepdims=True)
        dn = dy * g_ref[...]
        dx_ref[...] = r * (dn - n * jnp.mean(dn * n, axis=-1, keepdims=True))

    blk = pl.BlockSpec((tr, Dn), lambda i: (i, 0))
    vec = pl.BlockSpec((1, Dn), lambda i: (0, 0))
    one = pl.BlockSpec((1, 1), lambda i: (0, 0))
    return pl.pallas_call(
        body, grid=(S_ // tr,), in_specs=[blk, vec, blk], out_specs=[blk, vec, one],
        out_shape=[jax.ShapeDtypeStruct((S_, Dn), F32), jax.ShapeDtypeStruct((1, Dn), F32),
                   jax.ShapeDtypeStruct((1, 1), F32)],
        compiler_params=_cparams("arbitrary"), name=name,
    )(x, _row(g), target)


POOL_HALO = 16
POOL_CHUNK = 512


def _rows(ref, lo, hi, n_rows):
    parts = []
    if lo < 0:
        parts.append(jnp.zeros((-lo, ref.shape[1]), F32))
    parts.append(ref[max(lo, 0):min(hi, n_rows), :].astype(F32))
    if hi > n_rows:
        parts.append(jnp.zeros((hi - n_rows, ref.shape[1]), F32))
    return parts[0] if len(parts) == 1 else jnp.concatenate(parts, axis=0)


def _window_sum(e, w, back):
    n = e.shape[0]
    s, width = e, 1
    while width < w:
        s = s + pltpu.roll(s, width if back else n - width, 0)
        width *= 2
    return s


def _pool_call(h, out_dtype, name, backward):
    S_, Dn = h.shape
    C = Dn // N_GROUPS
    ch = _pick(S_, POOL_CHUNK, 8)

    def body(h_ref, o_ref):
        g = pl.program_id(0)
        for gi, w in enumerate(POOL_WINDOWS):
            @pl.when(g == gi)
            def _(w=w):
                for r0 in range(0, S_, ch):
                    t = (r0 + lax.broadcasted_iota(jnp.int32, (ch, C), 0)).astype(F32)
                    cnt = jnp.minimum(t + 1.0, float(w))
                    if not backward:
                        ext = _rows(h_ref, r0 - POOL_HALO, r0 + ch, S_)
                        cur = ext[POOL_HALO:]
                        mean = _window_sum(ext, w, True)[POOL_HALO:] / cnt
                        o_ref[r0:r0 + ch, :] = (mean - cur).astype(o_ref.dtype)
                    else:
                        ext = _rows(h_ref, r0, r0 + ch + POOL_HALO, S_)
                        text = (r0 + lax.broadcasted_iota(jnp.int32, (ch + POOL_HALO, C), 0)).astype(F32)
                        e = ext / jnp.minimum(text + 1.0, float(w))
                        o_ref[r0:r0 + ch, :] = (_window_sum(e, w, False)[:ch] - ext[:ch]).astype(o_ref.dtype)

    blk = pl.BlockSpec((S_, C), lambda g: (0, g))
    return pl.pallas_call(
        body, grid=(N_GROUPS,), in_specs=[blk], out_specs=blk,
        out_shape=jax.ShapeDtypeStruct((S_, Dn), out_dtype),
        compiler_params=_cparams("parallel"), name=name,
    )(h)


GLU_CHUNK = 512
GLU_HALO = 16
_SQRT_HALF = 0.7071067811865476
_INV_SQRT_2PI = 0.3989422804014327


def _gelu(a):
    return 0.5 * a * (1.0 + lax.erf(a * _SQRT_HALF))


def _gelu_grad(a):
    return 0.5 * (1.0 + lax.erf(a * _SQRT_HALF)) + a * (_INV_SQRT_2PI * jnp.exp(-0.5 * a * a))


def glu_fwd(u, conv_w, conv_b, name):
    S_, F2 = u.shape
    Fh = F2 // 2
    tf = GLU_TILE
    nt = Fh // tf
    ch = _pick(S_, GLU_CHUNK, GLU_HALO)

    def body(a_ref, v_ref, cw_ref, cb_ref, z_ref):
        cw0, cw1, cw2 = cw_ref[0:1, :], cw_ref[1:2, :], cw_ref[2:3, :]
        cb = cb_ref[...]
        for r0 in range(0, S_, ch):
            ext = _rows(a_ref, r0 - GLU_HALO, r0 + ch, S_)
            a0 = ext[GLU_HALO:]
            a1 = pltpu.roll(ext, 1, 0)[GLU_HALO:]
            a2 = pltpu.roll(ext, 2, 0)[GLU_HALO:]
            ac = a2 * cw0 + a1 * cw1 + a0 * cw2 + cb
            z_ref[r0:r0 + ch, :] = (_gelu(ac) * v_ref[r0:r0 + ch, :].astype(F32)).astype(z_ref.dtype)

    return pl.pallas_call(
        body, grid=(nt,),
        in_specs=[pl.BlockSpec((S_, tf), lambda j: (0, j)), pl.BlockSpec((S_, tf), lambda j: (0, j + nt)),
                  pl.BlockSpec((3, tf), lambda j: (0, j)), pl.BlockSpec((1, tf), lambda j: (0, j))],
        out_specs=pl.BlockSpec((S_, tf), lambda j: (0, j)),
        out_shape=jax.ShapeDtypeStruct((S_, Fh), BF16),
        compiler_params=_cparams("parallel"), name=name,
    )(u, u, conv_w, _row(conv_b))


def glu_bwd(u, dz, conv_w, conv_b, name):
    S_, F2 = u.shape
    Fh = F2 // 2
    tf = GLU_TILE
    nt = Fh // tf
    ch = _pick(S_, GLU_CHUNK, GLU_HALO)

    def body(a_ref, v_ref, dz_ref, cw_ref, cb_ref, du_ref, dcw_ref, dcb_ref, da_buf, dv_buf, sems):
        j = pl.program_id(0)
        slot = j % 2

        def writes(step, sl):
            lo = pl.multiple_of(step * tf, tf)
            return (pltpu.make_async_copy(da_buf.at[sl], du_ref.at[:, pl.ds(lo, tf)], sems.at[sl, 0]),
                    pltpu.make_async_copy(dv_buf.at[sl], du_ref.at[:, pl.ds(Fh + lo, tf)], sems.at[sl, 1]))

        @pl.when(j >= 2)
        def _():
            for cp in writes(j - 2, slot):
                cp.wait()

        cw0, cw1, cw2 = cw_ref[0:1, :], cw_ref[1:2, :], cw_ref[2:3, :]
        cb = cb_ref[...]
        acc = [jnp.zeros((1, tf), F32) for _ in range(4)]
        n = ch + GLU_HALO
        for r0 in range(0, S_, ch):
            ext = _rows(a_ref, r0 - GLU_HALO, r0 + n, S_)
            a0 = ext[GLU_HALO:]
            a1 = pltpu.roll(ext, 1, 0)[GLU_HALO:]
            a2 = pltpu.roll(ext, 2, 0)[GLU_HALO:]
            ac = a2 * cw0 + a1 * cw1 + a0 * cw2 + cb
            vv = _rows(v_ref, r0, r0 + n, S_)
            dzv = _rows(dz_ref, r0, r0 + n, S_)
            gl = _gelu(ac)
            dac = dzv * vv * _gelu_grad(ac)
            da = (dac * cw2 + pltpu.roll(dac, n - 1, 0) * cw1 + pltpu.roll(dac, n - 2, 0) * cw0)[:ch]
            da_buf[slot, r0:r0 + ch, :] = da.astype(da_buf.dtype)
            dv_buf[slot, r0:r0 + ch, :] = (dzv[:ch] * gl[:ch]).astype(dv_buf.dtype)
            dc = dac[:ch]
            acc[0] = acc[0] + jnp.sum(dc * a2[:ch], axis=0, keepdims=True)
            acc[1] = acc[1] + jnp.sum(dc * a1[:ch], axis=0, keepdims=True)
            acc[2] = acc[2] + jnp.sum(dc * a0[:ch], axis=0, keepdims=True)
            acc[3] = acc[3] + jnp.sum(dc, axis=0, keepdims=True)
        dcw_ref[0:1, :] = acc[0]
        dcw_ref[1:2, :] = acc[1]
        dcw_ref[2:3, :] = acc[2]
        dcb_ref[...] = acc[3]
        for cp in writes(j, slot):
            cp.start()

        @pl.when(j == nt - 1)
        def _():
            for cp in writes(j, slot):
                cp.wait()
            if nt > 1:
                for cp in writes(j - 1, 1 - slot):
                    cp.wait()

    return pl.pallas_call(
        body, grid=(nt,),
        in_specs=[pl.BlockSpec((S_, tf), lambda j: (0, j)), pl.BlockSpec((S_, tf), lambda j: (0, j + nt)),
                  pl.BlockSpec((S_, tf), lambda j: (0, j)),
                  pl.BlockSpec((3, tf), lambda j: (0, j)), pl.BlockSpec((1, tf), lambda j: (0, j))],
        out_specs=[_ANY, pl.BlockSpec((3, tf), lambda j: (0, j)), pl.BlockSpec((1, tf), lambda j: (0, j))],
        out_shape=[jax.ShapeDtypeStruct((S_, F2), BF16), jax.ShapeDtypeStruct((3, Fh), F32),
                   jax.ShapeDtypeStruct((1, Fh), F32)],
        scratch_shapes=[pltpu.VMEM((2, S_, tf), BF16), pltpu.VMEM((2, S_, tf), BF16), pltpu.SemaphoreType.DMA((2, 2))],
        compiler_params=_cparams("arbitrary"), name=name,
    )(u, u, dz, conv_w, _row(conv_b))


def rope_tables(pos, inv, name, tr=512):
    S_ = pos.shape[0]
    tr = _pick(S_, tr, 8)

    def body(p_ref, inv_ref, c_ref, s1_ref, s2_ref):
        ang = p_ref[...] * inv_ref[...]
        lane = lax.broadcasted_iota(jnp.int32, ang.shape, 1)
        half = QK_ROPE // 2
        cosv, sinv = jnp.cos(ang), jnp.sin(ang)
        c_ref[...] = jnp.where(lane < QK_ROPE, cosv, 0.0)
        s1_ref[...] = jnp.where(lane < half, -sinv, 0.0)
        s2_ref[...] = jnp.where((lane >= half) & (lane < QK_ROPE), sinv, 0.0)

    blk = pl.BlockSpec((tr, LANES), lambda i: (i, 0))
    shp = jax.ShapeDtypeStruct((S_, LANES), F32)
    return pl.pallas_call(
        body, grid=(S_ // tr,),
        in_specs=[pl.BlockSpec((tr, 1), lambda i: (i, 0)), pl.BlockSpec((1, LANES), lambda i: (0, 0))],
        out_specs=[blk, blk, blk], out_shape=[shp, shp, shp],
        compiler_params=_cparams("parallel"), name=name,
    )(pos, inv)


_HALF = QK_ROPE // 2


def _rope(t, c, s1, s2):
    return t * c + pltpu.roll(t, LANES - _HALF, 1) * s1 + pltpu.roll(t, _HALF, 1) * s2


def _rope_t(d, c, s1, s2):
    return d * c + pltpu.roll(d * s1, _HALF, 1) + pltpu.roll(d * s2, LANES - _HALF, 1)


def q_prep(q, tabs, scale, backward, name, tr=512):
    S_, W = q.shape
    tr = _pick(S_, tr, 8)

    def body(q_ref, c_ref, s1_ref, s2_ref, o_ref):
        o_ref[:, 0:LANES] = (q_ref[:, 0:LANES].astype(F32) * scale).astype(o_ref.dtype)
        t = q_ref[:, LANES:2 * LANES].astype(F32)
        fn = _rope_t if backward else _rope
        o_ref[:, LANES:2 * LANES] = (fn(t, c_ref[...], s1_ref[...], s2_ref[...]) * scale).astype(o_ref.dtype)

    blk = pl.BlockSpec((tr, HEAD_PAD), lambda i, h: (i, h))
    tab = pl.BlockSpec((tr, LANES), lambda i, h: (i, 0))
    return pl.pallas_call(
        body, grid=(S_ // tr, W // HEAD_PAD), in_specs=[blk, tab, tab, tab], out_specs=blk,
        out_shape=jax.ShapeDtypeStruct((S_, W), BF16),
        compiler_params=_cparams("parallel", "parallel"), name=name,
    )(q, *tabs)


def k_prep(knv, kv_ext, tabs, name, tr=512):
    S_ = knv.shape[0]
    tr = _pick(S_, tr, 8)

    def body(kn_ref, v_ref, t_ref, c_ref, s1_ref, s2_ref, o_ref, vx_ref):
        o_ref[:, 0:LANES] = kn_ref[...].astype(o_ref.dtype)
        o_ref[:, LANES:2 * LANES] = _rope(t_ref[...], c_ref[...], s1_ref[...], s2_ref[...]).astype(o_ref.dtype)
        vx_ref[:, 0:V_HEAD] = v_ref[...].astype(vx_ref.dtype)
        vx_ref[:, V_HEAD:HEAD_PAD] = jnp.ones((tr, HEAD_PAD - V_HEAD), vx_ref.dtype)

    tab = pl.BlockSpec((tr, LANES), lambda i, h: (i, 0))
    head = pl.BlockSpec((tr, HEAD_PAD), lambda i, h: (i, h))
    shp = jax.ShapeDtypeStruct((S_, N_HEADS * HEAD_PAD), BF16)
    return pl.pallas_call(
        body, grid=(S_ // tr, N_HEADS),
        in_specs=[pl.BlockSpec((tr, LANES), lambda i, h: (i, h)),
                  pl.BlockSpec((tr, V_HEAD), lambda i, h: (i, N_HEADS + h)),
                  pl.BlockSpec((tr, LANES), lambda i, h: (i, KV_RANK // LANES)), tab, tab, tab],
        out_specs=[head, head], out_shape=[shp, shp],
        compiler_params=_cparams("parallel", "parallel"), name=name,
    )(knv, knv, kv_ext, *tabs)


def k_prep_bwd(dk_a, dk_b, dv_a, dv_b, tabs, name, tr=256):
    S_ = dk_a.shape[0]
    tr = _pick(S_, tr, 8)
    HV = N_HEADS * V_HEAD

    def body(ka_ref, kb_ref, va_ref, vb_ref, c_ref, s1_ref, s2_ref, o_ref, t_ref):
        dr = jnp.zeros((tr, LANES), F32)
        for h in range(N_HEADS):
            lo = h * HEAD_PAD
            o_ref[:, h * LANES:(h + 1) * LANES] = (ka_ref[:, lo:lo + LANES] + kb_ref[:, lo:lo + LANES]).astype(o_ref.dtype)
            dr = dr + ka_ref[:, lo + LANES:lo + 2 * LANES] + kb_ref[:, lo + LANES:lo + 2 * LANES]
        o_ref[:, HV:2 * HV] = (va_ref[...] + vb_ref[...]).astype(o_ref.dtype)
        t_ref[...] = _rope_t(dr, c_ref[...], s1_ref[...], s2_ref[...])

    kblk = pl.BlockSpec((tr, N_HEADS * HEAD_PAD), lambda i: (i, 0))
    vblk = pl.BlockSpec((tr, HV), lambda i: (i, 0))
    tab = pl.BlockSpec((tr, LANES), lambda i: (i, 0))
    return pl.pallas_call(
        body, grid=(S_ // tr,), in_specs=[kblk, kblk, vblk, vblk, tab, tab, tab],
        out_specs=[pl.BlockSpec((tr, 2 * HV), lambda i: (i, 0)), tab],
        out_shape=[jax.ShapeDtypeStruct((S_, 2 * HV), BF16), jax.ShapeDtypeStruct((S_, LANES), F32)],
        compiler_params=_cparams("parallel"), name=name,
    )(dk_a, dk_b, dv_a, dv_b, *tabs)


_NEG = -1e30


def attn_fwd(q, k, vx, name):
    S_ = q.shape[0]
    TQ = _pick(S_, ATT_Q_BLOCK, 8)
    TK = _pick(S_, ATT_K_BLOCK, TQ)
    HP = ATT_HEADS_PER_STEP
    W = HP * HEAD_PAD
    ratio = TK // TQ

    def body(q_ref, k_ref, v_ref, o_ref, lse_ref):
        i = pl.program_id(1)
        qs = [q_ref[:, h * HEAD_PAD:(h + 1) * HEAD_PAD] for h in range(HP)]

        def step(j, carry, masked):
            start = pl.multiple_of(j * TK, TK)
            out = []
            for h in range(HP):
                m, acc = carry[h]
                cols = slice(h * HEAD_PAD, (h + 1) * HEAD_PAD)
                s = lax.dot_general(qs[h], k_ref[pl.ds(start, TK), cols], _DIMS["nt"], preferred_element_type=F32)
                if masked:
                    rowi = i * TQ + lax.broadcasted_iota(jnp.int32, (TQ, TK), 0)
                    coli = j * TK + lax.broadcasted_iota(jnp.int32, (TQ, TK), 1)
                    s = jnp.where(coli <= rowi, s, _NEG)
                m_new = jnp.maximum(m, jnp.max(s, axis=-1, keepdims=True))
                alpha = jnp.exp(m - m_new)
                p = jnp.exp(s - m_new).astype(BF16)
                acc = alpha * acc + lax.dot_general(p, v_ref[pl.ds(start, TK), cols], _DIMS["nn"],
                                                    preferred_element_type=F32)
                out.append((m_new, acc))
            return tuple(out)

        init = tuple((jnp.full((TQ, 1), _NEG, F32), jnp.zeros((TQ, HEAD_PAD), F32)) for _ in range(HP))
        last = i // ratio
        carry = step(last, lax.fori_loop(0, last, functools.partial(step, masked=False), init), True)
        for h in range(HP):
            m, acc = carry[h]
            l = acc[:, V_HEAD:]
            o_ref[:, h * V_HEAD:(h + 1) * V_HEAD] = (acc[:, :V_HEAD] / l).astype(o_ref.dtype)
            lse_ref[h] = m + jnp.log(jnp.max(l, axis=-1, keepdims=True))

    return pl.pallas_call(
        body, grid=(N_HEADS // HP, S_ // TQ),
        in_specs=[pl.BlockSpec((TQ, W), lambda g, i: (i, g)),
                  pl.BlockSpec((S_, W), lambda g, i: (0, g)),
                  pl.BlockSpec((S_, W), lambda g, i: (0, g))],
        out_specs=[pl.BlockSpec((TQ, HP * V_HEAD), lambda g, i: (i, g)),
                   pl.BlockSpec((HP, TQ, 1), lambda g, i: (g, i, 0))],
        out_shape=[jax.ShapeDtypeStruct((S_, N_HEADS * V_HEAD), BF16), jax.ShapeDtypeStruct((N_HEADS, S_, 1), F32)],
        compiler_params=_cparams("parallel", "parallel"), name=name,
    )(q, k, vx)


def attn_delta(o, do, name, tr=512):
    S_ = o.shape[0]
    tr = _pick(S_, tr, 8)

    def body(o_ref, do_ref, d_ref):
        d_ref[...] = jnp.sum(o_ref[...].astype(F32) * do_ref[...].astype(F32), axis=-1, keepdims=True)

    blk = pl.BlockSpec((tr, V_HEAD), lambda i, h: (i, h))
    return pl.pallas_call(
        body, grid=(S_ // tr, N_HEADS), in_specs=[blk, blk],
        out_specs=pl.BlockSpec((None, tr, 1), lambda i, h: (h, i, 0)),
        out_shape=jax.ShapeDtypeStruct((N_HEADS, S_, 1), F32),
        compiler_params=_cparams("parallel", "parallel"), name=name,
    )(o, do)


def attn_bwd(q, k, vx, do, lse_row, delta_row, name):
    S_ = q.shape[0]
    TK = _pick(S_, ATT_BWD_K_BLOCK, LANES)
    TQ = _pick(S_, ATT_BWD_Q_BLOCK, TK)
    HP = ATT_HEADS_PER_STEP
    W = HP * HEAD_PAD
    ratio = TQ // TK
    nq = S_ // TQ

    def body(q_ref, do_ref, lse_ref, dl_ref, k_ref, v_ref, dq_ref, dk_ref, dv_ref):
        j = pl.program_id(1)

        @pl.when(j == 0)
        def _():
            dq_ref[...] = jnp.zeros_like(dq_ref)

        ks = [k_ref[:, h * HEAD_PAD:(h + 1) * HEAD_PAD] for h in range(HP)]
        vs = [v_ref[:, h * HEAD_PAD:h * HEAD_PAD + V_HEAD] for h in range(HP)]

        def step(i, carry, masked):
            start = pl.multiple_of(i * TQ, TQ)
            out = []
            for h in range(HP):
                dk, dv = carry[h]
                cols = slice(h * HEAD_PAD, (h + 1) * HEAD_PAD)
                qv = q_ref[pl.ds(start, TQ), cols]
                dov = do_ref[pl.ds(start, TQ), h * V_HEAD:(h + 1) * V_HEAD]
                st = lax.dot_general(ks[h], qv, _DIMS["nt"], preferred_element_type=F32)
                pt = jnp.exp(st - lse_ref[h, :, pl.ds(start, TQ)])
                if masked:
                    keyi = j * TK + lax.broadcasted_iota(jnp.int32, (TK, TQ), 0)
                    qryi = i * TQ + lax.broadcasted_iota(jnp.int32, (TK, TQ), 1)
                    pt = jnp.where(keyi <= qryi, pt, 0.0)
                dpt = lax.dot_general(vs[h], dov, _DIMS["nt"], preferred_element_type=F32)
                dst = (pt * (dpt - dl_ref[h, :, pl.ds(start, TQ)])).astype(BF16)
                dv = dv + lax.dot_general(pt.astype(BF16), dov, _DIMS["nn"], preferred_element_type=F32)
                dk = dk + lax.dot_general(dst, qv, _DIMS["nn"], preferred_element_type=F32)
                dq_ref[pl.ds(start, TQ), cols] += lax.dot_general(dst, ks[h], _DIMS["tn"], preferred_element_type=F32)
                out.append((dk, dv))
            return tuple(out)

        init = tuple((jnp.zeros((TK, HEAD_PAD), F32), jnp.zeros((TK, V_HEAD), F32)) for _ in range(HP))
        first = j // ratio
        carry = lax.fori_loop(first + 1, nq, functools.partial(step, masked=False), step(first, init, True))
        for h in range(HP):
            dk_ref[:, h * HEAD_PAD:(h + 1) * HEAD_PAD] = carry[h][0]
            dv_ref[:, h * V_HEAD:(h + 1) * V_HEAD] = carry[h][1]

    return pl.pallas_call(
        body, grid=(N_HEADS // HP, S_ // TK),
        in_specs=[pl.BlockSpec((S_, W), lambda g, j: (0, g)),
                  pl.BlockSpec((S_, HP * V_HEAD), lambda g, j: (0, g)),
                  pl.BlockSpec((HP, 1, S_), lambda g, j: (g, 0, 0)),
                  pl.BlockSpec((HP, 1, S_), lambda g, j: (g, 0, 0)),
                  pl.BlockSpec((TK, W), lambda g, j: (j, g)),
                  pl.BlockSpec((TK, W), lambda g, j: (j, g))],
        out_specs=[pl.BlockSpec((S_, W), lambda g, j: (0, g)),
                   pl.BlockSpec((TK, W), lambda g, j: (j, g)),
                   pl.BlockSpec((TK, HP * V_HEAD), lambda g, j: (j, g))],
        out_shape=[jax.ShapeDtypeStruct((S_, N_HEADS * HEAD_PAD), F32),
                   jax.ShapeDtypeStruct((S_, N_HEADS * HEAD_PAD), F32),
                   jax.ShapeDtypeStruct((S_, N_HEADS * V_HEAD), F32)],
        compiler_params=_cparams("parallel", "arbitrary"), name=name,
    )(q, do, lse_row, delta_row, k, vx)


def mods_fwd(c_all, mod_w, mod_b, name, tn=512):
    L, Dn, E = mod_w.shape
    R = c_all.shape[0]
    tn = _pick(E, tn, LANES)

    def body(c_ref, w_ref, b_ref, o_ref):
        cv = c_ref[...]
        sc = (cv / (1.0 + jnp.exp(-cv))).astype(BF16)
        o_ref[...] = lax.dot_general(sc, w_ref[...].astype(BF16), _DIMS["nn"], preferred_element_type=F32) + b_ref[...]

    return pl.pallas_call(
        body, grid=(L, E // tn),
        in_specs=[pl.BlockSpec((R, Dn), lambda l, j: (0, 0)), pl.BlockSpec((None, Dn, tn), lambda l, j: (l, 0, j)),
                  pl.BlockSpec((None, 1, tn), lambda l, j: (l, 0, j))],
        out_specs=pl.BlockSpec((None, R, tn), lambda l, j: (l, 0, j)),
        out_shape=jax.ShapeDtypeStruct((L, R, E), F32),
        compiler_params=_cparams("parallel", "parallel"), name=name,
    )(c_all, mod_w, mod_b.reshape(L, 1, E))


def _adam_math(w, g, m, v):
    m = ADAM_B1 * m + (1.0 - ADAM_B1) * g
    v = ADAM_B2 * v + (1.0 - ADAM_B2) * (g * g)
    m_hat = m / (1.0 - ADAM_B1 ** ADAM_STEP)
    v_hat = v / (1.0 - ADAM_B2 ** ADAM_STEP)
    delta = -ADAM_LR * (m_hat / (jnp.sqrt(v_hat) + ADAM_EPS) + ADAM_WD * w)
    return delta, m, v


def _as2d(a):
    return a.reshape(-1, a.shape[-1]) if a.ndim != 2 else a


def adamw(w, g, m, v, name):
    shape = w.shape
    w2, g2, m2, v2 = _as2d(w), _as2d(g), _as2d(m), _as2d(v)
    R, C = w2.shape
    tr = _pick(R, max(8, (1 << 18) // C // 8 * 8), 8)

    def body(w_ref, g_ref, m_ref, v_ref, d_ref, mo_ref, vo_ref):
        d, mn, vn = _adam_math(w_ref[...], g_ref[...], m_ref[...], v_ref[...])
        d_ref[...] = d
        mo_ref[...] = mn
        vo_ref[...] = vn

    blk = pl.BlockSpec((tr, C), lambda i: (i, 0))
    shp = jax.ShapeDtypeStruct((R, C), F32)
    outs = pl.pallas_call(
        body, grid=(R // tr,), in_specs=[blk] * 4, out_specs=[blk] * 3, out_shape=[shp] * 3,
        compiler_params=_cparams("parallel"), name=name,
    )(w2, g2, m2, v2)
    return tuple(o.reshape(shape) for o in outs)


def adamw_sum(parts, w, m, v, name):
    P, R, C = parts.shape

    def body(p_ref, w_ref, m_ref, v_ref, g_ref, d_ref, mo_ref, vo_ref):
        g = p_ref[0]
        for k in range(1, P):
            g = g + p_ref[k]
        d, mn, vn = _adam_math(w_ref[...], g, m_ref[...], v_ref[...])
        g_ref[...] = g
        d_ref[...] = d
        mo_ref[...] = mn
        vo_ref[...] = vn

    shp = jax.ShapeDtypeStruct((R, C), F32)
    return pl.pallas_call(body, out_shape=[shp] * 4, compiler_params=_cparams(), name=name)(parts, w, m, v)


def adamw_modw(c_col, dm, w, m, v, name, tr=256, tn=512):
    L, Dn, E = w.shape
    B = c_col.shape[0]
    tr = _pick(Dn, tr, 8)
    tn = _pick(E, tn, LANES)

    def body(c_ref, dm_ref, w_ref, m_ref, v_ref, g_ref, d_ref, mo_ref, vo_ref):
        g = jnp.zeros((tr, tn), F32)
        for b in range(B):
            cv = c_ref[b]
            g = g + (cv / (1.0 + jnp.exp(-cv))) * dm_ref[b:b + 1, :]
        d, mn, vn = _adam_math(w_ref[...], g, m_ref[...], v_ref[...])
        g_ref[...] = g
        d_ref[...] = d
        mo_ref[...] = mn
        vo_ref[...] = vn

    blk = pl.BlockSpec((None, tr, tn), lambda l, i, j: (l, i, j))
    shp = jax.ShapeDtypeStruct((L, Dn, E), F32)
    return pl.pallas_call(
        body, grid=(L, Dn // tr, E // tn),
        in_specs=[pl.BlockSpec((B, tr, 1), lambda l, i, j: (0, i, 0)),
                  pl.BlockSpec((None, B, tn), lambda l, i, j: (l, 0, j)), blk, blk, blk],
        out_specs=[blk] * 4, out_shape=[shp] * 4,
        compiler_params=_cparams("parallel", "parallel", "parallel"), name=name,
    )(c_col, dm, w, m, v)


def add_round(a, b, name, tr=512):
    R, C = a.shape
    tr = _pick(R, tr, 16)

    def body(a_ref, b_ref, o_ref):
        o_ref[...] = (a_ref[...] + b_ref[...].astype(F32)).astype(BF16)

    blk = pl.BlockSpec((tr, C), lambda i: (i, 0))
    return pl.pallas_call(
        body, grid=(R // tr,), in_specs=[blk, blk], out_specs=blk, out_shape=jax.ShapeDtypeStruct((R, C), BF16),
        compiler_params=_cparams("parallel"), name=name,
    )(a, b)


def sum_parts(parts, name, tr=512):
    P, R, C = parts.shape
    tr = _pick(R, tr, 16)

    def body(p_ref, o_ref):
        s = p_ref[0].astype(F32)
        for k in range(1, P):
            s = s + p_ref[k].astype(F32)
        o_ref[...] = s

    return pl.pallas_call(
        body, grid=(R // tr,), in_specs=[pl.BlockSpec((P, tr, C), lambda i: (0, i, 0))],
        out_specs=pl.BlockSpec((tr, C), lambda i: (i, 0)), out_shape=jax.ShapeDtypeStruct((R, C), F32),
        compiler_params=_cparams("parallel"), name=name,
    )(parts)


_ANY = pl.BlockSpec(memory_space=pl.ANY)


def _place():
    return lax.axis_index("x"), lax.axis_index("y"), lax.axis_index("c")


def _flip(v, bit):
    return 1 - v if bit else v


def chip_gather(buf, name):
    def body(in_ref, out_ref, send_sems, recv_sems):
        x, y, c = _place()
        me = 2 * x + y
        sends = []
        for k in range(1, N_CHIPS):
            px, py = _flip(x, k >> 1), _flip(y, k & 1)
            cp = pltpu.make_async_remote_copy(src_ref=in_ref, dst_ref=out_ref.at[me], send_sem=send_sems.at[k - 1],
                                              recv_sem=recv_sems.at[k - 1], device_id=(px, py, c), device_id_type=MESH)
            cp.start()
            sends.append(cp)
        for k in range(1, N_CHIPS):
            px, py = _flip(x, k >> 1), _flip(y, k & 1)
            pltpu.make_async_remote_copy(src_ref=in_ref, dst_ref=out_ref.at[2 * px + py], send_sem=send_sems.at[k - 1],
                                         recv_sem=recv_sems.at[k - 1], device_id=(px, py, c),
                                         device_id_type=MESH).wait_recv()
        for cp in sends:
            cp.wait_send()

    out = pl.pallas_call(
        body, in_specs=[_ANY], out_specs=_ANY,
        out_shape=jax.ShapeDtypeStruct((N_CHIPS,) + buf.shape, buf.dtype),
        scratch_shapes=[pltpu.SemaphoreType.DMA((N_CHIPS - 1,)), pltpu.SemaphoreType.DMA((N_CHIPS - 1,))],
        name=name,
    )(buf)
    return lax.dynamic_update_index_in_dim(out, buf, 2 * lax.axis_index("x") + lax.axis_index("y"), 0)


def chip_all_to_all(buf, name):
    def body(in_ref, out_ref, send_sems, recv_sems):
        x, y, c = _place()
        me = 2 * x + y
        sends = []
        for k in range(1, N_CHIPS):
            px, py = _flip(x, k >> 1), _flip(y, k & 1)
            cp = pltpu.make_async_remote_copy(src_ref=in_ref.at[2 * px + py], dst_ref=out_ref.at[me],
                                              send_sem=send_sems.at[k - 1], recv_sem=recv_sems.at[k - 1],
                                              device_id=(px, py, c), device_id_type=MESH)
            cp.start()
            sends.append(cp)
        for k in range(1, N_CHIPS):
            px, py = _flip(x, k >> 1), _flip(y, k & 1)
            pltpu.make_async_remote_copy(src_ref=in_ref.at[me], dst_ref=out_ref.at[2 * px + py],
                                         send_sem=send_sems.at[k - 1], recv_sem=recv_sems.at[k - 1],
                                         device_id=(px, py, c), device_id_type=MESH).wait_recv()
        for cp in sends:
            cp.wait_send()

    out = pl.pallas_call(
        body, in_specs=[_ANY], out_specs=_ANY, out_shape=jax.ShapeDtypeStruct(buf.shape, buf.dtype),
        scratch_shapes=[pltpu.SemaphoreType.DMA((N_CHIPS - 1,)), pltpu.SemaphoreType.DMA((N_CHIPS - 1,))],
        name=name,
    )(buf)
    me = 2 * lax.axis_index("x") + lax.axis_index("y")
    return lax.dynamic_update_index_in_dim(out, _index(buf, me), me, 0)


def core_gather(buf, name):
    def body(in_ref, out_ref, send_sem, recv_sem):
        x, y, c = _place()
        cp = pltpu.make_async_remote_copy(src_ref=in_ref, dst_ref=out_ref.at[c], send_sem=send_sem, recv_sem=recv_sem,
                                          device_id=(x, y, 1 - c), device_id_type=MESH)
        cp.start()
        pltpu.make_async_remote_copy(src_ref=in_ref, dst_ref=out_ref.at[1 - c], send_sem=send_sem, recv_sem=recv_sem,
                                     device_id=(x, y, 1 - c), device_id_type=MESH).wait_recv()
        cp.wait_send()

    out = pl.pallas_call(
        body, in_specs=[_ANY], out_specs=_ANY, out_shape=jax.ShapeDtypeStruct((2,) + buf.shape, buf.dtype),
        scratch_shapes=[pltpu.SemaphoreType.DMA, pltpu.SemaphoreType.DMA],
        name=name,
    )(buf)
    return lax.dynamic_update_index_in_dim(out, buf, lax.axis_index("c"), 0)


def core_swap(buf, name):
    def body(in_ref, out_ref, send_sem, recv_sem):
        x, y, c = _place()
        cp = pltpu.make_async_remote_copy(src_ref=in_ref, dst_ref=out_ref, send_sem=send_sem, recv_sem=recv_sem,
                                          device_id=(x, y, 1 - c), device_id_type=MESH)
        cp.start()
        cp.wait()

    return pl.pallas_call(
        body, in_specs=[_ANY], out_specs=_ANY, out_shape=jax.ShapeDtypeStruct(buf.shape, buf.dtype),
        scratch_shapes=[pltpu.SemaphoreType.DMA, pltpu.SemaphoreType.DMA],
        name=name,
    )(buf)


def device_gather(buf, name):
    def body(in_ref, out_ref, send_sems, recv_sems, local_sem):
        x, y, c = _place()
        me = 4 * x + 2 * y + c
        mine = pltpu.make_async_copy(in_ref, out_ref.at[me], local_sem)
        mine.start()
        sends = []
        for k in range(1, N_DEV):
            peer = (_flip(x, (k >> 2) & 1), _flip(y, (k >> 1) & 1), _flip(c, k & 1))
            cp = pltpu.make_async_remote_copy(src_ref=in_ref, dst_ref=out_ref.at[me], send_sem=send_sems.at[k - 1],
                                              recv_sem=recv_sems.at[k - 1], device_id=peer, device_id_type=MESH)
            cp.start()
            sends.append(cp)
        for k in range(1, N_DEV):
            peer = (_flip(x, (k >> 2) & 1), _flip(y, (k >> 1) & 1), _flip(c, k & 1))
            pltpu.make_async_remote_copy(src_ref=in_ref, dst_ref=out_ref.at[4 * peer[0] + 2 * peer[1] + peer[2]],
                                         send_sem=send_sems.at[k - 1], recv_sem=recv_sems.at[k - 1], device_id=peer,
                                         device_id_type=MESH).wait_recv()
        for cp in sends:
            cp.wait_send()
        mine.wait()

    return pl.pallas_call(
        body, in_specs=[_ANY], out_specs=_ANY, out_shape=jax.ShapeDtypeStruct((N_DEV,) + buf.shape, buf.dtype),
        scratch_shapes=[pltpu.SemaphoreType.DMA((N_DEV - 1,)), pltpu.SemaphoreType.DMA((N_DEV - 1,)),
                        pltpu.SemaphoreType.DMA],
        name=name,
    )(buf)


WEIGHT_ORDER = ["mod_w", "mod_b", "norm1_g", "norm2_g", "pool_w", "pool_b", "pool_scale", "kv_in_g", "w_dkv",
                "ckv_norm_g", "w_uk", "w_uv", "w_dq", "q_norm_g", "w_uq", "w_o", "w_up", "conv_w", "conv_b", "w_down",
                "final_g"]
EXCHANGED = {"w_up": 2, "w_down": 1, "w_o": 1, "w_uq": 2, "w_dq": 1, "pool_w": 2, "w_dkv": 0, "w_uk": 1, "w_uv": 1,
             "conv_w": 2, "pool_b": 1, "pool_scale": 1}
KEPT_F32 = ("conv_w", "pool_b", "pool_scale")
REPLICATED = ["mod_b", "norm1_g", "norm2_g", "kv_in_g", "ckv_norm_g", "q_norm_g", "conv_b", "final_g"]
PACK_ALIGN = 2 * 16 * PACK_COLS


def _padded(n, align):
    return -(-n // align) * align


def _flat_pad(parts, total):
    flat = jnp.concatenate(parts, axis=-1)
    pad = total - flat.shape[-1]
    if pad:
        flat = jnp.concatenate([flat, jnp.zeros(flat.shape[:-1] + (pad,), flat.dtype)], axis=-1)
    return flat


def _split_shards(full, axis):
    shp = full.shape
    t = full.reshape(shp[:axis] + (N_CHIPS, shp[axis] // N_CHIPS) + shp[axis + 1:])
    return jnp.moveaxis(t, axis, 0).reshape(N_CHIPS, -1)


def _join_shards(rows, shard_shape, axis):
    t = jnp.moveaxis(rows.reshape((N_CHIPS,) + tuple(shard_shape)), 0, axis)
    return t.reshape(tuple(shard_shape[:axis]) + (N_CHIPS * shard_shape[axis],) + tuple(shard_shape[axis + 1:]))


def _index(a, i, axis=0):
    return lax.dynamic_index_in_dim(a, i, axis, keepdims=False)


def kernel(x, c, positions, mod_w, mod_b, norm1_g, norm2_g, pool_w, pool_b, pool_scale, kv_in_g, w_dkv, ckv_norm_g, w_uk, w_uv, w_dq, q_norm_g, w_uq, w_o, w_up, conv_w, conv_b, w_down, final_g, loss_target, m_mod_w, m_mod_b, m_norm1_g, m_norm2_g, m_pool_w, m_pool_b, m_pool_scale, m_kv_in_g, m_w_dkv, m_ckv_norm_g, m_w_uk, m_w_uv, m_w_dq, m_q_norm_g, m_w_uq, m_w_o, m_w_up, m_conv_w, m_conv_b, m_w_down, m_final_g, v_mod_w, v_mod_b, v_norm1_g, v_norm2_g, v_pool_w, v_pool_b, v_pool_scale, v_kv_in_g, v_w_dkv, v_ckv_norm_g, v_w_uk, v_w_uv, v_w_dq, v_q_norm_g, v_w_uq, v_w_o, v_w_up, v_conv_w, v_conv_b, v_w_down, v_final_g):
    given = dict(locals())
    W = {n: given[n] for n in WEIGHT_ORDER}
    M1 = {n: given["m_" + n] for n in WEIGHT_ORDER}
    V2 = {n: given["v_" + n] for n in WEIGHT_ORDER}
    xi, yi, ci = lax.axis_index("x"), lax.axis_index("y"), lax.axis_index("c")
    chip = 2 * xi + yi
    dev = 4 * xi + 2 * yi + ci
    x0 = x[0]
    S_, D = x0.shape
    Fh = conv_b.shape[1]
    E = mod_b.shape[1]
    Es = E // N_CHIPS
    zD = jnp.zeros((D,), F32)

    sizes = {n: math.prod(W[n].shape) * (2 if n in KEPT_F32 else 1) for n in EXCHANGED}
    T = _padded(sum(sizes.values()), PACK_ALIGN)
    R = T // PACK_COLS
    own = _flat_pad([lax.bitcast_convert_type(W[n], BF16).reshape(-1) if n in KEPT_F32 else W[n].astype(BF16).reshape(-1)
                     for n in EXCHANGED], T)
    own_half = _index(own.reshape(2, R // 2, PACK_COLS), ci)
    halves = core_gather(chip_gather(own_half, "gather_w_chips"), "gather_w_cores")
    rows = jnp.swapaxes(halves, 0, 1).reshape(N_CHIPS, T)
    full = {}
    off = 0
    for n, axis in EXCHANGED.items():
        seg = rows[:, off:off + sizes[n]]
        off += sizes[n]
        if n in KEPT_F32:
            seg = lax.bitcast_convert_type(seg.reshape(N_CHIPS, -1, 2), F32)
        full[n] = _join_shards(seg, W[n].shape, axis)

    n_mla = DEPTH - N_A
    q_rank = full["w_uq"].shape[1]
    wq = full["w_uq"].reshape(n_mla, q_rank, N_HEADS, QK_HEAD)
    w_uq_ext = jnp.concatenate([wq, jnp.zeros((n_mla, q_rank, N_HEADS, HEAD_PAD - QK_HEAD), BF16)],
                               axis=3).reshape(n_mla, q_rank, N_HEADS * HEAD_PAD)
    kv_w = KV_RANK + QK_ROPE
    w_dkv_ext = jnp.concatenate([full["w_dkv"], jnp.zeros((D, KV_RANK + LANES - kv_w), BF16)], axis=1)
    w_ukv = jnp.concatenate([full["w_uk"], full["w_uv"]], axis=1)

    c_all = device_gather(c, "gather_c").reshape(N_DEV, D)
    c_pad = jnp.concatenate([c_all, jnp.zeros((16 - N_DEV, D), F32)], axis=0)
    mod_b_mine = lax.dynamic_slice_in_dim(mod_b, chip * Es, Es, axis=1)
    mods_part = mods_fwd(c_pad, mod_w, mod_b_mine, "mods_fwd")
    mods_all = chip_gather(mods_part, "gather_mods")
    mods = jnp.swapaxes(_index(mods_all, dev, axis=2), 0, 1).reshape(DEPTH, E)
    mod = [[mods[l, k * D:(k + 1) * D] for k in range(6)] for l in range(DEPTH)]

    half = QK_ROPE // 2
    inv = 1.0 / (ROPE_THETA ** (jnp.arange(0, QK_ROPE, 2, dtype=F32) / QK_ROPE))
    inv_row = jnp.concatenate([inv, inv, jnp.zeros((LANES - 2 * half,), F32)]).reshape(1, LANES)
    tabs = rope_tables(positions[0].astype(F32).reshape(S_, 1), inv_row, "rope_tables")
    att_scale = QK_HEAD ** -0.5

    saved = []
    xcur = x0
    kv_saved = None
    K = VX = knv = None
    for l in range(DEPTH):
        sh1, sc1, g1, sh2, sc2, g2 = mod[l]
        st = {"xin": xcur}
        if l < N_A:
            h1 = norm_fwd(xcur, norm1_g[l], sc1, sh1, F32, f"norm1_fwd{l}")
            st["pooled"] = _pool_call(h1, BF16, f"pool_fwd{l}", False)
            st["cs"] = g1 * full["pool_scale"][l]
            st["ypre"], xmid = gmm(st["pooled"], full["pool_w"][l], "nn", F32, f"pool_mm{l}", bias=full["pool_b"][l],
                                   res=xcur, colscale=st["cs"])
        else:
            j = l - N_A
            st["h1"] = norm_fwd(xcur, norm1_g[l], sc1, sh1, BF16, f"norm1_fwd{l}")
            st["ql"] = mm(st["h1"], full["w_dq"], "nn", F32, f"dq_mm{l}", layer=j)
            st["cq"] = norm_fwd(st["ql"], q_norm_g[j], jnp.zeros_like(q_norm_g[j]), jnp.zeros_like(q_norm_g[j]), BF16,
                                f"qnorm_fwd{l}")
            qe = mm(st["cq"], w_uq_ext, "nn", F32, f"uq_mm{l}", layer=j)
            st["Q"] = q_prep(qe, tabs, att_scale, False, f"q_prep{l}")
            st["o"], lse = attn_fwd(st["Q"], K, VX, f"attn_fwd{l}")
            st["lse"] = lse.reshape(N_HEADS, 1, S_)
            st["y"], xmid = mm(st["o"], full["w_o"], "nn", F32, f"wo_mm{l}", res=xcur, colscale=g1, layer=j)
        st["xmid"] = xmid
        st["h2"] = norm_fwd(xmid, norm2_g[l], sc2, sh2, BF16, f"norm2_fwd{l}")
        st["u"] = mm(st["h2"], full["w_up"], "nn", BF16, f"up_mm{l}", layer=l)
        st["z"] = glu_fwd(st["u"], full["conv_w"][l], conv_b[l], f"glu_fwd{l}")
        st["f"], xcur = mm(st["z"], full["w_down"], "nn", F32, f"down_mm{l}", tk=1408, res=xmid, colscale=g2, layer=l)
        saved.append(st)
        if l == N_A - 1:
            xn = norm_fwd(xcur, kv_in_g, zD, zD, BF16, "kvin_fwd")
            kv_ext = mm(xn, w_dkv_ext, "nn", F32, "dkv_mm")
            lat = kv_ext[:, :KV_RANK]
            zk = jnp.zeros((KV_RANK,), F32)
            ckv = norm_fwd(lat, ckv_norm_g, zk, zk, BF16, "ckv_fwd")
            knv = mm(ckv, w_ukv, "nn", BF16, "ukv_mm")
            K, VX = k_prep(knv, kv_ext, tabs, "k_prep")
            kv_saved = {"x": xcur, "xn": xn, "lat": lat, "ckv": ckv}

    dx, d_final_g, loss_part = loss_head(xcur, final_g, loss_target[0], "loss_head")
    loss = lax.psum(loss_part[0, 0], ("x", "y", "c"))

    G = {}
    dmods = [None] * DEPTH
    d_norm1 = [None] * DEPTH
    d_norm2 = [None] * DEPTH
    d_conv_b = [None] * DEPTH
    d_qnorm = [None] * n_mla
    dkv_acc = []
    for l in reversed(range(DEPTH)):
        sh1, sc1, g1, sh2, sc2, g2 = mod[l]
        st = saved[l]
        df, a2, _ = gate_bwd(dx, st["f"], g2, f"gate2_bwd{l}")
        dz = mm(df, full["w_down"], "nt", BF16, f"down_dx{l}", layer=l)
        G[("w_down", l)] = mm(st["z"], df, "tn", F32, f"down_dw{l}")
        du, dcw, dcb = glu_bwd(st["u"], dz, full["conv_w"][l], conv_b[l], f"glu_bwd{l}")
        G[("conv_w", l)] = dcw
        d_conv_b[l] = dcb[0]
        dh2 = mm(du, full["w_up"], "nt", BF16, f"up_dx{l}", tk=1408, layer=l)
        G[("w_up", l)] = mm(st["h2"], du, "tn", F32, f"up_dw{l}")
        dxmid, s1, s2 = norm_bwd(st["xmid"], norm2_g[l], sc2, dh2, dx, f"norm2_bwd{l}")
        dsh2, dsc2, d_norm2[l] = s1[0], s2[0] * norm2_g[l], s2[0] * (1.0 + sc2)
        if l < N_A:
            dyp, a1, csum = gate_bwd(dxmid, st["ypre"], st["cs"], f"gate1_bwd{l}")
            dg1 = full["pool_scale"][l] * a1[0]
            G[("pool_scale", l)] = g1 * a1[0]
            G[("pool_b", l)] = st["cs"] * csum[0]
            dpooled = gmm(dyp, full["pool_w"][l], "nt", F32, f"pool_dx{l}")
            G[("pool_w", l)] = gmm(st["pooled"], dyp, "tn", F32, f"pool_dw{l}")
            dh1 = _pool_call(dpooled, F32, f"pool_bwd{l}", True)
        else:
            j = l - N_A
            dy, a1, _ = gate_bwd(dxmid, st["y"], g1, f"gate1_bwd{l}")
            dg1 = a1[0]
            do = mm(dy, full["w_o"], "nt", BF16, f"wo_dx{l}", layer=j)
            G[("w_o", j)] = mm(st["o"], dy, "tn", F32, f"wo_dw{l}")
            delta = attn_delta(st["o"], do, f"attn_delta{l}").reshape(N_HEADS, 1, S_)
            dQ, dK, dV = attn_bwd(st["Q"], K, VX, do, st["lse"], delta, f"attn_bwd{l}")
            dkv_acc.append((dK, dV))
            dqe = q_prep(dQ, tabs, att_scale, True, f"q_prep_bwd{l}")
            dcq = mm(dqe, w_uq_ext, "nt", F32, f"uq_dx{l}", layer=j)
            G[("w_uq", j)] = mm(st["cq"], dqe, "tn", F32, f"uq_dw{l}").reshape(q_rank, N_HEADS, HEAD_PAD)[
                :, :, :QK_HEAD].reshape(q_rank, N_HEADS * QK_HEAD)
            zq = jnp.zeros_like(q_norm_g[j])
            dql, _, s2q = norm_bwd(st["ql"], q_norm_g[j], zq, dcq, None, f"qnorm_bwd{l}")
            d_qnorm[j] = s2q[0]
            dh1 = mm(dql, full["w_dq"], "nt", BF16, f"dq_dx{l}", layer=j)
            G[("w_dq", j)] = mm(st["h1"], dql, "tn", F32, f"dq_dw{l}")
        dx, s1, s2 = norm_bwd(st["xin"], norm1_g[l], sc1, dh1, dxmid, f"norm1_bwd{l}")
        dsh1, dsc1, d_norm1[l] = s1[0], s2[0] * norm1_g[l], s2[0] * (1.0 + sc1)
        dmods[l] = jnp.concatenate([dsh1, dsc1, dg1, dsh2, dsc2, a2[0]])
        if l == N_A:
            (dk_a, dv_a), (dk_b, dv_b) = dkv_acc
            dknv, d_tk = k_prep_bwd(dk_a, dk_b, dv_a, dv_b, tabs, "k_prep_bwd")
            dckv = mm(dknv, w_ukv, "nt", F32, "ukv_dx")
            d_ukv = mm(kv_saved["ckv"], dknv, "tn", F32, "ukv_dw")
            G[("w_uk", 0)], G[("w_uv", 0)] = d_ukv[:, :N_HEADS * QK_NOPE], d_ukv[:, N_HEADS * QK_NOPE:]
            zk = jnp.zeros((KV_RANK,), F32)
            dlat, _, s2c = norm_bwd(kv_saved["lat"], ckv_norm_g, zk, dckv, None, "ckv_bwd")
            d_ckv_g = s2c[0]
            dkv_ext = jnp.concatenate([dlat, d_tk], axis=1)
            dxn = mm(dkv_ext, w_dkv_ext, "nt", BF16, "dkv_dx")
            G[("w_dkv", 0)] = mm(kv_saved["xn"], dkv_ext, "tn", F32, "dkv_dw")[:, :kv_w]
            dx, _, s2k = norm_bwd(kv_saved["x"], kv_in_g, zD, dxn, dx, "kvin_bwd")
            d_kvin_g = s2k[0]

    def stacked(n):
        k = W[n].shape[0] if W[n].ndim > 2 or n in ("pool_b", "pool_scale") else None
        return G[(n, 0)] if k is None else jnp.stack([G[(n, i)] for i in range(k)])

    gsizes = {n: math.prod(W[n].shape) for n in EXCHANGED}
    Tg = _padded(sum(gsizes.values()), PACK_ALIGN)
    Rg = Tg // PACK_COLS
    gflat = _flat_pad([_split_shards(stacked(n), axis) for n, axis in EXCHANGED.items()], Tg)
    gflat = gflat.reshape(N_CHIPS, 2, Rg // 2, PACK_COLS)
    keep = _index(gflat, ci, axis=1).reshape(N_CHIPS * Rg // 2, PACK_COLS)
    give = _index(gflat, 1 - ci, axis=1).astype(BF16).reshape(N_CHIPS * Rg // 2, PACK_COLS)
    chip_sum = add_round(keep, core_swap(give, "reduce_cores"), "reduce_cores_add")
    got = chip_all_to_all(chip_sum.reshape(N_CHIPS, Rg // 2, PACK_COLS), "reduce_chips")
    red = core_gather(sum_parts(got, "reduce_chips_add"), "reduce_gather").reshape(Tg)

    grads, deltas, new_m, new_v = {}, {}, {}, {}
    off = 0
    for n in EXCHANGED:
        grads[n] = red[off:off + gsizes[n]].reshape(W[n].shape)
        off += gsizes[n]
        deltas[n], new_m[n], new_v[n] = adamw(W[n], grads[n], M1[n], V2[n], f"adamw_{n}")

    small = {"mod_b": jnp.stack(dmods), "norm1_g": jnp.stack(d_norm1), "norm2_g": jnp.stack(d_norm2),
             "kv_in_g": d_kvin_g, "ckv_norm_g": d_ckv_g, "q_norm_g": jnp.stack(d_qnorm),
             "conv_b": jnp.stack(d_conv_b), "final_g": d_final_g[0]}
    ssizes = {n: math.prod(W[n].shape) for n in REPLICATED}
    Ts = _padded(sum(ssizes.values()), 8 * PACK_COLS)

    def pack_small(d):
        return _flat_pad([d[n].reshape(-1) for n in REPLICATED], Ts).reshape(Ts // PACK_COLS, PACK_COLS)

    parts = device_gather(pack_small(small), "gather_small")
    outs = adamw_sum(parts, pack_small(W), pack_small(M1), pack_small(V2), "adamw_small")
    off = 0
    for n in REPLICATED:
        for dst, o in zip((grads, deltas, new_m, new_v), outs):
            dst[n] = o.reshape(-1)[off:off + ssizes[n]].reshape(W[n].shape)
        off += ssizes[n]

    dm_all = parts.reshape(N_DEV, -1)[:, :DEPTH * E].reshape(N_DEV, DEPTH, E)
    dm_mine = jnp.swapaxes(lax.dynamic_slice_in_dim(dm_all, chip * Es, Es, axis=2), 0, 1)
    grads["mod_w"], deltas["mod_w"], new_m["mod_w"], new_v["mod_w"] = adamw_modw(
        c_all.reshape(N_DEV, D, 1), dm_mine, mod_w, m_mod_w, v_mod_w, "adamw_mod_w")

    return (loss, dx.reshape(x.shape), *[grads[n] for n in WEIGHT_ORDER], *[deltas[n] for n in WEIGHT_ORDER],
            *[new_m[n] for n in WEIGHT_ORDER], *[new_v[n] for n in WEIGHT_ORDER])
```

```python
import functools
import math

import jax
import jax.numpy as jnp
from jax import lax
from jax.experimental import pallas as pl
from jax.experimental.pallas import tpu as pltpu

F32 = jnp.float32
BF16 = jnp.bfloat16
MESH = pl.DeviceIdType.MESH

DEPTH = 4
N_A = 2
POOL_WINDOWS = (2, 4, 8, 16)
N_GROUPS = 4
N_HEADS = 8
QK_NOPE = 128
QK_ROPE = 64
V_HEAD = 128
QK_HEAD = QK_NOPE + QK_ROPE
HEAD_PAD = 256
KV_RANK = 256
ROPE_THETA = 10000.0
EPS = 1e-6
ADAM_LR = 0.001
ADAM_B1 = 0.9
ADAM_B2 = 0.999
ADAM_EPS = 1e-08
ADAM_WD = 0.01
ADAM_STEP = 10

N_CHIPS = 4
N_DEV = 8
LANES = 128
PACK_COLS = 1024
VMEM_LIMIT = 56 * 1024 * 1024
GLU_TILE = 256
ATT_BWD_K_BLOCK = 256
ATT_BWD_Q_BLOCK = 512
ATT_Q_BLOCK = 256
ATT_K_BLOCK = 512
ATT_HEADS_PER_STEP = 2


def _cparams(*sem):
    return pltpu.CompilerParams(dimension_semantics=sem if sem else None, vmem_limit_bytes=VMEM_LIMIT)


def _pick(n, target, mult):
    best = None
    d = mult
    while d <= min(n, target):
        if n % d == 0:
            best = d
        d += mult
    return n if best is None else best


def _row(v):
    return v.reshape(1, -1).astype(F32)


_DIMS = {"nn": (((1,), (0,)), ((), ())), "nt": (((1,), (1,)), ((), ())), "tn": (((0,), (0,)), ((), ()))}


def _mm_body(mode, nk, has_bias, has_res):
    def body(*refs):
        a_ref, b_ref = refs[0], refs[1]
        pos = 2
        bias_ref = res_ref = cs_ref = None
        if has_bias:
            bias_ref = refs[pos]
            pos += 1
        if has_res:
            res_ref, cs_ref = refs[pos], refs[pos + 1]
            pos += 2
        o_ref = refs[pos]
        pos += 1
        o2_ref = None
        if has_res:
            o2_ref = refs[pos]
            pos += 1
        acc_ref = refs[pos] if nk > 1 else None
        k = pl.program_id(2)
        part = lax.dot_general(a_ref[...].astype(BF16), b_ref[...].astype(BF16), _DIMS[mode],
                               preferred_element_type=F32)

        def finish(y):
            if has_bias:
                y = y + bias_ref[...]
            o_ref[...] = y.astype(o_ref.dtype)
            if has_res:
                o2_ref[...] = res_ref[...] + cs_ref[...] * y

        if nk == 1:
            finish(part)
            return

        @pl.when(k == 0)
        def _():
            acc_ref[...] = part

        @pl.when((k > 0) & (k < nk - 1))
        def _():
            acc_ref[...] += part

        @pl.when(k == nk - 1)
        def _():
            finish(acc_ref[...] + part)

    return body


def mm(a, b, mode, out_dtype, name, *, tm=1408, tn=1408, tk=1024, bias=None, res=None, colscale=None, layer=None):
    bshape = b.shape if layer is None else b.shape[1:]
    if mode == "nn":
        (M, K), N = a.shape, bshape[1]
    elif mode == "nt":
        (M, K), N = a.shape, bshape[0]
    else:
        (K, M), N = a.shape, bshape[1]
    tm = _pick(M, tm, LANES if mode == "tn" else 8)
    tn = _pick(N, tn, LANES)
    tk = _pick(K, tk, LANES) if mode != "tn" else _pick(K, tk, 8)
    nk = K // tk
    a_spec = {"nn": pl.BlockSpec((tm, tk), lambda i, j, k: (i, k)),
              "nt": pl.BlockSpec((tm, tk), lambda i, j, k: (i, k)),
              "tn": pl.BlockSpec((tk, tm), lambda i, j, k: (k, i))}[mode]
    b_blk, b_map = {"nn": ((tk, tn), lambda i, j, k: (k, j)),
                    "nt": ((tn, tk), lambda i, j, k: (j, k)),
                    "tn": ((tk, tn), lambda i, j, k: (k, j))}[mode]
    if layer is None:
        b_spec = pl.BlockSpec(b_blk, b_map)
    else:
        b_spec = pl.BlockSpec((None,) + b_blk, lambda i, j, k: (layer,) + b_map(i, j, k))
    o_spec = pl.BlockSpec((tm, tn), lambda i, j, k: (i, j))
    v_spec = pl.BlockSpec((1, tn), lambda i, j, k: (0, j))
    in_specs, args = [a_spec, b_spec], [a, b]
    if bias is not None:
        in_specs.append(v_spec)
        args.append(_row(bias))
    out_shape = [jax.ShapeDtypeStruct((M, N), out_dtype)]
    out_specs = [o_spec]
    if res is not None:
        in_specs += [o_spec, v_spec]
        args += [res, _row(colscale)]
        out_shape.append(jax.ShapeDtypeStruct((M, N), F32))
        out_specs.append(o_spec)
    outs = pl.pallas_call(
        _mm_body(mode, nk, bias is not None, res is not None),
        grid=(M // tm, N // tn, nk),
        in_specs=in_specs, out_specs=out_specs, out_shape=out_shape,
        scratch_shapes=[pltpu.VMEM((tm, tn), F32)] if nk > 1 else [],
        compiler_params=_cparams("parallel", "parallel", "arbitrary"),
        name=name,
    )(*args)
    return outs if res is not None else outs[0]


def gmm(a, w, mode, out_dtype, name, *, bias=None, res=None, colscale=None, tr=512):
    S_ = a.shape[0]
    G = N_GROUPS
    C = a.shape[1] // G
    tr = _pick(S_, tr, 8)
    nr = S_ // tr
    if mode == "tn":
        def body(a_ref, b_ref, o_ref, acc_ref):
            i = pl.program_id(1)

            @pl.when(i == 0)
            def _():
                acc_ref[...] = jnp.zeros_like(acc_ref)

            acc_ref[...] += lax.dot_general(a_ref[...].astype(BF16), b_ref[...].astype(BF16), _DIMS["tn"],
                                            preferred_element_type=F32)

            @pl.when(i == nr - 1)
            def _():
                o_ref[...] = acc_ref[...].astype(o_ref.dtype)

        blk = pl.BlockSpec((tr, C), lambda g, i: (i, g))
        return pl.pallas_call(
            body, grid=(G, nr), in_specs=[blk, blk],
            out_specs=pl.BlockSpec((None, C, C), lambda g, i: (g, 0, 0)),
            out_shape=jax.ShapeDtypeStruct((G, C, C), out_dtype),
            scratch_shapes=[pltpu.VMEM((C, C), F32)],
            compiler_params=_cparams("parallel", "arbitrary"), name=name,
        )(a, w)

    has_bias, has_res = bias is not None, res is not None

    def body(*refs):
        a_ref, w_ref = refs[0], refs[1]
        pos = 2
        if has_bias:
            bias_ref = refs[pos]
            pos += 1
        if has_res:
            res_ref, cs_ref = refs[pos], refs[pos + 1]
            pos += 2
        o_ref = refs[pos]
        y = lax.dot_general(a_ref[...].astype(BF16), w_ref[...].astype(BF16), _DIMS[mode],
                            preferred_element_type=F32)
        if has_bias:
            y = y + bias_ref[...]
        o_ref[...] = y.astype(o_ref.dtype)
        if has_res:
            refs[pos + 1][...] = res_ref[...] + cs_ref[...] * y

    blk = pl.BlockSpec((tr, C), lambda i, g: (i, g))
    vec = pl.BlockSpec((1, C), lambda i, g: (0, g))
    in_specs = [blk, pl.BlockSpec((None, C, C), lambda i, g: (g, 0, 0))]
    args = [a, w]
    if has_bias:
        in_specs.append(vec)
        args.append(_row(bias))
    out_shape = [jax.ShapeDtypeStruct(a.shape, out_dtype)]
    out_specs = [blk]
    if has_res:
        in_specs += [blk, vec]
        args += [res, _row(colscale)]
        out_shape.append(jax.ShapeDtypeStruct(a.shape, F32))
        out_specs.append(blk)
    outs = pl.pallas_call(
        body, grid=(nr, G), in_specs=in_specs, out_specs=out_specs, out_shape=out_shape,
        compiler_params=_cparams("parallel", "parallel"), name=name,
    )(*args)
    return outs if has_res else outs[0]


def norm_fwd(x, g, sc, sh, out_dtype, name, tr=512):
    S_, Dn = x.shape
    tr = _pick(S_, tr, 8)

    def body(x_ref, g_ref, sc_ref, sh_ref, o_ref):
        xv = x_ref[...]
        r = lax.rsqrt(jnp.mean(xv * xv, axis=-1, keepdims=True) + EPS)
        o_ref[...] = (((xv * r) * g_ref[...]) * (1.0 + sc_ref[...]) + sh_ref[...]).astype(o_ref.dtype)

    blk = pl.BlockSpec((tr, Dn), lambda i: (i, 0))
    vec = pl.BlockSpec((1, Dn), lambda i: (0, 0))
    return pl.pallas_call(
        body, grid=(S_ // tr,), in_specs=[blk, vec, vec, vec], out_specs=blk,
        out_shape=jax.ShapeDtypeStruct((S_, Dn), out_dtype),
        compiler_params=_cparams("parallel"), name=name,
    )(x, _row(g), _row(sc), _row(sh))


def norm_bwd(x, g, sc, dh, dres, name, tr=512):
    S_, Dn = x.shape
    tr = _pick(S_, tr, 8)
    has_res = dres is not None

    def body(*refs):
        x_ref, g_ref, sc_ref, dh_ref = refs[:4]
        pos = 4
        if has_res:
            dres_ref = refs[pos]
            pos += 1
        dx_ref, s1_ref, s2_ref = refs[pos:pos + 3]
        i = pl.program_id(0)

        @pl.when(i == 0)
        def _():
            s1_ref[...] = jnp.zeros_like(s1_ref)
            s2_ref[...] = jnp.zeros_like(s2_ref)

        xv = x_ref[...]
        r = lax.rsqrt(jnp.mean(xv * xv, axis=-1, keepdims=True) + EPS)
        n = xv * r
        dhv = dh_ref[...].astype(F32)
        dn = dhv * (g_ref[...] * (1.0 + sc_ref[...]))
        dx = r * (dn - n * jnp.mean(dn * n, axis=-1, keepdims=True))
        if has_res:
            dx = dx + dres_ref[...]
        dx_ref[...] = dx
        s1_ref[...] += jnp.sum(dhv, axis=0, keepdims=True)
        s2_ref[...] += jnp.sum(dhv * n, axis=0, keepdims=True)

    blk = pl.BlockSpec((tr, Dn), lambda i: (i, 0))
    vec = pl.BlockSpec((1, Dn), lambda i: (0, 0))
    in_specs, args = [blk, vec, vec, blk], [x, _row(g), _row(sc), dh]
    if has_res:
        in_specs.append(blk)
        args.append(dres)
    vshape = jax.ShapeDtypeStruct((1, Dn), F32)
    return pl.pallas_call(
        body, grid=(S_ // tr,), in_specs=in_specs, out_specs=[blk, vec, vec],
        out_shape=[jax.ShapeDtypeStruct((S_, Dn), F32), vshape, vshape],
        compiler_params=_cparams("arbitrary"), name=name,
    )(*args)


def gate_bwd(dx, y, colscale, name, tr=512):
    S_, Dn = dx.shape
    tr = _pick(S_, tr, 8)

    def body(dx_ref, y_ref, cs_ref, d_ref, a_ref, c_ref):
        i = pl.program_id(0)

        @pl.when(i == 0)
        def _():
            a_ref[...] = jnp.zeros_like(a_ref)
            c_ref[...] = jnp.zeros_like(c_ref)

        dxv = dx_ref[...]
        d_ref[...] = (dxv * cs_ref[...]).astype(d_ref.dtype)
        a_ref[...] += jnp.sum(dxv * y_ref[...].astype(F32), axis=0, keepdims=True)
        c_ref[...] += jnp.sum(dxv, axis=0, keepdims=True)

    blk = pl.BlockSpec((tr, Dn), lambda i: (i, 0))
    vec = pl.BlockSpec((1, Dn), lambda i: (0, 0))
    vshape = jax.ShapeDtypeStruct((1, Dn), F32)
    return pl.pallas_call(
        body, grid=(S_ // tr,), in_specs=[blk, blk, vec], out_specs=[blk, vec, vec],
        out_shape=[jax.ShapeDtypeStruct((S_, Dn), BF16), vshape, vshape],
        compiler_params=_cparams("arbitrary"), name=name,
    )(dx, y, _row(colscale))


def loss_head(x, g, target, name, tr=512):
    S_, Dn = x.shape
    tr = _pick(S_, tr, 8)

    def body(x_ref, g_ref, t_ref, dx_ref, dg_ref, loss_ref):
        i = pl.program_id(0)

        @pl.when(i == 0)
        def _():
            dg_ref[...] = jnp.zeros_like(dg_ref)
            loss_ref[...] = jnp.zeros_like(loss_ref)

        xv = x_ref[...]
        r = lax.rsqrt(jnp.mean(xv * xv, axis=-1, keepdims=True) + EPS)
        n = xv * r
        e = n * g_ref[...] - t_ref[...]
        loss_ref[...] += 0.5 * jnp.sum(jnp.mean(e * e, axis=-1, keepdims=True), axis=0, keepdims=True)
        dy = e * (1.0 / Dn)
        dg_ref[...] += jnp.sum(dy * n, axis=0, keepdims=True)
        dn = dy * g_ref[...]
        dx_ref[...] = r * (dn - n * jnp.mean(dn * n, axis=-1, keepdims=True))

    blk = pl.BlockSpec((tr, Dn), lambda i: (i, 0))
    vec = pl.BlockSpec((1, Dn), lambda i: (0, 0))
    one = pl.BlockSpec((1, 1), lambda i: (0, 0))
    return pl.pallas_call(
        body, grid=(S_ // tr,), in_specs=[blk, vec, blk], out_specs=[blk, vec, one],
        out_shape=[jax.ShapeDtypeStruct((S_, Dn), F32), jax.ShapeDtypeStruct((1, Dn), F32),
                   jax.ShapeDtypeStruct((1, 1), F32)],
        compiler_params=_cparams("arbitrary"), name=name,
    )(x, _row(g), target)


POOL_HALO = 16
POOL_CHUNK = 512


def _rows(ref, lo, hi, n_rows):
    parts = []
    if lo < 0:
        parts.append(jnp.zeros((-lo, ref.shape[1]), F32))
    parts.append(ref[max(lo, 0):min(hi, n_rows), :].astype(F32))
    if hi > n_rows:
        parts.append(jnp.zeros((hi - n_rows, ref.shape[1]), F32))
    return parts[0] if len(parts) == 1 else jnp.concatenate(parts, axis=0)


def _window_sum(e, w, back):
    n = e.shape[0]
    s, width = e, 1
    while width < w:
        s = s + pltpu.roll(s, width if back else n - width, 0)
        width *= 2
    return s


def _pool_call(h, out_dtype, name, backward):
    S_, Dn = h.shape
    C = Dn // N_GROUPS
    ch = _pick(S_, POOL_CHUNK, 8)

    def body(h_ref, o_ref):
        g = pl.program_id(0)
        for gi, w in enumerate(POOL_WINDOWS):
            @pl.when(g == gi)
            def _(w=w):
                for r0 in range(0, S_, ch):
                    t = (r0 + lax.broadcasted_iota(jnp.int32, (ch, C), 0)).astype(F32)
                    cnt = jnp.minimum(t + 1.0, float(w))
                    if not backward:
                        ext = _rows(h_ref, r0 - POOL_HALO, r0 + ch, S_)
                        cur = ext[POOL_HALO:]
                        mean = _window_sum(ext, w, True)[POOL_HALO:] / cnt
                        o_ref[r0:r0 + ch, :] = (mean - cur).astype(o_ref.dtype)
                    else:
                        ext = _rows(h_ref, r0, r0 + ch + POOL_HALO, S_)
                        text = (r0 + lax.broadcasted_iota(jnp.int32, (ch + POOL_HALO, C), 0)).astype(F32)
                        e = ext / jnp.minimum(text + 1.0, float(w))
                        o_ref[r0:r0 + ch, :] = (_window_sum(e, w, False)[:ch] - ext[:ch]).astype(o_ref.dtype)

    blk = pl.BlockSpec((S_, C), lambda g: (0, g))
    return pl.pallas_call(
        body, grid=(N_GROUPS,), in_specs=[blk], out_specs=blk,
        out_shape=jax.ShapeDtypeStruct((S_, Dn), out_dtype),
        compiler_params=_cparams("parallel"), name=name,
    )(h)


GLU_CHUNK = 512
GLU_HALO = 16
_SQRT_HALF = 0.7071067811865476
_INV_SQRT_2PI = 0.3989422804014327


def _gelu(a):
    return 0.5 * a * (1.0 + lax.erf(a * _SQRT_HALF))


def _gelu_grad(a):
    return 0.5 * (1.0 + lax.erf(a * _SQRT_HALF)) + a * (_INV_SQRT_2PI * jnp.exp(-0.5 * a * a))


def glu_fwd(u, conv_w, conv_b, name):
    S_, F2 = u.shape
    Fh = F2 // 2
    tf = GLU_TILE
    nt = Fh // tf
    ch = _pick(S_, GLU_CHUNK, GLU_HALO)

    def body(a_ref, v_ref, cw_ref, cb_ref, z_ref):
        cw0, cw1, cw2 = cw_ref[0:1, :], cw_ref[1:2, :], cw_ref[2:3, :]
        cb = cb_ref[...]
        for r0 in range(0, S_, ch):
            ext = _rows(a_ref, r0 - GLU_HALO, r0 + ch, S_)
            a0 = ext[GLU_HALO:]
            a1 = pltpu.roll(ext, 1, 0)[GLU_HALO:]
            a2 = pltpu.roll(ext, 2, 0)[GLU_HALO:]
            ac = a2 * cw0 + a1 * cw1 + a0 * cw2 + cb
            z_ref[r0:r0 + ch, :] = (_gelu(ac) * v_ref[r0:r0 + ch, :].astype(F32)).astype(z_ref.dtype)

    return pl.pallas_call(
        body, grid=(nt,),
        in_specs=[pl.BlockSpec((S_, tf), lambda j: (0, j)), pl.BlockSpec((S_, tf), lambda j: (0, j + nt)),
                  pl.BlockSpec((3, tf), lambda j: (0, j)), pl.BlockSpec((1, tf), lambda j: (0, j))],
        out_specs=pl.BlockSpec((S_, tf), lambda j: (0, j)),
        out_shape=jax.ShapeDtypeStruct((S_, Fh), BF16),
        compiler_params=_cparams("parallel"), name=name,
    )(u, u, conv_w, _row(conv_b))


def glu_bwd(u, dz, conv_w, conv_b, name):
    S_, F2 = u.shape
    Fh = F2 // 2
    tf = GLU_TILE
    nt = Fh // tf
    ch = _pick(S_, GLU_CHUNK, GLU_HALO)

    def body(a_ref, v_ref, dz_ref, cw_ref, cb_ref, du_ref, dcw_ref, dcb_ref, da_buf, dv_buf, sems):
        j = pl.program_id(0)
        slot = j % 2

        def writes(step, sl):
            lo = pl.multiple_of(step * tf, tf)
            return (pltpu.make_async_copy(da_buf.at[sl], du_ref.at[:, pl.ds(lo, tf)], sems.at[sl, 0]),
                    pltpu.make_async_copy(dv_buf.at[sl], du_ref.at[:, pl.ds(Fh + lo, tf)], sems.at[sl, 1]))

        @pl.when(j >= 2)
        def _():
            for cp in writes(j - 2, slot):
                cp.wait()

        cw0, cw1, cw2 = cw_ref[0:1, :], cw_ref[1:2, :], cw_ref[2:3, :]
        cb = cb_ref[...]
        acc = [jnp.zeros((1, tf), F32) for _ in range(4)]
        n = ch + GLU_HALO
        for r0 in range(0, S_, ch):
            ext = _rows(a_ref, r0 - GLU_HALO, r0 + n, S_)
            a0 = ext[GLU_HALO:]
            a1 = pltpu.roll(ext, 1, 0)[GLU_HALO:]
            a2 = pltpu.roll(ext, 2, 0)[GLU_HALO:]
            ac = a2 * cw0 + a1 * cw1 + a0 * cw2 + cb
            vv = _rows(v_ref, r0, r0 + n, S_)
            dzv = _rows(dz_ref, r0, r0 + n, S_)
            gl = _gelu(ac)
            dac = dzv * vv * _gelu_grad(ac)
            da = (dac * cw2 + pltpu.roll(dac, n - 1, 0) * cw1 + pltpu.roll(dac, n - 2, 0) * cw0)[:ch]
            da_buf[slot, r0:r0 + ch, :] = da.astype(da_buf.dtype)
            dv_buf[slot, r0:r0 + ch, :] = (dzv[:ch] * gl[:ch]).astype(dv_buf.dtype)
            dc = dac[:ch]
            acc[0] = acc[0] + jnp.sum(dc * a2[:ch], axis=0, keepdims=True)
            acc[1] = acc[1] + jnp.sum(dc * a1[:ch], axis=0, keepdims=True)
            acc[2] = acc[2] + jnp.sum(dc * a0[:ch], axis=0, keepdims=True)
            acc[3] = acc[3] + jnp.sum(dc, axis=0, keepdims=True)
        dcw_ref[0:1, :] = acc[0]
        dcw_ref[1:2, :] = acc[1]
        dcw_ref[2:3, :] = acc[2]
        dcb_ref[...] = acc[3]
        for cp in writes(j, slot):
            cp.start()

        @pl.when(j == nt - 1)
        def _():
            for cp in writes(j, slot):
                cp.wait()
            if nt > 1:
                for cp in writes(j - 1, 1 - slot):
                    cp.wait()

    return pl.pallas_call(
        body, grid=(nt,),
        in_specs=[pl.BlockSpec((S_, tf), lambda j: (0, j)), pl.BlockSpec((S_, tf), lambda j: (0, j + nt)),
                  pl.BlockSpec((S_, tf), lambda j: (0, j)),
                  pl.BlockSpec((3, tf), lambda j: (0, j)), pl.BlockSpec((1, tf), lambda j: (0, j))],
        out_specs=[_ANY, pl.BlockSpec((3, tf), lambda j: (0, j)), pl.BlockSpec((1, tf), lambda j: (0, j))],
        out_shape=[jax.ShapeDtypeStruct((S_, F2), BF16), jax.ShapeDtypeStruct((3, Fh), F32),
                   jax.ShapeDtypeStruct((1, Fh), F32)],
        scratch_shapes=[pltpu.VMEM((2, S_, tf), BF16), pltpu.VMEM((2, S_, tf), BF16), pltpu.SemaphoreType.DMA((2, 2))],
        compiler_params=_cparams("arbitrary"), name=name,
    )(u, u, dz, conv_w, _row(conv_b))


def rope_tables(pos, inv, name, tr=512):
    S_ = pos.shape[0]
    tr = _pick(S_, tr, 8)

    def body(p_ref, inv_ref, c_ref, s1_ref, s2_ref):
        ang = p_ref[...] * inv_ref[...]
        lane = lax.broadcasted_iota(jnp.int32, ang.shape, 1)
        half = QK_ROPE // 2
        cosv, sinv = jnp.cos(ang), jnp.sin(ang)
        c_ref[...] = jnp.where(lane < QK_ROPE, cosv, 0.0)
        s1_ref[...] = jnp.where(lane < half, -sinv, 0.0)
        s2_ref[...] = jnp.where((lane >= half) & (lane < QK_ROPE), sinv, 0.0)

    blk = pl.BlockSpec((tr, LANES), lambda i: (i, 0))
    shp = jax.ShapeDtypeStruct((S_, LANES), F32)
    return pl.pallas_call(
        body, grid=(S_ // tr,),
        in_specs=[pl.BlockSpec((tr, 1), lambda i: (i, 0)), pl.BlockSpec((1, LANES), lambda i: (0, 0))],
        out_specs=[blk, blk, blk], out_shape=[shp, shp, shp],
        compiler_params=_cparams("parallel"), name=name,
    )(pos, inv)


_HALF = QK_ROPE // 2


def _rope(t, c, s1, s2):
    return t * c + pltpu.roll(t, LANES - _HALF, 1) * s1 + pltpu.roll(t, _HALF, 1) * s2


def _rope_t(d, c, s1, s2):
    return d * c + pltpu.roll(d * s1, _HALF, 1) + pltpu.roll(d * s2, LANES - _HALF, 1)


def q_prep(q, tabs, scale, backward, name, tr=512):
    S_, W = q.shape
    tr = _pick(S_, tr, 8)

    def body(q_ref, c_ref, s1_ref, s2_ref, o_ref):
        o_ref[:, 0:LANES] = (q_ref[:, 0:LANES].astype(F32) * scale).astype(o_ref.dtype)
        t = q_ref[:, LANES:2 * LANES].astype(F32)
        fn = _rope_t if backward else _rope
        o_ref[:, LANES:2 * LANES] = (fn(t, c_ref[...], s1_ref[...], s2_ref[...]) * scale).astype(o_ref.dtype)

    blk = pl.BlockSpec((tr, HEAD_PAD), lambda i, h: (i, h))
    tab = pl.BlockSpec((tr, LANES), lambda i, h: (i, 0))
    return pl.pallas_call(
        body, grid=(S_ // tr, W // HEAD_PAD), in_specs=[blk, tab, tab, tab], out_specs=blk,
        out_shape=jax.ShapeDtypeStruct((S_, W), BF16),
        compiler_params=_cparams("parallel", "parallel"), name=name,
    )(q, *tabs)


def k_prep(knv, kv_ext, tabs, name, tr=512):
    S_ = knv.shape[0]
    tr = _pick(S_, tr, 8)

    def body(kn_ref, v_ref, t_ref, c_ref, s1_ref, s2_ref, o_ref, vx_ref):
        o_ref[:, 0:LANES] = kn_ref[...].astype(o_ref.dtype)
        o_ref[:, LANES:2 * LANES] = _rope(t_ref[...], c_ref[...], s1_ref[...], s2_ref[...]).astype(o_ref.dtype)
        vx_ref[:, 0:V_HEAD] = v_ref[...].astype(vx_ref.dtype)
        vx_ref[:, V_HEAD:HEAD_PAD] = jnp.ones((tr, HEAD_PAD - V_HEAD), vx_ref.dtype)

    tab = pl.BlockSpec((tr, LANES), lambda i, h: (i, 0))
    head = pl.BlockSpec((tr, HEAD_PAD), lambda i, h: (i, h))
    shp = jax.ShapeDtypeStruct((S_, N_HEADS * HEAD_PAD), BF16)
    return pl.pallas_call(
        body, grid=(S_ // tr, N_HEADS),
        in_specs=[pl.BlockSpec((tr, LANES), lambda i, h: (i, h)),
                  pl.BlockSpec((tr, V_HEAD), lambda i, h: (i, N_HEADS + h)),
                  pl.BlockSpec((tr, LANES), lambda i, h: (i, KV_RANK // LANES)), tab, tab, tab],
        out_specs=[head, head], out_shape=[shp, shp],
        compiler_params=_cparams("parallel", "parallel"), name=name,
    )(knv, knv, kv_ext, *tabs)


def k_prep_bwd(dk_a, dk_b, dv_a, dv_b, tabs, name, tr=256):
    S_ = dk_a.shape[0]
    tr = _pick(S_, tr, 8)
    HV = N_HEADS * V_HEAD

    def body(ka_ref, kb_ref, va_ref, vb_ref, c_ref, s1_ref, s2_ref, o_ref, t_ref):
        dr = jnp.zeros((tr, LANES), F32)
        for h in range(N_HEADS):
            lo = h * HEAD_PAD
            o_ref[:, h * LANES:(h + 1) * LANES] = (ka_ref[:, lo:lo + LANES] + kb_ref[:, lo:lo + LANES]).astype(o_ref.dtype)
            dr = dr + ka_ref[:, lo + LANES:lo + 2 * LANES] + kb_ref[:, lo + LANES:lo + 2 * LANES]
        o_ref[:, HV:2 * HV] = (va_ref[...] + vb_ref[...]).astype(o_ref.dtype)
        t_ref[...] = _rope_t(dr, c_ref[...], s1_ref[...], s2_ref[...])

    kblk = pl.BlockSpec((tr, N_HEADS * HEAD_PAD), lambda i: (i, 0))
    vblk = pl.BlockSpec((tr, HV), lambda i: (i, 0))
    tab = pl.BlockSpec((tr, LANES), lambda i: (i, 0))
    return pl.pallas_call(
        body, grid=(S_ // tr,), in_specs=[kblk, kblk, vblk, vblk, tab, tab, tab],
        out_specs=[pl.BlockSpec((tr, 2 * HV), lambda i: (i, 0)), tab],
        out_shape=[jax.ShapeDtypeStruct((S_, 2 * HV), BF16), jax.ShapeDtypeStruct((S_, LANES), F32)],
        compiler_params=_cparams("parallel"), name=name,
    )(dk_a, dk_b, dv_a, dv_b, *tabs)


_NEG = -1e30


def attn_fwd(q, k, vx, name):
    S_ = q.shape[0]
    TQ = _pick(S_, ATT_Q_BLOCK, 8)
    TK = _pick(S_, ATT_K_BLOCK, TQ)
    HP = ATT_HEADS_PER_STEP
    W = HP * HEAD_PAD
    ratio = TK // TQ

    def body(q_ref, k_ref, v_ref, o_ref, lse_ref):
        i = pl.program_id(1)
        qs = [q_ref[:, h * HEAD_PAD:(h + 1) * HEAD_PAD] for h in range(HP)]

        def step(j, carry, masked):
            start = pl.multiple_of(j * TK, TK)
            out = []
            for h in range(HP):
                m, acc = carry[h]
                cols = slice(h * HEAD_PAD, (h + 1) * HEAD_PAD)
                s = lax.dot_general(qs[h], k_ref[pl.ds(start, TK), cols], _DIMS["nt"], preferred_element_type=F32)
                if masked:
                    rowi = i * TQ + lax.broadcasted_iota(jnp.int32, (TQ, TK), 0)
                    coli = j * TK + lax.broadcasted_iota(jnp.int32, (TQ, TK), 1)
                    s = jnp.where(coli <= rowi, s, _NEG)
                m_new = jnp.maximum(m, jnp.max(s, axis=-1, keepdims=True))
                alpha = jnp.exp(m - m_new)
                p = jnp.exp(s - m_new).astype(BF16)
                acc = alpha * acc + lax.dot_general(p, v_ref[pl.ds(start, TK), cols], _DIMS["nn"],
                                                    preferred_element_type=F32)
                out.append((m_new, acc))
            return tuple(out)

        init = tuple((jnp.full((TQ, 1), _NEG, F32), jnp.zeros((TQ, HEAD_PAD), F32)) for _ in range(HP))
        last = i // ratio
        carry = step(last, lax.fori_loop(0, last, functools.partial(step, masked=False), init), True)
        for h in range(HP):
            m, acc = carry[h]
            l = acc[:, V_HEAD:]
            o_ref[:, h * V_HEAD:(h + 1) * V_HEAD] = (acc[:, :V_HEAD] / l).astype(o_ref.dtype)
            lse_ref[h] = m + jnp.log(jnp.max(l, axis=-1, keepdims=True))

    return pl.pallas_call(
        body, grid=(N_HEADS // HP, S_ // TQ),
        in_specs=[pl.BlockSpec((TQ, W), lambda g, i: (i, g)),
                  pl.BlockSpec((S_, W), lambda g, i: (0, g)),
                  pl.BlockSpec((S_, W), lambda g, i: (0, g))],
        out_specs=[pl.BlockSpec((TQ, HP * V_HEAD), lambda g, i: (i, g)),
                   pl.BlockSpec((HP, TQ, 1), lambda g, i: (g, i, 0))],
        out_shape=[jax.ShapeDtypeStruct((S_, N_HEADS * V_HEAD), BF16), jax.ShapeDtypeStruct((N_HEADS, S_, 1), F32)],
        compiler_params=_cparams("parallel", "parallel"), name=name,
    )(q, k, vx)


def attn_delta(o, do, name, tr=512):
    S_ = o.shape[0]
    tr = _pick(S_, tr, 8)

    def body(o_ref, do_ref, d_ref):
        d_ref[...] = jnp.sum(o_ref[...].astype(F32) * do_ref[...].astype(F32), axis=-1, keepdims=True)

    blk = pl.BlockSpec((tr, V_HEAD), lambda i, h: (i, h))
    return pl.pallas_call(
        body, grid=(S_ // tr, N_HEADS), in_specs=[blk, blk],
        out_specs=pl.BlockSpec((None, tr, 1), lambda i, h: (h, i, 0)),
        out_shape=jax.ShapeDtypeStruct((N_HEADS, S_, 1), F32),
        compiler_params=_cparams("parallel", "parallel"), name=name,
    )(o, do)


def attn_bwd(q, k, vx, do, lse_row, delta_row, name):
    S_ = q.shape[0]
    TK = _pick(S_, ATT_BWD_K_BLOCK, LANES)
    TQ = _pick(S_, ATT_BWD_Q_BLOCK, TK)
    HP = ATT_HEADS_PER_STEP
    W = HP * HEAD_PAD
    ratio = TQ // TK
    nq = S_ // TQ

    def body(q_ref, do_ref, lse_ref, dl_ref, k_ref, v_ref, dq_ref, dk_ref, dv_ref):
        j = pl.program_id(1)

        @pl.when(j == 0)
        def _():
            dq_ref[...] = jnp.zeros_like(dq_ref)

        ks = [k_ref[:, h * HEAD_PAD:(h + 1) * HEAD_PAD] for h in range(HP)]
        vs = [v_ref[:, h * HEAD_PAD:h * HEAD_PAD + V_HEAD] for h in range(HP)]

        def step(i, carry, masked):
            start = pl.multiple_of(i * TQ, TQ)
            out = []
            for h in range(HP):
                dk, dv = carry[h]
                cols = slice(h * HEAD_PAD, (h + 1) * HEAD_PAD)
                qv = q_ref[pl.ds(start, TQ), cols]
                dov = do_ref[pl.ds(start, TQ), h * V_HEAD:(h + 1) * V_HEAD]
                st = lax.dot_general(ks[h], qv, _DIMS["nt"], preferred_element_type=F32)
                pt = jnp.exp(st - lse_ref[h, :, pl.ds(start, TQ)])
                if masked:
                    keyi = j * TK + lax.broadcasted_iota(jnp.int32, (TK, TQ), 0)
                    qryi = i * TQ + lax.broadcasted_iota(jnp.int32, (TK, TQ), 1)
                    pt = jnp.where(keyi <= qryi, pt, 0.0)
                dpt = lax.dot_general(vs[h], dov, _DIMS["nt"], preferred_element_type=F32)
                dst = (pt * (dpt - dl_ref[h, :, pl.ds(start, TQ)])).astype(BF16)
                dv = dv + lax.dot_general(pt.astype(BF16), dov, _DIMS["nn"], preferred_element_type=F32)
                dk = dk + lax.dot_general(dst, qv, _DIMS["nn"], preferred_element_type=F32)
                dq_ref[pl.ds(start, TQ), cols] += lax.dot_general(dst, ks[h], _DIMS["tn"], preferred_element_type=F32)
                out.append((dk, dv))
            return tuple(out)

        init = tuple((jnp.zeros((TK, HEAD_PAD), F32), jnp.zeros((TK, V_HEAD), F32)) for _ in range(HP))
        first = j // ratio
        carry = lax.fori_loop(first + 1, nq, functools.partial(step, masked=False), step(first, init, True))
        for h in range(HP):
            dk_ref[:, h * HEAD_PAD:(h + 1) * HEAD_PAD] = carry[h][0]
            dv_ref[:, h * V_HEAD:(h + 1) * V_HEAD] = carry[h][1]

    return pl.pallas_call(
        body, grid=(N_HEADS // HP, S_ // TK),
        in_specs=[pl.BlockSpec((S_, W), lambda g, j: (0, g)),
                  pl.BlockSpec((S_, HP * V_HEAD), lambda g, j: (0, g)),
                  pl.BlockSpec((HP, 1, S_), lambda g, j: (g, 0, 0)),
                  pl.BlockSpec((HP, 1, S_), lambda g, j: (g, 0, 0)),
                  pl.BlockSpec((TK, W), lambda g, j: (j, g)),
                  pl.BlockSpec((TK, W), lambda g, j: (j, g))],
        out_specs=[pl.BlockSpec((S_, W), lambda g, j: (0, g)),
                   pl.BlockSpec((TK, W), lambda g, j: (j, g)),
                   pl.BlockSpec((TK, HP * V_HEAD), lambda g, j: (j, g))],
        out_shape=[jax.ShapeDtypeStruct((S_, N_HEADS * HEAD_PAD), F32),
                   jax.ShapeDtypeStruct((S_, N_HEADS * HEAD_PAD), F32),
                   jax.ShapeDtypeStruct((S_, N_HEADS * V_HEAD), F32)],
        compiler_params=_cparams("parallel", "arbitrary"), name=name,
    )(q, do, lse_row, delta_row, k, vx)


def mods_fwd(c_all, mod_w, mod_b, name, tn=512):
    L, Dn, E = mod_w.shape
    R = c_all.shape[0]
    tn = _pick(E, tn, LANES)

    def body(c_ref, w_ref, b_ref, o_ref):
        cv = c_ref[...]
        sc = (cv / (1.0 + jnp.exp(-cv))).astype(BF16)
        o_ref[...] = lax.dot_general(sc, w_ref[...].astype(BF16), _DIMS["nn"], preferred_element_type=F32) + b_ref[...]

    return pl.pallas_call(
        body, grid=(L, E // tn),
        in_specs=[pl.BlockSpec((R, Dn), lambda l, j: (0, 0)), pl.BlockSpec((None, Dn, tn), lambda l, j: (l, 0, j)),
                  pl.BlockSpec((None, 1, tn), lambda l, j: (l, 0, j))],
        out_specs=pl.BlockSpec((None, R, tn), lambda l, j: (l, 0, j)),
        out_shape=jax.ShapeDtypeStruct((L, R, E), F32),
        compiler_params=_cparams("parallel", "parallel"), name=name,
    )(c_all, mod_w, mod_b.reshape(L, 1, E))


def _adam_math(w, g, m, v):
    m = ADAM_B1 * m + (1.0 - ADAM_B1) * g
    v = ADAM_B2 * v + (1.0 - ADAM_B2) * (g * g)
    m_hat = m / (1.0 - ADAM_B1 ** ADAM_STEP)
    v_hat = v / (1.0 - ADAM_B2 ** ADAM_STEP)
    delta = -ADAM_LR * (m_hat / (jnp.sqrt(v_hat) + ADAM_EPS) + ADAM_WD * w)
    return delta, m, v


def _as2d(a):
    return a.reshape(-1, a.shape[-1]) if a.ndim != 2 else a


def adamw(w, g, m, v, name):
    shape = w.shape
    w2, g2, m2, v2 = _as2d(w), _as2d(g), _as2d(m), _as2d(v)
    R, C = w2.shape
    tr = _pick(R, max(8, (1 << 18) // C // 8 * 8), 8)

    def body(w_ref, g_ref, m_ref, v_ref, d_ref, mo_ref, vo_ref):
        d, mn, vn = _adam_math(w_ref[...], g_ref[...], m_ref[...], v_ref[...])
        d_ref[...] = d
        mo_ref[...] = mn
        vo_ref[...] = vn

    blk = pl.BlockSpec((tr, C), lambda i: (i, 0))
    shp = jax.ShapeDtypeStruct((R, C), F32)
    outs = pl.pallas_call(
        body, grid=(R // tr,), in_specs=[blk] * 4, out_specs=[blk] * 3, out_shape=[shp] * 3,
        compiler_params=_cparams("parallel"), name=name,
    )(w2, g2, m2, v2)
    return tuple(o.reshape(shape) for o in outs)


def adamw_sum(parts, w, m, v, name):
    P, R, C = parts.shape

    def body(p_ref, w_ref, m_ref, v_ref, g_ref, d_ref, mo_ref, vo_ref):
        g = p_ref[0]
        for k in range(1, P):
            g = g + p_ref[k]
        d, mn, vn = _adam_math(w_ref[...], g, m_ref[...], v_ref[...])
        g_ref[...] = g
        d_ref[...] = d
        mo_ref[...] = mn
        vo_ref[...] = vn

    shp = jax.ShapeDtypeStruct((R, C), F32)
    return pl.pallas_call(body, out_shape=[shp] * 4, compiler_params=_cparams(), name=name)(parts, w, m, v)


def adamw_modw(c_col, dm, w, m, v, name, tr=256, tn=512):
    L, Dn, E = w.shape
    B = c_col.shape[0]
    tr = _pick(Dn, tr, 8)
    tn = _pick(E, tn, LANES)

    def body(c_ref, dm_ref, w_ref, m_ref, v_ref, g_ref, d_ref, mo_ref, vo_ref):
        g = jnp.zeros((tr, tn), F32)
        for b in range(B):
            cv = c_ref[b]
            g = g + (cv / (1.0 + jnp.exp(-cv))) * dm_ref[b:b + 1, :]
        d, mn, vn = _adam_math(w_ref[...], g, m_ref[...], v_ref[...])
        g_ref[...] = g
        d_ref[...] = d
        mo_ref[...] = mn
        vo_ref[...] = vn

    blk = pl.BlockSpec((None, tr, tn), lambda l, i, j: (l, i, j))
    shp = jax.ShapeDtypeStruct((L, Dn, E), F32)
    return pl.pallas_call(
        body, grid=(L, Dn // tr, E // tn),
        in_specs=[pl.BlockSpec((B, tr, 1), lambda l, i, j: (0, i, 0)),
                  pl.BlockSpec((None, B, tn), lambda l, i, j: (l, 0, j)), blk, blk, blk],
        out_specs=[blk] * 4, out_shape=[shp] * 4,
        compiler_params=_cparams("parallel", "parallel", "parallel"), name=name,
    )(c_col, dm, w, m, v)


def add_round(a, b, name, tr=512):
    R, C = a.shape
    tr = _pick(R, tr, 16)

    def body(a_ref, b_ref, o_ref):
        o_ref[...] = (a_ref[...] + b_ref[...].astype(F32)).astype(BF16)

    blk = pl.BlockSpec((tr, C), lambda i: (i, 0))
    return pl.pallas_call(
        body, grid=(R // tr,), in_specs=[blk, blk], out_specs=blk, out_shape=jax.ShapeDtypeStruct((R, C), BF16),
        compiler_params=_cparams("parallel"), name=name,
    )(a, b)


def sum_parts(parts, name, tr=512):
    P, R, C = parts.shape
    tr = _pick(R, tr, 16)

    def body(p_ref, o_ref):
        s = p_ref[0].astype(F32)
        for k in range(1, P):
            s = s + p_ref[k].astype(F32)
        o_ref[...] = s

    return pl.pallas_call(
        body, grid=(R // tr,), in_specs=[pl.BlockSpec((P, tr, C), lambda i: (0, i, 0))],
        out_specs=pl.BlockSpec((tr, C), lambda i: (i, 0)), out_shape=jax.ShapeDtypeStruct((R, C), F32),
        compiler_params=_cparams("parallel"), name=name,
    )(parts)


_ANY = pl.BlockSpec(memory_space=pl.ANY)


def _place():
    return lax.axis_index("x"), lax.axis_index("y"), lax.axis_index("c")


def _flip(v, bit):
    return 1 - v if bit else v


def chip_gather(buf, name):
    def body(in_ref, out_ref, send_sems, recv_sems):
        x, y, c = _place()
        me = 2 * x + y
        sends = []
        for k in range(1, N_CHIPS):
            px, py = _flip(x, k >> 1), _flip(y, k & 1)
            cp = pltpu.make_async_remote_copy(src_ref=in_ref, dst_ref=out_ref.at[me], send_sem=send_sems.at[k - 1],
                                              recv_sem=recv_sems.at[k - 1], device_id=(px, py, c), device_id_type=MESH)
            cp.start()
            sends.append(cp)
        for k in range(1, N_CHIPS):
            px, py = _flip(x, k >> 1), _flip(y, k & 1)
            pltpu.make_async_remote_copy(src_ref=in_ref, dst_ref=out_ref.at[2 * px + py], send_sem=send_sems.at[k - 1],
                                         recv_sem=recv_sems.at[k - 1], device_id=(px, py, c),
                                         device_id_type=MESH).wait_recv()
        for cp in sends:
            cp.wait_send()

    out = pl.pallas_call(
        body, in_specs=[_ANY], out_specs=_ANY,
        out_shape=jax.ShapeDtypeStruct((N_CHIPS,) + buf.shape, buf.dtype),
        scratch_shapes=[pltpu.SemaphoreType.DMA((N_CHIPS - 1,)), pltpu.SemaphoreType.DMA((N_CHIPS - 1,))],
        name=name,
    )(buf)
    return lax.dynamic_update_index_in_dim(out, buf, 2 * lax.axis_index("x") + lax.axis_index("y"), 0)


def chip_all_to_all(buf, name):
    def body(in_ref, out_ref, send_sems, recv_sems):
        x, y, c = _place()
        me = 2 * x + y
        sends = []
        for k in range(1, N_CHIPS):
            px, py = _flip(x, k >> 1), _flip(y, k & 1)
            cp = pltpu.make_async_remote_copy(src_ref=in_ref.at[2 * px + py], dst_ref=out_ref.at[me],
                                              send_sem=send_sems.at[k - 1], recv_sem=recv_sems.at[k - 1],
                                              device_id=(px, py, c), device_id_type=MESH)
            cp.start()
            sends.append(cp)
        for k in range(1, N_CHIPS):
            px, py = _flip(x, k >> 1), _flip(y, k & 1)
            pltpu.make_async_remote_copy(src_ref=in_ref.at[me], dst_ref=out_ref.at[2 * px + py],
                                         send_sem=send_sems.at[k - 1], recv_sem=recv_sems.at[k - 1],
                                         device_id=(px, py, c), device_id_type=MESH).wait_recv()
        for cp in sends:
            cp.wait_send()

    out = pl.pallas_call(
        body, in_specs=[_ANY], out_specs=_ANY, out_shape=jax.ShapeDtypeStruct(buf.shape, buf.dtype),
        scratch_shapes=[pltpu.SemaphoreType.DMA((N_CHIPS - 1,)), pltpu.SemaphoreType.DMA((N_CHIPS - 1,))],
        name=name,
    )(buf)
    me = 2 * lax.axis_index("x") + lax.axis_index("y")
    return lax.dynamic_update_index_in_dim(out, _index(buf, me), me, 0)


def core_gather(buf, name):
    def body(in_ref, out_ref, send_sem, recv_sem):
        x, y, c = _place()
        cp = pltpu.make_async_remote_copy(src_ref=in_ref, dst_ref=out_ref.at[c], send_sem=send_sem, recv_sem=recv_sem,
                                          device_id=(x, y, 1 - c), device_id_type=MESH)
        cp.start()
        pltpu.make_async_remote_copy(src_ref=in_ref, dst_ref=out_ref.at[1 - c], send_sem=send_sem, recv_sem=recv_sem,
                                     device_id=(x, y, 1 - c), device_id_type=MESH).wait_recv()
        cp.wait_send()

    out = pl.pallas_call(
        body, in_specs=[_ANY], out_specs=_ANY, out_shape=jax.ShapeDtypeStruct((2,) + buf.shape, buf.dtype),
        scratch_shapes=[pltpu.SemaphoreType.DMA, pltpu.SemaphoreType.DMA],
        name=name,
    )(buf)
    return lax.dynamic_update_index_in_dim(out, buf, lax.axis_index("c"), 0)


def core_swap(buf, name):
    def body(in_ref, out_ref, send_sem, recv_sem):
        x, y, c = _place()
        cp = pltpu.make_async_remote_copy(src_ref=in_ref, dst_ref=out_ref, send_sem=send_sem, recv_sem=recv_sem,
                                          device_id=(x, y, 1 - c), device_id_type=MESH)
        cp.start()
        cp.wait()

    return pl.pallas_call(
        body, in_specs=[_ANY], out_specs=_ANY, out_shape=jax.ShapeDtypeStruct(buf.shape, buf.dtype),
        scratch_shapes=[pltpu.SemaphoreType.DMA, pltpu.SemaphoreType.DMA],
        name=name,
    )(buf)


def device_gather(buf, name):
    def body(in_ref, out_ref, send_sems, recv_sems, local_sem):
        x, y, c = _place()
        me = 4 * x + 2 * y + c
        mine = pltpu.make_async_copy(in_ref, out_ref.at[me], local_sem)
        mine.start()
        sends = []
        for k in range(1, N_DEV):
            peer = (_flip(x, (k >> 2) & 1), _flip(y, (k >> 1) & 1), _flip(c, k & 1))
            cp = pltpu.make_async_remote_copy(src_ref=in_ref, dst_ref=out_ref.at[me], send_sem=send_sems.at[k - 1],
                                              recv_sem=recv_sems.at[k - 1], device_id=peer, device_id_type=MESH)
            cp.start()
            sends.append(cp)
        for k in range(1, N_DEV):
            peer = (_flip(x, (k >> 2) & 1), _flip(y, (k >> 1) & 1), _flip(c, k & 1))
            pltpu.make_async_remote_copy(src_ref=in_ref, dst_ref=out_ref.at[4 * peer[0] + 2 * peer[1] + peer[2]],
                                         send_sem=send_sems.at[k - 1], recv_sem=recv_sems.at[k - 1], device_id=peer,
                                         device_id_type=MESH).wait_recv()
        for cp in sends:
            cp.wait_send()
        mine.wait()

    return pl.pallas_call(
        body, in_specs=[_ANY], out_specs=_ANY, out_shape=jax.ShapeDtypeStruct((N_DEV,) + buf.shape, buf.dtype),
        scratch_shapes=[pltpu.SemaphoreType.DMA((N_DEV - 1,)), pltpu.SemaphoreType.DMA((N_DEV - 1,)),
                        pltpu.SemaphoreType.DMA],
        name=name,
    )(buf)


WEIGHT_ORDER = ["mod_w", "mod_b", "norm1_g", "norm2_g", "pool_w", "pool_b", "pool_scale", "kv_in_g", "w_dkv",
                "ckv_norm_g", "w_uk", "w_uv", "w_dq", "q_norm_g", "w_uq", "w_o", "w_up", "conv_w", "conv_b", "w_down",
                "final_g"]
EXCHANGED = {"w_up": 2, "w_down": 1, "w_o": 1, "w_uq": 2, "w_dq": 1, "pool_w": 2, "w_dkv": 0, "w_uk": 1, "w_uv": 1,
             "conv_w": 2, "pool_b": 1, "pool_scale": 1}
KEPT_F32 = ("conv_w", "pool_b", "pool_scale")
REPLICATED = ["mod_b", "norm1_g", "norm2_g", "kv_in_g", "ckv_norm_g", "q_norm_g", "conv_b", "final_g"]
PACK_ALIGN = 2 * 16 * PACK_COLS


def _padded(n, align):
    return -(-n // align) * align


def _flat_pad(parts, total):
    flat = jnp.concatenate(parts, axis=-1)
    pad = total - flat.shape[-1]
    if pad:
        flat = jnp.concatenate([flat, jnp.zeros(flat.shape[:-1] + (pad,), flat.dtype)], axis=-1)
    return flat


def _split_shards(full, axis):
    shp = full.shape
    t = full.reshape(shp[:axis] + (N_CHIPS, shp[axis] // N_CHIPS) + shp[axis + 1:])
    return jnp.moveaxis(t, axis, 0).reshape(N_CHIPS, -1)


def _join_shards(rows, shard_shape, axis):
    t = jnp.moveaxis(rows.reshape((N_CHIPS,) + tuple(shard_shape)), 0, axis)
    return t.reshape(tuple(shard_shape[:axis]) + (N_CHIPS * shard_shape[axis],) + tuple(shard_shape[axis + 1:]))


def _index(a, i, axis=0):
    return lax.dynamic_index_in_dim(a, i, axis, keepdims=False)


def kernel(x, c, positions, mod_w, mod_b, norm1_g, norm2_g, pool_w, pool_b, pool_scale, kv_in_g, w_dkv, ckv_norm_g, w_uk, w_uv, w_dq, q_norm_g, w_uq, w_o, w_up, conv_w, conv_b, w_down, final_g, loss_target, m_mod_w, m_mod_b, m_norm1_g, m_norm2_g, m_pool_w, m_pool_b, m_pool_scale, m_kv_in_g, m_w_dkv, m_ckv_norm_g, m_w_uk, m_w_uv, m_w_dq, m_q_norm_g, m_w_uq, m_w_o, m_w_up, m_conv_w, m_conv_b, m_w_down, m_final_g, v_mod_w, v_mod_b, v_norm1_g, v_norm2_g, v_pool_w, v_pool_b, v_pool_scale, v_kv_in_g, v_w_dkv, v_ckv_norm_g, v_w_uk, v_w_uv, v_w_dq, v_q_norm_g, v_w_uq, v_w_o, v_w_up, v_conv_w, v_conv_b, v_w_down, v_final_g):
    given = dict(locals())
    W = {n: given[n] for n in WEIGHT_ORDER}
    M1 = {n: given["m_" + n] for n in WEIGHT_ORDER}
    V2 = {n: given["v_" + n] for n in WEIGHT_ORDER}
    xi, yi, ci = lax.axis_index("x"), lax.axis_index("y"), lax.axis_index("c")
    chip = 2 * xi + yi
    dev = 4 * xi + 2 * yi + ci
    x0 = x[0]
    S_, D = x0.shape
    Fh = conv_b.shape[1]
    E = mod_b.shape[1]
    Es = E // N_CHIPS
    zD = jnp.zeros((D,), F32)

    sizes = {n: math.prod(W[n].shape) * (2 if n in KEPT_F32 else 1) for n in EXCHANGED}
    T = _padded(sum(sizes.values()), PACK_ALIGN)
    R = T // PACK_COLS
    own = _flat_pad([lax.bitcast_convert_type(W[n], BF16).reshape(-1) if n in KEPT_F32 else W[n].astype(BF16).reshape(-1)
                     for n in EXCHANGED], T)
    own_half = _index(own.reshape(2, R // 2, PACK_COLS), ci)
    halves = core_gather(chip_gather(own_half, "gather_w_chips"), "gather_w_cores")
    rows = jnp.swapaxes(halves, 0, 1).reshape(N_CHIPS, T)
    full = {}
    off = 0
    for n, axis in EXCHANGED.items():
        seg = rows[:, off:off + sizes[n]]
        off += sizes[n]
        if n in KEPT_F32:
            seg = lax.bitcast_convert_type(seg.reshape(N_CHIPS, -1, 2), F32)
        full[n] = _join_shards(seg, W[n].shape, axis)

    n_mla = DEPTH - N_A
    q_rank = full["w_uq"].shape[1]
    wq = full["w_uq"].reshape(n_mla, q_rank, N_HEADS, QK_HEAD)
    w_uq_ext = jnp.concatenate([wq, jnp.zeros((n_mla, q_rank, N_HEADS, HEAD_PAD - QK_HEAD), BF16)],
                               axis=3).reshape(n_mla, q_rank, N_HEADS * HEAD_PAD)
    kv_w = KV_RANK + QK_ROPE
    w_dkv_ext = jnp.concatenate([full["w_dkv"], jnp.zeros((D, KV_RANK + LANES - kv_w), BF16)], axis=1)
    w_ukv = jnp.concatenate([full["w_uk"], full["w_uv"]], axis=1)

    c_all = device_gather(c, "gather_c").reshape(N_DEV, D)
    c_pad = jnp.concatenate([c_all, jnp.zeros((16 - N_DEV, D), F32)], axis=0)
    mod_b_mine = lax.dynamic_slice_in_dim(mod_b, chip * Es, Es, axis=1)
    mods_part = mods_fwd(c_pad, mod_w, mod_b_mine, "mods_fwd")
    mods_all = chip_gather(mods_part, "gather_mods")
    mods = jnp.swapaxes(_index(mods_all, dev, axis=2), 0, 1).reshape(DEPTH, E)
    mod = [[mods[l, k * D:(k + 1) * D] for k in range(6)] for l in range(DEPTH)]

    half = QK_ROPE // 2
    inv = 1.0 / (ROPE_THETA ** (jnp.arange(0, QK_ROPE, 2, dtype=F32) / QK_ROPE))
    inv_row = jnp.concatenate([inv, inv, jnp.zeros((LANES - 2 * half,), F32)]).reshape(1, LANES)
    tabs = rope_tables(positions[0].astype(F32).reshape(S_, 1), inv_row, "rope_tables")
    att_scale = QK_HEAD ** -0.5

    saved = []
    xcur = x0
    kv_saved = None
    K = VX = knv = None
    for l in range(DEPTH):
        sh1, sc1, g1, sh2, sc2, g2 = mod[l]
        st = {"xin": xcur}
        if l < N_A:
            h1 = norm_fwd(xcur, norm1_g[l], sc1, sh1, F32, f"norm1_fwd{l}")
            st["pooled"] = _pool_call(h1, BF16, f"pool_fwd{l}", False)
            st["cs"] = g1 * full["pool_scale"][l]
            st["ypre"], xmid = gmm(st["pooled"], full["pool_w"][l], "nn", F32, f"pool_mm{l}", bias=full["pool_b"][l],
                                   res=xcur, colscale=st["cs"])
        else:
            j = l - N_A
            st["h1"] = norm_fwd(xcur, norm1_g[l], sc1, sh1, BF16, f"norm1_fwd{l}")
            st["ql"] = mm(st["h1"], full["w_dq"], "nn", F32, f"dq_mm{l}", layer=j)
            st["cq"] = norm_fwd(st["ql"], q_norm_g[j], jnp.zeros_like(q_norm_g[j]), jnp.zeros_like(q_norm_g[j]), BF16,
                                f"qnorm_fwd{l}")
            qe = mm(st["cq"], w_uq_ext, "nn", F32, f"uq_mm{l}", layer=j)
            st["Q"] = q_prep(qe, tabs, att_scale, False, f"q_prep{l}")
            st["o"], lse = attn_fwd(st["Q"], K, VX, f"attn_fwd{l}")
            st["lse"] = lse.reshape(N_HEADS, 1, S_)
            st["y"], xmid = mm(st["o"], full["w_o"], "nn", F32, f"wo_mm{l}", res=xcur, colscale=g1, layer=j)
        st["xmid"] = xmid
        st["h2"] = norm_fwd(xmid, norm2_g[l], sc2, sh2, BF16, f"norm2_fwd{l}")
        st["u"] = mm(st["h2"], full["w_up"], "nn", BF16, f"up_mm{l}", layer=l)
        st["z"] = glu_fwd(st["u"], full["conv_w"][l], conv_b[l], f"glu_fwd{l}")
        st["f"], xcur = mm(st["z"], full["w_down"], "nn", F32, f"down_mm{l}", tk=1408, res=xmid, colscale=g2, layer=l)
        saved.append(st)
        if l == N_A - 1:
            xn = norm_fwd(xcur, kv_in_g, zD, zD, BF16, "kvin_fwd")
            kv_ext = mm(xn, w_dkv_ext, "nn", F32, "dkv_mm")
            lat = kv_ext[:, :KV_RANK]
            zk = jnp.zeros((KV_RANK,), F32)
            ckv = norm_fwd(lat, ckv_norm_g, zk, zk, BF16, "ckv_fwd")
            knv = mm(ckv, w_ukv, "nn", BF16, "ukv_mm")
            K, VX = k_prep(knv, kv_ext, tabs, "k_prep")
            kv_saved = {"x": xcur, "xn": xn, "lat": lat, "ckv": ckv}

    dx, d_final_g, loss_part = loss_head(xcur, final_g, loss_target[0], "loss_head")
    loss = lax.psum(loss_part[0, 0], ("x", "y", "c"))

    G = {}
    dmods = [None] * DEPTH
    d_norm1 = [None] * DEPTH
    d_norm2 = [None] * DEPTH
    d_conv_b = [None] * DEPTH
    d_qnorm = [None] * n_mla
    dkv_acc = []
    for l in reversed(range(DEPTH)):
        sh1, sc1, g1, sh2, sc2, g2 = mod[l]
        st = saved[l]
        df, a2, _ = gate_bwd(dx, st["f"], g2, f"gate2_bwd{l}")
        dz = mm(df, full["w_down"], "nt", BF16, f"down_dx{l}", layer=l)
        G[("w_down", l)] = mm(st["z"], df, "tn", F32, f"down_dw{l}")
        du, dcw, dcb = glu_bwd(st["u"], dz, full["conv_w"][l], conv_b[l], f"glu_bwd{l}")
        G[("conv_w", l)] = dcw
        d_conv_b[l] = dcb[0]
        dh2 = mm(du, full["w_up"], "nt", BF16, f"up_dx{l}", tk=1408, layer=l)
        G[("w_up", l)] = mm(st["h2"], du, "tn", F32, f"up_dw{l}")
        dxmid, s1, s2 = norm_bwd(st["xmid"], norm2_g[l], sc2, dh2, dx, f"norm2_bwd{l}")
        dsh2, dsc2, d_norm2[l] = s1[0], s2[0] * norm2_g[l], s2[0] * (1.0 + sc2)
        if l < N_A:
            dyp, a1, csum = gate_bwd(dxmid, st["ypre"], st["cs"], f"gate1_bwd{l}")
            dg1 = full["pool_scale"][l] * a1[0]
            G[("pool_scale", l)] = g1 * a1[0]
            G[("pool_b", l)] = st["cs"] * csum[0]
            dpooled = gmm(dyp, full["pool_w"][l], "nt", F32, f"pool_dx{l}")
            G[("pool_w", l)] = gmm(st["pooled"], dyp, "tn", F32, f"pool_dw{l}")
            dh1 = _pool_call(dpooled, F32, f"pool_bwd{l}", True)
        else:
            j = l - N_A
            dy, a1, _ = gate_bwd(dxmid, st["y"], g1, f"gate1_bwd{l}")
            dg1 = a1[0]
            do = mm(dy, full["w_o"], "nt", BF16, f"wo_dx{l}", layer=j)
            G[("w_o", j)] = mm(st["o"], dy, "tn", F32, f"wo_dw{l}")
            delta = attn_delta(st["o"], do, f"attn_delta{l}").reshape(N_HEADS, 1, S_)
            dQ, dK, dV = attn_bwd(st["Q"], K, VX, do, st["lse"], delta, f"attn_bwd{l}")
            dkv_acc.append((dK, dV))
            dqe = q_prep(dQ, tabs, att_scale, True, f"q_prep_bwd{l}")
            dcq = mm(dqe, w_uq_ext, "nt", F32, f"uq_dx{l}", layer=j)
            G[("w_uq", j)] = mm(st["cq"], dqe, "tn", F32, f"uq_dw{l}").reshape(q_rank, N_HEADS, HEAD_PAD)[
                :, :, :QK_HEAD].reshape(q_rank, N_HEADS * QK_HEAD)
            zq = jnp.zeros_like(q_norm_g[j])
            dql, _, s2q = norm_bwd(st["ql"], q_norm_g[j], zq, dcq, None, f"qnorm_bwd{l}")
            d_qnorm[j] = s2q[0]
            dh1 = mm(dql, full["w_dq"], "nt", BF16, f"dq_dx{l}", layer=j)
            G[("w_dq", j)] = mm(st["h1"], dql, "tn", F32, f"dq_dw{l}")
        dx, s1, s2 = norm_bwd(st["xin"], norm1_g[l], sc1, dh1, dxmid, f"norm1_bwd{l}")
        dsh1, dsc1, d_norm1[l] = s1[0], s2[0] * norm1_g[l], s2[0] * (1.0 + sc1)
        dmods[l] = jnp.concatenate([dsh1, dsc1, dg1, dsh2, dsc2, a2[0]])
        if l == N_A:
            (dk_a, dv_a), (dk_b, dv_b) = dkv_acc
            dknv, d_tk = k_prep_bwd(dk_a, dk_b, dv_a, dv_b, tabs, "k_prep_bwd")
            dckv = mm(dknv, w_ukv, "nt", F32, "ukv_dx")
            d_ukv = mm(kv_saved["ckv"], dknv, "tn", F32, "ukv_dw")
            G[("w_uk", 0)], G[("w_uv", 0)] = d_ukv[:, :N_HEADS * QK_NOPE], d_ukv[:, N_HEADS * QK_NOPE:]
            zk = jnp.zeros((KV_RANK,), F32)
            dlat, _, s2c = norm_bwd(kv_saved["lat"], ckv_norm_g, zk, dckv, None, "ckv_bwd")
            d_ckv_g = s2c[0]
            dkv_ext = jnp.concatenate([dlat, d_tk], axis=1)
            dxn = mm(dkv_ext, w_dkv_ext, "nt", BF16, "dkv_dx")
            G[("w_dkv", 0)] = mm(kv_saved["xn"], dkv_ext, "tn", F32, "dkv_dw")[:, :kv_w]
            dx, _, s2k = norm_bwd(kv_saved["x"], kv_in_g, zD, dxn, dx, "kvin_bwd")
            d_kvin_g = s2k[0]

    def stacked(n):
        k = W[n].shape[0] if W[n].ndim > 2 or n in ("pool_b", "pool_scale") else None
        return G[(n, 0)] if k is None else jnp.stack([G[(n, i)] for i in range(k)])

    gsizes = {n: math.prod(W[n].shape) for n in EXCHANGED}
    Tg = _padded(sum(gsizes.values()), PACK_ALIGN)
    Rg = Tg // PACK_COLS
    gflat = _flat_pad([_split_shards(stacked(n), axis) for n, axis in EXCHANGED.items()], Tg)
    gflat = gflat.reshape(N_CHIPS, 2, Rg // 2, PACK_COLS)
    keep = _index(gflat, ci, axis=1).reshape(N_CHIPS * Rg // 2, PACK_COLS)
    give = _index(gflat, 1 - ci, axis=1).astype(BF16).reshape(N_CHIPS * Rg // 2, PACK_COLS)
    chip_sum = add_round(keep, core_swap(give, "reduce_cores"), "reduce_cores_add")
    got = chip_all_to_all(chip_sum.reshape(N_CHIPS, Rg // 2, PACK_COLS), "reduce_chips")
    red = core_gather(sum_parts(got, "reduce_chips_add"), "reduce_gather").reshape(Tg)

    grads, deltas, new_m, new_v = {}, {}, {}, {}
    off = 0
    for n in EXCHANGED:
        grads[n] = red[off:off + gsizes[n]].reshape(W[n].shape)
        off += gsizes[n]
        deltas[n], new_m[n], new_v[n] = adamw(W[n], grads[n], M1[n], V2[n], f"adamw_{n}")

    small = {"mod_b": jnp.stack(dmods), "norm1_g": jnp.stack(d_norm1), "norm2_g": jnp.stack(d_norm2),
             "kv_in_g": d_kvin_g, "ckv_norm_g": d_ckv_g, "q_norm_g": jnp.stack(d_qnorm),
             "conv_b": jnp.stack(d_conv_b), "final_g": d_final_g[0]}
    ssizes = {n: math.prod(W[n].shape) for n in REPLICATED}
    Ts = _padded(sum(ssizes.values()), 8 * PACK_COLS)

    def pack_small(d):
        return _flat_pad([d[n].reshape(-1) for n in REPLICATED], Ts).reshape(Ts // PACK_COLS, PACK_COLS)

    parts = device_gather(pack_small(small), "gather_small")
    outs = adamw_sum(parts, pack_small(W), pack_small(M1), pack_small(V2), "adamw_small")
    off = 0
    for n in REPLICATED:
        for dst, o in zip((grads, deltas, new_m, new_v), outs):
            dst[n] = o.reshape(-1)[off:off + ssizes[n]].reshape(W[n].shape)
        off += ssizes[n]

    dm_all = parts.reshape(N_DEV, -1)[:, :DEPTH * E].reshape(N_DEV, DEPTH, E)
    dm_mine = jnp.swapaxes(lax.dynamic_slice_in_dim(dm_all, chip * Es, Es, axis=2), 0, 1)
    grads["mod_w"], deltas["mod_w"], new_m["mod_w"], new_v["mod_w"] = adamw_modw(
        c_all.reshape(N_DEV, D, 1), dm_mine, mod_w, m_mod_w, v_mod_w, "adamw_mod_w")

    return (loss, dx.reshape(x.shape), *[grads[n] for n in WEIGHT_ORDER], *[deltas[n] for n in WEIGHT_ORDER],
            *[new_m[n] for n in WEIGHT_ORDER], *[new_v[n] for n in WEIGHT_ORDER])
```

```python
import functools
import math

import jax
import jax.numpy as jnp
from jax import lax
from jax.experimental import pallas as pl
from jax.experimental.pallas import tpu as pltpu

F32 = jnp.float32
BF16 = jnp.bfloat16
MESH = pl.DeviceIdType.MESH

DEPTH = 4
N_A = 2
POOL_WINDOWS = (2, 4, 8, 16)
N_GROUPS = 4
N_HEADS = 8
QK_NOPE = 128
QK_ROPE = 64
V_HEAD = 128
QK_HEAD = QK_NOPE + QK_ROPE
HEAD_PAD = 256
KV_RANK = 256
ROPE_THETA = 10000.0
EPS = 1e-6
ADAM_LR = 0.001
ADAM_B1 = 0.9
ADAM_B2 = 0.999
ADAM_EPS = 1e-08
ADAM_WD = 0.01
ADAM_STEP = 10

N_CHIPS = 4
N_DEV = 8
LANES = 128
PACK_COLS = 1024
VMEM_LIMIT = 56 * 1024 * 1024
GLU_TILE = 256
ATT_BWD_K_BLOCK = 256
ATT_BWD_Q_BLOCK = 512
ATT_Q_BLOCK = 256
ATT_K_BLOCK = 512
ATT_HEADS_PER_STEP = 2


def _cparams(*sem):
    return pltpu.CompilerParams(dimension_semantics=sem if sem else None, vmem_limit_bytes=VMEM_LIMIT)


def _pick(n, target, mult):
    best = None
    d = mult
    while d <= min(n, target):
        if n % d == 0:
            best = d
        d += mult
    return n if best is None else best


def _row(v):
    return v.reshape(1, -1).astype(F32)


_DIMS = {"nn": (((1,), (0,)), ((), ())), "nt": (((1,), (1,)), ((), ())), "tn": (((0,), (0,)), ((), ()))}


def _mm_body(mode, nk, has_bias, has_res):
    def body(*refs):
        a_ref, b_ref = refs[0], refs[1]
        pos = 2
        bias_ref = res_ref = cs_ref = None
        if has_bias:
            bias_ref = refs[pos]
            pos += 1
        if has_res:
            res_ref, cs_ref = refs[pos], refs[pos + 1]
            pos += 2
        o_ref = refs[pos]
        pos += 1
        o2_ref = None
        if has_res:
            o2_ref = refs[pos]
            pos += 1
        acc_ref = refs[pos] if nk > 1 else None
        k = pl.program_id(2)
        part = lax.dot_general(a_ref[...].astype(BF16), b_ref[...].astype(BF16), _DIMS[mode],
                               preferred_element_type=F32)

        def finish(y):
            if has_bias:
                y = y + bias_ref[...]
            o_ref[...] = y.astype(o_ref.dtype)
            if has_res:
                o2_ref[...] = res_ref[...] + cs_ref[...] * y

        if nk == 1:
            finish(part)
            return

        @pl.when(k == 0)
        def _():
            acc_ref[...] = part

        @pl.when((k > 0) & (k < nk - 1))
        def _():
            acc_ref[...] += part

        @pl.when(k == nk - 1)
        def _():
            finish(acc_ref[...] + part)

    return body


def mm(a, b, mode, out_dtype, name, *, tm=1408, tn=1408, tk=1024, bias=None, res=None, colscale=None, layer=None):
    bshape = b.shape if layer is None else b.shape[1:]
    if mode == "nn":
        (M, K), N = a.shape, bshape[1]
    elif mode == "nt":
        (M, K), N = a.shape, bshape[0]
    else:
        (K, M), N = a.shape, bshape[1]
    tm = _pick(M, tm, LANES if mode == "tn" else 8)
    tn = _pick(N, tn, LANES)
    tk = _pick(K, tk, LANES) if mode != "tn" else _pick(K, tk, 8)
    nk = K // tk
    a_spec = {"nn": pl.BlockSpec((tm, tk), lambda i, j, k: (i, k)),
              "nt": pl.BlockSpec((tm, tk), lambda i, j, k: (i, k)),
              "tn": pl.BlockSpec((tk, tm), lambda i, j, k: (k, i))}[mode]
    b_blk, b_map = {"nn": ((tk, tn), lambda i, j, k: (k, j)),
                    "nt": ((tn, tk), lambda i, j, k: (j, k)),
                    "tn": ((tk, tn), lambda i, j, k: (k, j))}[mode]
    if layer is None:
        b_spec = pl.BlockSpec(b_blk, b_map)
    else:
        b_spec = pl.BlockSpec((None,) + b_blk, lambda i, j, k: (layer,) + b_map(i, j, k))
    o_spec = pl.BlockSpec((tm, tn), lambda i, j, k: (i, j))
    v_spec = pl.BlockSpec((1, tn), lambda i, j, k: (0, j))
    in_specs, args = [a_spec, b_spec], [a, b]
    if bias is not None:
        in_specs.append(v_spec)
        args.append(_row(bias))
    out_shape = [jax.ShapeDtypeStruct((M, N), out_dtype)]
    out_specs = [o_spec]
    if res is not None:
        in_specs += [o_spec, v_spec]
        args += [res, _row(colscale)]
        out_shape.append(jax.ShapeDtypeStruct((M, N), F32))
        out_specs.append(o_spec)
    outs = pl.pallas_call(
        _mm_body(mode, nk, bias is not None, res is not None),
        grid=(M // tm, N // tn, nk),
        in_specs=in_specs, out_specs=out_specs, out_shape=out_shape,
        scratch_shapes=[pltpu.VMEM((tm, tn), F32)] if nk > 1 else [],
        compiler_params=_cparams("parallel", "parallel", "arbitrary"),
        name=name,
    )(*args)
    return outs if res is not None else outs[0]


def gmm(a, w, mode, out_dtype, name, *, bias=None, res=None, colscale=None, tr=512):
    S_ = a.shape[0]
    G = N_GROUPS
    C = a.shape[1] // G
    tr = _pick(S_, tr, 8)
    nr = S_ // tr
    if mode == "tn":
        def body(a_ref, b_ref, o_ref, acc_ref):
            i = pl.program_id(1)

            @pl.when(i == 0)
            def _():
                acc_ref[...] = jnp.zeros_like(acc_ref)

            acc_ref[...] += lax.dot_general(a_ref[...].astype(BF16), b_ref[...].astype(BF16), _DIMS["tn"],
                                            preferred_element_type=F32)

            @pl.when(i == nr - 1)
            def _():
                o_ref[...] = acc_ref[...].astype(o_ref.dtype)

        blk = pl.BlockSpec((tr, C), lambda g, i: (i, g))
        return pl.pallas_call(
            body, grid=(G, nr), in_specs=[blk, blk],
            out_specs=pl.BlockSpec((None, C, C), lambda g, i: (g, 0, 0)),
            out_shape=jax.ShapeDtypeStruct((G, C, C), out_dtype),
            scratch_shapes=[pltpu.VMEM((C, C), F32)],
            compiler_params=_cparams("parallel", "arbitrary"), name=name,
        )(a, w)

    has_bias, has_res = bias is not None, res is not None

    def body(*refs):
        a_ref, w_ref = refs[0], refs[1]
        pos = 2
        if has_bias:
            bias_ref = refs[pos]
            pos += 1
        if has_res:
            res_ref, cs_ref = refs[pos], refs[pos + 1]
            pos += 2
        o_ref = refs[pos]
        y = lax.dot_general(a_ref[...].astype(BF16), w_ref[...].astype(BF16), _DIMS[mode],
                            preferred_element_type=F32)
        if has_bias:
            y = y + bias_ref[...]
        o_ref[...] = y.astype(o_ref.dtype)
        if has_res:
            refs[pos + 1][...] = res_ref[...] + cs_ref[...] * y

    blk = pl.BlockSpec((tr, C), lambda i, g: (i, g))
    vec = pl.BlockSpec((1, C), lambda i, g: (0, g))
    in_specs = [blk, pl.BlockSpec((None, C, C), lambda i, g: (g, 0, 0))]
    args = [a, w]
    if has_bias:
        in_specs.append(vec)
        args.append(_row(bias))
    out_shape = [jax.ShapeDtypeStruct(a.shape, out_dtype)]
    out_specs = [blk]
    if has_res:
        in_specs += [blk, vec]
        args += [res, _row(colscale)]
        out_shape.append(jax.ShapeDtypeStruct(a.shape, F32))
        out_specs.append(blk)
    outs = pl.pallas_call(
        body, grid=(nr, G), in_specs=in_specs, out_specs=out_specs, out_shape=out_shape,
        compiler_params=_cparams("parallel", "parallel"), name=name,
    )(*args)
    return outs if has_res else outs[0]


def norm_fwd(x, g, sc, sh, out_dtype, name, tr=512):
    S_, Dn = x.shape
    tr = _pick(S_, tr, 8)

    def body(x_ref, g_ref, sc_ref, sh_ref, o_ref):
        xv = x_ref[...]
        r = lax.rsqrt(jnp.mean(xv * xv, axis=-1, keepdims=True) + EPS)
        o_ref[...] = (((xv * r) * g_ref[...]) * (1.0 + sc_ref[...]) + sh_ref[...]).astype(o_ref.dtype)

    blk = pl.BlockSpec((tr, Dn), lambda i: (i, 0))
    vec = pl.BlockSpec((1, Dn), lambda i: (0, 0))
    return pl.pallas_call(
        body, grid=(S_ // tr,), in_specs=[blk, vec, vec, vec], out_specs=blk,
        out_shape=jax.ShapeDtypeStruct((S_, Dn), out_dtype),
        compiler_params=_cparams("parallel"), name=name,
    )(x, _row(g), _row(sc), _row(sh))


def norm_bwd(x, g, sc, dh, dres, name, tr=512):
    S_, Dn = x.shape
    tr = _pick(S_, tr, 8)
    has_res = dres is not None

    def body(*refs):
        x_ref, g_ref, sc_ref, dh_ref = refs[:4]
        pos = 4
        if has_res:
            dres_ref = refs[pos]
            pos += 1
        dx_ref, s1_ref, s2_ref = refs[pos:pos + 3]
        i = pl.program_id(0)

        @pl.when(i == 0)
        def _():
            s1_ref[...] = jnp.zeros_like(s1_ref)
            s2_ref[...] = jnp.zeros_like(s2_ref)

        xv = x_ref[...]
        r = lax.rsqrt(jnp.mean(xv * xv, axis=-1, keepdims=True) + EPS)
        n = xv * r
        dhv = dh_ref[...].astype(F32)
        dn = dhv * (g_ref[...] * (1.0 + sc_ref[...]))
        dx = r * (dn - n * jnp.mean(dn * n, axis=-1, keepdims=True))
        if has_res:
            dx = dx + dres_ref[...]
        dx_ref[...] = dx
        s1_ref[...] += jnp.sum(dhv, axis=0, keepdims=True)
        s2_ref[...] += jnp.sum(dhv * n, axis=0, keepdims=True)

    blk = pl.BlockSpec((tr, Dn), lambda i: (i, 0))
    vec = pl.BlockSpec((1, Dn), lambda i: (0, 0))
    in_specs, args = [blk, vec, vec, blk], [x, _row(g), _row(sc), dh]
    if has_res:
        in_specs.append(blk)
        args.append(dres)
    vshape = jax.ShapeDtypeStruct((1, Dn), F32)
    return pl.pallas_call(
        body, grid=(S_ // tr,), in_specs=in_specs, out_specs=[blk, vec, vec],
        out_shape=[jax.ShapeDtypeStruct((S_, Dn), F32), vshape, vshape],
        compiler_params=_cparams("arbitrary"), name=name,
    )(*args)


def gate_bwd(dx, y, colscale, name, tr=512):
    S_, Dn = dx.shape
    tr = _pick(S_, tr, 8)

    def body(dx_ref, y_ref, cs_ref, d_ref, a_ref, c_ref):
        i = pl.program_id(0)

        @pl.when(i == 0)
        def _():
            a_ref[...] = jnp.zeros_like(a_ref)
            c_ref[...] = jnp.zeros_like(c_ref)

        dxv = dx_ref[...]
        d_ref[...] = (dxv * cs_ref[...]).astype(d_ref.dtype)
        a_ref[...] += jnp.sum(dxv * y_ref[...].astype(F32), axis=0, keepdims=True)
        c_ref[...] += jnp.sum(dxv, axis=0, keepdims=True)

    blk = pl.BlockSpec((tr, Dn), lambda i: (i, 0))
    vec = pl.BlockSpec((1, Dn), lambda i: (0, 0))
    vshape = jax.ShapeDtypeStruct((1, Dn), F32)
    return pl.pallas_call(
        body, grid=(S_ // tr,), in_specs=[blk, blk, vec], out_specs=[blk, vec, vec],
        out_shape=[jax.ShapeDtypeStruct((S_, Dn), BF16), vshape, vshape],
        compiler_params=_cparams("arbitrary"), name=name,
    )(dx, y, _row(colscale))


def loss_head(x, g, target, name, tr=512):
    S_, Dn = x.shape
    tr = _pick(S_, tr, 8)

    def body(x_ref, g_ref, t_ref, dx_ref, dg_ref, loss_ref):
        i = pl.program_id(0)

        @pl.when(i == 0)
        def _():
            dg_ref[...] = jnp.zeros_like(dg_ref)
            loss_ref[...] = jnp.zeros_like(loss_ref)

        xv = x_ref[...]
        r = lax.rsqrt(jnp.mean(xv * xv, axis=-1, keepdims=True) + EPS)
        n = xv * r
        e = n * g_ref[...] - t_ref[...]
        loss_ref[...] += 0.5 * jnp.sum(jnp.mean(e * e, axis=-1, keepdims=True), axis=0, keepdims=True)
        dy = e * (1.0 / Dn)
        dg_ref[...] += jnp.sum(dy * n, axis=0, keepdims=True)
        dn = dy * g_ref[...]
        dx_ref[...] = r * (dn - n * jnp.mean(dn * n, axis=-1, keepdims=True))

    blk = pl.BlockSpec((tr, Dn), lambda i: (i, 0))
    vec = pl.BlockSpec((1, Dn), lambda i: (0, 0))
    one = pl.BlockSpec((1, 1), lambda i: (0, 0))
    return pl.pallas_call(
        body, grid=(S_ // tr,), in_specs=[blk, vec, blk], out_specs=[blk, vec, one],
        out_shape=[jax.ShapeDtypeStruct((S_, Dn), F32), jax.ShapeDtypeStruct((1, Dn), F32),
                   jax.ShapeDtypeStruct((1, 1), F32)],
        compiler_params=_cparams("arbitrary"), name=name,
    )(x, _row(g), target)


POOL_HALO = 16
POOL_CHUNK = 512


def _rows(ref, lo, hi, n_rows):
    parts = []
    if lo < 0:
        parts.append(jnp.zeros((-lo, ref.shape[1]), F32))
    parts.append(ref[max(lo, 0):min(hi, n_rows), :].astype(F32))
    if hi > n_rows:
        parts.append(jnp.zeros((hi - n_rows, ref.shape[1]), F32))
    return parts[0] if len(parts) == 1 else jnp.concatenate(parts, axis=0)


def _window_sum(e, w, back):
    n = e.shape[0]
    s, width = e, 1
    while width < w:
        s = s + pltpu.roll(s, width if back else n - width, 0)
        width *= 2
    return s


def _pool_call(h, out_dtype, name, backward):
    S_, Dn = h.shape
    C = Dn // N_GROUPS
    ch = _pick(S_, POOL_CHUNK, 8)

    def body(h_ref, o_ref):
        g = pl.program_id(0)
        for gi, w in enumerate(POOL_WINDOWS):
            @pl.when(g == gi)
            def _(w=w):
                for r0 in range(0, S_, ch):
                    t = (r0 + lax.broadcasted_iota(jnp.int32, (ch, C), 0)).astype(F32)
                    cnt = jnp.minimum(t + 1.0, float(w))
                    if not backward:
                        ext = _rows(h_ref, r0 - POOL_HALO, r0 + ch, S_)
                        cur = ext[POOL_HALO:]
                        mean = _window_sum(ext, w, True)[POOL_HALO:] / cnt
                        o_ref[r0:r0 + ch, :] = (mean - cur).astype(o_ref.dtype)
                    else:
                        ext = _rows(h_ref, r0, r0 + ch + POOL_HALO, S_)
                        text = (r0 + lax.broadcasted_iota(jnp.int32, (ch + POOL_HALO, C), 0)).astype(F32)
                        e = ext / jnp.minimum(text + 1.0, float(w))
                        o_ref[r0:r0 + ch, :] = (_window_sum(e, w, False)[:ch] - ext[:ch]).astype(o_ref.dtype)

    blk = pl.BlockSpec((S_, C), lambda g: (0, g))
    return pl.pallas_call(
        body, grid=(N_GROUPS,), in_specs=[blk], out_specs=blk,
        out_shape=jax.ShapeDtypeStruct((S_, Dn), out_dtype),
        compiler_params=_cparams("parallel"), name=name,
    )(h)


GLU_CHUNK = 512
GLU_HALO = 16
_SQRT_HALF = 0.7071067811865476
_INV_SQRT_2PI = 0.3989422804014327


def _gelu(a):
    return 0.5 * a * (1.0 + lax.erf(a * _SQRT_HALF))


def _gelu_grad(a):
    return 0.5 * (1.0 + lax.erf(a * _SQRT_HALF)) + a * (_INV_SQRT_2PI * jnp.exp(-0.5 * a * a))


def glu_fwd(u, conv_w, conv_b, name):
    S_, F2 = u.shape
    Fh = F2 // 2
    tf = GLU_TILE
    nt = Fh // tf
    ch = _pick(S_, GLU_CHUNK, GLU_HALO)

    def body(a_ref, v_ref, cw_ref, cb_ref, z_ref):
        cw0, cw1, cw2 = cw_ref[0:1, :], cw_ref[1:2, :], cw_ref[2:3, :]
        cb = cb_ref[...]
        for r0 in range(0, S_, ch):
            ext = _rows(a_ref, r0 - GLU_HALO, r0 + ch, S_)
            a0 = ext[GLU_HALO:]
            a1 = pltpu.roll(ext, 1, 0)[GLU_HALO:]
            a2 = pltpu.roll(ext, 2, 0)[GLU_HALO:]
            ac = a2 * cw0 + a1 * cw1 + a0 * cw2 + cb
            z_ref[r0:r0 + ch, :] = (_gelu(ac) * v_ref[r0:r0 + ch, :].astype(F32)).astype(z_ref.dtype)

    return pl.pallas_call(
        body, grid=(nt,),
        in_specs=[pl.BlockSpec((S_, tf), lambda j: (0, j)), pl.BlockSpec((S_, tf), lambda j: (0, j + nt)),
                  pl.BlockSpec((3, tf), lambda j: (0, j)), pl.BlockSpec((1, tf), lambda j: (0, j))],
        out_specs=pl.BlockSpec((S_, tf), lambda j: (0, j)),
        out_shape=jax.ShapeDtypeStruct((S_, Fh), BF16),
        compiler_params=_cparams("parallel"), name=name,
    )(u, u, conv_w, _row(conv_b))


def glu_bwd(u, dz, conv_w, conv_b, name):
    S_, F2 = u.shape
    Fh = F2 // 2
    tf = GLU_TILE
    nt = Fh // tf
    ch = _pick(S_, GLU_CHUNK, GLU_HALO)

    def body(a_ref, v_ref, dz_ref, cw_ref, cb_ref, du_ref, dcw_ref, dcb_ref, da_buf, dv_buf, sems):
        j = pl.program_id(0)
        slot = j % 2

        def writes(step, sl):
            lo = pl.multiple_of(step * tf, tf)
            return (pltpu.make_async_copy(da_buf.at[sl], du_ref.at[:, pl.ds(lo, tf)], sems.at[sl, 0]),
                    pltpu.make_async_copy(dv_buf.at[sl], du_ref.at[:, pl.ds(Fh + lo, tf)], sems.at[sl, 1]))

        @pl.when(j >= 2)
        def _():
            for cp in writes(j - 2, slot):
                cp.wait()

        cw0, cw1, cw2 = cw_ref[0:1, :], cw_ref[1:2, :], cw_ref[2:3, :]
        cb = cb_ref[...]
        acc = [jnp.zeros((1, tf), F32) for _ in range(4)]
        n = ch + GLU_HALO
        for r0 in range(0, S_, ch):
            ext = _rows(a_ref, r0 - GLU_HALO, r0 + n, S_)
            a0 = ext[GLU_HALO:]
            a1 = pltpu.roll(ext, 1, 0)[GLU_HALO:]
            a2 = pltpu.roll(ext, 2, 0)[GLU_HALO:]
            ac = a2 * cw0 + a1 * cw1 + a0 * cw2 + cb
            vv = _rows(v_ref, r0, r0 + n, S_)
            dzv = _rows(dz_ref, r0, r0 + n, S_)
            gl = _gelu(ac)
            dac = dzv * vv * _gelu_grad(ac)
            da = (dac * cw2 + pltpu.roll(dac, n - 1, 0) * cw1 + pltpu.roll(dac, n - 2, 0) * cw0)[:ch]
            da_buf[slot, r0:r0 + ch, :] = da.astype(da_buf.dtype)
            dv_buf[slot, r0:r0 + ch, :] = (dzv[:ch] * gl[:ch]).astype(dv_buf.dtype)
            dc = dac[:ch]
            acc[0] = acc[0] + jnp.sum(dc * a2[:ch], axis=0, keepdims=True)
            acc[1] = acc[1] + jnp.sum(dc * a1[:ch], axis=0, keepdims=True)
            acc[2] = acc[2] + jnp.sum(dc * a0[:ch], axis=0, keepdims=True)
            acc[3] = acc[3] + jnp.sum(dc, axis=0, keepdims=True)
        dcw_ref[0:1, :] = acc[0]
        dcw_ref[1:2, :] = acc[1]
        dcw_ref[2:3, :] = acc[2]
        dcb_ref[...] = acc[3]
        for cp in writes(j, slot):
            cp.start()

        @pl.when(j == nt - 1)
        def _():
            for cp in writes(j, slot):
                cp.wait()
            if nt > 1:
                for cp in writes(j - 1, 1 - slot):
                    cp.wait()

    return pl.pallas_call(
        body, grid=(nt,),
        in_specs=[pl.BlockSpec((S_, tf), lambda j: (0, j)), pl.BlockSpec((S_, tf), lambda j: (0, j + nt)),
                  pl.BlockSpec((S_, tf), lambda j: (0, j)),
                  pl.BlockSpec((3, tf), lambda j: (0, j)), pl.BlockSpec((1, tf), lambda j: (0, j))],
        out_specs=[_ANY, pl.BlockSpec((3, tf), lambda j: (0, j)), pl.BlockSpec((1, tf), lambda j: (0, j))],
        out_shape=[jax.ShapeDtypeStruct((S_, F2), BF16), jax.ShapeDtypeStruct((3, Fh), F32),
                   jax.ShapeDtypeStruct((1, Fh), F32)],
        scratch_shapes=[pltpu.VMEM((2, S_, tf), BF16), pltpu.VMEM((2, S_, tf), BF16), pltpu.SemaphoreType.DMA((2, 2))],
        compiler_params=_cparams("arbitrary"), name=name,
    )(u, u, dz, conv_w, _row(conv_b))


def rope_tables(pos, inv, name, tr=512):
    S_ = pos.shape[0]
    tr = _pick(S_, tr, 8)

    def body(p_ref, inv_ref, c_ref, s1_ref, s2_ref):
        ang = p_ref[...] * inv_ref[...]
        lane = lax.broadcasted_iota(jnp.int32, ang.shape, 1)
        half = QK_ROPE // 2
        cosv, sinv = jnp.cos(ang), jnp.sin(ang)
        c_ref[...] = jnp.where(lane < QK_ROPE, cosv, 0.0)
        s1_ref[...] = jnp.where(lane < half, -sinv, 0.0)
        s2_ref[...] = jnp.where((lane >= half) & (lane < QK_ROPE), sinv, 0.0)

    blk = pl.BlockSpec((tr, LANES), lambda i: (i, 0))
    shp = jax.ShapeDtypeStruct((S_, LANES), F32)
    return pl.pallas_call(
        body, grid=(S_ // tr,),
        in_specs=[pl.BlockSpec((tr, 1), lambda i: (i, 0)), pl.BlockSpec((1, LANES), lambda i: (0, 0))],
        out_specs=[blk, blk, blk], out_shape=[shp, shp, shp],
        compiler_params=_cparams("parallel"), name=name,
    )(pos, inv)


_HALF = QK_ROPE // 2


def _rope(t, c, s1, s2):
    return t * c + pltpu.roll(t, LANES - _HALF, 1) * s1 + pltpu.roll(t, _HALF, 1) * s2


def _rope_t(d, c, s1, s2):
    return d * c + pltpu.roll(d * s1, _HALF, 1) + pltpu.roll(d * s2, LANES - _HALF, 1)


def q_prep(q, tabs, scale, backward, name, tr=512):
    S_, W = q.shape
    tr = _pick(S_, tr, 8)

    def body(q_ref, c_ref, s1_ref, s2_ref, o_ref):
        o_ref[:, 0:LANES] = (q_ref[:, 0:LANES].astype(F32) * scale).astype(o_ref.dtype)
        t = q_ref[:, LANES:2 * LANES].astype(F32)
        fn = _rope_t if backward else _rope
        o_ref[:, LANES:2 * LANES] = (fn(t, c_ref[...], s1_ref[...], s2_ref[...]) * scale).astype(o_ref.dtype)

    blk = pl.BlockSpec((tr, HEAD_PAD), lambda i, h: (i, h))
    tab = pl.BlockSpec((tr, LANES), lambda i, h: (i, 0))
    return pl.pallas_call(
        body, grid=(S_ // tr, W // HEAD_PAD), in_specs=[blk, tab, tab, tab], out_specs=blk,
        out_shape=jax.ShapeDtypeStruct((S_, W), BF16),
        compiler_params=_cparams("parallel", "parallel"), name=name,
    )(q, *tabs)


def k_prep(knv, kv_ext, tabs, name, tr=512):
    S_ = knv.shape[0]
    tr = _pick(S_, tr, 8)

    def body(kn_ref, v_ref, t_ref, c_ref, s1_ref, s2_ref, o_ref, vx_ref):
        o_ref[:, 0:LANES] = kn_ref[...].astype(o_ref.dtype)
        o_ref[:, LANES:2 * LANES] = _rope(t_ref[...], c_ref[...], s1_ref[...], s2_ref[...]).astype(o_ref.dtype)
        vx_ref[:, 0:V_HEAD] = v_ref[...].astype(vx_ref.dtype)
        vx_ref[:, V_HEAD:HEAD_PAD] = jnp.ones((tr, HEAD_PAD - V_HEAD), vx_ref.dtype)

    tab = pl.BlockSpec((tr, LANES), lambda i, h: (i, 0))
    head = pl.BlockSpec((tr, HEAD_PAD), lambda i, h: (i, h))
    shp = jax.ShapeDtypeStruct((S_, N_HEADS * HEAD_PAD), BF16)
    return pl.pallas_call(
        body, grid=(S_ // tr, N_HEADS),
        in_specs=[pl.BlockSpec((tr, LANES), lambda i, h: (i, h)),
                  pl.BlockSpec((tr, V_HEAD), lambda i, h: (i, N_HEADS + h)),
                  pl.BlockSpec((tr, LANES), lambda i, h: (i, KV_RANK // LANES)), tab, tab, tab],
        out_specs=[head, head], out_shape=[shp, shp],
        compiler_params=_cparams("parallel", "parallel"), name=name,
    )(knv, knv, kv_ext, *tabs)


def k_prep_bwd(dk_a, dk_b, dv_a, dv_b, tabs, name, tr=256):
    S_ = dk_a.shape[0]
    tr = _pick(S_, tr, 8)
    HV = N_HEADS * V_HEAD

    def body(ka_ref, kb_ref, va_ref, vb_ref, c_ref, s1_ref, s2_ref, o_ref, t_ref):
        dr = jnp.zeros((tr, LANES), F32)
        for h in range(N_HEADS):
            lo = h * HEAD_PAD
            o_ref[:, h * LANES:(h + 1) * LANES] = (ka_ref[:, lo:lo + LANES] + kb_ref[:, lo:lo + LANES]).astype(o_ref.dtype)
            dr = dr + ka_ref[:, lo + LANES:lo + 2 * LANES] + kb_ref[:, lo + LANES:lo + 2 * LANES]
        o_ref[:, HV:2 * HV] = (va_ref[...] + vb_ref[...]).astype(o_ref.dtype)
        t_ref[...] = _rope_t(dr, c_ref[...], s1_ref[...], s2_ref[...])

    kblk = pl.BlockSpec((tr, N_HEADS * HEAD_PAD), lambda i: (i, 0))
    vblk = pl.BlockSpec((tr, HV), lambda i: (i, 0))
    tab = pl.BlockSpec((tr, LANES), lambda i: (i, 0))
    return pl.pallas_call(
        body, grid=(S_ // tr,), in_specs=[kblk, kblk, vblk, vblk, tab, tab, tab],
        out_specs=[pl.BlockSpec((tr, 2 * HV), lambda i: (i, 0)), tab],
        out_shape=[jax.ShapeDtypeStruct((S_, 2 * HV), BF16), jax.ShapeDtypeStruct((S_, LANES), F32)],
        compiler_params=_cparams("parallel"), name=name,
    )(dk_a, dk_b, dv_a, dv_b, *tabs)


_NEG = -1e30


def attn_fwd(q, k, vx, name):
    S_ = q.shape[0]
    TQ = _pick(S_, ATT_Q_BLOCK, 8)
    TK = _pick(S_, ATT_K_BLOCK, TQ)
    HP = ATT_HEADS_PER_STEP
    W = HP * HEAD_PAD
    ratio = TK // TQ

    def body(q_ref, k_ref, v_ref, o_ref, lse_ref):
        i = pl.program_id(1)
        qs = [q_ref[:, h * HEAD_PAD:(h + 1) * HEAD_PAD] for h in range(HP)]

        def step(j, carry, masked):
            start = pl.multiple_of(j * TK, TK)
            out = []
            for h in range(HP):
                m, acc = carry[h]
                cols = slice(h * HEAD_PAD, (h + 1) * HEAD_PAD)
                s = lax.dot_general(qs[h], k_ref[pl.ds(start, TK), cols], _DIMS["nt"], preferred_element_type=F32)
                if masked:
                    rowi = i * TQ + lax.broadcasted_iota(jnp.int32, (TQ, TK), 0)
                    coli = j * TK + lax.broadcasted_iota(jnp.int32, (TQ, TK), 1)
                    s = jnp.where(coli <= rowi, s, _NEG)
                m_new = jnp.maximum(m, jnp.max(s, axis=-1, keepdims=True))
                alpha = jnp.exp(m - m_new)
                p = jnp.exp(s - m_new).astype(BF16)
                acc = alpha * acc + lax.dot_general(p, v_ref[pl.ds(start, TK), cols], _DIMS["nn"],
                                                    preferred_element_type=F32)
                out.append((m_new, acc))
            return tuple(out)

        init = tuple((jnp.full((TQ, 1), _NEG, F32), jnp.zeros((TQ, HEAD_PAD), F32)) for _ in range(HP))
        last = i // ratio
        carry = step(last, lax.fori_loop(0, last, functools.partial(step, masked=False), init), True)
        for h in range(HP):
            m, acc = carry[h]
            l = acc[:, V_HEAD:]
            o_ref[:, h * V_HEAD:(h + 1) * V_HEAD] = (acc[:, :V_HEAD] / l).astype(o_ref.dtype)
            lse_ref[h] = m + jnp.log(jnp.max(l, axis=-1, keepdims=True))

    return pl.pallas_call(
        body, grid=(N_HEADS // HP, S_ // TQ),
        in_specs=[pl.BlockSpec((TQ, W), lambda g, i: (i, g)),
                  pl.BlockSpec((S_, W), lambda g, i: (0, g)),
                  pl.BlockSpec((S_, W), lambda g, i: (0, g))],
        out_specs=[pl.BlockSpec((TQ, HP * V_HEAD), lambda g, i: (i, g)),
                   pl.BlockSpec((HP, TQ, 1), lambda g, i: (g, i, 0))],
        out_shape=[jax.ShapeDtypeStruct((S_, N_HEADS * V_HEAD), BF16), jax.ShapeDtypeStruct((N_HEADS, S_, 1), F32)],
        compiler_params=_cparams("parallel", "parallel"), name=name,
    )(q, k, vx)


def attn_delta(o, do, name, tr=512):
    S_ = o.shape[0]
    tr = _pick(S_, tr, 8)

    def body(o_ref, do_ref, d_ref):
        d_ref[...] = jnp.sum(o_ref[...].astype(F32) * do_ref[...].astype(F32), axis=-1, keepdims=True)

    blk = pl.BlockSpec((tr, V_HEAD), lambda i, h: (i, h))
    return pl.pallas_call(
        body, grid=(S_ // tr, N_HEADS), in_specs=[blk, blk],
        out_specs=pl.BlockSpec((None, tr, 1), lambda i, h: (h, i, 0)),
        out_shape=jax.ShapeDtypeStruct((N_HEADS, S_, 1), F32),
        compiler_params=_cparams("parallel", "parallel"), name=name,
    )(o, do)


def attn_bwd(q, k, vx, do, lse_row, delta_row, name):
    S_ = q.shape[0]
    TK = _pick(S_, ATT_BWD_K_BLOCK, LANES)
    TQ = _pick(S_, ATT_BWD_Q_BLOCK, TK)
    HP = ATT_HEADS_PER_STEP
    W = HP * HEAD_PAD
    ratio = TQ // TK
    nq = S_ // TQ

    def body(q_ref, do_ref, lse_ref, dl_ref, k_ref, v_ref, dq_ref, dk_ref, dv_ref):
        j = pl.program_id(1)

        @pl.when(j == 0)
        def _():
            dq_ref[...] = jnp.zeros_like(dq_ref)

        ks = [k_ref[:, h * HEAD_PAD:(h + 1) * HEAD_PAD] for h in range(HP)]
        vs = [v_ref[:, h * HEAD_PAD:h * HEAD_PAD + V_HEAD] for h in range(HP)]

        def step(i, carry, masked):
            start = pl.multiple_of(i * TQ, TQ)
            out = []
            for h in range(HP):
                dk, dv = carry[h]
                cols = slice(h * HEAD_PAD, (h + 1) * HEAD_PAD)
                qv = q_ref[pl.ds(start, TQ), cols]
                dov = do_ref[pl.ds(start, TQ), h * V_HEAD:(h + 1) * V_HEAD]
                st = lax.dot_general(ks[h], qv, _DIMS["nt"], preferred_element_type=F32)
                pt = jnp.exp(st - lse_ref[h, :, pl.ds(start, TQ)])
                if masked:
                    keyi = j * TK + lax.broadcasted_iota(jnp.int32, (TK, TQ), 0)
                    qryi = i * TQ + lax.broadcasted_iota(jnp.int32, (TK, TQ), 1)
                    pt = jnp.where(keyi <= qryi, pt, 0.0)
                dpt = lax.dot_general(vs[h], dov, _DIMS["nt"], preferred_element_type=F32)
                dst = (pt * (dpt - dl_ref[h, :, pl.ds(start, TQ)])).astype(BF16)
                dv = dv + lax.dot_general(pt.astype(BF16), dov, _DIMS["nn"], preferred_element_type=F32)
                dk = dk + lax.dot_general(dst, qv, _DIMS["nn"], preferred_element_type=F32)
                dq_ref[pl.ds(start, TQ), cols] += lax.dot_general(dst, ks[h], _DIMS["tn"], preferred_element_type=F32)
                out.append((dk, dv))
            return tuple(out)

        init = tuple((jnp.zeros((TK, HEAD_PAD), F32), jnp.zeros((TK, V_HEAD), F32)) for _ in range(HP))
        first = j // ratio
        carry = lax.fori_loop(first + 1, nq, functools.partial(step, masked=False), step(first, init, True))
        for h in range(HP):
            dk_ref[:, h * HEAD_PAD:(h + 1) * HEAD_PAD] = carry[h][0]
            dv_ref[:, h * V_HEAD:(h + 1) * V_HEAD] = carry[h][1]

    return pl.pallas_call(
        body, grid=(N_HEADS // HP, S_ // TK),
        in_specs=[pl.BlockSpec((S_, W), lambda g, j: (0, g)),
                  pl.BlockSpec((S_, HP * V_HEAD), lambda g, j: (0, g)),
                  pl.BlockSpec((HP, 1, S_), lambda g, j: (g, 0, 0)),
                  pl.BlockSpec((HP, 1, S_), lambda g, j: (g, 0, 0)),
                  pl.BlockSpec((TK, W), lambda g, j: (j, g)),
                  pl.BlockSpec((TK, W), lambda g, j: (j, g))],
        out_specs=[pl.BlockSpec((S_, W), lambda g, j: (0, g)),
                   pl.BlockSpec((TK, W), lambda g, j: (j, g)),
                   pl.BlockSpec((TK, HP * V_HEAD), lambda g, j: (j, g))],
        out_shape=[jax.ShapeDtypeStruct((S_, N_HEADS * HEAD_PAD), F32),
                   jax.ShapeDtypeStruct((S_, N_HEADS * HEAD_PAD), F32),
                   jax.ShapeDtypeStruct((S_, N_HEADS * V_HEAD), F32)],
        compiler_params=_cparams("parallel", "arbitrary"), name=name,
    )(q, do, lse_row, delta_row, k, vx)


def mods_fwd(c_all, mod_w, mod_b, name, tn=512):
    L, Dn, E = mod_w.shape
    R = c_all.shape[0]
    tn = _pick(E, tn, LANES)

    def body(c_ref, w_ref, b_ref, o_ref):
        cv = c_ref[...]
        sc = (cv / (1.0 + jnp.exp(-cv))).astype(BF16)
        o_ref[...] = lax.dot_general(sc, w_ref[...].astype(BF16), _DIMS["nn"], preferred_element_type=F32) + b_ref[...]

    return pl.pallas_call(
        body, grid=(L, E // tn),
        in_specs=[pl.BlockSpec((R, Dn), lambda l, j: (0, 0)), pl.BlockSpec((None, Dn, tn), lambda l, j: (l, 0, j)),
                  pl.BlockSpec((None, 1, tn), lambda l, j: (l, 0, j))],
        out_specs=pl.BlockSpec((None, R, tn), lambda l, j: (l, 0, j)),
        out_shape=jax.ShapeDtypeStruct((L, R, E), F32),
        compiler_params=_cparams("parallel", "parallel"), name=name,
    )(c_all, mod_w, mod_b.reshape(L, 1, E))


def _adam_math(w, g, m, v):
    m = ADAM_B1 * m + (1.0 - ADAM_B1) * g
    v = ADAM_B2 * v + (1.0 - ADAM_B2) * (g * g)
    m_hat = m / (1.0 - ADAM_B1 ** ADAM_STEP)
    v_hat = v / (1.0 - ADAM_B2 ** ADAM_STEP)
    delta = -ADAM_LR * (m_hat / (jnp.sqrt(v_hat) + ADAM_EPS) + ADAM_WD * w)
    return delta, m, v


def _as2d(a):
    return a.reshape(-1, a.shape[-1]) if a.ndim != 2 else a


def adamw(w, g, m, v, name):
    shape = w.shape
    w2, g2, m2, v2 = _as2d(w), _as2d(g), _as2d(m), _as2d(v)
    R, C = w2.shape
    tr = _pick(R, max(8, (1 << 18) // C // 8 * 8), 8)

    def body(w_ref, g_ref, m_ref, v_ref, d_ref, mo_ref, vo_ref):
        d, mn, vn = _adam_math(w_ref[...], g_ref[...], m_ref[...], v_ref[...])
        d_ref[...] = d
        mo_ref[...] = mn
        vo_ref[...] = vn

    blk = pl.BlockSpec((tr, C), lambda i: (i, 0))
    shp = jax.ShapeDtypeStruct((R, C), F32)
    outs = pl.pallas_call(
        body, grid=(R // tr,), in_specs=[blk] * 4, out_specs=[blk] * 3, out_shape=[shp] * 3,
        compiler_params=_cparams("parallel"), name=name,
    )(w2, g2, m2, v2)
    return tuple(o.reshape(shape) for o in outs)


def adamw_sum(parts, w, m, v, name):
    P, R, C = parts.shape

    def body(p_ref, w_ref, m_ref, v_ref, g_ref, d_ref, mo_ref, vo_ref):
        g = p_ref[0]
        for k in range(1, P):
            g = g + p_ref[k]
        d, mn, vn = _adam_math(w_ref[...], g, m_ref[...], v_ref[...])
        g_ref[...] = g
        d_ref[...] = d
        mo_ref[...] = mn
        vo_ref[...] = vn

    shp = jax.ShapeDtypeStruct((R, C), F32)
    return pl.pallas_call(body, out_shape=[shp] * 4, compiler_params=_cparams(), name=name)(parts, w, m, v)


def adamw_modw(c_col, dm, w, m, v, name, tr=256, tn=512):
    L, Dn, E = w.shape
    B = c_col.shape[0]
    tr = _pick(Dn, tr, 8)
    tn = _pick(E, tn, LANES)

    def body(c_ref, dm_ref, w_ref, m_ref, v_ref, g_ref, d_ref, mo_ref, vo_ref):
        g = jnp.zeros((tr, tn), F32)
        for b in range(B):
            cv = c_ref[b]
            g = g + (cv / (1.0 + jnp.exp(-cv))) * dm_ref[b:b + 1, :]
        d, mn, vn = _adam_math(w_ref[...], g, m_ref[...], v_ref[...])
        g_ref[...] = g
        d_ref[...] = d
        mo_ref[...] = mn
        vo_ref[...] = vn

    blk = pl.BlockSpec((None, tr, tn), lambda l, i, j: (l, i, j))
    shp = jax.ShapeDtypeStruct((L, Dn, E), F32)
    return pl.pallas_call(
        body, grid=(L, Dn // tr, E // tn),
        in_specs=[pl.BlockSpec((B, tr, 1), lambda l, i, j: (0, i, 0)),
                  pl.BlockSpec((None, B, tn), lambda l, i, j: (l, 0, j)), blk, blk, blk],
        out_specs=[blk] * 4, out_shape=[shp] * 4,
        compiler_params=_cparams("parallel", "parallel", "parallel"), name=name,
    )(c_col, dm, w, m, v)


def add_round(a, b, name, tr=512):
    R, C = a.shape
    tr = _pick(R, tr, 16)

    def body(a_ref, b_ref, o_ref):
        o_ref[...] = (a_ref[...] + b_ref[...].astype(F32)).astype(BF16)

    blk = pl.BlockSpec((tr, C), lambda i: (i, 0))
    return pl.pallas_call(
        body, grid=(R // tr,), in_specs=[blk, blk], out_specs=blk, out_shape=jax.ShapeDtypeStruct((R, C), BF16),
        compiler_params=_cparams("parallel"), name=name,
    )(a, b)


def sum_parts(parts, name, tr=512):
    P, R, C = parts.shape
    tr = _pick(R, tr, 16)

    def body(p_ref, o_ref):
        s = p_ref[0].astype(F32)
        for k in range(1, P):
            s = s + p_ref[k].astype(F32)
        o_ref[...] = s

    return pl.pallas_call(
        body, grid=(R // tr,), in_specs=[pl.BlockSpec((P, tr, C), lambda i: (0, i, 0))],
        out_specs=pl.BlockSpec((tr, C), lambda i: (i, 0)), out_shape=jax.ShapeDtypeStruct((R, C), F32),
        compiler_params=_cparams("parallel"), name=name,
    )(parts)


_ANY = pl.BlockSpec(memory_space=pl.ANY)


def _place():
    return lax.axis_index("x"), lax.axis_index("y"), lax.axis_index("c")


def _flip(v, bit):
    return 1 - v if bit else v


def chip_gather(buf, name):
    def body(in_ref, out_ref, send_sems, recv_sems):
        x, y, c = _place()
        me = 2 * x + y
        sends = []
        for k in range(1, N_CHIPS):
            px, py = _flip(x, k >> 1), _flip(y, k & 1)
            cp = pltpu.make_async_remote_copy(src_ref=in_ref, dst_ref=out_ref.at[me], send_sem=send_sems.at[k - 1],
                                              recv_sem=recv_sems.at[k - 1], device_id=(px, py, c), device_id_type=MESH)
            cp.start()
            sends.append(cp)
        for k in range(1, N_CHIPS):
            px, py = _flip(x, k >> 1), _flip(y, k & 1)
            pltpu.make_async_remote_copy(src_ref=in_ref, dst_ref=out_ref.at[2 * px + py], send_sem=send_sems.at[k - 1],
                                         recv_sem=recv_sems.at[k - 1], device_id=(px, py, c),
                                         device_id_type=MESH).wait_recv()
        for cp in sends:
            cp.wait_send()

    out = pl.pallas_call(
        body, in_specs=[_ANY], out_specs=_ANY,
        out_shape=jax.ShapeDtypeStruct((N_CHIPS,) + buf.shape, buf.dtype),
        scratch_shapes=[pltpu.SemaphoreType.DMA((N_CHIPS - 1,)), pltpu.SemaphoreType.DMA((N_CHIPS - 1,))],
        name=name,
    )(buf)
    return lax.dynamic_update_index_in_dim(out, buf, 2 * lax.axis_index("x") + lax.axis_index("y"), 0)


def chip_all_to_all(buf, name):
    def body(in_ref, out_ref, send_sems, recv_sems):
        x, y, c = _place()
        me = 2 * x + y
        sends = []
        for k in range(1, N_CHIPS):
            px, py = _flip(x, k >> 1), _flip(y, k & 1)
            cp = pltpu.make_async_remote_copy(src_ref=in_ref.at[2 * px + py], dst_ref=out_ref.at[me],
                                              send_sem=send_sems.at[k - 1], recv_sem=recv_sems.at[k - 1],
                                              device_id=(px, py, c), device_id_type=MESH)
            cp.start()
            sends.append(cp)
        for k in range(1, N_CHIPS):
            px, py = _flip(x, k >> 1), _flip(y, k & 1)
            pltpu.make_async_remote_copy(src_ref=in_ref.at[me], dst_ref=out_ref.at[2 * px + py],
                                         send_sem=send_sems.at[k - 1], recv_sem=recv_sems.at[k - 1],
                                         device_id=(px, py, c), device_id_type=MESH).wait_recv()
        for cp in sends:
            cp.wait_send()

    out = pl.pallas_call(
        body, in_specs=[_ANY], out_specs=_ANY, out_shape=jax.ShapeDtypeStruct(buf.shape, buf.dtype),
        scratch_shapes=[pltpu.SemaphoreType.DMA((N_CHIPS - 1,)), pltpu.SemaphoreType.DMA((N_CHIPS - 1,))],
        name=name,
    )(buf)
    me = 2 * lax.axis_index("x") + lax.axis_index("y")
    return lax.dynamic_update_index_in_dim(out, _index(buf, me), me, 0)


def core_gather(buf, name):
    def body(in_ref, out_ref, send_sem, recv_sem):
        x, y, c = _place()
        cp = pltpu.make_async_remote_copy(src_ref=in_ref, dst_ref=out_ref.at[c], send_sem=send_sem, recv_sem=recv_sem,
                                          device_id=(x, y, 1 - c), device_id_type=MESH)
        cp.start()
        pltpu.make_async_remote_copy(src_ref=in_ref, dst_ref=out_ref.at[1 - c], send_sem=send_sem, recv_sem=recv_sem,
                                     device_id=(x, y, 1 - c), device_id_type=MESH).wait_recv()
        cp.wait_send()

    out = pl.pallas_call(
        body, in_specs=[_ANY], out_specs=_ANY, out_shape=jax.ShapeDtypeStruct((2,) + buf.shape, buf.dtype),
        scratch_shapes=[pltpu.SemaphoreType.DMA, pltpu.SemaphoreType.DMA],
        name=name,
    )(buf)
    return lax.dynamic_update_index_in_dim(out, buf, lax.axis_index("c"), 0)


def core_swap(buf, name):
    def body(in_ref, out_ref, send_sem, recv_sem):
        x, y, c = _place()
        cp = pltpu.make_async_remote_copy(src_ref=in_ref, dst_ref=out_ref, send_sem=send_sem, recv_sem=recv_sem,
                                          device_id=(x, y, 1 - c), device_id_type=MESH)
        cp.start()
        cp.wait()

    return pl.pallas_call(
        body, in_specs=[_ANY], out_specs=_ANY, out_shape=jax.ShapeDtypeStruct(buf.shape, buf.dtype),
        scratch_shapes=[pltpu.SemaphoreType.DMA, pltpu.SemaphoreType.DMA],
        name=name,
    )(buf)


def device_gather(buf, name):
    def body(in_ref, out_ref, send_sems, recv_sems, local_sem):
        x, y, c = _place()
        me = 4 * x + 2 * y + c
        mine = pltpu.make_async_copy(in_ref, out_ref.at[me], local_sem)
        mine.start()
        sends = []
        for k in range(1, N_DEV):
            peer = (_flip(x, (k >> 2) & 1), _flip(y, (k >> 1) & 1), _flip(c, k & 1))
            cp = pltpu.make_async_remote_copy(src_ref=in_ref, dst_ref=out_ref.at[me], send_sem=send_sems.at[k - 1],
                                              recv_sem=recv_sems.at[k - 1], device_id=peer, device_id_type=MESH)
            cp.start()
            sends.append(cp)
        for k in range(1, N_DEV):
            peer = (_flip(x, (k >> 2) & 1), _flip(y, (k >> 1) & 1), _flip(c, k & 1))
            pltpu.make_async_remote_copy(src_ref=in_ref, dst_ref=out_ref.at[4 * peer[0] + 2 * peer[1] + peer[2]],
                                         send_sem=send_sems.at[k - 1], recv_sem=recv_sems.at[k - 1], device_id=peer,
                                         device_id_type=MESH).wait_recv()
        for cp in sends:
            cp.wait_send()
        mine.wait()

    return pl.pallas_call(
        body, in_specs=[_ANY], out_specs=_ANY, out_shape=jax.ShapeDtypeStruct((N_DEV,) + buf.shape, buf.dtype),
        scratch_shapes=[pltpu.SemaphoreType.DMA((N_DEV - 1,)), pltpu.SemaphoreType.DMA((N_DEV - 1,)),
                        pltpu.SemaphoreType.DMA],
        name=name,
    )(buf)


def _region(ref, chip_axis=None, chip=None, chip_size=None, half_axis=None, half=None, half_size=None):
    idx = [slice(None)] * len(ref.shape)
    if chip is not None:
        idx[chip_axis] = pl.ds(chip * chip_size, chip_size)
    if half is not None:
        idx[half_axis] = pl.ds(half * half_size, half_size)
    return ref.at[tuple(idx)]


def gather_weights(shards, axes, name):
    n = len(shards)

    def full_shape(t):
        shp = list(shards[t].shape)
        shp[axes[t][0]] *= N_CHIPS
        return tuple(shp)

    def body(*refs):
        ins, outs = refs[:n], refs[n:2 * n]
        ici_send, ici_recv, d2d_send, d2d_recv = refs[2 * n:]
        x, y, c = _place()
        me = 2 * x + y

        def part(t, ref, chip, half):
            ca, ha = axes[t]
            return _region(ref, ca, chip, ins[t].shape[ca], ha, half, ins[t].shape[ha] // 2)

        started = []
        for t in range(n):
            for k in range(1, N_CHIPS):
                px, py = _flip(x, k >> 1), _flip(y, k & 1)
                cp = pltpu.make_async_remote_copy(src_ref=part(t, ins[t], None, c), dst_ref=part(t, outs[t], me, c),
                                                  send_sem=ici_send.at[t, k - 1], recv_sem=ici_recv.at[t, k - 1],
                                                  device_id=(px, py, c), device_id_type=MESH)
                cp.start()
                started.append(cp)
        for t in range(n):
            for k in range(1, N_CHIPS):
                px, py = _flip(x, k >> 1), _flip(y, k & 1)
                got = part(t, outs[t], 2 * px + py, c)
                pltpu.make_async_remote_copy(src_ref=part(t, ins[t], None, c), dst_ref=got,
                                             send_sem=ici_send.at[t, k - 1], recv_sem=ici_recv.at[t, k - 1],
                                             device_id=(px, py, c), device_id_type=MESH).wait_recv()
                fw = pltpu.make_async_remote_copy(src_ref=got, dst_ref=got, send_sem=d2d_send.at[t, k - 1],
                                                  recv_sem=d2d_recv.at[t, k - 1], device_id=(x, y, 1 - c),
                                                  device_id_type=MESH)
                fw.start()
                started.append(fw)
        for t in range(n):
            for k in range(1, N_CHIPS):
                px, py = _flip(x, k >> 1), _flip(y, k & 1)
                theirs = part(t, outs[t], 2 * px + py, 1 - c)
                pltpu.make_async_remote_copy(src_ref=theirs, dst_ref=theirs, send_sem=d2d_send.at[t, k - 1],
                                             recv_sem=d2d_recv.at[t, k - 1], device_id=(x, y, 1 - c),
                                             device_id_type=MESH).wait_recv()
        for cp in started:
            cp.wait_send()

    sem = pltpu.SemaphoreType.DMA((n, N_CHIPS - 1))
    return pl.pallas_call(
        body, in_specs=[_ANY] * n, out_specs=[_ANY] * n,
        out_shape=[jax.ShapeDtypeStruct(full_shape(t), shards[t].dtype) for t in range(n)],
        scratch_shapes=[sem, sem, sem, sem], name=name,
    )(*shards)


def reduce_to_sibling(lo, hi, name):
    n = len(lo)

    def body(*refs):
        los, his, outs = refs[:n], refs[n:2 * n], refs[2 * n:3 * n]
        send_sems, recv_sems = refs[3 * n:]
        x, y, c = _place()

        def copy(u, src):
            return pltpu.make_async_remote_copy(src_ref=src, dst_ref=outs[u], send_sem=send_sems.at[u],
                                                recv_sem=recv_sems.at[u], device_id=(x, y, 1 - c), device_id_type=MESH)

        for u in range(n):
            @pl.when(c == 0)
            def _(u=u):
                copy(u, his[u]).start()

            @pl.when(c == 1)
            def _(u=u):
                copy(u, los[u]).start()
        for u in range(n):
            copy(u, los[u]).wait_recv()
        for u in range(n):
            copy(u, los[u]).wait_send()

    return pl.pallas_call(
        body, in_specs=[_ANY] * (2 * n), out_specs=[_ANY] * n,
        out_shape=[jax.ShapeDtypeStruct(a.shape, a.dtype) for a in lo],
        scratch_shapes=[pltpu.SemaphoreType.DMA((n,)), pltpu.SemaphoreType.DMA((n,))], name=name,
    )(*lo, *hi)


def add_selected(lo, hi, other, name, tile_elems=1 << 19):
    R, C = lo.shape
    tr = _pick(R, max(16, tile_elems // C // 16 * 16), 16)

    def body(lo_ref, hi_ref, o_ref, out_ref):
        mine = jnp.where(lax.axis_index("c") == 0, lo_ref[...].astype(F32), hi_ref[...].astype(F32))
        out_ref[...] = (mine + o_ref[...].astype(F32)).astype(out_ref.dtype)

    blk = pl.BlockSpec((tr, C), lambda i: (i, 0))
    return pl.pallas_call(
        body, grid=(R // tr,), in_specs=[blk, blk, blk], out_specs=blk, out_shape=jax.ShapeDtypeStruct((R, C), BF16),
        compiler_params=_cparams("parallel"), name=name,
    )(lo, hi, other)


def scatter_to_chips(pieces, chip_axes, name):
    n = len(pieces)

    def block_shape(u):
        shp = list(pieces[u].shape)
        shp[chip_axes[u]] //= N_CHIPS
        return tuple(shp)

    def body(*refs):
        ins, outs = refs[:n], refs[n:2 * n]
        send_sems, recv_sems = refs[2 * n:]
        x, y, c = _place()
        me = 2 * x + y
        started = []
        for u in range(n):
            size = block_shape(u)[chip_axes[u]]
            for k in range(1, N_CHIPS):
                px, py = _flip(x, k >> 1), _flip(y, k & 1)
                cp = pltpu.make_async_remote_copy(src_ref=_region(ins[u], chip_axes[u], 2 * px + py, size),
                                                  dst_ref=outs[u].at[me], send_sem=send_sems.at[u, k - 1],
                                                  recv_sem=recv_sems.at[u, k - 1], device_id=(px, py, c),
                                                  device_id_type=MESH)
                cp.start()
                started.append(cp)
        for u in range(n):
            size = block_shape(u)[chip_axes[u]]
            for k in range(1, N_CHIPS):
                px, py = _flip(x, k >> 1), _flip(y, k & 1)
                pltpu.make_async_remote_copy(src_ref=_region(ins[u], chip_axes[u], me, size),
                                             dst_ref=outs[u].at[2 * px + py], send_sem=send_sems.at[u, k - 1],
                                             recv_sem=recv_sems.at[u, k - 1], device_id=(px, py, c),
                                             device_id_type=MESH).wait_recv()
        for cp in started:
            cp.wait_send()

    sem = pltpu.SemaphoreType.DMA((n, N_CHIPS - 1))
    return pl.pallas_call(
        body, in_specs=[_ANY] * n, out_specs=[_ANY] * n,
        out_shape=[jax.ShapeDtypeStruct((N_CHIPS,) + block_shape(u), pieces[u].dtype) for u in range(n)],
        scratch_shapes=[sem, sem], name=name,
    )(*pieces)


def gather_halves(parts, slots, out_shapes, name):
    n = len(parts)

    def body(*refs):
        ins, outs = refs[:n], refs[n:n + len(out_shapes)]
        send_sems, recv_sems = refs[n + len(out_shapes):]
        x, y, c = _place()
        started = []
        for u in range(n):
            t, s = slots[u]
            cp = pltpu.make_async_remote_copy(src_ref=ins[u], dst_ref=outs[t].at[c, s], send_sem=send_sems.at[u],
                                              recv_sem=recv_sems.at[u], device_id=(x, y, 1 - c), device_id_type=MESH)
            cp.start()
            started.append(cp)
        for u in range(n):
            t, s = slots[u]
            pltpu.make_async_remote_copy(src_ref=ins[u], dst_ref=outs[t].at[1 - c, s], send_sem=send_sems.at[u],
                                         recv_sem=recv_sems.at[u], device_id=(x, y, 1 - c),
                                         device_id_type=MESH).wait_recv()
        for cp in started:
            cp.wait_send()

    return pl.pallas_call(
        body, in_specs=[_ANY] * n, out_specs=[_ANY] * len(out_shapes),
        out_shape=[jax.ShapeDtypeStruct(shp, F32) for shp in out_shapes],
        scratch_shapes=[pltpu.SemaphoreType.DMA((n,)), pltpu.SemaphoreType.DMA((n,))], name=name,
    )(*parts)


WEIGHT_ORDER = ["mod_w", "mod_b", "norm1_g", "norm2_g", "pool_w", "pool_b", "pool_scale", "kv_in_g", "w_dkv",
                "ckv_norm_g", "w_uk", "w_uv", "w_dq", "q_norm_g", "w_uq", "w_o", "w_up", "conv_w", "conv_b", "w_down",
                "final_g"]
EXCHANGED = {"w_up": (2, 0), "w_down": (1, 0), "w_o": (1, 0), "w_uq": (2, 0), "w_dq": (1, 0), "pool_w": (2, 0),
             "w_dkv": (0, 1), "w_uk": (1, 0), "w_uv": (1, 0)}
SMALL_SHARDED = {"conv_w": 2, "pool_b": 1, "pool_scale": 1}
REPLICATED = ["mod_b", "norm1_g", "norm2_g", "kv_in_g", "ckv_norm_g", "q_norm_g", "conv_b", "final_g"]


def _padded(n, align):
    return -(-n // align) * align


def _flat_pad(parts, total):
    flat = jnp.concatenate(parts, axis=-1)
    pad = total - flat.shape[-1]
    if pad:
        flat = jnp.concatenate([flat, jnp.zeros(flat.shape[:-1] + (pad,), flat.dtype)], axis=-1)
    return flat


def _split_shards(full, axis):
    shp = full.shape
    t = full.reshape(shp[:axis] + (N_CHIPS, shp[axis] // N_CHIPS) + shp[axis + 1:])
    return jnp.moveaxis(t, axis, 0).reshape(N_CHIPS, -1)


def _join_shards(rows, shard_shape, axis):
    t = jnp.moveaxis(rows.reshape((N_CHIPS,) + tuple(shard_shape)), 0, axis)
    return t.reshape(tuple(shard_shape[:axis]) + (N_CHIPS * shard_shape[axis],) + tuple(shard_shape[axis + 1:]))


def _index(a, i, axis=0):
    return lax.dynamic_index_in_dim(a, i, axis, keepdims=False)


def kernel(x, c, positions, mod_w, mod_b, norm1_g, norm2_g, pool_w, pool_b, pool_scale, kv_in_g, w_dkv, ckv_norm_g, w_uk, w_uv, w_dq, q_norm_g, w_uq, w_o, w_up, conv_w, conv_b, w_down, final_g, loss_target, m_mod_w, m_mod_b, m_norm1_g, m_norm2_g, m_pool_w, m_pool_b, m_pool_scale, m_kv_in_g, m_w_dkv, m_ckv_norm_g, m_w_uk, m_w_uv, m_w_dq, m_q_norm_g, m_w_uq, m_w_o, m_w_up, m_conv_w, m_conv_b, m_w_down, m_final_g, v_mod_w, v_mod_b, v_norm1_g, v_norm2_g, v_pool_w, v_pool_b, v_pool_scale, v_kv_in_g, v_w_dkv, v_ckv_norm_g, v_w_uk, v_w_uv, v_w_dq, v_q_norm_g, v_w_uq, v_w_o, v_w_up, v_conv_w, v_conv_b, v_w_down, v_final_g):
    given = dict(locals())
    W = {n: given[n] for n in WEIGHT_ORDER}
    M1 = {n: given["m_" + n] for n in WEIGHT_ORDER}
    V2 = {n: given["v_" + n] for n in WEIGHT_ORDER}
    xi, yi, ci = lax.axis_index("x"), lax.axis_index("y"), lax.axis_index("c")
    chip = 2 * xi + yi
    dev = 4 * xi + 2 * yi + ci
    x0 = x[0]
    S_, D = x0.shape
    Fh = conv_b.shape[1]
    E = mod_b.shape[1]
    Es = E // N_CHIPS
    zD = jnp.zeros((D,), F32)

    def exchange_view(n, a):
        return a.reshape(1, 2, a.shape[0] // 2, a.shape[1]) if n == "w_dkv" else a

    names = list(EXCHANGED)
    shards = [exchange_view(n, W[n].astype(BF16)) for n in names]
    gathered = gather_weights(shards, [EXCHANGED[n] for n in names], "gather_weights")
    full = {}
    for n, shard, got in zip(names, shards, gathered):
        ca = EXCHANGED[n][0]
        full[n] = lax.dynamic_update_slice_in_dim(got, shard, chip * shard.shape[ca], axis=ca)
    full["w_dkv"] = full["w_dkv"].reshape(D, W["w_dkv"].shape[1])
    ssz = {n: math.prod(W[n].shape) for n in SMALL_SHARDED}
    Tw = _padded(sum(ssz.values()), 8 * PACK_COLS)
    small_rows = chip_gather(_flat_pad([W[n].reshape(-1) for n in SMALL_SHARDED], Tw).reshape(-1, PACK_COLS),
                             "gather_small_w").reshape(N_CHIPS, Tw)
    off = 0
    for n, axis in SMALL_SHARDED.items():
        full[n] = _join_shards(small_rows[:, off:off + ssz[n]], W[n].shape, axis)
        off += ssz[n]

    n_mla = DEPTH - N_A
    q_rank = full["w_uq"].shape[1]
    wq = full["w_uq"].reshape(n_mla, q_rank, N_HEADS, QK_HEAD)
    w_uq_ext = jnp.concatenate([wq, jnp.zeros((n_mla, q_rank, N_HEADS, HEAD_PAD - QK_HEAD), BF16)],
                               axis=3).reshape(n_mla, q_rank, N_HEADS * HEAD_PAD)
    kv_w = KV_RANK + QK_ROPE
    w_dkv_ext = jnp.concatenate([full["w_dkv"], jnp.zeros((D, KV_RANK + LANES - kv_w), BF16)], axis=1)
    w_ukv = jnp.concatenate([full["w_uk"], full["w_uv"]], axis=1)

    c_all = device_gather(c, "gather_c").reshape(N_DEV, D)
    c_pad = jnp.concatenate([c_all, jnp.zeros((16 - N_DEV, D), F32)], axis=0)
    mod_b_mine = lax.dynamic_slice_in_dim(mod_b, chip * Es, Es, axis=1)
    mods_part = mods_fwd(c_pad, mod_w, mod_b_mine, "mods_fwd")
    mods_all = chip_gather(mods_part, "gather_mods")
    mods = jnp.swapaxes(_index(mods_all, dev, axis=2), 0, 1).reshape(DEPTH, E)
    mod = [[mods[l, k * D:(k + 1) * D] for k in range(6)] for l in range(DEPTH)]

    half = QK_ROPE // 2
    inv = 1.0 / (ROPE_THETA ** (jnp.arange(0, QK_ROPE, 2, dtype=F32) / QK_ROPE))
    inv_row = jnp.concatenate([inv, inv, jnp.zeros((LANES - 2 * half,), F32)]).reshape(1, LANES)
    tabs = rope_tables(positions[0].astype(F32).reshape(S_, 1), inv_row, "rope_tables")
    att_scale = QK_HEAD ** -0.5

    saved = []
    xcur = x0
    kv_saved = None
    K = VX = knv = None
    for l in range(DEPTH):
        sh1, sc1, g1, sh2, sc2, g2 = mod[l]
        st = {"xin": xcur}
        if l < N_A:
            h1 = norm_fwd(xcur, norm1_g[l], sc1, sh1, F32, f"norm1_fwd{l}")
            st["pooled"] = _pool_call(h1, BF16, f"pool_fwd{l}", False)
            st["cs"] = g1 * full["pool_scale"][l]
            st["ypre"], xmid = gmm(st["pooled"], full["pool_w"][l], "nn", F32, f"pool_mm{l}", bias=full["pool_b"][l],
                                   res=xcur, colscale=st["cs"])
        else:
            j = l - N_A
            st["h1"] = norm_fwd(xcur, norm1_g[l], sc1, sh1, BF16, f"norm1_fwd{l}")
            st["ql"] = mm(st["h1"], full["w_dq"], "nn", F32, f"dq_mm{l}", layer=j)
            st["cq"] = norm_fwd(st["ql"], q_norm_g[j], jnp.zeros_like(q_norm_g[j]), jnp.zeros_like(q_norm_g[j]), BF16,
                                f"qnorm_fwd{l}")
            qe = mm(st["cq"], w_uq_ext, "nn", F32, f"uq_mm{l}", layer=j)
            st["Q"] = q_prep(qe, tabs, att_scale, False, f"q_prep{l}")
            st["o"], lse = attn_fwd(st["Q"], K, VX, f"attn_fwd{l}")
            st["lse"] = lse.reshape(N_HEADS, 1, S_)
            st["y"], xmid = mm(st["o"], full["w_o"], "nn", F32, f"wo_mm{l}", res=xcur, colscale=g1, layer=j)
        st["xmid"] = xmid
        st["h2"] = norm_fwd(xmid, norm2_g[l], sc2, sh2, BF16, f"norm2_fwd{l}")
        st["u"] = mm(st["h2"], full["w_up"], "nn", BF16, f"up_mm{l}", layer=l)
        st["z"] = glu_fwd(st["u"], full["conv_w"][l], conv_b[l], f"glu_fwd{l}")
        st["f"], xcur = mm(st["z"], full["w_down"], "nn", F32, f"down_mm{l}", tk=1408, res=xmid, colscale=g2, layer=l)
        saved.append(st)
        if l == N_A - 1:
            xn = norm_fwd(xcur, kv_in_g, zD, zD, BF16, "kvin_fwd")
            kv_ext = mm(xn, w_dkv_ext, "nn", F32, "dkv_mm")
            lat = kv_ext[:, :KV_RANK]
            zk = jnp.zeros((KV_RANK,), F32)
            ckv = norm_fwd(lat, ckv_norm_g, zk, zk, BF16, "ckv_fwd")
            knv = mm(ckv, w_ukv, "nn", BF16, "ukv_mm")
            K, VX = k_prep(knv, kv_ext, tabs, "k_prep")
            kv_saved = {"x": xcur, "xn": xn, "lat": lat, "ckv": ckv}

    dx, d_final_g, loss_part = loss_head(xcur, final_g, loss_target[0], "loss_head")
    loss = lax.psum(loss_part[0, 0], ("x", "y", "c"))

    G = {}
    dmods = [None] * DEPTH
    d_norm1 = [None] * DEPTH
    d_norm2 = [None] * DEPTH
    d_conv_b = [None] * DEPTH
    d_qnorm = [None] * n_mla
    dkv_acc = []
    for l in reversed(range(DEPTH)):
        sh1, sc1, g1, sh2, sc2, g2 = mod[l]
        st = saved[l]
        df, a2, _ = gate_bwd(dx, st["f"], g2, f"gate2_bwd{l}")
        dz = mm(df, full["w_down"], "nt", BF16, f"down_dx{l}", layer=l)
        G[("w_down", l)] = mm(st["z"], df, "tn", BF16, f"down_dw{l}")
        du, dcw, dcb = glu_bwd(st["u"], dz, full["conv_w"][l], conv_b[l], f"glu_bwd{l}")
        G[("conv_w", l)] = dcw
        d_conv_b[l] = dcb[0]
        dh2 = mm(du, full["w_up"], "nt", BF16, f"up_dx{l}", tk=1408, layer=l)
        G[("w_up", l)] = mm(st["h2"], du, "tn", BF16, f"up_dw{l}")
        dxmid, s1, s2 = norm_bwd(st["xmid"], norm2_g[l], sc2, dh2, dx, f"norm2_bwd{l}")
        dsh2, dsc2, d_norm2[l] = s1[0], s2[0] * norm2_g[l], s2[0] * (1.0 + sc2)
        if l < N_A:
            dyp, a1, csum = gate_bwd(dxmid, st["ypre"], st["cs"], f"gate1_bwd{l}")
            dg1 = full["pool_scale"][l] * a1[0]
            G[("pool_scale", l)] = g1 * a1[0]
            G[("pool_b", l)] = st["cs"] * csum[0]
            dpooled = gmm(dyp, full["pool_w"][l], "nt", F32, f"pool_dx{l}")
            G[("pool_w", l)] = gmm(st["pooled"], dyp, "tn", BF16, f"pool_dw{l}")
            dh1 = _pool_call(dpooled, F32, f"pool_bwd{l}", True)
        else:
            j = l - N_A
            dy, a1, _ = gate_bwd(dxmid, st["y"], g1, f"gate1_bwd{l}")
            dg1 = a1[0]
            do = mm(dy, full["w_o"], "nt", BF16, f"wo_dx{l}", layer=j)
            G[("w_o", j)] = mm(st["o"], dy, "tn", BF16, f"wo_dw{l}")
            delta = attn_delta(st["o"], do, f"attn_delta{l}").reshape(N_HEADS, 1, S_)
            dQ, dK, dV = attn_bwd(st["Q"], K, VX, do, st["lse"], delta, f"attn_bwd{l}")
            dkv_acc.append((dK, dV))
            dqe = q_prep(dQ, tabs, att_scale, True, f"q_prep_bwd{l}")
            dcq = mm(dqe, w_uq_ext, "nt", F32, f"uq_dx{l}", layer=j)
            G[("w_uq", j)] = mm(st["cq"], dqe, "tn", BF16, f"uq_dw{l}").reshape(q_rank, N_HEADS, HEAD_PAD)[
                :, :, :QK_HEAD].reshape(q_rank, N_HEADS * QK_HEAD)
            zq = jnp.zeros_like(q_norm_g[j])
            dql, _, s2q = norm_bwd(st["ql"], q_norm_g[j], zq, dcq, None, f"qnorm_bwd{l}")
            d_qnorm[j] = s2q[0]
            dh1 = mm(dql, full["w_dq"], "nt", BF16, f"dq_dx{l}", layer=j)
            G[("w_dq", j)] = mm(st["h1"], dql, "tn", BF16, f"dq_dw{l}")
        dx, s1, s2 = norm_bwd(st["xin"], norm1_g[l], sc1, dh1, dxmid, f"norm1_bwd{l}")
        dsh1, dsc1, d_norm1[l] = s1[0], s2[0] * norm1_g[l], s2[0] * (1.0 + sc1)
        dmods[l] = jnp.concatenate([dsh1, dsc1, dg1, dsh2, dsc2, a2[0]])
        if l == N_A:
            (dk_a, dv_a), (dk_b, dv_b) = dkv_acc
            dknv, d_tk = k_prep_bwd(dk_a, dk_b, dv_a, dv_b, tabs, "k_prep_bwd")
            dckv = mm(dknv, w_ukv, "nt", F32, "ukv_dx")
            d_ukv = mm(kv_saved["ckv"], dknv, "tn", BF16, "ukv_dw")
            G[("w_uk", 0)], G[("w_uv", 0)] = d_ukv[:, :N_HEADS * QK_NOPE], d_ukv[:, N_HEADS * QK_NOPE:]
            zk = jnp.zeros((KV_RANK,), F32)
            dlat, _, s2c = norm_bwd(kv_saved["lat"], ckv_norm_g, zk, dckv, None, "ckv_bwd")
            d_ckv_g = s2c[0]
            dkv_ext = jnp.concatenate([dlat, d_tk], axis=1)
            dxn = mm(dkv_ext, w_dkv_ext, "nt", BF16, "dkv_dx")
            G[("w_dkv", 0)] = mm(kv_saved["xn"], dkv_ext, "tn", BF16, "dkv_dw")[:, :kv_w]
            dx, _, s2k = norm_bwd(kv_saved["x"], kv_in_g, zD, dxn, dx, "kvin_bwd")
            d_kvin_g = s2k[0]

    units = []
    for n, (ca, ha) in EXCHANGED.items():
        if W[n].ndim > 2:
            half_layers = W[n].shape[0] // 2
            for sl in range(half_layers):
                units.append((n, sl, G[(n, sl)], G[(n, half_layers + sl)], ca - 1))
        elif n == "w_dkv":
            g4 = G[(n, 0)].reshape(N_CHIPS, 2, -1, kv_w)
            units.append((n, 0, g4[:, 0], g4[:, 1], 0))
        else:
            rows_half = W[n].shape[0] // 2
            units.append((n, 0, G[(n, 0)][:rows_half], G[(n, 0)][rows_half:], ca))
    lo = [u[2] for u in units]
    hi = [u[3] for u in units]
    theirs = reduce_to_sibling(lo, hi, "reduce_cores")

    def flat2(a):
        return a.reshape(-1, a.shape[-1])

    sums = [add_selected(flat2(l_), flat2(h_), flat2(t_), f"reduce_cores_add{i}").reshape(l_.shape)
            for i, (l_, h_, t_) in enumerate(zip(lo, hi, theirs))]
    axes = [u[4] for u in units]
    got = scatter_to_chips(sums, axes, "reduce_chips")
    reduced = []
    for i, (sm, ax, g4) in enumerate(zip(sums, axes, got)):
        size = sm.shape[ax] // N_CHIPS
        g4 = lax.dynamic_update_index_in_dim(g4, lax.dynamic_slice_in_dim(sm, chip * size, size, axis=ax), chip, 0)
        blk = g4.shape[1:]
        reduced.append(sum_parts(g4.reshape(N_CHIPS, -1, blk[-1]), f"reduce_chips_add{i}").reshape(blk))
    slots, out_shapes = [], []
    for n in EXCHANGED:
        mine = [i for i, u in enumerate(units) if u[0] == n]
        out_shapes.append((2, len(mine)) + reduced[mine[0]].shape)
        slots += [(len(out_shapes) - 1, units[i][1]) for i in mine]
    halves = gather_halves(reduced, slots, out_shapes, "reduce_gather")

    grads, deltas, new_m, new_v = {}, {}, {}, {}
    for ti, n in enumerate(EXCHANGED):
        g = halves[ti]
        for i, u in enumerate(units):
            if u[0] == n:
                g = lax.dynamic_update_slice(g, reduced[i][None, None], (ci, u[1]) + (0,) * reduced[i].ndim)
        grads[n] = g.reshape(W[n].shape)
        deltas[n], new_m[n], new_v[n] = adamw(W[n], grads[n], M1[n], V2[n], f"adamw_{n}")

    small = {"mod_b": jnp.stack(dmods), "norm1_g": jnp.stack(d_norm1), "norm2_g": jnp.stack(d_norm2),
             "kv_in_g": d_kvin_g, "ckv_norm_g": d_ckv_g, "q_norm_g": jnp.stack(d_qnorm),
             "conv_b": jnp.stack(d_conv_b), "final_g": d_final_g[0]}
    extra = {n: jnp.stack([G[(n, i)] for i in range(W[n].shape[0])]) for n in SMALL_SHARDED}
    ssizes = {n: math.prod(W[n].shape) for n in REPLICATED}
    esizes = {n: math.prod(extra[n].shape) for n in SMALL_SHARDED}
    Ts = _padded(sum(ssizes.values()) + sum(esizes.values()), 8 * PACK_COLS)

    def pack_small(d, tail=()):
        return _flat_pad([d[n].reshape(-1) for n in REPLICATED] + [t.reshape(-1) for t in tail],
                         Ts).reshape(Ts // PACK_COLS, PACK_COLS)

    parts = device_gather(pack_small(small, [extra[n] for n in SMALL_SHARDED]), "gather_small")
    outs = adamw_sum(parts, pack_small(W), pack_small(M1), pack_small(V2), "adamw_small")
    off = 0
    for n in REPLICATED:
        for dst, o in zip((grads, deltas, new_m, new_v), outs):
            dst[n] = o.reshape(-1)[off:off + ssizes[n]].reshape(W[n].shape)
        off += ssizes[n]
    for n, axis in SMALL_SHARDED.items():
        g_full = outs[0].reshape(-1)[off:off + esizes[n]].reshape(extra[n].shape)
        off += esizes[n]
        size = W[n].shape[axis]
        grads[n] = lax.dynamic_slice_in_dim(g_full, chip * size, size, axis=axis)
        deltas[n], new_m[n], new_v[n] = adamw(W[n], grads[n], M1[n], V2[n], f"adamw_{n}")

    dm_all = parts.reshape(N_DEV, -1)[:, :DEPTH * E].reshape(N_DEV, DEPTH, E)
    dm_mine = jnp.swapaxes(lax.dynamic_slice_in_dim(dm_all, chip * Es, Es, axis=2), 0, 1)
    grads["mod_w"], deltas["mod_w"], new_m["mod_w"], new_v["mod_w"] = adamw_modw(
        c_all.reshape(N_DEV, D, 1), dm_mine, mod_w, m_mod_w, v_mod_w, "adamw_mod_w")

    return (loss, dx.reshape(x.shape), *[grads[n] for n in WEIGHT_ORDER], *[deltas[n] for n in WEIGHT_ORDER],
            *[new_m[n] for n in WEIGHT_ORDER], *[new_v[n] for n in WEIGHT_ORDER])
```

```python
import functools
import math

import jax
import jax.numpy as jnp
from jax import lax
from jax.experimental import pallas as pl
from jax.experimental.pallas import tpu as pltpu

F32 = jnp.float32
BF16 = jnp.bfloat16
MESH = pl.DeviceIdType.MESH

DEPTH = 4
N_A = 2
POOL_WINDOWS = (2, 4, 8, 16)
N_GROUPS = 4
N_HEADS = 8
QK_NOPE = 128
QK_ROPE = 64
V_HEAD = 128
QK_HEAD = QK_NOPE + QK_ROPE
HEAD_PAD = 256
KV_RANK = 256
ROPE_THETA = 10000.0
EPS = 1e-6
ADAM_LR = 0.001
ADAM_B1 = 0.9
ADAM_B2 = 0.999
ADAM_EPS = 1e-08
ADAM_WD = 0.01
ADAM_STEP = 10

N_CHIPS = 4
N_DEV = 8
LANES = 128
PACK_COLS = 1024
VMEM_LIMIT = 56 * 1024 * 1024
GLU_TILE = 256
ATT_BWD_K_BLOCK = 256
ATT_BWD_Q_BLOCK = 512
ATT_Q_BLOCK = 256
ATT_K_BLOCK = 512
ATT_HEADS_PER_STEP = 2


def _cparams(*sem):
    return pltpu.CompilerParams(dimension_semantics=sem if sem else None, vmem_limit_bytes=VMEM_LIMIT)


def _pick(n, target, mult):
    best = None
    d = mult
    while d <= min(n, target):
        if n % d == 0:
            best = d
        d += mult
    return n if best is None else best


def _row(v):
    return v.reshape(1, -1).astype(F32)


_DIMS = {"nn": (((1,), (0,)), ((), ())), "nt": (((1,), (1,)), ((), ())), "tn": (((0,), (0,)), ((), ()))}


def _mm_body(mode, nk, has_bias, has_res):
    def body(*refs):
        a_ref, b_ref = refs[0], refs[1]
        pos = 2
        bias_ref = res_ref = cs_ref = None
        if has_bias:
            bias_ref = refs[pos]
            pos += 1
        if has_res:
            res_ref, cs_ref = refs[pos], refs[pos + 1]
            pos += 2
        o_ref = refs[pos]
        pos += 1
        o2_ref = None
        if has_res:
            o2_ref = refs[pos]
            pos += 1
        acc_ref = refs[pos] if nk > 1 else None
        k = pl.program_id(2)
        part = lax.dot_general(a_ref[...].astype(BF16), b_ref[...].astype(BF16), _DIMS[mode],
                               preferred_element_type=F32)

        def finish(y):
            if has_bias:
                y = y + bias_ref[...]
            o_ref[...] = y.astype(o_ref.dtype)
            if has_res:
                o2_ref[...] = res_ref[...] + cs_ref[...] * y

        if nk == 1:
            finish(part)
            return

        @pl.when(k == 0)
        def _():
            acc_ref[...] = part

        @pl.when((k > 0) & (k < nk - 1))
        def _():
            acc_ref[...] += part

        @pl.when(k == nk - 1)
        def _():
            finish(acc_ref[...] + part)

    return body


def mm(a, b, mode, out_dtype, name, *, tm=1408, tn=1408, tk=1024, bias=None, res=None, colscale=None, layer=None):
    bshape = b.shape if layer is None else b.shape[1:]
    if mode == "nn":
        (M, K), N = a.shape, bshape[1]
    elif mode == "nt":
        (M, K), N = a.shape, bshape[0]
    else:
        (K, M), N = a.shape, bshape[1]
    tm = _pick(M, tm, LANES if mode == "tn" else 8)
    tn = _pick(N, tn, LANES)
    tk = _pick(K, tk, LANES) if mode != "tn" else _pick(K, tk, 8)
    nk = K // tk
    a_spec = {"nn": pl.BlockSpec((tm, tk), lambda i, j, k: (i, k)),
              "nt": pl.BlockSpec((tm, tk), lambda i, j, k: (i, k)),
              "tn": pl.BlockSpec((tk, tm), lambda i, j, k: (k, i))}[mode]
    b_blk, b_map = {"nn": ((tk, tn), lambda i, j, k: (k, j)),
                    "nt": ((tn, tk), lambda i, j, k: (j, k)),
                    "tn": ((tk, tn), lambda i, j, k: (k, j))}[mode]
    if layer is None:
        b_spec = pl.BlockSpec(b_blk, b_map)
    else:
        b_spec = pl.BlockSpec((None,) + b_blk, lambda i, j, k: (layer,) + b_map(i, j, k))
    o_spec = pl.BlockSpec((tm, tn), lambda i, j, k: (i, j))
    v_spec = pl.BlockSpec((1, tn), lambda i, j, k: (0, j))
    in_specs, args = [a_spec, b_spec], [a, b]
    if bias is not None:
        in_specs.append(v_spec)
        args.append(_row(bias))
    out_shape = [jax.ShapeDtypeStruct((M, N), out_dtype)]
    out_specs = [o_spec]
    if res is not None:
        in_specs += [o_spec, v_spec]
        args += [res, _row(colscale)]
        out_shape.append(jax.ShapeDtypeStruct((M, N), F32))
        out_specs.append(o_spec)
    outs = pl.pallas_call(
        _mm_body(mode, nk, bias is not None, res is not None),
        grid=(M // tm, N // tn, nk),
        in_specs=in_specs, out_specs=out_specs, out_shape=out_shape,
        scratch_shapes=[pltpu.VMEM((tm, tn), F32)] if nk > 1 else [],
        compiler_params=_cparams("parallel", "parallel", "arbitrary"),
        name=name,
    )(*args)
    return outs if res is not None else outs[0]


def gmm(a, w, mode, out_dtype, name, *, bias=None, res=None, colscale=None, tr=512):
    S_ = a.shape[0]
    G = N_GROUPS
    C = a.shape[1] // G
    tr = _pick(S_, tr, 8)
    nr = S_ // tr
    if mode == "tn":
        def body(a_ref, b_ref, o_ref, acc_ref):
            i = pl.program_id(1)

            @pl.when(i == 0)
            def _():
                acc_ref[...] = jnp.zeros_like(acc_ref)

            acc_ref[...] += lax.dot_general(a_ref[...].astype(BF16), b_ref[...].astype(BF16), _DIMS["tn"],
                                            preferred_element_type=F32)

            @pl.when(i == nr - 1)
            def _():
                o_ref[...] = acc_ref[...].astype(o_ref.dtype)

        blk = pl.BlockSpec((tr, C), lambda g, i: (i, g))
        return pl.pallas_call(
            body, grid=(G, nr), in_specs=[blk, blk],
            out_specs=pl.BlockSpec((None, C, C), lambda g, i: (g, 0, 0)),
            out_shape=jax.ShapeDtypeStruct((G, C, C), out_dtype),
            scratch_shapes=[pltpu.VMEM((C, C), F32)],
            compiler_params=_cparams("parallel", "arbitrary"), name=name,
        )(a, w)

    has_bias, has_res = bias is not None, res is not None

    def body(*refs):
        a_ref, w_ref = refs[0], refs[1]
        pos = 2
        if has_bias:
            bias_ref = refs[pos]
            pos += 1
        if has_res:
            res_ref, cs_ref = refs[pos], refs[pos + 1]
            pos += 2
        o_ref = refs[pos]
        y = lax.dot_general(a_ref[...].astype(BF16), w_ref[...].astype(BF16), _DIMS[mode],
                            preferred_element_type=F32)
        if has_bias:
            y = y + bias_ref[...]
        o_ref[...] = y.astype(o_ref.dtype)
        if has_res:
            refs[pos + 1][...] = res_ref[...] + cs_ref[...] * y

    blk = pl.BlockSpec((tr, C), lambda i, g: (i, g))
    vec = pl.BlockSpec((1, C), lambda i, g: (0, g))
    in_specs = [blk, pl.BlockSpec((None, C, C), lambda i, g: (g, 0, 0))]
    args = [a, w]
    if has_bias:
        in_specs.append(vec)
        args.append(_row(bias))
    out_shape = [jax.ShapeDtypeStruct(a.shape, out_dtype)]
    out_specs = [blk]
    if has_res:
        in_specs += [blk, vec]
        args += [res, _row(colscale)]
        out_shape.append(jax.ShapeDtypeStruct(a.shape, F32))
        out_specs.append(blk)
    outs = pl.pallas_call(
        body, grid=(nr, G), in_specs=in_specs, out_specs=out_specs, out_shape=out_shape,
        compiler_params=_cparams("parallel", "parallel"), name=name,
    )(*args)
    return outs if has_res else outs[0]


def norm_fwd(x, g, sc, sh, out_dtype, name, tr=512):
    S_, Dn = x.shape
    tr = _pick(S_, tr, 8)

    def body(x_ref, g_ref, sc_ref, sh_ref, o_ref):
        xv = x_ref[...]
        r = lax.rsqrt(jnp.mean(xv * xv, axis=-1, keepdims=True) + EPS)
        o_ref[...] = (((xv * r) * g_ref[...]) * (1.0 + sc_ref[...]) + sh_ref[...]).astype(o_ref.dtype)

    blk = pl.BlockSpec((tr, Dn), lambda i: (i, 0))
    vec = pl.BlockSpec((1, Dn), lambda i: (0, 0))
    return pl.pallas_call(
        body, grid=(S_ // tr,), in_specs=[blk, vec, vec, vec], out_specs=blk,
        out_shape=jax.ShapeDtypeStruct((S_, Dn), out_dtype),
        compiler_params=_cparams("parallel"), name=name,
    )(x, _row(g), _row(sc), _row(sh))


def norm_bwd(x, g, sc, dh, dres, name, tr=512):
    S_, Dn = x.shape
    tr = _pick(S_, tr, 8)
    has_res = dres is not None

    def body(*refs):
        x_ref, g_ref, sc_ref, dh_ref = refs[:4]
        pos = 4
        if has_res:
            dres_ref = refs[pos]
            pos += 1
        dx_ref, s1_ref, s2_ref = refs[pos:pos + 3]
        i = pl.program_id(0)

        @pl.when(i == 0)
        def _():
            s1_ref[...] = jnp.zeros_like(s1_ref)
            s2_ref[...] = jnp.zeros_like(s2_ref)

        xv = x_ref[...]
        r = lax.rsqrt(jnp.mean(xv * xv, axis=-1, keepdims=True) + EPS)
        n = xv * r
        dhv = dh_ref[...].astype(F32)
        dn = dhv * (g_ref[...] * (1.0 + sc_ref[...]))
        dx = r * (dn - n * jnp.mean(dn * n, axis=-1, keepdims=True))
        if has_res:
            dx = dx + dres_ref[...]
        dx_ref[...] = dx
        s1_ref[...] += jnp.sum(dhv, axis=0, keepdims=True)
        s2_ref[...] += jnp.sum(dhv * n, axis=0, keepdims=True)

    blk = pl.BlockSpec((tr, Dn), lambda i: (i, 0))
    vec = pl.BlockSpec((1, Dn), lambda i: (0, 0))
    in_specs, args = [blk, vec, vec, blk], [x, _row(g), _row(sc), dh]
    if has_res:
        in_specs.append(blk)
        args.append(dres)
    vshape = jax.ShapeDtypeStruct((1, Dn), F32)
    return pl.pallas_call(
        body, grid=(S_ // tr,), in_specs=in_specs, out_specs=[blk, vec, vec],
        out_shape=[jax.ShapeDtypeStruct((S_, Dn), F32), vshape, vshape],
        compiler_params=_cparams("arbitrary"), name=name,
    )(*args)


def gate_bwd(dx, y, colscale, name, tr=512):
    S_, Dn = dx.shape
    tr = _pick(S_, tr, 8)

    def body(dx_ref, y_ref, cs_ref, d_ref, a_ref, c_ref):
        i = pl.program_id(0)

        @pl.when(i == 0)
        def _():
            a_ref[...] = jnp.zeros_like(a_ref)
            c_ref[...] = jnp.zeros_like(c_ref)

        dxv = dx_ref[...]
        d_ref[...] = (dxv * cs_ref[...]).astype(d_ref.dtype)
        a_ref[...] += jnp.sum(dxv * y_ref[...].astype(F32), axis=0, keepdims=True)
        c_ref[...] += jnp.sum(dxv, axis=0, keepdims=True)

    blk = pl.BlockSpec((tr, Dn), lambda i: (i, 0))
    vec = pl.BlockSpec((1, Dn), lambda i: (0, 0))
    vshape = jax.ShapeDtypeStruct((1, Dn), F32)
    return pl.pallas_call(
        body, grid=(S_ // tr,), in_specs=[blk, blk, vec], out_specs=[blk, vec, vec],
        out_shape=[jax.ShapeDtypeStruct((S_, Dn), BF16), vshape, vshape],
        compiler_params=_cparams("arbitrary"), name=name,
    )(dx, y, _row(colscale))


def loss_head(x, g, target, name, tr=512):
    S_, Dn = x.shape
    tr = _pick(S_, tr, 8)

    def body(x_ref, g_ref, t_ref, dx_ref, dg_ref, loss_ref):
        i = pl.program_id(0)

        @pl.when(i == 0)
        def _():
            dg_ref[...] = jnp.zeros_like(dg_ref)
            loss_ref[...] = jnp.zeros_like(loss_ref)

        xv = x_ref[...]
        r = lax.rsqrt(jnp.mean(xv * xv, axis=-1, keepdims=True) + EPS)
        n = xv * r
        e = n * g_ref[...] - t_ref[...]
        loss_ref[...] += 0.5 * jnp.sum(jnp.mean(e * e, axis=-1, keepdims=True), axis=0, keepdims=True)
        dy = e * (1.0 / Dn)
        dg_ref[...] += jnp.sum(dy * n, axis=0, keepdims=True)
        dn = dy * g_ref[...]
        dx_ref[...] = r * (dn - n * jnp.mean(dn * n, axis=-1, keepdims=True))

    blk = pl.BlockSpec((tr, Dn), lambda i: (i, 0))
    vec = pl.BlockSpec((1, Dn), lambda i: (0, 0))
    one = pl.BlockSpec((1, 1), lambda i: (0, 0))
    return pl.pallas_call(
        body, grid=(S_ // tr,), in_specs=[blk, vec, blk], out_specs=[blk, vec, one],
        out_shape=[jax.ShapeDtypeStruct((S_, Dn), F32), jax.ShapeDtypeStruct((1, Dn), F32),
                   jax.ShapeDtypeStruct((1, 1), F32)],
        compiler_params=_cparams("arbitrary"), name=name,
    )(x, _row(g), target)


POOL_HALO = 16
POOL_CHUNK = 512


def _rows(ref, lo, hi, n_rows):
    parts = []
    if lo < 0:
        parts.append(jnp.zeros((-lo, ref.shape[1]), F32))
    parts.append(ref[max(lo, 0):min(hi, n_rows), :].astype(F32))
    if hi > n_rows:
        parts.append(jnp.zeros((hi - n_rows, ref.shape[1]), F32))
    return parts[0] if len(parts) == 1 else jnp.concatenate(parts, axis=0)


def _window_sum(e, w, back):
    n = e.shape[0]
    s, width = e, 1
    while width < w:
        s = s + pltpu.roll(s, width if back else n - width, 0)
        width *= 2
    return s


def _pool_call(h, out_dtype, name, backward):
    S_, Dn = h.shape
    C = Dn // N_GROUPS
    ch = _pick(S_, POOL_CHUNK, 8)

    def body(h_ref, o_ref):
        g = pl.program_id(0)
        for gi, w in enumerate(POOL_WINDOWS):
            @pl.when(g == gi)
            def _(w=w):
                for r0 in range(0, S_, ch):
                    t = (r0 + lax.broadcasted_iota(jnp.int32, (ch, C), 0)).astype(F32)
                    cnt = jnp.minimum(t + 1.0, float(w))
                    if not backward:
                        ext = _rows(h_ref, r0 - POOL_HALO, r0 + ch, S_)
                        cur = ext[POOL_HALO:]
                        mean = _window_sum(ext, w, True)[POOL_HALO:] / cnt
                        o_ref[r0:r0 + ch, :] = (mean - cur).astype(o_ref.dtype)
                    else:
                        ext = _rows(h_ref, r0, r0 + ch + POOL_HALO, S_)
                        text = (r0 + lax.broadcasted_iota(jnp.int32, (ch + POOL_HALO, C), 0)).astype(F32)
                        e = ext / jnp.minimum(text + 1.0, float(w))
                        o_ref[r0:r0 + ch, :] = (_window_sum(e, w, False)[:ch] - ext[:ch]).astype(o_ref.dtype)

    blk = pl.BlockSpec((S_, C), lambda g: (0, g))
    return pl.pallas_call(
        body, grid=(N_GROUPS,), in_specs=[blk], out_specs=blk,
        out_shape=jax.ShapeDtypeStruct((S_, Dn), out_dtype),
        compiler_params=_cparams("parallel"), name=name,
    )(h)


GLU_CHUNK = 512
GLU_HALO = 16
_SQRT_HALF = 0.7071067811865476
_INV_SQRT_2PI = 0.3989422804014327


def _gelu(a):
    return 0.5 * a * (1.0 + lax.erf(a * _SQRT_HALF))


def _gelu_grad(a):
    return 0.5 * (1.0 + lax.erf(a * _SQRT_HALF)) + a * (_INV_SQRT_2PI * jnp.exp(-0.5 * a * a))


def glu_fwd(u, conv_w, conv_b, name):
    S_, F2 = u.shape
    Fh = F2 // 2
    tf = GLU_TILE
    nt = Fh // tf
    ch = _pick(S_, GLU_CHUNK, GLU_HALO)

    def body(a_ref, v_ref, cw_ref, cb_ref, z_ref):
        cw0, cw1, cw2 = cw_ref[0:1, :], cw_ref[1:2, :], cw_ref[2:3, :]
        cb = cb_ref[...]
        for r0 in range(0, S_, ch):
            ext = _rows(a_ref, r0 - GLU_HALO, r0 + ch, S_)
            a0 = ext[GLU_HALO:]
            a1 = pltpu.roll(ext, 1, 0)[GLU_HALO:]
            a2 = pltpu.roll(ext, 2, 0)[GLU_HALO:]
            ac = a2 * cw0 + a1 * cw1 + a0 * cw2 + cb
            z_ref[r0:r0 + ch, :] = (_gelu(ac) * v_ref[r0:r0 + ch, :].astype(F32)).astype(z_ref.dtype)

    return pl.pallas_call(
        body, grid=(nt,),
        in_specs=[pl.BlockSpec((S_, tf), lambda j: (0, j)), pl.BlockSpec((S_, tf), lambda j: (0, j + nt)),
                  pl.BlockSpec((3, tf), lambda j: (0, j)), pl.BlockSpec((1, tf), lambda j: (0, j))],
        out_specs=pl.BlockSpec((S_, tf), lambda j: (0, j)),
        out_shape=jax.ShapeDtypeStruct((S_, Fh), BF16),
        compiler_params=_cparams("parallel"), name=name,
    )(u, u, conv_w, _row(conv_b))


def glu_bwd(u, dz, conv_w, conv_b, name):
    S_, F2 = u.shape
    Fh = F2 // 2
    tf = GLU_TILE
    nt = Fh // tf
    ch = _pick(S_, GLU_CHUNK, GLU_HALO)

    def body(a_ref, v_ref, dz_ref, cw_ref, cb_ref, du_ref, dcw_ref, dcb_ref, da_buf, dv_buf, sems):
        j = pl.program_id(0)
        slot = j % 2

        def writes(step, sl):
            lo = pl.multiple_of(step * tf, tf)
            return (pltpu.make_async_copy(da_buf.at[sl], du_ref.at[:, pl.ds(lo, tf)], sems.at[sl, 0]),
                    pltpu.make_async_copy(dv_buf.at[sl], du_ref.at[:, pl.ds(Fh + lo, tf)], sems.at[sl, 1]))

        @pl.when(j >= 2)
        def _():
            for cp in writes(j - 2, slot):
                cp.wait()

        cw0, cw1, cw2 = cw_ref[0:1, :], cw_ref[1:2, :], cw_ref[2:3, :]
        cb = cb_ref[...]
        acc = [jnp.zeros((1, tf), F32) for _ in range(4)]
        n = ch + GLU_HALO
        for r0 in range(0, S_, ch):
            ext = _rows(a_ref, r0 - GLU_HALO, r0 + n, S_)
            a0 = ext[GLU_HALO:]
            a1 = pltpu.roll(ext, 1, 0)[GLU_HALO:]
            a2 = pltpu.roll(ext, 2, 0)[GLU_HALO:]
            ac = a2 * cw0 + a1 * cw1 + a0 * cw2 + cb
            vv = _rows(v_ref, r0, r0 + n, S_)
            dzv = _rows(dz_ref, r0, r0 + n, S_)
            gl = _gelu(ac)
            dac = dzv * vv * _gelu_grad(ac)
            da = (dac * cw2 + pltpu.roll(dac, n - 1, 0) * cw1 + pltpu.roll(dac, n - 2, 0) * cw0)[:ch]
            da_buf[slot, r0:r0 + ch, :] = da.astype(da_buf.dtype)
            dv_buf[slot, r0:r0 + ch, :] = (dzv[:ch] * gl[:ch]).astype(dv_buf.dtype)
            dc = dac[:ch]
            acc[0] = acc[0] + jnp.sum(dc * a2[:ch], axis=0, keepdims=True)
            acc[1] = acc[1] + jnp.sum(dc * a1[:ch], axis=0, keepdims=True)
            acc[2] = acc[2] + jnp.sum(dc * a0[:ch], axis=0, keepdims=True)
            acc[3] = acc[3] + jnp.sum(dc, axis=0, keepdims=True)
        dcw_ref[0:1, :] = acc[0]
        dcw_ref[1:2, :] = acc[1]
        dcw_ref[2:3, :] = acc[2]
        dcb_ref[...] = acc[3]
        for cp in writes(j, slot):
            cp.start()

        @pl.when(j == nt - 1)
        def _():
            for cp in writes(j, slot):
                cp.wait()
            if nt > 1:
                for cp in writes(j - 1, 1 - slot):
                    cp.wait()

    return pl.pallas_call(
        body, grid=(nt,),
        in_specs=[pl.BlockSpec((S_, tf), lambda j: (0, j)), pl.BlockSpec((S_, tf), lambda j: (0, j + nt)),
                  pl.BlockSpec((S_, tf), lambda j: (0, j)),
                  pl.BlockSpec((3, tf), lambda j: (0, j)), pl.BlockSpec((1, tf), lambda j: (0, j))],
        out_specs=[_ANY, pl.BlockSpec((3, tf), lambda j: (0, j)), pl.BlockSpec((1, tf), lambda j: (0, j))],
        out_shape=[jax.ShapeDtypeStruct((S_, F2), BF16), jax.ShapeDtypeStruct((3, Fh), F32),
                   jax.ShapeDtypeStruct((1, Fh), F32)],
        scratch_shapes=[pltpu.VMEM((2, S_, tf), BF16), pltpu.VMEM((2, S_, tf), BF16), pltpu.SemaphoreType.DMA((2, 2))],
        compiler_params=_cparams("arbitrary"), name=name,
    )(u, u, dz, conv_w, _row(conv_b))


def rope_tables(pos, inv, name, tr=512):
    S_ = pos.shape[0]
    tr = _pick(S_, tr, 8)

    def body(p_ref, inv_ref, c_ref, s1_ref, s2_ref):
        ang = p_ref[...] * inv_ref[...]
        lane = lax.broadcasted_iota(jnp.int32, ang.shape, 1)
        half = QK_ROPE // 2
        cosv, sinv = jnp.cos(ang), jnp.sin(ang)
        c_ref[...] = jnp.where(lane < QK_ROPE, cosv, 0.0)
        s1_ref[...] = jnp.where(lane < half, -sinv, 0.0)
        s2_ref[...] = jnp.where((lane >= half) & (lane < QK_ROPE), sinv, 0.0)

    blk = pl.BlockSpec((tr, LANES), lambda i: (i, 0))
    shp = jax.ShapeDtypeStruct((S_, LANES), F32)
    return pl.pallas_call(
        body, grid=(S_ // tr,),
        in_specs=[pl.BlockSpec((tr, 1), lambda i: (i, 0)), pl.BlockSpec((1, LANES), lambda i: (0, 0))],
        out_specs=[blk, blk, blk], out_shape=[shp, shp, shp],
        compiler_params=_cparams("parallel"), name=name,
    )(pos, inv)


_HALF = QK_ROPE // 2


def _rope(t, c, s1, s2):
    return t * c + pltpu.roll(t, LANES - _HALF, 1) * s1 + pltpu.roll(t, _HALF, 1) * s2


def _rope_t(d, c, s1, s2):
    return d * c + pltpu.roll(d * s1, _HALF, 1) + pltpu.roll(d * s2, LANES - _HALF, 1)


def q_prep(q, tabs, scale, backward, name, tr=512):
    S_, W = q.shape
    tr = _pick(S_, tr, 8)

    def body(q_ref, c_ref, s1_ref, s2_ref, o_ref):
        o_ref[:, 0:LANES] = (q_ref[:, 0:LANES].astype(F32) * scale).astype(o_ref.dtype)
        t = q_ref[:, LANES:2 * LANES].astype(F32)
        fn = _rope_t if backward else _rope
        o_ref[:, LANES:2 * LANES] = (fn(t, c_ref[...], s1_ref[...], s2_ref[...]) * scale).astype(o_ref.dtype)

    blk = pl.BlockSpec((tr, HEAD_PAD), lambda i, h: (i, h))
    tab = pl.BlockSpec((tr, LANES), lambda i, h: (i, 0))
    return pl.pallas_call(
        body, grid=(S_ // tr, W // HEAD_PAD), in_specs=[blk, tab, tab, tab], out_specs=blk,
        out_shape=jax.ShapeDtypeStruct((S_, W), BF16),
        compiler_params=_cparams("parallel", "parallel"), name=name,
    )(q, *tabs)


def k_prep(knv, kv_ext, tabs, name, tr=512):
    S_ = knv.shape[0]
    tr = _pick(S_, tr, 8)

    def body(kn_ref, v_ref, t_ref, c_ref, s1_ref, s2_ref, o_ref, vx_ref):
        o_ref[:, 0:LANES] = kn_ref[...].astype(o_ref.dtype)
        o_ref[:, LANES:2 * LANES] = _rope(t_ref[...], c_ref[...], s1_ref[...], s2_ref[...]).astype(o_ref.dtype)
        vx_ref[:, 0:V_HEAD] = v_ref[...].astype(vx_ref.dtype)
        vx_ref[:, V_HEAD:HEAD_PAD] = jnp.ones((tr, HEAD_PAD - V_HEAD), vx_ref.dtype)

    tab = pl.BlockSpec((tr, LANES), lambda i, h: (i, 0))
    head = pl.BlockSpec((tr, HEAD_PAD), lambda i, h: (i, h))
    shp = jax.ShapeDtypeStruct((S_, N_HEADS * HEAD_PAD), BF16)
    return pl.pallas_call(
        body, grid=(S_ // tr, N_HEADS),
        in_specs=[pl.BlockSpec((tr, LANES), lambda i, h: (i, h)),
                  pl.BlockSpec((tr, V_HEAD), lambda i, h: (i, N_HEADS + h)),
                  pl.BlockSpec((tr, LANES), lambda i, h: (i, KV_RANK // LANES)), tab, tab, tab],
        out_specs=[head, head], out_shape=[shp, shp],
        compiler_params=_cparams("parallel", "parallel"), name=name,
    )(knv, knv, kv_ext, *tabs)


def k_prep_bwd(dk_a, dk_b, dv_a, dv_b, tabs, name, tr=256):
    S_ = dk_a.shape[0]
    tr = _pick(S_, tr, 8)
    HV = N_HEADS * V_HEAD

    def body(ka_ref, kb_ref, va_ref, vb_ref, c_ref, s1_ref, s2_ref, o_ref, t_ref):
        dr = jnp.zeros((tr, LANES), F32)
        for h in range(N_HEADS):
            lo = h * HEAD_PAD
            o_ref[:, h * LANES:(h + 1) * LANES] = (ka_ref[:, lo:lo + LANES] + kb_ref[:, lo:lo + LANES]).astype(o_ref.dtype)
            dr = dr + ka_ref[:, lo + LANES:lo + 2 * LANES] + kb_ref[:, lo + LANES:lo + 2 * LANES]
        o_ref[:, HV:2 * HV] = (va_ref[...] + vb_ref[...]).astype(o_ref.dtype)
        t_ref[...] = _rope_t(dr, c_ref[...], s1_ref[...], s2_ref[...])

    kblk = pl.BlockSpec((tr, N_HEADS * HEAD_PAD), lambda i: (i, 0))
    vblk = pl.BlockSpec((tr, HV), lambda i: (i, 0))
    tab = pl.BlockSpec((tr, LANES), lambda i: (i, 0))
    return pl.pallas_call(
        body, grid=(S_ // tr,), in_specs=[kblk, kblk, vblk, vblk, tab, tab, tab],
        out_specs=[pl.BlockSpec((tr, 2 * HV), lambda i: (i, 0)), tab],
        out_shape=[jax.ShapeDtypeStruct((S_, 2 * HV), BF16), jax.ShapeDtypeStruct((S_, LANES), F32)],
        compiler_params=_cparams("parallel"), name=name,
    )(dk_a, dk_b, dv_a, dv_b, *tabs)


_NEG = -1e30


def attn_fwd(q, k, vx, name):
    S_ = q.shape[0]
    TQ = _pick(S_, ATT_Q_BLOCK, 8)
    TK = _pick(S_, ATT_K_BLOCK, TQ)
    HP = ATT_HEADS_PER_STEP
    W = HP * HEAD_PAD
    ratio = TK // TQ

    def body(q_ref, k_ref, v_ref, o_ref, lse_ref):
        i = pl.program_id(1)
        qs = [q_ref[:, h * HEAD_PAD:(h + 1) * HEAD_PAD] for h in range(HP)]

        def step(j, carry, masked):
            start = pl.multiple_of(j * TK, TK)
            out = []
            for h in range(HP):
                m, acc = carry[h]
                cols = slice(h * HEAD_PAD, (h + 1) * HEAD_PAD)
                s = lax.dot_general(qs[h], k_ref[pl.ds(start, TK), cols], _DIMS["nt"], preferred_element_type=F32)
                if masked:
                    rowi = i * TQ + lax.broadcasted_iota(jnp.int32, (TQ, TK), 0)
                    coli = j * TK + lax.broadcasted_iota(jnp.int32, (TQ, TK), 1)
                    s = jnp.where(coli <= rowi, s, _NEG)
                m_new = jnp.maximum(m, jnp.max(s, axis=-1, keepdims=True))
                alpha = jnp.exp(m - m_new)
                p = jnp.exp(s - m_new).astype(BF16)
                acc = alpha * acc + lax.dot_general(p, v_ref[pl.ds(start, TK), cols], _DIMS["nn"],
                                                    preferred_element_type=F32)
                out.append((m_new, acc))
            return tuple(out)

        init = tuple((jnp.full((TQ, 1), _NEG, F32), jnp.zeros((TQ, HEAD_PAD), F32)) for _ in range(HP))
        last = i // ratio
        carry = step(last, lax.fori_loop(0, last, functools.partial(step, masked=False), init), True)
        for h in range(HP):
            m, acc = carry[h]
            l = acc[:, V_HEAD:]
            o_ref[:, h * V_HEAD:(h + 1) * V_HEAD] = (acc[:, :V_HEAD] / l).astype(o_ref.dtype)
            lse_ref[h] = m + jnp.log(jnp.max(l, axis=-1, keepdims=True))

    return pl.pallas_call(
        body, grid=(N_HEADS // HP, S_ // TQ),
        in_specs=[pl.BlockSpec((TQ, W), lambda g, i: (i, g)),
                  pl.BlockSpec((S_, W), lambda g, i: (0, g)),
                  pl.BlockSpec((S_, W), lambda g, i: (0, g))],
        out_specs=[pl.BlockSpec((TQ, HP * V_HEAD), lambda g, i: (i, g)),
                   pl.BlockSpec((HP, TQ, 1), lambda g, i: (g, i, 0))],
        out_shape=[jax.ShapeDtypeStruct((S_, N_HEADS * V_HEAD), BF16), jax.ShapeDtypeStruct((N_HEADS, S_, 1), F32)],
        compiler_params=_cparams("parallel", "parallel"), name=name,
    )(q, k, vx)


def attn_delta(o, do, name, tr=512):
    S_ = o.shape[0]
    tr = _pick(S_, tr, 8)

    def body(o_ref, do_ref, d_ref):
        d_ref[...] = jnp.sum(o_ref[...].astype(F32) * do_ref[...].astype(F32), axis=-1, keepdims=True)

    blk = pl.BlockSpec((tr, V_HEAD), lambda i, h: (i, h))
    return pl.pallas_call(
        body, grid=(S_ // tr, N_HEADS), in_specs=[blk, blk],
        out_specs=pl.BlockSpec((None, tr, 1), lambda i, h: (h, i, 0)),
        out_shape=jax.ShapeDtypeStruct((N_HEADS, S_, 1), F32),
        compiler_params=_cparams("parallel", "parallel"), name=name,
    )(o, do)


def attn_bwd(q, k, vx, do, lse_row, delta_row, name):
    S_ = q.shape[0]
    TK = _pick(S_, ATT_BWD_K_BLOCK, LANES)
    TQ = _pick(S_, ATT_BWD_Q_BLOCK, TK)
    HP = ATT_HEADS_PER_STEP
    W = HP * HEAD_PAD
    ratio = TQ // TK
    nq = S_ // TQ

    def body(q_ref, do_ref, lse_ref, dl_ref, k_ref, v_ref, dq_ref, dk_ref, dv_ref):
        j = pl.program_id(1)

        @pl.when(j == 0)
        def _():
            dq_ref[...] = jnp.zeros_like(dq_ref)

        ks = [k_ref[:, h * HEAD_PAD:(h + 1) * HEAD_PAD] for h in range(HP)]
        vs = [v_ref[:, h * HEAD_PAD:h * HEAD_PAD + V_HEAD] for h in range(HP)]

        def step(i, carry, masked):
            start = pl.multiple_of(i * TQ, TQ)
            out = []
            for h in range(HP):
                dk, dv = carry[h]
                cols = slice(h * HEAD_PAD, (h + 1) * HEAD_PAD)
                qv = q_ref[pl.ds(start, TQ), cols]
                dov = do_ref[pl.ds(start, TQ), h * V_HEAD:(h + 1) * V_HEAD]
                st = lax.dot_general(ks[h], qv, _DIMS["nt"], preferred_element_type=F32)
                pt = jnp.exp(st - lse_ref[h, :, pl.ds(start, TQ)])
                if masked:
                    keyi = j * TK + lax.broadcasted_iota(jnp.int32, (TK, TQ), 0)
                    qryi = i * TQ + lax.broadcasted_iota(jnp.int32, (TK, TQ), 1)
                    pt = jnp.where(keyi <= qryi, pt, 0.0)
                dpt = lax.dot_general(vs[h], dov, _DIMS["nt"], preferred_element_type=F32)
                dst = (pt * (dpt - dl_ref[h, :, pl.ds(start, TQ)])).astype(BF16)
                dv = dv + lax.dot_general(pt.astype(BF16), dov, _DIMS["nn"], preferred_element_type=F32)
                dk = dk + lax.dot_general(dst, qv, _DIMS["nn"], preferred_element_type=F32)
                dq_ref[pl.ds(start, TQ), cols] += lax.dot_general(dst, ks[h], _DIMS["tn"], preferred_element_type=F32)
                out.append((dk, dv))
            return tuple(out)

        init = tuple((jnp.zeros((TK, HEAD_PAD), F32), jnp.zeros((TK, V_HEAD), F32)) for _ in range(HP))
        first = j // ratio
        carry = lax.fori_loop(first + 1, nq, functools.partial(step, masked=False), step(first, init, True))
        for h in range(HP):
            dk_ref[:, h * HEAD_PAD:(h + 1) * HEAD_PAD] = carry[h][0]
            dv_ref[:, h * V_HEAD:(h + 1) * V_HEAD] = carry[h][1]

    return pl.pallas_call(
        body, grid=(N_HEADS // HP, S_ // TK),
        in_specs=[pl.BlockSpec((S_, W), lambda g, j: (0, g)),
                  pl.BlockSpec((S_, HP * V_HEAD), lambda g, j: (0, g)),
                  pl.BlockSpec((HP, 1, S_), lambda g, j: (g, 0, 0)),
                  pl.BlockSpec((HP, 1, S_), lambda g, j: (g, 0, 0)),
                  pl.BlockSpec((TK, W), lambda g, j: (j, g)),
                  pl.BlockSpec((TK, W), lambda g, j: (j, g))],
        out_specs=[pl.BlockSpec((S_, W), lambda g, j: (0, g)),
                   pl.BlockSpec((TK, W), lambda g, j: (j, g)),
                   pl.BlockSpec((TK, HP * V_HEAD), lambda g, j: (j, g))],
        out_shape=[jax.ShapeDtypeStruct((S_, N_HEADS * HEAD_PAD), F32),
                   jax.ShapeDtypeStruct((S_, N_HEADS * HEAD_PAD), F32),
                   jax.ShapeDtypeStruct((S_, N_HEADS * V_HEAD), F32)],
        compiler_params=_cparams("parallel", "arbitrary"), name=name,
    )(q, do, lse_row, delta_row, k, vx)


def mods_fwd(c_all, mod_w, mod_b, name, tn=512):
    L, Dn, E = mod_w.shape
    R = c_all.shape[0]
    tn = _pick(E, tn, LANES)

    def body(c_ref, w_ref, b_ref, o_ref):
        cv = c_ref[...]
        sc = (cv / (1.0 + jnp.exp(-cv))).astype(BF16)
        o_ref[...] = lax.dot_general(sc, w_ref[...].astype(BF16), _DIMS["nn"], preferred_element_type=F32) + b_ref[...]

    return pl.pallas_call(
        body, grid=(L, E // tn),
        in_specs=[pl.BlockSpec((R, Dn), lambda l, j: (0, 0)), pl.BlockSpec((None, Dn, tn), lambda l, j: (l, 0, j)),
                  pl.BlockSpec((None, 1, tn), lambda l, j: (l, 0, j))],
        out_specs=pl.BlockSpec((None, R, tn), lambda l, j: (l, 0, j)),
        out_shape=jax.ShapeDtypeStruct((L, R, E), F32),
        compiler_params=_cparams("parallel", "parallel"), name=name,
    )(c_all, mod_w, mod_b.reshape(L, 1, E))


def _adam_math(w, g, m, v):
    m = ADAM_B1 * m + (1.0 - ADAM_B1) * g
    v = ADAM_B2 * v + (1.0 - ADAM_B2) * (g * g)
    m_hat = m / (1.0 - ADAM_B1 ** ADAM_STEP)
    v_hat = v / (1.0 - ADAM_B2 ** ADAM_STEP)
    delta = -ADAM_LR * (m_hat / (jnp.sqrt(v_hat) + ADAM_EPS) + ADAM_WD * w)
    return delta, m, v


def _as2d(a):
    return a.reshape(-1, a.shape[-1]) if a.ndim != 2 else a


def adamw(w, g, m, v, name):
    shape = w.shape
    w2, g2, m2, v2 = _as2d(w), _as2d(g), _as2d(m), _as2d(v)
    R, C = w2.shape
    tr = _pick(R, max(8, (1 << 18) // C // 8 * 8), 8)

    def body(w_ref, g_ref, m_ref, v_ref, d_ref, mo_ref, vo_ref):
        d, mn, vn = _adam_math(w_ref[...], g_ref[...], m_ref[...], v_ref[...])
        d_ref[...] = d
        mo_ref[...] = mn
        vo_ref[...] = vn

    blk = pl.BlockSpec((tr, C), lambda i: (i, 0))
    shp = jax.ShapeDtypeStruct((R, C), F32)
    outs = pl.pallas_call(
        body, grid=(R // tr,), in_specs=[blk] * 4, out_specs=[blk] * 3, out_shape=[shp] * 3,
        compiler_params=_cparams("parallel"), name=name,
    )(w2, g2, m2, v2)
    return tuple(o.reshape(shape) for o in outs)


def adamw_sum(parts, w, m, v, name):
    P, R, C = parts.shape

    def body(p_ref, w_ref, m_ref, v_ref, g_ref, d_ref, mo_ref, vo_ref):
        g = p_ref[0]
        for k in range(1, P):
            g = g + p_ref[k]
        d, mn, vn = _adam_math(w_ref[...], g, m_ref[...], v_ref[...])
        g_ref[...] = g
        d_ref[...] = d
        mo_ref[...] = mn
        vo_ref[...] = vn

    shp = jax.ShapeDtypeStruct((R, C), F32)
    return pl.pallas_call(body, out_shape=[shp] * 4, compiler_params=_cparams(), name=name)(parts, w, m, v)


def adamw_modw(c_col, dm, w, m, v, name, tr=256, tn=512):
    L, Dn, E = w.shape
    B = c_col.shape[0]
    tr = _pick(Dn, tr, 8)
    tn = _pick(E, tn, LANES)

    def body(c_ref, dm_ref, w_ref, m_ref, v_ref, g_ref, d_ref, mo_ref, vo_ref):
        g = jnp.zeros((tr, tn), F32)
        for b in range(B):
            cv = c_ref[b]
            g = g + (cv / (1.0 + jnp.exp(-cv))) * dm_ref[b:b + 1, :]
        d, mn, vn = _adam_math(w_ref[...], g, m_ref[...], v_ref[...])
        g_ref[...] = g
        d_ref[...] = d
        mo_ref[...] = mn
        vo_ref[...] = vn

    blk = pl.BlockSpec((None, tr, tn), lambda l, i, j: (l, i, j))
    shp = jax.ShapeDtypeStruct((L, Dn, E), F32)
    return pl.pallas_call(
        body, grid=(L, Dn // tr, E // tn),
        in_specs=[pl.BlockSpec((B, tr, 1), lambda l, i, j: (0, i, 0)),
                  pl.BlockSpec((None, B, tn), lambda l, i, j: (l, 0, j)), blk, blk, blk],
        out_specs=[blk] * 4, out_shape=[shp] * 4,
        compiler_params=_cparams("parallel", "parallel", "parallel"), name=name,
    )(c_col, dm, w, m, v)


def add_round(a, b, name, tr=512):
    R, C = a.shape
    tr = _pick(R, tr, 16)

    def body(a_ref, b_ref, o_ref):
        o_ref[...] = (a_ref[...] + b_ref[...].astype(F32)).astype(BF16)

    blk = pl.BlockSpec((tr, C), lambda i: (i, 0))
    return pl.pallas_call(
        body, grid=(R // tr,), in_specs=[blk, blk], out_specs=blk, out_shape=jax.ShapeDtypeStruct((R, C), BF16),
        compiler_params=_cparams("parallel"), name=name,
    )(a, b)


def sum_parts(parts, name, tr=512):
    P, R, C = parts.shape
    tr = _pick(R, tr, 16)

    def body(p_ref, o_ref):
        s = p_ref[0].astype(F32)
        for k in range(1, P):
            s = s + p_ref[k].astype(F32)
        o_ref[...] = s

    return pl.pallas_call(
        body, grid=(R // tr,), in_specs=[pl.BlockSpec((P, tr, C), lambda i: (0, i, 0))],
        out_specs=pl.BlockSpec((tr, C), lambda i: (i, 0)), out_shape=jax.ShapeDtypeStruct((R, C), F32),
        compiler_params=_cparams("parallel"), name=name,
    )(parts)


_ANY = pl.BlockSpec(memory_space=pl.ANY)


def _place():
    return lax.axis_index("x"), lax.axis_index("y"), lax.axis_index("c")


def _flip(v, bit):
    return 1 - v if bit else v


def chip_gather(buf, name):
    def body(in_ref, out_ref, send_sems, recv_sems):
        x, y, c = _place()
        me = 2 * x + y
        sends = []
        for k in range(1, N_CHIPS):
            px, py = _flip(x, k >> 1), _flip(y, k & 1)
            cp = pltpu.make_async_remote_copy(src_ref=in_ref, dst_ref=out_ref.at[me], send_sem=send_sems.at[k - 1],
                                              recv_sem=recv_sems.at[k - 1], device_id=(px, py, c), device_id_type=MESH)
            cp.start()
            sends.append(cp)
        for k in range(1, N_CHIPS):
            px, py = _flip(x, k >> 1), _flip(y, k & 1)
            pltpu.make_async_remote_copy(src_ref=in_ref, dst_ref=out_ref.at[2 * px + py], send_sem=send_sems.at[k - 1],
                                         recv_sem=recv_sems.at[k - 1], device_id=(px, py, c),
                                         device_id_type=MESH).wait_recv()
        for cp in sends:
            cp.wait_send()

    out = pl.pallas_call(
        body, in_specs=[_ANY], out_specs=_ANY,
        out_shape=jax.ShapeDtypeStruct((N_CHIPS,) + buf.shape, buf.dtype),
        scratch_shapes=[pltpu.SemaphoreType.DMA((N_CHIPS - 1,)), pltpu.SemaphoreType.DMA((N_CHIPS - 1,))],
        name=name,
    )(buf)
    return lax.dynamic_update_index_in_dim(out, buf, 2 * lax.axis_index("x") + lax.axis_index("y"), 0)


def chip_all_to_all(buf, name):
    def body(in_ref, out_ref, send_sems, recv_sems):
        x, y, c = _place()
        me = 2 * x + y
        sends = []
        for k in range(1, N_CHIPS):
            px, py = _flip(x, k >> 1), _flip(y, k & 1)
            cp = pltpu.make_async_remote_copy(src_ref=in_ref.at[2 * px + py], dst_ref=out_ref.at[me],
                                              send_sem=send_sems.at[k - 1], recv_sem=recv_sems.at[k - 1],
                                              device_id=(px, py, c), device_id_type=MESH)
            cp.start()
            sends.append(cp)
        for k in range(1, N_CHIPS):
            px, py = _flip(x, k >> 1), _flip(y, k & 1)
            pltpu.make_async_remote_copy(src_ref=in_ref.at[me], dst_ref=out_ref.at[2 * px + py],
                                         send_sem=send_sems.at[k - 1], recv_sem=recv_sems.at[k - 1],
                                         device_id=(px, py, c), device_id_type=MESH).wait_recv()
        for cp in sends:
            cp.wait_send()

    out = pl.pallas_call(
        body, in_specs=[_ANY], out_specs=_ANY, out_shape=jax.ShapeDtypeStruct(buf.shape, buf.dtype),
        scratch_shapes=[pltpu.SemaphoreType.DMA((N_CHIPS - 1,)), pltpu.SemaphoreType.DMA((N_CHIPS - 1,))],
        name=name,
    )(buf)
    me = 2 * lax.axis_index("x") + lax.axis_index("y")
    return lax.dynamic_update_index_in_dim(out, _index(buf, me), me, 0)


def core_gather(buf, name):
    def body(in_ref, out_ref, send_sem, recv_sem):
        x, y, c = _place()
        cp = pltpu.make_async_remote_copy(src_ref=in_ref, dst_ref=out_ref.at[c], send_sem=send_sem, recv_sem=recv_sem,
                                          device_id=(x, y, 1 - c), device_id_type=MESH)
        cp.start()
        pltpu.make_async_remote_copy(src_ref=in_ref, dst_ref=out_ref.at[1 - c], send_sem=send_sem, recv_sem=recv_sem,
                                     device_id=(x, y, 1 - c), device_id_type=MESH).wait_recv()
        cp.wait_send()

    out = pl.pallas_call(
        body, in_specs=[_ANY], out_specs=_ANY, out_shape=jax.ShapeDtypeStruct((2,) + buf.shape, buf.dtype),
        scratch_shapes=[pltpu.SemaphoreType.DMA, pltpu.SemaphoreType.DMA],
        name=name,
    )(buf)
    return lax.dynamic_update_index_in_dim(out, buf, lax.axis_index("c"), 0)


def core_swap(buf, name):
    def body(in_ref, out_ref, send_sem, recv_sem):
        x, y, c = _place()
        cp = pltpu.make_async_remote_copy(src_ref=in_ref, dst_ref=out_ref, send_sem=send_sem, recv_sem=recv_sem,
                                          device_id=(x, y, 1 - c), device_id_type=MESH)
        cp.start()
        cp.wait()

    return pl.pallas_call(
        body, in_specs=[_ANY], out_specs=_ANY, out_shape=jax.ShapeDtypeStruct(buf.shape, buf.dtype),
        scratch_shapes=[pltpu.SemaphoreType.DMA, pltpu.SemaphoreType.DMA],
        name=name,
    )(buf)


def device_gather(buf, name):
    def body(in_ref, out_ref, send_sems, recv_sems, local_sem):
        x, y, c = _place()
        me = 4 * x + 2 * y + c
        mine = pltpu.make_async_copy(in_ref, out_ref.at[me], local_sem)
        mine.start()
        sends = []
        for k in range(1, N_DEV):
            peer = (_flip(x, (k >> 2) & 1), _flip(y, (k >> 1) & 1), _flip(c, k & 1))
            cp = pltpu.make_async_remote_copy(src_ref=in_ref, dst_ref=out_ref.at[me], send_sem=send_sems.at[k - 1],
                                              recv_sem=recv_sems.at[k - 1], device_id=peer, device_id_type=MESH)
            cp.start()
            sends.append(cp)
        for k in range(1, N_DEV):
            peer = (_flip(x, (k >> 2) & 1), _flip(y, (k >> 1) & 1), _flip(c, k & 1))
            pltpu.make_async_remote_copy(src_ref=in_ref, dst_ref=out_ref.at[4 * peer[0] + 2 * peer[1] + peer[2]],
                                         send_sem=send_sems.at[k - 1], recv_sem=recv_sems.at[k - 1], device_id=peer,
                                         device_id_type=MESH).wait_recv()
        for cp in sends:
            cp.wait_send()
        mine.wait()

    return pl.pallas_call(
        body, in_specs=[_ANY], out_specs=_ANY, out_shape=jax.ShapeDtypeStruct((N_DEV,) + buf.shape, buf.dtype),
        scratch_shapes=[pltpu.SemaphoreType.DMA((N_DEV - 1,)), pltpu.SemaphoreType.DMA((N_DEV - 1,)),
                        pltpu.SemaphoreType.DMA],
        name=name,
    )(buf)


def _region(ref, chip_axis=None, chip=None, chip_size=None, half_axis=None, half=None, half_size=None):
    idx = [slice(None)] * len(ref.shape)
    if chip is not None:
        idx[chip_axis] = pl.ds(chip * chip_size, chip_size)
    if half is not None:
        idx[half_axis] = pl.ds(half * half_size, half_size)
    return ref.at[tuple(idx)]


def gather_weights(shards, axes, name):
    n = len(shards)

    def full_shape(t):
        shp = list(shards[t].shape)
        shp[axes[t][0]] *= N_CHIPS
        return tuple(shp)

    def body(*refs):
        ins, outs = refs[:n], refs[n:2 * n]
        ici_send, ici_recv, d2d_send, d2d_recv, own_send, own_recv = refs[2 * n:]
        x, y, c = _place()
        me = 2 * x + y

        def part(t, ref, chip, half):
            ca, ha = axes[t]
            return _region(ref, ca, chip, ins[t].shape[ca], ha, half, ins[t].shape[ha] // 2)

        def own(t):
            return pltpu.make_async_remote_copy(src_ref=ins[t], dst_ref=part(t, outs[t], me, None),
                                                send_sem=own_send.at[t], recv_sem=own_recv.at[t],
                                                device_id=(x, y, 1 - c), device_id_type=MESH)

        started = []
        for t in range(n):
            own(t).start()
            started.append(own(t))
        for t in range(n):
            for k in range(1, N_CHIPS):
                px, py = _flip(x, k >> 1), _flip(y, k & 1)
                cp = pltpu.make_async_remote_copy(src_ref=part(t, ins[t], None, c), dst_ref=part(t, outs[t], me, c),
                                                  send_sem=ici_send.at[t, k - 1], recv_sem=ici_recv.at[t, k - 1],
                                                  device_id=(px, py, c), device_id_type=MESH)
                cp.start()
                started.append(cp)
        for t in range(n):
            for k in range(1, N_CHIPS):
                px, py = _flip(x, k >> 1), _flip(y, k & 1)
                got = part(t, outs[t], 2 * px + py, c)
                pltpu.make_async_remote_copy(src_ref=part(t, ins[t], None, c), dst_ref=got,
                                             send_sem=ici_send.at[t, k - 1], recv_sem=ici_recv.at[t, k - 1],
                                             device_id=(px, py, c), device_id_type=MESH).wait_recv()
                fw = pltpu.make_async_remote_copy(src_ref=got, dst_ref=got, send_sem=d2d_send.at[t, k - 1],
                                                  recv_sem=d2d_recv.at[t, k - 1], device_id=(x, y, 1 - c),
                                                  device_id_type=MESH)
                fw.start()
                started.append(fw)
        for t in range(n):
            for k in range(1, N_CHIPS):
                px, py = _flip(x, k >> 1), _flip(y, k & 1)
                theirs = part(t, outs[t], 2 * px + py, 1 - c)
                pltpu.make_async_remote_copy(src_ref=theirs, dst_ref=theirs, send_sem=d2d_send.at[t, k - 1],
                                             recv_sem=d2d_recv.at[t, k - 1], device_id=(x, y, 1 - c),
                                             device_id_type=MESH).wait_recv()
        for t in range(n):
            own(t).wait_recv()
        for cp in started:
            cp.wait_send()

    sem = pltpu.SemaphoreType.DMA((n, N_CHIPS - 1))
    own_sem = pltpu.SemaphoreType.DMA((n,))
    return pl.pallas_call(
        body, in_specs=[_ANY] * n, out_specs=[_ANY] * n,
        out_shape=[jax.ShapeDtypeStruct(full_shape(t), shards[t].dtype) for t in range(n)],
        scratch_shapes=[sem, sem, sem, sem, own_sem, own_sem], name=name,
    )(*shards)


def reduce_to_sibling(lo, hi, name):
    n = len(lo)

    def body(*refs):
        los, his, outs = refs[:n], refs[n:2 * n], refs[2 * n:3 * n]
        send_sems, recv_sems = refs[3 * n:]
        x, y, c = _place()

        def copy(u, src):
            return pltpu.make_async_remote_copy(src_ref=src, dst_ref=outs[u], send_sem=send_sems.at[u],
                                                recv_sem=recv_sems.at[u], device_id=(x, y, 1 - c), device_id_type=MESH)

        for u in range(n):
            @pl.when(c == 0)
            def _(u=u):
                copy(u, his[u]).start()

            @pl.when(c == 1)
            def _(u=u):
                copy(u, los[u]).start()
        for u in range(n):
            copy(u, los[u]).wait_recv()
        for u in range(n):
            copy(u, los[u]).wait_send()

    return pl.pallas_call(
        body, in_specs=[_ANY] * (2 * n), out_specs=[_ANY] * n,
        out_shape=[jax.ShapeDtypeStruct(a.shape, a.dtype) for a in lo],
        scratch_shapes=[pltpu.SemaphoreType.DMA((n,)), pltpu.SemaphoreType.DMA((n,))], name=name,
    )(*lo, *hi)


def add_selected(lo, hi, other, name, tile_elems=1 << 19):
    R, C = lo.shape
    tr = _pick(R, max(16, tile_elems // C // 16 * 16), 16)

    def body(lo_ref, hi_ref, o_ref, out_ref):
        mine = jnp.where(lax.axis_index("c") == 0, lo_ref[...].astype(F32), hi_ref[...].astype(F32))
        out_ref[...] = (mine + o_ref[...].astype(F32)).astype(out_ref.dtype)

    blk = pl.BlockSpec((tr, C), lambda i: (i, 0))
    return pl.pallas_call(
        body, grid=(R // tr,), in_specs=[blk, blk, blk], out_specs=blk, out_shape=jax.ShapeDtypeStruct((R, C), BF16),
        compiler_params=_cparams("parallel"), name=name,
    )(lo, hi, other)


def scatter_to_chips(pieces, chip_axes, name):
    n = len(pieces)

    def block_shape(u):
        shp = list(pieces[u].shape)
        shp[chip_axes[u]] //= N_CHIPS
        return tuple(shp)

    def body(*refs):
        ins, outs = refs[:n], refs[n:2 * n]
        send_sems, recv_sems = refs[2 * n:]
        x, y, c = _place()
        me = 2 * x + y
        started = []
        for u in range(n):
            size = block_shape(u)[chip_axes[u]]
            for k in range(1, N_CHIPS):
                px, py = _flip(x, k >> 1), _flip(y, k & 1)
                cp = pltpu.make_async_remote_copy(src_ref=_region(ins[u], chip_axes[u], 2 * px + py, size),
                                                  dst_ref=outs[u].at[me], send_sem=send_sems.at[u, k - 1],
                                                  recv_sem=recv_sems.at[u, k - 1], device_id=(px, py, c),
                                                  device_id_type=MESH)
                cp.start()
                started.append(cp)
        for u in range(n):
            size = block_shape(u)[chip_axes[u]]
            for k in range(1, N_CHIPS):
                px, py = _flip(x, k >> 1), _flip(y, k & 1)
                pltpu.make_async_remote_copy(src_ref=_region(ins[u], chip_axes[u], me, size),
                                             dst_ref=outs[u].at[2 * px + py], send_sem=send_sems.at[u, k - 1],
                                             recv_sem=recv_sems.at[u, k - 1], device_id=(px, py, c),
                                             device_id_type=MESH).wait_recv()
        for cp in started:
            cp.wait_send()

    sem = pltpu.SemaphoreType.DMA((n, N_CHIPS - 1))
    return pl.pallas_call(
        body, in_specs=[_ANY] * n, out_specs=[_ANY] * n,
        out_shape=[jax.ShapeDtypeStruct((N_CHIPS,) + block_shape(u), pieces[u].dtype) for u in range(n)],
        scratch_shapes=[sem, sem], name=name,
    )(*pieces)


def gather_halves(parts, slots, out_shapes, name):
    n = len(parts)

    def body(*refs):
        ins, outs = refs[:n], refs[n:n + len(out_shapes)]
        send_sems, recv_sems = refs[n + len(out_shapes):]
        x, y, c = _place()
        started = []
        for u in range(n):
            t, s = slots[u]
            cp = pltpu.make_async_remote_copy(src_ref=ins[u], dst_ref=outs[t].at[c, s], send_sem=send_sems.at[u],
                                              recv_sem=recv_sems.at[u], device_id=(x, y, 1 - c), device_id_type=MESH)
            cp.start()
            started.append(cp)
        for u in range(n):
            t, s = slots[u]
            pltpu.make_async_remote_copy(src_ref=ins[u], dst_ref=outs[t].at[1 - c, s], send_sem=send_sems.at[u],
                                         recv_sem=recv_sems.at[u], device_id=(x, y, 1 - c),
                                         device_id_type=MESH).wait_recv()
        for cp in started:
            cp.wait_send()

    return pl.pallas_call(
        body, in_specs=[_ANY] * n, out_specs=[_ANY] * len(out_shapes),
        out_shape=[jax.ShapeDtypeStruct(shp, F32) for shp in out_shapes],
        scratch_shapes=[pltpu.SemaphoreType.DMA((n,)), pltpu.SemaphoreType.DMA((n,))], name=name,
    )(*parts)


WEIGHT_ORDER = ["mod_w", "mod_b", "norm1_g", "norm2_g", "pool_w", "pool_b", "pool_scale", "kv_in_g", "w_dkv",
                "ckv_norm_g", "w_uk", "w_uv", "w_dq", "q_norm_g", "w_uq", "w_o", "w_up", "conv_w", "conv_b", "w_down",
                "final_g"]
EXCHANGED = {"w_up": (2, 0), "w_down": (1, 0), "w_o": (1, 0), "w_uq": (2, 0), "w_dq": (1, 0), "pool_w": (2, 0),
             "w_dkv": (0, 1), "w_uk": (1, 0), "w_uv": (1, 0)}
SMALL_SHARDED = {"conv_w": 2, "pool_b": 1, "pool_scale": 1}
REPLICATED = ["mod_b", "norm1_g", "norm2_g", "kv_in_g", "ckv_norm_g", "q_norm_g", "conv_b", "final_g"]


def _padded(n, align):
    return -(-n // align) * align


def _flat_pad(parts, total):
    flat = jnp.concatenate(parts, axis=-1)
    pad = total - flat.shape[-1]
    if pad:
        flat = jnp.concatenate([flat, jnp.zeros(flat.shape[:-1] + (pad,), flat.dtype)], axis=-1)
    return flat


def _split_shards(full, axis):
    shp = full.shape
    t = full.reshape(shp[:axis] + (N_CHIPS, shp[axis] // N_CHIPS) + shp[axis + 1:])
    return jnp.moveaxis(t, axis, 0).reshape(N_CHIPS, -1)


def _join_shards(rows, shard_shape, axis):
    t = jnp.moveaxis(rows.reshape((N_CHIPS,) + tuple(shard_shape)), 0, axis)
    return t.reshape(tuple(shard_shape[:axis]) + (N_CHIPS * shard_shape[axis],) + tuple(shard_shape[axis + 1:]))


def _index(a, i, axis=0):
    return lax.dynamic_index_in_dim(a, i, axis, keepdims=False)


def kernel(x, c, positions, mod_w, mod_b, norm1_g, norm2_g, pool_w, pool_b, pool_scale, kv_in_g, w_dkv, ckv_norm_g, w_uk, w_uv, w_dq, q_norm_g, w_uq, w_o, w_up, conv_w, conv_b, w_down, final_g, loss_target, m_mod_w, m_mod_b, m_norm1_g, m_norm2_g, m_pool_w, m_pool_b, m_pool_scale, m_kv_in_g, m_w_dkv, m_ckv_norm_g, m_w_uk, m_w_uv, m_w_dq, m_q_norm_g, m_w_uq, m_w_o, m_w_up, m_conv_w, m_conv_b, m_w_down, m_final_g, v_mod_w, v_mod_b, v_norm1_g, v_norm2_g, v_pool_w, v_pool_b, v_pool_scale, v_kv_in_g, v_w_dkv, v_ckv_norm_g, v_w_uk, v_w_uv, v_w_dq, v_q_norm_g, v_w_uq, v_w_o, v_w_up, v_conv_w, v_conv_b, v_w_down, v_final_g):
    given = dict(locals())
    W = {n: given[n] for n in WEIGHT_ORDER}
    M1 = {n: given["m_" + n] for n in WEIGHT_ORDER}
    V2 = {n: given["v_" + n] for n in WEIGHT_ORDER}
    xi, yi, ci = lax.axis_index("x"), lax.axis_index("y"), lax.axis_index("c")
    chip = 2 * xi + yi
    dev = 4 * xi + 2 * yi + ci
    x0 = x[0]
    S_, D = x0.shape
    Fh = conv_b.shape[1]
    E = mod_b.shape[1]
    Es = E // N_CHIPS
    zD = jnp.zeros((D,), F32)

    def exchange_view(n, a):
        return a.reshape(1, 2, a.shape[0] // 2, a.shape[1]) if n == "w_dkv" else a

    names = list(EXCHANGED)
    shards = [exchange_view(n, W[n].astype(BF16)) for n in names]
    gathered = gather_weights(shards, [EXCHANGED[n] for n in names], "gather_weights")
    full = dict(zip(names, gathered))
    full["w_dkv"] = full["w_dkv"].reshape(D, W["w_dkv"].shape[1])
    ssz = {n: math.prod(W[n].shape) for n in SMALL_SHARDED}
    Tw = _padded(sum(ssz.values()), 8 * PACK_COLS)
    small_rows = chip_gather(_flat_pad([W[n].reshape(-1) for n in SMALL_SHARDED], Tw).reshape(-1, PACK_COLS),
                             "gather_small_w").reshape(N_CHIPS, Tw)
    off = 0
    for n, axis in SMALL_SHARDED.items():
        full[n] = _join_shards(small_rows[:, off:off + ssz[n]], W[n].shape, axis)
        off += ssz[n]

    n_mla = DEPTH - N_A
    q_rank = full["w_uq"].shape[1]
    wq = full["w_uq"].reshape(n_mla, q_rank, N_HEADS, QK_HEAD)
    w_uq_ext = jnp.concatenate([wq, jnp.zeros((n_mla, q_rank, N_HEADS, HEAD_PAD - QK_HEAD), BF16)],
                               axis=3).reshape(n_mla, q_rank, N_HEADS * HEAD_PAD)
    kv_w = KV_RANK + QK_ROPE
    w_dkv_ext = jnp.concatenate([full["w_dkv"], jnp.zeros((D, KV_RANK + LANES - kv_w), BF16)], axis=1)
    w_ukv = jnp.concatenate([full["w_uk"], full["w_uv"]], axis=1)

    c_all = device_gather(c, "gather_c").reshape(N_DEV, D)
    c_pad = jnp.concatenate([c_all, jnp.zeros((16 - N_DEV, D), F32)], axis=0)
    mod_b_mine = lax.dynamic_slice_in_dim(mod_b, chip * Es, Es, axis=1)
    mods_part = mods_fwd(c_pad, mod_w, mod_b_mine, "mods_fwd")
    mods_all = chip_gather(mods_part, "gather_mods")
    mods = jnp.swapaxes(_index(mods_all, dev, axis=2), 0, 1).reshape(DEPTH, E)
    mod = [[mods[l, k * D:(k + 1) * D] for k in range(6)] for l in range(DEPTH)]

    half = QK_ROPE // 2
    inv = 1.0 / (ROPE_THETA ** (jnp.arange(0, QK_ROPE, 2, dtype=F32) / QK_ROPE))
    inv_row = jnp.concatenate([inv, inv, jnp.zeros((LANES - 2 * half,), F32)]).reshape(1, LANES)
    tabs = rope_tables(positions[0].astype(F32).reshape(S_, 1), inv_row, "rope_tables")
    att_scale = QK_HEAD ** -0.5

    saved = []
    xcur = x0
    kv_saved = None
    K = VX = knv = None
    for l in range(DEPTH):
        sh1, sc1, g1, sh2, sc2, g2 = mod[l]
        st = {"xin": xcur}
        if l < N_A:
            h1 = norm_fwd(xcur, norm1_g[l], sc1, sh1, F32, f"norm1_fwd{l}")
            st["pooled"] = _pool_call(h1, BF16, f"pool_fwd{l}", False)
            st["cs"] = g1 * full["pool_scale"][l]
            st["ypre"], xmid = gmm(st["pooled"], full["pool_w"][l], "nn", F32, f"pool_mm{l}", bias=full["pool_b"][l],
                                   res=xcur, colscale=st["cs"])
        else:
            j = l - N_A
            st["h1"] = norm_fwd(xcur, norm1_g[l], sc1, sh1, BF16, f"norm1_fwd{l}")
            st["ql"] = mm(st["h1"], full["w_dq"], "nn", F32, f"dq_mm{l}", layer=j)
            st["cq"] = norm_fwd(st["ql"], q_norm_g[j], jnp.zeros_like(q_norm_g[j]), jnp.zeros_like(q_norm_g[j]), BF16,
                                f"qnorm_fwd{l}")
            qe = mm(st["cq"], w_uq_ext, "nn", F32, f"uq_mm{l}", layer=j)
            st["Q"] = q_prep(qe, tabs, att_scale, False, f"q_prep{l}")
            st["o"], lse = attn_fwd(st["Q"], K, VX, f"attn_fwd{l}")
            st["lse"] = lse.reshape(N_HEADS, 1, S_)
            st["y"], xmid = mm(st["o"], full["w_o"], "nn", F32, f"wo_mm{l}", res=xcur, colscale=g1, layer=j)
        st["xmid"] = xmid
        st["h2"] = norm_fwd(xmid, norm2_g[l], sc2, sh2, BF16, f"norm2_fwd{l}")
        st["u"] = mm(st["h2"], full["w_up"], "nn", BF16, f"up_mm{l}", layer=l)
        st["z"] = glu_fwd(st["u"], full["conv_w"][l], conv_b[l], f"glu_fwd{l}")
        st["f"], xcur = mm(st["z"], full["w_down"], "nn", F32, f"down_mm{l}", tk=1408, res=xmid, colscale=g2, layer=l)
        saved.append(st)
        if l == N_A - 1:
            xn = norm_fwd(xcur, kv_in_g, zD, zD, BF16, "kvin_fwd")
            kv_ext = mm(xn, w_dkv_ext, "nn", F32, "dkv_mm")
            lat = kv_ext[:, :KV_RANK]
            zk = jnp.zeros((KV_RANK,), F32)
            ckv = norm_fwd(lat, ckv_norm_g, zk, zk, BF16, "ckv_fwd")
            knv = mm(ckv, w_ukv, "nn", BF16, "ukv_mm")
            K, VX = k_prep(knv, kv_ext, tabs, "k_prep")
            kv_saved = {"x": xcur, "xn": xn, "lat": lat, "ckv": ckv}

    dx, d_final_g, loss_part = loss_head(xcur, final_g, loss_target[0], "loss_head")
    loss = lax.psum(loss_part[0, 0], ("x", "y", "c"))

    G = {}
    dmods = [None] * DEPTH
    d_norm1 = [None] * DEPTH
    d_norm2 = [None] * DEPTH
    d_conv_b = [None] * DEPTH
    d_qnorm = [None] * n_mla
    dkv_acc = []
    for l in reversed(range(DEPTH)):
        sh1, sc1, g1, sh2, sc2, g2 = mod[l]
        st = saved[l]
        df, a2, _ = gate_bwd(dx, st["f"], g2, f"gate2_bwd{l}")
        dz = mm(df, full["w_down"], "nt", BF16, f"down_dx{l}", layer=l)
        G[("w_down", l)] = mm(st["z"], df, "tn", BF16, f"down_dw{l}")
        du, dcw, dcb = glu_bwd(st["u"], dz, full["conv_w"][l], conv_b[l], f"glu_bwd{l}")
        G[("conv_w", l)] = dcw
        d_conv_b[l] = dcb[0]
        dh2 = mm(du, full["w_up"], "nt", BF16, f"up_dx{l}", tk=1408, layer=l)
        G[("w_up", l)] = mm(st["h2"], du, "tn", BF16, f"up_dw{l}")
        dxmid, s1, s2 = norm_bwd(st["xmid"], norm2_g[l], sc2, dh2, dx, f"norm2_bwd{l}")
        dsh2, dsc2, d_norm2[l] = s1[0], s2[0] * norm2_g[l], s2[0] * (1.0 + sc2)
        if l < N_A:
            dyp, a1, csum = gate_bwd(dxmid, st["ypre"], st["cs"], f"gate1_bwd{l}")
            dg1 = full["pool_scale"][l] * a1[0]
            G[("pool_scale", l)] = g1 * a1[0]
            G[("pool_b", l)] = st["cs"] * csum[0]
            dpooled = gmm(dyp, full["pool_w"][l], "nt", F32, f"pool_dx{l}")
            G[("pool_w", l)] = gmm(st["pooled"], dyp, "tn", BF16, f"pool_dw{l}")
            dh1 = _pool_call(dpooled, F32, f"pool_bwd{l}", True)
        else:
            j = l - N_A
            dy, a1, _ = gate_bwd(dxmid, st["y"], g1, f"gate1_bwd{l}")
            dg1 = a1[0]
            do = mm(dy, full["w_o"], "nt", BF16, f"wo_dx{l}", layer=j)
            G[("w_o", j)] = mm(st["o"], dy, "tn", BF16, f"wo_dw{l}")
            delta = attn_delta(st["o"], do, f"attn_delta{l}").reshape(N_HEADS, 1, S_)
            dQ, dK, dV = attn_bwd(st["Q"], K, VX, do, st["lse"], delta, f"attn_bwd{l}")
            dkv_acc.append((dK, dV))
            dqe = q_prep(dQ, tabs, att_scale, True, f"q_prep_bwd{l}")
            dcq = mm(dqe, w_uq_ext, "nt", F32, f"uq_dx{l}", layer=j)
            G[("w_uq", j)] = mm(st["cq"], dqe, "tn", BF16, f"uq_dw{l}").reshape(q_rank, N_HEADS, HEAD_PAD)[
                :, :, :QK_HEAD].reshape(q_rank, N_HEADS * QK_HEAD)
            zq = jnp.zeros_like(q_norm_g[j])
            dql, _, s2q = norm_bwd(st["ql"], q_norm_g[j], zq, dcq, None, f"qnorm_bwd{l}")
            d_qnorm[j] = s2q[0]
            dh1 = mm(dql, full["w_dq"], "nt", BF16, f"dq_dx{l}", layer=j)
            G[("w_dq", j)] = mm(st["h1"], dql, "tn", BF16, f"dq_dw{l}")
        dx, s1, s2 = norm_bwd(st["xin"], norm1_g[l], sc1, dh1, dxmid, f"norm1_bwd{l}")
        dsh1, dsc1, d_norm1[l] = s1[0], s2[0] * norm1_g[l], s2[0] * (1.0 + sc1)
        dmods[l] = jnp.concatenate([dsh1, dsc1, dg1, dsh2, dsc2, a2[0]])
        if l == N_A:
            (dk_a, dv_a), (dk_b, dv_b) = dkv_acc
            dknv, d_tk = k_prep_bwd(dk_a, dk_b, dv_a, dv_b, tabs, "k_prep_bwd")
            dckv = mm(dknv, w_ukv, "nt", F32, "ukv_dx")
            d_ukv = mm(kv_saved["ckv"], dknv, "tn", BF16, "ukv_dw")
            G[("w_uk", 0)], G[("w_uv", 0)] = d_ukv[:, :N_HEADS * QK_NOPE], d_ukv[:, N_HEADS * QK_NOPE:]
            zk = jnp.zeros((KV_RANK,), F32)
            dlat, _, s2c = norm_bwd(kv_saved["lat"], ckv_norm_g, zk, dckv, None, "ckv_bwd")
            d_ckv_g = s2c[0]
            dkv_ext = jnp.concatenate([dlat, d_tk], axis=1)
            dxn = mm(dkv_ext, w_dkv_ext, "nt", BF16, "dkv_dx")
            G[("w_dkv", 0)] = mm(kv_saved["xn"], dkv_ext, "tn", BF16, "dkv_dw")[:, :kv_w]
            dx, _, s2k = norm_bwd(kv_saved["x"], kv_in_g, zD, dxn, dx, "kvin_bwd")
            d_kvin_g = s2k[0]

    units = []
    for n, (ca, ha) in EXCHANGED.items():
        if W[n].ndim > 2:
            half_layers = W[n].shape[0] // 2
            for sl in range(half_layers):
                units.append((n, sl, G[(n, sl)], G[(n, half_layers + sl)], ca - 1))
        elif n == "w_dkv":
            g4 = G[(n, 0)].reshape(N_CHIPS, 2, -1, kv_w)
            units.append((n, 0, g4[:, 0], g4[:, 1], 0))
        else:
            rows_half = W[n].shape[0] // 2
            units.append((n, 0, G[(n, 0)][:rows_half], G[(n, 0)][rows_half:], ca))
    lo = [u[2] for u in units]
    hi = [u[3] for u in units]
    theirs = reduce_to_sibling(lo, hi, "reduce_cores")

    def flat2(a):
        return a.reshape(-1, a.shape[-1])

    sums = [add_selected(flat2(l_), flat2(h_), flat2(t_), f"reduce_cores_add{i}").reshape(l_.shape)
            for i, (l_, h_, t_) in enumerate(zip(lo, hi, theirs))]
    axes = [u[4] for u in units]
    got = scatter_to_chips(sums, axes, "reduce_chips")
    reduced = []
    for i, (sm, ax, g4) in enumerate(zip(sums, axes, got)):
        size = sm.shape[ax] // N_CHIPS
        g4 = lax.dynamic_update_index_in_dim(g4, lax.dynamic_slice_in_dim(sm, chip * size, size, axis=ax), chip, 0)
        blk = g4.shape[1:]
        reduced.append(sum_parts(g4.reshape(N_CHIPS, -1, blk[-1]), f"reduce_chips_add{i}").reshape(blk))
    slots, out_shapes = [], []
    for n in EXCHANGED:
        mine = [i for i, u in enumerate(units) if u[0] == n]
        out_shapes.append((2, len(mine)) + reduced[mine[0]].shape)
        slots += [(len(out_shapes) - 1, units[i][1]) for i in mine]
    halves = gather_halves(reduced, slots, out_shapes, "reduce_gather")

    grads, deltas, new_m, new_v = {}, {}, {}, {}
    for ti, n in enumerate(EXCHANGED):
        g = halves[ti]
        for i, u in enumerate(units):
            if u[0] == n:
                g = lax.dynamic_update_slice(g, reduced[i][None, None], (ci, u[1]) + (0,) * reduced[i].ndim)
        grads[n] = g.reshape(W[n].shape)
        deltas[n], new_m[n], new_v[n] = adamw(W[n], grads[n], M1[n], V2[n], f"adamw_{n}")

    small = {"mod_b": jnp.stack(dmods), "norm1_g": jnp.stack(d_norm1), "norm2_g": jnp.stack(d_norm2),
             "kv_in_g": d_kvin_g, "ckv_norm_g": d_ckv_g, "q_norm_g": jnp.stack(d_qnorm),
             "conv_b": jnp.stack(d_conv_b), "final_g": d_final_g[0]}
    extra = {n: jnp.stack([G[(n, i)] for i in range(W[n].shape[0])]) for n in SMALL_SHARDED}
    ssizes = {n: math.prod(W[n].shape) for n in REPLICATED}
    esizes = {n: math.prod(extra[n].shape) for n in SMALL_SHARDED}
    Ts = _padded(sum(ssizes.values()) + sum(esizes.values()), 8 * PACK_COLS)

    def pack_small(d, tail=()):
        return _flat_pad([d[n].reshape(-1) for n in REPLICATED] + [t.reshape(-1) for t in tail],
                         Ts).reshape(Ts // PACK_COLS, PACK_COLS)

    parts = device_gather(pack_small(small, [extra[n] for n in SMALL_SHARDED]), "gather_small")
    outs = adamw_sum(parts, pack_small(W), pack_small(M1), pack_small(V2), "adamw_small")
    off = 0
    for n in REPLICATED:
        for dst, o in zip((grads, deltas, new_m, new_v), outs):
            dst[n] = o.reshape(-1)[off:off + ssizes[n]].reshape(W[n].shape)
        off += ssizes[n]
    for n, axis in SMALL_SHARDED.items():
        g_full = outs[0].reshape(-1)[off:off + esizes[n]].reshape(extra[n].shape)
        off += esizes[n]
        size = W[n].shape[axis]
        grads[n] = lax.dynamic_slice_in_dim(g_full, chip * size, size, axis=axis)
        deltas[n], new_m[n], new_v[n] = adamw(W[n], grads[n], M1[n], V2[n], f"adamw_{n}")

    dm_all = parts.reshape(N_DEV, -1)[:, :DEPTH * E].reshape(N_DEV, DEPTH, E)
    dm_mine = jnp.swapaxes(lax.dynamic_slice_in_dim(dm_all, chip * Es, Es, axis=2), 0, 1)
    grads["mod_w"], deltas["mod_w"], new_m["mod_w"], new_v["mod_w"] = adamw_modw(
        c_all.reshape(N_DEV, D, 1), dm_mine, mod_w, m_mod_w, v_mod_w, "adamw_mod_w")

    return (loss, dx.reshape(x.shape), *[grads[n] for n in WEIGHT_ORDER], *[deltas[n] for n in WEIGHT_ORDER],
            *[new_m[n] for n in WEIGHT_ORDER], *[new_v[n] for n in WEIGHT_ORDER])
```

```python
import functools
import math

import jax
import jax.numpy as jnp
from jax import lax
from jax.experimental import pallas as pl
from jax.experimental.pallas import tpu as pltpu

F32 = jnp.float32
BF16 = jnp.bfloat16
MESH = pl.DeviceIdType.MESH

DEPTH = 4
N_A = 2
POOL_WINDOWS = (2, 4, 8, 16)
N_GROUPS = 4
N_HEADS = 8
QK_NOPE = 128
QK_ROPE = 64
V_HEAD = 128
QK_HEAD = QK_NOPE + QK_ROPE
HEAD_PAD = 256
KV_RANK = 256
ROPE_THETA = 10000.0
EPS = 1e-6
ADAM_LR = 0.001
ADAM_B1 = 0.9
ADAM_B2 = 0.999
ADAM_EPS = 1e-08
ADAM_WD = 0.01
ADAM_STEP = 10

N_CHIPS = 4
N_DEV = 8
LANES = 128
PACK_COLS = 1024
VMEM_LIMIT = 56 * 1024 * 1024
GLU_TILE = 256
ATT_BWD_K_BLOCK = 256
ATT_BWD_Q_BLOCK = 512
ATT_Q_BLOCK = 256
ATT_K_BLOCK = 512
ATT_HEADS_PER_STEP = 2


def _cparams(*sem):
    return pltpu.CompilerParams(dimension_semantics=sem if sem else None, vmem_limit_bytes=VMEM_LIMIT)


def _pick(n, target, mult):
    best = None
    d = mult
    while d <= min(n, target):
        if n % d == 0:
            best = d
        d += mult
    return n if best is None else best


def _row(v):
    return v.reshape(1, -1).astype(F32)


_DIMS = {"nn": (((1,), (0,)), ((), ())), "nt": (((1,), (1,)), ((), ())), "tn": (((0,), (0,)), ((), ()))}


def _mm_body(mode, nk, has_bias, has_res):
    def body(*refs):
        a_ref, b_ref = refs[0], refs[1]
        pos = 2
        bias_ref = res_ref = cs_ref = None
        if has_bias:
            bias_ref = refs[pos]
            pos += 1
        if has_res:
            res_ref, cs_ref = refs[pos], refs[pos + 1]
            pos += 2
        o_ref = refs[pos]
        pos += 1
        o2_ref = None
        if has_res:
            o2_ref = refs[pos]
            pos += 1
        acc_ref = refs[pos] if nk > 1 else None
        k = pl.program_id(2)
        part = lax.dot_general(a_ref[...].astype(BF16), b_ref[...].astype(BF16), _DIMS[mode],
                               preferred_element_type=F32)

        def finish(y):
            if has_bias:
                y = y + bias_ref[...]
            o_ref[...] = y.astype(o_ref.dtype)
            if has_res:
                o2_ref[...] = res_ref[...] + cs_ref[...] * y

        if nk == 1:
            finish(part)
            return

        @pl.when(k == 0)
        def _():
            acc_ref[...] = part

        @pl.when((k > 0) & (k < nk - 1))
        def _():
            acc_ref[...] += part

        @pl.when(k == nk - 1)
        def _():
            finish(acc_ref[...] + part)

    return body


def mm(a, b, mode, out_dtype, name, *, tm=1408, tn=1408, tk=1024, bias=None, res=None, colscale=None, layer=None):
    bshape = b.shape if layer is None else b.shape[1:]
    if mode == "nn":
        (M, K), N = a.shape, bshape[1]
    elif mode == "nt":
        (M, K), N = a.shape, bshape[0]
    else:
        (K, M), N = a.shape, bshape[1]
    tm = _pick(M, tm, LANES if mode == "tn" else 8)
    tn = _pick(N, tn, LANES)
    tk = _pick(K, tk, LANES) if mode != "tn" else _pick(K, tk, 8)
    nk = K // tk
    a_spec = {"nn": pl.BlockSpec((tm, tk), lambda i, j, k: (i, k)),
              "nt": pl.BlockSpec((tm, tk), lambda i, j, k: (i, k)),
              "tn": pl.BlockSpec((tk, tm), lambda i, j, k: (k, i))}[mode]
    b_blk, b_map = {"nn": ((tk, tn), lambda i, j, k: (k, j)),
                    "nt": ((tn, tk), lambda i, j, k: (j, k)),
                    "tn": ((tk, tn), lambda i, j, k: (k, j))}[mode]
    if layer is None:
        b_spec = pl.BlockSpec(b_blk, b_map)
    else:
        b_spec = pl.BlockSpec((None,) + b_blk, lambda i, j, k: (layer,) + b_map(i, j, k))
    o_spec = pl.BlockSpec((tm, tn), lambda i, j, k: (i, j))
    v_spec = pl.BlockSpec((1, tn), lambda i, j, k: (0, j))
    in_specs, args = [a_spec, b_spec], [a, b]
    if bias is not None:
        in_specs.append(v_spec)
        args.append(_row(bias))
    out_shape = [jax.ShapeDtypeStruct((M, N), out_dtype)]
    out_specs = [o_spec]
    if res is not None:
        in_specs += [o_spec, v_spec]
        args += [res, _row(colscale)]
        out_shape.append(jax.ShapeDtypeStruct((M, N), F32))
        out_specs.append(o_spec)
    outs = pl.pallas_call(
        _mm_body(mode, nk, bias is not None, res is not None),
        grid=(M // tm, N // tn, nk),
        in_specs=in_specs, out_specs=out_specs, out_shape=out_shape,
        scratch_shapes=[pltpu.VMEM((tm, tn), F32)] if nk > 1 else [],
        compiler_params=_cparams("parallel", "parallel", "arbitrary"),
        name=name,
    )(*args)
    return outs if res is not None else outs[0]


def gmm(a, w, mode, out_dtype, name, *, bias=None, res=None, colscale=None, tr=512):
    S_ = a.shape[0]
    G = N_GROUPS
    C = a.shape[1] // G
    tr = _pick(S_, tr, 8)
    nr = S_ // tr
    if mode == "tn":
        def body(a_ref, b_ref, o_ref, acc_ref):
            i = pl.program_id(1)

            @pl.when(i == 0)
            def _():
                acc_ref[...] = jnp.zeros_like(acc_ref)

            acc_ref[...] += lax.dot_general(a_ref[...].astype(BF16), b_ref[...].astype(BF16), _DIMS["tn"],
                                            preferred_element_type=F32)

            @pl.when(i == nr - 1)
            def _():
                o_ref[...] = acc_ref[...].astype(o_ref.dtype)

        blk = pl.BlockSpec((tr, C), lambda g, i: (i, g))
        return pl.pallas_call(
            body, grid=(G, nr), in_specs=[blk, blk],
            out_specs=pl.BlockSpec((None, C, C), lambda g, i: (g, 0, 0)),
            out_shape=jax.ShapeDtypeStruct((G, C, C), out_dtype),
            scratch_shapes=[pltpu.VMEM((C, C), F32)],
            compiler_params=_cparams("parallel", "arbitrary"), name=name,
        )(a, w)

    has_bias, has_res = bias is not None, res is not None

    def body(*refs):
        a_ref, w_ref = refs[0], refs[1]
        pos = 2
        if has_bias:
            bias_ref = refs[pos]
            pos += 1
        if has_res:
            res_ref, cs_ref = refs[pos], refs[pos + 1]
            pos += 2
        o_ref = refs[pos]
        y = lax.dot_general(a_ref[...].astype(BF16), w_ref[...].astype(BF16), _DIMS[mode],
                            preferred_element_type=F32)
        if has_bias:
            y = y + bias_ref[...]
        o_ref[...] = y.astype(o_ref.dtype)
        if has_res:
            refs[pos + 1][...] = res_ref[...] + cs_ref[...] * y

    blk = pl.BlockSpec((tr, C), lambda i, g: (i, g))
    vec = pl.BlockSpec((1, C), lambda i, g: (0, g))
    in_specs = [blk, pl.BlockSpec((None, C, C), lambda i, g: (g, 0, 0))]
    args = [a, w]
    if has_bias:
        in_specs.append(vec)
        args.append(_row(bias))
    out_shape = [jax.ShapeDtypeStruct(a.shape, out_dtype)]
    out_specs = [blk]
    if has_res:
        in_specs += [blk, vec]
        args += [res, _row(colscale)]
        out_shape.append(jax.ShapeDtypeStruct(a.shape, F32))
        out_specs.append(blk)
    outs = pl.pallas_call(
        body, grid=(nr, G), in_specs=in_specs, out_specs=out_specs, out_shape=out_shape,
        compiler_params=_cparams("parallel", "parallel"), name=name,
    )(*args)
    return outs if has_res else outs[0]


def norm_fwd(x, g, sc, sh, out_dtype, name, tr=512):
    S_, Dn = x.shape
    tr = _pick(S_, tr, 8)

    def body(x_ref, g_ref, sc_ref, sh_ref, o_ref):
        xv = x_ref[...]
        r = lax.rsqrt(jnp.mean(xv * xv, axis=-1, keepdims=True) + EPS)
        o_ref[...] = (((xv * r) * g_ref[...]) * (1.0 + sc_ref[...]) + sh_ref[...]).astype(o_ref.dtype)

    blk = pl.BlockSpec((tr, Dn), lambda i: (i, 0))
    vec = pl.BlockSpec((1, Dn), lambda i: (0, 0))
    return pl.pallas_call(
        body, grid=(S_ // tr,), in_specs=[blk, vec, vec, vec], out_specs=blk,
        out_shape=jax.ShapeDtypeStruct((S_, Dn), out_dtype),
        compiler_params=_cparams("parallel"), name=name,
    )(x, _row(g), _row(sc), _row(sh))


def norm_bwd(x, g, sc, dh, dres, name, tr=512):
    S_, Dn = x.shape
    tr = _pick(S_, tr, 8)
    has_res = dres is not None

    def body(*refs):
        x_ref, g_ref, sc_ref, dh_ref = refs[:4]
        pos = 4
        if has_res:
            dres_ref = refs[pos]
            pos += 1
        dx_ref, s1_ref, s2_ref = refs[pos:pos + 3]
        i = pl.program_id(0)

        @pl.when(i == 0)
        def _():
            s1_ref[...] = jnp.zeros_like(s1_ref)
            s2_ref[...] = jnp.zeros_like(s2_ref)

        xv = x_ref[...]
        r = lax.rsqrt(jnp.mean(xv * xv, axis=-1, keepdims=True) + EPS)
        n = xv * r
        dhv = dh_ref[...].astype(F32)
        dn = dhv * (g_ref[...] * (1.0 + sc_ref[...]))
        dx = r * (dn - n * jnp.mean(dn * n, axis=-1, keepdims=True))
        if has_res:
            dx = dx + dres_ref[...]
        dx_ref[...] = dx
        s1_ref[...] += jnp.sum(dhv, axis=0, keepdims=True)
        s2_ref[...] += jnp.sum(dhv * n, axis=0, keepdims=True)

    blk = pl.BlockSpec((tr, Dn), lambda i: (i, 0))
    vec = pl.BlockSpec((1, Dn), lambda i: (0, 0))
    in_specs, args = [blk, vec, vec, blk], [x, _row(g), _row(sc), dh]
    if has_res:
        in_specs.append(blk)
        args.append(dres)
    vshape = jax.ShapeDtypeStruct((1, Dn), F32)
    return pl.pallas_call(
        body, grid=(S_ // tr,), in_specs=in_specs, out_specs=[blk, vec, vec],
        out_shape=[jax.ShapeDtypeStruct((S_, Dn), F32), vshape, vshape],
        compiler_params=_cparams("arbitrary"), name=name,
    )(*args)


def gate_bwd(dx, y, colscale, name, tr=512):
    S_, Dn = dx.shape
    tr = _pick(S_, tr, 8)

    def body(dx_ref, y_ref, cs_ref, d_ref, a_ref, c_ref):
        i = pl.program_id(0)

        @pl.when(i == 0)
        def _():
            a_ref[...] = jnp.zeros_like(a_ref)
            c_ref[...] = jnp.zeros_like(c_ref)

        dxv = dx_ref[...]
        d_ref[...] = (dxv * cs_ref[...]).astype(d_ref.dtype)
        a_ref[...] += jnp.sum(dxv * y_ref[...].astype(F32), axis=0, keepdims=True)
        c_ref[...] += jnp.sum(dxv, axis=0, keepdims=True)

    blk = pl.BlockSpec((tr, Dn), lambda i: (i, 0))
    vec = pl.BlockSpec((1, Dn), lambda i: (0, 0))
    vshape = jax.ShapeDtypeStruct((1, Dn), F32)
    return pl.pallas_call(
        body, grid=(S_ // tr,), in_specs=[blk, blk, vec], out_specs=[blk, vec, vec],
        out_shape=[jax.ShapeDtypeStruct((S_, Dn), BF16), vshape, vshape],
        compiler_params=_cparams("arbitrary"), name=name,
    )(dx, y, _row(colscale))


def loss_head(x, g, target, name, tr=512):
    S_, Dn = x.shape
    tr = _pick(S_, tr, 8)

    def body(x_ref, g_ref, t_ref, dx_ref, dg_ref, loss_ref):
        i = pl.program_id(0)

        @pl.when(i == 0)
        def _():
            dg_ref[...] = jnp.zeros_like(dg_ref)
            loss_ref[...] = jnp.zeros_like(loss_ref)

        xv = x_ref[...]
        r = lax.rsqrt(jnp.mean(xv * xv, axis=-1, keepdims=True) + EPS)
        n = xv * r
        e = n * g_ref[...] - t_ref[...]
        loss_ref[...] += 0.5 * jnp.sum(jnp.mean(e * e, axis=-1, keepdims=True), axis=0, keepdims=True)
        dy = e * (1.0 / Dn)
        dg_ref[...] += jnp.sum(dy * n, axis=0, keepdims=True)
        dn = dy * g_ref[...]
        dx_ref[...] = r * (dn - n * jnp.mean(dn * n, axis=-1, keepdims=True))

    blk = pl.BlockSpec((tr, Dn), lambda i: (i, 0))
    vec = pl.BlockSpec((1, Dn), lambda i: (0, 0))
    one = pl.BlockSpec((1, 1), lambda i: (0, 0))
    return pl.pallas_call(
        body, grid=(S_ // tr,), in_specs=[blk, vec, blk], out_specs=[blk, vec, one],
        out_shape=[jax.ShapeDtypeStruct((S_, Dn), F32), jax.ShapeDtypeStruct((1, Dn), F32),
                   jax.ShapeDtypeStruct((1, 1), F32)],
        compiler_params=_cparams("arbitrary"), name=name,
    )(x, _row(g), target)


POOL_HALO = 16
POOL_CHUNK = 512


def _rows(ref, lo, hi, n_rows):
    parts = []
    if lo < 0:
        parts.append(jnp.zeros((-lo, ref.shape[1]), F32))
    parts.append(ref[max(lo, 0):min(hi, n_rows), :].astype(F32))
    if hi > n_rows:
        parts.append(jnp.zeros((hi - n_rows, ref.shape[1]), F32))
    return parts[0] if len(parts) == 1 else jnp.concatenate(parts, axis=0)


def _window_sum(e, w, back):
    n = e.shape[0]
    s, width = e, 1
    while width < w:
        s = s + pltpu.roll(s, width if back else n - width, 0)
        width *= 2
    return s


def _pool_call(h, out_dtype, name, backward):
    S_, Dn = h.shape
    C = Dn // N_GROUPS
    ch = _pick(S_, POOL_CHUNK, 8)

    def body(h_ref, o_ref):
        g = pl.program_id(0)
        for gi, w in enumerate(POOL_WINDOWS):
            @pl.when(g == gi)
            def _(w=w):
                for r0 in range(0, S_, ch):
                    t = (r0 + lax.broadcasted_iota(jnp.int32, (ch, C), 0)).astype(F32)
                    cnt = jnp.minimum(t + 1.0, float(w))
                    if not backward:
                        ext = _rows(h_ref, r0 - POOL_HALO, r0 + ch, S_)
                        cur = ext[POOL_HALO:]
                        mean = _window_sum(ext, w, True)[POOL_HALO:] / cnt
                        o_ref[r0:r0 + ch, :] = (mean - cur).astype(o_ref.dtype)
                    else:
                        ext = _rows(h_ref, r0, r0 + ch + POOL_HALO, S_)
                        text = (r0 + lax.broadcasted_iota(jnp.int32, (ch + POOL_HALO, C), 0)).astype(F32)
                        e = ext / jnp.minimum(text + 1.0, float(w))
                        o_ref[r0:r0 + ch, :] = (_window_sum(e, w, False)[:ch] - ext[:ch]).astype(o_ref.dtype)

    blk = pl.BlockSpec((S_, C), lambda g: (0, g))
    return pl.pallas_call(
        body, grid=(N_GROUPS,), in_specs=[blk], out_specs=blk,
        out_shape=jax.ShapeDtypeStruct((S_, Dn), out_dtype),
        compiler_params=_cparams("parallel"), name=name,
    )(h)


GLU_CHUNK = 512
GLU_HALO = 16
_SQRT_HALF = 0.7071067811865476
_INV_SQRT_2PI = 0.3989422804014327


def _gelu(a):
    return 0.5 * a * (1.0 + lax.erf(a * _SQRT_HALF))


def _gelu_grad(a):
    return 0.5 * (1.0 + lax.erf(a * _SQRT_HALF)) + a * (_INV_SQRT_2PI * jnp.exp(-0.5 * a * a))


def glu_fwd(u, conv_w, conv_b, name):
    S_, F2 = u.shape
    Fh = F2 // 2
    tf = GLU_TILE
    nt = Fh // tf
    ch = _pick(S_, GLU_CHUNK, GLU_HALO)

    def body(a_ref, v_ref, cw_ref, cb_ref, z_ref):
        cw0, cw1, cw2 = cw_ref[0:1, :], cw_ref[1:2, :], cw_ref[2:3, :]
        cb = cb_ref[...]
        for r0 in range(0, S_, ch):
            ext = _rows(a_ref, r0 - GLU_HALO, r0 + ch, S_)
            a0 = ext[GLU_HALO:]
            a1 = pltpu.roll(ext, 1, 0)[GLU_HALO:]
            a2 = pltpu.roll(ext, 2, 0)[GLU_HALO:]
            ac = a2 * cw0 + a1 * cw1 + a0 * cw2 + cb
            z_ref[r0:r0 + ch, :] = (_gelu(ac) * v_ref[r0:r0 + ch, :].astype(F32)).astype(z_ref.dtype)

    return pl.pallas_call(
        body, grid=(nt,),
        in_specs=[pl.BlockSpec((S_, tf), lambda j: (0, j)), pl.BlockSpec((S_, tf), lambda j: (0, j + nt)),
                  pl.BlockSpec((3, tf), lambda j: (0, j)), pl.BlockSpec((1, tf), lambda j: (0, j))],
        out_specs=pl.BlockSpec((S_, tf), lambda j: (0, j)),
        out_shape=jax.ShapeDtypeStruct((S_, Fh), BF16),
        compiler_params=_cparams("parallel"), name=name,
    )(u, u, conv_w, _row(conv_b))


def glu_bwd(u, dz, conv_w, conv_b, name):
    S_, F2 = u.shape
    Fh = F2 // 2
    tf = GLU_TILE
    nt = Fh // tf
    ch = _pick(S_, GLU_CHUNK, GLU_HALO)

    def body(a_ref, v_ref, dz_ref, cw_ref, cb_ref, du_ref, dcw_ref, dcb_ref, da_buf, dv_buf, sems):
        j = pl.program_id(0)
        slot = j % 2

        def writes(step, sl):
            lo = pl.multiple_of(step * tf, tf)
            return (pltpu.make_async_copy(da_buf.at[sl], du_ref.at[:, pl.ds(lo, tf)], sems.at[sl, 0]),
                    pltpu.make_async_copy(dv_buf.at[sl], du_ref.at[:, pl.ds(Fh + lo, tf)], sems.at[sl, 1]))

        @pl.when(j >= 2)
        def _():
            for cp in writes(j - 2, slot):
                cp.wait()

        cw0, cw1, cw2 = cw_ref[0:1, :], cw_ref[1:2, :], cw_ref[2:3, :]
        cb = cb_ref[...]
        acc = [jnp.zeros((1, tf), F32) for _ in range(4)]
        n = ch + GLU_HALO
        for r0 in range(0, S_, ch):
            ext = _rows(a_ref, r0 - GLU_HALO, r0 + n, S_)
            a0 = ext[GLU_HALO:]
            a1 = pltpu.roll(ext, 1, 0)[GLU_HALO:]
            a2 = pltpu.roll(ext, 2, 0)[GLU_HALO:]
            ac = a2 * cw0 + a1 * cw1 + a0 * cw2 + cb
            vv = _rows(v_ref, r0, r0 + n, S_)
            dzv = _rows(dz_ref, r0, r0 + n, S_)
            gl = _gelu(ac)
            dac = dzv * vv * _gelu_grad(ac)
            da = (dac * cw2 + pltpu.roll(dac, n - 1, 0) * cw1 + pltpu.roll(dac, n - 2, 0) * cw0)[:ch]
            da_buf[slot, r0:r0 + ch, :] = da.astype(da_buf.dtype)
            dv_buf[slot, r0:r0 + ch, :] = (dzv[:ch] * gl[:ch]).astype(dv_buf.dtype)
            dc = dac[:ch]
            acc[0] = acc[0] + jnp.sum(dc * a2[:ch], axis=0, keepdims=True)
            acc[1] = acc[1] + jnp.sum(dc * a1[:ch], axis=0, keepdims=True)
            acc[2] = acc[2] + jnp.sum(dc * a0[:ch], axis=0, keepdims=True)
            acc[3] = acc[3] + jnp.sum(dc, axis=0, keepdims=True)
        dcw_ref[0:1, :] = acc[0]
        dcw_ref[1:2, :] = acc[1]
        dcw_ref[2:3, :] = acc[2]
        dcb_ref[...] = acc[3]
        for cp in writes(j, slot):
            cp.start()

        @pl.when(j == nt - 1)
        def _():
            for cp in writes(j, slot):
                cp.wait()
            if nt > 1:
                for cp in writes(j - 1, 1 - slot):
                    cp.wait()

    return pl.pallas_call(
        body, grid=(nt,),
        in_specs=[pl.BlockSpec((S_, tf), lambda j: (0, j)), pl.BlockSpec((S_, tf), lambda j: (0, j + nt)),
                  pl.BlockSpec((S_, tf), lambda j: (0, j)),
                  pl.BlockSpec((3, tf), lambda j: (0, j)), pl.BlockSpec((1, tf), lambda j: (0, j))],
        out_specs=[_ANY, pl.BlockSpec((3, tf), lambda j: (0, j)), pl.BlockSpec((1, tf), lambda j: (0, j))],
        out_shape=[jax.ShapeDtypeStruct((S_, F2), BF16), jax.ShapeDtypeStruct((3, Fh), F32),
                   jax.ShapeDtypeStruct((1, Fh), F32)],
        scratch_shapes=[pltpu.VMEM((2, S_, tf), BF16), pltpu.VMEM((2, S_, tf), BF16), pltpu.SemaphoreType.DMA((2, 2))],
        compiler_params=_cparams("arbitrary"), name=name,
    )(u, u, dz, conv_w, _row(conv_b))


def rope_tables(pos, inv, name, tr=512):
    S_ = pos.shape[0]
    tr = _pick(S_, tr, 8)

    def body(p_ref, inv_ref, c_ref, s1_ref, s2_ref):
        ang = p_ref[...] * inv_ref[...]
        lane = lax.broadcasted_iota(jnp.int32, ang.shape, 1)
        half = QK_ROPE // 2
        cosv, sinv = jnp.cos(ang), jnp.sin(ang)
        c_ref[...] = jnp.where(lane < QK_ROPE, cosv, 0.0)
        s1_ref[...] = jnp.where(lane < half, -sinv, 0.0)
        s2_ref[...] = jnp.where((lane >= half) & (lane < QK_ROPE), sinv, 0.0)

    blk = pl.BlockSpec((tr, LANES), lambda i: (i, 0))
    shp = jax.ShapeDtypeStruct((S_, LANES), F32)
    return pl.pallas_call(
        body, grid=(S_ // tr,),
        in_specs=[pl.BlockSpec((tr, 1), lambda i: (i, 0)), pl.BlockSpec((1, LANES), lambda i: (0, 0))],
        out_specs=[blk, blk, blk], out_shape=[shp, shp, shp],
        compiler_params=_cparams("parallel"), name=name,
    )(pos, inv)


_HALF = QK_ROPE // 2


def _rope(t, c, s1, s2):
    return t * c + pltpu.roll(t, LANES - _HALF, 1) * s1 + pltpu.roll(t, _HALF, 1) * s2


def _rope_t(d, c, s1, s2):
    return d * c + pltpu.roll(d * s1, _HALF, 1) + pltpu.roll(d * s2, LANES - _HALF, 1)


def q_prep(q, tabs, scale, backward, name, tr=512):
    S_, W = q.shape
    tr = _pick(S_, tr, 8)

    def body(q_ref, c_ref, s1_ref, s2_ref, o_ref):
        o_ref[:, 0:LANES] = (q_ref[:, 0:LANES].astype(F32) * scale).astype(o_ref.dtype)
        t = q_ref[:, LANES:2 * LANES].astype(F32)
        fn = _rope_t if backward else _rope
        o_ref[:, LANES:2 * LANES] = (fn(t, c_ref[...], s1_ref[...], s2_ref[...]) * scale).astype(o_ref.dtype)

    blk = pl.BlockSpec((tr, HEAD_PAD), lambda i, h: (i, h))
    tab = pl.BlockSpec((tr, LANES), lambda i, h: (i, 0))
    return pl.pallas_call(
        body, grid=(S_ // tr, W // HEAD_PAD), in_specs=[blk, tab, tab, tab], out_specs=blk,
        out_shape=jax.ShapeDtypeStruct((S_, W), BF16),
        compiler_params=_cparams("parallel", "parallel"), name=name,
    )(q, *tabs)


def k_prep(knv, kv_ext, tabs, name, tr=512):
    S_ = knv.shape[0]
    tr = _pick(S_, tr, 8)

    def body(kn_ref, v_ref, t_ref, c_ref, s1_ref, s2_ref, o_ref, vx_ref):
        o_ref[:, 0:LANES] = kn_ref[...].astype(o_ref.dtype)
        o_ref[:, LANES:2 * LANES] = _rope(t_ref[...], c_ref[...], s1_ref[...], s2_ref[...]).astype(o_ref.dtype)
        vx_ref[:, 0:V_HEAD] = v_ref[...].astype(vx_ref.dtype)
        vx_ref[:, V_HEAD:HEAD_PAD] = jnp.ones((tr, HEAD_PAD - V_HEAD), vx_ref.dtype)

    tab = pl.BlockSpec((tr, LANES), lambda i, h: (i, 0))
    head = pl.BlockSpec((tr, HEAD_PAD), lambda i, h: (i, h))
    shp = jax.ShapeDtypeStruct((S_, N_HEADS * HEAD_PAD), BF16)
    return pl.pallas_call(
        body, grid=(S_ // tr, N_HEADS),
        in_specs=[pl.BlockSpec((tr, LANES), lambda i, h: (i, h)),
                  pl.BlockSpec((tr, V_HEAD), lambda i, h: (i, N_HEADS + h)),
                  pl.BlockSpec((tr, LANES), lambda i, h: (i, KV_RANK // LANES)), tab, tab, tab],
        out_specs=[head, head], out_shape=[shp, shp],
        compiler_params=_cparams("parallel", "parallel"), name=name,
    )(knv, knv, kv_ext, *tabs)


def k_prep_bwd(dk_a, dk_b, dv_a, dv_b, tabs, name, tr=256):
    S_ = dk_a.shape[0]
    tr = _pick(S_, tr, 8)
    HV = N_HEADS * V_HEAD

    def body(ka_ref, kb_ref, va_ref, vb_ref, c_ref, s1_ref, s2_ref, o_ref, t_ref):
        dr = jnp.zeros((tr, LANES), F32)
        for h in range(N_HEADS):
            lo = h * HEAD_PAD
            o_ref[:, h * LANES:(h + 1) * LANES] = (ka_ref[:, lo:lo + LANES] + kb_ref[:, lo:lo + LANES]).astype(o_ref.dtype)
            dr = dr + ka_ref[:, lo + LANES:lo + 2 * LANES] + kb_ref[:, lo + LANES:lo + 2 * LANES]
        o_ref[:, HV:2 * HV] = (va_ref[...] + vb_ref[...]).astype(o_ref.dtype)
        t_ref[...] = _rope_t(dr, c_ref[...], s1_ref[...], s2_ref[...])

    kblk = pl.BlockSpec((tr, N_HEADS * HEAD_PAD), lambda i: (i, 0))
    vblk = pl.BlockSpec((tr, HV), lambda i: (i, 0))
    tab = pl.BlockSpec((tr, LANES), lambda i: (i, 0))
    return pl.pallas_call(
        body, grid=(S_ // tr,), in_specs=[kblk, kblk, vblk, vblk, tab, tab, tab],
        out_specs=[pl.BlockSpec((tr, 2 * HV), lambda i: (i, 0)), tab],
        out_shape=[jax.ShapeDtypeStruct((S_, 2 * HV), BF16), jax.ShapeDtypeStruct((S_, LANES), F32)],
        compiler_params=_cparams("parallel"), name=name,
    )(dk_a, dk_b, dv_a, dv_b, *tabs)


_NEG = -1e30


def attn_fwd(q, k, vx, name):
    S_ = q.shape[0]
    TQ = _pick(S_, ATT_Q_BLOCK, 8)
    TK = _pick(S_, ATT_K_BLOCK, TQ)
    HP = ATT_HEADS_PER_STEP
    W = HP * HEAD_PAD
    ratio = TK // TQ

    def body(q_ref, k_ref, v_ref, o_ref, lse_ref):
        i = pl.program_id(1)
        qs = [q_ref[:, h * HEAD_PAD:(h + 1) * HEAD_PAD] for h in range(HP)]

        def step(j, carry, masked):
            start = pl.multiple_of(j * TK, TK)
            out = []
            for h in range(HP):
                m, acc = carry[h]
                cols = slice(h * HEAD_PAD, (h + 1) * HEAD_PAD)
                s = lax.dot_general(qs[h], k_ref[pl.ds(start, TK), cols], _DIMS["nt"], preferred_element_type=F32)
                if masked:
                    rowi = i * TQ + lax.broadcasted_iota(jnp.int32, (TQ, TK), 0)
                    coli = j * TK + lax.broadcasted_iota(jnp.int32, (TQ, TK), 1)
                    s = jnp.where(coli <= rowi, s, _NEG)
                m_new = jnp.maximum(m, jnp.max(s, axis=-1, keepdims=True))
                alpha = jnp.exp(m - m_new)
                p = jnp.exp(s - m_new).astype(BF16)
                acc = alpha * acc + lax.dot_general(p, v_ref[pl.ds(start, TK), cols], _DIMS["nn"],
                                                    preferred_element_type=F32)
                out.append((m_new, acc))
            return tuple(out)

        init = tuple((jnp.full((TQ, 1), _NEG, F32), jnp.zeros((TQ, HEAD_PAD), F32)) for _ in range(HP))
        last = i // ratio
        carry = step(last, lax.fori_loop(0, last, functools.partial(step, masked=False), init), True)
        for h in range(HP):
            m, acc = carry[h]
            l = acc[:, V_HEAD:]
            o_ref[:, h * V_HEAD:(h + 1) * V_HEAD] = (acc[:, :V_HEAD] / l).astype(o_ref.dtype)
            lse_ref[h] = m + jnp.log(jnp.max(l, axis=-1, keepdims=True))

    return pl.pallas_call(
        body, grid=(N_HEADS // HP, S_ // TQ),
        in_specs=[pl.BlockSpec((TQ, W), lambda g, i: (i, g)),
                  pl.BlockSpec((S_, W), lambda g, i: (0, g)),
                  pl.BlockSpec((S_, W), lambda g, i: (0, g))],
        out_specs=[pl.BlockSpec((TQ, HP * V_HEAD), lambda g, i: (i, g)),
                   pl.BlockSpec((HP, TQ, 1), lambda g, i: (g, i, 0))],
        out_shape=[jax.ShapeDtypeStruct((S_, N_HEADS * V_HEAD), BF16), jax.ShapeDtypeStruct((N_HEADS, S_, 1), F32)],
        compiler_params=_cparams("parallel", "parallel"), name=name,
    )(q, k, vx)


def attn_delta(o, do, name, tr=512):
    S_ = o.shape[0]
    tr = _pick(S_, tr, 8)

    def body(o_ref, do_ref, d_ref):
        d_ref[...] = jnp.sum(o_ref[...].astype(F32) * do_ref[...].astype(F32), axis=-1, keepdims=True)

    blk = pl.BlockSpec((tr, V_HEAD), lambda i, h: (i, h))
    return pl.pallas_call(
        body, grid=(S_ // tr, N_HEADS), in_specs=[blk, blk],
        out_specs=pl.BlockSpec((None, tr, 1), lambda i, h: (h, i, 0)),
        out_shape=jax.ShapeDtypeStruct((N_HEADS, S_, 1), F32),
        compiler_params=_cparams("parallel", "parallel"), name=name,
    )(o, do)


def attn_bwd(q, k, vx, do, lse_row, delta_row, name):
    S_ = q.shape[0]
    TK = _pick(S_, ATT_BWD_K_BLOCK, LANES)
    TQ = _pick(S_, ATT_BWD_Q_BLOCK, TK)
    HP = ATT_HEADS_PER_STEP
    W = HP * HEAD_PAD
    ratio = TQ // TK
    nq = S_ // TQ

    def body(q_ref, do_ref, lse_ref, dl_ref, k_ref, v_ref, dq_ref, dk_ref, dv_ref):
        j = pl.program_id(1)

        @pl.when(j == 0)
        def _():
            dq_ref[...] = jnp.zeros_like(dq_ref)

        ks = [k_ref[:, h * HEAD_PAD:(h + 1) * HEAD_PAD] for h in range(HP)]
        vs = [v_ref[:, h * HEAD_PAD:h * HEAD_PAD + V_HEAD] for h in range(HP)]

        def step(i, carry, masked):
            start = pl.multiple_of(i * TQ, TQ)
            out = []
            for h in range(HP):
                dk, dv = carry[h]
                cols = slice(h * HEAD_PAD, (h + 1) * HEAD_PAD)
                qv = q_ref[pl.ds(start, TQ), cols]
                dov = do_ref[pl.ds(start, TQ), h * V_HEAD:(h + 1) * V_HEAD]
                st = lax.dot_general(ks[h], qv, _DIMS["nt"], preferred_element_type=F32)
                pt = jnp.exp(st - lse_ref[h, :, pl.ds(start, TQ)])
                if masked:
                    keyi = j * TK + lax.broadcasted_iota(jnp.int32, (TK, TQ), 0)
                    qryi = i * TQ + lax.broadcasted_iota(jnp.int32, (TK, TQ), 1)
                    pt = jnp.where(keyi <= qryi, pt, 0.0)
                dpt = lax.dot_general(vs[h], dov, _DIMS["nt"], preferred_element_type=F32)
                dst = (pt * (dpt - dl_ref[h, :, pl.ds(start, TQ)])).astype(BF16)
                dv = dv + lax.dot_general(pt.astype(BF16), dov, _DIMS["nn"], preferred_element_type=F32)
                dk = dk + lax.dot_general(dst, qv, _DIMS["nn"], preferred_element_type=F32)
                dq_ref[pl.ds(start, TQ), cols] += lax.dot_general(dst, ks[h], _DIMS["tn"], preferred_element_type=F32)
                out.append((dk, dv))
            return tuple(out)

        init = tuple((jnp.zeros((TK, HEAD_PAD), F32), jnp.zeros((TK, V_HEAD), F32)) for _ in range(HP))
        first = j // ratio
        carry = lax.fori_loop(first + 1, nq, functools.partial(step, masked=False), step(first, init, True))
        for h in range(HP):
            dk_ref[:, h * HEAD_PAD:(h + 1) * HEAD_PAD] = carry[h][0]
            dv_ref[:, h * V_HEAD:(h + 1) * V_HEAD] = carry[h][1]

    return pl.pallas_call(
        body, grid=(N_HEADS // HP, S_ // TK),
        in_specs=[pl.BlockSpec((S_, W), lambda g, j: (0, g)),
                  pl.BlockSpec((S_, HP * V_HEAD), lambda g, j: (0, g)),
                  pl.BlockSpec((HP, 1, S_), lambda g, j: (g, 0, 0)),
                  pl.BlockSpec((HP, 1, S_), lambda g, j: (g, 0, 0)),
                  pl.BlockSpec((TK, W), lambda g, j: (j, g)),
                  pl.BlockSpec((TK, W), lambda g, j: (j, g))],
        out_specs=[pl.BlockSpec((S_, W), lambda g, j: (0, g)),
                   pl.BlockSpec((TK, W), lambda g, j: (j, g)),
                   pl.BlockSpec((TK, HP * V_HEAD), lambda g, j: (j, g))],
        out_shape=[jax.ShapeDtypeStruct((S_, N_HEADS * HEAD_PAD), F32),
                   jax.ShapeDtypeStruct((S_, N_HEADS * HEAD_PAD), F32),
                   jax.ShapeDtypeStruct((S_, N_HEADS * V_HEAD), F32)],
        compiler_params=_cparams("parallel", "arbitrary"), name=name,
    )(q, do, lse_row, delta_row, k, vx)


def mods_fwd(c_all, mod_w, mod_b, name, tn=512):
    L, Dn, E = mod_w.shape
    R = c_all.shape[0]
    tn = _pick(E, tn, LANES)

    def body(c_ref, w_ref, b_ref, o_ref):
        cv = c_ref[...]
        sc = (cv / (1.0 + jnp.exp(-cv))).astype(BF16)
        o_ref[...] = lax.dot_general(sc, w_ref[...].astype(BF16), _DIMS["nn"], preferred_element_type=F32) + b_ref[...]

    return pl.pallas_call(
        body, grid=(L, E // tn),
        in_specs=[pl.BlockSpec((R, Dn), lambda l, j: (0, 0)), pl.BlockSpec((None, Dn, tn), lambda l, j: (l, 0, j)),
                  pl.BlockSpec((None, 1, tn), lambda l, j: (l, 0, j))],
        out_specs=pl.BlockSpec((None, R, tn), lambda l, j: (l, 0, j)),
        out_shape=jax.ShapeDtypeStruct((L, R, E), F32),
        compiler_params=_cparams("parallel", "parallel"), name=name,
    )(c_all, mod_w, mod_b.reshape(L, 1, E))


def _adam_math(w, g, m, v):
    m = ADAM_B1 * m + (1.0 - ADAM_B1) * g
    v = ADAM_B2 * v + (1.0 - ADAM_B2) * (g * g)
    m_hat = m / (1.0 - ADAM_B1 ** ADAM_STEP)
    v_hat = v / (1.0 - ADAM_B2 ** ADAM_STEP)
    delta = -ADAM_LR * (m_hat / (jnp.sqrt(v_hat) + ADAM_EPS) + ADAM_WD * w)
    return delta, m, v


def _as2d(a):
    return a.reshape(-1, a.shape[-1]) if a.ndim != 2 else a


def adamw(w, g, m, v, name):
    shape = w.shape
    w2, g2, m2, v2 = _as2d(w), _as2d(g), _as2d(m), _as2d(v)
    R, C = w2.shape
    tr = _pick(R, max(8, (1 << 18) // C // 8 * 8), 8)

    def body(w_ref, g_ref, m_ref, v_ref, d_ref, mo_ref, vo_ref):
        d, mn, vn = _adam_math(w_ref[...], g_ref[...], m_ref[...], v_ref[...])
        d_ref[...] = d
        mo_ref[...] = mn
        vo_ref[...] = vn

    blk = pl.BlockSpec((tr, C), lambda i: (i, 0))
    shp = jax.ShapeDtypeStruct((R, C), F32)
    outs = pl.pallas_call(
        body, grid=(R // tr,), in_specs=[blk] * 4, out_specs=[blk] * 3, out_shape=[shp] * 3,
        compiler_params=_cparams("parallel"), name=name,
    )(w2, g2, m2, v2)
    return tuple(o.reshape(shape) for o in outs)


def adamw_sum(parts, w, m, v, name):
    P, R, C = parts.shape

    def body(p_ref, w_ref, m_ref, v_ref, g_ref, d_ref, mo_ref, vo_ref):
        g = p_ref[0]
        for k in range(1, P):
            g = g + p_ref[k]
        d, mn, vn = _adam_math(w_ref[...], g, m_ref[...], v_ref[...])
        g_ref[...] = g
        d_ref[...] = d
        mo_ref[...] = mn
        vo_ref[...] = vn

    shp = jax.ShapeDtypeStruct((R, C), F32)
    return pl.pallas_call(body, out_shape=[shp] * 4, compiler_params=_cparams(), name=name)(parts, w, m, v)


def adamw_modw(c_col, dm, w, m, v, name, tr=256, tn=512):
    L, Dn, E = w.shape
    B = c_col.shape[0]
    tr = _pick(Dn, tr, 8)
    tn = _pick(E, tn, LANES)

    def body(c_ref, dm_ref, w_ref, m_ref, v_ref, g_ref, d_ref, mo_ref, vo_ref):
        g = jnp.zeros((tr, tn), F32)
        for b in range(B):
            cv = c_ref[b]
            g = g + (cv / (1.0 + jnp.exp(-cv))) * dm_ref[b:b + 1, :]
        d, mn, vn = _adam_math(w_ref[...], g, m_ref[...], v_ref[...])
        g_ref[...] = g
        d_ref[...] = d
        mo_ref[...] = mn
        vo_ref[...] = vn

    blk = pl.BlockSpec((None, tr, tn), lambda l, i, j: (l, i, j))
    shp = jax.ShapeDtypeStruct((L, Dn, E), F32)
    return pl.pallas_call(
        body, grid=(L, Dn // tr, E // tn),
        in_specs=[pl.BlockSpec((B, tr, 1), lambda l, i, j: (0, i, 0)),
                  pl.BlockSpec((None, B, tn), lambda l, i, j: (l, 0, j)), blk, blk, blk],
        out_specs=[blk] * 4, out_shape=[shp] * 4,
        compiler_params=_cparams("parallel", "parallel", "parallel"), name=name,
    )(c_col, dm, w, m, v)


def add_round(a, b, name, tr=512):
    R, C = a.shape
    tr = _pick(R, tr, 16)

    def body(a_ref, b_ref, o_ref):
        o_ref[...] = (a_ref[...] + b_ref[...].astype(F32)).astype(BF16)

    blk = pl.BlockSpec((tr, C), lambda i: (i, 0))
    return pl.pallas_call(
        body, grid=(R // tr,), in_specs=[blk, blk], out_specs=blk, out_shape=jax.ShapeDtypeStruct((R, C), BF16),
        compiler_params=_cparams("parallel"), name=name,
    )(a, b)


def sum_parts(parts, name, tr=512):
    P, R, C = parts.shape
    tr = _pick(R, tr, 16)

    def body(p_ref, o_ref):
        s = p_ref[0].astype(F32)
        for k in range(1, P):
            s = s + p_ref[k].astype(F32)
        o_ref[...] = s

    return pl.pallas_call(
        body, grid=(R // tr,), in_specs=[pl.BlockSpec((P, tr, C), lambda i: (0, i, 0))],
        out_specs=pl.BlockSpec((tr, C), lambda i: (i, 0)), out_shape=jax.ShapeDtypeStruct((R, C), F32),
        compiler_params=_cparams("parallel"), name=name,
    )(parts)


_ANY = pl.BlockSpec(memory_space=pl.ANY)


def _place():
    return lax.axis_index("x"), lax.axis_index("y"), lax.axis_index("c")


def _flip(v, bit):
    return 1 - v if bit else v


def chip_gather(buf, name):
    def body(in_ref, out_ref, send_sems, recv_sems):
        x, y, c = _place()
        me = 2 * x + y
        sends = []
        for k in range(1, N_CHIPS):
            px, py = _flip(x, k >> 1), _flip(y, k & 1)
            cp = pltpu.make_async_remote_copy(src_ref=in_ref, dst_ref=out_ref.at[me], send_sem=send_sems.at[k - 1],
                                              recv_sem=recv_sems.at[k - 1], device_id=(px, py, c), device_id_type=MESH)
            cp.start()
            sends.append(cp)
        for k in range(1, N_CHIPS):
            px, py = _flip(x, k >> 1), _flip(y, k & 1)
            pltpu.make_async_remote_copy(src_ref=in_ref, dst_ref=out_ref.at[2 * px + py], send_sem=send_sems.at[k - 1],
                                         recv_sem=recv_sems.at[k - 1], device_id=(px, py, c),
                                         device_id_type=MESH).wait_recv()
        for cp in sends:
            cp.wait_send()

    out = pl.pallas_call(
        body, in_specs=[_ANY], out_specs=_ANY,
        out_shape=jax.ShapeDtypeStruct((N_CHIPS,) + buf.shape, buf.dtype),
        scratch_shapes=[pltpu.SemaphoreType.DMA((N_CHIPS - 1,)), pltpu.SemaphoreType.DMA((N_CHIPS - 1,))],
        name=name,
    )(buf)
    return lax.dynamic_update_index_in_dim(out, buf, 2 * lax.axis_index("x") + lax.axis_index("y"), 0)


def chip_all_to_all(buf, name):
    def body(in_ref, out_ref, send_sems, recv_sems):
        x, y, c = _place()
        me = 2 * x + y
        sends = []
        for k in range(1, N_CHIPS):
            px, py = _flip(x, k >> 1), _flip(y, k & 1)
            cp = pltpu.make_async_remote_copy(src_ref=in_ref.at[2 * px + py], dst_ref=out_ref.at[me],
                                              send_sem=send_sems.at[k - 1], recv_sem=recv_sems.at[k - 1],
                                              device_id=(px, py, c), device_id_type=MESH)
            cp.start()
            sends.append(cp)
        for k in range(1, N_CHIPS):
            px, py = _flip(x, k >> 1), _flip(y, k & 1)
            pltpu.make_async_remote_copy(src_ref=in_ref.at[me], dst_ref=out_ref.at[2 * px + py],
                                         send_sem=send_sems.at[k - 1], recv_sem=recv_sems.at[k - 1],
                                         device_id=(px, py, c), device_id_type=MESH).wait_recv()
        for cp in sends:
            cp.wait_send()

    out = pl.pallas_call(
        body, in_specs=[_ANY], out_specs=_ANY, out_shape=jax.ShapeDtypeStruct(buf.shape, buf.dtype),
        scratch_shapes=[pltpu.SemaphoreType.DMA((N_CHIPS - 1,)), pltpu.SemaphoreType.DMA((N_CHIPS - 1,))],
        name=name,
    )(buf)
    me = 2 * lax.axis_index("x") + lax.axis_index("y")
    return lax.dynamic_update_index_in_dim(out, _index(buf, me), me, 0)


def core_gather(buf, name):
    def body(in_ref, out_ref, send_sem, recv_sem):
        x, y, c = _place()
        cp = pltpu.make_async_remote_copy(src_ref=in_ref, dst_ref=out_ref.at[c], send_sem=send_sem, recv_sem=recv_sem,
                                          device_id=(x, y, 1 - c), device_id_type=MESH)
        cp.start()
        pltpu.make_async_remote_copy(src_ref=in_ref, dst_ref=out_ref.at[1 - c], send_sem=send_sem, recv_sem=recv_sem,
                                     device_id=(x, y, 1 - c), device_id_type=MESH).wait_recv()
        cp.wait_send()

    out = pl.pallas_call(
        body, in_specs=[_ANY], out_specs=_ANY, out_shape=jax.ShapeDtypeStruct((2,) + buf.shape, buf.dtype),
        scratch_shapes=[pltpu.SemaphoreType.DMA, pltpu.SemaphoreType.DMA],
        name=name,
    )(buf)
    return lax.dynamic_update_index_in_dim(out, buf, lax.axis_index("c"), 0)


def core_swap(buf, name):
    def body(in_ref, out_ref, send_sem, recv_sem):
        x, y, c = _place()
        cp = pltpu.make_async_remote_copy(src_ref=in_ref, dst_ref=out_ref, send_sem=send_sem, recv_sem=recv_sem,
                                          device_id=(x, y, 1 - c), device_id_type=MESH)
        cp.start()
        cp.wait()

    return pl.pallas_call(
        body, in_specs=[_ANY], out_specs=_ANY, out_shape=jax.ShapeDtypeStruct(buf.shape, buf.dtype),
        scratch_shapes=[pltpu.SemaphoreType.DMA, pltpu.SemaphoreType.DMA],
        name=name,
    )(buf)


def device_gather(buf, name):
    def body(in_ref, out_ref, send_sems, recv_sems, local_sem):
        x, y, c = _place()
        me = 4 * x + 2 * y + c
        mine = pltpu.make_async_copy(in_ref, out_ref.at[me], local_sem)
        mine.start()
        sends = []
        for k in range(1, N_DEV):
            peer = (_flip(x, (k >> 2) & 1), _flip(y, (k >> 1) & 1), _flip(c, k & 1))
            cp = pltpu.make_async_remote_copy(src_ref=in_ref, dst_ref=out_ref.at[me], send_sem=send_sems.at[k - 1],
                                              recv_sem=recv_sems.at[k - 1], device_id=peer, device_id_type=MESH)
            cp.start()
            sends.append(cp)
        for k in range(1, N_DEV):
            peer = (_flip(x, (k >> 2) & 1), _flip(y, (k >> 1) & 1), _flip(c, k & 1))
            pltpu.make_async_remote_copy(src_ref=in_ref, dst_ref=out_ref.at[4 * peer[0] + 2 * peer[1] + peer[2]],
                                         send_sem=send_sems.at[k - 1], recv_sem=recv_sems.at[k - 1], device_id=peer,
                                         device_id_type=MESH).wait_recv()
        for cp in sends:
            cp.wait_send()
        mine.wait()

    return pl.pallas_call(
        body, in_specs=[_ANY], out_specs=_ANY, out_shape=jax.ShapeDtypeStruct((N_DEV,) + buf.shape, buf.dtype),
        scratch_shapes=[pltpu.SemaphoreType.DMA((N_DEV - 1,)), pltpu.SemaphoreType.DMA((N_DEV - 1,)),
                        pltpu.SemaphoreType.DMA],
        name=name,
    )(buf)


def _region(ref, chip_axis=None, chip=None, chip_size=None, half_axis=None, half=None, half_size=None):
    idx = [slice(None)] * len(ref.shape)
    if chip is not None:
        idx[chip_axis] = pl.ds(chip * chip_size, chip_size)
    if half is not None:
        idx[half_axis] = pl.ds(half * half_size, half_size)
    return ref.at[tuple(idx)]


def gather_weights(shards, axes, name):
    n = len(shards)

    def full_shape(t):
        shp = list(shards[t].shape)
        shp[axes[t][0]] *= N_CHIPS
        return tuple(shp)

    def body(*refs):
        ins, outs = refs[:n], refs[n:2 * n]
        ici_send, ici_recv, d2d_send, d2d_recv, own_send, own_recv = refs[2 * n:]
        x, y, c = _place()
        me = 2 * x + y

        def part(t, ref, chip, half):
            ca, ha = axes[t]
            return _region(ref, ca, chip, ins[t].shape[ca], ha, half, ins[t].shape[ha] // 2)

        def own(t):
            return pltpu.make_async_remote_copy(src_ref=ins[t], dst_ref=part(t, outs[t], me, None),
                                                send_sem=own_send.at[t], recv_sem=own_recv.at[t],
                                                device_id=(x, y, 1 - c), device_id_type=MESH)

        started = []
        for t in range(n):
            own(t).start()
            started.append(own(t))
        for t in range(n):
            for k in range(1, N_CHIPS):
                px, py = _flip(x, k >> 1), _flip(y, k & 1)
                cp = pltpu.make_async_remote_copy(src_ref=part(t, ins[t], None, c), dst_ref=part(t, outs[t], me, c),
                                                  send_sem=ici_send.at[t, k - 1], recv_sem=ici_recv.at[t, k - 1],
                                                  device_id=(px, py, c), device_id_type=MESH)
                cp.start()
                started.append(cp)
        for t in range(n):
            for k in range(1, N_CHIPS):
                px, py = _flip(x, k >> 1), _flip(y, k & 1)
                got = part(t, outs[t], 2 * px + py, c)
                pltpu.make_async_remote_copy(src_ref=part(t, ins[t], None, c), dst_ref=got,
                                             send_sem=ici_send.at[t, k - 1], recv_sem=ici_recv.at[t, k - 1],
                                             device_id=(px, py, c), device_id_type=MESH).wait_recv()
                fw = pltpu.make_async_remote_copy(src_ref=got, dst_ref=got, send_sem=d2d_send.at[t, k - 1],
                                                  recv_sem=d2d_recv.at[t, k - 1], device_id=(x, y, 1 - c),
                                                  device_id_type=MESH)
                fw.start()
                started.append(fw)
        for t in range(n):
            for k in range(1, N_CHIPS):
                px, py = _flip(x, k >> 1), _flip(y, k & 1)
                theirs = part(t, outs[t], 2 * px + py, 1 - c)
                pltpu.make_async_remote_copy(src_ref=theirs, dst_ref=theirs, send_sem=d2d_send.at[t, k - 1],
                                             recv_sem=d2d_recv.at[t, k - 1], device_id=(x, y, 1 - c),
                                             device_id_type=MESH).wait_recv()
        for t in range(n):
            own(t).wait_recv()
        for cp in started:
            cp.wait_send()

    sem = pltpu.SemaphoreType.DMA((n, N_CHIPS - 1))
    own_sem = pltpu.SemaphoreType.DMA((n,))
    return pl.pallas_call(
        body, in_specs=[_ANY] * n, out_specs=[_ANY] * n,
        out_shape=[jax.ShapeDtypeStruct(full_shape(t), shards[t].dtype) for t in range(n)],
        scratch_shapes=[sem, sem, sem, sem, own_sem, own_sem], name=name,
    )(*shards)


_HBM = pl.BlockSpec(memory_space=pltpu.HBM)
_SEM = pl.BlockSpec(memory_space=pltpu.SEMAPHORE)
_EFFECT = pltpu.SideEffectType.DATAFLOW_SIDE_EFFECTING
WEIGHT_COPIES = N_CHIPS


def _weight_peer(k, x, y, c):
    return (x, y, 1 - c) if k == 0 else (_flip(x, k >> 1), _flip(y, k & 1), c)


def weights_start(shards, items, name):
    n_sh, n_it = len(shards), len(items)

    def src_of(refs, i):
        t, layer, _ = items[i]
        return refs[t] if layer is None else refs[t].at[layer]

    def land_shape(i):
        t, layer, ca = items[i]
        shp = list(shards[t].shape if layer is None else shards[t].shape[1:])
        shp[ca] *= N_CHIPS
        return tuple(shp)

    def body(*refs):
        shard_refs, land_refs = refs[:n_sh], refs[n_sh:n_sh + n_it]
        send_sems = refs[n_sh + n_it:n_sh + 2 * n_it]
        recv_sems = refs[n_sh + 2 * n_it:n_sh + 3 * n_it]
        token = refs[-1]
        x, y, c = _place()
        me = 2 * x + y
        for i in range(n_it):
            src = src_of(shard_refs, i)
            ca = items[i][2]
            dst = _region(land_refs[i], ca, me, src.shape[ca])
            for k in range(WEIGHT_COPIES):
                pltpu.make_async_remote_copy(src_ref=src, dst_ref=dst, send_sem=send_sems[i], recv_sem=recv_sems[i],
                                             device_id=_weight_peer(k, x, y, c), device_id_type=MESH).start()
        token[...] = jnp.zeros_like(token)

    lands = [pltpu.with_memory_space_constraint(lax.empty(land_shape(i), shards[0].dtype), pltpu.HBM)
             for i in range(n_it)]
    ins = [pltpu.with_memory_space_constraint(a, pltpu.HBM) for a in shards] + lands
    sems = (pltpu.SemaphoreType.DMA(()),) * (2 * n_it)
    outs = pl.pallas_call(
        body, name=name,
        out_shape=sems + tuple(pltpu.HBM(a.shape, a.dtype) for a in ins) + (jax.ShapeDtypeStruct((8, LANES), F32),),
        in_specs=[_HBM] * len(ins),
        out_specs=(_SEM,) * (2 * n_it) + (_HBM,) * len(ins) + (pl.BlockSpec(memory_space=pltpu.VMEM),),
        input_output_aliases={i: 2 * n_it + i for i in range(len(ins))},
        compiler_params=pltpu.CompilerParams(has_side_effects=_EFFECT),
    )(*ins)
    base = 2 * n_it
    return (list(outs[:n_it]), list(outs[n_it:base]), list(outs[base:base + n_sh]),
            list(outs[base + n_sh:base + n_sh + n_it]), outs[-1])


def weights_wait(send_sems, recv_sems, lands, after, keep, name):
    m = len(lands)

    def body(*refs):
        land_refs, send_refs, recv_refs = refs[:m], refs[m:2 * m], refs[2 * m:3 * m]
        x, y, c = _place()
        for j in range(m):
            cp = pltpu.make_async_remote_copy(src_ref=land_refs[j], dst_ref=land_refs[j], send_sem=send_refs[j],
                                              recv_sem=recv_refs[j], device_id=(x, y, 1 - c),
                                              device_id_type=MESH)
            cp.wait_send()
            cp.wait_recv()

    outs = pl.pallas_call(
        body, name=name,
        out_shape=tuple(pltpu.HBM(a.shape, a.dtype) for a in lands),
        in_specs=[_HBM] * m + [_SEM] * (2 * m) + [_ANY] + [_HBM] * len(keep),
        out_specs=(_HBM,) * m,
        input_output_aliases={j: j for j in range(m)},
        compiler_params=pltpu.CompilerParams(has_side_effects=_EFFECT),
    )(*lands, *send_sems, *recv_sems, after, *keep)
    return list(outs)


def reduce_to_sibling(lo, hi, name):
    n = len(lo)

    def body(*refs):
        los, his, outs = refs[:n], refs[n:2 * n], refs[2 * n:3 * n]
        send_sems, recv_sems = refs[3 * n:]
        x, y, c = _place()

        def copy(u, src):
            return pltpu.make_async_remote_copy(src_ref=src, dst_ref=outs[u], send_sem=send_sems.at[u],
                                                recv_sem=recv_sems.at[u], device_id=(x, y, 1 - c), device_id_type=MESH)

        for u in range(n):
            @pl.when(c == 0)
            def _(u=u):
                copy(u, his[u]).start()

            @pl.when(c == 1)
            def _(u=u):
                copy(u, los[u]).start()
        for u in range(n):
            copy(u, los[u]).wait_recv()
        for u in range(n):
            copy(u, los[u]).wait_send()

    return pl.pallas_call(
        body, in_specs=[_ANY] * (2 * n), out_specs=[_ANY] * n,
        out_shape=[jax.ShapeDtypeStruct(a.shape, a.dtype) for a in lo],
        scratch_shapes=[pltpu.SemaphoreType.DMA((n,)), pltpu.SemaphoreType.DMA((n,))], name=name,
    )(*lo, *hi)


def add_selected(lo, hi, other, name, tile_elems=1 << 19):
    R, C = lo.shape
    tr = _pick(R, max(16, tile_elems // C // 16 * 16), 16)

    def body(lo_ref, hi_ref, o_ref, out_ref):
        mine = jnp.where(lax.axis_index("c") == 0, lo_ref[...].astype(F32), hi_ref[...].astype(F32))
        out_ref[...] = (mine + o_ref[...].astype(F32)).astype(out_ref.dtype)

    blk = pl.BlockSpec((tr, C), lambda i: (i, 0))
    return pl.pallas_call(
        body, grid=(R // tr,), in_specs=[blk, blk, blk], out_specs=blk, out_shape=jax.ShapeDtypeStruct((R, C), BF16),
        compiler_params=_cparams("parallel"), name=name,
    )(lo, hi, other)


def scatter_to_chips(pieces, chip_axes, name):
    n = len(pieces)

    def block_shape(u):
        shp = list(pieces[u].shape)
        shp[chip_axes[u]] //= N_CHIPS
        return tuple(shp)

    def body(*refs):
        ins, outs = refs[:n], refs[n:2 * n]
        send_sems, recv_sems = refs[2 * n:]
        x, y, c = _place()
        me = 2 * x + y
        started = []
        for u in range(n):
            size = block_shape(u)[chip_axes[u]]
            for k in range(1, N_CHIPS):
                px, py = _flip(x, k >> 1), _flip(y, k & 1)
                cp = pltpu.make_async_remote_copy(src_ref=_region(ins[u], chip_axes[u], 2 * px + py, size),
                                                  dst_ref=outs[u].at[me], send_sem=send_sems.at[u, k - 1],
                                                  recv_sem=recv_sems.at[u, k - 1], device_id=(px, py, c),
                                                  device_id_type=MESH)
                cp.start()
                started.append(cp)
        for u in range(n):
            size = block_shape(u)[chip_axes[u]]
            for k in range(1, N_CHIPS):
                px, py = _flip(x, k >> 1), _flip(y, k & 1)
                pltpu.make_async_remote_copy(src_ref=_region(ins[u], chip_axes[u], me, size),
                                             dst_ref=outs[u].at[2 * px + py], send_sem=send_sems.at[u, k - 1],
                                             recv_sem=recv_sems.at[u, k - 1], device_id=(px, py, c),
                                             device_id_type=MESH).wait_recv()
        for cp in started:
            cp.wait_send()

    sem = pltpu.SemaphoreType.DMA((n, N_CHIPS - 1))
    return pl.pallas_call(
        body, in_specs=[_ANY] * n, out_specs=[_ANY] * n,
        out_shape=[jax.ShapeDtypeStruct((N_CHIPS,) + block_shape(u), pieces[u].dtype) for u in range(n)],
        scratch_shapes=[sem, sem], name=name,
    )(*pieces)


def gather_halves(parts, slots, out_shapes, name):
    n = len(parts)

    def body(*refs):
        ins, outs = refs[:n], refs[n:n + len(out_shapes)]
        send_sems, recv_sems = refs[n + len(out_shapes):]
        x, y, c = _place()
        started = []
        for u in range(n):
            t, s = slots[u]
            cp = pltpu.make_async_remote_copy(src_ref=ins[u], dst_ref=outs[t].at[c, s], send_sem=send_sems.at[u],
                                              recv_sem=recv_sems.at[u], device_id=(x, y, 1 - c), device_id_type=MESH)
            cp.start()
            started.append(cp)
        for u in range(n):
            t, s = slots[u]
            pltpu.make_async_remote_copy(src_ref=ins[u], dst_ref=outs[t].at[1 - c, s], send_sem=send_sems.at[u],
                                         recv_sem=recv_sems.at[u], device_id=(x, y, 1 - c),
                                         device_id_type=MESH).wait_recv()
        for cp in started:
            cp.wait_send()

    return pl.pallas_call(
        body, in_specs=[_ANY] * n, out_specs=[_ANY] * len(out_shapes),
        out_shape=[jax.ShapeDtypeStruct(shp, F32) for shp in out_shapes],
        scratch_shapes=[pltpu.SemaphoreType.DMA((n,)), pltpu.SemaphoreType.DMA((n,))], name=name,
    )(*parts)


WEIGHT_ORDER = ["mod_w", "mod_b", "norm1_g", "norm2_g", "pool_w", "pool_b", "pool_scale", "kv_in_g", "w_dkv",
                "ckv_norm_g", "w_uk", "w_uv", "w_dq", "q_norm_g", "w_uq", "w_o", "w_up", "conv_w", "conv_b", "w_down",
                "final_g"]
EXCHANGED = {"w_up": (2, 0), "w_down": (1, 0), "w_o": (1, 0), "w_uq": (2, 0), "w_dq": (1, 0), "pool_w": (2, 0),
             "w_dkv": (0, 1), "w_uk": (1, 0), "w_uv": (1, 0)}
SMALL_SHARDED = {"conv_w": 2, "pool_b": 1, "pool_scale": 1}
REPLICATED = ["mod_b", "norm1_g", "norm2_g", "kv_in_g", "ckv_norm_g", "q_norm_g", "conv_b", "final_g"]


def _padded(n, align):
    return -(-n // align) * align


def _flat_pad(parts, total):
    flat = jnp.concatenate(parts, axis=-1)
    pad = total - flat.shape[-1]
    if pad:
        flat = jnp.concatenate([flat, jnp.zeros(flat.shape[:-1] + (pad,), flat.dtype)], axis=-1)
    return flat


def _split_shards(full, axis):
    shp = full.shape
    t = full.reshape(shp[:axis] + (N_CHIPS, shp[axis] // N_CHIPS) + shp[axis + 1:])
    return jnp.moveaxis(t, axis, 0).reshape(N_CHIPS, -1)


def _join_shards(rows, shard_shape, axis):
    t = jnp.moveaxis(rows.reshape((N_CHIPS,) + tuple(shard_shape)), 0, axis)
    return t.reshape(tuple(shard_shape[:axis]) + (N_CHIPS * shard_shape[axis],) + tuple(shard_shape[axis + 1:]))


def _index(a, i, axis=0):
    return lax.dynamic_index_in_dim(a, i, axis, keepdims=False)


def kernel(x, c, positions, mod_w, mod_b, norm1_g, norm2_g, pool_w, pool_b, pool_scale, kv_in_g, w_dkv, ckv_norm_g, w_uk, w_uv, w_dq, q_norm_g, w_uq, w_o, w_up, conv_w, conv_b, w_down, final_g, loss_target, m_mod_w, m_mod_b, m_norm1_g, m_norm2_g, m_pool_w, m_pool_b, m_pool_scale, m_kv_in_g, m_w_dkv, m_ckv_norm_g, m_w_uk, m_w_uv, m_w_dq, m_q_norm_g, m_w_uq, m_w_o, m_w_up, m_conv_w, m_conv_b, m_w_down, m_final_g, v_mod_w, v_mod_b, v_norm1_g, v_norm2_g, v_pool_w, v_pool_b, v_pool_scale, v_kv_in_g, v_w_dkv, v_ckv_norm_g, v_w_uk, v_w_uv, v_w_dq, v_q_norm_g, v_w_uq, v_w_o, v_w_up, v_conv_w, v_conv_b, v_w_down, v_final_g):
    given = dict(locals())
    W = {n: given[n] for n in WEIGHT_ORDER}
    M1 = {n: given["m_" + n] for n in WEIGHT_ORDER}
    V2 = {n: given["v_" + n] for n in WEIGHT_ORDER}
    xi, yi, ci = lax.axis_index("x"), lax.axis_index("y"), lax.axis_index("c")
    chip = 2 * xi + yi
    dev = 4 * xi + 2 * yi + ci
    x0 = x[0]
    S_, D = x0.shape
    Fh = conv_b.shape[1]
    E = mod_b.shape[1]
    Es = E // N_CHIPS
    zD = jnp.zeros((D,), F32)

    names = list(EXCHANGED)
    shards = [W[n].astype(BF16) for n in names]
    n_mla = DEPTH - N_A
    items, groups = [], []

    def group(entries):
        groups.append(list(range(len(items), len(items) + len(entries))))
        for n, layer in entries:
            ca = EXCHANGED[n][0] - (0 if layer is None else 1)
            items.append((names.index(n), layer, 0 if n == "w_dkv" else ca))

    for l in range(N_A):
        group([("w_up", l), ("w_down", l), ("pool_w", l)])
    for j in range(n_mla):
        head = [("w_dkv", None), ("w_uk", None), ("w_uv", None)] if j == 0 else []
        group(head + [("w_dq", j), ("w_uq", j), ("w_o", j), ("w_up", N_A + j), ("w_down", N_A + j)])
    w_send, w_recv, shards_thru, lands, w_token = weights_start(shards, items, "weights_start")
    full = {}

    def weights_ready(g, after):
        keep = shards_thru if g == len(groups) - 1 else []
        got = weights_wait([w_send[i] for i in groups[g]], [w_recv[i] for i in groups[g]], [lands[i] for i in groups[g]],
                           after, keep, f"weights_wait{g}")
        for i, arr in zip(groups[g], got):
            t, layer, _ = items[i]
            full[(names[t], 0 if layer is None else layer)] = arr

    ssz = {n: math.prod(W[n].shape) for n in SMALL_SHARDED}
    Tw = _padded(sum(ssz.values()), 8 * PACK_COLS)
    small_rows = chip_gather(_flat_pad([W[n].reshape(-1) for n in SMALL_SHARDED], Tw).reshape(-1, PACK_COLS)
                             + w_token[0, 0], "gather_small_w").reshape(N_CHIPS, Tw)
    off = 0
    for n, axis in SMALL_SHARDED.items():
        full[n] = _join_shards(small_rows[:, off:off + ssz[n]], W[n].shape, axis)
        off += ssz[n]

    q_rank = W["w_uq"].shape[1]
    kv_w = KV_RANK + QK_ROPE

    def uq_ext(j):
        wq = full[("w_uq", j)].reshape(q_rank, N_HEADS, QK_HEAD)
        return jnp.concatenate([wq, jnp.zeros((q_rank, N_HEADS, HEAD_PAD - QK_HEAD), BF16)],
                               axis=2).reshape(q_rank, N_HEADS * HEAD_PAD)

    c_all = device_gather(c, "gather_c").reshape(N_DEV, D)
    c_pad = jnp.concatenate([c_all, jnp.zeros((16 - N_DEV, D), F32)], axis=0)
    mod_b_mine = lax.dynamic_slice_in_dim(mod_b, chip * Es, Es, axis=1)
    mods_part = mods_fwd(c_pad, mod_w, mod_b_mine, "mods_fwd")
    mods_all = chip_gather(mods_part, "gather_mods")
    mods = jnp.swapaxes(_index(mods_all, dev, axis=2), 0, 1).reshape(DEPTH, E)
    mod = [[mods[l, k * D:(k + 1) * D] for k in range(6)] for l in range(DEPTH)]

    half = QK_ROPE // 2
    inv = 1.0 / (ROPE_THETA ** (jnp.arange(0, QK_ROPE, 2, dtype=F32) / QK_ROPE))
    inv_row = jnp.concatenate([inv, inv, jnp.zeros((LANES - 2 * half,), F32)]).reshape(1, LANES)
    tabs = rope_tables(positions[0].astype(F32).reshape(S_, 1), inv_row, "rope_tables")
    att_scale = QK_HEAD ** -0.5

    saved = []
    xcur = x0
    kv_saved = None
    K = VX = knv = None
    for l in range(DEPTH):
        sh1, sc1, g1, sh2, sc2, g2 = mod[l]
        st = {"xin": xcur}
        weights_ready(l, xcur if l else mods)
        if l == N_A:
            w_dkv_ext = jnp.concatenate([full[("w_dkv", 0)], jnp.zeros((D, KV_RANK + LANES - kv_w), BF16)], axis=1)
            w_ukv = jnp.concatenate([full[("w_uk", 0)], full[("w_uv", 0)]], axis=1)
            xn = norm_fwd(xcur, kv_in_g, zD, zD, BF16, "kvin_fwd")
            kv_ext = mm(xn, w_dkv_ext, "nn", F32, "dkv_mm")
            lat = kv_ext[:, :KV_RANK]
            zk = jnp.zeros((KV_RANK,), F32)
            ckv = norm_fwd(lat, ckv_norm_g, zk, zk, BF16, "ckv_fwd")
            knv = mm(ckv, w_ukv, "nn", BF16, "ukv_mm")
            K, VX = k_prep(knv, kv_ext, tabs, "k_prep")
            kv_saved = {"x": xcur, "xn": xn, "lat": lat, "ckv": ckv}
        if l < N_A:
            h1 = norm_fwd(xcur, norm1_g[l], sc1, sh1, F32, f"norm1_fwd{l}")
            st["pooled"] = _pool_call(h1, BF16, f"pool_fwd{l}", False)
            st["cs"] = g1 * full["pool_scale"][l]
            st["ypre"], xmid = gmm(st["pooled"], full[("pool_w", l)], "nn", F32, f"pool_mm{l}", bias=full["pool_b"][l],
                                   res=xcur, colscale=st["cs"])
        else:
            j = l - N_A
            st["h1"] = norm_fwd(xcur, norm1_g[l], sc1, sh1, BF16, f"norm1_fwd{l}")
            st["ql"] = mm(st["h1"], full[("w_dq", j)], "nn", F32, f"dq_mm{l}")
            st["cq"] = norm_fwd(st["ql"], q_norm_g[j], jnp.zeros_like(q_norm_g[j]), jnp.zeros_like(q_norm_g[j]), BF16,
                                f"qnorm_fwd{l}")
            st["w_uq_ext"] = uq_ext(j)
            qe = mm(st["cq"], st["w_uq_ext"], "nn", F32, f"uq_mm{l}")
            st["Q"] = q_prep(qe, tabs, att_scale, False, f"q_prep{l}")
            st["o"], lse = attn_fwd(st["Q"], K, VX, f"attn_fwd{l}")
            st["lse"] = lse.reshape(N_HEADS, 1, S_)
            st["y"], xmid = mm(st["o"], full[("w_o", j)], "nn", F32, f"wo_mm{l}", res=xcur, colscale=g1)
        st["xmid"] = xmid
        st["h2"] = norm_fwd(xmid, norm2_g[l], sc2, sh2, BF16, f"norm2_fwd{l}")
        st["u"] = mm(st["h2"], full[("w_up", l)], "nn", BF16, f"up_mm{l}")
        st["z"] = glu_fwd(st["u"], full["conv_w"][l], conv_b[l], f"glu_fwd{l}")
        st["f"], xcur = mm(st["z"], full[("w_down", l)], "nn", F32, f"down_mm{l}", tk=1408, res=xmid, colscale=g2)
        saved.append(st)

    dx, d_final_g, loss_part = loss_head(xcur, final_g, loss_target[0], "loss_head")
    loss = lax.psum(loss_part[0, 0], ("x", "y", "c"))

    G = {}
    dmods = [None] * DEPTH
    d_norm1 = [None] * DEPTH
    d_norm2 = [None] * DEPTH
    d_conv_b = [None] * DEPTH
    d_qnorm = [None] * n_mla
    dkv_acc = []
    for l in reversed(range(DEPTH)):
        sh1, sc1, g1, sh2, sc2, g2 = mod[l]
        st = saved[l]
        df, a2, _ = gate_bwd(dx, st["f"], g2, f"gate2_bwd{l}")
        dz = mm(df, full[("w_down", l)], "nt", BF16, f"down_dx{l}")
        G[("w_down", l)] = mm(st["z"], df, "tn", BF16, f"down_dw{l}")
        du, dcw, dcb = glu_bwd(st["u"], dz, full["conv_w"][l], conv_b[l], f"glu_bwd{l}")
        G[("conv_w", l)] = dcw
        d_conv_b[l] = dcb[0]
        dh2 = mm(du, full[("w_up", l)], "nt", BF16, f"up_dx{l}", tk=1408)
        G[("w_up", l)] = mm(st["h2"], du, "tn", BF16, f"up_dw{l}")
        dxmid, s1, s2 = norm_bwd(st["xmid"], norm2_g[l], sc2, dh2, dx, f"norm2_bwd{l}")
        dsh2, dsc2, d_norm2[l] = s1[0], s2[0] * norm2_g[l], s2[0] * (1.0 + sc2)
        if l < N_A:
            dyp, a1, csum = gate_bwd(dxmid, st["ypre"], st["cs"], f"gate1_bwd{l}")
            dg1 = full["pool_scale"][l] * a1[0]
            G[("pool_scale", l)] = g1 * a1[0]
            G[("pool_b", l)] = st["cs"] * csum[0]
            dpooled = gmm(dyp, full[("pool_w", l)], "nt", F32, f"pool_dx{l}")
            G[("pool_w", l)] = gmm(st["pooled"], dyp, "tn", BF16, f"pool_dw{l}")
            dh1 = _pool_call(dpooled, F32, f"pool_bwd{l}", True)
        else:
            j = l - N_A
            dy, a1, _ = gate_bwd(dxmid, st["y"], g1, f"gate1_bwd{l}")
            dg1 = a1[0]
            do = mm(dy, full[("w_o", j)], "nt", BF16, f"wo_dx{l}")
            G[("w_o", j)] = mm(st["o"], dy, "tn", BF16, f"wo_dw{l}")
            delta = attn_delta(st["o"], do, f"attn_delta{l}").reshape(N_HEADS, 1, S_)
            dQ, dK, dV = attn_bwd(st["Q"], K, VX, do, st["lse"], delta, f"attn_bwd{l}")
            dkv_acc.append((dK, dV))
            dqe = q_prep(dQ, tabs, att_scale, True, f"q_prep_bwd{l}")
            dcq = mm(dqe, st["w_uq_ext"], "nt", F32, f"uq_dx{l}")
            G[("w_uq", j)] = mm(st["cq"], dqe, "tn", BF16, f"uq_dw{l}").reshape(q_rank, N_HEADS, HEAD_PAD)[
                :, :, :QK_HEAD].reshape(q_rank, N_HEADS * QK_HEAD)
            zq = jnp.zeros_like(q_norm_g[j])
            dql, _, s2q = norm_bwd(st["ql"], q_norm_g[j], zq, dcq, None, f"qnorm_bwd{l}")
            d_qnorm[j] = s2q[0]
            dh1 = mm(dql, full[("w_dq", j)], "nt", BF16, f"dq_dx{l}")
            G[("w_dq", j)] = mm(st["h1"], dql, "tn", BF16, f"dq_dw{l}")
        dx, s1, s2 = norm_bwd(st["xin"], norm1_g[l], sc1, dh1, dxmid, f"norm1_bwd{l}")
        dsh1, dsc1, d_norm1[l] = s1[0], s2[0] * norm1_g[l], s2[0] * (1.0 + sc1)
        dmods[l] = jnp.concatenate([dsh1, dsc1, dg1, dsh2, dsc2, a2[0]])
        if l == N_A:
            (dk_a, dv_a), (dk_b, dv_b) = dkv_acc
            dknv, d_tk = k_prep_bwd(dk_a, dk_b, dv_a, dv_b, tabs, "k_prep_bwd")
            dckv = mm(dknv, w_ukv, "nt", F32, "ukv_dx")
            d_ukv = mm(kv_saved["ckv"], dknv, "tn", BF16, "ukv_dw")
            G[("w_uk", 0)], G[("w_uv", 0)] = d_ukv[:, :N_HEADS * QK_NOPE], d_ukv[:, N_HEADS * QK_NOPE:]
            zk = jnp.zeros((KV_RANK,), F32)
            dlat, _, s2c = norm_bwd(kv_saved["lat"], ckv_norm_g, zk, dckv, None, "ckv_bwd")
            d_ckv_g = s2c[0]
            dkv_ext = jnp.concatenate([dlat, d_tk], axis=1)
            dxn = mm(dkv_ext, w_dkv_ext, "nt", BF16, "dkv_dx")
            G[("w_dkv", 0)] = mm(kv_saved["xn"], dkv_ext, "tn", BF16, "dkv_dw")[:, :kv_w]
            dx, _, s2k = norm_bwd(kv_saved["x"], kv_in_g, zD, dxn, dx, "kvin_bwd")
            d_kvin_g = s2k[0]

    units = []
    for n, (ca, ha) in EXCHANGED.items():
        if W[n].ndim > 2:
            half_layers = W[n].shape[0] // 2
            for sl in range(half_layers):
                units.append((n, sl, G[(n, sl)], G[(n, half_layers + sl)], ca - 1))
        elif n == "w_dkv":
            g4 = G[(n, 0)].reshape(N_CHIPS, 2, -1, kv_w)
            units.append((n, 0, g4[:, 0], g4[:, 1], 0))
        else:
            rows_half = W[n].shape[0] // 2
            units.append((n, 0, G[(n, 0)][:rows_half], G[(n, 0)][rows_half:], ca))
    lo = [u[2] for u in units]
    hi = [u[3] for u in units]
    theirs = reduce_to_sibling(lo, hi, "reduce_cores")

    def flat2(a):
        return a.reshape(-1, a.shape[-1])

    sums = [add_selected(flat2(l_), flat2(h_), flat2(t_), f"reduce_cores_add{i}").reshape(l_.shape)
            for i, (l_, h_, t_) in enumerate(zip(lo, hi, theirs))]
    axes = [u[4] for u in units]
    got = scatter_to_chips(sums, axes, "reduce_chips")
    reduced = []
    for i, (sm, ax, g4) in enumerate(zip(sums, axes, got)):
        size = sm.shape[ax] // N_CHIPS
        g4 = lax.dynamic_update_index_in_dim(g4, lax.dynamic_slice_in_dim(sm, chip * size, size, axis=ax), chip, 0)
        blk = g4.shape[1:]
        reduced.append(sum_parts(g4.reshape(N_CHIPS, -1, blk[-1]), f"reduce_chips_add{i}").reshape(blk))
    slots, out_shapes = [], []
    for n in EXCHANGED:
        mine = [i for i, u in enumerate(units) if u[0] == n]
        out_shapes.append((2, len(mine)) + reduced[mine[0]].shape)
        slots += [(len(out_shapes) - 1, units[i][1]) for i in mine]
    halves = gather_halves(reduced, slots, out_shapes, "reduce_gather")

    grads, deltas, new_m, new_v = {}, {}, {}, {}
    for ti, n in enumerate(EXCHANGED):
        g = halves[ti]
        for i, u in enumerate(units):
            if u[0] == n:
                g = lax.dynamic_update_slice(g, reduced[i][None, None], (ci, u[1]) + (0,) * reduced[i].ndim)
        grads[n] = g.reshape(W[n].shape)
        deltas[n], new_m[n], new_v[n] = adamw(W[n], grads[n], M1[n], V2[n], f"adamw_{n}")

    small = {"mod_b": jnp.stack(dmods), "norm1_g": jnp.stack(d_norm1), "norm2_g": jnp.stack(d_norm2),
             "kv_in_g": d_kvin_g, "ckv_norm_g": d_ckv_g, "q_norm_g": jnp.stack(d_qnorm),
             "conv_b": jnp.stack(d_conv_b), "final_g": d_final_g[0]}
    extra = {n: jnp.stack([G[(n, i)] for i in range(W[n].shape[0])]) for n in SMALL_SHARDED}
    ssizes = {n: math.prod(W[n].shape) for n in REPLICATED}
    esizes = {n: math.prod(extra[n].shape) for n in SMALL_SHARDED}
    Ts = _padded(sum(ssizes.values()) + sum(esizes.values()), 8 * PACK_COLS)

    def pack_small(d, tail=()):
        return _flat_pad([d[n].reshape(-1) for n in REPLICATED] + [t.reshape(-1) for t in tail],
                         Ts).reshape(Ts // PACK_COLS, PACK_COLS)

    parts = device_gather(pack_small(small, [extra[n] for n in SMALL_SHARDED]), "gather_small")
    outs = adamw_sum(parts, pack_small(W), pack_small(M1), pack_small(V2), "adamw_small")
    off = 0
    for n in REPLICATED:
        for dst, o in zip((grads, deltas, new_m, new_v), outs):
            dst[n] = o.reshape(-1)[off:off + ssizes[n]].reshape(W[n].shape)
        off += ssizes[n]
    for n, axis in SMALL_SHARDED.items():
        g_full = outs[0].reshape(-1)[off:off + esizes[n]].reshape(extra[n].shape)
        off += esizes[n]
        size = W[n].shape[axis]
        grads[n] = lax.dynamic_slice_in_dim(g_full, chip * size, size, axis=axis)
        deltas[n], new_m[n], new_v[n] = adamw(W[n], grads[n], M1[n], V2[n], f"adamw_{n}")

    dm_all = parts.reshape(N_DEV, -1)[:, :DEPTH * E].reshape(N_DEV, DEPTH, E)
    dm_mine = jnp.swapaxes(lax.dynamic_slice_in_dim(dm_all, chip * Es, Es, axis=2), 0, 1)
    grads["mod_w"], deltas["mod_w"], new_m["mod_w"], new_v["mod_w"] = adamw_modw(
        c_all.reshape(N_DEV, D, 1), dm_mine, mod_w, m_mod_w, v_mod_w, "adamw_mod_w")

    return (loss, dx.reshape(x.shape), *[grads[n] for n in WEIGHT_ORDER], *[deltas[n] for n in WEIGHT_ORDER],
            *[new_m[n] for n in WEIGHT_ORDER], *[new_v[n] for n in WEIGHT_ORDER])
```

```python
import functools
import math

import jax
import jax.numpy as jnp
from jax import lax
from jax.experimental import pallas as pl
from jax.experimental.pallas import tpu as pltpu

F32 = jnp.float32
BF16 = jnp.bfloat16
MESH = pl.DeviceIdType.MESH

DEPTH = 4
N_A = 2
POOL_WINDOWS = (2, 4, 8, 16)
N_GROUPS = 4
N_HEADS = 8
QK_NOPE = 128
QK_ROPE = 64
V_HEAD = 128
QK_HEAD = QK_NOPE + QK_ROPE
HEAD_PAD = 256
KV_RANK = 256
ROPE_THETA = 10000.0
EPS = 1e-6
ADAM_LR = 0.001
ADAM_B1 = 0.9
ADAM_B2 = 0.999
ADAM_EPS = 1e-08
ADAM_WD = 0.01
ADAM_STEP = 10

N_CHIPS = 4
N_DEV = 8
LANES = 128
PACK_COLS = 1024
VMEM_LIMIT = 56 * 1024 * 1024
GLU_TILE = 256
ATT_BWD_K_BLOCK = 256
ATT_BWD_Q_BLOCK = 512
ATT_Q_BLOCK = 256
ATT_K_BLOCK = 512
ATT_HEADS_PER_STEP = 2


def _cparams(*sem):
    return pltpu.CompilerParams(dimension_semantics=sem if sem else None, vmem_limit_bytes=VMEM_LIMIT)


def _pick(n, target, mult):
    best = None
    d = mult
    while d <= min(n, target):
        if n % d == 0:
            best = d
        d += mult
    return n if best is None else best


def _row(v):
    return v.reshape(1, -1).astype(F32)


_DIMS = {"nn": (((1,), (0,)), ((), ())), "nt": (((1,), (1,)), ((), ())), "tn": (((0,), (0,)), ((), ()))}


def _mm_body(mode, nk, has_bias, has_res):
    def body(*refs):
        a_ref, b_ref = refs[0], refs[1]
        pos = 2
        bias_ref = res_ref = cs_ref = None
        if has_bias:
            bias_ref = refs[pos]
            pos += 1
        if has_res:
            res_ref, cs_ref = refs[pos], refs[pos + 1]
            pos += 2
        o_ref = refs[pos]
        pos += 1
        o2_ref = None
        if has_res:
            o2_ref = refs[pos]
            pos += 1
        acc_ref = refs[pos] if nk > 1 else None
        k = pl.program_id(2)
        part = lax.dot_general(a_ref[...].astype(BF16), b_ref[...].astype(BF16), _DIMS[mode],
                               preferred_element_type=F32)

        def finish(y):
            if has_bias:
                y = y + bias_ref[...]
            o_ref[...] = y.astype(o_ref.dtype)
            if has_res:
                o2_ref[...] = res_ref[...] + cs_ref[...] * y

        if nk == 1:
            finish(part)
            return

        @pl.when(k == 0)
        def _():
            acc_ref[...] = part

        @pl.when((k > 0) & (k < nk - 1))
        def _():
            acc_ref[...] += part

        @pl.when(k == nk - 1)
        def _():
            finish(acc_ref[...] + part)

    return body


def mm(a, b, mode, out_dtype, name, *, tm=1408, tn=1408, tk=1024, bias=None, res=None, colscale=None, layer=None):
    bshape = b.shape if layer is None else b.shape[1:]
    if mode == "nn":
        (M, K), N = a.shape, bshape[1]
    elif mode == "nt":
        (M, K), N = a.shape, bshape[0]
    else:
        (K, M), N = a.shape, bshape[1]
    tm = _pick(M, tm, LANES if mode == "tn" else 8)
    tn = _pick(N, tn, LANES)
    tk = _pick(K, tk, LANES) if mode != "tn" else _pick(K, tk, 8)
    nk = K // tk
    a_spec = {"nn": pl.BlockSpec((tm, tk), lambda i, j, k: (i, k)),
              "nt": pl.BlockSpec((tm, tk), lambda i, j, k: (i, k)),
              "tn": pl.BlockSpec((tk, tm), lambda i, j, k: (k, i))}[mode]
    b_blk, b_map = {"nn": ((tk, tn), lambda i, j, k: (k, j)),
                    "nt": ((tn, tk), lambda i, j, k: (j, k)),
                    "tn": ((tk, tn), lambda i, j, k: (k, j))}[mode]
    if layer is None:
        b_spec = pl.BlockSpec(b_blk, b_map)
    else:
        b_spec = pl.BlockSpec((None,) + b_blk, lambda i, j, k: (layer,) + b_map(i, j, k))
    o_spec = pl.BlockSpec((tm, tn), lambda i, j, k: (i, j))
    v_spec = pl.BlockSpec((1, tn), lambda i, j, k: (0, j))
    in_specs, args = [a_spec, b_spec], [a, b]
    if bias is not None:
        in_specs.append(v_spec)
        args.append(_row(bias))
    out_shape = [jax.ShapeDtypeStruct((M, N), out_dtype)]
    out_specs = [o_spec]
    if res is not None:
        in_specs += [o_spec, v_spec]
        args += [res, _row(colscale)]
        out_shape.append(jax.ShapeDtypeStruct((M, N), F32))
        out_specs.append(o_spec)
    outs = pl.pallas_call(
        _mm_body(mode, nk, bias is not None, res is not None),
        grid=(M // tm, N // tn, nk),
        in_specs=in_specs, out_specs=out_specs, out_shape=out_shape,
        scratch_shapes=[pltpu.VMEM((tm, tn), F32)] if nk > 1 else [],
        compiler_params=_cparams("parallel", "parallel", "arbitrary"),
        name=name,
    )(*args)
    return outs if res is not None else outs[0]


def gmm(a, w, mode, out_dtype, name, *, bias=None, res=None, colscale=None, tr=512):
    S_ = a.shape[0]
    G = N_GROUPS
    C = a.shape[1] // G
    tr = _pick(S_, tr, 8)
    nr = S_ // tr
    if mode == "tn":
        def body(a_ref, b_ref, o_ref, acc_ref):
            i = pl.program_id(1)

            @pl.when(i == 0)
            def _():
                acc_ref[...] = jnp.zeros_like(acc_ref)

            acc_ref[...] += lax.dot_general(a_ref[...].astype(BF16), b_ref[...].astype(BF16), _DIMS["tn"],
                                            preferred_element_type=F32)

            @pl.when(i == nr - 1)
            def _():
                o_ref[...] = acc_ref[...].astype(o_ref.dtype)

        blk = pl.BlockSpec((tr, C), lambda g, i: (i, g))
        return pl.pallas_call(
            body, grid=(G, nr), in_specs=[blk, blk],
            out_specs=pl.BlockSpec((None, C, C), lambda g, i: (g, 0, 0)),
            out_shape=jax.ShapeDtypeStruct((G, C, C), out_dtype),
            scratch_shapes=[pltpu.VMEM((C, C), F32)],
            compiler_params=_cparams("parallel", "arbitrary"), name=name,
        )(a, w)

    has_bias, has_res = bias is not None, res is not None

    def body(*refs):
        a_ref, w_ref = refs[0], refs[1]
        pos = 2
        if has_bias:
            bias_ref = refs[pos]
            pos += 1
        if has_res:
            res_ref, cs_ref = refs[pos], refs[pos + 1]
            pos += 2
        o_ref = refs[pos]
        y = lax.dot_general(a_ref[...].astype(BF16), w_ref[...].astype(BF16), _DIMS[mode],
                            preferred_element_type=F32)
        if has_bias:
            y = y + bias_ref[...]
        o_ref[...] = y.astype(o_ref.dtype)
        if has_res:
            refs[pos + 1][...] = res_ref[...] + cs_ref[...] * y

    blk = pl.BlockSpec((tr, C), lambda i, g: (i, g))
    vec = pl.BlockSpec((1, C), lambda i, g: (0, g))
    in_specs = [blk, pl.BlockSpec((None, C, C), lambda i, g: (g, 0, 0))]
    args = [a, w]
    if has_bias:
        in_specs.append(vec)
        args.append(_row(bias))
    out_shape = [jax.ShapeDtypeStruct(a.shape, out_dtype)]
    out_specs = [blk]
    if has_res:
        in_specs += [blk, vec]
        args += [res, _row(colscale)]
        out_shape.append(jax.ShapeDtypeStruct(a.shape, F32))
        out_specs.append(blk)
    outs = pl.pallas_call(
        body, grid=(nr, G), in_specs=in_specs, out_specs=out_specs, out_shape=out_shape,
        compiler_params=_cparams("parallel", "parallel"), name=name,
    )(*args)
    return outs if has_res else outs[0]


def norm_fwd(x, g, sc, sh, out_dtype, name, tr=512):
    S_, Dn = x.shape
    tr = _pick(S_, tr, 8)

    def body(x_ref, g_ref, sc_ref, sh_ref, o_ref):
        xv = x_ref[...]
        r = lax.rsqrt(jnp.mean(xv * xv, axis=-1, keepdims=True) + EPS)
        o_ref[...] = (((xv * r) * g_ref[...]) * (1.0 + sc_ref[...]) + sh_ref[...]).astype(o_ref.dtype)

    blk = pl.BlockSpec((tr, Dn), lambda i: (i, 0))
    vec = pl.BlockSpec((1, Dn), lambda i: (0, 0))
    return pl.pallas_call(
        body, grid=(S_ // tr,), in_specs=[blk, vec, vec, vec], out_specs=blk,
        out_shape=jax.ShapeDtypeStruct((S_, Dn), out_dtype),
        compiler_params=_cparams("parallel"), name=name,
    )(x, _row(g), _row(sc), _row(sh))


def norm_bwd(x, g, sc, dh, dres, name, tr=512):
    S_, Dn = x.shape
    tr = _pick(S_, tr, 8)
    has_res = dres is not None

    def body(*refs):
        x_ref, g_ref, sc_ref, dh_ref = refs[:4]
        pos = 4
        if has_res:
            dres_ref = refs[pos]
            pos += 1
        dx_ref, s1_ref, s2_ref = refs[pos:pos + 3]
        i = pl.program_id(0)

        @pl.when(i == 0)
        def _():
            s1_ref[...] = jnp.zeros_like(s1_ref)
            s2_ref[...] = jnp.zeros_like(s2_ref)

        xv = x_ref[...]
        r = lax.rsqrt(jnp.mean(xv * xv, axis=-1, keepdims=True) + EPS)
        n = xv * r
        dhv = dh_ref[...].astype(F32)
        dn = dhv * (g_ref[...] * (1.0 + sc_ref[...]))
        dx = r * (dn - n * jnp.mean(dn * n, axis=-1, keepdims=True))
        if has_res:
            dx = dx + dres_ref[...]
        dx_ref[...] = dx
        s1_ref[...] += jnp.sum(dhv, axis=0, keepdims=True)
        s2_ref[...] += jnp.sum(dhv * n, axis=0, keepdims=True)

    blk = pl.BlockSpec((tr, Dn), lambda i: (i, 0))
    vec = pl.BlockSpec((1, Dn), lambda i: (0, 0))
    in_specs, args = [blk, vec, vec, blk], [x, _row(g), _row(sc), dh]
    if has_res:
        in_specs.append(blk)
        args.append(dres)
    vshape = jax.ShapeDtypeStruct((1, Dn), F32)
    return pl.pallas_call(
        body, grid=(S_ // tr,), in_specs=in_specs, out_specs=[blk, vec, vec],
        out_shape=[jax.ShapeDtypeStruct((S_, Dn), F32), vshape, vshape],
        compiler_params=_cparams("arbitrary"), name=name,
    )(*args)


def gate_bwd(dx, y, colscale, name, tr=512):
    S_, Dn = dx.shape
    tr = _pick(S_, tr, 8)

    def body(dx_ref, y_ref, cs_ref, d_ref, a_ref, c_ref):
        i = pl.program_id(0)

        @pl.when(i == 0)
        def _():
            a_ref[...] = jnp.zeros_like(a_ref)
            c_ref[...] = jnp.zeros_like(c_ref)

        dxv = dx_ref[...]
        d_ref[...] = (dxv * cs_ref[...]).astype(d_ref.dtype)
        a_ref[...] += jnp.sum(dxv * y_ref[...].astype(F32), axis=0, keepdims=True)
        c_ref[...] += jnp.sum(dxv, axis=0, keepdims=True)

    blk = pl.BlockSpec((tr, Dn), lambda i: (i, 0))
    vec = pl.BlockSpec((1, Dn), lambda i: (0, 0))
    vshape = jax.ShapeDtypeStruct((1, Dn), F32)
    return pl.pallas_call(
        body, grid=(S_ // tr,), in_specs=[blk, blk, vec], out_specs=[blk, vec, vec],
        out_shape=[jax.ShapeDtypeStruct((S_, Dn), BF16), vshape, vshape],
        compiler_params=_cparams("arbitrary"), name=name,
    )(dx, y, _row(colscale))


def loss_head(x, g, target, name, tr=512):
    S_, Dn = x.shape
    tr = _pick(S_, tr, 8)

    def body(x_ref, g_ref, t_ref, dx_ref, dg_ref, loss_ref):
        i = pl.program_id(0)

        @pl.when(i == 0)
        def _():
            dg_ref[...] = jnp.zeros_like(dg_ref)
            loss_ref[...] = jnp.zeros_like(loss_ref)

        xv = x_ref[...]
        r = lax.rsqrt(jnp.mean(xv * xv, axis=-1, keepdims=True) + EPS)
        n = xv * r
        e = n * g_ref[...] - t_ref[...]
        loss_ref[...] += 0.5 * jnp.sum(jnp.mean(e * e, axis=-1, keepdims=True), axis=0, keepdims=True)
        dy = e * (1.0 / Dn)
        dg_ref[...] += jnp.sum(dy * n, axis=0, keepdims=True)
        dn = dy * g_ref[...]
        dx_ref[...] = r * (dn - n * jnp.mean(dn * n, axis=-1, keepdims=True))

    blk = pl.BlockSpec((tr, Dn), lambda i: (i, 0))
    vec = pl.BlockSpec((1, Dn), lambda i: (0, 0))
    one = pl.BlockSpec((1, 1), lambda i: (0, 0))
    return pl.pallas_call(
        body, grid=(S_ // tr,), in_specs=[blk, vec, blk], out_specs=[blk, vec, one],
        out_shape=[jax.ShapeDtypeStruct((S_, Dn), F32), jax.ShapeDtypeStruct((1, Dn), F32),
                   jax.ShapeDtypeStruct((1, 1), F32)],
        compiler_params=_cparams("arbitrary"), name=name,
    )(x, _row(g), target)


POOL_HALO = 16
POOL_CHUNK = 512


def _rows(ref, lo, hi, n_rows):
    parts = []
    if lo < 0:
        parts.append(jnp.zeros((-lo, ref.shape[1]), F32))
    parts.append(ref[max(lo, 0):min(hi, n_rows), :].astype(F32))
    if hi > n_rows:
        parts.append(jnp.zeros((hi - n_rows, ref.shape[1]), F32))
    return parts[0] if len(parts) == 1 else jnp.concatenate(parts, axis=0)


def _window_sum(e, w, back):
    n = e.shape[0]
    s, width = e, 1
    while width < w:
        s = s + pltpu.roll(s, width if back else n - width, 0)
        width *= 2
    return s


def _pool_call(h, out_dtype, name, backward):
    S_, Dn = h.shape
    C = Dn // N_GROUPS
    ch = _pick(S_, POOL_CHUNK, 8)

    def body(h_ref, o_ref):
        g = pl.program_id(0)
        for gi, w in enumerate(POOL_WINDOWS):
            @pl.when(g == gi)
            def _(w=w):
                for r0 in range(0, S_, ch):
                    t = (r0 + lax.broadcasted_iota(jnp.int32, (ch, C), 0)).astype(F32)
                    cnt = jnp.minimum(t + 1.0, float(w))
                    if not backward:
                        ext = _rows(h_ref, r0 - POOL_HALO, r0 + ch, S_)
                        cur = ext[POOL_HALO:]
                        mean = _window_sum(ext, w, True)[POOL_HALO:] / cnt
                        o_ref[r0:r0 + ch, :] = (mean - cur).astype(o_ref.dtype)
                    else:
                        ext = _rows(h_ref, r0, r0 + ch + POOL_HALO, S_)
                        text = (r0 + lax.broadcasted_iota(jnp.int32, (ch + POOL_HALO, C), 0)).astype(F32)
                        e = ext / jnp.minimum(text + 1.0, float(w))
                        o_ref[r0:r0 + ch, :] = (_window_sum(e, w, False)[:ch] - ext[:ch]).astype(o_ref.dtype)

    blk = pl.BlockSpec((S_, C), lambda g: (0, g))
    return pl.pallas_call(
        body, grid=(N_GROUPS,), in_specs=[blk], out_specs=blk,
        out_shape=jax.ShapeDtypeStruct((S_, Dn), out_dtype),
        compiler_params=_cparams("parallel"), name=name,
    )(h)


GLU_CHUNK = 512
GLU_HALO = 16
_SQRT_HALF = 0.7071067811865476
_INV_SQRT_2PI = 0.3989422804014327


def _gelu(a):
    return 0.5 * a * (1.0 + lax.erf(a * _SQRT_HALF))


def _gelu_grad(a):
    return 0.5 * (1.0 + lax.erf(a * _SQRT_HALF)) + a * (_INV_SQRT_2PI * jnp.exp(-0.5 * a * a))


def glu_fwd(u, conv_w, conv_b, name):
    S_, F2 = u.shape
    Fh = F2 // 2
    tf = GLU_TILE
    nt = Fh // tf
    ch = _pick(S_, GLU_CHUNK, GLU_HALO)

    def body(a_ref, v_ref, cw_ref, cb_ref, z_ref):
        cw0, cw1, cw2 = cw_ref[0:1, :], cw_ref[1:2, :], cw_ref[2:3, :]
        cb = cb_ref[...]
        for r0 in range(0, S_, ch):
            ext = _rows(a_ref, r0 - GLU_HALO, r0 + ch, S_)
            a0 = ext[GLU_HALO:]
            a1 = pltpu.roll(ext, 1, 0)[GLU_HALO:]
            a2 = pltpu.roll(ext, 2, 0)[GLU_HALO:]
            ac = a2 * cw0 + a1 * cw1 + a0 * cw2 + cb
            z_ref[r0:r0 + ch, :] = (_gelu(ac) * v_ref[r0:r0 + ch, :].astype(F32)).astype(z_ref.dtype)

    return pl.pallas_call(
        body, grid=(nt,),
        in_specs=[pl.BlockSpec((S_, tf), lambda j: (0, j)), pl.BlockSpec((S_, tf), lambda j: (0, j + nt)),
                  pl.BlockSpec((3, tf), lambda j: (0, j)), pl.BlockSpec((1, tf), lambda j: (0, j))],
        out_specs=pl.BlockSpec((S_, tf), lambda j: (0, j)),
        out_shape=jax.ShapeDtypeStruct((S_, Fh), BF16),
        compiler_params=_cparams("parallel"), name=name,
    )(u, u, conv_w, _row(conv_b))


def glu_bwd(u, dz, conv_w, conv_b, name):
    S_, F2 = u.shape
    Fh = F2 // 2
    tf = GLU_TILE
    nt = Fh // tf
    ch = _pick(S_, GLU_CHUNK, GLU_HALO)

    def body(a_ref, v_ref, dz_ref, cw_ref, cb_ref, du_ref, dcw_ref, dcb_ref, da_buf, dv_buf, sems):
        j = pl.program_id(0)
        slot = j % 2

        def writes(step, sl):
            lo = pl.multiple_of(step * tf, tf)
            return (pltpu.make_async_copy(da_buf.at[sl], du_ref.at[:, pl.ds(lo, tf)], sems.at[sl, 0]),
                    pltpu.make_async_copy(dv_buf.at[sl], du_ref.at[:, pl.ds(Fh + lo, tf)], sems.at[sl, 1]))

        @pl.when(j >= 2)
        def _():
            for cp in writes(j - 2, slot):
                cp.wait()

        cw0, cw1, cw2 = cw_ref[0:1, :], cw_ref[1:2, :], cw_ref[2:3, :]
        cb = cb_ref[...]
        acc = [jnp.zeros((1, tf), F32) for _ in range(4)]
        n = ch + GLU_HALO
        for r0 in range(0, S_, ch):
            ext = _rows(a_ref, r0 - GLU_HALO, r0 + n, S_)
            a0 = ext[GLU_HALO:]
            a1 = pltpu.roll(ext, 1, 0)[GLU_HALO:]
            a2 = pltpu.roll(ext, 2, 0)[GLU_HALO:]
            ac = a2 * cw0 + a1 * cw1 + a0 * cw2 + cb
            vv = _rows(v_ref, r0, r0 + n, S_)
            dzv = _rows(dz_ref, r0, r0 + n, S_)
            gl = _gelu(ac)
            dac = dzv * vv * _gelu_grad(ac)
            da = (dac * cw2 + pltpu.roll(dac, n - 1, 0) * cw1 + pltpu.roll(dac, n - 2, 0) * cw0)[:ch]
            da_buf[slot, r0:r0 + ch, :] = da.astype(da_buf.dtype)
            dv_buf[slot, r0:r0 + ch, :] = (dzv[:ch] * gl[:ch]).astype(dv_buf.dtype)
            dc = dac[:ch]
            acc[0] = acc[0] + jnp.sum(dc * a2[:ch], axis=0, keepdims=True)
            acc[1] = acc[1] + jnp.sum(dc * a1[:ch], axis=0, keepdims=True)
            acc[2] = acc[2] + jnp.sum(dc * a0[:ch], axis=0, keepdims=True)
            acc[3] = acc[3] + jnp.sum(dc, axis=0, keepdims=True)
        dcw_ref[0:1, :] = acc[0]
        dcw_ref[1:2, :] = acc[1]
        dcw_ref[2:3, :] = acc[2]
        dcb_ref[...] = acc[3]
        for cp in writes(j, slot):
            cp.start()

        @pl.when(j == nt - 1)
        def _():
            for cp in writes(j, slot):
                cp.wait()
            if nt > 1:
                for cp in writes(j - 1, 1 - slot):
                    cp.wait()

    return pl.pallas_call(
        body, grid=(nt,),
        in_specs=[pl.BlockSpec((S_, tf), lambda j: (0, j)), pl.BlockSpec((S_, tf), lambda j: (0, j + nt)),
                  pl.BlockSpec((S_, tf), lambda j: (0, j)),
                  pl.BlockSpec((3, tf), lambda j: (0, j)), pl.BlockSpec((1, tf), lambda j: (0, j))],
        out_specs=[_ANY, pl.BlockSpec((3, tf), lambda j: (0, j)), pl.BlockSpec((1, tf), lambda j: (0, j))],
        out_shape=[jax.ShapeDtypeStruct((S_, F2), BF16), jax.ShapeDtypeStruct((3, Fh), F32),
                   jax.ShapeDtypeStruct((1, Fh), F32)],
        scratch_shapes=[pltpu.VMEM((2, S_, tf), BF16), pltpu.VMEM((2, S_, tf), BF16), pltpu.SemaphoreType.DMA((2, 2))],
        compiler_params=_cparams("arbitrary"), name=name,
    )(u, u, dz, conv_w, _row(conv_b))


def rope_tables(pos, inv, name, tr=512):
    S_ = pos.shape[0]
    tr = _pick(S_, tr, 8)

    def body(p_ref, inv_ref, c_ref, s1_ref, s2_ref):
        ang = p_ref[...] * inv_ref[...]
        lane = lax.broadcasted_iota(jnp.int32, ang.shape, 1)
        half = QK_ROPE // 2
        cosv, sinv = jnp.cos(ang), jnp.sin(ang)
        c_ref[...] = jnp.where(lane < QK_ROPE, cosv, 0.0)
        s1_ref[...] = jnp.where(lane < half, -sinv, 0.0)
        s2_ref[...] = jnp.where((lane >= half) & (lane < QK_ROPE), sinv, 0.0)

    blk = pl.BlockSpec((tr, LANES), lambda i: (i, 0))
    shp = jax.ShapeDtypeStruct((S_, LANES), F32)
    return pl.pallas_call(
        body, grid=(S_ // tr,),
        in_specs=[pl.BlockSpec((tr, 1), lambda i: (i, 0)), pl.BlockSpec((1, LANES), lambda i: (0, 0))],
        out_specs=[blk, blk, blk], out_shape=[shp, shp, shp],
        compiler_params=_cparams("parallel"), name=name,
    )(pos, inv)


_HALF = QK_ROPE // 2


def _rope(t, c, s1, s2):
    return t * c + pltpu.roll(t, LANES - _HALF, 1) * s1 + pltpu.roll(t, _HALF, 1) * s2


def _rope_t(d, c, s1, s2):
    return d * c + pltpu.roll(d * s1, _HALF, 1) + pltpu.roll(d * s2, LANES - _HALF, 1)


def q_prep(q, tabs, scale, backward, name, tr=512):
    S_, W = q.shape
    tr = _pick(S_, tr, 8)

    def body(q_ref, c_ref, s1_ref, s2_ref, o_ref):
        o_ref[:, 0:LANES] = (q_ref[:, 0:LANES].astype(F32) * scale).astype(o_ref.dtype)
        t = q_ref[:, LANES:2 * LANES].astype(F32)
        fn = _rope_t if backward else _rope
        o_ref[:, LANES:2 * LANES] = (fn(t, c_ref[...], s1_ref[...], s2_ref[...]) * scale).astype(o_ref.dtype)

    blk = pl.BlockSpec((tr, HEAD_PAD), lambda i, h: (i, h))
    tab = pl.BlockSpec((tr, LANES), lambda i, h: (i, 0))
    return pl.pallas_call(
        body, grid=(S_ // tr, W // HEAD_PAD), in_specs=[blk, tab, tab, tab], out_specs=blk,
        out_shape=jax.ShapeDtypeStruct((S_, W), BF16),
        compiler_params=_cparams("parallel", "parallel"), name=name,
    )(q, *tabs)


def k_prep(knv, kv_ext, tabs, name, tr=512):
    S_ = knv.shape[0]
    tr = _pick(S_, tr, 8)

    def body(kn_ref, v_ref, t_ref, c_ref, s1_ref, s2_ref, o_ref, vx_ref):
        o_ref[:, 0:LANES] = kn_ref[...].astype(o_ref.dtype)
        o_ref[:, LANES:2 * LANES] = _rope(t_ref[...], c_ref[...], s1_ref[...], s2_ref[...]).astype(o_ref.dtype)
        vx_ref[:, 0:V_HEAD] = v_ref[...].astype(vx_ref.dtype)
        vx_ref[:, V_HEAD:HEAD_PAD] = jnp.ones((tr, HEAD_PAD - V_HEAD), vx_ref.dtype)

    tab = pl.BlockSpec((tr, LANES), lambda i, h: (i, 0))
    head = pl.BlockSpec((tr, HEAD_PAD), lambda i, h: (i, h))
    shp = jax.ShapeDtypeStruct((S_, N_HEADS * HEAD_PAD), BF16)
    return pl.pallas_call(
        body, grid=(S_ // tr, N_HEADS),
        in_specs=[pl.BlockSpec((tr, LANES), lambda i, h: (i, h)),
                  pl.BlockSpec((tr, V_HEAD), lambda i, h: (i, N_HEADS + h)),
                  pl.BlockSpec((tr, LANES), lambda i, h: (i, KV_RANK // LANES)), tab, tab, tab],
        out_specs=[head, head], out_shape=[shp, shp],
        compiler_params=_cparams("parallel", "parallel"), name=name,
    )(knv, knv, kv_ext, *tabs)


def k_prep_bwd(dk_a, dk_b, dv_a, dv_b, tabs, name, tr=256):
    S_ = dk_a.shape[0]
    tr = _pick(S_, tr, 8)
    HV = N_HEADS * V_HEAD

    def body(ka_ref, kb_ref, va_ref, vb_ref, c_ref, s1_ref, s2_ref, o_ref, t_ref):
        dr = jnp.zeros((tr, LANES), F32)
        for h in range(N_HEADS):
            lo = h * HEAD_PAD
            o_ref[:, h * LANES:(h + 1) * LANES] = (ka_ref[:, lo:lo + LANES] + kb_ref[:, lo:lo + LANES]).astype(o_ref.dtype)
            dr = dr + ka_ref[:, lo + LANES:lo + 2 * LANES] + kb_ref[:, lo + LANES:lo + 2 * LANES]
        o_ref[:, HV:2 * HV] = (va_ref[...] + vb_ref[...]).astype(o_ref.dtype)
        t_ref[...] = _rope_t(dr, c_ref[...], s1_ref[...], s2_ref[...])

    kblk = pl.BlockSpec((tr, N_HEADS * HEAD_PAD), lambda i: (i, 0))
    vblk = pl.BlockSpec((tr, HV), lambda i: (i, 0))
    tab = pl.BlockSpec((tr, LANES), lambda i: (i, 0))
    return pl.pallas_call(
        body, grid=(S_ // tr,), in_specs=[kblk, kblk, vblk, vblk, tab, tab, tab],
        out_specs=[pl.BlockSpec((tr, 2 * HV), lambda i: (i, 0)), tab],
        out_shape=[jax.ShapeDtypeStruct((S_, 2 * HV), BF16), jax.ShapeDtypeStruct((S_, LANES), F32)],
        compiler_params=_cparams("parallel"), name=name,
    )(dk_a, dk_b, dv_a, dv_b, *tabs)


_NEG = -1e30


def attn_fwd(q, k, vx, name):
    S_ = q.shape[0]
    TQ = _pick(S_, ATT_Q_BLOCK, 8)
    TK = _pick(S_, ATT_K_BLOCK, TQ)
    HP = ATT_HEADS_PER_STEP
    W = HP * HEAD_PAD
    ratio = TK // TQ

    def body(q_ref, k_ref, v_ref, o_ref, lse_ref):
        i = pl.program_id(1)
        qs = [q_ref[:, h * HEAD_PAD:(h + 1) * HEAD_PAD] for h in range(HP)]

        def step(j, carry, masked):
            start = pl.multiple_of(j * TK, TK)
            out = []
            for h in range(HP):
                m, acc = carry[h]
                cols = slice(h * HEAD_PAD, (h + 1) * HEAD_PAD)
                s = lax.dot_general(qs[h], k_ref[pl.ds(start, TK), cols], _DIMS["nt"], preferred_element_type=F32)
                if masked:
                    rowi = i * TQ + lax.broadcasted_iota(jnp.int32, (TQ, TK), 0)
                    coli = j * TK + lax.broadcasted_iota(jnp.int32, (TQ, TK), 1)
                    s = jnp.where(coli <= rowi, s, _NEG)
                m_new = jnp.maximum(m, jnp.max(s, axis=-1, keepdims=True))
                alpha = jnp.exp(m - m_new)
                p = jnp.exp(s - m_new).astype(BF16)
                acc = alpha * acc + lax.dot_general(p, v_ref[pl.ds(start, TK), cols], _DIMS["nn"],
                                                    preferred_element_type=F32)
                out.append((m_new, acc))
            return tuple(out)

        init = tuple((jnp.full((TQ, 1), _NEG, F32), jnp.zeros((TQ, HEAD_PAD), F32)) for _ in range(HP))
        last = i // ratio
        carry = step(last, lax.fori_loop(0, last, functools.partial(step, masked=False), init), True)
        for h in range(HP):
            m, acc = carry[h]
            l = acc[:, V_HEAD:]
            o_ref[:, h * V_HEAD:(h + 1) * V_HEAD] = (acc[:, :V_HEAD] / l).astype(o_ref.dtype)
            lse_ref[h] = m + jnp.log(jnp.max(l, axis=-1, keepdims=True))

    return pl.pallas_call(
        body, grid=(N_HEADS // HP, S_ // TQ),
        in_specs=[pl.BlockSpec((TQ, W), lambda g, i: (i, g)),
                  pl.BlockSpec((S_, W), lambda g, i: (0, g)),
                  pl.BlockSpec((S_, W), lambda g, i: (0, g))],
        out_specs=[pl.BlockSpec((TQ, HP * V_HEAD), lambda g, i: (i, g)),
                   pl.BlockSpec((HP, TQ, 1), lambda g, i: (g, i, 0))],
        out_shape=[jax.ShapeDtypeStruct((S_, N_HEADS * V_HEAD), BF16), jax.ShapeDtypeStruct((N_HEADS, S_, 1), F32)],
        compiler_params=_cparams("parallel", "parallel"), name=name,
    )(q, k, vx)


def attn_delta(o, do, name, tr=512):
    S_ = o.shape[0]
    tr = _pick(S_, tr, 8)

    def body(o_ref, do_ref, d_ref):
        d_ref[...] = jnp.sum(o_ref[...].astype(F32) * do_ref[...].astype(F32), axis=-1, keepdims=True)

    blk = pl.BlockSpec((tr, V_HEAD), lambda i, h: (i, h))
    return pl.pallas_call(
        body, grid=(S_ // tr, N_HEADS), in_specs=[blk, blk],
        out_specs=pl.BlockSpec((None, tr, 1), lambda i, h: (h, i, 0)),
        out_shape=jax.ShapeDtypeStruct((N_HEADS, S_, 1), F32),
        compiler_params=_cparams("parallel", "parallel"), name=name,
    )(o, do)


def attn_bwd(q, k, vx, do, lse_row, delta_row, name):
    S_ = q.shape[0]
    TK = _pick(S_, ATT_BWD_K_BLOCK, LANES)
    TQ = _pick(S_, ATT_BWD_Q_BLOCK, TK)
    HP = ATT_HEADS_PER_STEP
    W = HP * HEAD_PAD
    ratio = TQ // TK
    nq = S_ // TQ

    def body(q_ref, do_ref, lse_ref, dl_ref, k_ref, v_ref, dq_ref, dk_ref, dv_ref):
        j = pl.program_id(1)

        @pl.when(j == 0)
        def _():
            dq_ref[...] = jnp.zeros_like(dq_ref)

        ks = [k_ref[:, h * HEAD_PAD:(h + 1) * HEAD_PAD] for h in range(HP)]
        vs = [v_ref[:, h * HEAD_PAD:h * HEAD_PAD + V_HEAD] for h in range(HP)]

        def step(i, carry, masked):
            start = pl.multiple_of(i * TQ, TQ)
            out = []
            for h in range(HP):
                dk, dv = carry[h]
                cols = slice(h * HEAD_PAD, (h + 1) * HEAD_PAD)
                qv = q_ref[pl.ds(start, TQ), cols]
                dov = do_ref[pl.ds(start, TQ), h * V_HEAD:(h + 1) * V_HEAD]
                st = lax.dot_general(ks[h], qv, _DIMS["nt"], preferred_element_type=F32)
                pt = jnp.exp(st - lse_ref[h, :, pl.ds(start, TQ)])
                if masked:
                    keyi = j * TK + lax.broadcasted_iota(jnp.int32, (TK, TQ), 0)
                    qryi = i * TQ + lax.broadcasted_iota(jnp.int32, (TK, TQ), 1)
                    pt = jnp.where(keyi <= qryi, pt, 0.0)
                dpt = lax.dot_general(vs[h], dov, _DIMS["nt"], preferred_element_type=F32)
                dst = (pt * (dpt - dl_ref[h, :, pl.ds(start, TQ)])).astype(BF16)
                dv = dv + lax.dot_general(pt.astype(BF16), dov, _DIMS["nn"], preferred_element_type=F32)
                dk = dk + lax.dot_general(dst, qv, _DIMS["nn"], preferred_element_type=F32)
                dq_ref[pl.ds(start, TQ), cols] += lax.dot_general(dst, ks[h], _DIMS["tn"], preferred_element_type=F32)
                out.append((dk, dv))
            return tuple(out)

        init = tuple((jnp.zeros((TK, HEAD_PAD), F32), jnp.zeros((TK, V_HEAD), F32)) for _ in range(HP))
        first = j // ratio
        carry = lax.fori_loop(first + 1, nq, functools.partial(step, masked=False), step(first, init, True))
        for h in range(HP):
            dk_ref[:, h * HEAD_PAD:(h + 1) * HEAD_PAD] = carry[h][0]
            dv_ref[:, h * V_HEAD:(h + 1) * V_HEAD] = carry[h][1]

    return pl.pallas_call(
        body, grid=(N_HEADS // HP, S_ // TK),
        in_specs=[pl.BlockSpec((S_, W), lambda g, j: (0, g)),
                  pl.BlockSpec((S_, HP * V_HEAD), lambda g, j: (0, g)),
                  pl.BlockSpec((HP, 1, S_), lambda g, j: (g, 0, 0)),
                  pl.BlockSpec((HP, 1, S_), lambda g, j: (g, 0, 0)),
                  pl.BlockSpec((TK, W), lambda g, j: (j, g)),
                  pl.BlockSpec((TK, W), lambda g, j: (j, g))],
        out_specs=[pl.BlockSpec((S_, W), lambda g, j: (0, g)),
                   pl.BlockSpec((TK, W), lambda g, j: (j, g)),
                   pl.BlockSpec((TK, HP * V_HEAD), lambda g, j: (j, g))],
        out_shape=[jax.ShapeDtypeStruct((S_, N_HEADS * HEAD_PAD), F32),
                   jax.ShapeDtypeStruct((S_, N_HEADS * HEAD_PAD), F32),
                   jax.ShapeDtypeStruct((S_, N_HEADS * V_HEAD), F32)],
        compiler_params=_cparams("parallel", "arbitrary"), name=name,
    )(q, do, lse_row, delta_row, k, vx)


def mods_fwd(c_all, mod_w, mod_b, name, tn=512):
    L, Dn, E = mod_w.shape
    R = c_all.shape[0]
    tn = _pick(E, tn, LANES)

    def body(c_ref, w_ref, b_ref, o_ref):
        cv = c_ref[...]
        sc = (cv / (1.0 + jnp.exp(-cv))).astype(BF16)
        o_ref[...] = lax.dot_general(sc, w_ref[...].astype(BF16), _DIMS["nn"], preferred_element_type=F32) + b_ref[...]

    return pl.pallas_call(
        body, grid=(L, E // tn),
        in_specs=[pl.BlockSpec((R, Dn), lambda l, j: (0, 0)), pl.BlockSpec((None, Dn, tn), lambda l, j: (l, 0, j)),
                  pl.BlockSpec((None, 1, tn), lambda l, j: (l, 0, j))],
        out_specs=pl.BlockSpec((None, R, tn), lambda l, j: (l, 0, j)),
        out_shape=jax.ShapeDtypeStruct((L, R, E), F32),
        compiler_params=_cparams("parallel", "parallel"), name=name,
    )(c_all, mod_w, mod_b.reshape(L, 1, E))


def _adam_math(w, g, m, v):
    m = ADAM_B1 * m + (1.0 - ADAM_B1) * g
    v = ADAM_B2 * v + (1.0 - ADAM_B2) * (g * g)
    m_hat = m / (1.0 - ADAM_B1 ** ADAM_STEP)
    v_hat = v / (1.0 - ADAM_B2 ** ADAM_STEP)
    delta = -ADAM_LR * (m_hat / (jnp.sqrt(v_hat) + ADAM_EPS) + ADAM_WD * w)
    return delta, m, v


def _as2d(a):
    return a.reshape(-1, a.shape[-1]) if a.ndim != 2 else a


def adamw(w, g, m, v, name):
    shape = w.shape
    w2, g2, m2, v2 = _as2d(w), _as2d(g), _as2d(m), _as2d(v)
    R, C = w2.shape
    tr = _pick(R, max(8, (1 << 18) // C // 8 * 8), 8)

    def body(w_ref, g_ref, m_ref, v_ref, d_ref, mo_ref, vo_ref):
        d, mn, vn = _adam_math(w_ref[...], g_ref[...], m_ref[...], v_ref[...])
        d_ref[...] = d
        mo_ref[...] = mn
        vo_ref[...] = vn

    blk = pl.BlockSpec((tr, C), lambda i: (i, 0))
    shp = jax.ShapeDtypeStruct((R, C), F32)
    outs = pl.pallas_call(
        body, grid=(R // tr,), in_specs=[blk] * 4, out_specs=[blk] * 3, out_shape=[shp] * 3,
        compiler_params=_cparams("parallel"), name=name,
    )(w2, g2, m2, v2)
    return tuple(o.reshape(shape) for o in outs)


def adamw_sum(parts, w, m, v, name):
    P, R, C = parts.shape

    def body(p_ref, w_ref, m_ref, v_ref, g_ref, d_ref, mo_ref, vo_ref):
        g = p_ref[0]
        for k in range(1, P):
            g = g + p_ref[k]
        d, mn, vn = _adam_math(w_ref[...], g, m_ref[...], v_ref[...])
        g_ref[...] = g
        d_ref[...] = d
        mo_ref[...] = mn
        vo_ref[...] = vn

    shp = jax.ShapeDtypeStruct((R, C), F32)
    return pl.pallas_call(body, out_shape=[shp] * 4, compiler_params=_cparams(), name=name)(parts, w, m, v)


def adamw_modw(c_col, dm, w, m, v, name, tr=256, tn=512):
    L, Dn, E = w.shape
    B = c_col.shape[0]
    tr = _pick(Dn, tr, 8)
    tn = _pick(E, tn, LANES)

    def body(c_ref, dm_ref, w_ref, m_ref, v_ref, g_ref, d_ref, mo_ref, vo_ref):
        g = jnp.zeros((tr, tn), F32)
        for b in range(B):
            cv = c_ref[b]
            g = g + (cv / (1.0 + jnp.exp(-cv))) * dm_ref[b:b + 1, :]
        d, mn, vn = _adam_math(w_ref[...], g, m_ref[...], v_ref[...])
        g_ref[...] = g
        d_ref[...] = d
        mo_ref[...] = mn
        vo_ref[...] = vn

    blk = pl.BlockSpec((None, tr, tn), lambda l, i, j: (l, i, j))
    shp = jax.ShapeDtypeStruct((L, Dn, E), F32)
    return pl.pallas_call(
        body, grid=(L, Dn // tr, E // tn),
        in_specs=[pl.BlockSpec((B, tr, 1), lambda l, i, j: (0, i, 0)),
                  pl.BlockSpec((None, B, tn), lambda l, i, j: (l, 0, j)), blk, blk, blk],
        out_specs=[blk] * 4, out_shape=[shp] * 4,
        compiler_params=_cparams("parallel", "parallel", "parallel"), name=name,
    )(c_col, dm, w, m, v)


def add_round(a, b, name, tr=512):
    R, C = a.shape
    tr = _pick(R, tr, 16)

    def body(a_ref, b_ref, o_ref):
        o_ref[...] = (a_ref[...] + b_ref[...].astype(F32)).astype(BF16)

    blk = pl.BlockSpec((tr, C), lambda i: (i, 0))
    return pl.pallas_call(
        body, grid=(R // tr,), in_specs=[blk, blk], out_specs=blk, out_shape=jax.ShapeDtypeStruct((R, C), BF16),
        compiler_params=_cparams("parallel"), name=name,
    )(a, b)


def sum_parts(parts, name, tr=512):
    P, R, C = parts.shape
    tr = _pick(R, tr, 16)

    def body(p_ref, o_ref):
        s = p_ref[0].astype(F32)
        for k in range(1, P):
            s = s + p_ref[k].astype(F32)
        o_ref[...] = s

    return pl.pallas_call(
        body, grid=(R // tr,), in_specs=[pl.BlockSpec((P, tr, C), lambda i: (0, i, 0))],
        out_specs=pl.BlockSpec((tr, C), lambda i: (i, 0)), out_shape=jax.ShapeDtypeStruct((R, C), F32),
        compiler_params=_cparams("parallel"), name=name,
    )(parts)


_ANY = pl.BlockSpec(memory_space=pl.ANY)


def _place():
    return lax.axis_index("x"), lax.axis_index("y"), lax.axis_index("c")


def _flip(v, bit):
    return 1 - v if bit else v


def chip_gather(buf, name):
    def body(in_ref, out_ref, send_sems, recv_sems):
        x, y, c = _place()
        me = 2 * x + y
        sends = []
        for k in range(1, N_CHIPS):
            px, py = _flip(x, k >> 1), _flip(y, k & 1)
            cp = pltpu.make_async_remote_copy(src_ref=in_ref, dst_ref=out_ref.at[me], send_sem=send_sems.at[k - 1],
                                              recv_sem=recv_sems.at[k - 1], device_id=(px, py, c), device_id_type=MESH)
            cp.start()
            sends.append(cp)
        for k in range(1, N_CHIPS):
            px, py = _flip(x, k >> 1), _flip(y, k & 1)
            pltpu.make_async_remote_copy(src_ref=in_ref, dst_ref=out_ref.at[2 * px + py], send_sem=send_sems.at[k - 1],
                                         recv_sem=recv_sems.at[k - 1], device_id=(px, py, c),
                                         device_id_type=MESH).wait_recv()
        for cp in sends:
            cp.wait_send()

    out = pl.pallas_call(
        body, in_specs=[_ANY], out_specs=_ANY,
        out_shape=jax.ShapeDtypeStruct((N_CHIPS,) + buf.shape, buf.dtype),
        scratch_shapes=[pltpu.SemaphoreType.DMA((N_CHIPS - 1,)), pltpu.SemaphoreType.DMA((N_CHIPS - 1,))],
        name=name,
    )(buf)
    return lax.dynamic_update_index_in_dim(out, buf, 2 * lax.axis_index("x") + lax.axis_index("y"), 0)


def chip_all_to_all(buf, name):
    def body(in_ref, out_ref, send_sems, recv_sems):
        x, y, c = _place()
        me = 2 * x + y
        sends = []
        for k in range(1, N_CHIPS):
            px, py = _flip(x, k >> 1), _flip(y, k & 1)
            cp = pltpu.make_async_remote_copy(src_ref=in_ref.at[2 * px + py], dst_ref=out_ref.at[me],
                                              send_sem=send_sems.at[k - 1], recv_sem=recv_sems.at[k - 1],
                                              device_id=(px, py, c), device_id_type=MESH)
            cp.start()
            sends.append(cp)
        for k in range(1, N_CHIPS):
            px, py = _flip(x, k >> 1), _flip(y, k & 1)
            pltpu.make_async_remote_copy(src_ref=in_ref.at[me], dst_ref=out_ref.at[2 * px + py],
                                         send_sem=send_sems.at[k - 1], recv_sem=recv_sems.at[k - 1],
                                         device_id=(px, py, c), device_id_type=MESH).wait_recv()
        for cp in sends:
            cp.wait_send()

    out = pl.pallas_call(
        body, in_specs=[_ANY], out_specs=_ANY, out_shape=jax.ShapeDtypeStruct(buf.shape, buf.dtype),
        scratch_shapes=[pltpu.SemaphoreType.DMA((N_CHIPS - 1,)), pltpu.SemaphoreType.DMA((N_CHIPS - 1,))],
        name=name,
    )(buf)
    me = 2 * lax.axis_index("x") + lax.axis_index("y")
    return lax.dynamic_update_index_in_dim(out, _index(buf, me), me, 0)


def core_gather(buf, name):
    def body(in_ref, out_ref, send_sem, recv_sem):
        x, y, c = _place()
        cp = pltpu.make_async_remote_copy(src_ref=in_ref, dst_ref=out_ref.at[c], send_sem=send_sem, recv_sem=recv_sem,
                                          device_id=(x, y, 1 - c), device_id_type=MESH)
        cp.start()
        pltpu.make_async_remote_copy(src_ref=in_ref, dst_ref=out_ref.at[1 - c], send_sem=send_sem, recv_sem=recv_sem,
                                     device_id=(x, y, 1 - c), device_id_type=MESH).wait_recv()
        cp.wait_send()

    out = pl.pallas_call(
        body, in_specs=[_ANY], out_specs=_ANY, out_shape=jax.ShapeDtypeStruct((2,) + buf.shape, buf.dtype),
        scratch_shapes=[pltpu.SemaphoreType.DMA, pltpu.SemaphoreType.DMA],
        name=name,
    )(buf)
    return lax.dynamic_update_index_in_dim(out, buf, lax.axis_index("c"), 0)


def core_swap(buf, name):
    def body(in_ref, out_ref, send_sem, recv_sem):
        x, y, c = _place()
        cp = pltpu.make_async_remote_copy(src_ref=in_ref, dst_ref=out_ref, send_sem=send_sem, recv_sem=recv_sem,
                                          device_id=(x, y, 1 - c), device_id_type=MESH)
        cp.start()
        cp.wait()

    return pl.pallas_call(
        body, in_specs=[_ANY], out_specs=_ANY, out_shape=jax.ShapeDtypeStruct(buf.shape, buf.dtype),
        scratch_shapes=[pltpu.SemaphoreType.DMA, pltpu.SemaphoreType.DMA],
        name=name,
    )(buf)


def device_gather(buf, name):
    def body(in_ref, out_ref, send_sems, recv_sems, local_sem):
        x, y, c = _place()
        me = 4 * x + 2 * y + c
        mine = pltpu.make_async_copy(in_ref, out_ref.at[me], local_sem)
        mine.start()
        sends = []
        for k in range(1, N_DEV):
            peer = (_flip(x, (k >> 2) & 1), _flip(y, (k >> 1) & 1), _flip(c, k & 1))
            cp = pltpu.make_async_remote_copy(src_ref=in_ref, dst_ref=out_ref.at[me], send_sem=send_sems.at[k - 1],
                                              recv_sem=recv_sems.at[k - 1], device_id=peer, device_id_type=MESH)
            cp.start()
            sends.append(cp)
        for k in range(1, N_DEV):
            peer = (_flip(x, (k >> 2) & 1), _flip(y, (k >> 1) & 1), _flip(c, k & 1))
            pltpu.make_async_remote_copy(src_ref=in_ref, dst_ref=out_ref.at[4 * peer[0] + 2 * peer[1] + peer[2]],
                                         send_sem=send_sems.at[k - 1], recv_sem=recv_sems.at[k - 1], device_id=peer,
                                         device_id_type=MESH).wait_recv()
        for cp in sends:
            cp.wait_send()
        mine.wait()

    return pl.pallas_call(
        body, in_specs=[_ANY], out_specs=_ANY, out_shape=jax.ShapeDtypeStruct((N_DEV,) + buf.shape, buf.dtype),
        scratch_shapes=[pltpu.SemaphoreType.DMA((N_DEV - 1,)), pltpu.SemaphoreType.DMA((N_DEV - 1,)),
                        pltpu.SemaphoreType.DMA],
        name=name,
    )(buf)


def _region(ref, chip_axis=None, chip=None, chip_size=None, half_axis=None, half=None, half_size=None):
    idx = [slice(None)] * len(ref.shape)
    if chip is not None:
        idx[chip_axis] = pl.ds(chip * chip_size, chip_size)
    if half is not None:
        idx[half_axis] = pl.ds(half * half_size, half_size)
    return ref.at[tuple(idx)]


def gather_weights(shards, axes, name, after=()):
    n = len(shards)

    def full_shape(t):
        shp = list(shards[t].shape)
        shp[axes[t][0]] *= N_CHIPS
        return tuple(shp)

    def body(*refs):
        ins, outs = refs[:n], refs[n + len(after):2 * n + len(after)]
        ici_send, ici_recv, d2d_send, d2d_recv, own_send, own_recv = refs[2 * n + len(after):]
        x, y, c = _place()
        me = 2 * x + y

        def part(t, ref, chip, half):
            ca, ha = axes[t]
            return _region(ref, ca, chip, ins[t].shape[ca], ha, half, ins[t].shape[ha] // 2)

        def own(t):
            return pltpu.make_async_remote_copy(src_ref=ins[t], dst_ref=part(t, outs[t], me, None),
                                                send_sem=own_send.at[t], recv_sem=own_recv.at[t],
                                                device_id=(x, y, 1 - c), device_id_type=MESH)

        started = []
        for t in range(n):
            own(t).start()
            started.append(own(t))
        for t in range(n):
            for k in range(1, N_CHIPS):
                px, py = _flip(x, k >> 1), _flip(y, k & 1)
                cp = pltpu.make_async_remote_copy(src_ref=part(t, ins[t], None, c), dst_ref=part(t, outs[t], me, c),
                                                  send_sem=ici_send.at[t, k - 1], recv_sem=ici_recv.at[t, k - 1],
                                                  device_id=(px, py, c), device_id_type=MESH)
                cp.start()
                started.append(cp)
        for t in range(n):
            for k in range(1, N_CHIPS):
                px, py = _flip(x, k >> 1), _flip(y, k & 1)
                got = part(t, outs[t], 2 * px + py, c)
                pltpu.make_async_remote_copy(src_ref=part(t, ins[t], None, c), dst_ref=got,
                                             send_sem=ici_send.at[t, k - 1], recv_sem=ici_recv.at[t, k - 1],
                                             device_id=(px, py, c), device_id_type=MESH).wait_recv()
                fw = pltpu.make_async_remote_copy(src_ref=got, dst_ref=got, send_sem=d2d_send.at[t, k - 1],
                                                  recv_sem=d2d_recv.at[t, k - 1], device_id=(x, y, 1 - c),
                                                  device_id_type=MESH)
                fw.start()
                started.append(fw)
        for t in range(n):
            for k in range(1, N_CHIPS):
                px, py = _flip(x, k >> 1), _flip(y, k & 1)
                theirs = part(t, outs[t], 2 * px + py, 1 - c)
                pltpu.make_async_remote_copy(src_ref=theirs, dst_ref=theirs, send_sem=d2d_send.at[t, k - 1],
                                             recv_sem=d2d_recv.at[t, k - 1], device_id=(x, y, 1 - c),
                                             device_id_type=MESH).wait_recv()
        for t in range(n):
            own(t).wait_recv()
        for cp in started:
            cp.wait_send()

    sem = pltpu.SemaphoreType.DMA((n, N_CHIPS - 1))
    own_sem = pltpu.SemaphoreType.DMA((n,))
    return pl.pallas_call(
        body, in_specs=[_ANY] * (n + len(after)), out_specs=[_ANY] * n,
        out_shape=[jax.ShapeDtypeStruct(full_shape(t), shards[t].dtype) for t in range(n)],
        scratch_shapes=[sem, sem, sem, sem, own_sem, own_sem], name=name,
    )(*shards, *after)


_HBM = pl.BlockSpec(memory_space=pltpu.HBM)
_SEM = pl.BlockSpec(memory_space=pltpu.SEMAPHORE)
_EFFECT = pltpu.SideEffectType.DATAFLOW_SIDE_EFFECTING
WEIGHT_COPIES = N_CHIPS


def _weight_peer(k, x, y, c):
    return (x, y, 1 - c) if k == 0 else (_flip(x, k >> 1), _flip(y, k & 1), c)


def weights_start(shards, items, name, after=()):
    n_sh, n_it = len(shards), len(items)

    def src_of(refs, i):
        t, layer, _ = items[i]
        return refs[t] if layer is None else refs[t].at[layer]

    def land_shape(i):
        t, layer, ca = items[i]
        shp = list(shards[t].shape if layer is None else shards[t].shape[1:])
        shp[ca] *= N_CHIPS
        return tuple(shp)

    def body(*refs):
        shard_refs, land_refs = refs[:n_sh], refs[n_sh:n_sh + n_it]
        first_out = n_sh + n_it + len(after)
        send_sems = refs[first_out:first_out + n_it]
        recv_sems = refs[first_out + n_it:first_out + 2 * n_it]
        token = refs[-1]
        x, y, c = _place()
        me = 2 * x + y
        for i in range(n_it):
            src = src_of(shard_refs, i)
            ca = items[i][2]
            dst = _region(land_refs[i], ca, me, src.shape[ca])
            for k in range(WEIGHT_COPIES):
                pltpu.make_async_remote_copy(src_ref=src, dst_ref=dst, send_sem=send_sems[i], recv_sem=recv_sems[i],
                                             device_id=_weight_peer(k, x, y, c), device_id_type=MESH).start()
        token[...] = jnp.zeros_like(token)

    lands = [pltpu.with_memory_space_constraint(lax.empty(land_shape(i), shards[0].dtype), pltpu.HBM)
             for i in range(n_it)]
    ins = [pltpu.with_memory_space_constraint(a, pltpu.HBM) for a in shards] + lands
    sems = (pltpu.SemaphoreType.DMA(()),) * (2 * n_it)
    outs = pl.pallas_call(
        body, name=name,
        out_shape=sems + tuple(pltpu.HBM(a.shape, a.dtype) for a in ins) + (jax.ShapeDtypeStruct((8, LANES), F32),),
        in_specs=[_HBM] * len(ins) + [_ANY] * len(after),
        out_specs=(_SEM,) * (2 * n_it) + (_HBM,) * len(ins) + (pl.BlockSpec(memory_space=pltpu.VMEM),),
        input_output_aliases={i: 2 * n_it + i for i in range(len(ins))},
        compiler_params=pltpu.CompilerParams(has_side_effects=_EFFECT),
    )(*ins, *after)
    base = 2 * n_it
    return (list(outs[:n_it]), list(outs[n_it:base]), list(outs[base:base + n_sh]),
            list(outs[base + n_sh:base + n_sh + n_it]), outs[-1])


def weights_wait(send_sems, recv_sems, lands, after, keep, name):
    m = len(lands)

    def body(*refs):
        land_refs, send_refs, recv_refs = refs[:m], refs[m:2 * m], refs[2 * m:3 * m]
        x, y, c = _place()
        for j in range(m):
            cp = pltpu.make_async_remote_copy(src_ref=land_refs[j], dst_ref=land_refs[j], send_sem=send_refs[j],
                                              recv_sem=recv_refs[j], device_id=(x, y, 1 - c),
                                              device_id_type=MESH)
            cp.wait_send()
            cp.wait_recv()

    outs = pl.pallas_call(
        body, name=name,
        out_shape=tuple(pltpu.HBM(a.shape, a.dtype) for a in lands),
        in_specs=[_HBM] * m + [_SEM] * (2 * m) + [_ANY] + [_HBM] * len(keep),
        out_specs=(_HBM,) * m,
        input_output_aliases={j: j for j in range(m)},
        compiler_params=pltpu.CompilerParams(has_side_effects=_EFFECT),
    )(*lands, *send_sems, *recv_sems, after, *keep)
    return list(outs)


def reduce_to_sibling(lo, hi, name):
    n = len(lo)

    def body(*refs):
        los, his, outs = refs[:n], refs[n:2 * n], refs[2 * n:3 * n]
        send_sems, recv_sems = refs[3 * n:]
        x, y, c = _place()

        def copy(u, src):
            return pltpu.make_async_remote_copy(src_ref=src, dst_ref=outs[u], send_sem=send_sems.at[u],
                                                recv_sem=recv_sems.at[u], device_id=(x, y, 1 - c), device_id_type=MESH)

        for u in range(n):
            @pl.when(c == 0)
            def _(u=u):
                copy(u, his[u]).start()

            @pl.when(c == 1)
            def _(u=u):
                copy(u, los[u]).start()
        for u in range(n):
            copy(u, los[u]).wait_recv()
        for u in range(n):
            copy(u, los[u]).wait_send()

    return pl.pallas_call(
        body, in_specs=[_ANY] * (2 * n), out_specs=[_ANY] * n,
        out_shape=[jax.ShapeDtypeStruct(a.shape, a.dtype) for a in lo],
        scratch_shapes=[pltpu.SemaphoreType.DMA((n,)), pltpu.SemaphoreType.DMA((n,))], name=name,
    )(*lo, *hi)


def add_selected(lo, hi, other, name, tile_elems=1 << 19):
    R, C = lo.shape
    tr = _pick(R, max(16, tile_elems // C // 16 * 16), 16)

    def body(lo_ref, hi_ref, o_ref, out_ref):
        mine = jnp.where(lax.axis_index("c") == 0, lo_ref[...].astype(F32), hi_ref[...].astype(F32))
        out_ref[...] = (mine + o_ref[...].astype(F32)).astype(out_ref.dtype)

    blk = pl.BlockSpec((tr, C), lambda i: (i, 0))
    return pl.pallas_call(
        body, grid=(R // tr,), in_specs=[blk, blk, blk], out_specs=blk, out_shape=jax.ShapeDtypeStruct((R, C), BF16),
        compiler_params=_cparams("parallel"), name=name,
    )(lo, hi, other)


def scatter_to_chips(pieces, chip_axes, name):
    n = len(pieces)

    def block_shape(u):
        shp = list(pieces[u].shape)
        shp[chip_axes[u]] //= N_CHIPS
        return tuple(shp)

    def body(*refs):
        ins, outs = refs[:n], refs[n:2 * n]
        send_sems, recv_sems = refs[2 * n:]
        x, y, c = _place()
        me = 2 * x + y
        started = []
        for u in range(n):
            size = block_shape(u)[chip_axes[u]]
            for k in range(1, N_CHIPS):
                px, py = _flip(x, k >> 1), _flip(y, k & 1)
                cp = pltpu.make_async_remote_copy(src_ref=_region(ins[u], chip_axes[u], 2 * px + py, size),
                                                  dst_ref=outs[u].at[me], send_sem=send_sems.at[u, k - 1],
                                                  recv_sem=recv_sems.at[u, k - 1], device_id=(px, py, c),
                                                  device_id_type=MESH)
                cp.start()
                started.append(cp)
        for u in range(n):
            size = block_shape(u)[chip_axes[u]]
            for k in range(1, N_CHIPS):
                px, py = _flip(x, k >> 1), _flip(y, k & 1)
                pltpu.make_async_remote_copy(src_ref=_region(ins[u], chip_axes[u], me, size),
                                             dst_ref=outs[u].at[2 * px + py], send_sem=send_sems.at[u, k - 1],
                                             recv_sem=recv_sems.at[u, k - 1], device_id=(px, py, c),
                                             device_id_type=MESH).wait_recv()
        for cp in started:
            cp.wait_send()

    sem = pltpu.SemaphoreType.DMA((n, N_CHIPS - 1))
    return pl.pallas_call(
        body, in_specs=[_ANY] * n, out_specs=[_ANY] * n,
        out_shape=[jax.ShapeDtypeStruct((N_CHIPS,) + block_shape(u), pieces[u].dtype) for u in range(n)],
        scratch_shapes=[sem, sem], name=name,
    )(*pieces)


def gather_halves(parts, slots, out_shapes, name):
    n = len(parts)

    def body(*refs):
        ins, outs = refs[:n], refs[n:n + len(out_shapes)]
        send_sems, recv_sems = refs[n + len(out_shapes):]
        x, y, c = _place()
        started = []
        for u in range(n):
            t, s = slots[u]
            cp = pltpu.make_async_remote_copy(src_ref=ins[u], dst_ref=outs[t].at[c, s], send_sem=send_sems.at[u],
                                              recv_sem=recv_sems.at[u], device_id=(x, y, 1 - c), device_id_type=MESH)
            cp.start()
            started.append(cp)
        for u in range(n):
            t, s = slots[u]
            pltpu.make_async_remote_copy(src_ref=ins[u], dst_ref=outs[t].at[1 - c, s], send_sem=send_sems.at[u],
                                         recv_sem=recv_sems.at[u], device_id=(x, y, 1 - c),
                                         device_id_type=MESH).wait_recv()
        for cp in started:
            cp.wait_send()

    return pl.pallas_call(
        body, in_specs=[_ANY] * n, out_specs=[_ANY] * len(out_shapes),
        out_shape=[jax.ShapeDtypeStruct(shp, F32) for shp in out_shapes],
        scratch_shapes=[pltpu.SemaphoreType.DMA((n,)), pltpu.SemaphoreType.DMA((n,))], name=name,
    )(*parts)


WEIGHT_ORDER = ["mod_w", "mod_b", "norm1_g", "norm2_g", "pool_w", "pool_b", "pool_scale", "kv_in_g", "w_dkv",
                "ckv_norm_g", "w_uk", "w_uv", "w_dq", "q_norm_g", "w_uq", "w_o", "w_up", "conv_w", "conv_b", "w_down",
                "final_g"]
EXCHANGED = {"w_up": (2, 0), "w_down": (1, 0), "w_o": (1, 0), "w_uq": (2, 0), "w_dq": (1, 0), "pool_w": (2, 0),
             "w_dkv": (0, 1), "w_uk": (1, 0), "w_uv": (1, 0)}
SMALL_SHARDED = {"conv_w": 2, "pool_b": 1, "pool_scale": 1}
REPLICATED = ["mod_b", "norm1_g", "norm2_g", "kv_in_g", "ckv_norm_g", "q_norm_g", "conv_b", "final_g"]


def _padded(n, align):
    return -(-n // align) * align


def _flat_pad(parts, total):
    flat = jnp.concatenate(parts, axis=-1)
    pad = total - flat.shape[-1]
    if pad:
        flat = jnp.concatenate([flat, jnp.zeros(flat.shape[:-1] + (pad,), flat.dtype)], axis=-1)
    return flat


def _split_shards(full, axis):
    shp = full.shape
    t = full.reshape(shp[:axis] + (N_CHIPS, shp[axis] // N_CHIPS) + shp[axis + 1:])
    return jnp.moveaxis(t, axis, 0).reshape(N_CHIPS, -1)


def _join_shards(rows, shard_shape, axis):
    t = jnp.moveaxis(rows.reshape((N_CHIPS,) + tuple(shard_shape)), 0, axis)
    return t.reshape(tuple(shard_shape[:axis]) + (N_CHIPS * shard_shape[axis],) + tuple(shard_shape[axis + 1:]))


def _index(a, i, axis=0):
    return lax.dynamic_index_in_dim(a, i, axis, keepdims=False)


def kernel(x, c, positions, mod_w, mod_b, norm1_g, norm2_g, pool_w, pool_b, pool_scale, kv_in_g, w_dkv, ckv_norm_g, w_uk, w_uv, w_dq, q_norm_g, w_uq, w_o, w_up, conv_w, conv_b, w_down, final_g, loss_target, m_mod_w, m_mod_b, m_norm1_g, m_norm2_g, m_pool_w, m_pool_b, m_pool_scale, m_kv_in_g, m_w_dkv, m_ckv_norm_g, m_w_uk, m_w_uv, m_w_dq, m_q_norm_g, m_w_uq, m_w_o, m_w_up, m_conv_w, m_conv_b, m_w_down, m_final_g, v_mod_w, v_mod_b, v_norm1_g, v_norm2_g, v_pool_w, v_pool_b, v_pool_scale, v_kv_in_g, v_w_dkv, v_ckv_norm_g, v_w_uk, v_w_uv, v_w_dq, v_q_norm_g, v_w_uq, v_w_o, v_w_up, v_conv_w, v_conv_b, v_w_down, v_final_g):
    given = dict(locals())
    W = {n: given[n] for n in WEIGHT_ORDER}
    M1 = {n: given["m_" + n] for n in WEIGHT_ORDER}
    V2 = {n: given["v_" + n] for n in WEIGHT_ORDER}
    xi, yi, ci = lax.axis_index("x"), lax.axis_index("y"), lax.axis_index("c")
    chip = 2 * xi + yi
    dev = 4 * xi + 2 * yi + ci
    x0 = x[0]
    S_, D = x0.shape
    Fh = conv_b.shape[1]
    E = mod_b.shape[1]
    Es = E // N_CHIPS
    zD = jnp.zeros((D,), F32)

    c_all = device_gather(c, "gather_c").reshape(N_DEV, D)
    c_pad = jnp.concatenate([c_all, jnp.zeros((16 - N_DEV, D), F32)], axis=0)
    mod_b_mine = lax.dynamic_slice_in_dim(mod_b, chip * Es, Es, axis=1)
    mods_part = mods_fwd(c_pad, mod_w, mod_b_mine, "mods_fwd")
    mods_all = chip_gather(mods_part, "gather_mods")
    mods = jnp.swapaxes(_index(mods_all, dev, axis=2), 0, 1).reshape(DEPTH, E)
    mod = [[mods[l, k * D:(k + 1) * D] for k in range(6)] for l in range(DEPTH)]

    full = {}
    ssz = {n: math.prod(W[n].shape) for n in SMALL_SHARDED}
    Tw = _padded(sum(ssz.values()), 8 * PACK_COLS)
    small_rows = chip_gather(_flat_pad([W[n].reshape(-1) for n in SMALL_SHARDED], Tw).reshape(-1, PACK_COLS),
                             "gather_small_w").reshape(N_CHIPS, Tw)
    off = 0
    for n, axis in SMALL_SHARDED.items():
        full[n] = _join_shards(small_rows[:, off:off + ssz[n]], W[n].shape, axis)
        off += ssz[n]

    names = list(EXCHANGED)
    shards = [W[n].astype(BF16) for n in names]
    n_mla = DEPTH - N_A
    first_axes = {"w_up": (1, 0), "w_down": (0, 1), "pool_w": (1, 0)}
    first = gather_weights([shards[names.index(n)][0] for n in first_axes], list(first_axes.values()), "gather_weights0",
                           after=[mods, small_rows])
    for n, arr in zip(first_axes, first):
        full[(n, 0)] = arr
    items, groups = [], []

    def group(entries):
        groups.append(list(range(len(items), len(items) + len(entries))))
        for n, layer in entries:
            ca = EXCHANGED[n][0] - (0 if layer is None else 1)
            items.append((names.index(n), layer, 0 if n == "w_dkv" else ca))

    for l in range(1, N_A):
        group([("w_up", l), ("w_down", l), ("pool_w", l)])
    for j in range(n_mla):
        head = [("w_dkv", None), ("w_uk", None), ("w_uv", None)] if j == 0 else []
        group(head + [("w_dq", j), ("w_uq", j), ("w_o", j), ("w_up", N_A + j), ("w_down", N_A + j)])
    w_send, w_recv, shards_thru, lands, _ = weights_start(shards, items, "weights_start", after=first)

    def weights_ready(g, after):
        keep = shards_thru if g == len(groups) - 1 else []
        got = weights_wait([w_send[i] for i in groups[g]], [w_recv[i] for i in groups[g]], [lands[i] for i in groups[g]],
                           after, keep, f"weights_wait{g}")
        for i, arr in zip(groups[g], got):
            t, layer, _ = items[i]
            full[(names[t], 0 if layer is None else layer)] = arr

    q_rank = W["w_uq"].shape[1]
    kv_w = KV_RANK + QK_ROPE

    def uq_ext(j):
        wq = full[("w_uq", j)].reshape(q_rank, N_HEADS, QK_HEAD)
        return jnp.concatenate([wq, jnp.zeros((q_rank, N_HEADS, HEAD_PAD - QK_HEAD), BF16)],
                               axis=2).reshape(q_rank, N_HEADS * HEAD_PAD)


    half = QK_ROPE // 2
    inv = 1.0 / (ROPE_THETA ** (jnp.arange(0, QK_ROPE, 2, dtype=F32) / QK_ROPE))
    inv_row = jnp.concatenate([inv, inv, jnp.zeros((LANES - 2 * half,), F32)]).reshape(1, LANES)
    tabs = rope_tables(positions[0].astype(F32).reshape(S_, 1), inv_row, "rope_tables")
    att_scale = QK_HEAD ** -0.5

    saved = []
    xcur = x0
    kv_saved = None
    K = VX = knv = None
    for l in range(DEPTH):
        sh1, sc1, g1, sh2, sc2, g2 = mod[l]
        st = {"xin": xcur}
        if l:
            weights_ready(l - 1, xcur)
        if l == N_A:
            w_dkv_ext = jnp.concatenate([full[("w_dkv", 0)], jnp.zeros((D, KV_RANK + LANES - kv_w), BF16)], axis=1)
            w_ukv = jnp.concatenate([full[("w_uk", 0)], full[("w_uv", 0)]], axis=1)
            xn = norm_fwd(xcur, kv_in_g, zD, zD, BF16, "kvin_fwd")
            kv_ext = mm(xn, w_dkv_ext, "nn", F32, "dkv_mm")
            lat = kv_ext[:, :KV_RANK]
            zk = jnp.zeros((KV_RANK,), F32)
            ckv = norm_fwd(lat, ckv_norm_g, zk, zk, BF16, "ckv_fwd")
            knv = mm(ckv, w_ukv, "nn", BF16, "ukv_mm")
            K, VX = k_prep(knv, kv_ext, tabs, "k_prep")
            kv_saved = {"x": xcur, "xn": xn, "lat": lat, "ckv": ckv}
        if l < N_A:
            h1 = norm_fwd(xcur, norm1_g[l], sc1, sh1, F32, f"norm1_fwd{l}")
            st["pooled"] = _pool_call(h1, BF16, f"pool_fwd{l}", False)
            st["cs"] = g1 * full["pool_scale"][l]
            st["ypre"], xmid = gmm(st["pooled"], full[("pool_w", l)], "nn", F32, f"pool_mm{l}", bias=full["pool_b"][l],
                                   res=xcur, colscale=st["cs"])
        else:
            j = l - N_A
            st["h1"] = norm_fwd(xcur, norm1_g[l], sc1, sh1, BF16, f"norm1_fwd{l}")
            st["ql"] = mm(st["h1"], full[("w_dq", j)], "nn", F32, f"dq_mm{l}")
            st["cq"] = norm_fwd(st["ql"], q_norm_g[j], jnp.zeros_like(q_norm_g[j]), jnp.zeros_like(q_norm_g[j]), BF16,
                                f"qnorm_fwd{l}")
            st["w_uq_ext"] = uq_ext(j)
            qe = mm(st["cq"], st["w_uq_ext"], "nn", F32, f"uq_mm{l}")
            st["Q"] = q_prep(qe, tabs, att_scale, False, f"q_prep{l}")
            st["o"], lse = attn_fwd(st["Q"], K, VX, f"attn_fwd{l}")
            st["lse"] = lse.reshape(N_HEADS, 1, S_)
            st["y"], xmid = mm(st["o"], full[("w_o", j)], "nn", F32, f"wo_mm{l}", res=xcur, colscale=g1)
        st["xmid"] = xmid
        st["h2"] = norm_fwd(xmid, norm2_g[l], sc2, sh2, BF16, f"norm2_fwd{l}")
        st["u"] = mm(st["h2"], full[("w_up", l)], "nn", BF16, f"up_mm{l}")
        st["z"] = glu_fwd(st["u"], full["conv_w"][l], conv_b[l], f"glu_fwd{l}")
        st["f"], xcur = mm(st["z"], full[("w_down", l)], "nn", F32, f"down_mm{l}", tk=1408, res=xmid, colscale=g2)
        saved.append(st)

    dx, d_final_g, loss_part = loss_head(xcur, final_g, loss_target[0], "loss_head")
    loss = lax.psum(loss_part[0, 0], ("x", "y", "c"))

    G = {}
    dmods = [None] * DEPTH
    d_norm1 = [None] * DEPTH
    d_norm2 = [None] * DEPTH
    d_conv_b = [None] * DEPTH
    d_qnorm = [None] * n_mla
    dkv_acc = []
    for l in reversed(range(DEPTH)):
        sh1, sc1, g1, sh2, sc2, g2 = mod[l]
        st = saved[l]
        df, a2, _ = gate_bwd(dx, st["f"], g2, f"gate2_bwd{l}")
        dz = mm(df, full[("w_down", l)], "nt", BF16, f"down_dx{l}")
        G[("w_down", l)] = mm(st["z"], df, "tn", BF16, f"down_dw{l}")
        du, dcw, dcb = glu_bwd(st["u"], dz, full["conv_w"][l], conv_b[l], f"glu_bwd{l}")
        G[("conv_w", l)] = dcw
        d_conv_b[l] = dcb[0]
        dh2 = mm(du, full[("w_up", l)], "nt", BF16, f"up_dx{l}", tk=1408)
        G[("w_up", l)] = mm(st["h2"], du, "tn", BF16, f"up_dw{l}")
        dxmid, s1, s2 = norm_bwd(st["xmid"], norm2_g[l], sc2, dh2, dx, f"norm2_bwd{l}")
        dsh2, dsc2, d_norm2[l] = s1[0], s2[0] * norm2_g[l], s2[0] * (1.0 + sc2)
        if l < N_A:
            dyp, a1, csum = gate_bwd(dxmid, st["ypre"], st["cs"], f"gate1_bwd{l}")
            dg1 = full["pool_scale"][l] * a1[0]
            G[("pool_scale", l)] = g1 * a1[0]
            G[("pool_b", l)] = st["cs"] * csum[0]
            dpooled = gmm(dyp, full[("pool_w", l)], "nt", F32, f"pool_dx{l}")
            G[("pool_w", l)] = gmm(st["pooled"], dyp, "tn", BF16, f"pool_dw{l}")
            dh1 = _pool_call(dpooled, F32, f"pool_bwd{l}", True)
        else:
            j = l - N_A
            dy, a1, _ = gate_bwd(dxmid, st["y"], g1, f"gate1_bwd{l}")
            dg1 = a1[0]
            do = mm(dy, full[("w_o", j)], "nt", BF16, f"wo_dx{l}")
            G[("w_o", j)] = mm(st["o"], dy, "tn", BF16, f"wo_dw{l}")
            delta = attn_delta(st["o"], do, f"attn_delta{l}").reshape(N_HEADS, 1, S_)
            dQ, dK, dV = attn_bwd(st["Q"], K, VX, do, st["lse"], delta, f"attn_bwd{l}")
            dkv_acc.append((dK, dV))
            dqe = q_prep(dQ, tabs, att_scale, True, f"q_prep_bwd{l}")
            dcq = mm(dqe, st["w_uq_ext"], "nt", F32, f"uq_dx{l}")
            G[("w_uq", j)] = mm(st["cq"], dqe, "tn", BF16, f"uq_dw{l}").reshape(q_rank, N_HEADS, HEAD_PAD)[
                :, :, :QK_HEAD].reshape(q_rank, N_HEADS * QK_HEAD)
            zq = jnp.zeros_like(q_norm_g[j])
            dql, _, s2q = norm_bwd(st["ql"], q_norm_g[j], zq, dcq, None, f"qnorm_bwd{l}")
            d_qnorm[j] = s2q[0]
            dh1 = mm(dql, full[("w_dq", j)], "nt", BF16, f"dq_dx{l}")
            G[("w_dq", j)] = mm(st["h1"], dql, "tn", BF16, f"dq_dw{l}")
        dx, s1, s2 = norm_bwd(st["xin"], norm1_g[l], sc1, dh1, dxmid, f"norm1_bwd{l}")
        dsh1, dsc1, d_norm1[l] = s1[0], s2[0] * norm1_g[l], s2[0] * (1.0 + sc1)
        dmods[l] = jnp.concatenate([dsh1, dsc1, dg1, dsh2, dsc2, a2[0]])
        if l == N_A:
            (dk_a, dv_a), (dk_b, dv_b) = dkv_acc
            dknv, d_tk = k_prep_bwd(dk_a, dk_b, dv_a, dv_b, tabs, "k_prep_bwd")
            dckv = mm(dknv, w_ukv, "nt", F32, "ukv_dx")
            d_ukv = mm(kv_saved["ckv"], dknv, "tn", BF16, "ukv_dw")
            G[("w_uk", 0)], G[("w_uv", 0)] = d_ukv[:, :N_HEADS * QK_NOPE], d_ukv[:, N_HEADS * QK_NOPE:]
            zk = jnp.zeros((KV_RANK,), F32)
            dlat, _, s2c = norm_bwd(kv_saved["lat"], ckv_norm_g, zk, dckv, None, "ckv_bwd")
            d_ckv_g = s2c[0]
            dkv_ext = jnp.concatenate([dlat, d_tk], axis=1)
            dxn = mm(dkv_ext, w_dkv_ext, "nt", BF16, "dkv_dx")
            G[("w_dkv", 0)] = mm(kv_saved["xn"], dkv_ext, "tn", BF16, "dkv_dw")[:, :kv_w]
            dx, _, s2k = norm_bwd(kv_saved["x"], kv_in_g, zD, dxn, dx, "kvin_bwd")
            d_kvin_g = s2k[0]

    units = []
    for n, (ca, ha) in EXCHANGED.items():
        if W[n].ndim > 2:
            half_layers = W[n].shape[0] // 2
            for sl in range(half_layers):
                units.append((n, sl, G[(n, sl)], G[(n, half_layers + sl)], ca - 1))
        elif n == "w_dkv":
            g4 = G[(n, 0)].reshape(N_CHIPS, 2, -1, kv_w)
            units.append((n, 0, g4[:, 0], g4[:, 1], 0))
        else:
            rows_half = W[n].shape[0] // 2
            units.append((n, 0, G[(n, 0)][:rows_half], G[(n, 0)][rows_half:], ca))
    lo = [u[2] for u in units]
    hi = [u[3] for u in units]
    theirs = reduce_to_sibling(lo, hi, "reduce_cores")

    def flat2(a):
        return a.reshape(-1, a.shape[-1])

    sums = [add_selected(flat2(l_), flat2(h_), flat2(t_), f"reduce_cores_add{i}").reshape(l_.shape)
            for i, (l_, h_, t_) in enumerate(zip(lo, hi, theirs))]
    axes = [u[4] for u in units]
    got = scatter_to_chips(sums, axes, "reduce_chips")
    reduced = []
    for i, (sm, ax, g4) in enumerate(zip(sums, axes, got)):
        size = sm.shape[ax] // N_CHIPS
        g4 = lax.dynamic_update_index_in_dim(g4, lax.dynamic_slice_in_dim(sm, chip * size, size, axis=ax), chip, 0)
        blk = g4.shape[1:]
        reduced.append(sum_parts(g4.reshape(N_CHIPS, -1, blk[-1]), f"reduce_chips_add{i}").reshape(blk))
    slots, out_shapes = [], []
    for n in EXCHANGED:
        mine = [i for i, u in enumerate(units) if u[0] == n]
        out_shapes.append((2, len(mine)) + reduced[mine[0]].shape)
        slots += [(len(out_shapes) - 1, units[i][1]) for i in mine]
    halves = gather_halves(reduced, slots, out_shapes, "reduce_gather")

    grads, deltas, new_m, new_v = {}, {}, {}, {}
    for ti, n in enumerate(EXCHANGED):
        g = halves[ti]
        for i, u in enumerate(units):
            if u[0] == n:
                g = lax.dynamic_update_slice(g, reduced[i][None, None], (ci, u[1]) + (0,) * reduced[i].ndim)
        grads[n] = g.reshape(W[n].shape)
        deltas[n], new_m[n], new_v[n] = adamw(W[n], grads[n], M1[n], V2[n], f"adamw_{n}")

    small = {"mod_b": jnp.stack(dmods), "norm1_g": jnp.stack(d_norm1), "norm2_g": jnp.stack(d_norm2),
             "kv_in_g": d_kvin_g, "ckv_norm_g": d_ckv_g, "q_norm_g": jnp.stack(d_qnorm),
             "conv_b": jnp.stack(d_conv_b), "final_g": d_final_g[0]}
    extra = {n: jnp.stack([G[(n, i)] for i in range(W[n].shape[0])]) for n in SMALL_SHARDED}
    ssizes = {n: math.prod(W[n].shape) for n in REPLICATED}
    esizes = {n: math.prod(extra[n].shape) for n in SMALL_SHARDED}
    Ts = _padded(sum(ssizes.values()) + sum(esizes.values()), 8 * PACK_COLS)

    def pack_small(d, tail=()):
        return _flat_pad([d[n].reshape(-1) for n in REPLICATED] + [t.reshape(-1) for t in tail],
                         Ts).reshape(Ts // PACK_COLS, PACK_COLS)

    parts = device_gather(pack_small(small, [extra[n] for n in SMALL_SHARDED]), "gather_small")
    outs = adamw_sum(parts, pack_small(W), pack_small(M1), pack_small(V2), "adamw_small")
    off = 0
    for n in REPLICATED:
        for dst, o in zip((grads, deltas, new_m, new_v), outs):
            dst[n] = o.reshape(-1)[off:off + ssizes[n]].reshape(W[n].shape)
        off += ssizes[n]
    for n, axis in SMALL_SHARDED.items():
        g_full = outs[0].reshape(-1)[off:off + esizes[n]].reshape(extra[n].shape)
        off += esizes[n]
        size = W[n].shape[axis]
        grads[n] = lax.dynamic_slice_in_dim(g_full, chip * size, size, axis=axis)
        deltas[n], new_m[n], new_v[n] = adamw(W[n], grads[n], M1[n], V2[n], f"adamw_{n}")

    dm_all = parts.reshape(N_DEV, -1)[:, :DEPTH * E].reshape(N_DEV, DEPTH, E)
    dm_mine = jnp.swapaxes(lax.dynamic_slice_in_dim(dm_all, chip * Es, Es, axis=2), 0, 1)
    grads["mod_w"], deltas["mod_w"], new_m["mod_w"], new_v["mod_w"] = adamw_modw(
        c_all.reshape(N_DEV, D, 1), dm_mine, mod_w, m_mod_w, v_mod_w, "adamw_mod_w")

    return (loss, dx.reshape(x.shape), *[grads[n] for n in WEIGHT_ORDER], *[deltas[n] for n in WEIGHT_ORDER],
            *[new_m[n] for n in WEIGHT_ORDER], *[new_v[n] for n in WEIGHT_ORDER])
```

```python
import functools
import math

import jax
import jax.numpy as jnp
from jax import lax
from jax.experimental import pallas as pl
from jax.experimental.pallas import tpu as pltpu

F32 = jnp.float32
BF16 = jnp.bfloat16
MESH = pl.DeviceIdType.MESH

DEPTH = 4
N_A = 2
POOL_WINDOWS = (2, 4, 8, 16)
N_GROUPS = 4
N_HEADS = 8
QK_NOPE = 128
QK_ROPE = 64
V_HEAD = 128
QK_HEAD = QK_NOPE + QK_ROPE
HEAD_PAD = 256
KV_RANK = 256
ROPE_THETA = 10000.0
EPS = 1e-6
ADAM_LR = 0.001
ADAM_B1 = 0.9
ADAM_B2 = 0.999
ADAM_EPS = 1e-08
ADAM_WD = 0.01
ADAM_STEP = 10

N_CHIPS = 4
N_DEV = 8
LANES = 128
PACK_COLS = 1024
VMEM_LIMIT = 56 * 1024 * 1024
GLU_TILE = 256
ATT_BWD_K_BLOCK = 256
ATT_BWD_Q_BLOCK = 512
ATT_Q_BLOCK = 256
ATT_K_BLOCK = 512
ATT_HEADS_PER_STEP = 2


def _cparams(*sem):
    return pltpu.CompilerParams(dimension_semantics=sem if sem else None, vmem_limit_bytes=VMEM_LIMIT)


def _pick(n, target, mult):
    best = None
    d = mult
    while d <= min(n, target):
        if n % d == 0:
            best = d
        d += mult
    return n if best is None else best


def _row(v):
    return v.reshape(1, -1).astype(F32)


_DIMS = {"nn": (((1,), (0,)), ((), ())), "nt": (((1,), (1,)), ((), ())), "tn": (((0,), (0,)), ((), ()))}


def _mm_body(mode, nk, has_bias, has_res):
    def body(*refs):
        a_ref, b_ref = refs[0], refs[1]
        pos = 2
        bias_ref = res_ref = cs_ref = None
        if has_bias:
            bias_ref = refs[pos]
            pos += 1
        if has_res:
            res_ref, cs_ref = refs[pos], refs[pos + 1]
            pos += 2
        o_ref = refs[pos]
        pos += 1
        o2_ref = None
        if has_res:
            o2_ref = refs[pos]
            pos += 1
        acc_ref = refs[pos] if nk > 1 else None
        k = pl.program_id(2)
        part = lax.dot_general(a_ref[...].astype(BF16), b_ref[...].astype(BF16), _DIMS[mode],
                               preferred_element_type=F32)

        def finish(y):
            if has_bias:
                y = y + bias_ref[...]
            o_ref[...] = y.astype(o_ref.dtype)
            if has_res:
                o2_ref[...] = res_ref[...] + cs_ref[...] * y

        if nk == 1:
            finish(part)
            return

        @pl.when(k == 0)
        def _():
            acc_ref[...] = part

        @pl.when((k > 0) & (k < nk - 1))
        def _():
            acc_ref[...] += part

        @pl.when(k == nk - 1)
        def _():
            finish(acc_ref[...] + part)

    return body


def mm(a, b, mode, out_dtype, name, *, tm=1408, tn=1408, tk=1024, bias=None, res=None, colscale=None, layer=None):
    bshape = b.shape if layer is None else b.shape[1:]
    if mode == "nn":
        (M, K), N = a.shape, bshape[1]
    elif mode == "nt":
        (M, K), N = a.shape, bshape[0]
    else:
        (K, M), N = a.shape, bshape[1]
    tm = _pick(M, tm, LANES if mode == "tn" else 8)
    tn = _pick(N, tn, LANES)
    tk = _pick(K, tk, LANES) if mode != "tn" else _pick(K, tk, 8)
    nk = K // tk
    a_spec = {"nn": pl.BlockSpec((tm, tk), lambda i, j, k: (i, k)),
              "nt": pl.BlockSpec((tm, tk), lambda i, j, k: (i, k)),
              "tn": pl.BlockSpec((tk, tm), lambda i, j, k: (k, i))}[mode]
    b_blk, b_map = {"nn": ((tk, tn), lambda i, j, k: (k, j)),
                    "nt": ((tn, tk), lambda i, j, k: (j, k)),
                    "tn": ((tk, tn), lambda i, j, k: (k, j))}[mode]
    if layer is None:
        b_spec = pl.BlockSpec(b_blk, b_map)
    else:
        b_spec = pl.BlockSpec((None,) + b_blk, lambda i, j, k: (layer,) + b_map(i, j, k))
    o_spec = pl.BlockSpec((tm, tn), lambda i, j, k: (i, j))
    v_spec = pl.BlockSpec((1, tn), lambda i, j, k: (0, j))
    in_specs, args = [a_spec, b_spec], [a, b]
    if bias is not None:
        in_specs.append(v_spec)
        args.append(_row(bias))
    out_shape = [jax.ShapeDtypeStruct((M, N), out_dtype)]
    out_specs = [o_spec]
    if res is not None:
        in_specs += [o_spec, v_spec]
        args += [res, _row(colscale)]
        out_shape.append(jax.ShapeDtypeStruct((M, N), F32))
        out_specs.append(o_spec)
    outs = pl.pallas_call(
        _mm_body(mode, nk, bias is not None, res is not None),
        grid=(M // tm, N // tn, nk),
        in_specs=in_specs, out_specs=out_specs, out_shape=out_shape,
        scratch_shapes=[pltpu.VMEM((tm, tn), F32)] if nk > 1 else [],
        compiler_params=_cparams("parallel", "parallel", "arbitrary"),
        name=name,
    )(*args)
    return outs if res is not None else outs[0]


def gmm(a, w, mode, out_dtype, name, *, bias=None, res=None, colscale=None, tr=512):
    S_ = a.shape[0]
    G = N_GROUPS
    C = a.shape[1] // G
    tr = _pick(S_, tr, 8)
    nr = S_ // tr
    if mode == "tn":
        def body(a_ref, b_ref, o_ref, acc_ref):
            i = pl.program_id(1)

            @pl.when(i == 0)
            def _():
                acc_ref[...] = jnp.zeros_like(acc_ref)

            acc_ref[...] += lax.dot_general(a_ref[...].astype(BF16), b_ref[...].astype(BF16), _DIMS["tn"],
                                            preferred_element_type=F32)

            @pl.when(i == nr - 1)
            def _():
                o_ref[...] = acc_ref[...].astype(o_ref.dtype)

        blk = pl.BlockSpec((tr, C), lambda g, i: (i, g))
        return pl.pallas_call(
            body, grid=(G, nr), in_specs=[blk, blk],
            out_specs=pl.BlockSpec((None, C, C), lambda g, i: (g, 0, 0)),
            out_shape=jax.ShapeDtypeStruct((G, C, C), out_dtype),
            scratch_shapes=[pltpu.VMEM((C, C), F32)],
            compiler_params=_cparams("parallel", "arbitrary"), name=name,
        )(a, w)

    has_bias, has_res = bias is not None, res is not None

    def body(*refs):
        a_ref, w_ref = refs[0], refs[1]
        pos = 2
        if has_bias:
            bias_ref = refs[pos]
            pos += 1
        if has_res:
            res_ref, cs_ref = refs[pos], refs[pos + 1]
            pos += 2
        o_ref = refs[pos]
        y = lax.dot_general(a_ref[...].astype(BF16), w_ref[...].astype(BF16), _DIMS[mode],
                            preferred_element_type=F32)
        if has_bias:
            y = y + bias_ref[...]
        o_ref[...] = y.astype(o_ref.dtype)
        if has_res:
            refs[pos + 1][...] = res_ref[...] + cs_ref[...] * y

    blk = pl.BlockSpec((tr, C), lambda i, g: (i, g))
    vec = pl.BlockSpec((1, C), lambda i, g: (0, g))
    in_specs = [blk, pl.BlockSpec((None, C, C), lambda i, g: (g, 0, 0))]
    args = [a, w]
    if has_bias:
        in_specs.append(vec)
        args.append(_row(bias))
    out_shape = [jax.ShapeDtypeStruct(a.shape, out_dtype)]
    out_specs = [blk]
    if has_res:
        in_specs += [blk, vec]
        args += [res, _row(colscale)]
        out_shape.append(jax.ShapeDtypeStruct(a.shape, F32))
        out_specs.append(blk)
    outs = pl.pallas_call(
        body, grid=(nr, G), in_specs=in_specs, out_specs=out_specs, out_shape=out_shape,
        compiler_params=_cparams("parallel", "parallel"), name=name,
    )(*args)
    return outs if has_res else outs[0]


def norm_fwd(x, g, sc, sh, out_dtype, name, tr=512):
    S_, Dn = x.shape
    tr = _pick(S_, tr, 8)

    def body(x_ref, g_ref, sc_ref, sh_ref, o_ref):
        xv = x_ref[...]
        r = lax.rsqrt(jnp.mean(xv * xv, axis=-1, keepdims=True) + EPS)
        o_ref[...] = (((xv * r) * g_ref[...]) * (1.0 + sc_ref[...]) + sh_ref[...]).astype(o_ref.dtype)

    blk = pl.BlockSpec((tr, Dn), lambda i: (i, 0))
    vec = pl.BlockSpec((1, Dn), lambda i: (0, 0))
    return pl.pallas_call(
        body, grid=(S_ // tr,), in_specs=[blk, vec, vec, vec], out_specs=blk,
        out_shape=jax.ShapeDtypeStruct((S_, Dn), out_dtype),
        compiler_params=_cparams("parallel"), name=name,
    )(x, _row(g), _row(sc), _row(sh))


def norm_bwd(x, g, sc, dh, dres, name, tr=512):
    S_, Dn = x.shape
    tr = _pick(S_, tr, 8)
    has_res = dres is not None

    def body(*refs):
        x_ref, g_ref, sc_ref, dh_ref = refs[:4]
        pos = 4
        if has_res:
            dres_ref = refs[pos]
            pos += 1
        dx_ref, s1_ref, s2_ref = refs[pos:pos + 3]
        i = pl.program_id(0)

        @pl.when(i == 0)
        def _():
            s1_ref[...] = jnp.zeros_like(s1_ref)
            s2_ref[...] = jnp.zeros_like(s2_ref)

        xv = x_ref[...]
        r = lax.rsqrt(jnp.mean(xv * xv, axis=-1, keepdims=True) + EPS)
        n = xv * r
        dhv = dh_ref[...].astype(F32)
        dn = dhv * (g_ref[...] * (1.0 + sc_ref[...]))
        dx = r * (dn - n * jnp.mean(dn * n, axis=-1, keepdims=True))
        if has_res:
            dx = dx + dres_ref[...]
        dx_ref[...] = dx
        s1_ref[...] += jnp.sum(dhv, axis=0, keepdims=True)
        s2_ref[...] += jnp.sum(dhv * n, axis=0, keepdims=True)

    blk = pl.BlockSpec((tr, Dn), lambda i: (i, 0))
    vec = pl.BlockSpec((1, Dn), lambda i: (0, 0))
    in_specs, args = [blk, vec, vec, blk], [x, _row(g), _row(sc), dh]
    if has_res:
        in_specs.append(blk)
        args.append(dres)
    vshape = jax.ShapeDtypeStruct((1, Dn), F32)
    return pl.pallas_call(
        body, grid=(S_ // tr,), in_specs=in_specs, out_specs=[blk, vec, vec],
        out_shape=[jax.ShapeDtypeStruct((S_, Dn), F32), vshape, vshape],
        compiler_params=_cparams("arbitrary"), name=name,
    )(*args)


def gate_bwd(dx, y, colscale, name, tr=512):
    S_, Dn = dx.shape
    tr = _pick(S_, tr, 8)

    def body(dx_ref, y_ref, cs_ref, d_ref, a_ref, c_ref):
        i = pl.program_id(0)

        @pl.when(i == 0)
        def _():
            a_ref[...] = jnp.zeros_like(a_ref)
            c_ref[...] = jnp.zeros_like(c_ref)

        dxv = dx_ref[...]
        d_ref[...] = (dxv * cs_ref[...]).astype(d_ref.dtype)
        a_ref[...] += jnp.sum(dxv * y_ref[...].astype(F32), axis=0, keepdims=True)
        c_ref[...] += jnp.sum(dxv, axis=0, keepdims=True)

    blk = pl.BlockSpec((tr, Dn), lambda i: (i, 0))
    vec = pl.BlockSpec((1, Dn), lambda i: (0, 0))
    vshape = jax.ShapeDtypeStruct((1, Dn), F32)
    return pl.pallas_call(
        body, grid=(S_ // tr,), in_specs=[blk, blk, vec], out_specs=[blk, vec, vec],
        out_shape=[jax.ShapeDtypeStruct((S_, Dn), BF16), vshape, vshape],
        compiler_params=_cparams("arbitrary"), name=name,
    )(dx, y, _row(colscale))


def loss_head(x, g, target, name, tr=512):
    S_, Dn = x.shape
    tr = _pick(S_, tr, 8)

    def body(x_ref, g_ref, t_ref, dx_ref, dg_ref, loss_ref):
        i = pl.program_id(0)

        @pl.when(i == 0)
        def _():
            dg_ref[...] = jnp.zeros_like(dg_ref)
            loss_ref[...] = jnp.zeros_like(loss_ref)

        xv = x_ref[...]
        r = lax.rsqrt(jnp.mean(xv * xv, axis=-1, keepdims=True) + EPS)
        n = xv * r
        e = n * g_ref[...] - t_ref[...]
        loss_ref[...] += 0.5 * jnp.sum(jnp.mean(e * e, axis=-1, keepdims=True), axis=0, keepdims=True)
        dy = e * (1.0 / Dn)
        dg_ref[...] += jnp.sum(dy * n, axis=0, keepdims=True)
        dn = dy * g_ref[...]
        dx_ref[...] = r * (dn - n * jnp.mean(dn * n, axis=-1, keepdims=True))

    blk = pl.BlockSpec((tr, Dn), lambda i: (i, 0))
    vec = pl.BlockSpec((1, Dn), lambda i: (0, 0))
    one = pl.BlockSpec((1, 1), lambda i: (0, 0))
    return pl.pallas_call(
        body, grid=(S_ // tr,), in_specs=[blk, vec, blk], out_specs=[blk, vec, one],
        out_shape=[jax.ShapeDtypeStruct((S_, Dn), F32), jax.ShapeDtypeStruct((1, Dn), F32),
                   jax.ShapeDtypeStruct((1, 1), F32)],
        compiler_params=_cparams("arbitrary"), name=name,
    )(x, _row(g), target)


POOL_HALO = 16
POOL_CHUNK = 512


def _rows(ref, lo, hi, n_rows):
    parts = []
    if lo < 0:
        parts.append(jnp.zeros((-lo, ref.shape[1]), F32))
    parts.append(ref[max(lo, 0):min(hi, n_rows), :].astype(F32))
    if hi > n_rows:
        parts.append(jnp.zeros((hi - n_rows, ref.shape[1]), F32))
    return parts[0] if len(parts) == 1 else jnp.concatenate(parts, axis=0)


def _window_sum(e, w, back):
    n = e.shape[0]
    s, width = e, 1
    while width < w:
        s = s + pltpu.roll(s, width if back else n - width, 0)
        width *= 2
    return s


def _pool_call(h, out_dtype, name, backward):
    S_, Dn = h.shape
    C = Dn // N_GROUPS
    ch = _pick(S_, POOL_CHUNK, 8)

    def body(h_ref, o_ref):
        g = pl.program_id(0)
        for gi, w in enumerate(POOL_WINDOWS):
            @pl.when(g == gi)
            def _(w=w):
                for r0 in range(0, S_, ch):
                    t = (r0 + lax.broadcasted_iota(jnp.int32, (ch, C), 0)).astype(F32)
                    cnt = jnp.minimum(t + 1.0, float(w))
                    if not backward:
                        ext = _rows(h_ref, r0 - POOL_HALO, r0 + ch, S_)
                        cur = ext[POOL_HALO:]
                        mean = _window_sum(ext, w, True)[POOL_HALO:] / cnt
                        o_ref[r0:r0 + ch, :] = (mean - cur).astype(o_ref.dtype)
                    else:
                        ext = _rows(h_ref, r0, r0 + ch + POOL_HALO, S_)
                        text = (r0 + lax.broadcasted_iota(jnp.int32, (ch + POOL_HALO, C), 0)).astype(F32)
                        e = ext / jnp.minimum(text + 1.0, float(w))
                        o_ref[r0:r0 + ch, :] = (_window_sum(e, w, False)[:ch] - ext[:ch]).astype(o_ref.dtype)

    blk = pl.BlockSpec((S_, C), lambda g: (0, g))
    return pl.pallas_call(
        body, grid=(N_GROUPS,), in_specs=[blk], out_specs=blk,
        out_shape=jax.ShapeDtypeStruct((S_, Dn), out_dtype),
        compiler_params=_cparams("parallel"), name=name,
    )(h)


GLU_CHUNK = 512
GLU_HALO = 16
_SQRT_HALF = 0.7071067811865476
_INV_SQRT_2PI = 0.3989422804014327


def _gelu(a):
    return 0.5 * a * (1.0 + lax.erf(a * _SQRT_HALF))


def _gelu_grad(a):
    return 0.5 * (1.0 + lax.erf(a * _SQRT_HALF)) + a * (_INV_SQRT_2PI * jnp.exp(-0.5 * a * a))


def glu_fwd(u, conv_w, conv_b, name):
    S_, F2 = u.shape
    Fh = F2 // 2
    tf = GLU_TILE
    nt = Fh // tf
    ch = _pick(S_, GLU_CHUNK, GLU_HALO)

    def body(a_ref, v_ref, cw_ref, cb_ref, z_ref):
        cw0, cw1, cw2 = cw_ref[0:1, :], cw_ref[1:2, :], cw_ref[2:3, :]
        cb = cb_ref[...]
        for r0 in range(0, S_, ch):
            ext = _rows(a_ref, r0 - GLU_HALO, r0 + ch, S_)
            a0 = ext[GLU_HALO:]
            a1 = pltpu.roll(ext, 1, 0)[GLU_HALO:]
            a2 = pltpu.roll(ext, 2, 0)[GLU_HALO:]
            ac = a2 * cw0 + a1 * cw1 + a0 * cw2 + cb
            z_ref[r0:r0 + ch, :] = (_gelu(ac) * v_ref[r0:r0 + ch, :].astype(F32)).astype(z_ref.dtype)

    return pl.pallas_call(
        body, grid=(nt,),
        in_specs=[pl.BlockSpec((S_, tf), lambda j: (0, j)), pl.BlockSpec((S_, tf), lambda j: (0, j + nt)),
                  pl.BlockSpec((3, tf), lambda j: (0, j)), pl.BlockSpec((1, tf), lambda j: (0, j))],
        out_specs=pl.BlockSpec((S_, tf), lambda j: (0, j)),
        out_shape=jax.ShapeDtypeStruct((S_, Fh), BF16),
        compiler_params=_cparams("parallel"), name=name,
    )(u, u, conv_w, _row(conv_b))


def glu_bwd(u, dz, conv_w, conv_b, name):
    S_, F2 = u.shape
    Fh = F2 // 2
    tf = GLU_TILE
    nt = Fh // tf
    ch = _pick(S_, GLU_CHUNK, GLU_HALO)

    def body(a_ref, v_ref, dz_ref, cw_ref, cb_ref, du_ref, dcw_ref, dcb_ref, da_buf, dv_buf, sems):
        j = pl.program_id(0)
        slot = j % 2

        def writes(step, sl):
            lo = pl.multiple_of(step * tf, tf)
            return (pltpu.make_async_copy(da_buf.at[sl], du_ref.at[:, pl.ds(lo, tf)], sems.at[sl, 0]),
                    pltpu.make_async_copy(dv_buf.at[sl], du_ref.at[:, pl.ds(Fh + lo, tf)], sems.at[sl, 1]))

        @pl.when(j >= 2)
        def _():
            for cp in writes(j - 2, slot):
                cp.wait()

        cw0, cw1, cw2 = cw_ref[0:1, :], cw_ref[1:2, :], cw_ref[2:3, :]
        cb = cb_ref[...]
        acc = [jnp.zeros((1, tf), F32) for _ in range(4)]
        n = ch + GLU_HALO
        for r0 in range(0, S_, ch):
            ext = _rows(a_ref, r0 - GLU_HALO, r0 + n, S_)
            a0 = ext[GLU_HALO:]
            a1 = pltpu.roll(ext, 1, 0)[GLU_HALO:]
            a2 = pltpu.roll(ext, 2, 0)[GLU_HALO:]
            ac = a2 * cw0 + a1 * cw1 + a0 * cw2 + cb
            vv = _rows(v_ref, r0, r0 + n, S_)
            dzv = _rows(dz_ref, r0, r0 + n, S_)
            gl = _gelu(ac)
            dac = dzv * vv * _gelu_grad(ac)
            da = (dac * cw2 + pltpu.roll(dac, n - 1, 0) * cw1 + pltpu.roll(dac, n - 2, 0) * cw0)[:ch]
            da_buf[slot, r0:r0 + ch, :] = da.astype(da_buf.dtype)
            dv_buf[slot, r0:r0 + ch, :] = (dzv[:ch] * gl[:ch]).astype(dv_buf.dtype)
            dc = dac[:ch]
            acc[0] = acc[0] + jnp.sum(dc * a2[:ch], axis=0, keepdims=True)
            acc[1] = acc[1] + jnp.sum(dc * a1[:ch], axis=0, keepdims=True)
            acc[2] = acc[2] + jnp.sum(dc * a0[:ch], axis=0, keepdims=True)
            acc[3] = acc[3] + jnp.sum(dc, axis=0, keepdims=True)
        dcw_ref[0:1, :] = acc[0]
        dcw_ref[1:2, :] = acc[1]
        dcw_ref[2:3, :] = acc[2]
        dcb_ref[...] = acc[3]
        for cp in writes(j, slot):
            cp.start()

        @pl.when(j == nt - 1)
        def _():
            for cp in writes(j, slot):
                cp.wait()
            if nt > 1:
                for cp in writes(j - 1, 1 - slot):
                    cp.wait()

    return pl.pallas_call(
        body, grid=(nt,),
        in_specs=[pl.BlockSpec((S_, tf), lambda j: (0, j)), pl.BlockSpec((S_, tf), lambda j: (0, j + nt)),
                  pl.BlockSpec((S_, tf), lambda j: (0, j)),
                  pl.BlockSpec((3, tf), lambda j: (0, j)), pl.BlockSpec((1, tf), lambda j: (0, j))],
        out_specs=[_ANY, pl.BlockSpec((3, tf), lambda j: (0, j)), pl.BlockSpec((1, tf), lambda j: (0, j))],
        out_shape=[jax.ShapeDtypeStruct((S_, F2), BF16), jax.ShapeDtypeStruct((3, Fh), F32),
                   jax.ShapeDtypeStruct((1, Fh), F32)],
        scratch_shapes=[pltpu.VMEM((2, S_, tf), BF16), pltpu.VMEM((2, S_, tf), BF16), pltpu.SemaphoreType.DMA((2, 2))],
        compiler_params=_cparams("arbitrary"), name=name,
    )(u, u, dz, conv_w, _row(conv_b))


def rope_tables(pos, inv, name, tr=512):
    S_ = pos.shape[0]
    tr = _pick(S_, tr, 8)

    def body(p_ref, inv_ref, c_ref, s1_ref, s2_ref):
        ang = p_ref[...] * inv_ref[...]
        lane = lax.broadcasted_iota(jnp.int32, ang.shape, 1)
        half = QK_ROPE // 2
        cosv, sinv = jnp.cos(ang), jnp.sin(ang)
        c_ref[...] = jnp.where(lane < QK_ROPE, cosv, 0.0)
        s1_ref[...] = jnp.where(lane < half, -sinv, 0.0)
        s2_ref[...] = jnp.where((lane >= half) & (lane < QK_ROPE), sinv, 0.0)

    blk = pl.BlockSpec((tr, LANES), lambda i: (i, 0))
    shp = jax.ShapeDtypeStruct((S_, LANES), F32)
    return pl.pallas_call(
        body, grid=(S_ // tr,),
        in_specs=[pl.BlockSpec((tr, 1), lambda i: (i, 0)), pl.BlockSpec((1, LANES), lambda i: (0, 0))],
        out_specs=[blk, blk, blk], out_shape=[shp, shp, shp],
        compiler_params=_cparams("parallel"), name=name,
    )(pos, inv)


_HALF = QK_ROPE // 2


def _rope(t, c, s1, s2):
    return t * c + pltpu.roll(t, LANES - _HALF, 1) * s1 + pltpu.roll(t, _HALF, 1) * s2


def _rope_t(d, c, s1, s2):
    return d * c + pltpu.roll(d * s1, _HALF, 1) + pltpu.roll(d * s2, LANES - _HALF, 1)


def q_prep(q, tabs, scale, backward, name, tr=512):
    S_, W = q.shape
    tr = _pick(S_, tr, 8)

    def body(q_ref, c_ref, s1_ref, s2_ref, o_ref):
        o_ref[:, 0:LANES] = (q_ref[:, 0:LANES].astype(F32) * scale).astype(o_ref.dtype)
        t = q_ref[:, LANES:2 * LANES].astype(F32)
        fn = _rope_t if backward else _rope
        o_ref[:, LANES:2 * LANES] = (fn(t, c_ref[...], s1_ref[...], s2_ref[...]) * scale).astype(o_ref.dtype)

    blk = pl.BlockSpec((tr, HEAD_PAD), lambda i, h: (i, h))
    tab = pl.BlockSpec((tr, LANES), lambda i, h: (i, 0))
    return pl.pallas_call(
        body, grid=(S_ // tr, W // HEAD_PAD), in_specs=[blk, tab, tab, tab], out_specs=blk,
        out_shape=jax.ShapeDtypeStruct((S_, W), BF16),
        compiler_params=_cparams("parallel", "parallel"), name=name,
    )(q, *tabs)


def k_prep(knv, kv_ext, tabs, name, tr=512):
    S_ = knv.shape[0]
    tr = _pick(S_, tr, 8)

    def body(kn_ref, v_ref, t_ref, c_ref, s1_ref, s2_ref, o_ref, vx_ref):
        o_ref[:, 0:LANES] = kn_ref[...].astype(o_ref.dtype)
        o_ref[:, LANES:2 * LANES] = _rope(t_ref[...], c_ref[...], s1_ref[...], s2_ref[...]).astype(o_ref.dtype)
        vx_ref[:, 0:V_HEAD] = v_ref[...].astype(vx_ref.dtype)
        vx_ref[:, V_HEAD:HEAD_PAD] = jnp.ones((tr, HEAD_PAD - V_HEAD), vx_ref.dtype)

    tab = pl.BlockSpec((tr, LANES), lambda i, h: (i, 0))
    head = pl.BlockSpec((tr, HEAD_PAD), lambda i, h: (i, h))
    shp = jax.ShapeDtypeStruct((S_, N_HEADS * HEAD_PAD), BF16)
    return pl.pallas_call(
        body, grid=(S_ // tr, N_HEADS),
        in_specs=[pl.BlockSpec((tr, LANES), lambda i, h: (i, h)),
                  pl.BlockSpec((tr, V_HEAD), lambda i, h: (i, N_HEADS + h)),
                  pl.BlockSpec((tr, LANES), lambda i, h: (i, KV_RANK // LANES)), tab, tab, tab],
        out_specs=[head, head], out_shape=[shp, shp],
        compiler_params=_cparams("parallel", "parallel"), name=name,
    )(knv, knv, kv_ext, *tabs)


def k_prep_bwd(dk_a, dk_b, dv_a, dv_b, tabs, name, tr=256):
    S_ = dk_a.shape[0]
    tr = _pick(S_, tr, 8)
    HV = N_HEADS * V_HEAD

    def body(ka_ref, kb_ref, va_ref, vb_ref, c_ref, s1_ref, s2_ref, o_ref, t_ref):
        dr = jnp.zeros((tr, LANES), F32)
        for h in range(N_HEADS):
            lo = h * HEAD_PAD
            o_ref[:, h * LANES:(h + 1) * LANES] = (ka_ref[:, lo:lo + LANES] + kb_ref[:, lo:lo + LANES]).astype(o_ref.dtype)
            dr = dr + ka_ref[:, lo + LANES:lo + 2 * LANES] + kb_ref[:, lo + LANES:lo + 2 * LANES]
        o_ref[:, HV:2 * HV] = (va_ref[...] + vb_ref[...]).astype(o_ref.dtype)
        t_ref[...] = _rope_t(dr, c_ref[...], s1_ref[...], s2_ref[...])

    kblk = pl.BlockSpec((tr, N_HEADS * HEAD_PAD), lambda i: (i, 0))
    vblk = pl.BlockSpec((tr, HV), lambda i: (i, 0))
    tab = pl.BlockSpec((tr, LANES), lambda i: (i, 0))
    return pl.pallas_call(
        body, grid=(S_ // tr,), in_specs=[kblk, kblk, vblk, vblk, tab, tab, tab],
        out_specs=[pl.BlockSpec((tr, 2 * HV), lambda i: (i, 0)), tab],
        out_shape=[jax.ShapeDtypeStruct((S_, 2 * HV), BF16), jax.ShapeDtypeStruct((S_, LANES), F32)],
        compiler_params=_cparams("parallel"), name=name,
    )(dk_a, dk_b, dv_a, dv_b, *tabs)


_NEG = -1e30


def attn_fwd(q, k, vx, name):
    S_ = q.shape[0]
    TQ = _pick(S_, ATT_Q_BLOCK, 8)
    TK = _pick(S_, ATT_K_BLOCK, TQ)
    HP = ATT_HEADS_PER_STEP
    W = HP * HEAD_PAD
    ratio = TK // TQ

    def body(q_ref, k_ref, v_ref, o_ref, lse_ref):
        i = pl.program_id(1)
        qs = [q_ref[:, h * HEAD_PAD:(h + 1) * HEAD_PAD] for h in range(HP)]

        def step(j, carry, masked):
            start = pl.multiple_of(j * TK, TK)
            out = []
            for h in range(HP):
                m, acc = carry[h]
                cols = slice(h * HEAD_PAD, (h + 1) * HEAD_PAD)
                s = lax.dot_general(qs[h], k_ref[pl.ds(start, TK), cols], _DIMS["nt"], preferred_element_type=F32)
                if masked:
                    rowi = i * TQ + lax.broadcasted_iota(jnp.int32, (TQ, TK), 0)
                    coli = j * TK + lax.broadcasted_iota(jnp.int32, (TQ, TK), 1)
                    s = jnp.where(coli <= rowi, s, _NEG)
                m_new = jnp.maximum(m, jnp.max(s, axis=-1, keepdims=True))
                alpha = jnp.exp(m - m_new)
                p = jnp.exp(s - m_new).astype(BF16)
                acc = alpha * acc + lax.dot_general(p, v_ref[pl.ds(start, TK), cols], _DIMS["nn"],
                                                    preferred_element_type=F32)
                out.append((m_new, acc))
            return tuple(out)

        init = tuple((jnp.full((TQ, 1), _NEG, F32), jnp.zeros((TQ, HEAD_PAD), F32)) for _ in range(HP))
        last = i // ratio
        carry = step(last, lax.fori_loop(0, last, functools.partial(step, masked=False), init), True)
        for h in range(HP):
            m, acc = carry[h]
            l = acc[:, V_HEAD:]
            o_ref[:, h * V_HEAD:(h + 1) * V_HEAD] = (acc[:, :V_HEAD] / l).astype(o_ref.dtype)
            lse_ref[h] = m + jnp.log(jnp.max(l, axis=-1, keepdims=True))

    return pl.pallas_call(
        body, grid=(N_HEADS // HP, S_ // TQ),
        in_specs=[pl.BlockSpec((TQ, W), lambda g, i: (i, g)),
                  pl.BlockSpec((S_, W), lambda g, i: (0, g)),
                  pl.BlockSpec((S_, W), lambda g, i: (0, g))],
        out_specs=[pl.BlockSpec((TQ, HP * V_HEAD), lambda g, i: (i, g)),
                   pl.BlockSpec((HP, TQ, 1), lambda g, i: (g, i, 0))],
        out_shape=[jax.ShapeDtypeStruct((S_, N_HEADS * V_HEAD), BF16), jax.ShapeDtypeStruct((N_HEADS, S_, 1), F32)],
        compiler_params=_cparams("parallel", "parallel"), name=name,
    )(q, k, vx)


def attn_delta(o, do, name, tr=512):
    S_ = o.shape[0]
    tr = _pick(S_, tr, 8)

    def body(o_ref, do_ref, d_ref):
        d_ref[...] = jnp.sum(o_ref[...].astype(F32) * do_ref[...].astype(F32), axis=-1, keepdims=True)

    blk = pl.BlockSpec((tr, V_HEAD), lambda i, h: (i, h))
    return pl.pallas_call(
        body, grid=(S_ // tr, N_HEADS), in_specs=[blk, blk],
        out_specs=pl.BlockSpec((None, tr, 1), lambda i, h: (h, i, 0)),
        out_shape=jax.ShapeDtypeStruct((N_HEADS, S_, 1), F32),
        compiler_params=_cparams("parallel", "parallel"), name=name,
    )(o, do)


def attn_bwd(q, k, vx, do, lse_row, delta_row, name):
    S_ = q.shape[0]
    TK = _pick(S_, ATT_BWD_K_BLOCK, LANES)
    TQ = _pick(S_, ATT_BWD_Q_BLOCK, TK)
    HP = ATT_HEADS_PER_STEP
    W = HP * HEAD_PAD
    ratio = TQ // TK
    nq = S_ // TQ

    def body(q_ref, do_ref, lse_ref, dl_ref, k_ref, v_ref, dq_ref, dk_ref, dv_ref):
        j = pl.program_id(1)

        @pl.when(j == 0)
        def _():
            dq_ref[...] = jnp.zeros_like(dq_ref)

        ks = [k_ref[:, h * HEAD_PAD:(h + 1) * HEAD_PAD] for h in range(HP)]
        vs = [v_ref[:, h * HEAD_PAD:h * HEAD_PAD + V_HEAD] for h in range(HP)]

        def step(i, carry, masked):
            start = pl.multiple_of(i * TQ, TQ)
            out = []
            for h in range(HP):
                dk, dv = carry[h]
                cols = slice(h * HEAD_PAD, (h + 1) * HEAD_PAD)
                qv = q_ref[pl.ds(start, TQ), cols]
                dov = do_ref[pl.ds(start, TQ), h * V_HEAD:(h + 1) * V_HEAD]
                st = lax.dot_general(ks[h], qv, _DIMS["nt"], preferred_element_type=F32)
                pt = jnp.exp(st - lse_ref[h, :, pl.ds(start, TQ)])
                if masked:
                    keyi = j * TK + lax.broadcasted_iota(jnp.int32, (TK, TQ), 0)
                    qryi = i * TQ + lax.broadcasted_iota(jnp.int32, (TK, TQ), 1)
                    pt = jnp.where(keyi <= qryi, pt, 0.0)
                dpt = lax.dot_general(vs[h], dov, _DIMS["nt"], preferred_element_type=F32)
                dst = (pt * (dpt - dl_ref[h, :, pl.ds(start, TQ)])).astype(BF16)
                dv = dv + lax.dot_general(pt.astype(BF16), dov, _DIMS["nn"], preferred_element_type=F32)
                dk = dk + lax.dot_general(dst, qv, _DIMS["nn"], preferred_element_type=F32)
                dq_ref[pl.ds(start, TQ), cols] += lax.dot_general(dst, ks[h], _DIMS["tn"], preferred_element_type=F32)
                out.append((dk, dv))
            return tuple(out)

        init = tuple((jnp.zeros((TK, HEAD_PAD), F32), jnp.zeros((TK, V_HEAD), F32)) for _ in range(HP))
        first = j // ratio
        carry = lax.fori_loop(first + 1, nq, functools.partial(step, masked=False), step(first, init, True))
        for h in range(HP):
            dk_ref[:, h * HEAD_PAD:(h + 1) * HEAD_PAD] = carry[h][0]
            dv_ref[:, h * V_HEAD:(h + 1) * V_HEAD] = carry[h][1]

    return pl.pallas_call(
        body, grid=(N_HEADS // HP, S_ // TK),
        in_specs=[pl.BlockSpec((S_, W), lambda g, j: (0, g)),
                  pl.BlockSpec((S_, HP * V_HEAD), lambda g, j: (0, g)),
                  pl.BlockSpec((HP, 1, S_), lambda g, j: (g, 0, 0)),
                  pl.BlockSpec((HP, 1, S_), lambda g, j: (g, 0, 0)),
                  pl.BlockSpec((TK, W), lambda g, j: (j, g)),
                  pl.BlockSpec((TK, W), lambda g, j: (j, g))],
        out_specs=[pl.BlockSpec((S_, W), lambda g, j: (0, g)),
                   pl.BlockSpec((TK, W), lambda g, j: (j, g)),
                   pl.BlockSpec((TK, HP * V_HEAD), lambda g, j: (j, g))],
        out_shape=[jax.ShapeDtypeStruct((S_, N_HEADS * HEAD_PAD), F32),
                   jax.ShapeDtypeStruct((S_, N_HEADS * HEAD_PAD), F32),
                   jax.ShapeDtypeStruct((S_, N_HEADS * V_HEAD), F32)],
        compiler_params=_cparams("parallel", "arbitrary"), name=name,
    )(q, do, lse_row, delta_row, k, vx)


def mods_fwd(c_all, mod_w, mod_b, name, tn=512):
    L, Dn, E = mod_w.shape
    R = c_all.shape[0]
    tn = _pick(E, tn, LANES)

    def body(c_ref, w_ref, b_ref, o_ref):
        cv = c_ref[...]
        sc = (cv / (1.0 + jnp.exp(-cv))).astype(BF16)
        o_ref[...] = lax.dot_general(sc, w_ref[...].astype(BF16), _DIMS["nn"], preferred_element_type=F32) + b_ref[...]

    return pl.pallas_call(
        body, grid=(L, E // tn),
        in_specs=[pl.BlockSpec((R, Dn), lambda l, j: (0, 0)), pl.BlockSpec((None, Dn, tn), lambda l, j: (l, 0, j)),
                  pl.BlockSpec((None, 1, tn), lambda l, j: (l, 0, j))],
        out_specs=pl.BlockSpec((None, R, tn), lambda l, j: (l, 0, j)),
        out_shape=jax.ShapeDtypeStruct((L, R, E), F32),
        compiler_params=_cparams("parallel", "parallel"), name=name,
    )(c_all, mod_w, mod_b.reshape(L, 1, E))


def _adam_math(w, g, m, v):
    m = ADAM_B1 * m + (1.0 - ADAM_B1) * g
    v = ADAM_B2 * v + (1.0 - ADAM_B2) * (g * g)
    m_hat = m / (1.0 - ADAM_B1 ** ADAM_STEP)
    v_hat = v / (1.0 - ADAM_B2 ** ADAM_STEP)
    delta = -ADAM_LR * (m_hat / (jnp.sqrt(v_hat) + ADAM_EPS) + ADAM_WD * w)
    return delta, m, v


def _as2d(a):
    return a.reshape(-1, a.shape[-1]) if a.ndim != 2 else a


def adamw(w, g, m, v, name):
    shape = w.shape
    w2, g2, m2, v2 = _as2d(w), _as2d(g), _as2d(m), _as2d(v)
    R, C = w2.shape
    tr = _pick(R, max(8, (1 << 18) // C // 8 * 8), 8)

    def body(w_ref, g_ref, m_ref, v_ref, d_ref, mo_ref, vo_ref):
        d, mn, vn = _adam_math(w_ref[...], g_ref[...], m_ref[...], v_ref[...])
        d_ref[...] = d
        mo_ref[...] = mn
        vo_ref[...] = vn

    blk = pl.BlockSpec((tr, C), lambda i: (i, 0))
    shp = jax.ShapeDtypeStruct((R, C), F32)
    outs = pl.pallas_call(
        body, grid=(R // tr,), in_specs=[blk] * 4, out_specs=[blk] * 3, out_shape=[shp] * 3,
        compiler_params=_cparams("parallel"), name=name,
    )(w2, g2, m2, v2)
    return tuple(o.reshape(shape) for o in outs)


def adamw_sum(parts, w, m, v, name):
    P, R, C = parts.shape

    def body(p_ref, w_ref, m_ref, v_ref, g_ref, d_ref, mo_ref, vo_ref):
        g = p_ref[0]
        for k in range(1, P):
            g = g + p_ref[k]
        d, mn, vn = _adam_math(w_ref[...], g, m_ref[...], v_ref[...])
        g_ref[...] = g
        d_ref[...] = d
        mo_ref[...] = mn
        vo_ref[...] = vn

    shp = jax.ShapeDtypeStruct((R, C), F32)
    return pl.pallas_call(body, out_shape=[shp] * 4, compiler_params=_cparams(), name=name)(parts, w, m, v)


def adamw_modw(c_col, dm, w, m, v, name, tr=256, tn=512):
    L, Dn, E = w.shape
    B = c_col.shape[0]
    tr = _pick(Dn, tr, 8)
    tn = _pick(E, tn, LANES)

    def body(c_ref, dm_ref, w_ref, m_ref, v_ref, g_ref, d_ref, mo_ref, vo_ref):
        g = jnp.zeros((tr, tn), F32)
        for b in range(B):
            cv = c_ref[b]
            g = g + (cv / (1.0 + jnp.exp(-cv))) * dm_ref[b:b + 1, :]
        d, mn, vn = _adam_math(w_ref[...], g, m_ref[...], v_ref[...])
        g_ref[...] = g
        d_ref[...] = d
        mo_ref[...] = mn
        vo_ref[...] = vn

    blk = pl.BlockSpec((None, tr, tn), lambda l, i, j: (l, i, j))
    shp = jax.ShapeDtypeStruct((L, Dn, E), F32)
    return pl.pallas_call(
        body, grid=(L, Dn // tr, E // tn),
        in_specs=[pl.BlockSpec((B, tr, 1), lambda l, i, j: (0, i, 0)),
                  pl.BlockSpec((None, B, tn), lambda l, i, j: (l, 0, j)), blk, blk, blk],
        out_specs=[blk] * 4, out_shape=[shp] * 4,
        compiler_params=_cparams("parallel", "parallel", "parallel"), name=name,
    )(c_col, dm, w, m, v)


def add_round(a, b, name, tr=512):
    R, C = a.shape
    tr = _pick(R, tr, 16)

    def body(a_ref, b_ref, o_ref):
        o_ref[...] = (a_ref[...] + b_ref[...].astype(F32)).astype(BF16)

    blk = pl.BlockSpec((tr, C), lambda i: (i, 0))
    return pl.pallas_call(
        body, grid=(R // tr,), in_specs=[blk, blk], out_specs=blk, out_shape=jax.ShapeDtypeStruct((R, C), BF16),
        compiler_params=_cparams("parallel"), name=name,
    )(a, b)


def sum_parts(parts, name, tr=512):
    P, R, C = parts.shape
    tr = _pick(R, tr, 16)

    def body(p_ref, o_ref):
        s = p_ref[0].astype(F32)
        for k in range(1, P):
            s = s + p_ref[k].astype(F32)
        o_ref[...] = s

    return pl.pallas_call(
        body, grid=(R // tr,), in_specs=[pl.BlockSpec((P, tr, C), lambda i: (0, i, 0))],
        out_specs=pl.BlockSpec((tr, C), lambda i: (i, 0)), out_shape=jax.ShapeDtypeStruct((R, C), F32),
        compiler_params=_cparams("parallel"), name=name,
    )(parts)


_ANY = pl.BlockSpec(memory_space=pl.ANY)


def _place():
    return lax.axis_index("x"), lax.axis_index("y"), lax.axis_index("c")


def _flip(v, bit):
    return 1 - v if bit else v


def chip_gather(buf, name):
    def body(in_ref, out_ref, send_sems, recv_sems):
        x, y, c = _place()
        me = 2 * x + y
        sends = []
        for k in range(1, N_CHIPS):
            px, py = _flip(x, k >> 1), _flip(y, k & 1)
            cp = pltpu.make_async_remote_copy(src_ref=in_ref, dst_ref=out_ref.at[me], send_sem=send_sems.at[k - 1],
                                              recv_sem=recv_sems.at[k - 1], device_id=(px, py, c), device_id_type=MESH)
            cp.start()
            sends.append(cp)
        for k in range(1, N_CHIPS):
            px, py = _flip(x, k >> 1), _flip(y, k & 1)
            pltpu.make_async_remote_copy(src_ref=in_ref, dst_ref=out_ref.at[2 * px + py], send_sem=send_sems.at[k - 1],
                                         recv_sem=recv_sems.at[k - 1], device_id=(px, py, c),
                                         device_id_type=MESH).wait_recv()
        for cp in sends:
            cp.wait_send()

    out = pl.pallas_call(
        body, in_specs=[_ANY], out_specs=_ANY,
        out_shape=jax.ShapeDtypeStruct((N_CHIPS,) + buf.shape, buf.dtype),
        scratch_shapes=[pltpu.SemaphoreType.DMA((N_CHIPS - 1,)), pltpu.SemaphoreType.DMA((N_CHIPS - 1,))],
        name=name,
    )(buf)
    return lax.dynamic_update_index_in_dim(out, buf, 2 * lax.axis_index("x") + lax.axis_index("y"), 0)


def chip_all_to_all(buf, name):
    def body(in_ref, out_ref, send_sems, recv_sems):
        x, y, c = _place()
        me = 2 * x + y
        sends = []
        for k in range(1, N_CHIPS):
            px, py = _flip(x, k >> 1), _flip(y, k & 1)
            cp = pltpu.make_async_remote_copy(src_ref=in_ref.at[2 * px + py], dst_ref=out_ref.at[me],
                                              send_sem=send_sems.at[k - 1], recv_sem=recv_sems.at[k - 1],
                                              device_id=(px, py, c), device_id_type=MESH)
            cp.start()
            sends.append(cp)
        for k in range(1, N_CHIPS):
            px, py = _flip(x, k >> 1), _flip(y, k & 1)
            pltpu.make_async_remote_copy(src_ref=in_ref.at[me], dst_ref=out_ref.at[2 * px + py],
                                         send_sem=send_sems.at[k - 1], recv_sem=recv_sems.at[k - 1],
                                         device_id=(px, py, c), device_id_type=MESH).wait_recv()
        for cp in sends:
            cp.wait_send()

    out = pl.pallas_call(
        body, in_specs=[_ANY], out_specs=_ANY, out_shape=jax.ShapeDtypeStruct(buf.shape, buf.dtype),
        scratch_shapes=[pltpu.SemaphoreType.DMA((N_CHIPS - 1,)), pltpu.SemaphoreType.DMA((N_CHIPS - 1,))],
        name=name,
    )(buf)
    me = 2 * lax.axis_index("x") + lax.axis_index("y")
    return lax.dynamic_update_index_in_dim(out, _index(buf, me), me, 0)


def core_gather(buf, name):
    def body(in_ref, out_ref, send_sem, recv_sem):
        x, y, c = _place()
        cp = pltpu.make_async_remote_copy(src_ref=in_ref, dst_ref=out_ref.at[c], send_sem=send_sem, recv_sem=recv_sem,
                                          device_id=(x, y, 1 - c), device_id_type=MESH)
        cp.start()
        pltpu.make_async_remote_copy(src_ref=in_ref, dst_ref=out_ref.at[1 - c], send_sem=send_sem, recv_sem=recv_sem,
                                     device_id=(x, y, 1 - c), device_id_type=MESH).wait_recv()
        cp.wait_send()

    out = pl.pallas_call(
        body, in_specs=[_ANY], out_specs=_ANY, out_shape=jax.ShapeDtypeStruct((2,) + buf.shape, buf.dtype),
        scratch_shapes=[pltpu.SemaphoreType.DMA, pltpu.SemaphoreType.DMA],
        name=name,
    )(buf)
    return lax.dynamic_update_index_in_dim(out, buf, lax.axis_index("c"), 0)


def core_swap(buf, name):
    def body(in_ref, out_ref, send_sem, recv_sem):
        x, y, c = _place()
        cp = pltpu.make_async_remote_copy(src_ref=in_ref, dst_ref=out_ref, send_sem=send_sem, recv_sem=recv_sem,
                                          device_id=(x, y, 1 - c), device_id_type=MESH)
        cp.start()
        cp.wait()

    return pl.pallas_call(
        body, in_specs=[_ANY], out_specs=_ANY, out_shape=jax.ShapeDtypeStruct(buf.shape, buf.dtype),
        scratch_shapes=[pltpu.SemaphoreType.DMA, pltpu.SemaphoreType.DMA],
        name=name,
    )(buf)


def device_gather(buf, name):
    def body(in_ref, out_ref, send_sems, recv_sems, local_sem):
        x, y, c = _place()
        me = 4 * x + 2 * y + c
        mine = pltpu.make_async_copy(in_ref, out_ref.at[me], local_sem)
        mine.start()
        sends = []
        for k in range(1, N_DEV):
            peer = (_flip(x, (k >> 2) & 1), _flip(y, (k >> 1) & 1), _flip(c, k & 1))
            cp = pltpu.make_async_remote_copy(src_ref=in_ref, dst_ref=out_ref.at[me], send_sem=send_sems.at[k - 1],
                                              recv_sem=recv_sems.at[k - 1], device_id=peer, device_id_type=MESH)
            cp.start()
            sends.append(cp)
        for k in range(1, N_DEV):
            peer = (_flip(x, (k >> 2) & 1), _flip(y, (k >> 1) & 1), _flip(c, k & 1))
            pltpu.make_async_remote_copy(src_ref=in_ref, dst_ref=out_ref.at[4 * peer[0] + 2 * peer[1] + peer[2]],
                                         send_sem=send_sems.at[k - 1], recv_sem=recv_sems.at[k - 1], device_id=peer,
                                         device_id_type=MESH).wait_recv()
        for cp in sends:
            cp.wait_send()
        mine.wait()

    return pl.pallas_call(
        body, in_specs=[_ANY], out_specs=_ANY, out_shape=jax.ShapeDtypeStruct((N_DEV,) + buf.shape, buf.dtype),
        scratch_shapes=[pltpu.SemaphoreType.DMA((N_DEV - 1,)), pltpu.SemaphoreType.DMA((N_DEV - 1,)),
                        pltpu.SemaphoreType.DMA],
        name=name,
    )(buf)


def _region(ref, chip_axis=None, chip=None, chip_size=None, half_axis=None, half=None, half_size=None):
    idx = [slice(None)] * len(ref.shape)
    if chip is not None:
        idx[chip_axis] = pl.ds(chip * chip_size, chip_size)
    if half is not None:
        idx[half_axis] = pl.ds(half * half_size, half_size)
    return ref.at[tuple(idx)]


def gather_weights(shards, axes, name, after=()):
    n = len(shards)

    def full_shape(t):
        shp = list(shards[t].shape)
        shp[axes[t][0]] *= N_CHIPS
        return tuple(shp)

    def body(*refs):
        ins, outs = refs[:n], refs[n + len(after):2 * n + len(after)]
        ici_send, ici_recv, d2d_send, d2d_recv, own_send, own_recv = refs[2 * n + len(after):]
        x, y, c = _place()
        me = 2 * x + y

        def part(t, ref, chip, half):
            ca, ha = axes[t]
            return _region(ref, ca, chip, ins[t].shape[ca], ha, half, ins[t].shape[ha] // 2)

        def own(t):
            return pltpu.make_async_remote_copy(src_ref=ins[t], dst_ref=part(t, outs[t], me, None),
                                                send_sem=own_send.at[t], recv_sem=own_recv.at[t],
                                                device_id=(x, y, 1 - c), device_id_type=MESH)

        started = []
        for t in range(n):
            own(t).start()
            started.append(own(t))
        for t in range(n):
            for k in range(1, N_CHIPS):
                px, py = _flip(x, k >> 1), _flip(y, k & 1)
                cp = pltpu.make_async_remote_copy(src_ref=part(t, ins[t], None, c), dst_ref=part(t, outs[t], me, c),
                                                  send_sem=ici_send.at[t, k - 1], recv_sem=ici_recv.at[t, k - 1],
                                                  device_id=(px, py, c), device_id_type=MESH)
                cp.start()
                started.append(cp)
        for t in range(n):
            for k in range(1, N_CHIPS):
                px, py = _flip(x, k >> 1), _flip(y, k & 1)
                got = part(t, outs[t], 2 * px + py, c)
                pltpu.make_async_remote_copy(src_ref=part(t, ins[t], None, c), dst_ref=got,
                                             send_sem=ici_send.at[t, k - 1], recv_sem=ici_recv.at[t, k - 1],
                                             device_id=(px, py, c), device_id_type=MESH).wait_recv()
                fw = pltpu.make_async_remote_copy(src_ref=got, dst_ref=got, send_sem=d2d_send.at[t, k - 1],
                                                  recv_sem=d2d_recv.at[t, k - 1], device_id=(x, y, 1 - c),
                                                  device_id_type=MESH)
                fw.start()
                started.append(fw)
        for t in range(n):
            for k in range(1, N_CHIPS):
                px, py = _flip(x, k >> 1), _flip(y, k & 1)
                theirs = part(t, outs[t], 2 * px + py, 1 - c)
                pltpu.make_async_remote_copy(src_ref=theirs, dst_ref=theirs, send_sem=d2d_send.at[t, k - 1],
                                             recv_sem=d2d_recv.at[t, k - 1], device_id=(x, y, 1 - c),
                                             device_id_type=MESH).wait_recv()
        for t in range(n):
            own(t).wait_recv()
        for cp in started:
            cp.wait_send()

    sem = pltpu.SemaphoreType.DMA((n, N_CHIPS - 1))
    own_sem = pltpu.SemaphoreType.DMA((n,))
    return pl.pallas_call(
        body, in_specs=[_ANY] * (n + len(after)), out_specs=[_ANY] * n,
        out_shape=[jax.ShapeDtypeStruct(full_shape(t), shards[t].dtype) for t in range(n)],
        scratch_shapes=[sem, sem, sem, sem, own_sem, own_sem], name=name,
    )(*shards, *after)


_HBM = pl.BlockSpec(memory_space=pltpu.HBM)
_SEM = pl.BlockSpec(memory_space=pltpu.SEMAPHORE)
_EFFECT = pltpu.SideEffectType.DATAFLOW_SIDE_EFFECTING
WEIGHT_COPIES = N_CHIPS


def _weight_peer(k, x, y, c):
    return (x, y, 1 - c) if k == 0 else (_flip(x, k >> 1), _flip(y, k & 1), c)


def weights_start(shards, items, name, after=()):
    n_sh, n_it = len(shards), len(items)

    def src_of(refs, i):
        t, layer, _ = items[i]
        return refs[t] if layer is None else refs[t].at[layer]

    def land_shape(i):
        t, layer, ca = items[i]
        shp = list(shards[t].shape if layer is None else shards[t].shape[1:])
        shp[ca] *= N_CHIPS
        return tuple(shp)

    def body(*refs):
        shard_refs, land_refs = refs[:n_sh], refs[n_sh:n_sh + n_it]
        first_out = n_sh + n_it + len(after)
        send_sems = refs[first_out:first_out + n_it]
        recv_sems = refs[first_out + n_it:first_out + 2 * n_it]
        token = refs[-1]
        x, y, c = _place()
        me = 2 * x + y
        for i in range(n_it):
            src = src_of(shard_refs, i)
            ca = items[i][2]
            dst = _region(land_refs[i], ca, me, src.shape[ca])
            for k in range(WEIGHT_COPIES):
                pltpu.make_async_remote_copy(src_ref=src, dst_ref=dst, send_sem=send_sems[i], recv_sem=recv_sems[i],
                                             device_id=_weight_peer(k, x, y, c), device_id_type=MESH).start()
        token[...] = jnp.zeros_like(token)

    lands = [pltpu.with_memory_space_constraint(lax.empty(land_shape(i), shards[0].dtype), pltpu.HBM)
             for i in range(n_it)]
    ins = [pltpu.with_memory_space_constraint(a, pltpu.HBM) for a in shards] + lands
    sems = (pltpu.SemaphoreType.DMA(()),) * (2 * n_it)
    outs = pl.pallas_call(
        body, name=name,
        out_shape=sems + tuple(pltpu.HBM(a.shape, a.dtype) for a in ins) + (jax.ShapeDtypeStruct((8, LANES), F32),),
        in_specs=[_HBM] * len(ins) + [_ANY] * len(after),
        out_specs=(_SEM,) * (2 * n_it) + (_HBM,) * len(ins) + (pl.BlockSpec(memory_space=pltpu.VMEM),),
        input_output_aliases={i: 2 * n_it + i for i in range(len(ins))},
        compiler_params=pltpu.CompilerParams(has_side_effects=_EFFECT),
    )(*ins, *after)
    base = 2 * n_it
    return (list(outs[:n_it]), list(outs[n_it:base]), list(outs[base:base + n_sh]),
            list(outs[base + n_sh:base + n_sh + n_it]), outs[-1])


def weights_wait(send_sems, recv_sems, lands, after, keep, name):
    m = len(lands)

    def body(*refs):
        land_refs, send_refs, recv_refs = refs[:m], refs[m:2 * m], refs[2 * m:3 * m]
        x, y, c = _place()
        for j in range(m):
            cp = pltpu.make_async_remote_copy(src_ref=land_refs[j], dst_ref=land_refs[j], send_sem=send_refs[j],
                                              recv_sem=recv_refs[j], device_id=(x, y, 1 - c),
                                              device_id_type=MESH)
            cp.wait_send()
            cp.wait_recv()

    outs = pl.pallas_call(
        body, name=name,
        out_shape=tuple(pltpu.HBM(a.shape, a.dtype) for a in lands),
        in_specs=[_HBM] * m + [_SEM] * (2 * m) + [_ANY] + [_HBM] * len(keep),
        out_specs=(_HBM,) * m,
        input_output_aliases={j: j for j in range(m)},
        compiler_params=pltpu.CompilerParams(has_side_effects=_EFFECT),
    )(*lands, *send_sems, *recv_sems, after, *keep)
    return list(outs)


def reduce_to_sibling(lo, hi, name):
    n = len(lo)

    def body(*refs):
        los, his, outs = refs[:n], refs[n:2 * n], refs[2 * n:3 * n]
        send_sems, recv_sems = refs[3 * n:]
        x, y, c = _place()

        def copy(u, src):
            return pltpu.make_async_remote_copy(src_ref=src, dst_ref=outs[u], send_sem=send_sems.at[u],
                                                recv_sem=recv_sems.at[u], device_id=(x, y, 1 - c), device_id_type=MESH)

        for u in range(n):
            @pl.when(c == 0)
            def _(u=u):
                copy(u, his[u]).start()

            @pl.when(c == 1)
            def _(u=u):
                copy(u, los[u]).start()
        for u in range(n):
            copy(u, los[u]).wait_recv()
        for u in range(n):
            copy(u, los[u]).wait_send()

    return pl.pallas_call(
        body, in_specs=[_ANY] * (2 * n), out_specs=[_ANY] * n,
        out_shape=[jax.ShapeDtypeStruct(a.shape, a.dtype) for a in lo],
        scratch_shapes=[pltpu.SemaphoreType.DMA((n,)), pltpu.SemaphoreType.DMA((n,))], name=name,
    )(*lo, *hi)


def add_selected(lo, hi, other, name, tile_elems=1 << 19):
    R, C = lo.shape
    tr = _pick(R, max(16, tile_elems // C // 16 * 16), 16)

    def body(lo_ref, hi_ref, o_ref, out_ref):
        mine = jnp.where(lax.axis_index("c") == 0, lo_ref[...].astype(F32), hi_ref[...].astype(F32))
        out_ref[...] = (mine + o_ref[...].astype(F32)).astype(out_ref.dtype)

    blk = pl.BlockSpec((tr, C), lambda i: (i, 0))
    return pl.pallas_call(
        body, grid=(R // tr,), in_specs=[blk, blk, blk], out_specs=blk, out_shape=jax.ShapeDtypeStruct((R, C), BF16),
        compiler_params=_cparams("parallel"), name=name,
    )(lo, hi, other)


def scatter_to_chips(pieces, chip_axes, name):
    n = len(pieces)

    def block_shape(u):
        shp = list(pieces[u].shape)
        shp[chip_axes[u]] //= N_CHIPS
        return tuple(shp)

    def body(*refs):
        ins, outs = refs[:n], refs[n:2 * n]
        send_sems, recv_sems = refs[2 * n:]
        x, y, c = _place()
        me = 2 * x + y
        started = []
        for u in range(n):
            size = block_shape(u)[chip_axes[u]]
            for k in range(1, N_CHIPS):
                px, py = _flip(x, k >> 1), _flip(y, k & 1)
                cp = pltpu.make_async_remote_copy(src_ref=_region(ins[u], chip_axes[u], 2 * px + py, size),
                                                  dst_ref=outs[u].at[me], send_sem=send_sems.at[u, k - 1],
                                                  recv_sem=recv_sems.at[u, k - 1], device_id=(px, py, c),
                                                  device_id_type=MESH)
                cp.start()
                started.append(cp)
        for u in range(n):
            size = block_shape(u)[chip_axes[u]]
            for k in range(1, N_CHIPS):
                px, py = _flip(x, k >> 1), _flip(y, k & 1)
                pltpu.make_async_remote_copy(src_ref=_region(ins[u], chip_axes[u], me, size),
                                             dst_ref=outs[u].at[2 * px + py], send_sem=send_sems.at[u, k - 1],
                                             recv_sem=recv_sems.at[u, k - 1], device_id=(px, py, c),
                                             device_id_type=MESH).wait_recv()
        for cp in started:
            cp.wait_send()

    sem = pltpu.SemaphoreType.DMA((n, N_CHIPS - 1))
    return pl.pallas_call(
        body, in_specs=[_ANY] * n, out_specs=[_ANY] * n,
        out_shape=[jax.ShapeDtypeStruct((N_CHIPS,) + block_shape(u), pieces[u].dtype) for u in range(n)],
        scratch_shapes=[sem, sem], name=name,
    )(*pieces)


def scatter_start(pieces, chip_axes, name):
    n = len(pieces)

    def block_shape(u):
        shp = list(pieces[u].shape)
        shp[chip_axes[u]] //= N_CHIPS
        return tuple(shp)

    def body(*refs):
        ins, land_refs = refs[:n], refs[n:2 * n]
        send_sems, recv_sems = refs[2 * n:3 * n], refs[3 * n:4 * n]
        token = refs[-1]
        x, y, c = _place()
        me = 2 * x + y
        for u in range(n):
            size = block_shape(u)[chip_axes[u]]
            for k in range(1, N_CHIPS):
                px, py = _flip(x, k >> 1), _flip(y, k & 1)
                pltpu.make_async_remote_copy(src_ref=_region(ins[u], chip_axes[u], 2 * px + py, size),
                                             dst_ref=land_refs[u].at[me], send_sem=send_sems[u], recv_sem=recv_sems[u],
                                             device_id=(px, py, c), device_id_type=MESH).start()
        token[...] = jnp.zeros_like(token)

    lands = [pltpu.with_memory_space_constraint(lax.empty((N_CHIPS,) + block_shape(u), pieces[u].dtype), pltpu.HBM)
             for u in range(n)]
    ins = [pltpu.with_memory_space_constraint(a, pltpu.HBM) for a in pieces] + lands
    sems = (pltpu.SemaphoreType.DMA(()),) * (2 * n)
    outs = pl.pallas_call(
        body, name=name,
        out_shape=sems + tuple(pltpu.HBM(a.shape, a.dtype) for a in ins) + (jax.ShapeDtypeStruct((8, LANES), F32),),
        in_specs=[_HBM] * len(ins),
        out_specs=(_SEM,) * (2 * n) + (_HBM,) * len(ins) + (pl.BlockSpec(memory_space=pltpu.VMEM),),
        input_output_aliases={i: 2 * n + i for i in range(len(ins))},
        compiler_params=pltpu.CompilerParams(has_side_effects=_EFFECT),
    )(*ins)
    return list(outs[:n]), list(outs[n:2 * n]), list(outs[2 * n:3 * n]), list(outs[3 * n:4 * n]), outs[-1]


def scatter_wait(send_sems, recv_sems, lands, pieces, after, name):
    n = len(lands)

    def body(*refs):
        land_refs, send_refs, recv_refs = refs[:n], refs[n:2 * n], refs[2 * n:3 * n]
        x, y, c = _place()
        for u in range(n):
            three = land_refs[u].at[pl.ds(0, N_CHIPS - 1)]
            cp = pltpu.make_async_remote_copy(src_ref=three, dst_ref=three, send_sem=send_refs[u], recv_sem=recv_refs[u],
                                              device_id=(x, y, 1 - c), device_id_type=MESH)
            cp.wait_send()
            cp.wait_recv()

    outs = pl.pallas_call(
        body, name=name,
        out_shape=tuple(pltpu.HBM(a.shape, a.dtype) for a in lands),
        in_specs=[_HBM] * n + [_SEM] * (2 * n) + [_ANY] + [_HBM] * len(pieces),
        out_specs=(_HBM,) * n,
        input_output_aliases={j: j for j in range(n)},
        compiler_params=pltpu.CompilerParams(has_side_effects=_EFFECT),
    )(*lands, *send_sems, *recv_sems, after, *pieces)
    return list(outs)


def gather_halves(parts, slots, out_shapes, name):
    n = len(parts)

    def body(*refs):
        ins, outs = refs[:n], refs[n:n + len(out_shapes)]
        send_sems, recv_sems = refs[n + len(out_shapes):]
        x, y, c = _place()
        started = []
        for u in range(n):
            t, s = slots[u]
            cp = pltpu.make_async_remote_copy(src_ref=ins[u], dst_ref=outs[t].at[s, c], send_sem=send_sems.at[u],
                                              recv_sem=recv_sems.at[u], device_id=(x, y, 1 - c), device_id_type=MESH)
            cp.start()
            started.append(cp)
        for u in range(n):
            t, s = slots[u]
            pltpu.make_async_remote_copy(src_ref=ins[u], dst_ref=outs[t].at[s, 1 - c], send_sem=send_sems.at[u],
                                         recv_sem=recv_sems.at[u], device_id=(x, y, 1 - c),
                                         device_id_type=MESH).wait_recv()
        for cp in started:
            cp.wait_send()

    return pl.pallas_call(
        body, in_specs=[_ANY] * n, out_specs=[_ANY] * len(out_shapes),
        out_shape=[jax.ShapeDtypeStruct(shp, F32) for shp in out_shapes],
        scratch_shapes=[pltpu.SemaphoreType.DMA((n,)), pltpu.SemaphoreType.DMA((n,))], name=name,
    )(*parts)


WEIGHT_ORDER = ["mod_w", "mod_b", "norm1_g", "norm2_g", "pool_w", "pool_b", "pool_scale", "kv_in_g", "w_dkv",
                "ckv_norm_g", "w_uk", "w_uv", "w_dq", "q_norm_g", "w_uq", "w_o", "w_up", "conv_w", "conv_b", "w_down",
                "final_g"]
EXCHANGED = {"w_up": (2, 0), "w_down": (1, 0), "w_o": (1, 0), "w_uq": (2, 0), "w_dq": (1, 0), "pool_w": (2, 0),
             "w_dkv": (0, 1), "w_uk": (1, 0), "w_uv": (1, 0)}
SMALL_SHARDED = {"conv_w": 2, "pool_b": 1, "pool_scale": 1}
REPLICATED = ["mod_b", "norm1_g", "norm2_g", "kv_in_g", "ckv_norm_g", "q_norm_g", "conv_b", "final_g"]


def _padded(n, align):
    return -(-n // align) * align


def _flat_pad(parts, total):
    flat = jnp.concatenate(parts, axis=-1)
    pad = total - flat.shape[-1]
    if pad:
        flat = jnp.concatenate([flat, jnp.zeros(flat.shape[:-1] + (pad,), flat.dtype)], axis=-1)
    return flat


def _split_shards(full, axis):
    shp = full.shape
    t = full.reshape(shp[:axis] + (N_CHIPS, shp[axis] // N_CHIPS) + shp[axis + 1:])
    return jnp.moveaxis(t, axis, 0).reshape(N_CHIPS, -1)


def _join_shards(rows, shard_shape, axis):
    t = jnp.moveaxis(rows.reshape((N_CHIPS,) + tuple(shard_shape)), 0, axis)
    return t.reshape(tuple(shard_shape[:axis]) + (N_CHIPS * shard_shape[axis],) + tuple(shard_shape[axis + 1:]))


def _index(a, i, axis=0):
    return lax.dynamic_index_in_dim(a, i, axis, keepdims=False)


def kernel(x, c, positions, mod_w, mod_b, norm1_g, norm2_g, pool_w, pool_b, pool_scale, kv_in_g, w_dkv, ckv_norm_g, w_uk, w_uv, w_dq, q_norm_g, w_uq, w_o, w_up, conv_w, conv_b, w_down, final_g, loss_target, m_mod_w, m_mod_b, m_norm1_g, m_norm2_g, m_pool_w, m_pool_b, m_pool_scale, m_kv_in_g, m_w_dkv, m_ckv_norm_g, m_w_uk, m_w_uv, m_w_dq, m_q_norm_g, m_w_uq, m_w_o, m_w_up, m_conv_w, m_conv_b, m_w_down, m_final_g, v_mod_w, v_mod_b, v_norm1_g, v_norm2_g, v_pool_w, v_pool_b, v_pool_scale, v_kv_in_g, v_w_dkv, v_ckv_norm_g, v_w_uk, v_w_uv, v_w_dq, v_q_norm_g, v_w_uq, v_w_o, v_w_up, v_conv_w, v_conv_b, v_w_down, v_final_g):
    given = dict(locals())
    W = {n: given[n] for n in WEIGHT_ORDER}
    M1 = {n: given["m_" + n] for n in WEIGHT_ORDER}
    V2 = {n: given["v_" + n] for n in WEIGHT_ORDER}
    xi, yi, ci = lax.axis_index("x"), lax.axis_index("y"), lax.axis_index("c")
    chip = 2 * xi + yi
    dev = 4 * xi + 2 * yi + ci
    x0 = x[0]
    S_, D = x0.shape
    Fh = conv_b.shape[1]
    E = mod_b.shape[1]
    Es = E // N_CHIPS
    zD = jnp.zeros((D,), F32)

    c_all = device_gather(c, "gather_c").reshape(N_DEV, D)
    c_pad = jnp.concatenate([c_all, jnp.zeros((16 - N_DEV, D), F32)], axis=0)
    mod_b_mine = lax.dynamic_slice_in_dim(mod_b, chip * Es, Es, axis=1)
    mods_part = mods_fwd(c_pad, mod_w, mod_b_mine, "mods_fwd")
    mods_all = chip_gather(mods_part, "gather_mods")
    mods = jnp.swapaxes(_index(mods_all, dev, axis=2), 0, 1).reshape(DEPTH, E)
    mod = [[mods[l, k * D:(k + 1) * D] for k in range(6)] for l in range(DEPTH)]

    full = {}
    ssz = {n: math.prod(W[n].shape) for n in SMALL_SHARDED}
    Tw = _padded(sum(ssz.values()), 8 * PACK_COLS)
    small_rows = chip_gather(_flat_pad([W[n].reshape(-1) for n in SMALL_SHARDED], Tw).reshape(-1, PACK_COLS),
                             "gather_small_w").reshape(N_CHIPS, Tw)
    off = 0
    for n, axis in SMALL_SHARDED.items():
        full[n] = _join_shards(small_rows[:, off:off + ssz[n]], W[n].shape, axis)
        off += ssz[n]

    names = list(EXCHANGED)
    shards = [W[n].astype(BF16) for n in names]
    n_mla = DEPTH - N_A
    first_axes = {"w_up": (1, 0), "w_down": (0, 1), "pool_w": (1, 0)}
    first = gather_weights([shards[names.index(n)][0] for n in first_axes], list(first_axes.values()), "gather_weights0",
                           after=[mods, small_rows])
    for n, arr in zip(first_axes, first):
        full[(n, 0)] = arr
    items, groups = [], []

    def group(entries):
        groups.append(list(range(len(items), len(items) + len(entries))))
        for n, layer in entries:
            ca = EXCHANGED[n][0] - (0 if layer is None else 1)
            items.append((names.index(n), layer, 0 if n == "w_dkv" else ca))

    for l in range(1, N_A):
        group([("w_up", l), ("w_down", l), ("pool_w", l)])
    for j in range(n_mla):
        head = [("w_dkv", None), ("w_uk", None), ("w_uv", None)] if j == 0 else []
        group(head + [("w_dq", j), ("w_uq", j), ("w_o", j), ("w_up", N_A + j), ("w_down", N_A + j)])
    w_send, w_recv, shards_thru, lands, _ = weights_start(shards, items, "weights_start", after=first)

    def weights_ready(g, after):
        keep = shards_thru if g == len(groups) - 1 else []
        got = weights_wait([w_send[i] for i in groups[g]], [w_recv[i] for i in groups[g]], [lands[i] for i in groups[g]],
                           after, keep, f"weights_wait{g}")
        for i, arr in zip(groups[g], got):
            t, layer, _ = items[i]
            full[(names[t], 0 if layer is None else layer)] = arr

    q_rank = W["w_uq"].shape[1]
    kv_w = KV_RANK + QK_ROPE

    def uq_ext(j):
        wq = full[("w_uq", j)].reshape(q_rank, N_HEADS, QK_HEAD)
        return jnp.concatenate([wq, jnp.zeros((q_rank, N_HEADS, HEAD_PAD - QK_HEAD), BF16)],
                               axis=2).reshape(q_rank, N_HEADS * HEAD_PAD)


    half = QK_ROPE // 2
    inv = 1.0 / (ROPE_THETA ** (jnp.arange(0, QK_ROPE, 2, dtype=F32) / QK_ROPE))
    inv_row = jnp.concatenate([inv, inv, jnp.zeros((LANES - 2 * half,), F32)]).reshape(1, LANES)
    tabs = rope_tables(positions[0].astype(F32).reshape(S_, 1), inv_row, "rope_tables")
    att_scale = QK_HEAD ** -0.5

    saved = []
    xcur = x0
    kv_saved = None
    K = VX = knv = None
    for l in range(DEPTH):
        sh1, sc1, g1, sh2, sc2, g2 = mod[l]
        st = {"xin": xcur}
        if l:
            weights_ready(l - 1, xcur)
        if l == N_A:
            w_dkv_ext = jnp.concatenate([full[("w_dkv", 0)], jnp.zeros((D, KV_RANK + LANES - kv_w), BF16)], axis=1)
            w_ukv = jnp.concatenate([full[("w_uk", 0)], full[("w_uv", 0)]], axis=1)
            xn = norm_fwd(xcur, kv_in_g, zD, zD, BF16, "kvin_fwd")
            kv_ext = mm(xn, w_dkv_ext, "nn", F32, "dkv_mm")
            lat = kv_ext[:, :KV_RANK]
            zk = jnp.zeros((KV_RANK,), F32)
            ckv = norm_fwd(lat, ckv_norm_g, zk, zk, BF16, "ckv_fwd")
            knv = mm(ckv, w_ukv, "nn", BF16, "ukv_mm")
            K, VX = k_prep(knv, kv_ext, tabs, "k_prep")
            kv_saved = {"x": xcur, "xn": xn, "lat": lat, "ckv": ckv}
        if l < N_A:
            h1 = norm_fwd(xcur, norm1_g[l], sc1, sh1, F32, f"norm1_fwd{l}")
            st["pooled"] = _pool_call(h1, BF16, f"pool_fwd{l}", False)
            st["cs"] = g1 * full["pool_scale"][l]
            st["ypre"], xmid = gmm(st["pooled"], full[("pool_w", l)], "nn", F32, f"pool_mm{l}", bias=full["pool_b"][l],
                                   res=xcur, colscale=st["cs"])
        else:
            j = l - N_A
            st["h1"] = norm_fwd(xcur, norm1_g[l], sc1, sh1, BF16, f"norm1_fwd{l}")
            st["ql"] = mm(st["h1"], full[("w_dq", j)], "nn", F32, f"dq_mm{l}")
            st["cq"] = norm_fwd(st["ql"], q_norm_g[j], jnp.zeros_like(q_norm_g[j]), jnp.zeros_like(q_norm_g[j]), BF16,
                                f"qnorm_fwd{l}")
            st["w_uq_ext"] = uq_ext(j)
            qe = mm(st["cq"], st["w_uq_ext"], "nn", F32, f"uq_mm{l}")
            st["Q"] = q_prep(qe, tabs, att_scale, False, f"q_prep{l}")
            st["o"], lse = attn_fwd(st["Q"], K, VX, f"attn_fwd{l}")
            st["lse"] = lse.reshape(N_HEADS, 1, S_)
            st["y"], xmid = mm(st["o"], full[("w_o", j)], "nn", F32, f"wo_mm{l}", res=xcur, colscale=g1)
        st["xmid"] = xmid
        st["h2"] = norm_fwd(xmid, norm2_g[l], sc2, sh2, BF16, f"norm2_fwd{l}")
        st["u"] = mm(st["h2"], full[("w_up", l)], "nn", BF16, f"up_mm{l}")
        st["z"] = glu_fwd(st["u"], full["conv_w"][l], conv_b[l], f"glu_fwd{l}")
        st["f"], xcur = mm(st["z"], full[("w_down", l)], "nn", F32, f"down_mm{l}", tk=1408, res=xmid, colscale=g2)
        saved.append(st)

    dx, d_final_g, loss_part = loss_head(xcur, final_g, loss_target[0], "loss_head")
    loss = lax.psum(loss_part[0, 0], ("x", "y", "c"))

    def begin_reduce(tensors, first_slot, tag):
        units = []
        for n in tensors:
            ca = EXCHANGED[n][0]
            if W[n].ndim > 2:
                n_slots = W[n].shape[0] // 2
                for sl in range(first_slot if n_slots > 1 else 0, first_slot + 1 if n_slots > 1 else 1):
                    units.append((n, sl, G[(n, 2 * sl)], G[(n, 2 * sl + 1)], ca - 1))
            elif n == "w_dkv":
                g4 = G[(n, 0)].reshape(N_CHIPS, 2, -1, kv_w)
                units.append((n, 0, g4[:, 0], g4[:, 1], 0))
            else:
                rows_half = W[n].shape[0] // 2
                units.append((n, 0, G[(n, 0)][:rows_half], G[(n, 0)][rows_half:], ca))
        lo = [u[2] for u in units]
        hi = [u[3] for u in units]
        theirs = reduce_to_sibling(lo, hi, f"reduce_cores_{tag}")
        sums = [add_selected(l_.reshape(-1, l_.shape[-1]), h_.reshape(-1, l_.shape[-1]), t_.reshape(-1, l_.shape[-1]),
                             f"reduce_cores_add_{tag}{i}").reshape(l_.shape)
                for i, (l_, h_, t_) in enumerate(zip(lo, hi, theirs))]
        return units, sums, [u[4] for u in units]

    G = {}
    dmods = [None] * DEPTH
    d_norm1 = [None] * DEPTH
    d_norm2 = [None] * DEPTH
    d_conv_b = [None] * DEPTH
    d_qnorm = [None] * n_mla
    dkv_acc = []
    for l in reversed(range(DEPTH)):
        sh1, sc1, g1, sh2, sc2, g2 = mod[l]
        st = saved[l]
        df, a2, _ = gate_bwd(dx, st["f"], g2, f"gate2_bwd{l}")
        dz = mm(df, full[("w_down", l)], "nt", BF16, f"down_dx{l}")
        G[("w_down", l)] = mm(st["z"], df, "tn", BF16, f"down_dw{l}")
        du, dcw, dcb = glu_bwd(st["u"], dz, full["conv_w"][l], conv_b[l], f"glu_bwd{l}")
        G[("conv_w", l)] = dcw
        d_conv_b[l] = dcb[0]
        dh2 = mm(du, full[("w_up", l)], "nt", BF16, f"up_dx{l}", tk=1408)
        G[("w_up", l)] = mm(st["h2"], du, "tn", BF16, f"up_dw{l}")
        dxmid, s1, s2 = norm_bwd(st["xmid"], norm2_g[l], sc2, dh2, dx, f"norm2_bwd{l}")
        dsh2, dsc2, d_norm2[l] = s1[0], s2[0] * norm2_g[l], s2[0] * (1.0 + sc2)
        if l < N_A:
            dyp, a1, csum = gate_bwd(dxmid, st["ypre"], st["cs"], f"gate1_bwd{l}")
            dg1 = full["pool_scale"][l] * a1[0]
            G[("pool_scale", l)] = g1 * a1[0]
            G[("pool_b", l)] = st["cs"] * csum[0]
            dpooled = gmm(dyp, full[("pool_w", l)], "nt", F32, f"pool_dx{l}")
            G[("pool_w", l)] = gmm(st["pooled"], dyp, "tn", BF16, f"pool_dw{l}")
            dh1 = _pool_call(dpooled, F32, f"pool_bwd{l}", True)
        else:
            j = l - N_A
            dy, a1, _ = gate_bwd(dxmid, st["y"], g1, f"gate1_bwd{l}")
            dg1 = a1[0]
            do = mm(dy, full[("w_o", j)], "nt", BF16, f"wo_dx{l}")
            G[("w_o", j)] = mm(st["o"], dy, "tn", BF16, f"wo_dw{l}")
            delta = attn_delta(st["o"], do, f"attn_delta{l}").reshape(N_HEADS, 1, S_)
            dQ, dK, dV = attn_bwd(st["Q"], K, VX, do, st["lse"], delta, f"attn_bwd{l}")
            dkv_acc.append((dK, dV))
            dqe = q_prep(dQ, tabs, att_scale, True, f"q_prep_bwd{l}")
            dcq = mm(dqe, st["w_uq_ext"], "nt", F32, f"uq_dx{l}")
            G[("w_uq", j)] = mm(st["cq"], dqe, "tn", BF16, f"uq_dw{l}").reshape(q_rank, N_HEADS, HEAD_PAD)[
                :, :, :QK_HEAD].reshape(q_rank, N_HEADS * QK_HEAD)
            zq = jnp.zeros_like(q_norm_g[j])
            dql, _, s2q = norm_bwd(st["ql"], q_norm_g[j], zq, dcq, None, f"qnorm_bwd{l}")
            d_qnorm[j] = s2q[0]
            dh1 = mm(dql, full[("w_dq", j)], "nt", BF16, f"dq_dx{l}")
            G[("w_dq", j)] = mm(st["h1"], dql, "tn", BF16, f"dq_dw{l}")
        dx, s1, s2 = norm_bwd(st["xin"], norm1_g[l], sc1, dh1, dxmid, f"norm1_bwd{l}")
        dsh1, dsc1, d_norm1[l] = s1[0], s2[0] * norm1_g[l], s2[0] * (1.0 + sc1)
        dmods[l] = jnp.concatenate([dsh1, dsc1, dg1, dsh2, dsc2, a2[0]])
        if l == N_A:
            (dk_a, dv_a), (dk_b, dv_b) = dkv_acc
            dknv, d_tk = k_prep_bwd(dk_a, dk_b, dv_a, dv_b, tabs, "k_prep_bwd")
            dckv = mm(dknv, w_ukv, "nt", F32, "ukv_dx")
            d_ukv = mm(kv_saved["ckv"], dknv, "tn", BF16, "ukv_dw")
            G[("w_uk", 0)], G[("w_uv", 0)] = d_ukv[:, :N_HEADS * QK_NOPE], d_ukv[:, N_HEADS * QK_NOPE:]
            zk = jnp.zeros((KV_RANK,), F32)
            dlat, _, s2c = norm_bwd(kv_saved["lat"], ckv_norm_g, zk, dckv, None, "ckv_bwd")
            d_ckv_g = s2c[0]
            dkv_ext = jnp.concatenate([dlat, d_tk], axis=1)
            dxn = mm(dkv_ext, w_dkv_ext, "nt", BF16, "dkv_dx")
            G[("w_dkv", 0)] = mm(kv_saved["xn"], dkv_ext, "tn", BF16, "dkv_dw")[:, :kv_w]
            dx, _, s2k = norm_bwd(kv_saved["x"], kv_in_g, zD, dxn, dx, "kvin_bwd")
            d_kvin_g = s2k[0]
            e_units, e_sums, e_axes = begin_reduce([n for n in EXCHANGED if n != "pool_w"], 1, "early")
            e_send, e_recv, e_pieces, e_lands, e_token = scatter_start(e_sums, e_axes, "reduce_chips_start")
            early = (e_units, e_send, e_recv, e_lands, e_pieces, e_axes)
            mod[l - 1][5] = mod[l - 1][5] + e_token[0, 0]

    late = begin_reduce([n for n in EXCHANGED if W[n].ndim > 2 and W[n].shape[0] == DEPTH] + ["pool_w"], 0, "late")
    late_got = scatter_to_chips(late[1], late[2], "reduce_chips_late")
    e_units, e_send, e_recv, e_lands, e_pieces, e_axes = early
    early_got = scatter_wait(e_send, e_recv, e_lands, e_pieces, dx, "reduce_chips_wait")
    units = e_units + late[0]
    reduced = []
    for i, (sm, ax, g4) in enumerate(zip(e_pieces + late[1], e_axes + late[2], list(early_got) + list(late_got))):
        size = sm.shape[ax] // N_CHIPS
        g4 = lax.dynamic_update_index_in_dim(g4, lax.dynamic_slice_in_dim(sm, chip * size, size, axis=ax), chip, 0)
        blk = g4.shape[1:]
        reduced.append(sum_parts(g4.reshape(N_CHIPS, -1, blk[-1]), f"reduce_chips_add{i}").reshape(blk))
    slots, out_shapes = [], []
    for n in EXCHANGED:
        mine = [i for i, u in enumerate(units) if u[0] == n]
        out_shapes.append((len(mine), 2) + reduced[mine[0]].shape)
        slots += [(len(out_shapes) - 1, units[i][1]) for i in mine]
    order = [i for n in EXCHANGED for i, u in enumerate(units) if u[0] == n]
    halves = gather_halves([reduced[i] for i in order], slots, out_shapes, "reduce_gather")

    grads, deltas, new_m, new_v = {}, {}, {}, {}
    for ti, n in enumerate(EXCHANGED):
        g = halves[ti]
        for i, u in enumerate(units):
            if u[0] == n:
                g = lax.dynamic_update_slice(g, reduced[i][None, None], (u[1], ci) + (0,) * reduced[i].ndim)
        grads[n] = g.reshape(W[n].shape)
        deltas[n], new_m[n], new_v[n] = adamw(W[n], grads[n], M1[n], V2[n], f"adamw_{n}")

    small = {"mod_b": jnp.stack(dmods), "norm1_g": jnp.stack(d_norm1), "norm2_g": jnp.stack(d_norm2),
             "kv_in_g": d_kvin_g, "ckv_norm_g": d_ckv_g, "q_norm_g": jnp.stack(d_qnorm),
             "conv_b": jnp.stack(d_conv_b), "final_g": d_final_g[0]}
    extra = {n: jnp.stack([G[(n, i)] for i in range(W[n].shape[0])]) for n in SMALL_SHARDED}
    ssizes = {n: math.prod(W[n].shape) for n in REPLICATED}
    esizes = {n: math.prod(extra[n].shape) for n in SMALL_SHARDED}
    Ts = _padded(sum(ssizes.values()) + sum(esizes.values()), 8 * PACK_COLS)

    def pack_small(d, tail=()):
        return _flat_pad([d[n].reshape(-1) for n in REPLICATED] + [t.reshape(-1) for t in tail],
                         Ts).reshape(Ts // PACK_COLS, PACK_COLS)

    parts = device_gather(pack_small(small, [extra[n] for n in SMALL_SHARDED]), "gather_small")
    outs = adamw_sum(parts, pack_small(W), pack_small(M1), pack_small(V2), "adamw_small")
    off = 0
    for n in REPLICATED:
        for dst, o in zip((grads, deltas, new_m, new_v), outs):
            dst[n] = o.reshape(-1)[off:off + ssizes[n]].reshape(W[n].shape)
        off += ssizes[n]
    for n, axis in SMALL_SHARDED.items():
        g_full = outs[0].reshape(-1)[off:off + esizes[n]].reshape(extra[n].shape)
        off += esizes[n]
        size = W[n].shape[axis]
        grads[n] = lax.dynamic_slice_in_dim(g_full, chip * size, size, axis=axis)
        deltas[n], new_m[n], new_v[n] = adamw(W[n], grads[n], M1[n], V2[n], f"adamw_{n}")

    dm_all = parts.reshape(N_DEV, -1)[:, :DEPTH * E].reshape(N_DEV, DEPTH, E)
    dm_mine = jnp.swapaxes(lax.dynamic_slice_in_dim(dm_all, chip * Es, Es, axis=2), 0, 1)
    grads["mod_w"], deltas["mod_w"], new_m["mod_w"], new_v["mod_w"] = adamw_modw(
        c_all.reshape(N_DEV, D, 1), dm_mine, mod_w, m_mod_w, v_mod_w, "adamw_mod_w")

    return (loss, dx.reshape(x.shape), *[grads[n] for n in WEIGHT_ORDER], *[deltas[n] for n in WEIGHT_ORDER],
            *[new_m[n] for n in WEIGHT_ORDER], *[new_v[n] for n in WEIGHT_ORDER])
```

```python
import functools
import math

import jax
import jax.numpy as jnp
from jax import lax
from jax.experimental import pallas as pl
from jax.experimental.pallas import tpu as pltpu

F32 = jnp.float32
BF16 = jnp.bfloat16
MESH = pl.DeviceIdType.MESH

DEPTH = 4
N_A = 2
POOL_WINDOWS = (2, 4, 8, 16)
N_GROUPS = 4
N_HEADS = 8
QK_NOPE = 128
QK_ROPE = 64
V_HEAD = 128
QK_HEAD = QK_NOPE + QK_ROPE
HEAD_PAD = 256
KV_RANK = 256
ROPE_THETA = 10000.0
EPS = 1e-6
ADAM_LR = 0.001
ADAM_B1 = 0.9
ADAM_B2 = 0.999
ADAM_EPS = 1e-08
ADAM_WD = 0.01
ADAM_STEP = 10

N_CHIPS = 4
N_DEV = 8
LANES = 128
PACK_COLS = 1024
VMEM_LIMIT = 56 * 1024 * 1024
GLU_TILE = 256
ATT_BWD_K_BLOCK = 256
ATT_BWD_Q_BLOCK = 512
ATT_Q_BLOCK = 256
ATT_K_BLOCK = 512
ATT_HEADS_PER_STEP = 2


def _cparams(*sem):
    return pltpu.CompilerParams(dimension_semantics=sem if sem else None, vmem_limit_bytes=VMEM_LIMIT)


def _pick(n, target, mult):
    best = None
    d = mult
    while d <= min(n, target):
        if n % d == 0:
            best = d
        d += mult
    return n if best is None else best


def _row(v):
    return v.reshape(1, -1).astype(F32)


_DIMS = {"nn": (((1,), (0,)), ((), ())), "nt": (((1,), (1,)), ((), ())), "tn": (((0,), (0,)), ((), ()))}


def _mm_body(mode, nk, has_bias, has_res):
    def body(*refs):
        a_ref, b_ref = refs[0], refs[1]
        pos = 2
        bias_ref = res_ref = cs_ref = None
        if has_bias:
            bias_ref = refs[pos]
            pos += 1
        if has_res:
            res_ref, cs_ref = refs[pos], refs[pos + 1]
            pos += 2
        o_ref = refs[pos]
        pos += 1
        o2_ref = None
        if has_res:
            o2_ref = refs[pos]
            pos += 1
        acc_ref = refs[pos] if nk > 1 else None
        k = pl.program_id(2)
        part = lax.dot_general(a_ref[...].astype(BF16), b_ref[...].astype(BF16), _DIMS[mode],
                               preferred_element_type=F32)

        def finish(y):
            if has_bias:
                y = y + bias_ref[...]
            o_ref[...] = y.astype(o_ref.dtype)
            if has_res:
                o2_ref[...] = res_ref[...] + cs_ref[...] * y

        if nk == 1:
            finish(part)
            return

        @pl.when(k == 0)
        def _():
            acc_ref[...] = part

        @pl.when((k > 0) & (k < nk - 1))
        def _():
            acc_ref[...] += part

        @pl.when(k == nk - 1)
        def _():
            finish(acc_ref[...] + part)

    return body


def mm(a, b, mode, out_dtype, name, *, tm=1408, tn=1408, tk=1024, bias=None, res=None, colscale=None, layer=None):
    bshape = b.shape if layer is None else b.shape[1:]
    if mode == "nn":
        (M, K), N = a.shape, bshape[1]
    elif mode == "nt":
        (M, K), N = a.shape, bshape[0]
    else:
        (K, M), N = a.shape, bshape[1]
    tm = _pick(M, tm, LANES if mode == "tn" else 8)
    tn = _pick(N, tn, LANES)
    tk = _pick(K, tk, LANES) if mode != "tn" else _pick(K, tk, 8)
    nk = K // tk
    a_spec = {"nn": pl.BlockSpec((tm, tk), lambda i, j, k: (i, k)),
              "nt": pl.BlockSpec((tm, tk), lambda i, j, k: (i, k)),
              "tn": pl.BlockSpec((tk, tm), lambda i, j, k: (k, i))}[mode]
    b_blk, b_map = {"nn": ((tk, tn), lambda i, j, k: (k, j)),
                    "nt": ((tn, tk), lambda i, j, k: (j, k)),
                    "tn": ((tk, tn), lambda i, j, k: (k, j))}[mode]
    if layer is None:
        b_spec = pl.BlockSpec(b_blk, b_map)
    else:
        b_spec = pl.BlockSpec((None,) + b_blk, lambda i, j, k: (layer,) + b_map(i, j, k))
    o_spec = pl.BlockSpec((tm, tn), lambda i, j, k: (i, j))
    v_spec = pl.BlockSpec((1, tn), lambda i, j, k: (0, j))
    in_specs, args = [a_spec, b_spec], [a, b]
    if bias is not None:
        in_specs.append(v_spec)
        args.append(_row(bias))
    out_shape = [jax.ShapeDtypeStruct((M, N), out_dtype)]
    out_specs = [o_spec]
    if res is not None:
        in_specs += [o_spec, v_spec]
        args += [res, _row(colscale)]
        out_shape.append(jax.ShapeDtypeStruct((M, N), F32))
        out_specs.append(o_spec)
    outs = pl.pallas_call(
        _mm_body(mode, nk, bias is not None, res is not None),
        grid=(M // tm, N // tn, nk),
        in_specs=in_specs, out_specs=out_specs, out_shape=out_shape,
        scratch_shapes=[pltpu.VMEM((tm, tn), F32)] if nk > 1 else [],
        compiler_params=_cparams("parallel", "parallel", "arbitrary"),
        name=name,
    )(*args)
    return outs if res is not None else outs[0]


def gmm(a, w, mode, out_dtype, name, *, bias=None, res=None, colscale=None, tr=512):
    S_ = a.shape[0]
    G = N_GROUPS
    C = a.shape[1] // G
    tr = _pick(S_, tr, 8)
    nr = S_ // tr
    if mode == "tn":
        def body(a_ref, b_ref, o_ref, acc_ref):
            i = pl.program_id(1)

            @pl.when(i == 0)
            def _():
                acc_ref[...] = jnp.zeros_like(acc_ref)

            acc_ref[...] += lax.dot_general(a_ref[...].astype(BF16), b_ref[...].astype(BF16), _DIMS["tn"],
                                            preferred_element_type=F32)

            @pl.when(i == nr - 1)
            def _():
                o_ref[...] = acc_ref[...].astype(o_ref.dtype)

        blk = pl.BlockSpec((tr, C), lambda g, i: (i, g))
        return pl.pallas_call(
            body, grid=(G, nr), in_specs=[blk, blk],
            out_specs=pl.BlockSpec((None, C, C), lambda g, i: (g, 0, 0)),
            out_shape=jax.ShapeDtypeStruct((G, C, C), out_dtype),
            scratch_shapes=[pltpu.VMEM((C, C), F32)],
            compiler_params=_cparams("parallel", "arbitrary"), name=name,
        )(a, w)

    has_bias, has_res = bias is not None, res is not None

    def body(*refs):
        a_ref, w_ref = refs[0], refs[1]
        pos = 2
        if has_bias:
            bias_ref = refs[pos]
            pos += 1
        if has_res:
            res_ref, cs_ref = refs[pos], refs[pos + 1]
            pos += 2
        o_ref = refs[pos]
        y = lax.dot_general(a_ref[...].astype(BF16), w_ref[...].astype(BF16), _DIMS[mode],
                            preferred_element_type=F32)
        if has_bias:
            y = y + bias_ref[...]
        o_ref[...] = y.astype(o_ref.dtype)
        if has_res:
            refs[pos + 1][...] = res_ref[...] + cs_ref[...] * y

    blk = pl.BlockSpec((tr, C), lambda i, g: (i, g))
    vec = pl.BlockSpec((1, C), lambda i, g: (0, g))
    in_specs = [blk, pl.BlockSpec((None, C, C), lambda i, g: (g, 0, 0))]
    args = [a, w]
    if has_bias:
        in_specs.append(vec)
        args.append(_row(bias))
    out_shape = [jax.ShapeDtypeStruct(a.shape, out_dtype)]
    out_specs = [blk]
    if has_res:
        in_specs += [blk, vec]
        args += [res, _row(colscale)]
        out_shape.append(jax.ShapeDtypeStruct(a.shape, F32))
        out_specs.append(blk)
    outs = pl.pallas_call(
        body, grid=(nr, G), in_specs=in_specs, out_specs=out_specs, out_shape=out_shape,
        compiler_params=_cparams("parallel", "parallel"), name=name,
    )(*args)
    return outs if has_res else outs[0]


def norm_fwd(x, g, sc, sh, out_dtype, name, tr=512):
    S_, Dn = x.shape
    tr = _pick(S_, tr, 8)

    def body(x_ref, g_ref, sc_ref, sh_ref, o_ref):
        xv = x_ref[...]
        r = lax.rsqrt(jnp.mean(xv * xv, axis=-1, keepdims=True) + EPS)
        o_ref[...] = (((xv * r) * g_ref[...]) * (1.0 + sc_ref[...]) + sh_ref[...]).astype(o_ref.dtype)

    blk = pl.BlockSpec((tr, Dn), lambda i: (i, 0))
    vec = pl.BlockSpec((1, Dn), lambda i: (0, 0))
    return pl.pallas_call(
        body, grid=(S_ // tr,), in_specs=[blk, vec, vec, vec], out_specs=blk,
        out_shape=jax.ShapeDtypeStruct((S_, Dn), out_dtype),
        compiler_params=_cparams("parallel"), name=name,
    )(x, _row(g), _row(sc), _row(sh))


def norm_bwd(x, g, sc, dh, dres, name, tr=512):
    S_, Dn = x.shape
    tr = _pick(S_, tr, 8)
    has_res = dres is not None

    def body(*refs):
        x_ref, g_ref, sc_ref, dh_ref = refs[:4]
        pos = 4
        if has_res:
            dres_ref = refs[pos]
            pos += 1
        dx_ref, s1_ref, s2_ref = refs[pos:pos + 3]
        i = pl.program_id(0)

        @pl.when(i == 0)
        def _():
            s1_ref[...] = jnp.zeros_like(s1_ref)
            s2_ref[...] = jnp.zeros_like(s2_ref)

        xv = x_ref[...]
        r = lax.rsqrt(jnp.mean(xv * xv, axis=-1, keepdims=True) + EPS)
        n = xv * r
        dhv = dh_ref[...].astype(F32)
        dn = dhv * (g_ref[...] * (1.0 + sc_ref[...]))
        dx = r * (dn - n * jnp.mean(dn * n, axis=-1, keepdims=True))
        if has_res:
            dx = dx + dres_ref[...]
        dx_ref[...] = dx
        s1_ref[...] += jnp.sum(dhv, axis=0, keepdims=True)
        s2_ref[...] += jnp.sum(dhv * n, axis=0, keepdims=True)

    blk = pl.BlockSpec((tr, Dn), lambda i: (i, 0))
    vec = pl.BlockSpec((1, Dn), lambda i: (0, 0))
    in_specs, args = [blk, vec, vec, blk], [x, _row(g), _row(sc), dh]
    if has_res:
        in_specs.append(blk)
        args.append(dres)
    vshape = jax.ShapeDtypeStruct((1, Dn), F32)
    return pl.pallas_call(
        body, grid=(S_ // tr,), in_specs=in_specs, out_specs=[blk, vec, vec],
        out_shape=[jax.ShapeDtypeStruct((S_, Dn), F32), vshape, vshape],
        compiler_params=_cparams("arbitrary"), name=name,
    )(*args)


def gate_bwd(dx, y, colscale, name, tr=512):
    S_, Dn = dx.shape
    tr = _pick(S_, tr, 8)

    def body(dx_ref, y_ref, cs_ref, d_ref, a_ref, c_ref):
        i = pl.program_id(0)

        @pl.when(i == 0)
        def _():
            a_ref[...] = jnp.zeros_like(a_ref)
            c_ref[...] = jnp.zeros_like(c_ref)

        dxv = dx_ref[...]
        d_ref[...] = (dxv * cs_ref[...]).astype(d_ref.dtype)
        a_ref[...] += jnp.sum(dxv * y_ref[...].astype(F32), axis=0, keepdims=True)
        c_ref[...] += jnp.sum(dxv, axis=0, keepdims=True)

    blk = pl.BlockSpec((tr, Dn), lambda i: (i, 0))
    vec = pl.BlockSpec((1, Dn), lambda i: (0, 0))
    vshape = jax.ShapeDtypeStruct((1, Dn), F32)
    return pl.pallas_call(
        body, grid=(S_ // tr,), in_specs=[blk, blk, vec], out_specs=[blk, vec, vec],
        out_shape=[jax.ShapeDtypeStruct((S_, Dn), BF16), vshape, vshape],
        compiler_params=_cparams("arbitrary"), name=name,
    )(dx, y, _row(colscale))


def loss_head(x, g, target, name, tr=512):
    S_, Dn = x.shape
    tr = _pick(S_, tr, 8)

    def body(x_ref, g_ref, t_ref, dx_ref, dg_ref, loss_ref):
        i = pl.program_id(0)

        @pl.when(i == 0)
        def _():
            dg_ref[...] = jnp.zeros_like(dg_ref)
            loss_ref[...] = jnp.zeros_like(loss_ref)

        xv = x_ref[...]
        r = lax.rsqrt(jnp.mean(xv * xv, axis=-1, keepdims=True) + EPS)
        n = xv * r
        e = n * g_ref[...] - t_ref[...]
        loss_ref[...] += 0.5 * jnp.sum(jnp.mean(e * e, axis=-1, keepdims=True), axis=0, keepdims=True)
        dy = e * (1.0 / Dn)
        dg_ref[...] += jnp.sum(dy * n, axis=0, keepdims=True)
        dn = dy * g_ref[...]
        dx_ref[...] = r * (dn - n * jnp.mean(dn * n, axis=-1, keepdims=True))

    blk = pl.BlockSpec((tr, Dn), lambda i: (i, 0))
    vec = pl.BlockSpec((1, Dn), lambda i: (0, 0))
    one = pl.BlockSpec((1, 1), lambda i: (0, 0))
    return pl.pallas_call(
        body, grid=(S_ // tr,), in_specs=[blk, vec, blk], out_specs=[blk, vec, one],
        out_shape=[jax.ShapeDtypeStruct((S_, Dn), F32), jax.ShapeDtypeStruct((1, Dn), F32),
                   jax.ShapeDtypeStruct((1, 1), F32)],
        compiler_params=_cparams("arbitrary"), name=name,
    )(x, _row(g), target)


POOL_HALO = 16
POOL_CHUNK = 512


def _rows(ref, lo, hi, n_rows):
    parts = []
    if lo < 0:
        parts.append(jnp.zeros((-lo, ref.shape[1]), F32))
    parts.append(ref[max(lo, 0):min(hi, n_rows), :].astype(F32))
    if hi > n_rows:
        parts.append(jnp.zeros((hi - n_rows, ref.shape[1]), F32))
    return parts[0] if len(parts) == 1 else jnp.concatenate(parts, axis=0)


def _window_sum(e, w, back):
    n = e.shape[0]
    s, width = e, 1
    while width < w:
        s = s + pltpu.roll(s, width if back else n - width, 0)
        width *= 2
    return s


def _pool_call(h, out_dtype, name, backward):
    S_, Dn = h.shape
    C = Dn // N_GROUPS
    ch = _pick(S_, POOL_CHUNK, 8)

    def body(h_ref, o_ref):
        g = pl.program_id(0)
        for gi, w in enumerate(POOL_WINDOWS):
            @pl.when(g == gi)
            def _(w=w):
                for r0 in range(0, S_, ch):
                    t = (r0 + lax.broadcasted_iota(jnp.int32, (ch, C), 0)).astype(F32)
                    cnt = jnp.minimum(t + 1.0, float(w))
                    if not backward:
                        ext = _rows(h_ref, r0 - POOL_HALO, r0 + ch, S_)
                        cur = ext[POOL_HALO:]
                        mean = _window_sum(ext, w, True)[POOL_HALO:] / cnt
                        o_ref[r0:r0 + ch, :] = (mean - cur).astype(o_ref.dtype)
                    else:
                        ext = _rows(h_ref, r0, r0 + ch + POOL_HALO, S_)
                        text = (r0 + lax.broadcasted_iota(jnp.int32, (ch + POOL_HALO, C), 0)).astype(F32)
                        e = ext / jnp.minimum(text + 1.0, float(w))
                        o_ref[r0:r0 + ch, :] = (_window_sum(e, w, False)[:ch] - ext[:ch]).astype(o_ref.dtype)

    blk = pl.BlockSpec((S_, C), lambda g: (0, g))
    return pl.pallas_call(
        body, grid=(N_GROUPS,), in_specs=[blk], out_specs=blk,
        out_shape=jax.ShapeDtypeStruct((S_, Dn), out_dtype),
        compiler_params=_cparams("parallel"), name=name,
    )(h)


GLU_CHUNK = 512
GLU_HALO = 16
_SQRT_HALF = 0.7071067811865476
_INV_SQRT_2PI = 0.3989422804014327


def _gelu(a):
    return 0.5 * a * (1.0 + lax.erf(a * _SQRT_HALF))


def _gelu_grad(a):
    return 0.5 * (1.0 + lax.erf(a * _SQRT_HALF)) + a * (_INV_SQRT_2PI * jnp.exp(-0.5 * a * a))


def glu_fwd(u, conv_w, conv_b, name):
    S_, F2 = u.shape
    Fh = F2 // 2
    tf = GLU_TILE
    nt = Fh // tf
    ch = _pick(S_, GLU_CHUNK, GLU_HALO)

    def body(a_ref, v_ref, cw_ref, cb_ref, z_ref):
        cw0, cw1, cw2 = cw_ref[0:1, :], cw_ref[1:2, :], cw_ref[2:3, :]
        cb = cb_ref[...]
        for r0 in range(0, S_, ch):
            ext = _rows(a_ref, r0 - GLU_HALO, r0 + ch, S_)
            a0 = ext[GLU_HALO:]
            a1 = pltpu.roll(ext, 1, 0)[GLU_HALO:]
            a2 = pltpu.roll(ext, 2, 0)[GLU_HALO:]
            ac = a2 * cw0 + a1 * cw1 + a0 * cw2 + cb
            z_ref[r0:r0 + ch, :] = (_gelu(ac) * v_ref[r0:r0 + ch, :].astype(F32)).astype(z_ref.dtype)

    return pl.pallas_call(
        body, grid=(nt,),
        in_specs=[pl.BlockSpec((S_, tf), lambda j: (0, j)), pl.BlockSpec((S_, tf), lambda j: (0, j + nt)),
                  pl.BlockSpec((3, tf), lambda j: (0, j)), pl.BlockSpec((1, tf), lambda j: (0, j))],
        out_specs=pl.BlockSpec((S_, tf), lambda j: (0, j)),
        out_shape=jax.ShapeDtypeStruct((S_, Fh), BF16),
        compiler_params=_cparams("parallel"), name=name,
    )(u, u, conv_w, _row(conv_b))


def glu_bwd(u, dz, conv_w, conv_b, name):
    S_, F2 = u.shape
    Fh = F2 // 2
    tf = GLU_TILE
    nt = Fh // tf
    ch = _pick(S_, GLU_CHUNK, GLU_HALO)

    def body(a_ref, v_ref, dz_ref, cw_ref, cb_ref, du_ref, dcw_ref, dcb_ref, da_buf, dv_buf, sems):
        j = pl.program_id(0)
        slot = j % 2

        def writes(step, sl):
            lo = pl.multiple_of(step * tf, tf)
            return (pltpu.make_async_copy(da_buf.at[sl], du_ref.at[:, pl.ds(lo, tf)], sems.at[sl, 0]),
                    pltpu.make_async_copy(dv_buf.at[sl], du_ref.at[:, pl.ds(Fh + lo, tf)], sems.at[sl, 1]))

        @pl.when(j >= 2)
        def _():
            for cp in writes(j - 2, slot):
                cp.wait()

        cw0, cw1, cw2 = cw_ref[0:1, :], cw_ref[1:2, :], cw_ref[2:3, :]
        cb = cb_ref[...]
        acc = [jnp.zeros((1, tf), F32) for _ in range(4)]
        n = ch + GLU_HALO
        for r0 in range(0, S_, ch):
            ext = _rows(a_ref, r0 - GLU_HALO, r0 + n, S_)
            a0 = ext[GLU_HALO:]
            a1 = pltpu.roll(ext, 1, 0)[GLU_HALO:]
            a2 = pltpu.roll(ext, 2, 0)[GLU_HALO:]
            ac = a2 * cw0 + a1 * cw1 + a0 * cw2 + cb
            vv = _rows(v_ref, r0, r0 + n, S_)
            dzv = _rows(dz_ref, r0, r0 + n, S_)
            gl = _gelu(ac)
            dac = dzv * vv * _gelu_grad(ac)
            da = (dac * cw2 + pltpu.roll(dac, n - 1, 0) * cw1 + pltpu.roll(dac, n - 2, 0) * cw0)[:ch]
            da_buf[slot, r0:r0 + ch, :] = da.astype(da_buf.dtype)
            dv_buf[slot, r0:r0 + ch, :] = (dzv[:ch] * gl[:ch]).astype(dv_buf.dtype)
            dc = dac[:ch]
            acc[0] = acc[0] + jnp.sum(dc * a2[:ch], axis=0, keepdims=True)
            acc[1] = acc[1] + jnp.sum(dc * a1[:ch], axis=0, keepdims=True)
            acc[2] = acc[2] + jnp.sum(dc * a0[:ch], axis=0, keepdims=True)
            acc[3] = acc[3] + jnp.sum(dc, axis=0, keepdims=True)
        dcw_ref[0:1, :] = acc[0]
        dcw_ref[1:2, :] = acc[1]
        dcw_ref[2:3, :] = acc[2]
        dcb_ref[...] = acc[3]
        for cp in writes(j, slot):
            cp.start()

        @pl.when(j == nt - 1)
        def _():
            for cp in writes(j, slot):
                cp.wait()
            if nt > 1:
                for cp in writes(j - 1, 1 - slot):
                    cp.wait()

    return pl.pallas_call(
        body, grid=(nt,),
        in_specs=[pl.BlockSpec((S_, tf), lambda j: (0, j)), pl.BlockSpec((S_, tf), lambda j: (0, j + nt)),
                  pl.BlockSpec((S_, tf), lambda j: (0, j)),
                  pl.BlockSpec((3, tf), lambda j: (0, j)), pl.BlockSpec((1, tf), lambda j: (0, j))],
        out_specs=[_ANY, pl.BlockSpec((3, tf), lambda j: (0, j)), pl.BlockSpec((1, tf), lambda j: (0, j))],
        out_shape=[jax.ShapeDtypeStruct((S_, F2), BF16), jax.ShapeDtypeStruct((3, Fh), F32),
                   jax.ShapeDtypeStruct((1, Fh), F32)],
        scratch_shapes=[pltpu.VMEM((2, S_, tf), BF16), pltpu.VMEM((2, S_, tf), BF16), pltpu.SemaphoreType.DMA((2, 2))],
        compiler_params=_cparams("arbitrary"), name=name,
    )(u, u, dz, conv_w, _row(conv_b))


def rope_tables(pos, inv, name, tr=512):
    S_ = pos.shape[0]
    tr = _pick(S_, tr, 8)

    def body(p_ref, inv_ref, c_ref, s1_ref, s2_ref):
        ang = p_ref[...] * inv_ref[...]
        lane = lax.broadcasted_iota(jnp.int32, ang.shape, 1)
        half = QK_ROPE // 2
        cosv, sinv = jnp.cos(ang), jnp.sin(ang)
        c_ref[...] = jnp.where(lane < QK_ROPE, cosv, 0.0)
        s1_ref[...] = jnp.where(lane < half, -sinv, 0.0)
        s2_ref[...] = jnp.where((lane >= half) & (lane < QK_ROPE), sinv, 0.0)

    blk = pl.BlockSpec((tr, LANES), lambda i: (i, 0))
    shp = jax.ShapeDtypeStruct((S_, LANES), F32)
    return pl.pallas_call(
        body, grid=(S_ // tr,),
        in_specs=[pl.BlockSpec((tr, 1), lambda i: (i, 0)), pl.BlockSpec((1, LANES), lambda i: (0, 0))],
        out_specs=[blk, blk, blk], out_shape=[shp, shp, shp],
        compiler_params=_cparams("parallel"), name=name,
    )(pos, inv)


_HALF = QK_ROPE // 2


def _rope(t, c, s1, s2):
    return t * c + pltpu.roll(t, LANES - _HALF, 1) * s1 + pltpu.roll(t, _HALF, 1) * s2


def _rope_t(d, c, s1, s2):
    return d * c + pltpu.roll(d * s1, _HALF, 1) + pltpu.roll(d * s2, LANES - _HALF, 1)


def q_prep(q, tabs, scale, backward, name, tr=512):
    S_, W = q.shape
    tr = _pick(S_, tr, 8)

    def body(q_ref, c_ref, s1_ref, s2_ref, o_ref):
        o_ref[:, 0:LANES] = (q_ref[:, 0:LANES].astype(F32) * scale).astype(o_ref.dtype)
        t = q_ref[:, LANES:2 * LANES].astype(F32)
        fn = _rope_t if backward else _rope
        o_ref[:, LANES:2 * LANES] = (fn(t, c_ref[...], s1_ref[...], s2_ref[...]) * scale).astype(o_ref.dtype)

    blk = pl.BlockSpec((tr, HEAD_PAD), lambda i, h: (i, h))
    tab = pl.BlockSpec((tr, LANES), lambda i, h: (i, 0))
    return pl.pallas_call(
        body, grid=(S_ // tr, W // HEAD_PAD), in_specs=[blk, tab, tab, tab], out_specs=blk,
        out_shape=jax.ShapeDtypeStruct((S_, W), BF16),
        compiler_params=_cparams("parallel", "parallel"), name=name,
    )(q, *tabs)


def k_prep(knv, kv_ext, tabs, name, tr=512):
    S_ = knv.shape[0]
    tr = _pick(S_, tr, 8)

    def body(kn_ref, v_ref, t_ref, c_ref, s1_ref, s2_ref, o_ref, vx_ref):
        o_ref[:, 0:LANES] = kn_ref[...].astype(o_ref.dtype)
        o_ref[:, LANES:2 * LANES] = _rope(t_ref[...], c_ref[...], s1_ref[...], s2_ref[...]).astype(o_ref.dtype)
        vx_ref[:, 0:V_HEAD] = v_ref[...].astype(vx_ref.dtype)
        vx_ref[:, V_HEAD:HEAD_PAD] = jnp.ones((tr, HEAD_PAD - V_HEAD), vx_ref.dtype)

    tab = pl.BlockSpec((tr, LANES), lambda i, h: (i, 0))
    head = pl.BlockSpec((tr, HEAD_PAD), lambda i, h: (i, h))
    shp = jax.ShapeDtypeStruct((S_, N_HEADS * HEAD_PAD), BF16)
    return pl.pallas_call(
        body, grid=(S_ // tr, N_HEADS),
        in_specs=[pl.BlockSpec((tr, LANES), lambda i, h: (i, h)),
                  pl.BlockSpec((tr, V_HEAD), lambda i, h: (i, N_HEADS + h)),
                  pl.BlockSpec((tr, LANES), lambda i, h: (i, KV_RANK // LANES)), tab, tab, tab],
        out_specs=[head, head], out_shape=[shp, shp],
        compiler_params=_cparams("parallel", "parallel"), name=name,
    )(knv, knv, kv_ext, *tabs)


def k_prep_bwd(dk_a, dk_b, dv_a, dv_b, tabs, name, tr=256):
    S_ = dk_a.shape[0]
    tr = _pick(S_, tr, 8)
    HV = N_HEADS * V_HEAD

    def body(ka_ref, kb_ref, va_ref, vb_ref, c_ref, s1_ref, s2_ref, o_ref, t_ref):
        dr = jnp.zeros((tr, LANES), F32)
        for h in range(N_HEADS):
            lo = h * HEAD_PAD
            o_ref[:, h * LANES:(h + 1) * LANES] = (ka_ref[:, lo:lo + LANES] + kb_ref[:, lo:lo + LANES]).astype(o_ref.dtype)
            dr = dr + ka_ref[:, lo + LANES:lo + 2 * LANES] + kb_ref[:, lo + LANES:lo + 2 * LANES]
        o_ref[:, HV:2 * HV] = (va_ref[...] + vb_ref[...]).astype(o_ref.dtype)
        t_ref[...] = _rope_t(dr, c_ref[...], s1_ref[...], s2_ref[...])

    kblk = pl.BlockSpec((tr, N_HEADS * HEAD_PAD), lambda i: (i, 0))
    vblk = pl.BlockSpec((tr, HV), lambda i: (i, 0))
    tab = pl.BlockSpec((tr, LANES), lambda i: (i, 0))
    return pl.pallas_call(
        body, grid=(S_ // tr,), in_specs=[kblk, kblk, vblk, vblk, tab, tab, tab],
        out_specs=[pl.BlockSpec((tr, 2 * HV), lambda i: (i, 0)), tab],
        out_shape=[jax.ShapeDtypeStruct((S_, 2 * HV), BF16), jax.ShapeDtypeStruct((S_, LANES), F32)],
        compiler_params=_cparams("parallel"), name=name,
    )(dk_a, dk_b, dv_a, dv_b, *tabs)


_NEG = -1e30


def attn_fwd(q, k, vx, name):
    S_ = q.shape[0]
    TQ = _pick(S_, ATT_Q_BLOCK, 8)
    TK = _pick(S_, ATT_K_BLOCK, TQ)
    HP = ATT_HEADS_PER_STEP
    W = HP * HEAD_PAD
    ratio = TK // TQ

    def body(q_ref, k_ref, v_ref, o_ref, lse_ref):
        i = pl.program_id(1)
        qs = [q_ref[:, h * HEAD_PAD:(h + 1) * HEAD_PAD] for h in range(HP)]

        def step(j, carry, masked):
            start = pl.multiple_of(j * TK, TK)
            out = []
            for h in range(HP):
                m, acc = carry[h]
                cols = slice(h * HEAD_PAD, (h + 1) * HEAD_PAD)
                s = lax.dot_general(qs[h], k_ref[pl.ds(start, TK), cols], _DIMS["nt"], preferred_element_type=F32)
                if masked:
                    rowi = i * TQ + lax.broadcasted_iota(jnp.int32, (TQ, TK), 0)
                    coli = j * TK + lax.broadcasted_iota(jnp.int32, (TQ, TK), 1)
                    s = jnp.where(coli <= rowi, s, _NEG)
                m_new = jnp.maximum(m, jnp.max(s, axis=-1, keepdims=True))
                alpha = jnp.exp(m - m_new)
                p = jnp.exp(s - m_new).astype(BF16)
                acc = alpha * acc + lax.dot_general(p, v_ref[pl.ds(start, TK), cols], _DIMS["nn"],
                                                    preferred_element_type=F32)
                out.append((m_new, acc))
            return tuple(out)

        init = tuple((jnp.full((TQ, 1), _NEG, F32), jnp.zeros((TQ, HEAD_PAD), F32)) for _ in range(HP))
        last = i // ratio
        carry = step(last, lax.fori_loop(0, last, functools.partial(step, masked=False), init), True)
        for h in range(HP):
            m, acc = carry[h]
            l = acc[:, V_HEAD:]
            o_ref[:, h * V_HEAD:(h + 1) * V_HEAD] = (acc[:, :V_HEAD] / l).astype(o_ref.dtype)
            lse_ref[h] = m + jnp.log(jnp.max(l, axis=-1, keepdims=True))

    return pl.pallas_call(
        body, grid=(N_HEADS // HP, S_ // TQ),
        in_specs=[pl.BlockSpec((TQ, W), lambda g, i: (i, g)),
                  pl.BlockSpec((S_, W), lambda g, i: (0, g)),
                  pl.BlockSpec((S_, W), lambda g, i: (0, g))],
        out_specs=[pl.BlockSpec((TQ, HP * V_HEAD), lambda g, i: (i, g)),
                   pl.BlockSpec((HP, TQ, 1), lambda g, i: (g, i, 0))],
        out_shape=[jax.ShapeDtypeStruct((S_, N_HEADS * V_HEAD), BF16), jax.ShapeDtypeStruct((N_HEADS, S_, 1), F32)],
        compiler_params=_cparams("parallel", "parallel"), name=name,
    )(q, k, vx)


def attn_delta(o, do, name, tr=512):
    S_ = o.shape[0]
    tr = _pick(S_, tr, 8)

    def body(o_ref, do_ref, d_ref):
        d_ref[...] = jnp.sum(o_ref[...].astype(F32) * do_ref[...].astype(F32), axis=-1, keepdims=True)

    blk = pl.BlockSpec((tr, V_HEAD), lambda i, h: (i, h))
    return pl.pallas_call(
        body, grid=(S_ // tr, N_HEADS), in_specs=[blk, blk],
        out_specs=pl.BlockSpec((None, tr, 1), lambda i, h: (h, i, 0)),
        out_shape=jax.ShapeDtypeStruct((N_HEADS, S_, 1), F32),
        compiler_params=_cparams("parallel", "parallel"), name=name,
    )(o, do)


def attn_bwd(q, k, vx, do, lse_row, delta_row, name):
    S_ = q.shape[0]
    TK = _pick(S_, ATT_BWD_K_BLOCK, LANES)
    TQ = _pick(S_, ATT_BWD_Q_BLOCK, TK)
    HP = ATT_HEADS_PER_STEP
    W = HP * HEAD_PAD
    ratio = TQ // TK
    nq = S_ // TQ

    def body(q_ref, do_ref, lse_ref, dl_ref, k_ref, v_ref, dq_ref, dk_ref, dv_ref):
        j = pl.program_id(1)

        @pl.when(j == 0)
        def _():
            dq_ref[...] = jnp.zeros_like(dq_ref)

        ks = [k_ref[:, h * HEAD_PAD:(h + 1) * HEAD_PAD] for h in range(HP)]
        vs = [v_ref[:, h * HEAD_PAD:h * HEAD_PAD + V_HEAD] for h in range(HP)]

        def step(i, carry, masked):
            start = pl.multiple_of(i * TQ, TQ)
            out = []
            for h in range(HP):
                dk, dv = carry[h]
                cols = slice(h * HEAD_PAD, (h + 1) * HEAD_PAD)
                qv = q_ref[pl.ds(start, TQ), cols]
                dov = do_ref[pl.ds(start, TQ), h * V_HEAD:(h + 1) * V_HEAD]
                st = lax.dot_general(ks[h], qv, _DIMS["nt"], preferred_element_type=F32)
                pt = jnp.exp(st - lse_ref[h, :, pl.ds(start, TQ)])
                if masked:
                    keyi = j * TK + lax.broadcasted_iota(jnp.int32, (TK, TQ), 0)
                    qryi = i * TQ + lax.broadcasted_iota(jnp.int32, (TK, TQ), 1)
                    pt = jnp.where(keyi <= qryi, pt, 0.0)
                dpt = lax.dot_general(vs[h], dov, _DIMS["nt"], preferred_element_type=F32)
                dst = (pt * (dpt - dl_ref[h, :, pl.ds(start, TQ)])).astype(BF16)
                dv = dv + lax.dot_general(pt.astype(BF16), dov, _DIMS["nn"], preferred_element_type=F32)
                dk = dk + lax.dot_general(dst, qv, _DIMS["nn"], preferred_element_type=F32)
                dq_ref[pl.ds(start, TQ), cols] += lax.dot_general(dst, ks[h], _DIMS["tn"], preferred_element_type=F32)
                out.append((dk, dv))
            return tuple(out)

        init = tuple((jnp.zeros((TK, HEAD_PAD), F32), jnp.zeros((TK, V_HEAD), F32)) for _ in range(HP))
        first = j // ratio
        carry = lax.fori_loop(first + 1, nq, functools.partial(step, masked=False), step(first, init, True))
        for h in range(HP):
            dk_ref[:, h * HEAD_PAD:(h + 1) * HEAD_PAD] = carry[h][0]
            dv_ref[:, h * V_HEAD:(h + 1) * V_HEAD] = carry[h][1]

    return pl.pallas_call(
        body, grid=(N_HEADS // HP, S_ // TK),
        in_specs=[pl.BlockSpec((S_, W), lambda g, j: (0, g)),
                  pl.BlockSpec((S_, HP * V_HEAD), lambda g, j: (0, g)),
                  pl.BlockSpec((HP, 1, S_), lambda g, j: (g, 0, 0)),
                  pl.BlockSpec((HP, 1, S_), lambda g, j: (g, 0, 0)),
                  pl.BlockSpec((TK, W), lambda g, j: (j, g)),
                  pl.BlockSpec((TK, W), lambda g, j: (j, g))],
        out_specs=[pl.BlockSpec((S_, W), lambda g, j: (0, g)),
                   pl.BlockSpec((TK, W), lambda g, j: (j, g)),
                   pl.BlockSpec((TK, HP * V_HEAD), lambda g, j: (j, g))],
        out_shape=[jax.ShapeDtypeStruct((S_, N_HEADS * HEAD_PAD), F32),
                   jax.ShapeDtypeStruct((S_, N_HEADS * HEAD_PAD), F32),
                   jax.ShapeDtypeStruct((S_, N_HEADS * V_HEAD), F32)],
        compiler_params=_cparams("parallel", "arbitrary"), name=name,
    )(q, do, lse_row, delta_row, k, vx)


def mods_fwd(c_all, mod_w, mod_b, name, tn=512):
    L, Dn, E = mod_w.shape
    R = c_all.shape[0]
    tn = _pick(E, tn, LANES)

    def body(c_ref, w_ref, b_ref, o_ref):
        cv = c_ref[...]
        sc = (cv / (1.0 + jnp.exp(-cv))).astype(BF16)
        o_ref[...] = lax.dot_general(sc, w_ref[...].astype(BF16), _DIMS["nn"], preferred_element_type=F32) + b_ref[...]

    return pl.pallas_call(
        body, grid=(L, E // tn),
        in_specs=[pl.BlockSpec((R, Dn), lambda l, j: (0, 0)), pl.BlockSpec((None, Dn, tn), lambda l, j: (l, 0, j)),
                  pl.BlockSpec((None, 1, tn), lambda l, j: (l, 0, j))],
        out_specs=pl.BlockSpec((None, R, tn), lambda l, j: (l, 0, j)),
        out_shape=jax.ShapeDtypeStruct((L, R, E), F32),
        compiler_params=_cparams("parallel", "parallel"), name=name,
    )(c_all, mod_w, mod_b.reshape(L, 1, E))


def _adam_math(w, g, m, v):
    m = ADAM_B1 * m + (1.0 - ADAM_B1) * g
    v = ADAM_B2 * v + (1.0 - ADAM_B2) * (g * g)
    m_hat = m / (1.0 - ADAM_B1 ** ADAM_STEP)
    v_hat = v / (1.0 - ADAM_B2 ** ADAM_STEP)
    delta = -ADAM_LR * (m_hat / (jnp.sqrt(v_hat) + ADAM_EPS) + ADAM_WD * w)
    return delta, m, v


def _as2d(a):
    return a.reshape(-1, a.shape[-1]) if a.ndim != 2 else a


def adamw(w, g, m, v, name):
    shape = w.shape
    w2, g2, m2, v2 = _as2d(w), _as2d(g), _as2d(m), _as2d(v)
    R, C = w2.shape
    tr = _pick(R, max(8, (1 << 18) // C // 8 * 8), 8)

    def body(w_ref, g_ref, m_ref, v_ref, d_ref, mo_ref, vo_ref):
        d, mn, vn = _adam_math(w_ref[...], g_ref[...], m_ref[...], v_ref[...])
        d_ref[...] = d
        mo_ref[...] = mn
        vo_ref[...] = vn

    blk = pl.BlockSpec((tr, C), lambda i: (i, 0))
    shp = jax.ShapeDtypeStruct((R, C), F32)
    outs = pl.pallas_call(
        body, grid=(R // tr,), in_specs=[blk] * 4, out_specs=[blk] * 3, out_shape=[shp] * 3,
        compiler_params=_cparams("parallel"), name=name,
    )(w2, g2, m2, v2)
    return tuple(o.reshape(shape) for o in outs)


def adamw_sum(parts, w, m, v, name):
    P, R, C = parts.shape

    def body(p_ref, w_ref, m_ref, v_ref, g_ref, d_ref, mo_ref, vo_ref):
        g = p_ref[0]
        for k in range(1, P):
            g = g + p_ref[k]
        d, mn, vn = _adam_math(w_ref[...], g, m_ref[...], v_ref[...])
        g_ref[...] = g
        d_ref[...] = d
        mo_ref[...] = mn
        vo_ref[...] = vn

    shp = jax.ShapeDtypeStruct((R, C), F32)
    return pl.pallas_call(body, out_shape=[shp] * 4, compiler_params=_cparams(), name=name)(parts, w, m, v)


def adamw_modw(c_col, dm, w, m, v, name, tr=256, tn=512):
    L, Dn, E = w.shape
    B = c_col.shape[0]
    tr = _pick(Dn, tr, 8)
    tn = _pick(E, tn, LANES)

    def body(c_ref, dm_ref, w_ref, m_ref, v_ref, g_ref, d_ref, mo_ref, vo_ref):
        g = jnp.zeros((tr, tn), F32)
        for b in range(B):
            cv = c_ref[b]
            g = g + (cv / (1.0 + jnp.exp(-cv))) * dm_ref[b:b + 1, :]
        d, mn, vn = _adam_math(w_ref[...], g, m_ref[...], v_ref[...])
        g_ref[...] = g
        d_ref[...] = d
        mo_ref[...] = mn
        vo_ref[...] = vn

    blk = pl.BlockSpec((None, tr, tn), lambda l, i, j: (l, i, j))
    shp = jax.ShapeDtypeStruct((L, Dn, E), F32)
    return pl.pallas_call(
        body, grid=(L, Dn // tr, E // tn),
        in_specs=[pl.BlockSpec((B, tr, 1), lambda l, i, j: (0, i, 0)),
                  pl.BlockSpec((None, B, tn), lambda l, i, j: (l, 0, j)), blk, blk, blk],
        out_specs=[blk] * 4, out_shape=[shp] * 4,
        compiler_params=_cparams("parallel", "parallel", "parallel"), name=name,
    )(c_col, dm, w, m, v)


def add_round(a, b, name, tr=512):
    R, C = a.shape
    tr = _pick(R, tr, 16)

    def body(a_ref, b_ref, o_ref):
        o_ref[...] = (a_ref[...] + b_ref[...].astype(F32)).astype(BF16)

    blk = pl.BlockSpec((tr, C), lambda i: (i, 0))
    return pl.pallas_call(
        body, grid=(R // tr,), in_specs=[blk, blk], out_specs=blk, out_shape=jax.ShapeDtypeStruct((R, C), BF16),
        compiler_params=_cparams("parallel"), name=name,
    )(a, b)


def sum_parts(parts, name, tr=512):
    P, R, C = parts.shape
    tr = _pick(R, tr, 16)

    def body(p_ref, o_ref):
        s = p_ref[0].astype(F32)
        for k in range(1, P):
            s = s + p_ref[k].astype(F32)
        o_ref[...] = s

    return pl.pallas_call(
        body, grid=(R // tr,), in_specs=[pl.BlockSpec((P, tr, C), lambda i: (0, i, 0))],
        out_specs=pl.BlockSpec((tr, C), lambda i: (i, 0)), out_shape=jax.ShapeDtypeStruct((R, C), F32),
        compiler_params=_cparams("parallel"), name=name,
    )(parts)


_ANY = pl.BlockSpec(memory_space=pl.ANY)


def _place():
    return lax.axis_index("x"), lax.axis_index("y"), lax.axis_index("c")


def _flip(v, bit):
    return 1 - v if bit else v


def chip_gather(buf, name):
    def body(in_ref, out_ref, send_sems, recv_sems):
        x, y, c = _place()
        me = 2 * x + y
        sends = []
        for k in range(1, N_CHIPS):
            px, py = _flip(x, k >> 1), _flip(y, k & 1)
            cp = pltpu.make_async_remote_copy(src_ref=in_ref, dst_ref=out_ref.at[me], send_sem=send_sems.at[k - 1],
                                              recv_sem=recv_sems.at[k - 1], device_id=(px, py, c), device_id_type=MESH)
            cp.start()
            sends.append(cp)
        for k in range(1, N_CHIPS):
            px, py = _flip(x, k >> 1), _flip(y, k & 1)
            pltpu.make_async_remote_copy(src_ref=in_ref, dst_ref=out_ref.at[2 * px + py], send_sem=send_sems.at[k - 1],
                                         recv_sem=recv_sems.at[k - 1], device_id=(px, py, c),
                                         device_id_type=MESH).wait_recv()
        for cp in sends:
            cp.wait_send()

    out = pl.pallas_call(
        body, in_specs=[_ANY], out_specs=_ANY,
        out_shape=jax.ShapeDtypeStruct((N_CHIPS,) + buf.shape, buf.dtype),
        scratch_shapes=[pltpu.SemaphoreType.DMA((N_CHIPS - 1,)), pltpu.SemaphoreType.DMA((N_CHIPS - 1,))],
        name=name,
    )(buf)
    return lax.dynamic_update_index_in_dim(out, buf, 2 * lax.axis_index("x") + lax.axis_index("y"), 0)


def chip_all_to_all(buf, name):
    def body(in_ref, out_ref, send_sems, recv_sems):
        x, y, c = _place()
        me = 2 * x + y
        sends = []
        for k in range(1, N_CHIPS):
            px, py = _flip(x, k >> 1), _flip(y, k & 1)
            cp = pltpu.make_async_remote_copy(src_ref=in_ref.at[2 * px + py], dst_ref=out_ref.at[me],
                                              send_sem=send_sems.at[k - 1], recv_sem=recv_sems.at[k - 1],
                                              device_id=(px, py, c), device_id_type=MESH)
            cp.start()
            sends.append(cp)
        for k in range(1, N_CHIPS):
            px, py = _flip(x, k >> 1), _flip(y, k & 1)
            pltpu.make_async_remote_copy(src_ref=in_ref.at[me], dst_ref=out_ref.at[2 * px + py],
                                         send_sem=send_sems.at[k - 1], recv_sem=recv_sems.at[k - 1],
                                         device_id=(px, py, c), device_id_type=MESH).wait_recv()
        for cp in sends:
            cp.wait_send()

    out = pl.pallas_call(
        body, in_specs=[_ANY], out_specs=_ANY, out_shape=jax.ShapeDtypeStruct(buf.shape, buf.dtype),
        scratch_shapes=[pltpu.SemaphoreType.DMA((N_CHIPS - 1,)), pltpu.SemaphoreType.DMA((N_CHIPS - 1,))],
        name=name,
    )(buf)
    me = 2 * lax.axis_index("x") + lax.axis_index("y")
    return lax.dynamic_update_index_in_dim(out, _index(buf, me), me, 0)


def core_gather(buf, name):
    def body(in_ref, out_ref, send_sem, recv_sem):
        x, y, c = _place()
        cp = pltpu.make_async_remote_copy(src_ref=in_ref, dst_ref=out_ref.at[c], send_sem=send_sem, recv_sem=recv_sem,
                                          device_id=(x, y, 1 - c), device_id_type=MESH)
        cp.start()
        pltpu.make_async_remote_copy(src_ref=in_ref, dst_ref=out_ref.at[1 - c], send_sem=send_sem, recv_sem=recv_sem,
                                     device_id=(x, y, 1 - c), device_id_type=MESH).wait_recv()
        cp.wait_send()

    out = pl.pallas_call(
        body, in_specs=[_ANY], out_specs=_ANY, out_shape=jax.ShapeDtypeStruct((2,) + buf.shape, buf.dtype),
        scratch_shapes=[pltpu.SemaphoreType.DMA, pltpu.SemaphoreType.DMA],
        name=name,
    )(buf)
    return lax.dynamic_update_index_in_dim(out, buf, lax.axis_index("c"), 0)


def core_swap(buf, name):
    def body(in_ref, out_ref, send_sem, recv_sem):
        x, y, c = _place()
        cp = pltpu.make_async_remote_copy(src_ref=in_ref, dst_ref=out_ref, send_sem=send_sem, recv_sem=recv_sem,
                                          device_id=(x, y, 1 - c), device_id_type=MESH)
        cp.start()
        cp.wait()

    return pl.pallas_call(
        body, in_specs=[_ANY], out_specs=_ANY, out_shape=jax.ShapeDtypeStruct(buf.shape, buf.dtype),
        scratch_shapes=[pltpu.SemaphoreType.DMA, pltpu.SemaphoreType.DMA],
        name=name,
    )(buf)


def device_gather(buf, name):
    def body(in_ref, out_ref, send_sems, recv_sems, local_sem):
        x, y, c = _place()
        me = 4 * x + 2 * y + c
        mine = pltpu.make_async_copy(in_ref, out_ref.at[me], local_sem)
        mine.start()
        sends = []
        for k in range(1, N_DEV):
            peer = (_flip(x, (k >> 2) & 1), _flip(y, (k >> 1) & 1), _flip(c, k & 1))
            cp = pltpu.make_async_remote_copy(src_ref=in_ref, dst_ref=out_ref.at[me], send_sem=send_sems.at[k - 1],
                                              recv_sem=recv_sems.at[k - 1], device_id=peer, device_id_type=MESH)
            cp.start()
            sends.append(cp)
        for k in range(1, N_DEV):
            peer = (_flip(x, (k >> 2) & 1), _flip(y, (k >> 1) & 1), _flip(c, k & 1))
            pltpu.make_async_remote_copy(src_ref=in_ref, dst_ref=out_ref.at[4 * peer[0] + 2 * peer[1] + peer[2]],
                                         send_sem=send_sems.at[k - 1], recv_sem=recv_sems.at[k - 1], device_id=peer,
                                         device_id_type=MESH).wait_recv()
        for cp in sends:
            cp.wait_send()
        mine.wait()

    return pl.pallas_call(
        body, in_specs=[_ANY], out_specs=_ANY, out_shape=jax.ShapeDtypeStruct((N_DEV,) + buf.shape, buf.dtype),
        scratch_shapes=[pltpu.SemaphoreType.DMA((N_DEV - 1,)), pltpu.SemaphoreType.DMA((N_DEV - 1,)),
                        pltpu.SemaphoreType.DMA],
        name=name,
    )(buf)


def _region(ref, chip_axis=None, chip=None, chip_size=None, half_axis=None, half=None, half_size=None):
    idx = [slice(None)] * len(ref.shape)
    if chip is not None:
        idx[chip_axis] = pl.ds(chip * chip_size, chip_size)
    if half is not None:
        idx[half_axis] = pl.ds(half * half_size, half_size)
    return ref.at[tuple(idx)]


def gather_weights(shards, axes, name, after=()):
    n = len(shards)

    def full_shape(t):
        shp = list(shards[t].shape)
        shp[axes[t][0]] *= N_CHIPS
        return tuple(shp)

    def body(*refs):
        ins, outs = refs[:n], refs[n + len(after):2 * n + len(after)]
        ici_send, ici_recv, d2d_send, d2d_recv, own_send, own_recv = refs[2 * n + len(after):]
        x, y, c = _place()
        me = 2 * x + y

        def part(t, ref, chip, half):
            ca, ha = axes[t]
            return _region(ref, ca, chip, ins[t].shape[ca], ha, half, ins[t].shape[ha] // 2)

        def own(t):
            return pltpu.make_async_remote_copy(src_ref=ins[t], dst_ref=part(t, outs[t], me, None),
                                                send_sem=own_send.at[t], recv_sem=own_recv.at[t],
                                                device_id=(x, y, 1 - c), device_id_type=MESH)

        started = []
        for t in range(n):
            own(t).start()
            started.append(own(t))
        for t in range(n):
            for k in range(1, N_CHIPS):
                px, py = _flip(x, k >> 1), _flip(y, k & 1)
                cp = pltpu.make_async_remote_copy(src_ref=part(t, ins[t], None, c), dst_ref=part(t, outs[t], me, c),
                                                  send_sem=ici_send.at[t, k - 1], recv_sem=ici_recv.at[t, k - 1],
                                                  device_id=(px, py, c), device_id_type=MESH)
                cp.start()
                started.append(cp)
        for t in range(n):
            for k in range(1, N_CHIPS):
                px, py = _flip(x, k >> 1), _flip(y, k & 1)
                got = part(t, outs[t], 2 * px + py, c)
                pltpu.make_async_remote_copy(src_ref=part(t, ins[t], None, c), dst_ref=got,
                                             send_sem=ici_send.at[t, k - 1], recv_sem=ici_recv.at[t, k - 1],
                                             device_id=(px, py, c), device_id_type=MESH).wait_recv()
                fw = pltpu.make_async_remote_copy(src_ref=got, dst_ref=got, send_sem=d2d_send.at[t, k - 1],
                                                  recv_sem=d2d_recv.at[t, k - 1], device_id=(x, y, 1 - c),
                                                  device_id_type=MESH)
                fw.start()
                started.append(fw)
        for t in range(n):
            for k in range(1, N_CHIPS):
                px, py = _flip(x, k >> 1), _flip(y, k & 1)
                theirs = part(t, outs[t], 2 * px + py, 1 - c)
                pltpu.make_async_remote_copy(src_ref=theirs, dst_ref=theirs, send_sem=d2d_send.at[t, k - 1],
                                             recv_sem=d2d_recv.at[t, k - 1], device_id=(x, y, 1 - c),
                                             device_id_type=MESH).wait_recv()
        for t in range(n):
            own(t).wait_recv()
        for cp in started:
            cp.wait_send()

    sem = pltpu.SemaphoreType.DMA((n, N_CHIPS - 1))
    own_sem = pltpu.SemaphoreType.DMA((n,))
    return pl.pallas_call(
        body, in_specs=[_ANY] * (n + len(after)), out_specs=[_ANY] * n,
        out_shape=[jax.ShapeDtypeStruct(full_shape(t), shards[t].dtype) for t in range(n)],
        scratch_shapes=[sem, sem, sem, sem, own_sem, own_sem], name=name,
    )(*shards, *after)


_HBM = pl.BlockSpec(memory_space=pltpu.HBM)
_SEM = pl.BlockSpec(memory_space=pltpu.SEMAPHORE)
_EFFECT = pltpu.SideEffectType.DATAFLOW_SIDE_EFFECTING
WEIGHT_COPIES = N_CHIPS


def _weight_peer(k, x, y, c):
    return (x, y, 1 - c) if k == 0 else (_flip(x, k >> 1), _flip(y, k & 1), c)


def weights_start(shards, items, name, after=()):
    n_sh, n_it = len(shards), len(items)

    def src_of(refs, i):
        t, layer, _ = items[i]
        return refs[t] if layer is None else refs[t].at[layer]

    def land_shape(i):
        t, layer, ca = items[i]
        shp = list(shards[t].shape if layer is None else shards[t].shape[1:])
        shp[ca] *= N_CHIPS
        return tuple(shp)

    def body(*refs):
        shard_refs, land_refs = refs[:n_sh], refs[n_sh:n_sh + n_it]
        first_out = n_sh + n_it + len(after)
        send_sems = refs[first_out:first_out + n_it]
        recv_sems = refs[first_out + n_it:first_out + 2 * n_it]
        token = refs[-1]
        x, y, c = _place()
        me = 2 * x + y
        for i in range(n_it):
            src = src_of(shard_refs, i)
            ca = items[i][2]
            dst = _region(land_refs[i], ca, me, src.shape[ca])
            for k in range(WEIGHT_COPIES):
                pltpu.make_async_remote_copy(src_ref=src, dst_ref=dst, send_sem=send_sems[i], recv_sem=recv_sems[i],
                                             device_id=_weight_peer(k, x, y, c), device_id_type=MESH).start()
        token[...] = jnp.zeros_like(token)

    lands = [pltpu.with_memory_space_constraint(lax.empty(land_shape(i), shards[0].dtype), pltpu.HBM)
             for i in range(n_it)]
    ins = [pltpu.with_memory_space_constraint(a, pltpu.HBM) for a in shards] + lands
    sems = (pltpu.SemaphoreType.DMA(()),) * (2 * n_it)
    outs = pl.pallas_call(
        body, name=name,
        out_shape=sems + tuple(pltpu.HBM(a.shape, a.dtype) for a in ins) + (jax.ShapeDtypeStruct((8, LANES), F32),),
        in_specs=[_HBM] * len(ins) + [_ANY] * len(after),
        out_specs=(_SEM,) * (2 * n_it) + (_HBM,) * len(ins) + (pl.BlockSpec(memory_space=pltpu.VMEM),),
        input_output_aliases={i: 2 * n_it + i for i in range(len(ins))},
        compiler_params=pltpu.CompilerParams(has_side_effects=_EFFECT),
    )(*ins, *after)
    base = 2 * n_it
    return (list(outs[:n_it]), list(outs[n_it:base]), list(outs[base:base + n_sh]),
            list(outs[base + n_sh:base + n_sh + n_it]), outs[-1])


def weights_wait(send_sems, recv_sems, lands, after, keep, name):
    m = len(lands)

    def body(*refs):
        land_refs, send_refs, recv_refs = refs[:m], refs[m:2 * m], refs[2 * m:3 * m]
        x, y, c = _place()
        for j in range(m):
            cp = pltpu.make_async_remote_copy(src_ref=land_refs[j], dst_ref=land_refs[j], send_sem=send_refs[j],
                                              recv_sem=recv_refs[j], device_id=(x, y, 1 - c),
                                              device_id_type=MESH)
            cp.wait_send()
            cp.wait_recv()

    outs = pl.pallas_call(
        body, name=name,
        out_shape=tuple(pltpu.HBM(a.shape, a.dtype) for a in lands),
        in_specs=[_HBM] * m + [_SEM] * (2 * m) + [_ANY] + [_HBM] * len(keep),
        out_specs=(_HBM,) * m,
        input_output_aliases={j: j for j in range(m)},
        compiler_params=pltpu.CompilerParams(has_side_effects=_EFFECT),
    )(*lands, *send_sems, *recv_sems, after, *keep)
    return list(outs)


def reduce_to_sibling(lo, hi, name):
    n = len(lo)

    def body(*refs):
        los, his, outs = refs[:n], refs[n:2 * n], refs[2 * n:3 * n]
        send_sems, recv_sems = refs[3 * n:]
        x, y, c = _place()

        def copy(u, src):
            return pltpu.make_async_remote_copy(src_ref=src, dst_ref=outs[u], send_sem=send_sems.at[u],
                                                recv_sem=recv_sems.at[u], device_id=(x, y, 1 - c), device_id_type=MESH)

        for u in range(n):
            @pl.when(c == 0)
            def _(u=u):
                copy(u, his[u]).start()

            @pl.when(c == 1)
            def _(u=u):
                copy(u, los[u]).start()
        for u in range(n):
            copy(u, los[u]).wait_recv()
        for u in range(n):
            copy(u, los[u]).wait_send()

    return pl.pallas_call(
        body, in_specs=[_ANY] * (2 * n), out_specs=[_ANY] * n,
        out_shape=[jax.ShapeDtypeStruct(a.shape, a.dtype) for a in lo],
        scratch_shapes=[pltpu.SemaphoreType.DMA((n,)), pltpu.SemaphoreType.DMA((n,))], name=name,
    )(*lo, *hi)


def add_selected(lo, hi, other, name, tile_elems=1 << 19):
    R, C = lo.shape
    tr = _pick(R, max(16, tile_elems // C // 16 * 16), 16)

    def body(lo_ref, hi_ref, o_ref, out_ref):
        mine = jnp.where(lax.axis_index("c") == 0, lo_ref[...].astype(F32), hi_ref[...].astype(F32))
        out_ref[...] = (mine + o_ref[...].astype(F32)).astype(out_ref.dtype)

    blk = pl.BlockSpec((tr, C), lambda i: (i, 0))
    return pl.pallas_call(
        body, grid=(R // tr,), in_specs=[blk, blk, blk], out_specs=blk, out_shape=jax.ShapeDtypeStruct((R, C), BF16),
        compiler_params=_cparams("parallel"), name=name,
    )(lo, hi, other)


def scatter_to_chips(pieces, chip_axes, name):
    n = len(pieces)

    def block_shape(u):
        shp = list(pieces[u].shape)
        shp[chip_axes[u]] //= N_CHIPS
        return tuple(shp)

    def body(*refs):
        ins, outs = refs[:n], refs[n:2 * n]
        send_sems, recv_sems = refs[2 * n:]
        x, y, c = _place()
        me = 2 * x + y
        started = []
        for u in range(n):
            size = block_shape(u)[chip_axes[u]]
            for k in range(1, N_CHIPS):
                px, py = _flip(x, k >> 1), _flip(y, k & 1)
                cp = pltpu.make_async_remote_copy(src_ref=_region(ins[u], chip_axes[u], 2 * px + py, size),
                                                  dst_ref=outs[u].at[me], send_sem=send_sems.at[u, k - 1],
                                                  recv_sem=recv_sems.at[u, k - 1], device_id=(px, py, c),
                                                  device_id_type=MESH)
                cp.start()
                started.append(cp)
        for u in range(n):
            size = block_shape(u)[chip_axes[u]]
            for k in range(1, N_CHIPS):
                px, py = _flip(x, k >> 1), _flip(y, k & 1)
                pltpu.make_async_remote_copy(src_ref=_region(ins[u], chip_axes[u], me, size),
                                             dst_ref=outs[u].at[2 * px + py], send_sem=send_sems.at[u, k - 1],
                                             recv_sem=recv_sems.at[u, k - 1], device_id=(px, py, c),
                                             device_id_type=MESH).wait_recv()
        for cp in started:
            cp.wait_send()

    sem = pltpu.SemaphoreType.DMA((n, N_CHIPS - 1))
    return pl.pallas_call(
        body, in_specs=[_ANY] * n, out_specs=[_ANY] * n,
        out_shape=[jax.ShapeDtypeStruct((N_CHIPS,) + block_shape(u), pieces[u].dtype) for u in range(n)],
        scratch_shapes=[sem, sem], name=name,
    )(*pieces)


def scatter_start(pieces, chip_axes, name, after=()):
    n = len(pieces)

    def block_shape(u):
        shp = list(pieces[u].shape)
        shp[chip_axes[u]] //= N_CHIPS
        return tuple(shp)

    def body(*refs):
        ins, land_refs = refs[:n], refs[n:2 * n]
        first_out = 2 * n + len(after)
        send_sems, recv_sems = refs[first_out:first_out + n], refs[first_out + n:first_out + 2 * n]
        token = refs[-1]
        x, y, c = _place()
        me = 2 * x + y
        for u in range(n):
            size = block_shape(u)[chip_axes[u]]
            for k in range(1, N_CHIPS):
                px, py = _flip(x, k >> 1), _flip(y, k & 1)
                pltpu.make_async_remote_copy(src_ref=_region(ins[u], chip_axes[u], 2 * px + py, size),
                                             dst_ref=land_refs[u].at[me], send_sem=send_sems[u], recv_sem=recv_sems[u],
                                             device_id=(px, py, c), device_id_type=MESH).start()
        token[...] = jnp.zeros_like(token)

    lands = [pltpu.with_memory_space_constraint(lax.empty((N_CHIPS,) + block_shape(u), pieces[u].dtype), pltpu.HBM)
             for u in range(n)]
    ins = [pltpu.with_memory_space_constraint(a, pltpu.HBM) for a in pieces] + lands
    sems = (pltpu.SemaphoreType.DMA(()),) * (2 * n)
    outs = pl.pallas_call(
        body, name=name,
        out_shape=sems + tuple(pltpu.HBM(a.shape, a.dtype) for a in ins) + (jax.ShapeDtypeStruct((8, LANES), F32),),
        in_specs=[_HBM] * len(ins) + [_ANY] * len(after),
        out_specs=(_SEM,) * (2 * n) + (_HBM,) * len(ins) + (pl.BlockSpec(memory_space=pltpu.VMEM),),
        input_output_aliases={i: 2 * n + i for i in range(len(ins))},
        compiler_params=pltpu.CompilerParams(has_side_effects=_EFFECT),
    )(*ins, *after)
    return list(outs[:n]), list(outs[n:2 * n]), list(outs[2 * n:3 * n]), list(outs[3 * n:4 * n]), outs[-1]


def scatter_wait(send_sems, recv_sems, lands, pieces, after, name):
    n = len(lands)

    def body(*refs):
        land_refs, send_refs, recv_refs = refs[:n], refs[n:2 * n], refs[2 * n:3 * n]
        x, y, c = _place()
        for u in range(n):
            three = land_refs[u].at[pl.ds(0, N_CHIPS - 1)]
            cp = pltpu.make_async_remote_copy(src_ref=three, dst_ref=three, send_sem=send_refs[u], recv_sem=recv_refs[u],
                                              device_id=(x, y, 1 - c), device_id_type=MESH)
            cp.wait_send()
            cp.wait_recv()

    outs = pl.pallas_call(
        body, name=name,
        out_shape=tuple(pltpu.HBM(a.shape, a.dtype) for a in lands),
        in_specs=[_HBM] * n + [_SEM] * (2 * n) + [_ANY] + [_HBM] * len(pieces),
        out_specs=(_HBM,) * n,
        input_output_aliases={j: j for j in range(n)},
        compiler_params=pltpu.CompilerParams(has_side_effects=_EFFECT),
    )(*lands, *send_sems, *recv_sems, after, *pieces)
    return list(outs)


def gather_halves(parts, slots, out_shapes, name):
    n = len(parts)

    def body(*refs):
        ins, outs = refs[:n], refs[n:n + len(out_shapes)]
        send_sems, recv_sems = refs[n + len(out_shapes):]
        x, y, c = _place()
        started = []
        for u in range(n):
            t, s = slots[u]
            cp = pltpu.make_async_remote_copy(src_ref=ins[u], dst_ref=outs[t].at[s, c], send_sem=send_sems.at[u],
                                              recv_sem=recv_sems.at[u], device_id=(x, y, 1 - c), device_id_type=MESH)
            cp.start()
            started.append(cp)
        for u in range(n):
            t, s = slots[u]
            pltpu.make_async_remote_copy(src_ref=ins[u], dst_ref=outs[t].at[s, 1 - c], send_sem=send_sems.at[u],
                                         recv_sem=recv_sems.at[u], device_id=(x, y, 1 - c),
                                         device_id_type=MESH).wait_recv()
        for cp in started:
            cp.wait_send()

    return pl.pallas_call(
        body, in_specs=[_ANY] * n, out_specs=[_ANY] * len(out_shapes),
        out_shape=[jax.ShapeDtypeStruct(shp, F32) for shp in out_shapes],
        scratch_shapes=[pltpu.SemaphoreType.DMA((n,)), pltpu.SemaphoreType.DMA((n,))], name=name,
    )(*parts)


WEIGHT_ORDER = ["mod_w", "mod_b", "norm1_g", "norm2_g", "pool_w", "pool_b", "pool_scale", "kv_in_g", "w_dkv",
                "ckv_norm_g", "w_uk", "w_uv", "w_dq", "q_norm_g", "w_uq", "w_o", "w_up", "conv_w", "conv_b", "w_down",
                "final_g"]
EXCHANGED = {"w_up": (2, 0), "w_down": (1, 0), "w_o": (1, 0), "w_uq": (2, 0), "w_dq": (1, 0), "pool_w": (2, 0),
             "w_dkv": (0, 1), "w_uk": (1, 0), "w_uv": (1, 0)}
SMALL_SHARDED = {"conv_w": 2, "pool_b": 1, "pool_scale": 1}
REPLICATED = ["mod_b", "norm1_g", "norm2_g", "kv_in_g", "ckv_norm_g", "q_norm_g", "conv_b", "final_g"]


def _padded(n, align):
    return -(-n // align) * align


def _flat_pad(parts, total):
    flat = jnp.concatenate(parts, axis=-1)
    pad = total - flat.shape[-1]
    if pad:
        flat = jnp.concatenate([flat, jnp.zeros(flat.shape[:-1] + (pad,), flat.dtype)], axis=-1)
    return flat


def _split_shards(full, axis):
    shp = full.shape
    t = full.reshape(shp[:axis] + (N_CHIPS, shp[axis] // N_CHIPS) + shp[axis + 1:])
    return jnp.moveaxis(t, axis, 0).reshape(N_CHIPS, -1)


def _join_shards(rows, shard_shape, axis):
    t = jnp.moveaxis(rows.reshape((N_CHIPS,) + tuple(shard_shape)), 0, axis)
    return t.reshape(tuple(shard_shape[:axis]) + (N_CHIPS * shard_shape[axis],) + tuple(shard_shape[axis + 1:]))


def _index(a, i, axis=0):
    return lax.dynamic_index_in_dim(a, i, axis, keepdims=False)


def kernel(x, c, positions, mod_w, mod_b, norm1_g, norm2_g, pool_w, pool_b, pool_scale, kv_in_g, w_dkv, ckv_norm_g, w_uk, w_uv, w_dq, q_norm_g, w_uq, w_o, w_up, conv_w, conv_b, w_down, final_g, loss_target, m_mod_w, m_mod_b, m_norm1_g, m_norm2_g, m_pool_w, m_pool_b, m_pool_scale, m_kv_in_g, m_w_dkv, m_ckv_norm_g, m_w_uk, m_w_uv, m_w_dq, m_q_norm_g, m_w_uq, m_w_o, m_w_up, m_conv_w, m_conv_b, m_w_down, m_final_g, v_mod_w, v_mod_b, v_norm1_g, v_norm2_g, v_pool_w, v_pool_b, v_pool_scale, v_kv_in_g, v_w_dkv, v_ckv_norm_g, v_w_uk, v_w_uv, v_w_dq, v_q_norm_g, v_w_uq, v_w_o, v_w_up, v_conv_w, v_conv_b, v_w_down, v_final_g):
    given = dict(locals())
    W = {n: given[n] for n in WEIGHT_ORDER}
    M1 = {n: given["m_" + n] for n in WEIGHT_ORDER}
    V2 = {n: given["v_" + n] for n in WEIGHT_ORDER}
    xi, yi, ci = lax.axis_index("x"), lax.axis_index("y"), lax.axis_index("c")
    chip = 2 * xi + yi
    dev = 4 * xi + 2 * yi + ci
    x0 = x[0]
    S_, D = x0.shape
    Fh = conv_b.shape[1]
    E = mod_b.shape[1]
    Es = E // N_CHIPS
    zD = jnp.zeros((D,), F32)

    c_all = device_gather(c, "gather_c").reshape(N_DEV, D)
    c_pad = jnp.concatenate([c_all, jnp.zeros((16 - N_DEV, D), F32)], axis=0)
    mod_b_mine = lax.dynamic_slice_in_dim(mod_b, chip * Es, Es, axis=1)
    mods_part = mods_fwd(c_pad, mod_w, mod_b_mine, "mods_fwd")
    mods_all = chip_gather(mods_part, "gather_mods")
    mods = jnp.swapaxes(_index(mods_all, dev, axis=2), 0, 1).reshape(DEPTH, E)
    mod = [[mods[l, k * D:(k + 1) * D] for k in range(6)] for l in range(DEPTH)]

    full = {}
    ssz = {n: math.prod(W[n].shape) for n in SMALL_SHARDED}
    Tw = _padded(sum(ssz.values()), 8 * PACK_COLS)
    small_rows = chip_gather(_flat_pad([W[n].reshape(-1) for n in SMALL_SHARDED], Tw).reshape(-1, PACK_COLS),
                             "gather_small_w").reshape(N_CHIPS, Tw)
    off = 0
    for n, axis in SMALL_SHARDED.items():
        full[n] = _join_shards(small_rows[:, off:off + ssz[n]], W[n].shape, axis)
        off += ssz[n]

    names = list(EXCHANGED)
    shards = [W[n].astype(BF16) for n in names]
    n_mla = DEPTH - N_A
    first_axes = {"w_up": (1, 0), "w_down": (0, 1), "pool_w": (1, 0)}
    first = gather_weights([shards[names.index(n)][0] for n in first_axes], list(first_axes.values()), "gather_weights0",
                           after=[mods, small_rows])
    for n, arr in zip(first_axes, first):
        full[(n, 0)] = arr
    items, groups = [], []

    def group(entries):
        groups.append(list(range(len(items), len(items) + len(entries))))
        for n, layer in entries:
            ca = EXCHANGED[n][0] - (0 if layer is None else 1)
            items.append((names.index(n), layer, 0 if n == "w_dkv" else ca))

    for l in range(1, N_A):
        group([("w_up", l), ("w_down", l), ("pool_w", l)])
    for j in range(n_mla):
        head = [("w_dkv", None), ("w_uk", None), ("w_uv", None)] if j == 0 else []
        group(head + [("w_dq", j), ("w_uq", j), ("w_o", j), ("w_up", N_A + j), ("w_down", N_A + j)])
    w_send, w_recv, shards_thru, lands, _ = weights_start(shards, items, "weights_start", after=first)

    def weights_ready(g, after):
        keep = shards_thru if g == len(groups) - 1 else []
        got = weights_wait([w_send[i] for i in groups[g]], [w_recv[i] for i in groups[g]], [lands[i] for i in groups[g]],
                           after, keep, f"weights_wait{g}")
        for i, arr in zip(groups[g], got):
            t, layer, _ = items[i]
            full[(names[t], 0 if layer is None else layer)] = arr

    q_rank = W["w_uq"].shape[1]
    kv_w = KV_RANK + QK_ROPE

    def uq_ext(j):
        wq = full[("w_uq", j)].reshape(q_rank, N_HEADS, QK_HEAD)
        return jnp.concatenate([wq, jnp.zeros((q_rank, N_HEADS, HEAD_PAD - QK_HEAD), BF16)],
                               axis=2).reshape(q_rank, N_HEADS * HEAD_PAD)


    half = QK_ROPE // 2
    inv = 1.0 / (ROPE_THETA ** (jnp.arange(0, QK_ROPE, 2, dtype=F32) / QK_ROPE))
    inv_row = jnp.concatenate([inv, inv, jnp.zeros((LANES - 2 * half,), F32)]).reshape(1, LANES)
    tabs = rope_tables(positions[0].astype(F32).reshape(S_, 1), inv_row, "rope_tables")
    att_scale = QK_HEAD ** -0.5

    saved = []
    xcur = x0
    kv_saved = None
    K = VX = knv = None
    for l in range(DEPTH):
        sh1, sc1, g1, sh2, sc2, g2 = mod[l]
        st = {"xin": xcur}
        if l:
            weights_ready(l - 1, xcur)
        if l == N_A:
            w_dkv_ext = jnp.concatenate([full[("w_dkv", 0)], jnp.zeros((D, KV_RANK + LANES - kv_w), BF16)], axis=1)
            w_ukv = jnp.concatenate([full[("w_uk", 0)], full[("w_uv", 0)]], axis=1)
            xn = norm_fwd(xcur, kv_in_g, zD, zD, BF16, "kvin_fwd")
            kv_ext = mm(xn, w_dkv_ext, "nn", F32, "dkv_mm")
            lat = kv_ext[:, :KV_RANK]
            zk = jnp.zeros((KV_RANK,), F32)
            ckv = norm_fwd(lat, ckv_norm_g, zk, zk, BF16, "ckv_fwd")
            knv = mm(ckv, w_ukv, "nn", BF16, "ukv_mm")
            K, VX = k_prep(knv, kv_ext, tabs, "k_prep")
            kv_saved = {"x": xcur, "xn": xn, "lat": lat, "ckv": ckv}
        if l < N_A:
            h1 = norm_fwd(xcur, norm1_g[l], sc1, sh1, F32, f"norm1_fwd{l}")
            st["pooled"] = _pool_call(h1, BF16, f"pool_fwd{l}", False)
            st["cs"] = g1 * full["pool_scale"][l]
            st["ypre"], xmid = gmm(st["pooled"], full[("pool_w", l)], "nn", F32, f"pool_mm{l}", bias=full["pool_b"][l],
                                   res=xcur, colscale=st["cs"])
        else:
            j = l - N_A
            st["h1"] = norm_fwd(xcur, norm1_g[l], sc1, sh1, BF16, f"norm1_fwd{l}")
            st["ql"] = mm(st["h1"], full[("w_dq", j)], "nn", F32, f"dq_mm{l}")
            st["cq"] = norm_fwd(st["ql"], q_norm_g[j], jnp.zeros_like(q_norm_g[j]), jnp.zeros_like(q_norm_g[j]), BF16,
                                f"qnorm_fwd{l}")
            st["w_uq_ext"] = uq_ext(j)
            qe = mm(st["cq"], st["w_uq_ext"], "nn", F32, f"uq_mm{l}")
            st["Q"] = q_prep(qe, tabs, att_scale, False, f"q_prep{l}")
            st["o"], lse = attn_fwd(st["Q"], K, VX, f"attn_fwd{l}")
            st["lse"] = lse.reshape(N_HEADS, 1, S_)
            st["y"], xmid = mm(st["o"], full[("w_o", j)], "nn", F32, f"wo_mm{l}", res=xcur, colscale=g1)
        st["xmid"] = xmid
        st["h2"] = norm_fwd(xmid, norm2_g[l], sc2, sh2, BF16, f"norm2_fwd{l}")
        st["u"] = mm(st["h2"], full[("w_up", l)], "nn", BF16, f"up_mm{l}")
        st["z"] = glu_fwd(st["u"], full["conv_w"][l], conv_b[l], f"glu_fwd{l}")
        st["f"], xcur = mm(st["z"], full[("w_down", l)], "nn", F32, f"down_mm{l}", tk=1408, res=xmid, colscale=g2)
        saved.append(st)

    dx, d_final_g, loss_part = loss_head(xcur, final_g, loss_target[0], "loss_head")
    loss = lax.psum(loss_part[0, 0], ("x", "y", "c"))

    def begin_reduce(tensors, first_slot, tag):
        units = []
        for n in tensors:
            ca = EXCHANGED[n][0]
            if W[n].ndim > 2:
                n_slots = W[n].shape[0] // 2
                for sl in range(first_slot if n_slots > 1 else 0, first_slot + 1 if n_slots > 1 else 1):
                    units.append((n, sl, G[(n, 2 * sl)], G[(n, 2 * sl + 1)], ca - 1))
            elif n == "w_dkv":
                g4 = G[(n, 0)].reshape(N_CHIPS, 2, -1, kv_w)
                units.append((n, 0, g4[:, 0], g4[:, 1], 0))
            else:
                rows_half = W[n].shape[0] // 2
                units.append((n, 0, G[(n, 0)][:rows_half], G[(n, 0)][rows_half:], ca))
        lo = [u[2] for u in units]
        hi = [u[3] for u in units]
        theirs = reduce_to_sibling(lo, hi, f"reduce_cores_{tag}")
        sums = [add_selected(l_.reshape(-1, l_.shape[-1]), h_.reshape(-1, l_.shape[-1]), t_.reshape(-1, l_.shape[-1]),
                             f"reduce_cores_add_{tag}{i}").reshape(l_.shape)
                for i, (l_, h_, t_) in enumerate(zip(lo, hi, theirs))]
        return units, sums, [u[4] for u in units]

    G = {}
    dmods = [None] * DEPTH
    d_norm1 = [None] * DEPTH
    d_norm2 = [None] * DEPTH
    d_conv_b = [None] * DEPTH
    d_qnorm = [None] * n_mla
    dkv_acc = []
    for l in reversed(range(DEPTH)):
        sh1, sc1, g1, sh2, sc2, g2 = mod[l]
        st = saved[l]
        df, a2, _ = gate_bwd(dx, st["f"], g2, f"gate2_bwd{l}")
        dz = mm(df, full[("w_down", l)], "nt", BF16, f"down_dx{l}")
        G[("w_down", l)] = mm(st["z"], df, "tn", BF16, f"down_dw{l}")
        du, dcw, dcb = glu_bwd(st["u"], dz, full["conv_w"][l], conv_b[l], f"glu_bwd{l}")
        G[("conv_w", l)] = dcw
        d_conv_b[l] = dcb[0]
        dh2 = mm(du, full[("w_up", l)], "nt", BF16, f"up_dx{l}", tk=1408)
        G[("w_up", l)] = mm(st["h2"], du, "tn", BF16, f"up_dw{l}")
        dxmid, s1, s2 = norm_bwd(st["xmid"], norm2_g[l], sc2, dh2, dx, f"norm2_bwd{l}")
        dsh2, dsc2, d_norm2[l] = s1[0], s2[0] * norm2_g[l], s2[0] * (1.0 + sc2)
        if l < N_A:
            dyp, a1, csum = gate_bwd(dxmid, st["ypre"], st["cs"], f"gate1_bwd{l}")
            dg1 = full["pool_scale"][l] * a1[0]
            G[("pool_scale", l)] = g1 * a1[0]
            G[("pool_b", l)] = st["cs"] * csum[0]
            dpooled = gmm(dyp, full[("pool_w", l)], "nt", F32, f"pool_dx{l}")
            G[("pool_w", l)] = gmm(st["pooled"], dyp, "tn", BF16, f"pool_dw{l}")
            dh1 = _pool_call(dpooled, F32, f"pool_bwd{l}", True)
        else:
            j = l - N_A
            dy, a1, _ = gate_bwd(dxmid, st["y"], g1, f"gate1_bwd{l}")
            dg1 = a1[0]
            do = mm(dy, full[("w_o", j)], "nt", BF16, f"wo_dx{l}")
            G[("w_o", j)] = mm(st["o"], dy, "tn", BF16, f"wo_dw{l}")
            delta = attn_delta(st["o"], do, f"attn_delta{l}").reshape(N_HEADS, 1, S_)
            dQ, dK, dV = attn_bwd(st["Q"], K, VX, do, st["lse"], delta, f"attn_bwd{l}")
            dkv_acc.append((dK, dV))
            dqe = q_prep(dQ, tabs, att_scale, True, f"q_prep_bwd{l}")
            dcq = mm(dqe, st["w_uq_ext"], "nt", F32, f"uq_dx{l}")
            G[("w_uq", j)] = mm(st["cq"], dqe, "tn", BF16, f"uq_dw{l}").reshape(q_rank, N_HEADS, HEAD_PAD)[
                :, :, :QK_HEAD].reshape(q_rank, N_HEADS * QK_HEAD)
            zq = jnp.zeros_like(q_norm_g[j])
            dql, _, s2q = norm_bwd(st["ql"], q_norm_g[j], zq, dcq, None, f"qnorm_bwd{l}")
            d_qnorm[j] = s2q[0]
            dh1 = mm(dql, full[("w_dq", j)], "nt", BF16, f"dq_dx{l}")
            G[("w_dq", j)] = mm(st["h1"], dql, "tn", BF16, f"dq_dw{l}")
        dx, s1, s2 = norm_bwd(st["xin"], norm1_g[l], sc1, dh1, dxmid, f"norm1_bwd{l}")
        dsh1, dsc1, d_norm1[l] = s1[0], s2[0] * norm1_g[l], s2[0] * (1.0 + sc1)
        dmods[l] = jnp.concatenate([dsh1, dsc1, dg1, dsh2, dsc2, a2[0]])
        if l == N_A:
            (dk_a, dv_a), (dk_b, dv_b) = dkv_acc
            dknv, d_tk = k_prep_bwd(dk_a, dk_b, dv_a, dv_b, tabs, "k_prep_bwd")
            dckv = mm(dknv, w_ukv, "nt", F32, "ukv_dx")
            d_ukv = mm(kv_saved["ckv"], dknv, "tn", BF16, "ukv_dw")
            G[("w_uk", 0)], G[("w_uv", 0)] = d_ukv[:, :N_HEADS * QK_NOPE], d_ukv[:, N_HEADS * QK_NOPE:]
            zk = jnp.zeros((KV_RANK,), F32)
            dlat, _, s2c = norm_bwd(kv_saved["lat"], ckv_norm_g, zk, dckv, None, "ckv_bwd")
            d_ckv_g = s2c[0]
            dkv_ext = jnp.concatenate([dlat, d_tk], axis=1)
            dxn = mm(dkv_ext, w_dkv_ext, "nt", BF16, "dkv_dx")
            G[("w_dkv", 0)] = mm(kv_saved["xn"], dkv_ext, "tn", BF16, "dkv_dw")[:, :kv_w]
            dx, _, s2k = norm_bwd(kv_saved["x"], kv_in_g, zD, dxn, dx, "kvin_bwd")
            d_kvin_g = s2k[0]
            e_units, e_sums, e_axes = begin_reduce([n for n in EXCHANGED if n != "pool_w"], 1, "early")
            e_send, e_recv, e_pieces, e_lands, e_token = scatter_start(e_sums, e_axes, "reduce_chips_start")
            early = (e_units, e_send, e_recv, e_lands, e_pieces, e_axes)
            mod[l - 1][5] = mod[l - 1][5] + e_token[0, 0]

    small = {"mod_b": jnp.stack(dmods), "norm1_g": jnp.stack(d_norm1), "norm2_g": jnp.stack(d_norm2),
             "kv_in_g": d_kvin_g, "ckv_norm_g": d_ckv_g, "q_norm_g": jnp.stack(d_qnorm),
             "conv_b": jnp.stack(d_conv_b), "final_g": d_final_g[0]}
    extra = {n: jnp.stack([G[(n, i)] for i in range(W[n].shape[0])]) for n in SMALL_SHARDED}
    ssizes = {n: math.prod(W[n].shape) for n in REPLICATED}
    esizes = {n: math.prod(extra[n].shape) for n in SMALL_SHARDED}
    Ts = _padded(sum(ssizes.values()) + sum(esizes.values()), 8 * PACK_COLS)

    def pack_small(d, tail=()):
        return _flat_pad([d[n].reshape(-1) for n in REPLICATED] + [t.reshape(-1) for t in tail],
                         Ts).reshape(Ts // PACK_COLS, PACK_COLS)

    parts = device_gather(pack_small(small, [extra[n] for n in SMALL_SHARDED]), "gather_small")

    l_units, l_sums, l_axes = begin_reduce([n for n in EXCHANGED if W[n].ndim > 2 and W[n].shape[0] == DEPTH] + ["pool_w"],
                                           0, "late")
    l_send, l_recv, l_pieces, l_lands, l_token = scatter_start(l_sums, l_axes, "reduce_chips_late_start", after=[parts])
    parts = parts + l_token[0, 0]

    grads, deltas, new_m, new_v = {}, {}, {}, {}
    outs = adamw_sum(parts, pack_small(W), pack_small(M1), pack_small(V2), "adamw_small")
    off = 0
    for n in REPLICATED:
        for dst, o in zip((grads, deltas, new_m, new_v), outs):
            dst[n] = o.reshape(-1)[off:off + ssizes[n]].reshape(W[n].shape)
        off += ssizes[n]
    for n, axis in SMALL_SHARDED.items():
        g_full = outs[0].reshape(-1)[off:off + esizes[n]].reshape(extra[n].shape)
        off += esizes[n]
        size = W[n].shape[axis]
        grads[n] = lax.dynamic_slice_in_dim(g_full, chip * size, size, axis=axis)
        deltas[n], new_m[n], new_v[n] = adamw(W[n], grads[n], M1[n], V2[n], f"adamw_{n}")

    dm_all = parts.reshape(N_DEV, -1)[:, :DEPTH * E].reshape(N_DEV, DEPTH, E)
    dm_mine = jnp.swapaxes(lax.dynamic_slice_in_dim(dm_all, chip * Es, Es, axis=2), 0, 1)
    grads["mod_w"], deltas["mod_w"], new_m["mod_w"], new_v["mod_w"] = adamw_modw(
        c_all.reshape(N_DEV, D, 1), dm_mine, mod_w, m_mod_w, v_mod_w, "adamw_mod_w")

    def finish_reduce(pieces, axes, got, tag):
        out = []
        for i, (sm, ax, g4) in enumerate(zip(pieces, axes, got)):
            size = sm.shape[ax] // N_CHIPS
            g4 = lax.dynamic_update_index_in_dim(g4, lax.dynamic_slice_in_dim(sm, chip * size, size, axis=ax), chip, 0)
            blk = g4.shape[1:]
            out.append(sum_parts(g4.reshape(N_CHIPS, -1, blk[-1]), f"reduce_chips_add_{tag}{i}").reshape(blk))
        return out

    e_units, e_send, e_recv, e_lands, e_pieces, e_axes = early
    early_got = scatter_wait(e_send, e_recv, e_lands, e_pieces, dx, "reduce_chips_wait")
    reduced = finish_reduce(e_pieces, e_axes, early_got, "early")
    late_got = scatter_wait(l_send, l_recv, l_lands, l_pieces, new_v["mod_w"], "reduce_chips_late_wait")
    reduced += finish_reduce(l_pieces, l_axes, late_got, "late")
    units = e_units + l_units
    slots, out_shapes = [], []
    for n in EXCHANGED:
        mine = [i for i, u in enumerate(units) if u[0] == n]
        out_shapes.append((len(mine), 2) + reduced[mine[0]].shape)
        slots += [(len(out_shapes) - 1, units[i][1]) for i in mine]
    order = [i for n in EXCHANGED for i, u in enumerate(units) if u[0] == n]
    halves = gather_halves([reduced[i] for i in order], slots, out_shapes, "reduce_gather")
    for ti, n in enumerate(EXCHANGED):
        g = halves[ti]
        for i, u in enumerate(units):
            if u[0] == n:
                g = lax.dynamic_update_slice(g, reduced[i][None, None], (u[1], ci) + (0,) * reduced[i].ndim)
        grads[n] = g.reshape(W[n].shape)
        deltas[n], new_m[n], new_v[n] = adamw(W[n], grads[n], M1[n], V2[n], f"adamw_{n}")

    return (loss, dx.reshape(x.shape), *[grads[n] for n in WEIGHT_ORDER], *[deltas[n] for n in WEIGHT_ORDER],
            *[new_m[n] for n in WEIGHT_ORDER], *[new_v[n] for n in WEIGHT_ORDER])
```

```python
import functools
import math

import jax
import jax.numpy as jnp
from jax import lax
from jax.experimental import pallas as pl
from jax.experimental.pallas import tpu as pltpu

F32 = jnp.float32
BF16 = jnp.bfloat16
MESH = pl.DeviceIdType.MESH

DEPTH = 4
N_A = 2
POOL_WINDOWS = (2, 4, 8, 16)
N_GROUPS = 4
N_HEADS = 8
QK_NOPE = 128
QK_ROPE = 64
V_HEAD = 128
QK_HEAD = QK_NOPE + QK_ROPE
HEAD_PAD = 256
KV_RANK = 256
ROPE_THETA = 10000.0
EPS = 1e-6
ADAM_LR = 0.001
ADAM_B1 = 0.9
ADAM_B2 = 0.999
ADAM_EPS = 1e-08
ADAM_WD = 0.01
ADAM_STEP = 10

N_CHIPS = 4
N_DEV = 8
LANES = 128
PACK_COLS = 1024
VMEM_LIMIT = 56 * 1024 * 1024
GLU_TILE = 256
ATT_BWD_K_BLOCK = 512
ATT_BWD_Q_BLOCK = 512
ATT_Q_BLOCK = 1024
ATT_K_BLOCK = 512
ATT_HEADS_PER_STEP = 2


def _cparams(*sem):
    return pltpu.CompilerParams(dimension_semantics=sem if sem else None, vmem_limit_bytes=VMEM_LIMIT)


def _pick(n, target, mult):
    best = None
    d = mult
    while d <= min(n, target):
        if n % d == 0:
            best = d
        d += mult
    return n if best is None else best


def _row(v):
    return v.reshape(1, -1).astype(F32)


_DIMS = {"nn": (((1,), (0,)), ((), ())), "nt": (((1,), (1,)), ((), ())), "tn": (((0,), (0,)), ((), ()))}


def _mm_body(mode, nk, has_bias, has_res):
    def body(*refs):
        a_ref, b_ref = refs[0], refs[1]
        pos = 2
        bias_ref = res_ref = cs_ref = None
        if has_bias:
            bias_ref = refs[pos]
            pos += 1
        if has_res:
            res_ref, cs_ref = refs[pos], refs[pos + 1]
            pos += 2
        o_ref = refs[pos]
        pos += 1
        o2_ref = None
        if has_res:
            o2_ref = refs[pos]
            pos += 1
        acc_ref = refs[pos] if nk > 1 else None
        k = pl.program_id(2)
        part = lax.dot_general(a_ref[...].astype(BF16), b_ref[...].astype(BF16), _DIMS[mode],
                               preferred_element_type=F32)

        def finish(y):
            if has_bias:
                y = y + bias_ref[...]
            o_ref[...] = y.astype(o_ref.dtype)
            if has_res:
                o2_ref[...] = res_ref[...] + cs_ref[...] * y

        if nk == 1:
            finish(part)
            return

        @pl.when(k == 0)
        def _():
            acc_ref[...] = part

        @pl.when((k > 0) & (k < nk - 1))
        def _():
            acc_ref[...] += part

        @pl.when(k == nk - 1)
        def _():
            finish(acc_ref[...] + part)

    return body


def mm(a, b, mode, out_dtype, name, *, tm=1408, tn=1408, tk=1024, bias=None, res=None, colscale=None, layer=None):
    bshape = b.shape if layer is None else b.shape[1:]
    if mode == "nn":
        (M, K), N = a.shape, bshape[1]
    elif mode == "nt":
        (M, K), N = a.shape, bshape[0]
    else:
        (K, M), N = a.shape, bshape[1]
    tm = _pick(M, tm, LANES if mode == "tn" else 8)
    tn = _pick(N, tn, LANES)
    tk = _pick(K, tk, LANES) if mode != "tn" else _pick(K, tk, 8)
    nk = K // tk
    a_spec = {"nn": pl.BlockSpec((tm, tk), lambda i, j, k: (i, k)),
              "nt": pl.BlockSpec((tm, tk), lambda i, j, k: (i, k)),
              "tn": pl.BlockSpec((tk, tm), lambda i, j, k: (k, i))}[mode]
    b_blk, b_map = {"nn": ((tk, tn), lambda i, j, k: (k, j)),
                    "nt": ((tn, tk), lambda i, j, k: (j, k)),
                    "tn": ((tk, tn), lambda i, j, k: (k, j))}[mode]
    if layer is None:
        b_spec = pl.BlockSpec(b_blk, b_map)
    else:
        b_spec = pl.BlockSpec((None,) + b_blk, lambda i, j, k: (layer,) + b_map(i, j, k))
    o_spec = pl.BlockSpec((tm, tn), lambda i, j, k: (i, j))
    v_spec = pl.BlockSpec((1, tn), lambda i, j, k: (0, j))
    in_specs, args = [a_spec, b_spec], [a, b]
    if bias is not None:
        in_specs.append(v_spec)
        args.append(_row(bias))
    out_shape = [jax.ShapeDtypeStruct((M, N), out_dtype)]
    out_specs = [o_spec]
    if res is not None:
        in_specs += [o_spec, v_spec]
        args += [res, _row(colscale)]
        out_shape.append(jax.ShapeDtypeStruct((M, N), F32))
        out_specs.append(o_spec)
    outs = pl.pallas_call(
        _mm_body(mode, nk, bias is not None, res is not None),
        grid=(M // tm, N // tn, nk),
        in_specs=in_specs, out_specs=out_specs, out_shape=out_shape,
        scratch_shapes=[pltpu.VMEM((tm, tn), F32)] if nk > 1 else [],
        compiler_params=_cparams("parallel", "parallel", "arbitrary"),
        name=name,
    )(*args)
    return outs if res is not None else outs[0]


def gmm(a, w, mode, out_dtype, name, *, bias=None, res=None, colscale=None, tr=512):
    S_ = a.shape[0]
    G = N_GROUPS
    C = a.shape[1] // G
    tr = _pick(S_, tr, 8)
    nr = S_ // tr
    if mode == "tn":
        def body(a_ref, b_ref, o_ref, acc_ref):
            i = pl.program_id(1)

            @pl.when(i == 0)
            def _():
                acc_ref[...] = jnp.zeros_like(acc_ref)

            acc_ref[...] += lax.dot_general(a_ref[...].astype(BF16), b_ref[...].astype(BF16), _DIMS["tn"],
                                            preferred_element_type=F32)

            @pl.when(i == nr - 1)
            def _():
                o_ref[...] = acc_ref[...].astype(o_ref.dtype)

        blk = pl.BlockSpec((tr, C), lambda g, i: (i, g))
        return pl.pallas_call(
            body, grid=(G, nr), in_specs=[blk, blk],
            out_specs=pl.BlockSpec((None, C, C), lambda g, i: (g, 0, 0)),
            out_shape=jax.ShapeDtypeStruct((G, C, C), out_dtype),
            scratch_shapes=[pltpu.VMEM((C, C), F32)],
            compiler_params=_cparams("parallel", "arbitrary"), name=name,
        )(a, w)

    has_bias, has_res = bias is not None, res is not None

    def body(*refs):
        a_ref, w_ref = refs[0], refs[1]
        pos = 2
        if has_bias:
            bias_ref = refs[pos]
            pos += 1
        if has_res:
            res_ref, cs_ref = refs[pos], refs[pos + 1]
            pos += 2
        o_ref = refs[pos]
        y = lax.dot_general(a_ref[...].astype(BF16), w_ref[...].astype(BF16), _DIMS[mode],
                            preferred_element_type=F32)
        if has_bias:
            y = y + bias_ref[...]
        o_ref[...] = y.astype(o_ref.dtype)
        if has_res:
            refs[pos + 1][...] = res_ref[...] + cs_ref[...] * y

    blk = pl.BlockSpec((tr, C), lambda i, g: (i, g))
    vec = pl.BlockSpec((1, C), lambda i, g: (0, g))
    in_specs = [blk, pl.BlockSpec((None, C, C), lambda i, g: (g, 0, 0))]
    args = [a, w]
    if has_bias:
        in_specs.append(vec)
        args.append(_row(bias))
    out_shape = [jax.ShapeDtypeStruct(a.shape, out_dtype)]
    out_specs = [blk]
    if has_res:
        in_specs += [blk, vec]
        args += [res, _row(colscale)]
        out_shape.append(jax.ShapeDtypeStruct(a.shape, F32))
        out_specs.append(blk)
    outs = pl.pallas_call(
        body, grid=(nr, G), in_specs=in_specs, out_specs=out_specs, out_shape=out_shape,
        compiler_params=_cparams("parallel", "parallel"), name=name,
    )(*args)
    return outs if has_res else outs[0]


def norm_fwd(x, g, sc, sh, out_dtype, name, tr=512):
    S_, Dn = x.shape
    tr = _pick(S_, tr, 8)

    def body(x_ref, g_ref, sc_ref, sh_ref, o_ref):
        xv = x_ref[...]
        r = lax.rsqrt(jnp.mean(xv * xv, axis=-1, keepdims=True) + EPS)
        o_ref[...] = (((xv * r) * g_ref[...]) * (1.0 + sc_ref[...]) + sh_ref[...]).astype(o_ref.dtype)

    blk = pl.BlockSpec((tr, Dn), lambda i: (i, 0))
    vec = pl.BlockSpec((1, Dn), lambda i: (0, 0))
    return pl.pallas_call(
        body, grid=(S_ // tr,), in_specs=[blk, vec, vec, vec], out_specs=blk,
        out_shape=jax.ShapeDtypeStruct((S_, Dn), out_dtype),
        compiler_params=_cparams("parallel"), name=name,
    )(x, _row(g), _row(sc), _row(sh))


def norm_bwd(x, g, sc, dh, dres, name, tr=512):
    S_, Dn = x.shape
    tr = _pick(S_, tr, 8)
    has_res = dres is not None

    def body(*refs):
        x_ref, g_ref, sc_ref, dh_ref = refs[:4]
        pos = 4
        if has_res:
            dres_ref = refs[pos]
            pos += 1
        dx_ref, s1_ref, s2_ref = refs[pos:pos + 3]
        i = pl.program_id(0)

        @pl.when(i == 0)
        def _():
            s1_ref[...] = jnp.zeros_like(s1_ref)
            s2_ref[...] = jnp.zeros_like(s2_ref)

        xv = x_ref[...]
        r = lax.rsqrt(jnp.mean(xv * xv, axis=-1, keepdims=True) + EPS)
        n = xv * r
        dhv = dh_ref[...].astype(F32)
        dn = dhv * (g_ref[...] * (1.0 + sc_ref[...]))
        dx = r * (dn - n * jnp.mean(dn * n, axis=-1, keepdims=True))
        if has_res:
            dx = dx + dres_ref[...]
        dx_ref[...] = dx
        s1_ref[...] += jnp.sum(dhv, axis=0, keepdims=True)
        s2_ref[...] += jnp.sum(dhv * n, axis=0, keepdims=True)

    blk = pl.BlockSpec((tr, Dn), lambda i: (i, 0))
    vec = pl.BlockSpec((1, Dn), lambda i: (0, 0))
    in_specs, args = [blk, vec, vec, blk], [x, _row(g), _row(sc), dh]
    if has_res:
        in_specs.append(blk)
        args.append(dres)
    vshape = jax.ShapeDtypeStruct((1, Dn), F32)
    return pl.pallas_call(
        body, grid=(S_ // tr,), in_specs=in_specs, out_specs=[blk, vec, vec],
        out_shape=[jax.ShapeDtypeStruct((S_, Dn), F32), vshape, vshape],
        compiler_params=_cparams("arbitrary"), name=name,
    )(*args)


def gate_bwd(dx, y, colscale, name, tr=512):
    S_, Dn = dx.shape
    tr = _pick(S_, tr, 8)

    def body(dx_ref, y_ref, cs_ref, d_ref, a_ref, c_ref):
        i = pl.program_id(0)

        @pl.when(i == 0)
        def _():
            a_ref[...] = jnp.zeros_like(a_ref)
            c_ref[...] = jnp.zeros_like(c_ref)

        dxv = dx_ref[...]
        d_ref[...] = (dxv * cs_ref[...]).astype(d_ref.dtype)
        a_ref[...] += jnp.sum(dxv * y_ref[...].astype(F32), axis=0, keepdims=True)
        c_ref[...] += jnp.sum(dxv, axis=0, keepdims=True)

    blk = pl.BlockSpec((tr, Dn), lambda i: (i, 0))
    vec = pl.BlockSpec((1, Dn), lambda i: (0, 0))
    vshape = jax.ShapeDtypeStruct((1, Dn), F32)
    return pl.pallas_call(
        body, grid=(S_ // tr,), in_specs=[blk, blk, vec], out_specs=[blk, vec, vec],
        out_shape=[jax.ShapeDtypeStruct((S_, Dn), BF16), vshape, vshape],
        compiler_params=_cparams("arbitrary"), name=name,
    )(dx, y, _row(colscale))


def loss_head(x, g, target, name, tr=512):
    S_, Dn = x.shape
    tr = _pick(S_, tr, 8)

    def body(x_ref, g_ref, t_ref, dx_ref, dg_ref, loss_ref):
        i = pl.program_id(0)

        @pl.when(i == 0)
        def _():
            dg_ref[...] = jnp.zeros_like(dg_ref)
            loss_ref[...] = jnp.zeros_like(loss_ref)

        xv = x_ref[...]
        r = lax.rsqrt(jnp.mean(xv * xv, axis=-1, keepdims=True) + EPS)
        n = xv * r
        e = n * g_ref[...] - t_ref[...]
        loss_ref[...] += 0.5 * jnp.sum(jnp.mean(e * e, axis=-1, keepdims=True), axis=0, keepdims=True)
        dy = e * (1.0 / Dn)
        dg_ref[...] += jnp.sum(dy * n, axis=0, keepdims=True)
        dn = dy * g_ref[...]
        dx_ref[...] = r * (dn - n * jnp.mean(dn * n, axis=-1, keepdims=True))

    blk = pl.BlockSpec((tr, Dn), lambda i: (i, 0))
    vec = pl.BlockSpec((1, Dn), lambda i: (0, 0))
    one = pl.BlockSpec((1, 1), lambda i: (0, 0))
    return pl.pallas_call(
        body, grid=(S_ // tr,), in_specs=[blk, vec, blk], out_specs=[blk, vec, one],
        out_shape=[jax.ShapeDtypeStruct((S_, Dn), F32), jax.ShapeDtypeStruct((1, Dn), F32),
                   jax.ShapeDtypeStruct((1, 1), F32)],
        compiler_params=_cparams("arbitrary"), name=name,
    )(x, _row(g), target)


POOL_HALO = 16
POOL_CHUNK = 512


def _rows(ref, lo, hi, n_rows):
    parts = []
    if lo < 0:
        parts.append(jnp.zeros((-lo, ref.shape[1]), F32))
    parts.append(ref[max(lo, 0):min(hi, n_rows), :].astype(F32))
    if hi > n_rows:
        parts.append(jnp.zeros((hi - n_rows, ref.shape[1]), F32))
    return parts[0] if len(parts) == 1 else jnp.concatenate(parts, axis=0)


def _window_sum(e, w, back):
    n = e.shape[0]
    s, width = e, 1
    while width < w:
        s = s + pltpu.roll(s, width if back else n - width, 0)
        width *= 2
    return s


def _pool_call(h, out_dtype, name, backward):
    S_, Dn = h.shape
    C = Dn // N_GROUPS
    ch = _pick(S_, POOL_CHUNK, 8)

    def body(h_ref, o_ref):
        g = pl.program_id(0)
        for gi, w in enumerate(POOL_WINDOWS):
            @pl.when(g == gi)
            def _(w=w):
                for r0 in range(0, S_, ch):
                    t = (r0 + lax.broadcasted_iota(jnp.int32, (ch, C), 0)).astype(F32)
                    cnt = jnp.minimum(t + 1.0, float(w))
                    if not backward:
                        ext = _rows(h_ref, r0 - POOL_HALO, r0 + ch, S_)
                        cur = ext[POOL_HALO:]
                        mean = _window_sum(ext, w, True)[POOL_HALO:] / cnt
                        o_ref[r0:r0 + ch, :] = (mean - cur).astype(o_ref.dtype)
                    else:
                        ext = _rows(h_ref, r0, r0 + ch + POOL_HALO, S_)
                        text = (r0 + lax.broadcasted_iota(jnp.int32, (ch + POOL_HALO, C), 0)).astype(F32)
                        e = ext / jnp.minimum(text + 1.0, float(w))
                        o_ref[r0:r0 + ch, :] = (_window_sum(e, w, False)[:ch] - ext[:ch]).astype(o_ref.dtype)

    blk = pl.BlockSpec((S_, C), lambda g: (0, g))
    return pl.pallas_call(
        body, grid=(N_GROUPS,), in_specs=[blk], out_specs=blk,
        out_shape=jax.ShapeDtypeStruct((S_, Dn), out_dtype),
        compiler_params=_cparams("parallel"), name=name,
    )(h)


GLU_CHUNK = 512
GLU_HALO = 16
_SQRT_HALF = 0.7071067811865476
_INV_SQRT_2PI = 0.3989422804014327


def _gelu(a):
    return 0.5 * a * (1.0 + lax.erf(a * _SQRT_HALF))


def _gelu_grad(a):
    return 0.5 * (1.0 + lax.erf(a * _SQRT_HALF)) + a * (_INV_SQRT_2PI * jnp.exp(-0.5 * a * a))


def glu_fwd(u, conv_w, conv_b, name):
    S_, F2 = u.shape
    Fh = F2 // 2
    tf = GLU_TILE
    nt = Fh // tf
    ch = _pick(S_, GLU_CHUNK, GLU_HALO)

    def body(a_ref, v_ref, cw_ref, cb_ref, z_ref):
        cw0, cw1, cw2 = cw_ref[0:1, :], cw_ref[1:2, :], cw_ref[2:3, :]
        cb = cb_ref[...]
        for r0 in range(0, S_, ch):
            ext = _rows(a_ref, r0 - GLU_HALO, r0 + ch, S_)
            a0 = ext[GLU_HALO:]
            a1 = pltpu.roll(ext, 1, 0)[GLU_HALO:]
            a2 = pltpu.roll(ext, 2, 0)[GLU_HALO:]
            ac = a2 * cw0 + a1 * cw1 + a0 * cw2 + cb
            z_ref[r0:r0 + ch, :] = (_gelu(ac) * v_ref[r0:r0 + ch, :].astype(F32)).astype(z_ref.dtype)

    return pl.pallas_call(
        body, grid=(nt,),
        in_specs=[pl.BlockSpec((S_, tf), lambda j: (0, j)), pl.BlockSpec((S_, tf), lambda j: (0, j + nt)),
                  pl.BlockSpec((3, tf), lambda j: (0, j)), pl.BlockSpec((1, tf), lambda j: (0, j))],
        out_specs=pl.BlockSpec((S_, tf), lambda j: (0, j)),
        out_shape=jax.ShapeDtypeStruct((S_, Fh), BF16),
        compiler_params=_cparams("parallel"), name=name,
    )(u, u, conv_w, _row(conv_b))


def glu_bwd(u, dz, conv_w, conv_b, name):
    S_, F2 = u.shape
    Fh = F2 // 2
    tf = GLU_TILE
    nt = Fh // tf
    ch = _pick(S_, GLU_CHUNK, GLU_HALO)

    def body(a_ref, v_ref, dz_ref, cw_ref, cb_ref, du_ref, dcw_ref, dcb_ref, da_buf, dv_buf, sems):
        j = pl.program_id(0)
        slot = j % 2

        def writes(step, sl):
            lo = pl.multiple_of(step * tf, tf)
            return (pltpu.make_async_copy(da_buf.at[sl], du_ref.at[:, pl.ds(lo, tf)], sems.at[sl, 0]),
                    pltpu.make_async_copy(dv_buf.at[sl], du_ref.at[:, pl.ds(Fh + lo, tf)], sems.at[sl, 1]))

        @pl.when(j >= 2)
        def _():
            for cp in writes(j - 2, slot):
                cp.wait()

        cw0, cw1, cw2 = cw_ref[0:1, :], cw_ref[1:2, :], cw_ref[2:3, :]
        cb = cb_ref[...]
        acc = [jnp.zeros((1, tf), F32) for _ in range(4)]
        n = ch + GLU_HALO
        for r0 in range(0, S_, ch):
            ext = _rows(a_ref, r0 - GLU_HALO, r0 + n, S_)
            a0 = ext[GLU_HALO:]
            a1 = pltpu.roll(ext, 1, 0)[GLU_HALO:]
            a2 = pltpu.roll(ext, 2, 0)[GLU_HALO:]
            ac = a2 * cw0 + a1 * cw1 + a0 * cw2 + cb
            vv = _rows(v_ref, r0, r0 + n, S_)
            dzv = _rows(dz_ref, r0, r0 + n, S_)
            gl = _gelu(ac)
            dac = dzv * vv * _gelu_grad(ac)
            da = (dac * cw2 + pltpu.roll(dac, n - 1, 0) * cw1 + pltpu.roll(dac, n - 2, 0) * cw0)[:ch]
            da_buf[slot, r0:r0 + ch, :] = da.astype(da_buf.dtype)
            dv_buf[slot, r0:r0 + ch, :] = (dzv[:ch] * gl[:ch]).astype(dv_buf.dtype)
            dc = dac[:ch]
            acc[0] = acc[0] + jnp.sum(dc * a2[:ch], axis=0, keepdims=True)
            acc[1] = acc[1] + jnp.sum(dc * a1[:ch], axis=0, keepdims=True)
            acc[2] = acc[2] + jnp.sum(dc * a0[:ch], axis=0, keepdims=True)
            acc[3] = acc[3] + jnp.sum(dc, axis=0, keepdims=True)
        dcw_ref[0:1, :] = acc[0]
        dcw_ref[1:2, :] = acc[1]
        dcw_ref[2:3, :] = acc[2]
        dcb_ref[...] = acc[3]
        for cp in writes(j, slot):
            cp.start()

        @pl.when(j == nt - 1)
        def _():
            for cp in writes(j, slot):
                cp.wait()
            if nt > 1:
                for cp in writes(j - 1, 1 - slot):
                    cp.wait()

    return pl.pallas_call(
        body, grid=(nt,),
        in_specs=[pl.BlockSpec((S_, tf), lambda j: (0, j)), pl.BlockSpec((S_, tf), lambda j: (0, j + nt)),
                  pl.BlockSpec((S_, tf), lambda j: (0, j)),
                  pl.BlockSpec((3, tf), lambda j: (0, j)), pl.BlockSpec((1, tf), lambda j: (0, j))],
        out_specs=[_ANY, pl.BlockSpec((3, tf), lambda j: (0, j)), pl.BlockSpec((1, tf), lambda j: (0, j))],
        out_shape=[jax.ShapeDtypeStruct((S_, F2), BF16), jax.ShapeDtypeStruct((3, Fh), F32),
                   jax.ShapeDtypeStruct((1, Fh), F32)],
        scratch_shapes=[pltpu.VMEM((2, S_, tf), BF16), pltpu.VMEM((2, S_, tf), BF16), pltpu.SemaphoreType.DMA((2, 2))],
        compiler_params=_cparams("arbitrary"), name=name,
    )(u, u, dz, conv_w, _row(conv_b))


def rope_tables(pos, inv, name, tr=512):
    S_ = pos.shape[0]
    tr = _pick(S_, tr, 8)

    def body(p_ref, inv_ref, c_ref, s1_ref, s2_ref):
        ang = p_ref[...] * inv_ref[...]
        lane = lax.broadcasted_iota(jnp.int32, ang.shape, 1)
        half = QK_ROPE // 2
        cosv, sinv = jnp.cos(ang), jnp.sin(ang)
        c_ref[...] = jnp.where(lane < QK_ROPE, cosv, 0.0)
        s1_ref[...] = jnp.where(lane < half, -sinv, 0.0)
        s2_ref[...] = jnp.where((lane >= half) & (lane < QK_ROPE), sinv, 0.0)

    blk = pl.BlockSpec((tr, LANES), lambda i: (i, 0))
    shp = jax.ShapeDtypeStruct((S_, LANES), F32)
    return pl.pallas_call(
        body, grid=(S_ // tr,),
        in_specs=[pl.BlockSpec((tr, 1), lambda i: (i, 0)), pl.BlockSpec((1, LANES), lambda i: (0, 0))],
        out_specs=[blk, blk, blk], out_shape=[shp, shp, shp],
        compiler_params=_cparams("parallel"), name=name,
    )(pos, inv)


_HALF = QK_ROPE // 2


def _rope(t, c, s1, s2):
    return t * c + pltpu.roll(t, LANES - _HALF, 1) * s1 + pltpu.roll(t, _HALF, 1) * s2


def _rope_t(d, c, s1, s2):
    return d * c + pltpu.roll(d * s1, _HALF, 1) + pltpu.roll(d * s2, LANES - _HALF, 1)


def q_prep(q, tabs, scale, backward, name, tr=512):
    S_, W = q.shape
    tr = _pick(S_, tr, 8)

    def body(q_ref, c_ref, s1_ref, s2_ref, o_ref):
        o_ref[:, 0:LANES] = (q_ref[:, 0:LANES].astype(F32) * scale).astype(o_ref.dtype)
        t = q_ref[:, LANES:2 * LANES].astype(F32)
        fn = _rope_t if backward else _rope
        o_ref[:, LANES:2 * LANES] = (fn(t, c_ref[...], s1_ref[...], s2_ref[...]) * scale).astype(o_ref.dtype)

    blk = pl.BlockSpec((tr, HEAD_PAD), lambda i, h: (i, h))
    tab = pl.BlockSpec((tr, LANES), lambda i, h: (i, 0))
    return pl.pallas_call(
        body, grid=(S_ // tr, W // HEAD_PAD), in_specs=[blk, tab, tab, tab], out_specs=blk,
        out_shape=jax.ShapeDtypeStruct((S_, W), BF16),
        compiler_params=_cparams("parallel", "parallel"), name=name,
    )(q, *tabs)


def k_prep(knv, kv_ext, tabs, name, tr=512):
    S_ = knv.shape[0]
    tr = _pick(S_, tr, 8)

    def body(kn_ref, v_ref, t_ref, c_ref, s1_ref, s2_ref, o_ref, vx_ref):
        o_ref[:, 0:LANES] = kn_ref[...].astype(o_ref.dtype)
        o_ref[:, LANES:2 * LANES] = _rope(t_ref[...], c_ref[...], s1_ref[...], s2_ref[...]).astype(o_ref.dtype)
        vx_ref[:, 0:V_HEAD] = v_ref[...].astype(vx_ref.dtype)
        vx_ref[:, V_HEAD:HEAD_PAD] = jnp.ones((tr, HEAD_PAD - V_HEAD), vx_ref.dtype)

    tab = pl.BlockSpec((tr, LANES), lambda i, h: (i, 0))
    head = pl.BlockSpec((tr, HEAD_PAD), lambda i, h: (i, h))
    shp = jax.ShapeDtypeStruct((S_, N_HEADS * HEAD_PAD), BF16)
    return pl.pallas_call(
        body, grid=(S_ // tr, N_HEADS),
        in_specs=[pl.BlockSpec((tr, LANES), lambda i, h: (i, h)),
                  pl.BlockSpec((tr, V_HEAD), lambda i, h: (i, N_HEADS + h)),
                  pl.BlockSpec((tr, LANES), lambda i, h: (i, KV_RANK // LANES)), tab, tab, tab],
        out_specs=[head, head], out_shape=[shp, shp],
        compiler_params=_cparams("parallel", "parallel"), name=name,
    )(knv, knv, kv_ext, *tabs)


def k_prep_bwd(dk_a, dk_b, dv_a, dv_b, tabs, name, tr=256):
    S_ = dk_a.shape[0]
    tr = _pick(S_, tr, 8)
    HV = N_HEADS * V_HEAD

    def body(ka_ref, kb_ref, va_ref, vb_ref, c_ref, s1_ref, s2_ref, o_ref, t_ref):
        dr = jnp.zeros((tr, LANES), F32)
        for h in range(N_HEADS):
            lo = h * HEAD_PAD
            o_ref[:, h * LANES:(h + 1) * LANES] = (ka_ref[:, lo:lo + LANES] + kb_ref[:, lo:lo + LANES]).astype(o_ref.dtype)
            dr = dr + ka_ref[:, lo + LANES:lo + 2 * LANES] + kb_ref[:, lo + LANES:lo + 2 * LANES]
        o_ref[:, HV:2 * HV] = (va_ref[...] + vb_ref[...]).astype(o_ref.dtype)
        t_ref[...] = _rope_t(dr, c_ref[...], s1_ref[...], s2_ref[...])

    kblk = pl.BlockSpec((tr, N_HEADS * HEAD_PAD), lambda i: (i, 0))
    vblk = pl.BlockSpec((tr, HV), lambda i: (i, 0))
    tab = pl.BlockSpec((tr, LANES), lambda i: (i, 0))
    return pl.pallas_call(
        body, grid=(S_ // tr,), in_specs=[kblk, kblk, vblk, vblk, tab, tab, tab],
        out_specs=[pl.BlockSpec((tr, 2 * HV), lambda i: (i, 0)), tab],
        out_shape=[jax.ShapeDtypeStruct((S_, 2 * HV), BF16), jax.ShapeDtypeStruct((S_, LANES), F32)],
        compiler_params=_cparams("parallel"), name=name,
    )(dk_a, dk_b, dv_a, dv_b, *tabs)


_NEG = -1e30


def attn_fwd(q, k, vx, name):
    S_ = q.shape[0]
    TQ = _pick(S_, ATT_Q_BLOCK, 8)
    TK = _pick(S_, ATT_K_BLOCK, 8)
    assert TQ % TK == 0 or TK % TQ == 0
    HP = ATT_HEADS_PER_STEP
    W = HP * HEAD_PAD

    def body(q_ref, k_ref, v_ref, o_ref, lse_ref):
        i = pl.program_id(1)
        qs = [q_ref[:, h * HEAD_PAD:(h + 1) * HEAD_PAD] for h in range(HP)]

        def step(j, carry, masked):
            start = pl.multiple_of(j * TK, TK)
            out = []
            for h in range(HP):
                m, acc = carry[h]
                cols = slice(h * HEAD_PAD, (h + 1) * HEAD_PAD)
                s = lax.dot_general(qs[h], k_ref[pl.ds(start, TK), cols], _DIMS["nt"], preferred_element_type=F32)
                if masked:
                    rowi = i * TQ + lax.broadcasted_iota(jnp.int32, (TQ, TK), 0)
                    coli = j * TK + lax.broadcasted_iota(jnp.int32, (TQ, TK), 1)
                    s = jnp.where(coli <= rowi, s, _NEG)
                m_new = jnp.maximum(m, jnp.max(s, axis=-1, keepdims=True))
                alpha = jnp.exp(m - m_new)
                p = jnp.exp(s - m_new).astype(BF16)
                acc = alpha * acc + lax.dot_general(p, v_ref[pl.ds(start, TK), cols], _DIMS["nn"],
                                                    preferred_element_type=F32)
                out.append((m_new, acc))
            return tuple(out)

        init = tuple((jnp.full((TQ, 1), _NEG, F32), jnp.zeros((TQ, HEAD_PAD), F32)) for _ in range(HP))
        n_full, n_diag = (i * (TQ // TK), TQ // TK) if TQ >= TK else (i // (TK // TQ), 1)
        carry = lax.fori_loop(0, n_full, functools.partial(step, masked=False), init)
        for d in range(n_diag):
            carry = step(n_full + d, carry, True)
        for h in range(HP):
            m, acc = carry[h]
            l = acc[:, V_HEAD:]
            o_ref[:, h * V_HEAD:(h + 1) * V_HEAD] = (acc[:, :V_HEAD] / l).astype(o_ref.dtype)
            lse_ref[h] = m + jnp.log(jnp.max(l, axis=-1, keepdims=True))

    return pl.pallas_call(
        body, grid=(N_HEADS // HP, S_ // TQ),
        in_specs=[pl.BlockSpec((TQ, W), lambda g, i: (i, g)),
                  pl.BlockSpec((S_, W), lambda g, i: (0, g)),
                  pl.BlockSpec((S_, W), lambda g, i: (0, g))],
        out_specs=[pl.BlockSpec((TQ, HP * V_HEAD), lambda g, i: (i, g)),
                   pl.BlockSpec((HP, TQ, 1), lambda g, i: (g, i, 0))],
        out_shape=[jax.ShapeDtypeStruct((S_, N_HEADS * V_HEAD), BF16), jax.ShapeDtypeStruct((N_HEADS, S_, 1), F32)],
        compiler_params=_cparams("parallel", "parallel"), name=name,
    )(q, k, vx)


def attn_delta(o, do, name, tr=512):
    S_ = o.shape[0]
    tr = _pick(S_, tr, 8)

    def body(o_ref, do_ref, d_ref):
        d_ref[...] = jnp.sum(o_ref[...].astype(F32) * do_ref[...].astype(F32), axis=-1, keepdims=True)

    blk = pl.BlockSpec((tr, V_HEAD), lambda i, h: (i, h))
    return pl.pallas_call(
        body, grid=(S_ // tr, N_HEADS), in_specs=[blk, blk],
        out_specs=pl.BlockSpec((None, tr, 1), lambda i, h: (h, i, 0)),
        out_shape=jax.ShapeDtypeStruct((N_HEADS, S_, 1), F32),
        compiler_params=_cparams("parallel", "parallel"), name=name,
    )(o, do)


def attn_bwd(q, k, vx, do, lse_row, delta_row, name):
    S_ = q.shape[0]
    TK = _pick(S_, ATT_BWD_K_BLOCK, LANES)
    TQ = _pick(S_, ATT_BWD_Q_BLOCK, TK)
    HP = ATT_HEADS_PER_STEP
    W = HP * HEAD_PAD
    ratio = TQ // TK
    nq = S_ // TQ

    def body(q_ref, do_ref, lse_ref, dl_ref, k_ref, v_ref, dq_ref, dk_ref, dv_ref):
        j = pl.program_id(1)

        @pl.when(j == 0)
        def _():
            dq_ref[...] = jnp.zeros_like(dq_ref)

        ks = [k_ref[:, h * HEAD_PAD:(h + 1) * HEAD_PAD] for h in range(HP)]
        vs = [v_ref[:, h * HEAD_PAD:h * HEAD_PAD + V_HEAD] for h in range(HP)]

        def step(i, carry, masked):
            start = pl.multiple_of(i * TQ, TQ)
            out = []
            for h in range(HP):
                dk, dv = carry[h]
                cols = slice(h * HEAD_PAD, (h + 1) * HEAD_PAD)
                qv = q_ref[pl.ds(start, TQ), cols]
                dov = do_ref[pl.ds(start, TQ), h * V_HEAD:(h + 1) * V_HEAD]
                st = lax.dot_general(ks[h], qv, _DIMS["nt"], preferred_element_type=F32)
                pt = jnp.exp(st - lse_ref[h, :, pl.ds(start, TQ)])
                if masked:
                    keyi = j * TK + lax.broadcasted_iota(jnp.int32, (TK, TQ), 0)
                    qryi = i * TQ + lax.broadcasted_iota(jnp.int32, (TK, TQ), 1)
                    pt = jnp.where(keyi <= qryi, pt, 0.0)
                dpt = lax.dot_general(vs[h], dov, _DIMS["nt"], preferred_element_type=F32)
                dst = (pt * (dpt - dl_ref[h, :, pl.ds(start, TQ)])).astype(BF16)
                dv = dv + lax.dot_general(pt.astype(BF16), dov, _DIMS["nn"], preferred_element_type=F32)
                dk = dk + lax.dot_general(dst, qv, _DIMS["nn"], preferred_element_type=F32)
                dq_ref[pl.ds(start, TQ), cols] += lax.dot_general(dst, ks[h], _DIMS["tn"], preferred_element_type=F32)
                out.append((dk, dv))
            return tuple(out)

        init = tuple((jnp.zeros((TK, HEAD_PAD), F32), jnp.zeros((TK, V_HEAD), F32)) for _ in range(HP))
        first = j // ratio
        carry = lax.fori_loop(first + 1, nq, functools.partial(step, masked=False), step(first, init, True))
        for h in range(HP):
            dk_ref[:, h * HEAD_PAD:(h + 1) * HEAD_PAD] = carry[h][0]
            dv_ref[:, h * V_HEAD:(h + 1) * V_HEAD] = carry[h][1]

    return pl.pallas_call(
        body, grid=(N_HEADS // HP, S_ // TK),
        in_specs=[pl.BlockSpec((S_, W), lambda g, j: (0, g)),
                  pl.BlockSpec((S_, HP * V_HEAD), lambda g, j: (0, g)),
                  pl.BlockSpec((HP, 1, S_), lambda g, j: (g, 0, 0)),
                  pl.BlockSpec((HP, 1, S_), lambda g, j: (g, 0, 0)),
                  pl.BlockSpec((TK, W), lambda g, j: (j, g)),
                  pl.BlockSpec((TK, W), lambda g, j: (j, g))],
        out_specs=[pl.BlockSpec((S_, W), lambda g, j: (0, g)),
                   pl.BlockSpec((TK, W), lambda g, j: (j, g)),
                   pl.BlockSpec((TK, HP * V_HEAD), lambda g, j: (j, g))],
        out_shape=[jax.ShapeDtypeStruct((S_, N_HEADS * HEAD_PAD), F32),
                   jax.ShapeDtypeStruct((S_, N_HEADS * HEAD_PAD), F32),
                   jax.ShapeDtypeStruct((S_, N_HEADS * V_HEAD), F32)],
        compiler_params=_cparams("parallel", "arbitrary"), name=name,
    )(q, do, lse_row, delta_row, k, vx)


def mods_fwd(c_all, mod_w, mod_b, name, tn=512):
    L, Dn, E = mod_w.shape
    R = c_all.shape[0]
    tn = _pick(E, tn, LANES)

    def body(c_ref, w_ref, b_ref, o_ref):
        cv = c_ref[...]
        sc = (cv / (1.0 + jnp.exp(-cv))).astype(BF16)
        o_ref[...] = lax.dot_general(sc, w_ref[...].astype(BF16), _DIMS["nn"], preferred_element_type=F32) + b_ref[...]

    return pl.pallas_call(
        body, grid=(L, E // tn),
        in_specs=[pl.BlockSpec((R, Dn), lambda l, j: (0, 0)), pl.BlockSpec((None, Dn, tn), lambda l, j: (l, 0, j)),
                  pl.BlockSpec((None, 1, tn), lambda l, j: (l, 0, j))],
        out_specs=pl.BlockSpec((None, R, tn), lambda l, j: (l, 0, j)),
        out_shape=jax.ShapeDtypeStruct((L, R, E), F32),
        compiler_params=_cparams("parallel", "parallel"), name=name,
    )(c_all, mod_w, mod_b.reshape(L, 1, E))


def _adam_math(w, g, m, v):
    m = ADAM_B1 * m + (1.0 - ADAM_B1) * g
    v = ADAM_B2 * v + (1.0 - ADAM_B2) * (g * g)
    m_hat = m / (1.0 - ADAM_B1 ** ADAM_STEP)
    v_hat = v / (1.0 - ADAM_B2 ** ADAM_STEP)
    delta = -ADAM_LR * (m_hat / (jnp.sqrt(v_hat) + ADAM_EPS) + ADAM_WD * w)
    return delta, m, v


def _as2d(a):
    return a.reshape(-1, a.shape[-1]) if a.ndim != 2 else a


def adamw(w, g, m, v, name):
    shape = w.shape
    w2, g2, m2, v2 = _as2d(w), _as2d(g), _as2d(m), _as2d(v)
    R, C = w2.shape
    tr = _pick(R, max(8, (1 << 18) // C // 8 * 8), 8)

    def body(w_ref, g_ref, m_ref, v_ref, d_ref, mo_ref, vo_ref):
        d, mn, vn = _adam_math(w_ref[...], g_ref[...], m_ref[...], v_ref[...])
        d_ref[...] = d
        mo_ref[...] = mn
        vo_ref[...] = vn

    blk = pl.BlockSpec((tr, C), lambda i: (i, 0))
    shp = jax.ShapeDtypeStruct((R, C), F32)
    outs = pl.pallas_call(
        body, grid=(R // tr,), in_specs=[blk] * 4, out_specs=[blk] * 3, out_shape=[shp] * 3,
        compiler_params=_cparams("parallel"), name=name,
    )(w2, g2, m2, v2)
    return tuple(o.reshape(shape) for o in outs)


def adamw_sum(parts, w, m, v, name):
    P, R, C = parts.shape

    def body(p_ref, w_ref, m_ref, v_ref, g_ref, d_ref, mo_ref, vo_ref):
        g = p_ref[0]
        for k in range(1, P):
            g = g + p_ref[k]
        d, mn, vn = _adam_math(w_ref[...], g, m_ref[...], v_ref[...])
        g_ref[...] = g
        d_ref[...] = d
        mo_ref[...] = mn
        vo_ref[...] = vn

    shp = jax.ShapeDtypeStruct((R, C), F32)
    return pl.pallas_call(body, out_shape=[shp] * 4, compiler_params=_cparams(), name=name)(parts, w, m, v)


def adamw_modw(c_col, dm, w, m, v, name, tr=256, tn=512):
    L, Dn, E = w.shape
    B = c_col.shape[0]
    tr = _pick(Dn, tr, 8)
    tn = _pick(E, tn, LANES)

    def body(c_ref, dm_ref, w_ref, m_ref, v_ref, g_ref, d_ref, mo_ref, vo_ref):
        g = jnp.zeros((tr, tn), F32)
        for b in range(B):
            cv = c_ref[b]
            g = g + (cv / (1.0 + jnp.exp(-cv))) * dm_ref[b:b + 1, :]
        d, mn, vn = _adam_math(w_ref[...], g, m_ref[...], v_ref[...])
        g_ref[...] = g
        d_ref[...] = d
        mo_ref[...] = mn
        vo_ref[...] = vn

    blk = pl.BlockSpec((None, tr, tn), lambda l, i, j: (l, i, j))
    shp = jax.ShapeDtypeStruct((L, Dn, E), F32)
    return pl.pallas_call(
        body, grid=(L, Dn // tr, E // tn),
        in_specs=[pl.BlockSpec((B, tr, 1), lambda l, i, j: (0, i, 0)),
                  pl.BlockSpec((None, B, tn), lambda l, i, j: (l, 0, j)), blk, blk, blk],
        out_specs=[blk] * 4, out_shape=[shp] * 4,
        compiler_params=_cparams("parallel", "parallel", "parallel"), name=name,
    )(c_col, dm, w, m, v)


def add_round(a, b, name, tr=512):
    R, C = a.shape
    tr = _pick(R, tr, 16)

    def body(a_ref, b_ref, o_ref):
        o_ref[...] = (a_ref[...] + b_ref[...].astype(F32)).astype(BF16)

    blk = pl.BlockSpec((tr, C), lambda i: (i, 0))
    return pl.pallas_call(
        body, grid=(R // tr,), in_specs=[blk, blk], out_specs=blk, out_shape=jax.ShapeDtypeStruct((R, C), BF16),
        compiler_params=_cparams("parallel"), name=name,
    )(a, b)


def sum_parts(parts, name, tr=512):
    P, R, C = parts.shape
    tr = _pick(R, tr, 16)

    def body(p_ref, o_ref):
        s = p_ref[0].astype(F32)
        for k in range(1, P):
            s = s + p_ref[k].astype(F32)
        o_ref[...] = s

    return pl.pallas_call(
        body, grid=(R // tr,), in_specs=[pl.BlockSpec((P, tr, C), lambda i: (0, i, 0))],
        out_specs=pl.BlockSpec((tr, C), lambda i: (i, 0)), out_shape=jax.ShapeDtypeStruct((R, C), F32),
        compiler_params=_cparams("parallel"), name=name,
    )(parts)


_ANY = pl.BlockSpec(memory_space=pl.ANY)


def _place():
    return lax.axis_index("x"), lax.axis_index("y"), lax.axis_index("c")


def _flip(v, bit):
    return 1 - v if bit else v


def chip_gather(buf, name):
    def body(in_ref, out_ref, send_sems, recv_sems):
        x, y, c = _place()
        me = 2 * x + y
        sends = []
        for k in range(1, N_CHIPS):
            px, py = _flip(x, k >> 1), _flip(y, k & 1)
            cp = pltpu.make_async_remote_copy(src_ref=in_ref, dst_ref=out_ref.at[me], send_sem=send_sems.at[k - 1],
                                              recv_sem=recv_sems.at[k - 1], device_id=(px, py, c), device_id_type=MESH)
            cp.start()
            sends.append(cp)
        for k in range(1, N_CHIPS):
            px, py = _flip(x, k >> 1), _flip(y, k & 1)
            pltpu.make_async_remote_copy(src_ref=in_ref, dst_ref=out_ref.at[2 * px + py], send_sem=send_sems.at[k - 1],
                                         recv_sem=recv_sems.at[k - 1], device_id=(px, py, c),
                                         device_id_type=MESH).wait_recv()
        for cp in sends:
            cp.wait_send()

    out = pl.pallas_call(
        body, in_specs=[_ANY], out_specs=_ANY,
        out_shape=jax.ShapeDtypeStruct((N_CHIPS,) + buf.shape, buf.dtype),
        scratch_shapes=[pltpu.SemaphoreType.DMA((N_CHIPS - 1,)), pltpu.SemaphoreType.DMA((N_CHIPS - 1,))],
        name=name,
    )(buf)
    return lax.dynamic_update_index_in_dim(out, buf, 2 * lax.axis_index("x") + lax.axis_index("y"), 0)


def chip_all_to_all(buf, name):
    def body(in_ref, out_ref, send_sems, recv_sems):
        x, y, c = _place()
        me = 2 * x + y
        sends = []
        for k in range(1, N_CHIPS):
            px, py = _flip(x, k >> 1), _flip(y, k & 1)
            cp = pltpu.make_async_remote_copy(src_ref=in_ref.at[2 * px + py], dst_ref=out_ref.at[me],
                                              send_sem=send_sems.at[k - 1], recv_sem=recv_sems.at[k - 1],
                                              device_id=(px, py, c), device_id_type=MESH)
            cp.start()
            sends.append(cp)
        for k in range(1, N_CHIPS):
            px, py = _flip(x, k >> 1), _flip(y, k & 1)
            pltpu.make_async_remote_copy(src_ref=in_ref.at[me], dst_ref=out_ref.at[2 * px + py],
                                         send_sem=send_sems.at[k - 1], recv_sem=recv_sems.at[k - 1],
                                         device_id=(px, py, c), device_id_type=MESH).wait_recv()
        for cp in sends:
            cp.wait_send()

    out = pl.pallas_call(
        body, in_specs=[_ANY], out_specs=_ANY, out_shape=jax.ShapeDtypeStruct(buf.shape, buf.dtype),
        scratch_shapes=[pltpu.SemaphoreType.DMA((N_CHIPS - 1,)), pltpu.SemaphoreType.DMA((N_CHIPS - 1,))],
        name=name,
    )(buf)
    me = 2 * lax.axis_index("x") + lax.axis_index("y")
    return lax.dynamic_update_index_in_dim(out, _index(buf, me), me, 0)


def core_gather(buf, name):
    def body(in_ref, out_ref, send_sem, recv_sem):
        x, y, c = _place()
        cp = pltpu.make_async_remote_copy(src_ref=in_ref, dst_ref=out_ref.at[c], send_sem=send_sem, recv_sem=recv_sem,
                                          device_id=(x, y, 1 - c), device_id_type=MESH)
        cp.start()
        pltpu.make_async_remote_copy(src_ref=in_ref, dst_ref=out_ref.at[1 - c], send_sem=send_sem, recv_sem=recv_sem,
                                     device_id=(x, y, 1 - c), device_id_type=MESH).wait_recv()
        cp.wait_send()

    out = pl.pallas_call(
        body, in_specs=[_ANY], out_specs=_ANY, out_shape=jax.ShapeDtypeStruct((2,) + buf.shape, buf.dtype),
        scratch_shapes=[pltpu.SemaphoreType.DMA, pltpu.SemaphoreType.DMA],
        name=name,
    )(buf)
    return lax.dynamic_update_index_in_dim(out, buf, lax.axis_index("c"), 0)


def core_swap(buf, name):
    def body(in_ref, out_ref, send_sem, recv_sem):
        x, y, c = _place()
        cp = pltpu.make_async_remote_copy(src_ref=in_ref, dst_ref=out_ref, send_sem=send_sem, recv_sem=recv_sem,
                                          device_id=(x, y, 1 - c), device_id_type=MESH)
        cp.start()
        cp.wait()

    return pl.pallas_call(
        body, in_specs=[_ANY], out_specs=_ANY, out_shape=jax.ShapeDtypeStruct(buf.shape, buf.dtype),
        scratch_shapes=[pltpu.SemaphoreType.DMA, pltpu.SemaphoreType.DMA],
        name=name,
    )(buf)


def device_gather(buf, name):
    def body(in_ref, out_ref, send_sems, recv_sems, local_sem):
        x, y, c = _place()
        me = 4 * x + 2 * y + c
        mine = pltpu.make_async_copy(in_ref, out_ref.at[me], local_sem)
        mine.start()
        sends = []
        for k in range(1, N_DEV):
            peer = (_flip(x, (k >> 2) & 1), _flip(y, (k >> 1) & 1), _flip(c, k & 1))
            cp = pltpu.make_async_remote_copy(src_ref=in_ref, dst_ref=out_ref.at[me], send_sem=send_sems.at[k - 1],
                                              recv_sem=recv_sems.at[k - 1], device_id=peer, device_id_type=MESH)
            cp.start()
            sends.append(cp)
        for k in range(1, N_DEV):
            peer = (_flip(x, (k >> 2) & 1), _flip(y, (k >> 1) & 1), _flip(c, k & 1))
            pltpu.make_async_remote_copy(src_ref=in_ref, dst_ref=out_ref.at[4 * peer[0] + 2 * peer[1] + peer[2]],
                                         send_sem=send_sems.at[k - 1], recv_sem=recv_sems.at[k - 1], device_id=peer,
                                         device_id_type=MESH).wait_recv()
        for cp in sends:
            cp.wait_send()
        mine.wait()

    return pl.pallas_call(
        body, in_specs=[_ANY], out_specs=_ANY, out_shape=jax.ShapeDtypeStruct((N_DEV,) + buf.shape, buf.dtype),
        scratch_shapes=[pltpu.SemaphoreType.DMA((N_DEV - 1,)), pltpu.SemaphoreType.DMA((N_DEV - 1,)),
                        pltpu.SemaphoreType.DMA],
        name=name,
    )(buf)


def _region(ref, chip_axis=None, chip=None, chip_size=None, half_axis=None, half=None, half_size=None):
    idx = [slice(None)] * len(ref.shape)
    if chip is not None:
        idx[chip_axis] = pl.ds(chip * chip_size, chip_size)
    if half is not None:
        idx[half_axis] = pl.ds(half * half_size, half_size)
    return ref.at[tuple(idx)]


def gather_weights(shards, axes, name, after=()):
    n = len(shards)

    def full_shape(t):
        shp = list(shards[t].shape)
        shp[axes[t][0]] *= N_CHIPS
        return tuple(shp)

    def body(*refs):
        ins, outs = refs[:n], refs[n + len(after):2 * n + len(after)]
        ici_send, ici_recv, d2d_send, d2d_recv, own_send, own_recv = refs[2 * n + len(after):]
        x, y, c = _place()
        me = 2 * x + y

        def part(t, ref, chip, half):
            ca, ha = axes[t]
            return _region(ref, ca, chip, ins[t].shape[ca], ha, half, ins[t].shape[ha] // 2)

        def own(t):
            return pltpu.make_async_remote_copy(src_ref=ins[t], dst_ref=part(t, outs[t], me, None),
                                                send_sem=own_send.at[t], recv_sem=own_recv.at[t],
                                                device_id=(x, y, 1 - c), device_id_type=MESH)

        started = []
        for t in range(n):
            own(t).start()
            started.append(own(t))
        for t in range(n):
            for k in range(1, N_CHIPS):
                px, py = _flip(x, k >> 1), _flip(y, k & 1)
                cp = pltpu.make_async_remote_copy(src_ref=part(t, ins[t], None, c), dst_ref=part(t, outs[t], me, c),
                                                  send_sem=ici_send.at[t, k - 1], recv_sem=ici_recv.at[t, k - 1],
                                                  device_id=(px, py, c), device_id_type=MESH)
                cp.start()
                started.append(cp)
        for t in range(n):
            for k in range(1, N_CHIPS):
                px, py = _flip(x, k >> 1), _flip(y, k & 1)
                got = part(t, outs[t], 2 * px + py, c)
                pltpu.make_async_remote_copy(src_ref=part(t, ins[t], None, c), dst_ref=got,
                                             send_sem=ici_send.at[t, k - 1], recv_sem=ici_recv.at[t, k - 1],
                                             device_id=(px, py, c), device_id_type=MESH).wait_recv()
                fw = pltpu.make_async_remote_copy(src_ref=got, dst_ref=got, send_sem=d2d_send.at[t, k - 1],
                                                  recv_sem=d2d_recv.at[t, k - 1], device_id=(x, y, 1 - c),
                                                  device_id_type=MESH)
                fw.start()
                started.append(fw)
        for t in range(n):
            for k in range(1, N_CHIPS):
                px, py = _flip(x, k >> 1), _flip(y, k & 1)
                theirs = part(t, outs[t], 2 * px + py, 1 - c)
                pltpu.make_async_remote_copy(src_ref=theirs, dst_ref=theirs, send_sem=d2d_send.at[t, k - 1],
                                             recv_sem=d2d_recv.at[t, k - 1], device_id=(x, y, 1 - c),
                                             device_id_type=MESH).wait_recv()
        for t in range(n):
            own(t).wait_recv()
        for cp in started:
            cp.wait_send()

    sem = pltpu.SemaphoreType.DMA((n, N_CHIPS - 1))
    own_sem = pltpu.SemaphoreType.DMA((n,))
    return pl.pallas_call(
        body, in_specs=[_ANY] * (n + len(after)), out_specs=[_ANY] * n,
        out_shape=[jax.ShapeDtypeStruct(full_shape(t), shards[t].dtype) for t in range(n)],
        scratch_shapes=[sem, sem, sem, sem, own_sem, own_sem], name=name,
    )(*shards, *after)


_HBM = pl.BlockSpec(memory_space=pltpu.HBM)
_SEM = pl.BlockSpec(memory_space=pltpu.SEMAPHORE)
_EFFECT = pltpu.SideEffectType.DATAFLOW_SIDE_EFFECTING
WEIGHT_COPIES = N_CHIPS


def _weight_peer(k, x, y, c):
    return (x, y, 1 - c) if k == 0 else (_flip(x, k >> 1), _flip(y, k & 1), c)


def weights_start(shards, items, name, after=()):
    n_sh, n_it = len(shards), len(items)

    def src_of(refs, i):
        t, layer, _ = items[i]
        return refs[t] if layer is None else refs[t].at[layer]

    def land_shape(i):
        t, layer, ca = items[i]
        shp = list(shards[t].shape if layer is None else shards[t].shape[1:])
        shp[ca] *= N_CHIPS
        return tuple(shp)

    def body(*refs):
        shard_refs, land_refs = refs[:n_sh], refs[n_sh:n_sh + n_it]
        first_out = n_sh + n_it + len(after)
        send_sems = refs[first_out:first_out + n_it]
        recv_sems = refs[first_out + n_it:first_out + 2 * n_it]
        token = refs[-1]
        x, y, c = _place()
        me = 2 * x + y
        for i in range(n_it):
            src = src_of(shard_refs, i)
            ca = items[i][2]
            dst = _region(land_refs[i], ca, me, src.shape[ca])
            for k in range(WEIGHT_COPIES):
                pltpu.make_async_remote_copy(src_ref=src, dst_ref=dst, send_sem=send_sems[i], recv_sem=recv_sems[i],
                                             device_id=_weight_peer(k, x, y, c), device_id_type=MESH).start()
        token[...] = jnp.zeros_like(token)

    lands = [pltpu.with_memory_space_constraint(lax.empty(land_shape(i), shards[0].dtype), pltpu.HBM)
             for i in range(n_it)]
    ins = [pltpu.with_memory_space_constraint(a, pltpu.HBM) for a in shards] + lands
    sems = (pltpu.SemaphoreType.DMA(()),) * (2 * n_it)
    outs = pl.pallas_call(
        body, name=name,
        out_shape=sems + tuple(pltpu.HBM(a.shape, a.dtype) for a in ins) + (jax.ShapeDtypeStruct((8, LANES), F32),),
        in_specs=[_HBM] * len(ins) + [_ANY] * len(after),
        out_specs=(_SEM,) * (2 * n_it) + (_HBM,) * len(ins) + (pl.BlockSpec(memory_space=pltpu.VMEM),),
        input_output_aliases={i: 2 * n_it + i for i in range(len(ins))},
        compiler_params=pltpu.CompilerParams(has_side_effects=_EFFECT),
    )(*ins, *after)
    base = 2 * n_it
    return (list(outs[:n_it]), list(outs[n_it:base]), list(outs[base:base + n_sh]),
            list(outs[base + n_sh:base + n_sh + n_it]), outs[-1])


def weights_wait(send_sems, recv_sems, lands, after, keep, name):
    m = len(lands)

    def body(*refs):
        land_refs, send_refs, recv_refs = refs[:m], refs[m:2 * m], refs[2 * m:3 * m]
        x, y, c = _place()
        for j in range(m):
            cp = pltpu.make_async_remote_copy(src_ref=land_refs[j], dst_ref=land_refs[j], send_sem=send_refs[j],
                                              recv_sem=recv_refs[j], device_id=(x, y, 1 - c),
                                              device_id_type=MESH)
            cp.wait_send()
            cp.wait_recv()

    outs = pl.pallas_call(
        body, name=name,
        out_shape=tuple(pltpu.HBM(a.shape, a.dtype) for a in lands),
        in_specs=[_HBM] * m + [_SEM] * (2 * m) + [_ANY] + [_HBM] * len(keep),
        out_specs=(_HBM,) * m,
        input_output_aliases={j: j for j in range(m)},
        compiler_params=pltpu.CompilerParams(has_side_effects=_EFFECT),
    )(*lands, *send_sems, *recv_sems, after, *keep)
    return list(outs)


def reduce_to_sibling(lo, hi, name):
    n = len(lo)

    def body(*refs):
        los, his, outs = refs[:n], refs[n:2 * n], refs[2 * n:3 * n]
        send_sems, recv_sems = refs[3 * n:]
        x, y, c = _place()

        def copy(u, src):
            return pltpu.make_async_remote_copy(src_ref=src, dst_ref=outs[u], send_sem=send_sems.at[u],
                                                recv_sem=recv_sems.at[u], device_id=(x, y, 1 - c), device_id_type=MESH)

        for u in range(n):
            @pl.when(c == 0)
            def _(u=u):
                copy(u, his[u]).start()

            @pl.when(c == 1)
            def _(u=u):
                copy(u, los[u]).start()
        for u in range(n):
            copy(u, los[u]).wait_recv()
        for u in range(n):
            copy(u, los[u]).wait_send()

    return pl.pallas_call(
        body, in_specs=[_ANY] * (2 * n), out_specs=[_ANY] * n,
        out_shape=[jax.ShapeDtypeStruct(a.shape, a.dtype) for a in lo],
        scratch_shapes=[pltpu.SemaphoreType.DMA((n,)), pltpu.SemaphoreType.DMA((n,))], name=name,
    )(*lo, *hi)


def add_selected(lo, hi, other, name, tile_elems=1 << 19):
    R, C = lo.shape
    tr = _pick(R, max(16, tile_elems // C // 16 * 16), 16)

    def body(lo_ref, hi_ref, o_ref, out_ref):
        mine = jnp.where(lax.axis_index("c") == 0, lo_ref[...].astype(F32), hi_ref[...].astype(F32))
        out_ref[...] = (mine + o_ref[...].astype(F32)).astype(out_ref.dtype)

    blk = pl.BlockSpec((tr, C), lambda i: (i, 0))
    return pl.pallas_call(
        body, grid=(R // tr,), in_specs=[blk, blk, blk], out_specs=blk, out_shape=jax.ShapeDtypeStruct((R, C), BF16),
        compiler_params=_cparams("parallel"), name=name,
    )(lo, hi, other)


def scatter_to_chips(pieces, chip_axes, name):
    n = len(pieces)

    def block_shape(u):
        shp = list(pieces[u].shape)
        shp[chip_axes[u]] //= N_CHIPS
        return tuple(shp)

    def body(*refs):
        ins, outs = refs[:n], refs[n:2 * n]
        send_sems, recv_sems = refs[2 * n:]
        x, y, c = _place()
        me = 2 * x + y
        started = []
        for u in range(n):
            size = block_shape(u)[chip_axes[u]]
            for k in range(1, N_CHIPS):
                px, py = _flip(x, k >> 1), _flip(y, k & 1)
                cp = pltpu.make_async_remote_copy(src_ref=_region(ins[u], chip_axes[u], 2 * px + py, size),
                                                  dst_ref=outs[u].at[me], send_sem=send_sems.at[u, k - 1],
                                                  recv_sem=recv_sems.at[u, k - 1], device_id=(px, py, c),
                                                  device_id_type=MESH)
                cp.start()
                started.append(cp)
        for u in range(n):
            size = block_shape(u)[chip_axes[u]]
            for k in range(1, N_CHIPS):
                px, py = _flip(x, k >> 1), _flip(y, k & 1)
                pltpu.make_async_remote_copy(src_ref=_region(ins[u], chip_axes[u], me, size),
                                             dst_ref=outs[u].at[2 * px + py], send_sem=send_sems.at[u, k - 1],
                                             recv_sem=recv_sems.at[u, k - 1], device_id=(px, py, c),
                                             device_id_type=MESH).wait_recv()
        for cp in started:
            cp.wait_send()

    sem = pltpu.SemaphoreType.DMA((n, N_CHIPS - 1))
    return pl.pallas_call(
        body, in_specs=[_ANY] * n, out_specs=[_ANY] * n,
        out_shape=[jax.ShapeDtypeStruct((N_CHIPS,) + block_shape(u), pieces[u].dtype) for u in range(n)],
        scratch_shapes=[sem, sem], name=name,
    )(*pieces)


def scatter_start(pieces, chip_axes, name, after=()):
    n = len(pieces)

    def block_shape(u):
        shp = list(pieces[u].shape)
        shp[chip_axes[u]] //= N_CHIPS
        return tuple(shp)

    def body(*refs):
        ins, land_refs = refs[:n], refs[n:2 * n]
        first_out = 2 * n + len(after)
        send_sems, recv_sems = refs[first_out:first_out + n], refs[first_out + n:first_out + 2 * n]
        token = refs[-1]
        x, y, c = _place()
        me = 2 * x + y
        for u in range(n):
            size = block_shape(u)[chip_axes[u]]
            for k in range(1, N_CHIPS):
                px, py = _flip(x, k >> 1), _flip(y, k & 1)
                pltpu.make_async_remote_copy(src_ref=_region(ins[u], chip_axes[u], 2 * px + py, size),
                                             dst_ref=land_refs[u].at[me], send_sem=send_sems[u], recv_sem=recv_sems[u],
                                             device_id=(px, py, c), device_id_type=MESH).start()
        token[...] = jnp.zeros_like(token)

    lands = [pltpu.with_memory_space_constraint(lax.empty((N_CHIPS,) + block_shape(u), pieces[u].dtype), pltpu.HBM)
             for u in range(n)]
    ins = [pltpu.with_memory_space_constraint(a, pltpu.HBM) for a in pieces] + lands
    sems = (pltpu.SemaphoreType.DMA(()),) * (2 * n)
    outs = pl.pallas_call(
        body, name=name,
        out_shape=sems + tuple(pltpu.HBM(a.shape, a.dtype) for a in ins) + (jax.ShapeDtypeStruct((8, LANES), F32),),
        in_specs=[_HBM] * len(ins) + [_ANY] * len(after),
        out_specs=(_SEM,) * (2 * n) + (_HBM,) * len(ins) + (pl.BlockSpec(memory_space=pltpu.VMEM),),
        input_output_aliases={i: 2 * n + i for i in range(len(ins))},
        compiler_params=pltpu.CompilerParams(has_side_effects=_EFFECT),
    )(*ins, *after)
    return list(outs[:n]), list(outs[n:2 * n]), list(outs[2 * n:3 * n]), list(outs[3 * n:4 * n]), outs[-1]


def scatter_wait(send_sems, recv_sems, lands, pieces, after, name):
    n = len(lands)

    def body(*refs):
        land_refs, send_refs, recv_refs = refs[:n], refs[n:2 * n], refs[2 * n:3 * n]
        x, y, c = _place()
        for u in range(n):
            three = land_refs[u].at[pl.ds(0, N_CHIPS - 1)]
            cp = pltpu.make_async_remote_copy(src_ref=three, dst_ref=three, send_sem=send_refs[u], recv_sem=recv_refs[u],
                                              device_id=(x, y, 1 - c), device_id_type=MESH)
            cp.wait_send()
            cp.wait_recv()

    outs = pl.pallas_call(
        body, name=name,
        out_shape=tuple(pltpu.HBM(a.shape, a.dtype) for a in lands),
        in_specs=[_HBM] * n + [_SEM] * (2 * n) + [_ANY] + [_HBM] * len(pieces),
        out_specs=(_HBM,) * n,
        input_output_aliases={j: j for j in range(n)},
        compiler_params=pltpu.CompilerParams(has_side_effects=_EFFECT),
    )(*lands, *send_sems, *recv_sems, after, *pieces)
    return list(outs)


def gather_halves(parts, slots, out_shapes, name):
    n = len(parts)

    def body(*refs):
        ins, outs = refs[:n], refs[n:n + len(out_shapes)]
        send_sems, recv_sems = refs[n + len(out_shapes):]
        x, y, c = _place()
        started = []
        for u in range(n):
            t, s = slots[u]
            cp = pltpu.make_async_remote_copy(src_ref=ins[u], dst_ref=outs[t].at[s, c], send_sem=send_sems.at[u],
                                              recv_sem=recv_sems.at[u], device_id=(x, y, 1 - c), device_id_type=MESH)
            cp.start()
            started.append(cp)
        for u in range(n):
            t, s = slots[u]
            pltpu.make_async_remote_copy(src_ref=ins[u], dst_ref=outs[t].at[s, 1 - c], send_sem=send_sems.at[u],
                                         recv_sem=recv_sems.at[u], device_id=(x, y, 1 - c),
                                         device_id_type=MESH).wait_recv()
        for cp in started:
            cp.wait_send()

    return pl.pallas_call(
        body, in_specs=[_ANY] * n, out_specs=[_ANY] * len(out_shapes),
        out_shape=[jax.ShapeDtypeStruct(shp, F32) for shp in out_shapes],
        scratch_shapes=[pltpu.SemaphoreType.DMA((n,)), pltpu.SemaphoreType.DMA((n,))], name=name,
    )(*parts)


WEIGHT_ORDER = ["mod_w", "mod_b", "norm1_g", "norm2_g", "pool_w", "pool_b", "pool_scale", "kv_in_g", "w_dkv",
                "ckv_norm_g", "w_uk", "w_uv", "w_dq", "q_norm_g", "w_uq", "w_o", "w_up", "conv_w", "conv_b", "w_down",
                "final_g"]
EXCHANGED = {"w_up": (2, 0), "w_down": (1, 0), "w_o": (1, 0), "w_uq": (2, 0), "w_dq": (1, 0), "pool_w": (2, 0),
             "w_dkv": (0, 1), "w_uk": (1, 0), "w_uv": (1, 0)}
SMALL_SHARDED = {"conv_w": 2, "pool_b": 1, "pool_scale": 1}
REPLICATED = ["mod_b", "norm1_g", "norm2_g", "kv_in_g", "ckv_norm_g", "q_norm_g", "conv_b", "final_g"]


def _padded(n, align):
    return -(-n // align) * align


def _flat_pad(parts, total):
    flat = jnp.concatenate(parts, axis=-1)
    pad = total - flat.shape[-1]
    if pad:
        flat = jnp.concatenate([flat, jnp.zeros(flat.shape[:-1] + (pad,), flat.dtype)], axis=-1)
    return flat


def _split_shards(full, axis):
    shp = full.shape
    t = full.reshape(shp[:axis] + (N_CHIPS, shp[axis] // N_CHIPS) + shp[axis + 1:])
    return jnp.moveaxis(t, axis, 0).reshape(N_CHIPS, -1)


def _join_shards(rows, shard_shape, axis):
    t = jnp.moveaxis(rows.reshape((N_CHIPS,) + tuple(shard_shape)), 0, axis)
    return t.reshape(tuple(shard_shape[:axis]) + (N_CHIPS * shard_shape[axis],) + tuple(shard_shape[axis + 1:]))


def _index(a, i, axis=0):
    return lax.dynamic_index_in_dim(a, i, axis, keepdims=False)


def kernel(x, c, positions, mod_w, mod_b, norm1_g, norm2_g, pool_w, pool_b, pool_scale, kv_in_g, w_dkv, ckv_norm_g, w_uk, w_uv, w_dq, q_norm_g, w_uq, w_o, w_up, conv_w, conv_b, w_down, final_g, loss_target, m_mod_w, m_mod_b, m_norm1_g, m_norm2_g, m_pool_w, m_pool_b, m_pool_scale, m_kv_in_g, m_w_dkv, m_ckv_norm_g, m_w_uk, m_w_uv, m_w_dq, m_q_norm_g, m_w_uq, m_w_o, m_w_up, m_conv_w, m_conv_b, m_w_down, m_final_g, v_mod_w, v_mod_b, v_norm1_g, v_norm2_g, v_pool_w, v_pool_b, v_pool_scale, v_kv_in_g, v_w_dkv, v_ckv_norm_g, v_w_uk, v_w_uv, v_w_dq, v_q_norm_g, v_w_uq, v_w_o, v_w_up, v_conv_w, v_conv_b, v_w_down, v_final_g):
    given = dict(locals())
    W = {n: given[n] for n in WEIGHT_ORDER}
    M1 = {n: given["m_" + n] for n in WEIGHT_ORDER}
    V2 = {n: given["v_" + n] for n in WEIGHT_ORDER}
    xi, yi, ci = lax.axis_index("x"), lax.axis_index("y"), lax.axis_index("c")
    chip = 2 * xi + yi
    dev = 4 * xi + 2 * yi + ci
    x0 = x[0]
    S_, D = x0.shape
    Fh = conv_b.shape[1]
    E = mod_b.shape[1]
    Es = E // N_CHIPS
    zD = jnp.zeros((D,), F32)

    c_all = device_gather(c, "gather_c").reshape(N_DEV, D)
    c_pad = jnp.concatenate([c_all, jnp.zeros((16 - N_DEV, D), F32)], axis=0)
    mod_b_mine = lax.dynamic_slice_in_dim(mod_b, chip * Es, Es, axis=1)
    mods_part = mods_fwd(c_pad, mod_w, mod_b_mine, "mods_fwd")
    mods_all = chip_gather(mods_part, "gather_mods")
    mods = jnp.swapaxes(_index(mods_all, dev, axis=2), 0, 1).reshape(DEPTH, E)
    mod = [[mods[l, k * D:(k + 1) * D] for k in range(6)] for l in range(DEPTH)]

    full = {}
    ssz = {n: math.prod(W[n].shape) for n in SMALL_SHARDED}
    Tw = _padded(sum(ssz.values()), 8 * PACK_COLS)
    small_rows = chip_gather(_flat_pad([W[n].reshape(-1) for n in SMALL_SHARDED], Tw).reshape(-1, PACK_COLS),
                             "gather_small_w").reshape(N_CHIPS, Tw)
    off = 0
    for n, axis in SMALL_SHARDED.items():
        full[n] = _join_shards(small_rows[:, off:off + ssz[n]], W[n].shape, axis)
        off += ssz[n]

    names = list(EXCHANGED)
    shards = [W[n].astype(BF16) for n in names]
    n_mla = DEPTH - N_A
    first_axes = {"w_up": (1, 0), "w_down": (0, 1), "pool_w": (1, 0)}
    first = gather_weights([shards[names.index(n)][0] for n in first_axes], list(first_axes.values()), "gather_weights0",
                           after=[mods, small_rows])
    for n, arr in zip(first_axes, first):
        full[(n, 0)] = arr
    items, groups = [], []

    def group(entries):
        groups.append(list(range(len(items), len(items) + len(entries))))
        for n, layer in entries:
            ca = EXCHANGED[n][0] - (0 if layer is None else 1)
            items.append((names.index(n), layer, 0 if n == "w_dkv" else ca))

    for l in range(1, N_A):
        group([("w_up", l), ("w_down", l), ("pool_w", l)])
    for j in range(n_mla):
        head = [("w_dkv", None), ("w_uk", None), ("w_uv", None)] if j == 0 else []
        group(head + [("w_dq", j), ("w_uq", j), ("w_o", j), ("w_up", N_A + j), ("w_down", N_A + j)])
    w_send, w_recv, shards_thru, lands, _ = weights_start(shards, items, "weights_start", after=first)

    def weights_ready(g, after):
        keep = shards_thru if g == len(groups) - 1 else []
        got = weights_wait([w_send[i] for i in groups[g]], [w_recv[i] for i in groups[g]], [lands[i] for i in groups[g]],
                           after, keep, f"weights_wait{g}")
        for i, arr in zip(groups[g], got):
            t, layer, _ = items[i]
            full[(names[t], 0 if layer is None else layer)] = arr

    q_rank = W["w_uq"].shape[1]
    kv_w = KV_RANK + QK_ROPE

    def uq_ext(j):
        wq = full[("w_uq", j)].reshape(q_rank, N_HEADS, QK_HEAD)
        return jnp.concatenate([wq, jnp.zeros((q_rank, N_HEADS, HEAD_PAD - QK_HEAD), BF16)],
                               axis=2).reshape(q_rank, N_HEADS * HEAD_PAD)


    half = QK_ROPE // 2
    inv = 1.0 / (ROPE_THETA ** (jnp.arange(0, QK_ROPE, 2, dtype=F32) / QK_ROPE))
    inv_row = jnp.concatenate([inv, inv, jnp.zeros((LANES - 2 * half,), F32)]).reshape(1, LANES)
    tabs = rope_tables(positions[0].astype(F32).reshape(S_, 1), inv_row, "rope_tables")
    att_scale = QK_HEAD ** -0.5

    saved = []
    xcur = x0
    kv_saved = None
    K = VX = knv = None
    for l in range(DEPTH):
        sh1, sc1, g1, sh2, sc2, g2 = mod[l]
        st = {"xin": xcur}
        if l:
            weights_ready(l - 1, xcur)
        if l == N_A:
            w_dkv_ext = jnp.concatenate([full[("w_dkv", 0)], jnp.zeros((D, KV_RANK + LANES - kv_w), BF16)], axis=1)
            w_ukv = jnp.concatenate([full[("w_uk", 0)], full[("w_uv", 0)]], axis=1)
            xn = norm_fwd(xcur, kv_in_g, zD, zD, BF16, "kvin_fwd")
            kv_ext = mm(xn, w_dkv_ext, "nn", F32, "dkv_mm")
            lat = kv_ext[:, :KV_RANK]
            zk = jnp.zeros((KV_RANK,), F32)
            ckv = norm_fwd(lat, ckv_norm_g, zk, zk, BF16, "ckv_fwd")
            knv = mm(ckv, w_ukv, "nn", BF16, "ukv_mm")
            K, VX = k_prep(knv, kv_ext, tabs, "k_prep")
            kv_saved = {"x": xcur, "xn": xn, "lat": lat, "ckv": ckv}
        if l < N_A:
            h1 = norm_fwd(xcur, norm1_g[l], sc1, sh1, F32, f"norm1_fwd{l}")
            st["pooled"] = _pool_call(h1, BF16, f"pool_fwd{l}", False)
            st["cs"] = g1 * full["pool_scale"][l]
            st["ypre"], xmid = gmm(st["pooled"], full[("pool_w", l)], "nn", F32, f"pool_mm{l}", bias=full["pool_b"][l],
                                   res=xcur, colscale=st["cs"])
        else:
            j = l - N_A
            st["h1"] = norm_fwd(xcur, norm1_g[l], sc1, sh1, BF16, f"norm1_fwd{l}")
            st["ql"] = mm(st["h1"], full[("w_dq", j)], "nn", F32, f"dq_mm{l}")
            st["cq"] = norm_fwd(st["ql"], q_norm_g[j], jnp.zeros_like(q_norm_g[j]), jnp.zeros_like(q_norm_g[j]), BF16,
                                f"qnorm_fwd{l}")
            st["w_uq_ext"] = uq_ext(j)
            qe = mm(st["cq"], st["w_uq_ext"], "nn", F32, f"uq_mm{l}")
            st["Q"] = q_prep(qe, tabs, att_scale, False, f"q_prep{l}")
            st["o"], lse = attn_fwd(st["Q"], K, VX, f"attn_fwd{l}")
            st["lse"] = lse.reshape(N_HEADS, 1, S_)
            st["y"], xmid = mm(st["o"], full[("w_o", j)], "nn", F32, f"wo_mm{l}", res=xcur, colscale=g1)
        st["xmid"] = xmid
        st["h2"] = norm_fwd(xmid, norm2_g[l], sc2, sh2, BF16, f"norm2_fwd{l}")
        st["u"] = mm(st["h2"], full[("w_up", l)], "nn", BF16, f"up_mm{l}")
        st["z"] = glu_fwd(st["u"], full["conv_w"][l], conv_b[l], f"glu_fwd{l}")
        st["f"], xcur = mm(st["z"], full[("w_down", l)], "nn", F32, f"down_mm{l}", tk=1408, res=xmid, colscale=g2)
        saved.append(st)

    dx, d_final_g, loss_part = loss_head(xcur, final_g, loss_target[0], "loss_head")
    loss = lax.psum(loss_part[0, 0], ("x", "y", "c"))

    def begin_reduce(tensors, first_slot, tag):
        units = []
        for n in tensors:
            ca = EXCHANGED[n][0]
            if W[n].ndim > 2:
                n_slots = W[n].shape[0] // 2
                for sl in range(first_slot if n_slots > 1 else 0, first_slot + 1 if n_slots > 1 else 1):
                    units.append((n, sl, G[(n, 2 * sl)], G[(n, 2 * sl + 1)], ca - 1))
            elif n == "w_dkv":
                g4 = G[(n, 0)].reshape(N_CHIPS, 2, -1, kv_w)
                units.append((n, 0, g4[:, 0], g4[:, 1], 0))
            else:
                rows_half = W[n].shape[0] // 2
                units.append((n, 0, G[(n, 0)][:rows_half], G[(n, 0)][rows_half:], ca))
        lo = [u[2] for u in units]
        hi = [u[3] for u in units]
        theirs = reduce_to_sibling(lo, hi, f"reduce_cores_{tag}")
        sums = [add_selected(l_.reshape(-1, l_.shape[-1]), h_.reshape(-1, l_.shape[-1]), t_.reshape(-1, l_.shape[-1]),
                             f"reduce_cores_add_{tag}{i}").reshape(l_.shape)
                for i, (l_, h_, t_) in enumerate(zip(lo, hi, theirs))]
        return units, sums, [u[4] for u in units]

    G = {}
    dmods = [None] * DEPTH
    d_norm1 = [None] * DEPTH
    d_norm2 = [None] * DEPTH
    d_conv_b = [None] * DEPTH
    d_qnorm = [None] * n_mla
    dkv_acc = []
    for l in reversed(range(DEPTH)):
        sh1, sc1, g1, sh2, sc2, g2 = mod[l]
        st = saved[l]
        df, a2, _ = gate_bwd(dx, st["f"], g2, f"gate2_bwd{l}")
        dz = mm(df, full[("w_down", l)], "nt", BF16, f"down_dx{l}")
        G[("w_down", l)] = mm(st["z"], df, "tn", BF16, f"down_dw{l}")
        du, dcw, dcb = glu_bwd(st["u"], dz, full["conv_w"][l], conv_b[l], f"glu_bwd{l}")
        G[("conv_w", l)] = dcw
        d_conv_b[l] = dcb[0]
        dh2 = mm(du, full[("w_up", l)], "nt", BF16, f"up_dx{l}", tk=1408)
        G[("w_up", l)] = mm(st["h2"], du, "tn", BF16, f"up_dw{l}")
        dxmid, s1, s2 = norm_bwd(st["xmid"], norm2_g[l], sc2, dh2, dx, f"norm2_bwd{l}")
        dsh2, dsc2, d_norm2[l] = s1[0], s2[0] * norm2_g[l], s2[0] * (1.0 + sc2)
        if l < N_A:
            dyp, a1, csum = gate_bwd(dxmid, st["ypre"], st["cs"], f"gate1_bwd{l}")
            dg1 = full["pool_scale"][l] * a1[0]
            G[("pool_scale", l)] = g1 * a1[0]
            G[("pool_b", l)] = st["cs"] * csum[0]
            dpooled = gmm(dyp, full[("pool_w", l)], "nt", F32, f"pool_dx{l}")
            G[("pool_w", l)] = gmm(st["pooled"], dyp, "tn", BF16, f"pool_dw{l}")
            dh1 = _pool_call(dpooled, F32, f"pool_bwd{l}", True)
        else:
            j = l - N_A
            dy, a1, _ = gate_bwd(dxmid, st["y"], g1, f"gate1_bwd{l}")
            dg1 = a1[0]
            do = mm(dy, full[("w_o", j)], "nt", BF16, f"wo_dx{l}")
            G[("w_o", j)] = mm(st["o"], dy, "tn", BF16, f"wo_dw{l}")
            delta = attn_delta(st["o"], do, f"attn_delta{l}").reshape(N_HEADS, 1, S_)
            dQ, dK, dV = attn_bwd(st["Q"], K, VX, do, st["lse"], delta, f"attn_bwd{l}")
            dkv_acc.append((dK, dV))
            dqe = q_prep(dQ, tabs, att_scale, True, f"q_prep_bwd{l}")
            dcq = mm(dqe, st["w_uq_ext"], "nt", F32, f"uq_dx{l}")
            G[("w_uq", j)] = mm(st["cq"], dqe, "tn", BF16, f"uq_dw{l}").reshape(q_rank, N_HEADS, HEAD_PAD)[
                :, :, :QK_HEAD].reshape(q_rank, N_HEADS * QK_HEAD)
            zq = jnp.zeros_like(q_norm_g[j])
            dql, _, s2q = norm_bwd(st["ql"], q_norm_g[j], zq, dcq, None, f"qnorm_bwd{l}")
            d_qnorm[j] = s2q[0]
            dh1 = mm(dql, full[("w_dq", j)], "nt", BF16, f"dq_dx{l}")
            G[("w_dq", j)] = mm(st["h1"], dql, "tn", BF16, f"dq_dw{l}")
        dx, s1, s2 = norm_bwd(st["xin"], norm1_g[l], sc1, dh1, dxmid, f"norm1_bwd{l}")
        dsh1, dsc1, d_norm1[l] = s1[0], s2[0] * norm1_g[l], s2[0] * (1.0 + sc1)
        dmods[l] = jnp.concatenate([dsh1, dsc1, dg1, dsh2, dsc2, a2[0]])
        if l == N_A:
            (dk_a, dv_a), (dk_b, dv_b) = dkv_acc
            dknv, d_tk = k_prep_bwd(dk_a, dk_b, dv_a, dv_b, tabs, "k_prep_bwd")
            dckv = mm(dknv, w_ukv, "nt", F32, "ukv_dx")
            d_ukv = mm(kv_saved["ckv"], dknv, "tn", BF16, "ukv_dw")
            G[("w_uk", 0)], G[("w_uv", 0)] = d_ukv[:, :N_HEADS * QK_NOPE], d_ukv[:, N_HEADS * QK_NOPE:]
            zk = jnp.zeros((KV_RANK,), F32)
            dlat, _, s2c = norm_bwd(kv_saved["lat"], ckv_norm_g, zk, dckv, None, "ckv_bwd")
            d_ckv_g = s2c[0]
            dkv_ext = jnp.concatenate([dlat, d_tk], axis=1)
            dxn = mm(dkv_ext, w_dkv_ext, "nt", BF16, "dkv_dx")
            G[("w_dkv", 0)] = mm(kv_saved["xn"], dkv_ext, "tn", BF16, "dkv_dw")[:, :kv_w]
            dx, _, s2k = norm_bwd(kv_saved["x"], kv_in_g, zD, dxn, dx, "kvin_bwd")
            d_kvin_g = s2k[0]
            e_units, e_sums, e_axes = begin_reduce([n for n in EXCHANGED if n != "pool_w"], 1, "early")
            e_send, e_recv, e_pieces, e_lands, e_token = scatter_start(e_sums, e_axes, "reduce_chips_start")
            early = (e_units, e_send, e_recv, e_lands, e_pieces, e_axes)
            mod[l - 1][5] = mod[l - 1][5] + e_token[0, 0]

    small = {"mod_b": jnp.stack(dmods), "norm1_g": jnp.stack(d_norm1), "norm2_g": jnp.stack(d_norm2),
             "kv_in_g": d_kvin_g, "ckv_norm_g": d_ckv_g, "q_norm_g": jnp.stack(d_qnorm),
             "conv_b": jnp.stack(d_conv_b), "final_g": d_final_g[0]}
    extra = {n: jnp.stack([G[(n, i)] for i in range(W[n].shape[0])]) for n in SMALL_SHARDED}
    ssizes = {n: math.prod(W[n].shape) for n in REPLICATED}
    esizes = {n: math.prod(extra[n].shape) for n in SMALL_SHARDED}
    Ts = _padded(sum(ssizes.values()) + sum(esizes.values()), 8 * PACK_COLS)

    def pack_small(d, tail=()):
        return _flat_pad([d[n].reshape(-1) for n in REPLICATED] + [t.reshape(-1) for t in tail],
                         Ts).reshape(Ts // PACK_COLS, PACK_COLS)

    parts = device_gather(pack_small(small, [extra[n] for n in SMALL_SHARDED]), "gather_small")

    l_units, l_sums, l_axes = begin_reduce([n for n in EXCHANGED if W[n].ndim > 2 and W[n].shape[0] == DEPTH] + ["pool_w"],
                                           0, "late")
    l_send, l_recv, l_pieces, l_lands, l_token = scatter_start(l_sums, l_axes, "reduce_chips_late_start", after=[parts])
    parts = parts + l_token[0, 0]

    grads, deltas, new_m, new_v = {}, {}, {}, {}
    outs = adamw_sum(parts, pack_small(W), pack_small(M1), pack_small(V2), "adamw_small")
    off = 0
    for n in REPLICATED:
        for dst, o in zip((grads, deltas, new_m, new_v), outs):
            dst[n] = o.reshape(-1)[off:off + ssizes[n]].reshape(W[n].shape)
        off += ssizes[n]
    for n, axis in SMALL_SHARDED.items():
        g_full = outs[0].reshape(-1)[off:off + esizes[n]].reshape(extra[n].shape)
        off += esizes[n]
        size = W[n].shape[axis]
        grads[n] = lax.dynamic_slice_in_dim(g_full, chip * size, size, axis=axis)
        deltas[n], new_m[n], new_v[n] = adamw(W[n], grads[n], M1[n], V2[n], f"adamw_{n}")

    dm_all = parts.reshape(N_DEV, -1)[:, :DEPTH * E].reshape(N_DEV, DEPTH, E)
    dm_mine = jnp.swapaxes(lax.dynamic_slice_in_dim(dm_all, chip * Es, Es, axis=2), 0, 1)
    grads["mod_w"], deltas["mod_w"], new_m["mod_w"], new_v["mod_w"] = adamw_modw(
        c_all.reshape(N_DEV, D, 1), dm_mine, mod_w, m_mod_w, v_mod_w, "adamw_mod_w")

    def finish_reduce(pieces, axes, got, tag):
        out = []
        for i, (sm, ax, g4) in enumerate(zip(pieces, axes, got)):
            size = sm.shape[ax] // N_CHIPS
            g4 = lax.dynamic_update_index_in_dim(g4, lax.dynamic_slice_in_dim(sm, chip * size, size, axis=ax), chip, 0)
            blk = g4.shape[1:]
            out.append(sum_parts(g4.reshape(N_CHIPS, -1, blk[-1]), f"reduce_chips_add_{tag}{i}").reshape(blk))
        return out

    e_units, e_send, e_recv, e_lands, e_pieces, e_axes = early
    early_got = scatter_wait(e_send, e_recv, e_lands, e_pieces, dx, "reduce_chips_wait")
    reduced = finish_reduce(e_pieces, e_axes, early_got, "early")
    late_got = scatter_wait(l_send, l_recv, l_lands, l_pieces, new_v["mod_w"], "reduce_chips_late_wait")
    reduced += finish_reduce(l_pieces, l_axes, late_got, "late")
    units = e_units + l_units
    slots, out_shapes = [], []
    for n in EXCHANGED:
        mine = [i for i, u in enumerate(units) if u[0] == n]
        out_shapes.append((len(mine), 2) + reduced[mine[0]].shape)
        slots += [(len(out_shapes) - 1, units[i][1]) for i in mine]
    order = [i for n in EXCHANGED for i, u in enumerate(units) if u[0] == n]
    halves = gather_halves([reduced[i] for i in order], slots, out_shapes, "reduce_gather")
    for ti, n in enumerate(EXCHANGED):
        g = halves[ti]
        for i, u in enumerate(units):
            if u[0] == n:
                g = lax.dynamic_update_slice(g, reduced[i][None, None], (u[1], ci) + (0,) * reduced[i].ndim)
        grads[n] = g.reshape(W[n].shape)
        deltas[n], new_m[n], new_v[n] = adamw(W[n], grads[n], M1[n], V2[n], f"adamw_{n}")

    return (loss, dx.reshape(x.shape), *[grads[n] for n in WEIGHT_ORDER], *[deltas[n] for n in WEIGHT_ORDER],
            *[new_m[n] for n in WEIGHT_ORDER], *[new_v[n] for n in WEIGHT_ORDER])
```

```python
import functools
import math

import jax
import jax.numpy as jnp
from jax import lax
from jax.experimental import pallas as pl
from jax.experimental.pallas import tpu as pltpu

F32 = jnp.float32
BF16 = jnp.bfloat16
MESH = pl.DeviceIdType.MESH

DEPTH = 4
N_A = 2
POOL_WINDOWS = (2, 4, 8, 16)
N_GROUPS = 4
N_HEADS = 8
QK_NOPE = 128
QK_ROPE = 64
V_HEAD = 128
QK_HEAD = QK_NOPE + QK_ROPE
HEAD_PAD = 256
KV_RANK = 256
ROPE_THETA = 10000.0
EPS = 1e-6
ADAM_LR = 0.001
ADAM_B1 = 0.9
ADAM_B2 = 0.999
ADAM_EPS = 1e-08
ADAM_WD = 0.01
ADAM_STEP = 10

N_CHIPS = 4
N_DEV = 8
LANES = 128
PACK_COLS = 1024
VMEM_LIMIT = 56 * 1024 * 1024
GLU_TILE = 256
ATT_BWD_K_BLOCK = 512
ATT_BWD_Q_BLOCK = 512
ATT_Q_BLOCK = 1024
ATT_K_BLOCK = 512
ATT_HEADS_PER_STEP = 2


def _cparams(*sem):
    return pltpu.CompilerParams(dimension_semantics=sem if sem else None, vmem_limit_bytes=VMEM_LIMIT)


def _pick(n, target, mult):
    best = None
    d = mult
    while d <= min(n, target):
        if n % d == 0:
            best = d
        d += mult
    return n if best is None else best


def _row(v):
    return v.reshape(1, -1).astype(F32)


_DIMS = {"nn": (((1,), (0,)), ((), ())), "nt": (((1,), (1,)), ((), ())), "tn": (((0,), (0,)), ((), ()))}


def _mm_body(mode, nk, has_bias, has_res):
    def body(*refs):
        a_ref, b_ref = refs[0], refs[1]
        pos = 2
        bias_ref = res_ref = cs_ref = None
        if has_bias:
            bias_ref = refs[pos]
            pos += 1
        if has_res:
            res_ref, cs_ref = refs[pos], refs[pos + 1]
            pos += 2
        o_ref = refs[pos]
        pos += 1
        o2_ref = None
        if has_res:
            o2_ref = refs[pos]
            pos += 1
        acc_ref = refs[pos] if nk > 1 else None
        k = pl.program_id(2)
        part = lax.dot_general(a_ref[...].astype(BF16), b_ref[...].astype(BF16), _DIMS[mode],
                               preferred_element_type=F32)

        def finish(y):
            if has_bias:
                y = y + bias_ref[...]
            o_ref[...] = y.astype(o_ref.dtype)
            if has_res:
                o2_ref[...] = res_ref[...] + cs_ref[...] * y

        if nk == 1:
            finish(part)
            return

        @pl.when(k == 0)
        def _():
            acc_ref[...] = part

        @pl.when((k > 0) & (k < nk - 1))
        def _():
            acc_ref[...] += part

        @pl.when(k == nk - 1)
        def _():
            finish(acc_ref[...] + part)

    return body


def mm(a, b, mode, out_dtype, name, *, tm=1408, tn=1408, tk=1024, bias=None, res=None, colscale=None, layer=None):
    bshape = b.shape if layer is None else b.shape[1:]
    if mode == "nn":
        (M, K), N = a.shape, bshape[1]
    elif mode == "nt":
        (M, K), N = a.shape, bshape[0]
    else:
        (K, M), N = a.shape, bshape[1]
    tm = _pick(M, tm, LANES if mode == "tn" else 8)
    tn = _pick(N, tn, LANES)
    tk = _pick(K, tk, LANES) if mode != "tn" else _pick(K, tk, 8)
    nk = K // tk
    a_spec = {"nn": pl.BlockSpec((tm, tk), lambda i, j, k: (i, k)),
              "nt": pl.BlockSpec((tm, tk), lambda i, j, k: (i, k)),
              "tn": pl.BlockSpec((tk, tm), lambda i, j, k: (k, i))}[mode]
    b_blk, b_map = {"nn": ((tk, tn), lambda i, j, k: (k, j)),
                    "nt": ((tn, tk), lambda i, j, k: (j, k)),
                    "tn": ((tk, tn), lambda i, j, k: (k, j))}[mode]
    if layer is None:
        b_spec = pl.BlockSpec(b_blk, b_map)
    else:
        b_spec = pl.BlockSpec((None,) + b_blk, lambda i, j, k: (layer,) + b_map(i, j, k))
    o_spec = pl.BlockSpec((tm, tn), lambda i, j, k: (i, j))
    v_spec = pl.BlockSpec((1, tn), lambda i, j, k: (0, j))
    in_specs, args = [a_spec, b_spec], [a, b]
    if bias is not None:
        in_specs.append(v_spec)
        args.append(_row(bias))
    out_shape = [jax.ShapeDtypeStruct((M, N), out_dtype)]
    out_specs = [o_spec]
    if res is not None:
        in_specs += [o_spec, v_spec]
        args += [res, _row(colscale)]
        out_shape.append(jax.ShapeDtypeStruct((M, N), F32))
        out_specs.append(o_spec)
    outs = pl.pallas_call(
        _mm_body(mode, nk, bias is not None, res is not None),
        grid=(M // tm, N // tn, nk),
        in_specs=in_specs, out_specs=out_specs, out_shape=out_shape,
        scratch_shapes=[pltpu.VMEM((tm, tn), F32)] if nk > 1 else [],
        compiler_params=_cparams("parallel", "parallel", "arbitrary"),
        name=name,
    )(*args)
    return outs if res is not None else outs[0]


def gmm(a, w, mode, out_dtype, name, *, bias=None, res=None, colscale=None, tr=512):
    S_ = a.shape[0]
    G = N_GROUPS
    C = a.shape[1] // G
    tr = _pick(S_, tr, 8)
    nr = S_ // tr
    if mode == "tn":
        def body(a_ref, b_ref, o_ref, acc_ref):
            i = pl.program_id(1)

            @pl.when(i == 0)
            def _():
                acc_ref[...] = jnp.zeros_like(acc_ref)

            acc_ref[...] += lax.dot_general(a_ref[...].astype(BF16), b_ref[...].astype(BF16), _DIMS["tn"],
                                            preferred_element_type=F32)

            @pl.when(i == nr - 1)
            def _():
                o_ref[...] = acc_ref[...].astype(o_ref.dtype)

        blk = pl.BlockSpec((tr, C), lambda g, i: (i, g))
        return pl.pallas_call(
            body, grid=(G, nr), in_specs=[blk, blk],
            out_specs=pl.BlockSpec((None, C, C), lambda g, i: (g, 0, 0)),
            out_shape=jax.ShapeDtypeStruct((G, C, C), out_dtype),
            scratch_shapes=[pltpu.VMEM((C, C), F32)],
            compiler_params=_cparams("parallel", "arbitrary"), name=name,
        )(a, w)

    has_bias, has_res = bias is not None, res is not None

    def body(*refs):
        a_ref, w_ref = refs[0], refs[1]
        pos = 2
        if has_bias:
            bias_ref = refs[pos]
            pos += 1
        if has_res:
            res_ref, cs_ref = refs[pos], refs[pos + 1]
            pos += 2
        o_ref = refs[pos]
        y = lax.dot_general(a_ref[...].astype(BF16), w_ref[...].astype(BF16), _DIMS[mode],
                            preferred_element_type=F32)
        if has_bias:
            y = y + bias_ref[...]
        o_ref[...] = y.astype(o_ref.dtype)
        if has_res:
            refs[pos + 1][...] = res_ref[...] + cs_ref[...] * y

    blk = pl.BlockSpec((tr, C), lambda i, g: (i, g))
    vec = pl.BlockSpec((1, C), lambda i, g: (0, g))
    in_specs = [blk, pl.BlockSpec((None, C, C), lambda i, g: (g, 0, 0))]
    args = [a, w]
    if has_bias:
        in_specs.append(vec)
        args.append(_row(bias))
    out_shape = [jax.ShapeDtypeStruct(a.shape, out_dtype)]
    out_specs = [blk]
    if has_res:
        in_specs += [blk, vec]
        args += [res, _row(colscale)]
        out_shape.append(jax.ShapeDtypeStruct(a.shape, F32))
        out_specs.append(blk)
    outs = pl.pallas_call(
        body, grid=(nr, G), in_specs=in_specs, out_specs=out_specs, out_shape=out_shape,
        compiler_params=_cparams("parallel", "parallel"), name=name,
    )(*args)
    return outs if has_res else outs[0]


def norm_fwd(x, g, sc, sh, out_dtype, name, tr=512):
    S_, Dn = x.shape
    tr = _pick(S_, tr, 8)

    def body(x_ref, g_ref, sc_ref, sh_ref, o_ref):
        xv = x_ref[...]
        r = lax.rsqrt(jnp.mean(xv * xv, axis=-1, keepdims=True) + EPS)
        o_ref[...] = (((xv * r) * g_ref[...]) * (1.0 + sc_ref[...]) + sh_ref[...]).astype(o_ref.dtype)

    blk = pl.BlockSpec((tr, Dn), lambda i: (i, 0))
    vec = pl.BlockSpec((1, Dn), lambda i: (0, 0))
    return pl.pallas_call(
        body, grid=(S_ // tr,), in_specs=[blk, vec, vec, vec], out_specs=blk,
        out_shape=jax.ShapeDtypeStruct((S_, Dn), out_dtype),
        compiler_params=_cparams("parallel"), name=name,
    )(x, _row(g), _row(sc), _row(sh))


def norm_bwd(x, g, sc, dh, dres, name, gate=None, tr=512):
    S_, Dn = x.shape
    tr = _pick(S_, tr, 8)
    has_res = dres is not None
    has_gate = gate is not None

    def body(*refs):
        x_ref, g_ref, sc_ref, dh_ref = refs[:4]
        pos = 4
        if has_res:
            dres_ref = refs[pos]
            pos += 1
        if has_gate:
            y_ref, cs_ref = refs[pos:pos + 2]
            pos += 2
        dx_ref, s1_ref, s2_ref = refs[pos:pos + 3]
        if has_gate:
            d_ref, a_ref, c_ref = refs[pos + 3:pos + 6]
        i = pl.program_id(0)

        @pl.when(i == 0)
        def _():
            s1_ref[...] = jnp.zeros_like(s1_ref)
            s2_ref[...] = jnp.zeros_like(s2_ref)
            if has_gate:
                a_ref[...] = jnp.zeros_like(a_ref)
                c_ref[...] = jnp.zeros_like(c_ref)

        xv = x_ref[...]
        r = lax.rsqrt(jnp.mean(xv * xv, axis=-1, keepdims=True) + EPS)
        n = xv * r
        dhv = dh_ref[...].astype(F32)
        dn = dhv * (g_ref[...] * (1.0 + sc_ref[...]))
        dx = r * (dn - n * jnp.mean(dn * n, axis=-1, keepdims=True))
        if has_res:
            dx = dx + dres_ref[...]
        dx_ref[...] = dx
        s1_ref[...] += jnp.sum(dhv, axis=0, keepdims=True)
        s2_ref[...] += jnp.sum(dhv * n, axis=0, keepdims=True)
        if has_gate:
            d_ref[...] = (dx * cs_ref[...]).astype(d_ref.dtype)
            a_ref[...] += jnp.sum(dx * y_ref[...].astype(F32), axis=0, keepdims=True)
            c_ref[...] += jnp.sum(dx, axis=0, keepdims=True)

    blk = pl.BlockSpec((tr, Dn), lambda i: (i, 0))
    vec = pl.BlockSpec((1, Dn), lambda i: (0, 0))
    in_specs, args = [blk, vec, vec, blk], [x, _row(g), _row(sc), dh]
    if has_res:
        in_specs.append(blk)
        args.append(dres)
    vshape = jax.ShapeDtypeStruct((1, Dn), F32)
    out_specs = [blk, vec, vec]
    out_shape = [jax.ShapeDtypeStruct((S_, Dn), F32), vshape, vshape]
    if has_gate:
        in_specs += [blk, vec]
        args += [gate[0], _row(gate[1])]
        out_specs += [blk, vec, vec]
        out_shape += [jax.ShapeDtypeStruct((S_, Dn), BF16), vshape, vshape]
    return pl.pallas_call(
        body, grid=(S_ // tr,), in_specs=in_specs, out_specs=out_specs, out_shape=out_shape,
        compiler_params=_cparams("arbitrary"), name=name,
    )(*args)


def gate_bwd(dx, y, colscale, name, tr=512):
    S_, Dn = dx.shape
    tr = _pick(S_, tr, 8)

    def body(dx_ref, y_ref, cs_ref, d_ref, a_ref, c_ref):
        i = pl.program_id(0)

        @pl.when(i == 0)
        def _():
            a_ref[...] = jnp.zeros_like(a_ref)
            c_ref[...] = jnp.zeros_like(c_ref)

        dxv = dx_ref[...]
        d_ref[...] = (dxv * cs_ref[...]).astype(d_ref.dtype)
        a_ref[...] += jnp.sum(dxv * y_ref[...].astype(F32), axis=0, keepdims=True)
        c_ref[...] += jnp.sum(dxv, axis=0, keepdims=True)

    blk = pl.BlockSpec((tr, Dn), lambda i: (i, 0))
    vec = pl.BlockSpec((1, Dn), lambda i: (0, 0))
    vshape = jax.ShapeDtypeStruct((1, Dn), F32)
    return pl.pallas_call(
        body, grid=(S_ // tr,), in_specs=[blk, blk, vec], out_specs=[blk, vec, vec],
        out_shape=[jax.ShapeDtypeStruct((S_, Dn), BF16), vshape, vshape],
        compiler_params=_cparams("arbitrary"), name=name,
    )(dx, y, _row(colscale))


def loss_head(x, g, target, name, tr=512):
    S_, Dn = x.shape
    tr = _pick(S_, tr, 8)

    def body(x_ref, g_ref, t_ref, dx_ref, dg_ref, loss_ref):
        i = pl.program_id(0)

        @pl.when(i == 0)
        def _():
            dg_ref[...] = jnp.zeros_like(dg_ref)
            loss_ref[...] = jnp.zeros_like(loss_ref)

        xv = x_ref[...]
        r = lax.rsqrt(jnp.mean(xv * xv, axis=-1, keepdims=True) + EPS)
        n = xv * r
        e = n * g_ref[...] - t_ref[...]
        loss_ref[...] += 0.5 * jnp.sum(jnp.mean(e * e, axis=-1, keepdims=True), axis=0, keepdims=True)
        dy = e * (1.0 / Dn)
        dg_ref[...] += jnp.sum(dy * n, axis=0, keepdims=True)
        dn = dy * g_ref[...]
        dx_ref[...] = r * (dn - n * jnp.mean(dn * n, axis=-1, keepdims=True))

    blk = pl.BlockSpec((tr, Dn), lambda i: (i, 0))
    vec = pl.BlockSpec((1, Dn), lambda i: (0, 0))
    one = pl.BlockSpec((1, 1), lambda i: (0, 0))
    return pl.pallas_call(
        body, grid=(S_ // tr,), in_specs=[blk, vec, blk], out_specs=[blk, vec, one],
        out_shape=[jax.ShapeDtypeStruct((S_, Dn), F32), jax.ShapeDtypeStruct((1, Dn), F32),
                   jax.ShapeDtypeStruct((1, 1), F32)],
        compiler_params=_cparams("arbitrary"), name=name,
    )(x, _row(g), target)


POOL_HALO = 16
POOL_CHUNK = 512


def _rows(ref, lo, hi, n_rows):
    parts = []
    if lo < 0:
        parts.append(jnp.zeros((-lo, ref.shape[1]), F32))
    parts.append(ref[max(lo, 0):min(hi, n_rows), :].astype(F32))
    if hi > n_rows:
        parts.append(jnp.zeros((hi - n_rows, ref.shape[1]), F32))
    return parts[0] if len(parts) == 1 else jnp.concatenate(parts, axis=0)


def _window_sum(e, w, back):
    n = e.shape[0]
    s, width = e, 1
    while width < w:
        s = s + pltpu.roll(s, width if back else n - width, 0)
        width *= 2
    return s


def _pool_call(h, out_dtype, name, backward):
    S_, Dn = h.shape
    C = Dn // N_GROUPS
    ch = _pick(S_, POOL_CHUNK, 8)

    def body(h_ref, o_ref):
        g = pl.program_id(0)
        for gi, w in enumerate(POOL_WINDOWS):
            @pl.when(g == gi)
            def _(w=w):
                for r0 in range(0, S_, ch):
                    t = (r0 + lax.broadcasted_iota(jnp.int32, (ch, C), 0)).astype(F32)
                    cnt = jnp.minimum(t + 1.0, float(w))
                    if not backward:
                        ext = _rows(h_ref, r0 - POOL_HALO, r0 + ch, S_)
                        cur = ext[POOL_HALO:]
                        mean = _window_sum(ext, w, True)[POOL_HALO:] / cnt
                        o_ref[r0:r0 + ch, :] = (mean - cur).astype(o_ref.dtype)
                    else:
                        ext = _rows(h_ref, r0, r0 + ch + POOL_HALO, S_)
                        text = (r0 + lax.broadcasted_iota(jnp.int32, (ch + POOL_HALO, C), 0)).astype(F32)
                        e = ext / jnp.minimum(text + 1.0, float(w))
                        o_ref[r0:r0 + ch, :] = (_window_sum(e, w, False)[:ch] - ext[:ch]).astype(o_ref.dtype)

    blk = pl.BlockSpec((S_, C), lambda g: (0, g))
    return pl.pallas_call(
        body, grid=(N_GROUPS,), in_specs=[blk], out_specs=blk,
        out_shape=jax.ShapeDtypeStruct((S_, Dn), out_dtype),
        compiler_params=_cparams("parallel"), name=name,
    )(h)


GLU_CHUNK = 512
GLU_HALO = 16
_SQRT_HALF = 0.7071067811865476
_INV_SQRT_2PI = 0.3989422804014327


def _gelu(a):
    return 0.5 * a * (1.0 + lax.erf(a * _SQRT_HALF))


def _gelu_grad(a):
    return 0.5 * (1.0 + lax.erf(a * _SQRT_HALF)) + a * (_INV_SQRT_2PI * jnp.exp(-0.5 * a * a))


def glu_fwd(u, conv_w, conv_b, name):
    S_, F2 = u.shape
    Fh = F2 // 2
    tf = GLU_TILE
    nt = Fh // tf
    ch = _pick(S_, GLU_CHUNK, GLU_HALO)

    def body(a_ref, v_ref, cw_ref, cb_ref, z_ref):
        cw0, cw1, cw2 = cw_ref[0:1, :], cw_ref[1:2, :], cw_ref[2:3, :]
        cb = cb_ref[...]
        for r0 in range(0, S_, ch):
            ext = _rows(a_ref, r0 - GLU_HALO, r0 + ch, S_)
            a0 = ext[GLU_HALO:]
            a1 = pltpu.roll(ext, 1, 0)[GLU_HALO:]
            a2 = pltpu.roll(ext, 2, 0)[GLU_HALO:]
            ac = a2 * cw0 + a1 * cw1 + a0 * cw2 + cb
            z_ref[r0:r0 + ch, :] = (_gelu(ac) * v_ref[r0:r0 + ch, :].astype(F32)).astype(z_ref.dtype)

    return pl.pallas_call(
        body, grid=(nt,),
        in_specs=[pl.BlockSpec((S_, tf), lambda j: (0, j)), pl.BlockSpec((S_, tf), lambda j: (0, j + nt)),
                  pl.BlockSpec((3, tf), lambda j: (0, j)), pl.BlockSpec((1, tf), lambda j: (0, j))],
        out_specs=pl.BlockSpec((S_, tf), lambda j: (0, j)),
        out_shape=jax.ShapeDtypeStruct((S_, Fh), BF16),
        compiler_params=_cparams("parallel"), name=name,
    )(u, u, conv_w, _row(conv_b))


def glu_bwd(u, dz, conv_w, conv_b, name):
    S_, F2 = u.shape
    Fh = F2 // 2
    tf = GLU_TILE
    nt = Fh // tf
    ch = _pick(S_, GLU_CHUNK, GLU_HALO)

    def body(a_ref, v_ref, dz_ref, cw_ref, cb_ref, du_ref, dcw_ref, dcb_ref, da_buf, dv_buf, sems):
        j = pl.program_id(0)
        slot = j % 2

        def writes(step, sl):
            lo = pl.multiple_of(step * tf, tf)
            return (pltpu.make_async_copy(da_buf.at[sl], du_ref.at[:, pl.ds(lo, tf)], sems.at[sl, 0]),
                    pltpu.make_async_copy(dv_buf.at[sl], du_ref.at[:, pl.ds(Fh + lo, tf)], sems.at[sl, 1]))

        @pl.when(j >= 2)
        def _():
            for cp in writes(j - 2, slot):
                cp.wait()

        cw0, cw1, cw2 = cw_ref[0:1, :], cw_ref[1:2, :], cw_ref[2:3, :]
        cb = cb_ref[...]
        acc = [jnp.zeros((1, tf), F32) for _ in range(4)]
        n = ch + GLU_HALO
        for r0 in range(0, S_, ch):
            ext = _rows(a_ref, r0 - GLU_HALO, r0 + n, S_)
            a0 = ext[GLU_HALO:]
            a1 = pltpu.roll(ext, 1, 0)[GLU_HALO:]
            a2 = pltpu.roll(ext, 2, 0)[GLU_HALO:]
            ac = a2 * cw0 + a1 * cw1 + a0 * cw2 + cb
            vv = _rows(v_ref, r0, r0 + n, S_)
            dzv = _rows(dz_ref, r0, r0 + n, S_)
            gl = _gelu(ac)
            dac = dzv * vv * _gelu_grad(ac)
            da = (dac * cw2 + pltpu.roll(dac, n - 1, 0) * cw1 + pltpu.roll(dac, n - 2, 0) * cw0)[:ch]
            da_buf[slot, r0:r0 + ch, :] = da.astype(da_buf.dtype)
            dv_buf[slot, r0:r0 + ch, :] = (dzv[:ch] * gl[:ch]).astype(dv_buf.dtype)
            dc = dac[:ch]
            acc[0] = acc[0] + jnp.sum(dc * a2[:ch], axis=0, keepdims=True)
            acc[1] = acc[1] + jnp.sum(dc * a1[:ch], axis=0, keepdims=True)
            acc[2] = acc[2] + jnp.sum(dc * a0[:ch], axis=0, keepdims=True)
            acc[3] = acc[3] + jnp.sum(dc, axis=0, keepdims=True)
        dcw_ref[0:1, :] = acc[0]
        dcw_ref[1:2, :] = acc[1]
        dcw_ref[2:3, :] = acc[2]
        dcb_ref[...] = acc[3]
        for cp in writes(j, slot):
            cp.start()

        @pl.when(j == nt - 1)
        def _():
            for cp in writes(j, slot):
                cp.wait()
            if nt > 1:
                for cp in writes(j - 1, 1 - slot):
                    cp.wait()

    return pl.pallas_call(
        body, grid=(nt,),
        in_specs=[pl.BlockSpec((S_, tf), lambda j: (0, j)), pl.BlockSpec((S_, tf), lambda j: (0, j + nt)),
                  pl.BlockSpec((S_, tf), lambda j: (0, j)),
                  pl.BlockSpec((3, tf), lambda j: (0, j)), pl.BlockSpec((1, tf), lambda j: (0, j))],
        out_specs=[_ANY, pl.BlockSpec((3, tf), lambda j: (0, j)), pl.BlockSpec((1, tf), lambda j: (0, j))],
        out_shape=[jax.ShapeDtypeStruct((S_, F2), BF16), jax.ShapeDtypeStruct((3, Fh), F32),
                   jax.ShapeDtypeStruct((1, Fh), F32)],
        scratch_shapes=[pltpu.VMEM((2, S_, tf), BF16), pltpu.VMEM((2, S_, tf), BF16), pltpu.SemaphoreType.DMA((2, 2))],
        compiler_params=_cparams("arbitrary"), name=name,
    )(u, u, dz, conv_w, _row(conv_b))


def rope_tables(pos, inv, name, tr=512):
    S_ = pos.shape[0]
    tr = _pick(S_, tr, 8)

    def body(p_ref, inv_ref, c_ref, s1_ref, s2_ref):
        ang = p_ref[...] * inv_ref[...]
        lane = lax.broadcasted_iota(jnp.int32, ang.shape, 1)
        half = QK_ROPE // 2
        cosv, sinv = jnp.cos(ang), jnp.sin(ang)
        c_ref[...] = jnp.where(lane < QK_ROPE, cosv, 0.0)
        s1_ref[...] = jnp.where(lane < half, -sinv, 0.0)
        s2_ref[...] = jnp.where((lane >= half) & (lane < QK_ROPE), sinv, 0.0)

    blk = pl.BlockSpec((tr, LANES), lambda i: (i, 0))
    shp = jax.ShapeDtypeStruct((S_, LANES), F32)
    return pl.pallas_call(
        body, grid=(S_ // tr,),
        in_specs=[pl.BlockSpec((tr, 1), lambda i: (i, 0)), pl.BlockSpec((1, LANES), lambda i: (0, 0))],
        out_specs=[blk, blk, blk], out_shape=[shp, shp, shp],
        compiler_params=_cparams("parallel"), name=name,
    )(pos, inv)


_HALF = QK_ROPE // 2


def _rope(t, c, s1, s2):
    return t * c + pltpu.roll(t, LANES - _HALF, 1) * s1 + pltpu.roll(t, _HALF, 1) * s2


def _rope_t(d, c, s1, s2):
    return d * c + pltpu.roll(d * s1, _HALF, 1) + pltpu.roll(d * s2, LANES - _HALF, 1)


def q_prep(q, tabs, scale, backward, name, tr=512):
    S_, W = q.shape
    tr = _pick(S_, tr, 8)

    def body(q_ref, c_ref, s1_ref, s2_ref, o_ref):
        o_ref[:, 0:LANES] = (q_ref[:, 0:LANES].astype(F32) * scale).astype(o_ref.dtype)
        t = q_ref[:, LANES:2 * LANES].astype(F32)
        fn = _rope_t if backward else _rope
        o_ref[:, LANES:2 * LANES] = (fn(t, c_ref[...], s1_ref[...], s2_ref[...]) * scale).astype(o_ref.dtype)

    blk = pl.BlockSpec((tr, HEAD_PAD), lambda i, h: (i, h))
    tab = pl.BlockSpec((tr, LANES), lambda i, h: (i, 0))
    return pl.pallas_call(
        body, grid=(S_ // tr, W // HEAD_PAD), in_specs=[blk, tab, tab, tab], out_specs=blk,
        out_shape=jax.ShapeDtypeStruct((S_, W), BF16),
        compiler_params=_cparams("parallel", "parallel"), name=name,
    )(q, *tabs)


def k_prep(knv, kv_ext, tabs, name, tr=512):
    S_ = knv.shape[0]
    tr = _pick(S_, tr, 8)

    def body(kn_ref, v_ref, t_ref, c_ref, s1_ref, s2_ref, o_ref, vx_ref):
        o_ref[:, 0:LANES] = kn_ref[...].astype(o_ref.dtype)
        o_ref[:, LANES:2 * LANES] = _rope(t_ref[...], c_ref[...], s1_ref[...], s2_ref[...]).astype(o_ref.dtype)
        vx_ref[:, 0:V_HEAD] = v_ref[...].astype(vx_ref.dtype)
        vx_ref[:, V_HEAD:HEAD_PAD] = jnp.ones((tr, HEAD_PAD - V_HEAD), vx_ref.dtype)

    tab = pl.BlockSpec((tr, LANES), lambda i, h: (i, 0))
    head = pl.BlockSpec((tr, HEAD_PAD), lambda i, h: (i, h))
    shp = jax.ShapeDtypeStruct((S_, N_HEADS * HEAD_PAD), BF16)
    return pl.pallas_call(
        body, grid=(S_ // tr, N_HEADS),
        in_specs=[pl.BlockSpec((tr, LANES), lambda i, h: (i, h)),
                  pl.BlockSpec((tr, V_HEAD), lambda i, h: (i, N_HEADS + h)),
                  pl.BlockSpec((tr, LANES), lambda i, h: (i, KV_RANK // LANES)), tab, tab, tab],
        out_specs=[head, head], out_shape=[shp, shp],
        compiler_params=_cparams("parallel", "parallel"), name=name,
    )(knv, knv, kv_ext, *tabs)


def k_prep_bwd(dk_a, dk_b, dv_a, dv_b, tabs, name, tr=256):
    S_ = dk_a.shape[0]
    tr = _pick(S_, tr, 8)
    HV = N_HEADS * V_HEAD

    def body(ka_ref, kb_ref, va_ref, vb_ref, c_ref, s1_ref, s2_ref, o_ref, t_ref):
        dr = jnp.zeros((tr, LANES), F32)
        for h in range(N_HEADS):
            lo = h * HEAD_PAD
            o_ref[:, h * LANES:(h + 1) * LANES] = (ka_ref[:, lo:lo + LANES] + kb_ref[:, lo:lo + LANES]).astype(o_ref.dtype)
            dr = dr + ka_ref[:, lo + LANES:lo + 2 * LANES] + kb_ref[:, lo + LANES:lo + 2 * LANES]
        o_ref[:, HV:2 * HV] = (va_ref[...] + vb_ref[...]).astype(o_ref.dtype)
        t_ref[...] = _rope_t(dr, c_ref[...], s1_ref[...], s2_ref[...])

    kblk = pl.BlockSpec((tr, N_HEADS * HEAD_PAD), lambda i: (i, 0))
    vblk = pl.BlockSpec((tr, HV), lambda i: (i, 0))
    tab = pl.BlockSpec((tr, LANES), lambda i: (i, 0))
    return pl.pallas_call(
        body, grid=(S_ // tr,), in_specs=[kblk, kblk, vblk, vblk, tab, tab, tab],
        out_specs=[pl.BlockSpec((tr, 2 * HV), lambda i: (i, 0)), tab],
        out_shape=[jax.ShapeDtypeStruct((S_, 2 * HV), BF16), jax.ShapeDtypeStruct((S_, LANES), F32)],
        compiler_params=_cparams("parallel"), name=name,
    )(dk_a, dk_b, dv_a, dv_b, *tabs)


_NEG = -1e30


def attn_fwd(q, k, vx, name):
    S_ = q.shape[0]
    TQ = _pick(S_, ATT_Q_BLOCK, 8)
    TK = _pick(S_, ATT_K_BLOCK, 8)
    assert TQ % TK == 0 or TK % TQ == 0
    HP = ATT_HEADS_PER_STEP
    W = HP * HEAD_PAD

    def body(q_ref, k_ref, v_ref, o_ref, lse_ref):
        i = pl.program_id(1)
        qs = [q_ref[:, h * HEAD_PAD:(h + 1) * HEAD_PAD] for h in range(HP)]

        def step(j, carry, masked):
            start = pl.multiple_of(j * TK, TK)
            out = []
            for h in range(HP):
                m, acc = carry[h]
                cols = slice(h * HEAD_PAD, (h + 1) * HEAD_PAD)
                s = lax.dot_general(qs[h], k_ref[pl.ds(start, TK), cols], _DIMS["nt"], preferred_element_type=F32)
                if masked:
                    rowi = i * TQ + lax.broadcasted_iota(jnp.int32, (TQ, TK), 0)
                    coli = j * TK + lax.broadcasted_iota(jnp.int32, (TQ, TK), 1)
                    s = jnp.where(coli <= rowi, s, _NEG)
                m_new = jnp.maximum(m, jnp.max(s, axis=-1, keepdims=True))
                alpha = jnp.exp(m - m_new)
                p = jnp.exp(s - m_new).astype(BF16)
                acc = alpha * acc + lax.dot_general(p, v_ref[pl.ds(start, TK), cols], _DIMS["nn"],
                                                    preferred_element_type=F32)
                out.append((m_new, acc))
            return tuple(out)

        init = tuple((jnp.full((TQ, 1), _NEG, F32), jnp.zeros((TQ, HEAD_PAD), F32)) for _ in range(HP))
        n_full, n_diag = (i * (TQ // TK), TQ // TK) if TQ >= TK else (i // (TK // TQ), 1)
        carry = lax.fori_loop(0, n_full, functools.partial(step, masked=False), init)
        for d in range(n_diag):
            carry = step(n_full + d, carry, True)
        for h in range(HP):
            m, acc = carry[h]
            l = acc[:, V_HEAD:]
            o_ref[:, h * V_HEAD:(h + 1) * V_HEAD] = (acc[:, :V_HEAD] / l).astype(o_ref.dtype)
            lse_ref[h] = m + jnp.log(jnp.max(l, axis=-1, keepdims=True))

    return pl.pallas_call(
        body, grid=(N_HEADS // HP, S_ // TQ),
        in_specs=[pl.BlockSpec((TQ, W), lambda g, i: (i, g)),
                  pl.BlockSpec((S_, W), lambda g, i: (0, g)),
                  pl.BlockSpec((S_, W), lambda g, i: (0, g))],
        out_specs=[pl.BlockSpec((TQ, HP * V_HEAD), lambda g, i: (i, g)),
                   pl.BlockSpec((HP, TQ, 1), lambda g, i: (g, i, 0))],
        out_shape=[jax.ShapeDtypeStruct((S_, N_HEADS * V_HEAD), BF16), jax.ShapeDtypeStruct((N_HEADS, S_, 1), F32)],
        compiler_params=_cparams("parallel", "parallel"), name=name,
    )(q, k, vx)


def attn_delta(o, do, name, tr=512):
    S_ = o.shape[0]
    tr = _pick(S_, tr, 8)

    def body(o_ref, do_ref, d_ref):
        d_ref[...] = jnp.sum(o_ref[...].astype(F32) * do_ref[...].astype(F32), axis=-1, keepdims=True)

    blk = pl.BlockSpec((tr, V_HEAD), lambda i, h: (i, h))
    return pl.pallas_call(
        body, grid=(S_ // tr, N_HEADS), in_specs=[blk, blk],
        out_specs=pl.BlockSpec((None, tr, 1), lambda i, h: (h, i, 0)),
        out_shape=jax.ShapeDtypeStruct((N_HEADS, S_, 1), F32),
        compiler_params=_cparams("parallel", "parallel"), name=name,
    )(o, do)


def attn_bwd(q, k, vx, do, lse_row, delta_row, name):
    S_ = q.shape[0]
    TK = _pick(S_, ATT_BWD_K_BLOCK, LANES)
    TQ = _pick(S_, ATT_BWD_Q_BLOCK, TK)
    HP = ATT_HEADS_PER_STEP
    W = HP * HEAD_PAD
    ratio = TQ // TK
    nq = S_ // TQ

    def body(q_ref, do_ref, lse_ref, dl_ref, k_ref, v_ref, dq_ref, dk_ref, dv_ref):
        j = pl.program_id(1)

        @pl.when(j == 0)
        def _():
            dq_ref[...] = jnp.zeros_like(dq_ref)

        ks = [k_ref[:, h * HEAD_PAD:(h + 1) * HEAD_PAD] for h in range(HP)]
        vs = [v_ref[:, h * HEAD_PAD:h * HEAD_PAD + V_HEAD] for h in range(HP)]

        def step(i, carry, masked):
            start = pl.multiple_of(i * TQ, TQ)
            out = []
            for h in range(HP):
                dk, dv = carry[h]
                cols = slice(h * HEAD_PAD, (h + 1) * HEAD_PAD)
                qv = q_ref[pl.ds(start, TQ), cols]
                dov = do_ref[pl.ds(start, TQ), h * V_HEAD:(h + 1) * V_HEAD]
                st = lax.dot_general(ks[h], qv, _DIMS["nt"], preferred_element_type=F32)
                pt = jnp.exp(st - lse_ref[h, :, pl.ds(start, TQ)])
                if masked:
                    keyi = j * TK + lax.broadcasted_iota(jnp.int32, (TK, TQ), 0)
                    qryi = i * TQ + lax.broadcasted_iota(jnp.int32, (TK, TQ), 1)
                    pt = jnp.where(keyi <= qryi, pt, 0.0)
                dpt = lax.dot_general(vs[h], dov, _DIMS["nt"], preferred_element_type=F32)
                dst = (pt * (dpt - dl_ref[h, :, pl.ds(start, TQ)])).astype(BF16)
                dv = dv + lax.dot_general(pt.astype(BF16), dov, _DIMS["nn"], preferred_element_type=F32)
                dk = dk + lax.dot_general(dst, qv, _DIMS["nn"], preferred_element_type=F32)
                dq_ref[pl.ds(start, TQ), cols] += lax.dot_general(dst, ks[h], _DIMS["tn"], preferred_element_type=F32)
                out.append((dk, dv))
            return tuple(out)

        init = tuple((jnp.zeros((TK, HEAD_PAD), F32), jnp.zeros((TK, V_HEAD), F32)) for _ in range(HP))
        first = j // ratio
        carry = lax.fori_loop(first + 1, nq, functools.partial(step, masked=False), step(first, init, True))
        for h in range(HP):
            dk_ref[:, h * HEAD_PAD:(h + 1) * HEAD_PAD] = carry[h][0]
            dv_ref[:, h * V_HEAD:(h + 1) * V_HEAD] = carry[h][1]

    return pl.pallas_call(
        body, grid=(N_HEADS // HP, S_ // TK),
        in_specs=[pl.BlockSpec((S_, W), lambda g, j: (0, g)),
                  pl.BlockSpec((S_, HP * V_HEAD), lambda g, j: (0, g)),
                  pl.BlockSpec((HP, 1, S_), lambda g, j: (g, 0, 0)),
                  pl.BlockSpec((HP, 1, S_), lambda g, j: (g, 0, 0)),
                  pl.BlockSpec((TK, W), lambda g, j: (j, g)),
                  pl.BlockSpec((TK, W), lambda g, j: (j, g))],
        out_specs=[pl.BlockSpec((S_, W), lambda g, j: (0, g)),
                   pl.BlockSpec((TK, W), lambda g, j: (j, g)),
                   pl.BlockSpec((TK, HP * V_HEAD), lambda g, j: (j, g))],
        out_shape=[jax.ShapeDtypeStruct((S_, N_HEADS * HEAD_PAD), F32),
                   jax.ShapeDtypeStruct((S_, N_HEADS * HEAD_PAD), F32),
                   jax.ShapeDtypeStruct((S_, N_HEADS * V_HEAD), F32)],
        compiler_params=_cparams("parallel", "arbitrary"), name=name,
    )(q, do, lse_row, delta_row, k, vx)


def mods_fwd(c_all, mod_w, mod_b, name, tn=512):
    L, Dn, E = mod_w.shape
    R = c_all.shape[0]
    tn = _pick(E, tn, LANES)

    def body(c_ref, w_ref, b_ref, o_ref):
        cv = c_ref[...]
        sc = (cv / (1.0 + jnp.exp(-cv))).astype(BF16)
        o_ref[...] = lax.dot_general(sc, w_ref[...].astype(BF16), _DIMS["nn"], preferred_element_type=F32) + b_ref[...]

    return pl.pallas_call(
        body, grid=(L, E // tn),
        in_specs=[pl.BlockSpec((R, Dn), lambda l, j: (0, 0)), pl.BlockSpec((None, Dn, tn), lambda l, j: (l, 0, j)),
                  pl.BlockSpec((None, 1, tn), lambda l, j: (l, 0, j))],
        out_specs=pl.BlockSpec((None, R, tn), lambda l, j: (l, 0, j)),
        out_shape=jax.ShapeDtypeStruct((L, R, E), F32),
        compiler_params=_cparams("parallel", "parallel"), name=name,
    )(c_all, mod_w, mod_b.reshape(L, 1, E))


def _adam_math(w, g, m, v):
    m = ADAM_B1 * m + (1.0 - ADAM_B1) * g
    v = ADAM_B2 * v + (1.0 - ADAM_B2) * (g * g)
    m_hat = m / (1.0 - ADAM_B1 ** ADAM_STEP)
    v_hat = v / (1.0 - ADAM_B2 ** ADAM_STEP)
    delta = -ADAM_LR * (m_hat / (jnp.sqrt(v_hat) + ADAM_EPS) + ADAM_WD * w)
    return delta, m, v


def _as2d(a):
    return a.reshape(-1, a.shape[-1]) if a.ndim != 2 else a


def adamw(w, g, m, v, name):
    shape = w.shape
    w2, g2, m2, v2 = _as2d(w), _as2d(g), _as2d(m), _as2d(v)
    R, C = w2.shape
    tr = _pick(R, max(8, (1 << 18) // C // 8 * 8), 8)

    def body(w_ref, g_ref, m_ref, v_ref, d_ref, mo_ref, vo_ref):
        d, mn, vn = _adam_math(w_ref[...], g_ref[...], m_ref[...], v_ref[...])
        d_ref[...] = d
        mo_ref[...] = mn
        vo_ref[...] = vn

    blk = pl.BlockSpec((tr, C), lambda i: (i, 0))
    shp = jax.ShapeDtypeStruct((R, C), F32)
    outs = pl.pallas_call(
        body, grid=(R // tr,), in_specs=[blk] * 4, out_specs=[blk] * 3, out_shape=[shp] * 3,
        compiler_params=_cparams("parallel"), name=name,
    )(w2, g2, m2, v2)
    return tuple(o.reshape(shape) for o in outs)


def adamw_sum(parts, w, m, v, name):
    P, R, C = parts.shape

    def body(p_ref, w_ref, m_ref, v_ref, g_ref, d_ref, mo_ref, vo_ref):
        g = p_ref[0]
        for k in range(1, P):
            g = g + p_ref[k]
        d, mn, vn = _adam_math(w_ref[...], g, m_ref[...], v_ref[...])
        g_ref[...] = g
        d_ref[...] = d
        mo_ref[...] = mn
        vo_ref[...] = vn

    shp = jax.ShapeDtypeStruct((R, C), F32)
    return pl.pallas_call(body, out_shape=[shp] * 4, compiler_params=_cparams(), name=name)(parts, w, m, v)


def adamw_modw(c_col, dm, w, m, v, name, tr=256, tn=512):
    L, Dn, E = w.shape
    B = c_col.shape[0]
    tr = _pick(Dn, tr, 8)
    tn = _pick(E, tn, LANES)

    def body(c_ref, dm_ref, w_ref, m_ref, v_ref, g_ref, d_ref, mo_ref, vo_ref):
        g = jnp.zeros((tr, tn), F32)
        for b in range(B):
            cv = c_ref[b]
            g = g + (cv / (1.0 + jnp.exp(-cv))) * dm_ref[b:b + 1, :]
        d, mn, vn = _adam_math(w_ref[...], g, m_ref[...], v_ref[...])
        g_ref[...] = g
        d_ref[...] = d
        mo_ref[...] = mn
        vo_ref[...] = vn

    blk = pl.BlockSpec((None, tr, tn), lambda l, i, j: (l, i, j))
    shp = jax.ShapeDtypeStruct((L, Dn, E), F32)
    return pl.pallas_call(
        body, grid=(L, Dn // tr, E // tn),
        in_specs=[pl.BlockSpec((B, tr, 1), lambda l, i, j: (0, i, 0)),
                  pl.BlockSpec((None, B, tn), lambda l, i, j: (l, 0, j)), blk, blk, blk],
        out_specs=[blk] * 4, out_shape=[shp] * 4,
        compiler_params=_cparams("parallel", "parallel", "parallel"), name=name,
    )(c_col, dm, w, m, v)


def add_round(a, b, name, tr=512):
    R, C = a.shape
    tr = _pick(R, tr, 16)

    def body(a_ref, b_ref, o_ref):
        o_ref[...] = (a_ref[...] + b_ref[...].astype(F32)).astype(BF16)

    blk = pl.BlockSpec((tr, C), lambda i: (i, 0))
    return pl.pallas_call(
        body, grid=(R // tr,), in_specs=[blk, blk], out_specs=blk, out_shape=jax.ShapeDtypeStruct((R, C), BF16),
        compiler_params=_cparams("parallel"), name=name,
    )(a, b)


def sum_parts(parts, name, tr=512):
    P, R, C = parts.shape
    tr = _pick(R, tr, 16)

    def body(p_ref, o_ref):
        s = p_ref[0].astype(F32)
        for k in range(1, P):
            s = s + p_ref[k].astype(F32)
        o_ref[...] = s

    return pl.pallas_call(
        body, grid=(R // tr,), in_specs=[pl.BlockSpec((P, tr, C), lambda i: (0, i, 0))],
        out_specs=pl.BlockSpec((tr, C), lambda i: (i, 0)), out_shape=jax.ShapeDtypeStruct((R, C), F32),
        compiler_params=_cparams("parallel"), name=name,
    )(parts)


_ANY = pl.BlockSpec(memory_space=pl.ANY)


def _place():
    return lax.axis_index("x"), lax.axis_index("y"), lax.axis_index("c")


def _flip(v, bit):
    return 1 - v if bit else v


def chip_gather(buf, name):
    def body(in_ref, out_ref, send_sems, recv_sems):
        x, y, c = _place()
        me = 2 * x + y
        sends = []
        for k in range(1, N_CHIPS):
            px, py = _flip(x, k >> 1), _flip(y, k & 1)
            cp = pltpu.make_async_remote_copy(src_ref=in_ref, dst_ref=out_ref.at[me], send_sem=send_sems.at[k - 1],
                                              recv_sem=recv_sems.at[k - 1], device_id=(px, py, c), device_id_type=MESH)
            cp.start()
            sends.append(cp)
        for k in range(1, N_CHIPS):
            px, py = _flip(x, k >> 1), _flip(y, k & 1)
            pltpu.make_async_remote_copy(src_ref=in_ref, dst_ref=out_ref.at[2 * px + py], send_sem=send_sems.at[k - 1],
                                         recv_sem=recv_sems.at[k - 1], device_id=(px, py, c),
                                         device_id_type=MESH).wait_recv()
        for cp in sends:
            cp.wait_send()

    out = pl.pallas_call(
        body, in_specs=[_ANY], out_specs=_ANY,
        out_shape=jax.ShapeDtypeStruct((N_CHIPS,) + buf.shape, buf.dtype),
        scratch_shapes=[pltpu.SemaphoreType.DMA((N_CHIPS - 1,)), pltpu.SemaphoreType.DMA((N_CHIPS - 1,))],
        name=name,
    )(buf)
    return lax.dynamic_update_index_in_dim(out, buf, 2 * lax.axis_index("x") + lax.axis_index("y"), 0)


def chip_all_to_all(buf, name):
    def body(in_ref, out_ref, send_sems, recv_sems):
        x, y, c = _place()
        me = 2 * x + y
        sends = []
        for k in range(1, N_CHIPS):
            px, py = _flip(x, k >> 1), _flip(y, k & 1)
            cp = pltpu.make_async_remote_copy(src_ref=in_ref.at[2 * px + py], dst_ref=out_ref.at[me],
                                              send_sem=send_sems.at[k - 1], recv_sem=recv_sems.at[k - 1],
                                              device_id=(px, py, c), device_id_type=MESH)
            cp.start()
            sends.append(cp)
        for k in range(1, N_CHIPS):
            px, py = _flip(x, k >> 1), _flip(y, k & 1)
            pltpu.make_async_remote_copy(src_ref=in_ref.at[me], dst_ref=out_ref.at[2 * px + py],
                                         send_sem=send_sems.at[k - 1], recv_sem=recv_sems.at[k - 1],
                                         device_id=(px, py, c), device_id_type=MESH).wait_recv()
        for cp in sends:
            cp.wait_send()

    out = pl.pallas_call(
        body, in_specs=[_ANY], out_specs=_ANY, out_shape=jax.ShapeDtypeStruct(buf.shape, buf.dtype),
        scratch_shapes=[pltpu.SemaphoreType.DMA((N_CHIPS - 1,)), pltpu.SemaphoreType.DMA((N_CHIPS - 1,))],
        name=name,
    )(buf)
    me = 2 * lax.axis_index("x") + lax.axis_index("y")
    return lax.dynamic_update_index_in_dim(out, _index(buf, me), me, 0)


def core_gather(buf, name):
    def body(in_ref, out_ref, send_sem, recv_sem):
        x, y, c = _place()
        cp = pltpu.make_async_remote_copy(src_ref=in_ref, dst_ref=out_ref.at[c], send_sem=send_sem, recv_sem=recv_sem,
                                          device_id=(x, y, 1 - c), device_id_type=MESH)
        cp.start()
        pltpu.make_async_remote_copy(src_ref=in_ref, dst_ref=out_ref.at[1 - c], send_sem=send_sem, recv_sem=recv_sem,
                                     device_id=(x, y, 1 - c), device_id_type=MESH).wait_recv()
        cp.wait_send()

    out = pl.pallas_call(
        body, in_specs=[_ANY], out_specs=_ANY, out_shape=jax.ShapeDtypeStruct((2,) + buf.shape, buf.dtype),
        scratch_shapes=[pltpu.SemaphoreType.DMA, pltpu.SemaphoreType.DMA],
        name=name,
    )(buf)
    return lax.dynamic_update_index_in_dim(out, buf, lax.axis_index("c"), 0)


def core_swap(buf, name):
    def body(in_ref, out_ref, send_sem, recv_sem):
        x, y, c = _place()
        cp = pltpu.make_async_remote_copy(src_ref=in_ref, dst_ref=out_ref, send_sem=send_sem, recv_sem=recv_sem,
                                          device_id=(x, y, 1 - c), device_id_type=MESH)
        cp.start()
        cp.wait()

    return pl.pallas_call(
        body, in_specs=[_ANY], out_specs=_ANY, out_shape=jax.ShapeDtypeStruct(buf.shape, buf.dtype),
        scratch_shapes=[pltpu.SemaphoreType.DMA, pltpu.SemaphoreType.DMA],
        name=name,
    )(buf)


def device_gather(buf, name):
    def body(in_ref, out_ref, send_sems, recv_sems, local_sem):
        x, y, c = _place()
        me = 4 * x + 2 * y + c
        mine = pltpu.make_async_copy(in_ref, out_ref.at[me], local_sem)
        mine.start()
        sends = []
        for k in range(1, N_DEV):
            peer = (_flip(x, (k >> 2) & 1), _flip(y, (k >> 1) & 1), _flip(c, k & 1))
            cp = pltpu.make_async_remote_copy(src_ref=in_ref, dst_ref=out_ref.at[me], send_sem=send_sems.at[k - 1],
                                              recv_sem=recv_sems.at[k - 1], device_id=peer, device_id_type=MESH)
            cp.start()
            sends.append(cp)
        for k in range(1, N_DEV):
            peer = (_flip(x, (k >> 2) & 1), _flip(y, (k >> 1) & 1), _flip(c, k & 1))
            pltpu.make_async_remote_copy(src_ref=in_ref, dst_ref=out_ref.at[4 * peer[0] + 2 * peer[1] + peer[2]],
                                         send_sem=send_sems.at[k - 1], recv_sem=recv_sems.at[k - 1], device_id=peer,
                                         device_id_type=MESH).wait_recv()
        for cp in sends:
            cp.wait_send()
        mine.wait()

    return pl.pallas_call(
        body, in_specs=[_ANY], out_specs=_ANY, out_shape=jax.ShapeDtypeStruct((N_DEV,) + buf.shape, buf.dtype),
        scratch_shapes=[pltpu.SemaphoreType.DMA((N_DEV - 1,)), pltpu.SemaphoreType.DMA((N_DEV - 1,)),
                        pltpu.SemaphoreType.DMA],
        name=name,
    )(buf)


def _region(ref, chip_axis=None, chip=None, chip_size=None, half_axis=None, half=None, half_size=None):
    idx = [slice(None)] * len(ref.shape)
    if chip is not None:
        idx[chip_axis] = pl.ds(chip * chip_size, chip_size)
    if half is not None:
        idx[half_axis] = pl.ds(half * half_size, half_size)
    return ref.at[tuple(idx)]


def gather_weights(shards, axes, name, after=()):
    n = len(shards)

    def full_shape(t):
        shp = list(shards[t].shape)
        shp[axes[t][0]] *= N_CHIPS
        return tuple(shp)

    def body(*refs):
        ins, outs = refs[:n], refs[n + len(after):2 * n + len(after)]
        ici_send, ici_recv, d2d_send, d2d_recv, own_send, own_recv = refs[2 * n + len(after):]
        x, y, c = _place()
        me = 2 * x + y

        def part(t, ref, chip, half):
            ca, ha = axes[t]
            return _region(ref, ca, chip, ins[t].shape[ca], ha, half, ins[t].shape[ha] // 2)

        def own(t):
            return pltpu.make_async_remote_copy(src_ref=ins[t], dst_ref=part(t, outs[t], me, None),
                                                send_sem=own_send.at[t], recv_sem=own_recv.at[t],
                                                device_id=(x, y, 1 - c), device_id_type=MESH)

        started = []
        for t in range(n):
            own(t).start()
            started.append(own(t))
        for t in range(n):
            for k in range(1, N_CHIPS):
                px, py = _flip(x, k >> 1), _flip(y, k & 1)
                cp = pltpu.make_async_remote_copy(src_ref=part(t, ins[t], None, c), dst_ref=part(t, outs[t], me, c),
                                                  send_sem=ici_send.at[t, k - 1], recv_sem=ici_recv.at[t, k - 1],
                                                  device_id=(px, py, c), device_id_type=MESH)
                cp.start()
                started.append(cp)
        for t in range(n):
            for k in range(1, N_CHIPS):
                px, py = _flip(x, k >> 1), _flip(y, k & 1)
                got = part(t, outs[t], 2 * px + py, c)
                pltpu.make_async_remote_copy(src_ref=part(t, ins[t], None, c), dst_ref=got,
                                             send_sem=ici_send.at[t, k - 1], recv_sem=ici_recv.at[t, k - 1],
                                             device_id=(px, py, c), device_id_type=MESH).wait_recv()
                fw = pltpu.make_async_remote_copy(src_ref=got, dst_ref=got, send_sem=d2d_send.at[t, k - 1],
                                                  recv_sem=d2d_recv.at[t, k - 1], device_id=(x, y, 1 - c),
                                                  device_id_type=MESH)
                fw.start()
                started.append(fw)
        for t in range(n):
            for k in range(1, N_CHIPS):
                px, py = _flip(x, k >> 1), _flip(y, k & 1)
                theirs = part(t, outs[t], 2 * px + py, 1 - c)
                pltpu.make_async_remote_copy(src_ref=theirs, dst_ref=theirs, send_sem=d2d_send.at[t, k - 1],
                                             recv_sem=d2d_recv.at[t, k - 1], device_id=(x, y, 1 - c),
                                             device_id_type=MESH).wait_recv()
        for t in range(n):
            own(t).wait_recv()
        for cp in started:
            cp.wait_send()

    sem = pltpu.SemaphoreType.DMA((n, N_CHIPS - 1))
    own_sem = pltpu.SemaphoreType.DMA((n,))
    return pl.pallas_call(
        body, in_specs=[_ANY] * (n + len(after)), out_specs=[_ANY] * n,
        out_shape=[jax.ShapeDtypeStruct(full_shape(t), shards[t].dtype) for t in range(n)],
        scratch_shapes=[sem, sem, sem, sem, own_sem, own_sem], name=name,
    )(*shards, *after)


_HBM = pl.BlockSpec(memory_space=pltpu.HBM)
_SEM = pl.BlockSpec(memory_space=pltpu.SEMAPHORE)
_EFFECT = pltpu.SideEffectType.DATAFLOW_SIDE_EFFECTING
WEIGHT_COPIES = N_CHIPS


def _weight_peer(k, x, y, c):
    return (x, y, 1 - c) if k == 0 else (_flip(x, k >> 1), _flip(y, k & 1), c)


def weights_start(shards, items, name, after=()):
    n_sh, n_it = len(shards), len(items)

    def src_of(refs, i):
        t, layer, _ = items[i]
        return refs[t] if layer is None else refs[t].at[layer]

    def land_shape(i):
        t, layer, ca = items[i]
        shp = list(shards[t].shape if layer is None else shards[t].shape[1:])
        shp[ca] *= N_CHIPS
        return tuple(shp)

    def body(*refs):
        shard_refs, land_refs = refs[:n_sh], refs[n_sh:n_sh + n_it]
        first_out = n_sh + n_it + len(after)
        send_sems = refs[first_out:first_out + n_it]
        recv_sems = refs[first_out + n_it:first_out + 2 * n_it]
        token = refs[-1]
        x, y, c = _place()
        me = 2 * x + y
        for i in range(n_it):
            src = src_of(shard_refs, i)
            ca = items[i][2]
            dst = _region(land_refs[i], ca, me, src.shape[ca])
            for k in range(WEIGHT_COPIES):
                pltpu.make_async_remote_copy(src_ref=src, dst_ref=dst, send_sem=send_sems[i], recv_sem=recv_sems[i],
                                             device_id=_weight_peer(k, x, y, c), device_id_type=MESH).start()
        token[...] = jnp.zeros_like(token)

    lands = [pltpu.with_memory_space_constraint(lax.empty(land_shape(i), shards[0].dtype), pltpu.HBM)
             for i in range(n_it)]
    ins = [pltpu.with_memory_space_constraint(a, pltpu.HBM) for a in shards] + lands
    sems = (pltpu.SemaphoreType.DMA(()),) * (2 * n_it)
    outs = pl.pallas_call(
        body, name=name,
        out_shape=sems + tuple(pltpu.HBM(a.shape, a.dtype) for a in ins) + (jax.ShapeDtypeStruct((8, LANES), F32),),
        in_specs=[_HBM] * len(ins) + [_ANY] * len(after),
        out_specs=(_SEM,) * (2 * n_it) + (_HBM,) * len(ins) + (pl.BlockSpec(memory_space=pltpu.VMEM),),
        input_output_aliases={i: 2 * n_it + i for i in range(len(ins))},
        compiler_params=pltpu.CompilerParams(has_side_effects=_EFFECT),
    )(*ins, *after)
    base = 2 * n_it
    return (list(outs[:n_it]), list(outs[n_it:base]), list(outs[base:base + n_sh]),
            list(outs[base + n_sh:base + n_sh + n_it]), outs[-1])


def weights_wait(send_sems, recv_sems, lands, after, keep, name):
    m = len(lands)

    def body(*refs):
        land_refs, send_refs, recv_refs = refs[:m], refs[m:2 * m], refs[2 * m:3 * m]
        x, y, c = _place()
        for j in range(m):
            cp = pltpu.make_async_remote_copy(src_ref=land_refs[j], dst_ref=land_refs[j], send_sem=send_refs[j],
                                              recv_sem=recv_refs[j], device_id=(x, y, 1 - c),
                                              device_id_type=MESH)
            cp.wait_send()
            cp.wait_recv()

    outs = pl.pallas_call(
        body, name=name,
        out_shape=tuple(pltpu.HBM(a.shape, a.dtype) for a in lands),
        in_specs=[_HBM] * m + [_SEM] * (2 * m) + [_ANY] + [_HBM] * len(keep),
        out_specs=(_HBM,) * m,
        input_output_aliases={j: j for j in range(m)},
        compiler_params=pltpu.CompilerParams(has_side_effects=_EFFECT),
    )(*lands, *send_sems, *recv_sems, after, *keep)
    return list(outs)


def reduce_to_sibling(lo, hi, name):
    n = len(lo)

    def body(*refs):
        los, his, outs = refs[:n], refs[n:2 * n], refs[2 * n:3 * n]
        send_sems, recv_sems = refs[3 * n:]
        x, y, c = _place()

        def copy(u, src):
            return pltpu.make_async_remote_copy(src_ref=src, dst_ref=outs[u], send_sem=send_sems.at[u],
                                                recv_sem=recv_sems.at[u], device_id=(x, y, 1 - c), device_id_type=MESH)

        for u in range(n):
            @pl.when(c == 0)
            def _(u=u):
                copy(u, his[u]).start()

            @pl.when(c == 1)
            def _(u=u):
                copy(u, los[u]).start()
        for u in range(n):
            copy(u, los[u]).wait_recv()
        for u in range(n):
            copy(u, los[u]).wait_send()

    return pl.pallas_call(
        body, in_specs=[_ANY] * (2 * n), out_specs=[_ANY] * n,
        out_shape=[jax.ShapeDtypeStruct(a.shape, a.dtype) for a in lo],
        scratch_shapes=[pltpu.SemaphoreType.DMA((n,)), pltpu.SemaphoreType.DMA((n,))], name=name,
    )(*lo, *hi)


def add_selected(lo, hi, other, name, tile_elems=1 << 19):
    R, C = lo.shape
    tr = _pick(R, max(16, tile_elems // C // 16 * 16), 16)

    def body(lo_ref, hi_ref, o_ref, out_ref):
        mine = jnp.where(lax.axis_index("c") == 0, lo_ref[...].astype(F32), hi_ref[...].astype(F32))
        out_ref[...] = (mine + o_ref[...].astype(F32)).astype(out_ref.dtype)

    blk = pl.BlockSpec((tr, C), lambda i: (i, 0))
    return pl.pallas_call(
        body, grid=(R // tr,), in_specs=[blk, blk, blk], out_specs=blk, out_shape=jax.ShapeDtypeStruct((R, C), BF16),
        compiler_params=_cparams("parallel"), name=name,
    )(lo, hi, other)


def scatter_to_chips(pieces, chip_axes, name):
    n = len(pieces)

    def block_shape(u):
        shp = list(pieces[u].shape)
        shp[chip_axes[u]] //= N_CHIPS
        return tuple(shp)

    def body(*refs):
        ins, outs = refs[:n], refs[n:2 * n]
        send_sems, recv_sems = refs[2 * n:]
        x, y, c = _place()
        me = 2 * x + y
        started = []
        for u in range(n):
            size = block_shape(u)[chip_axes[u]]
            for k in range(1, N_CHIPS):
                px, py = _flip(x, k >> 1), _flip(y, k & 1)
                cp = pltpu.make_async_remote_copy(src_ref=_region(ins[u], chip_axes[u], 2 * px + py, size),
                                                  dst_ref=outs[u].at[me], send_sem=send_sems.at[u, k - 1],
                                                  recv_sem=recv_sems.at[u, k - 1], device_id=(px, py, c),
                                                  device_id_type=MESH)
                cp.start()
                started.append(cp)
        for u in range(n):
            size = block_shape(u)[chip_axes[u]]
            for k in range(1, N_CHIPS):
                px, py = _flip(x, k >> 1), _flip(y, k & 1)
                pltpu.make_async_remote_copy(src_ref=_region(ins[u], chip_axes[u], me, size),
                                             dst_ref=outs[u].at[2 * px + py], send_sem=send_sems.at[u, k - 1],
                                             recv_sem=recv_sems.at[u, k - 1], device_id=(px, py, c),
                                             device_id_type=MESH).wait_recv()
        for cp in started:
            cp.wait_send()

    sem = pltpu.SemaphoreType.DMA((n, N_CHIPS - 1))
    return pl.pallas_call(
        body, in_specs=[_ANY] * n, out_specs=[_ANY] * n,
        out_shape=[jax.ShapeDtypeStruct((N_CHIPS,) + block_shape(u), pieces[u].dtype) for u in range(n)],
        scratch_shapes=[sem, sem], name=name,
    )(*pieces)


def scatter_start(pieces, chip_axes, name, after=()):
    n = len(pieces)

    def block_shape(u):
        shp = list(pieces[u].shape)
        shp[chip_axes[u]] //= N_CHIPS
        return tuple(shp)

    def body(*refs):
        ins, land_refs = refs[:n], refs[n:2 * n]
        first_out = 2 * n + len(after)
        send_sems, recv_sems = refs[first_out:first_out + n], refs[first_out + n:first_out + 2 * n]
        token = refs[-1]
        x, y, c = _place()
        me = 2 * x + y
        for u in range(n):
            size = block_shape(u)[chip_axes[u]]
            for k in range(1, N_CHIPS):
                px, py = _flip(x, k >> 1), _flip(y, k & 1)
                pltpu.make_async_remote_copy(src_ref=_region(ins[u], chip_axes[u], 2 * px + py, size),
                                             dst_ref=land_refs[u].at[me], send_sem=send_sems[u], recv_sem=recv_sems[u],
                                             device_id=(px, py, c), device_id_type=MESH).start()
        token[...] = jnp.zeros_like(token)

    lands = [pltpu.with_memory_space_constraint(lax.empty((N_CHIPS,) + block_shape(u), pieces[u].dtype), pltpu.HBM)
             for u in range(n)]
    ins = [pltpu.with_memory_space_constraint(a, pltpu.HBM) for a in pieces] + lands
    sems = (pltpu.SemaphoreType.DMA(()),) * (2 * n)
    outs = pl.pallas_call(
        body, name=name,
        out_shape=sems + tuple(pltpu.HBM(a.shape, a.dtype) for a in ins) + (jax.ShapeDtypeStruct((8, LANES), F32),),
        in_specs=[_HBM] * len(ins) + [_ANY] * len(after),
        out_specs=(_SEM,) * (2 * n) + (_HBM,) * len(ins) + (pl.BlockSpec(memory_space=pltpu.VMEM),),
        input_output_aliases={i: 2 * n + i for i in range(len(ins))},
        compiler_params=pltpu.CompilerParams(has_side_effects=_EFFECT),
    )(*ins, *after)
    return list(outs[:n]), list(outs[n:2 * n]), list(outs[2 * n:3 * n]), list(outs[3 * n:4 * n]), outs[-1]


def scatter_wait(send_sems, recv_sems, lands, pieces, after, name):
    n = len(lands)

    def body(*refs):
        land_refs, send_refs, recv_refs = refs[:n], refs[n:2 * n], refs[2 * n:3 * n]
        x, y, c = _place()
        for u in range(n):
            three = land_refs[u].at[pl.ds(0, N_CHIPS - 1)]
            cp = pltpu.make_async_remote_copy(src_ref=three, dst_ref=three, send_sem=send_refs[u], recv_sem=recv_refs[u],
                                              device_id=(x, y, 1 - c), device_id_type=MESH)
            cp.wait_send()
            cp.wait_recv()

    outs = pl.pallas_call(
        body, name=name,
        out_shape=tuple(pltpu.HBM(a.shape, a.dtype) for a in lands),
        in_specs=[_HBM] * n + [_SEM] * (2 * n) + [_ANY] + [_HBM] * len(pieces),
        out_specs=(_HBM,) * n,
        input_output_aliases={j: j for j in range(n)},
        compiler_params=pltpu.CompilerParams(has_side_effects=_EFFECT),
    )(*lands, *send_sems, *recv_sems, after, *pieces)
    return list(outs)


def gather_halves(parts, slots, out_shapes, name):
    n = len(parts)

    def body(*refs):
        ins, outs = refs[:n], refs[n:n + len(out_shapes)]
        send_sems, recv_sems = refs[n + len(out_shapes):]
        x, y, c = _place()
        started = []
        for u in range(n):
            t, s = slots[u]
            cp = pltpu.make_async_remote_copy(src_ref=ins[u], dst_ref=outs[t].at[s, c], send_sem=send_sems.at[u],
                                              recv_sem=recv_sems.at[u], device_id=(x, y, 1 - c), device_id_type=MESH)
            cp.start()
            started.append(cp)
        for u in range(n):
            t, s = slots[u]
            pltpu.make_async_remote_copy(src_ref=ins[u], dst_ref=outs[t].at[s, 1 - c], send_sem=send_sems.at[u],
                                         recv_sem=recv_sems.at[u], device_id=(x, y, 1 - c),
                                         device_id_type=MESH).wait_recv()
        for cp in started:
            cp.wait_send()

    return pl.pallas_call(
        body, in_specs=[_ANY] * n, out_specs=[_ANY] * len(out_shapes),
        out_shape=[jax.ShapeDtypeStruct(shp, F32) for shp in out_shapes],
        scratch_shapes=[pltpu.SemaphoreType.DMA((n,)), pltpu.SemaphoreType.DMA((n,))], name=name,
    )(*parts)


WEIGHT_ORDER = ["mod_w", "mod_b", "norm1_g", "norm2_g", "pool_w", "pool_b", "pool_scale", "kv_in_g", "w_dkv",
                "ckv_norm_g", "w_uk", "w_uv", "w_dq", "q_norm_g", "w_uq", "w_o", "w_up", "conv_w", "conv_b", "w_down",
                "final_g"]
EXCHANGED = {"w_up": (2, 0), "w_down": (1, 0), "w_o": (1, 0), "w_uq": (2, 0), "w_dq": (1, 0), "pool_w": (2, 0),
             "w_dkv": (0, 1), "w_uk": (1, 0), "w_uv": (1, 0)}
SMALL_SHARDED = {"conv_w": 2, "pool_b": 1, "pool_scale": 1}
REPLICATED = ["mod_b", "norm1_g", "norm2_g", "kv_in_g", "ckv_norm_g", "q_norm_g", "conv_b", "final_g"]


def _padded(n, align):
    return -(-n // align) * align


def _flat_pad(parts, total):
    flat = jnp.concatenate(parts, axis=-1)
    pad = total - flat.shape[-1]
    if pad:
        flat = jnp.concatenate([flat, jnp.zeros(flat.shape[:-1] + (pad,), flat.dtype)], axis=-1)
    return flat


def _split_shards(full, axis):
    shp = full.shape
    t = full.reshape(shp[:axis] + (N_CHIPS, shp[axis] // N_CHIPS) + shp[axis + 1:])
    return jnp.moveaxis(t, axis, 0).reshape(N_CHIPS, -1)


def _join_shards(rows, shard_shape, axis):
    t = jnp.moveaxis(rows.reshape((N_CHIPS,) + tuple(shard_shape)), 0, axis)
    return t.reshape(tuple(shard_shape[:axis]) + (N_CHIPS * shard_shape[axis],) + tuple(shard_shape[axis + 1:]))


def _index(a, i, axis=0):
    return lax.dynamic_index_in_dim(a, i, axis, keepdims=False)


def kernel(x, c, positions, mod_w, mod_b, norm1_g, norm2_g, pool_w, pool_b, pool_scale, kv_in_g, w_dkv, ckv_norm_g, w_uk, w_uv, w_dq, q_norm_g, w_uq, w_o, w_up, conv_w, conv_b, w_down, final_g, loss_target, m_mod_w, m_mod_b, m_norm1_g, m_norm2_g, m_pool_w, m_pool_b, m_pool_scale, m_kv_in_g, m_w_dkv, m_ckv_norm_g, m_w_uk, m_w_uv, m_w_dq, m_q_norm_g, m_w_uq, m_w_o, m_w_up, m_conv_w, m_conv_b, m_w_down, m_final_g, v_mod_w, v_mod_b, v_norm1_g, v_norm2_g, v_pool_w, v_pool_b, v_pool_scale, v_kv_in_g, v_w_dkv, v_ckv_norm_g, v_w_uk, v_w_uv, v_w_dq, v_q_norm_g, v_w_uq, v_w_o, v_w_up, v_conv_w, v_conv_b, v_w_down, v_final_g):
    given = dict(locals())
    W = {n: given[n] for n in WEIGHT_ORDER}
    M1 = {n: given["m_" + n] for n in WEIGHT_ORDER}
    V2 = {n: given["v_" + n] for n in WEIGHT_ORDER}
    xi, yi, ci = lax.axis_index("x"), lax.axis_index("y"), lax.axis_index("c")
    chip = 2 * xi + yi
    dev = 4 * xi + 2 * yi + ci
    x0 = x[0]
    S_, D = x0.shape
    Fh = conv_b.shape[1]
    E = mod_b.shape[1]
    Es = E // N_CHIPS
    zD = jnp.zeros((D,), F32)

    c_all = device_gather(c, "gather_c").reshape(N_DEV, D)
    c_pad = jnp.concatenate([c_all, jnp.zeros((16 - N_DEV, D), F32)], axis=0)
    mod_b_mine = lax.dynamic_slice_in_dim(mod_b, chip * Es, Es, axis=1)
    mods_part = mods_fwd(c_pad, mod_w, mod_b_mine, "mods_fwd")
    mods_all = chip_gather(mods_part, "gather_mods")
    mods = jnp.swapaxes(_index(mods_all, dev, axis=2), 0, 1).reshape(DEPTH, E)
    mod = [[mods[l, k * D:(k + 1) * D] for k in range(6)] for l in range(DEPTH)]

    full = {}
    ssz = {n: math.prod(W[n].shape) for n in SMALL_SHARDED}
    Tw = _padded(sum(ssz.values()), 8 * PACK_COLS)
    small_rows = chip_gather(_flat_pad([W[n].reshape(-1) for n in SMALL_SHARDED], Tw).reshape(-1, PACK_COLS),
                             "gather_small_w").reshape(N_CHIPS, Tw)
    off = 0
    for n, axis in SMALL_SHARDED.items():
        full[n] = _join_shards(small_rows[:, off:off + ssz[n]], W[n].shape, axis)
        off += ssz[n]

    names = list(EXCHANGED)
    shards = [W[n].astype(BF16) for n in names]
    n_mla = DEPTH - N_A
    first_axes = {"w_up": (1, 0), "w_down": (0, 1), "pool_w": (1, 0)}
    first = gather_weights([shards[names.index(n)][0] for n in first_axes], list(first_axes.values()), "gather_weights0",
                           after=[mods, small_rows])
    for n, arr in zip(first_axes, first):
        full[(n, 0)] = arr
    items, groups = [], []

    def group(entries):
        groups.append(list(range(len(items), len(items) + len(entries))))
        for n, layer in entries:
            ca = EXCHANGED[n][0] - (0 if layer is None else 1)
            items.append((names.index(n), layer, 0 if n == "w_dkv" else ca))

    for l in range(1, N_A):
        group([("w_up", l), ("w_down", l), ("pool_w", l)])
    for j in range(n_mla):
        head = [("w_dkv", None), ("w_uk", None), ("w_uv", None)] if j == 0 else []
        group(head + [("w_dq", j), ("w_uq", j), ("w_o", j), ("w_up", N_A + j), ("w_down", N_A + j)])
    w_send, w_recv, shards_thru, lands, _ = weights_start(shards, items, "weights_start", after=first)

    def weights_ready(g, after):
        keep = shards_thru if g == len(groups) - 1 else []
        got = weights_wait([w_send[i] for i in groups[g]], [w_recv[i] for i in groups[g]], [lands[i] for i in groups[g]],
                           after, keep, f"weights_wait{g}")
        for i, arr in zip(groups[g], got):
            t, layer, _ = items[i]
            full[(names[t], 0 if layer is None else layer)] = arr

    q_rank = W["w_uq"].shape[1]
    kv_w = KV_RANK + QK_ROPE

    def uq_ext(j):
        wq = full[("w_uq", j)].reshape(q_rank, N_HEADS, QK_HEAD)
        return jnp.concatenate([wq, jnp.zeros((q_rank, N_HEADS, HEAD_PAD - QK_HEAD), BF16)],
                               axis=2).reshape(q_rank, N_HEADS * HEAD_PAD)


    half = QK_ROPE // 2
    inv = 1.0 / (ROPE_THETA ** (jnp.arange(0, QK_ROPE, 2, dtype=F32) / QK_ROPE))
    inv_row = jnp.concatenate([inv, inv, jnp.zeros((LANES - 2 * half,), F32)]).reshape(1, LANES)
    tabs = rope_tables(positions[0].astype(F32).reshape(S_, 1), inv_row, "rope_tables")
    att_scale = QK_HEAD ** -0.5

    saved = []
    xcur = x0
    kv_saved = None
    K = VX = knv = None
    for l in range(DEPTH):
        sh1, sc1, g1, sh2, sc2, g2 = mod[l]
        st = {"xin": xcur}
        if l:
            weights_ready(l - 1, xcur)
        if l == N_A:
            w_dkv_ext = jnp.concatenate([full[("w_dkv", 0)], jnp.zeros((D, KV_RANK + LANES - kv_w), BF16)], axis=1)
            w_ukv = jnp.concatenate([full[("w_uk", 0)], full[("w_uv", 0)]], axis=1)
            xn = norm_fwd(xcur, kv_in_g, zD, zD, BF16, "kvin_fwd")
            kv_ext = mm(xn, w_dkv_ext, "nn", F32, "dkv_mm")
            lat = kv_ext[:, :KV_RANK]
            zk = jnp.zeros((KV_RANK,), F32)
            ckv = norm_fwd(lat, ckv_norm_g, zk, zk, BF16, "ckv_fwd")
            knv = mm(ckv, w_ukv, "nn", BF16, "ukv_mm")
            K, VX = k_prep(knv, kv_ext, tabs, "k_prep")
            kv_saved = {"x": xcur, "xn": xn, "lat": lat, "ckv": ckv}
        if l < N_A:
            h1 = norm_fwd(xcur, norm1_g[l], sc1, sh1, F32, f"norm1_fwd{l}")
            st["pooled"] = _pool_call(h1, BF16, f"pool_fwd{l}", False)
            st["cs"] = g1 * full["pool_scale"][l]
            st["ypre"], xmid = gmm(st["pooled"], full[("pool_w", l)], "nn", BF16, f"pool_mm{l}", bias=full["pool_b"][l],
                                   res=xcur, colscale=st["cs"])
        else:
            j = l - N_A
            st["h1"] = norm_fwd(xcur, norm1_g[l], sc1, sh1, BF16, f"norm1_fwd{l}")
            st["ql"] = mm(st["h1"], full[("w_dq", j)], "nn", F32, f"dq_mm{l}")
            st["cq"] = norm_fwd(st["ql"], q_norm_g[j], jnp.zeros_like(q_norm_g[j]), jnp.zeros_like(q_norm_g[j]), BF16,
                                f"qnorm_fwd{l}")
            st["w_uq_ext"] = uq_ext(j)
            qe = mm(st["cq"], st["w_uq_ext"], "nn", F32, f"uq_mm{l}")
            st["Q"] = q_prep(qe, tabs, att_scale, False, f"q_prep{l}")
            st["o"], lse = attn_fwd(st["Q"], K, VX, f"attn_fwd{l}")
            st["lse"] = lse.reshape(N_HEADS, 1, S_)
            st["y"], xmid = mm(st["o"], full[("w_o", j)], "nn", BF16, f"wo_mm{l}", res=xcur, colscale=g1)
        st["xmid"] = xmid
        st["h2"] = norm_fwd(xmid, norm2_g[l], sc2, sh2, BF16, f"norm2_fwd{l}")
        st["u"] = mm(st["h2"], full[("w_up", l)], "nn", BF16, f"up_mm{l}")
        st["z"] = glu_fwd(st["u"], full["conv_w"][l], conv_b[l], f"glu_fwd{l}")
        st["f"], xcur = mm(st["z"], full[("w_down", l)], "nn", BF16, f"down_mm{l}", tk=1408, res=xmid, colscale=g2)
        saved.append(st)

    dx, d_final_g, loss_part = loss_head(xcur, final_g, loss_target[0], "loss_head")
    loss = lax.psum(loss_part[0, 0], ("x", "y", "c"))

    def begin_reduce(tensors, first_slot, tag):
        units = []
        for n in tensors:
            ca = EXCHANGED[n][0]
            if W[n].ndim > 2:
                n_slots = W[n].shape[0] // 2
                for sl in range(first_slot if n_slots > 1 else 0, first_slot + 1 if n_slots > 1 else 1):
                    units.append((n, sl, G[(n, 2 * sl)], G[(n, 2 * sl + 1)], ca - 1))
            elif n == "w_dkv":
                g4 = G[(n, 0)].reshape(N_CHIPS, 2, -1, kv_w)
                units.append((n, 0, g4[:, 0], g4[:, 1], 0))
            else:
                rows_half = W[n].shape[0] // 2
                units.append((n, 0, G[(n, 0)][:rows_half], G[(n, 0)][rows_half:], ca))
        lo = [u[2] for u in units]
        hi = [u[3] for u in units]
        theirs = reduce_to_sibling(lo, hi, f"reduce_cores_{tag}")
        sums = [add_selected(l_.reshape(-1, l_.shape[-1]), h_.reshape(-1, l_.shape[-1]), t_.reshape(-1, l_.shape[-1]),
                             f"reduce_cores_add_{tag}{i}").reshape(l_.shape)
                for i, (l_, h_, t_) in enumerate(zip(lo, hi, theirs))]
        return units, sums, [u[4] for u in units]

    G = {}
    dmods = [None] * DEPTH
    d_norm1 = [None] * DEPTH
    d_norm2 = [None] * DEPTH
    d_conv_b = [None] * DEPTH
    d_qnorm = [None] * n_mla
    dkv_acc = []
    df, a2, _ = gate_bwd(dx, saved[DEPTH - 1]["f"], mod[DEPTH - 1][5], f"gate2_bwd{DEPTH - 1}")
    for l in reversed(range(DEPTH)):
        sh1, sc1, g1, sh2, sc2, g2 = mod[l]
        st = saved[l]
        next_gate = (saved[l - 1]["f"], mod[l - 1][5]) if l else None
        dz = mm(df, full[("w_down", l)], "nt", BF16, f"down_dx{l}")
        G[("w_down", l)] = mm(st["z"], df, "tn", BF16, f"down_dw{l}")
        du, dcw, dcb = glu_bwd(st["u"], dz, full["conv_w"][l], conv_b[l], f"glu_bwd{l}")
        G[("conv_w", l)] = dcw
        d_conv_b[l] = dcb[0]
        dh2 = mm(du, full[("w_up", l)], "nt", BF16, f"up_dx{l}", tk=1408)
        G[("w_up", l)] = mm(st["h2"], du, "tn", BF16, f"up_dw{l}")
        dxmid, s1, s2, dgate, a1, csum = norm_bwd(st["xmid"], norm2_g[l], sc2, dh2, dx, f"norm2_bwd{l}",
                                                  gate=(st["ypre"], st["cs"]) if l < N_A else (st["y"], g1))
        dsh2, dsc2, d_norm2[l] = s1[0], s2[0] * norm2_g[l], s2[0] * (1.0 + sc2)
        if l < N_A:
            dyp = dgate
            dg1 = full["pool_scale"][l] * a1[0]
            G[("pool_scale", l)] = g1 * a1[0]
            G[("pool_b", l)] = st["cs"] * csum[0]
            dpooled = gmm(dyp, full[("pool_w", l)], "nt", F32, f"pool_dx{l}")
            G[("pool_w", l)] = gmm(st["pooled"], dyp, "tn", BF16, f"pool_dw{l}")
            dh1 = _pool_call(dpooled, F32, f"pool_bwd{l}", True)
        else:
            j = l - N_A
            dy = dgate
            dg1 = a1[0]
            do = mm(dy, full[("w_o", j)], "nt", BF16, f"wo_dx{l}")
            G[("w_o", j)] = mm(st["o"], dy, "tn", BF16, f"wo_dw{l}")
            delta = attn_delta(st["o"], do, f"attn_delta{l}").reshape(N_HEADS, 1, S_)
            dQ, dK, dV = attn_bwd(st["Q"], K, VX, do, st["lse"], delta, f"attn_bwd{l}")
            dkv_acc.append((dK, dV))
            dqe = q_prep(dQ, tabs, att_scale, True, f"q_prep_bwd{l}")
            dcq = mm(dqe, st["w_uq_ext"], "nt", F32, f"uq_dx{l}")
            G[("w_uq", j)] = mm(st["cq"], dqe, "tn", BF16, f"uq_dw{l}").reshape(q_rank, N_HEADS, HEAD_PAD)[
                :, :, :QK_HEAD].reshape(q_rank, N_HEADS * QK_HEAD)
            zq = jnp.zeros_like(q_norm_g[j])
            dql, _, s2q = norm_bwd(st["ql"], q_norm_g[j], zq, dcq, None, f"qnorm_bwd{l}")
            d_qnorm[j] = s2q[0]
            dh1 = mm(dql, full[("w_dq", j)], "nt", BF16, f"dq_dx{l}")
            G[("w_dq", j)] = mm(st["h1"], dql, "tn", BF16, f"dq_dw{l}")
        a2_mine = a2
        if l and l != N_A:
            dx, s1, s2, df, a2, _ = norm_bwd(st["xin"], norm1_g[l], sc1, dh1, dxmid, f"norm1_bwd{l}", gate=next_gate)
        else:
            dx, s1, s2 = norm_bwd(st["xin"], norm1_g[l], sc1, dh1, dxmid, f"norm1_bwd{l}")
        dsh1, dsc1, d_norm1[l] = s1[0], s2[0] * norm1_g[l], s2[0] * (1.0 + sc1)
        dmods[l] = jnp.concatenate([dsh1, dsc1, dg1, dsh2, dsc2, a2_mine[0]])
        if l == N_A:
            (dk_a, dv_a), (dk_b, dv_b) = dkv_acc
            dknv, d_tk = k_prep_bwd(dk_a, dk_b, dv_a, dv_b, tabs, "k_prep_bwd")
            dckv = mm(dknv, w_ukv, "nt", F32, "ukv_dx")
            d_ukv = mm(kv_saved["ckv"], dknv, "tn", BF16, "ukv_dw")
            G[("w_uk", 0)], G[("w_uv", 0)] = d_ukv[:, :N_HEADS * QK_NOPE], d_ukv[:, N_HEADS * QK_NOPE:]
            zk = jnp.zeros((KV_RANK,), F32)
            dlat, _, s2c = norm_bwd(kv_saved["lat"], ckv_norm_g, zk, dckv, None, "ckv_bwd")
            d_ckv_g = s2c[0]
            dkv_ext = jnp.concatenate([dlat, d_tk], axis=1)
            dxn = mm(dkv_ext, w_dkv_ext, "nt", BF16, "dkv_dx")
            G[("w_dkv", 0)] = mm(kv_saved["xn"], dkv_ext, "tn", BF16, "dkv_dw")[:, :kv_w]
            dx, _, s2k, df, a2, _ = norm_bwd(kv_saved["x"], kv_in_g, zD, dxn, dx, "kvin_bwd", gate=next_gate)
            d_kvin_g = s2k[0]
            e_units, e_sums, e_axes = begin_reduce([n for n in EXCHANGED if n != "pool_w"], 1, "early")
            e_send, e_recv, e_pieces, e_lands, e_token = scatter_start(e_sums, e_axes, "reduce_chips_start")
            early = (e_units, e_send, e_recv, e_lands, e_pieces, e_axes)
            mod[l - 1][4] = mod[l - 1][4] + e_token[0, 0]

    small = {"mod_b": jnp.stack(dmods), "norm1_g": jnp.stack(d_norm1), "norm2_g": jnp.stack(d_norm2),
             "kv_in_g": d_kvin_g, "ckv_norm_g": d_ckv_g, "q_norm_g": jnp.stack(d_qnorm),
             "conv_b": jnp.stack(d_conv_b), "final_g": d_final_g[0]}
    extra = {n: jnp.stack([G[(n, i)] for i in range(W[n].shape[0])]) for n in SMALL_SHARDED}
    ssizes = {n: math.prod(W[n].shape) for n in REPLICATED}
    esizes = {n: math.prod(extra[n].shape) for n in SMALL_SHARDED}
    Ts = _padded(sum(ssizes.values()) + sum(esizes.values()), 8 * PACK_COLS)

    def pack_small(d, tail=()):
        return _flat_pad([d[n].reshape(-1) for n in REPLICATED] + [t.reshape(-1) for t in tail],
                         Ts).reshape(Ts // PACK_COLS, PACK_COLS)

    parts = device_gather(pack_small(small, [extra[n] for n in SMALL_SHARDED]), "gather_small")

    l_units, l_sums, l_axes = begin_reduce([n for n in EXCHANGED if W[n].ndim > 2 and W[n].shape[0] == DEPTH] + ["pool_w"],
                                           0, "late")
    l_send, l_recv, l_pieces, l_lands, l_token = scatter_start(l_sums, l_axes, "reduce_chips_late_start", after=[parts])
    parts = parts + l_token[0, 0]

    grads, deltas, new_m, new_v = {}, {}, {}, {}
    outs = adamw_sum(parts, pack_small(W), pack_small(M1), pack_small(V2), "adamw_small")
    off = 0
    for n in REPLICATED:
        for dst, o in zip((grads, deltas, new_m, new_v), outs):
            dst[n] = o.reshape(-1)[off:off + ssizes[n]].reshape(W[n].shape)
        off += ssizes[n]
    for n, axis in SMALL_SHARDED.items():
        g_full = outs[0].reshape(-1)[off:off + esizes[n]].reshape(extra[n].shape)
        off += esizes[n]
        size = W[n].shape[axis]
        grads[n] = lax.dynamic_slice_in_dim(g_full, chip * size, size, axis=axis)
        deltas[n], new_m[n], new_v[n] = adamw(W[n], grads[n], M1[n], V2[n], f"adamw_{n}")

    dm_all = parts.reshape(N_DEV, -1)[:, :DEPTH * E].reshape(N_DEV, DEPTH, E)
    dm_mine = jnp.swapaxes(lax.dynamic_slice_in_dim(dm_all, chip * Es, Es, axis=2), 0, 1)
    grads["mod_w"], deltas["mod_w"], new_m["mod_w"], new_v["mod_w"] = adamw_modw(
        c_all.reshape(N_DEV, D, 1), dm_mine, mod_w, m_mod_w, v_mod_w, "adamw_mod_w")

    def finish_reduce(pieces, axes, got, tag):
        out = []
        for i, (sm, ax, g4) in enumerate(zip(pieces, axes, got)):
            size = sm.shape[ax] // N_CHIPS
            g4 = lax.dynamic_update_index_in_dim(g4, lax.dynamic_slice_in_dim(sm, chip * size, size, axis=ax), chip, 0)
            blk = g4.shape[1:]
            out.append(sum_parts(g4.reshape(N_CHIPS, -1, blk[-1]), f"reduce_chips_add_{tag}{i}").reshape(blk))
        return out

    e_units, e_send, e_recv, e_lands, e_pieces, e_axes = early
    early_got = scatter_wait(e_send, e_recv, e_lands, e_pieces, dx, "reduce_chips_wait")
    reduced = finish_reduce(e_pieces, e_axes, early_got, "early")
    late_got = scatter_wait(l_send, l_recv, l_lands, l_pieces, new_v["mod_w"], "reduce_chips_late_wait")
    reduced += finish_reduce(l_pieces, l_axes, late_got, "late")
    units = e_units + l_units
    slots, out_shapes = [], []
    for n in EXCHANGED:
        mine = [i for i, u in enumerate(units) if u[0] == n]
        out_shapes.append((len(mine), 2) + reduced[mine[0]].shape)
        slots += [(len(out_shapes) - 1, units[i][1]) for i in mine]
    order = [i for n in EXCHANGED for i, u in enumerate(units) if u[0] == n]
    halves = gather_halves([reduced[i] for i in order], slots, out_shapes, "reduce_gather")
    for ti, n in enumerate(EXCHANGED):
        g = halves[ti]
        for i, u in enumerate(units):
            if u[0] == n:
                g = lax.dynamic_update_slice(g, reduced[i][None, None], (u[1], ci) + (0,) * reduced[i].ndim)
        grads[n] = g.reshape(W[n].shape)
        deltas[n], new_m[n], new_v[n] = adamw(W[n], grads[n], M1[n], V2[n], f"adamw_{n}")

    return (loss, dx.reshape(x.shape), *[grads[n] for n in WEIGHT_ORDER], *[deltas[n] for n in WEIGHT_ORDER],
            *[new_m[n] for n in WEIGHT_ORDER], *[new_v[n] for n in WEIGHT_ORDER])
```

```python
import functools
import math

import jax
import jax.numpy as jnp
from jax import lax
from jax.experimental import pallas as pl
from jax.experimental.pallas import tpu as pltpu

F32 = jnp.float32
BF16 = jnp.bfloat16
MESH = pl.DeviceIdType.MESH

DEPTH = 4
N_A = 2
POOL_WINDOWS = (2, 4, 8, 16)
N_GROUPS = 4
N_HEADS = 8
QK_NOPE = 128
QK_ROPE = 64
V_HEAD = 128
QK_HEAD = QK_NOPE + QK_ROPE
HEAD_PAD = 256
KV_RANK = 256
ROPE_THETA = 10000.0
EPS = 1e-6
ADAM_LR = 0.001
ADAM_B1 = 0.9
ADAM_B2 = 0.999
ADAM_EPS = 1e-08
ADAM_WD = 0.01
ADAM_STEP = 10

N_CHIPS = 4
N_DEV = 8
LANES = 128
PACK_COLS = 1024
VMEM_LIMIT = 56 * 1024 * 1024
GLU_TILE = 256
ATT_BWD_K_BLOCK = 512
ATT_BWD_Q_BLOCK = 512
ATT_Q_BLOCK = 1024
ATT_K_BLOCK = 512
ATT_HEADS_PER_STEP = 2


def _cparams(*sem):
    return pltpu.CompilerParams(dimension_semantics=sem if sem else None, vmem_limit_bytes=VMEM_LIMIT)


def _pick(n, target, mult):
    best = None
    d = mult
    while d <= min(n, target):
        if n % d == 0:
            best = d
        d += mult
    return n if best is None else best


def _row(v):
    return v.reshape(1, -1).astype(F32)


_DIMS = {"nn": (((1,), (0,)), ((), ())), "nt": (((1,), (1,)), ((), ())), "tn": (((0,), (0,)), ((), ()))}


def _mm_body(mode, nk, has_bias, has_res):
    def body(*refs):
        a_ref, b_ref = refs[0], refs[1]
        pos = 2
        bias_ref = res_ref = cs_ref = None
        if has_bias:
            bias_ref = refs[pos]
            pos += 1
        if has_res:
            res_ref, cs_ref = refs[pos], refs[pos + 1]
            pos += 2
        o_ref = refs[pos]
        pos += 1
        o2_ref = None
        if has_res:
            o2_ref = refs[pos]
            pos += 1
        acc_ref = refs[pos] if nk > 1 else None
        k = pl.program_id(2)
        part = lax.dot_general(a_ref[...].astype(BF16), b_ref[...].astype(BF16), _DIMS[mode],
                               preferred_element_type=F32)

        def finish(y):
            if has_bias:
                y = y + bias_ref[...]
            o_ref[...] = y.astype(o_ref.dtype)
            if has_res:
                o2_ref[...] = res_ref[...] + cs_ref[...] * y

        if nk == 1:
            finish(part)
            return

        @pl.when(k == 0)
        def _():
            acc_ref[...] = part

        @pl.when((k > 0) & (k < nk - 1))
        def _():
            acc_ref[...] += part

        @pl.when(k == nk - 1)
        def _():
            finish(acc_ref[...] + part)

    return body


def mm(a, b, mode, out_dtype, name, *, tm=1408, tn=1408, tk=1024, bias=None, res=None, colscale=None, layer=None):
    bshape = b.shape if layer is None else b.shape[1:]
    if mode == "nn":
        (M, K), N = a.shape, bshape[1]
    elif mode == "nt":
        (M, K), N = a.shape, bshape[0]
    else:
        (K, M), N = a.shape, bshape[1]
    tm = _pick(M, tm, LANES if mode == "tn" else 8)
    tn = _pick(N, tn, LANES)
    tk = _pick(K, tk, LANES) if mode != "tn" else _pick(K, tk, 8)
    nk = K // tk
    a_spec = {"nn": pl.BlockSpec((tm, tk), lambda i, j, k: (i, k)),
              "nt": pl.BlockSpec((tm, tk), lambda i, j, k: (i, k)),
              "tn": pl.BlockSpec((tk, tm), lambda i, j, k: (k, i))}[mode]
    b_blk, b_map = {"nn": ((tk, tn), lambda i, j, k: (k, j)),
                    "nt": ((tn, tk), lambda i, j, k: (j, k)),
                    "tn": ((tk, tn), lambda i, j, k: (k, j))}[mode]
    if layer is None:
        b_spec = pl.BlockSpec(b_blk, b_map)
    else:
        b_spec = pl.BlockSpec((None,) + b_blk, lambda i, j, k: (layer,) + b_map(i, j, k))
    o_spec = pl.BlockSpec((tm, tn), lambda i, j, k: (i, j))
    v_spec = pl.BlockSpec((1, tn), lambda i, j, k: (0, j))
    in_specs, args = [a_spec, b_spec], [a, b]
    if bias is not None:
        in_specs.append(v_spec)
        args.append(_row(bias))
    out_shape = [jax.ShapeDtypeStruct((M, N), out_dtype)]
    out_specs = [o_spec]
    if res is not None:
        in_specs += [o_spec, v_spec]
        args += [res, _row(colscale)]
        out_shape.append(jax.ShapeDtypeStruct((M, N), F32))
        out_specs.append(o_spec)
    outs = pl.pallas_call(
        _mm_body(mode, nk, bias is not None, res is not None),
        grid=(M // tm, N // tn, nk),
        in_specs=in_specs, out_specs=out_specs, out_shape=out_shape,
        scratch_shapes=[pltpu.VMEM((tm, tn), F32)] if nk > 1 else [],
        compiler_params=_cparams("parallel", "parallel", "arbitrary"),
        name=name,
    )(*args)
    return outs if res is not None else outs[0]


def gmm(a, w, mode, out_dtype, name, *, bias=None, res=None, colscale=None, tr=512):
    S_ = a.shape[0]
    G = N_GROUPS
    C = a.shape[1] // G
    tr = _pick(S_, tr, 8)
    nr = S_ // tr
    if mode == "tn":
        def body(a_ref, b_ref, o_ref, acc_ref):
            i = pl.program_id(1)

            @pl.when(i == 0)
            def _():
                acc_ref[...] = jnp.zeros_like(acc_ref)

            acc_ref[...] += lax.dot_general(a_ref[...].astype(BF16), b_ref[...].astype(BF16), _DIMS["tn"],
                                            preferred_element_type=F32)

            @pl.when(i == nr - 1)
            def _():
                o_ref[...] = acc_ref[...].astype(o_ref.dtype)

        blk = pl.BlockSpec((tr, C), lambda g, i: (i, g))
        return pl.pallas_call(
            body, grid=(G, nr), in_specs=[blk, blk],
            out_specs=pl.BlockSpec((None, C, C), lambda g, i: (g, 0, 0)),
            out_shape=jax.ShapeDtypeStruct((G, C, C), out_dtype),
            scratch_shapes=[pltpu.VMEM((C, C), F32)],
            compiler_params=_cparams("parallel", "arbitrary"), name=name,
        )(a, w)

    has_bias, has_res = bias is not None, res is not None

    def body(*refs):
        a_ref, w_ref = refs[0], refs[1]
        pos = 2
        if has_bias:
            bias_ref = refs[pos]
            pos += 1
        if has_res:
            res_ref, cs_ref = refs[pos], refs[pos + 1]
            pos += 2
        o_ref = refs[pos]
        y = lax.dot_general(a_ref[...].astype(BF16), w_ref[...].astype(BF16), _DIMS[mode],
                            preferred_element_type=F32)
        if has_bias:
            y = y + bias_ref[...]
        o_ref[...] = y.astype(o_ref.dtype)
        if has_res:
            refs[pos + 1][...] = res_ref[...] + cs_ref[...] * y

    blk = pl.BlockSpec((tr, C), lambda i, g: (i, g))
    vec = pl.BlockSpec((1, C), lambda i, g: (0, g))
    in_specs = [blk, pl.BlockSpec((None, C, C), lambda i, g: (g, 0, 0))]
    args = [a, w]
    if has_bias:
        in_specs.append(vec)
        args.append(_row(bias))
    out_shape = [jax.ShapeDtypeStruct(a.shape, out_dtype)]
    out_specs = [blk]
    if has_res:
        in_specs += [blk, vec]
        args += [res, _row(colscale)]
        out_shape.append(jax.ShapeDtypeStruct(a.shape, F32))
        out_specs.append(blk)
    outs = pl.pallas_call(
        body, grid=(nr, G), in_specs=in_specs, out_specs=out_specs, out_shape=out_shape,
        compiler_params=_cparams("parallel", "parallel"), name=name,
    )(*args)
    return outs if has_res else outs[0]


def norm_fwd(x, g, sc, sh, out_dtype, name, tr=512):
    S_, Dn = x.shape
    tr = _pick(S_, tr, 8)

    def body(x_ref, g_ref, sc_ref, sh_ref, o_ref):
        xv = x_ref[...]
        r = lax.rsqrt(jnp.mean(xv * xv, axis=-1, keepdims=True) + EPS)
        o_ref[...] = (((xv * r) * g_ref[...]) * (1.0 + sc_ref[...]) + sh_ref[...]).astype(o_ref.dtype)

    blk = pl.BlockSpec((tr, Dn), lambda i: (i, 0))
    vec = pl.BlockSpec((1, Dn), lambda i: (0, 0))
    return pl.pallas_call(
        body, grid=(S_ // tr,), in_specs=[blk, vec, vec, vec], out_specs=blk,
        out_shape=jax.ShapeDtypeStruct((S_, Dn), out_dtype),
        compiler_params=_cparams("parallel"), name=name,
    )(x, _row(g), _row(sc), _row(sh))


def norm_bwd(x, g, sc, dh, dres, name, gate=None, tr=512):
    S_, Dn = x.shape
    tr = _pick(S_, tr, 8)
    has_res = dres is not None
    has_gate = gate is not None

    def body(*refs):
        x_ref, g_ref, sc_ref, dh_ref = refs[:4]
        pos = 4
        if has_res:
            dres_ref = refs[pos]
            pos += 1
        if has_gate:
            y_ref, cs_ref = refs[pos:pos + 2]
            pos += 2
        dx_ref, s1_ref, s2_ref = refs[pos:pos + 3]
        if has_gate:
            d_ref, a_ref, c_ref = refs[pos + 3:pos + 6]
        i = pl.program_id(0)

        @pl.when(i == 0)
        def _():
            s1_ref[...] = jnp.zeros_like(s1_ref)
            s2_ref[...] = jnp.zeros_like(s2_ref)
            if has_gate:
                a_ref[...] = jnp.zeros_like(a_ref)
                c_ref[...] = jnp.zeros_like(c_ref)

        xv = x_ref[...]
        r = lax.rsqrt(jnp.mean(xv * xv, axis=-1, keepdims=True) + EPS)
        n = xv * r
        dhv = dh_ref[...].astype(F32)
        dn = dhv * (g_ref[...] * (1.0 + sc_ref[...]))
        dx = r * (dn - n * jnp.mean(dn * n, axis=-1, keepdims=True))
        if has_res:
            dx = dx + dres_ref[...]
        dx_ref[...] = dx
        s1_ref[...] += jnp.sum(dhv, axis=0, keepdims=True)
        s2_ref[...] += jnp.sum(dhv * n, axis=0, keepdims=True)
        if has_gate:
            d_ref[...] = (dx * cs_ref[...]).astype(d_ref.dtype)
            a_ref[...] += jnp.sum(dx * y_ref[...].astype(F32), axis=0, keepdims=True)
            c_ref[...] += jnp.sum(dx, axis=0, keepdims=True)

    blk = pl.BlockSpec((tr, Dn), lambda i: (i, 0))
    vec = pl.BlockSpec((1, Dn), lambda i: (0, 0))
    in_specs, args = [blk, vec, vec, blk], [x, _row(g), _row(sc), dh]
    if has_res:
        in_specs.append(blk)
        args.append(dres)
    vshape = jax.ShapeDtypeStruct((1, Dn), F32)
    out_specs = [blk, vec, vec]
    out_shape = [jax.ShapeDtypeStruct((S_, Dn), F32), vshape, vshape]
    if has_gate:
        in_specs += [blk, vec]
        args += [gate[0], _row(gate[1])]
        out_specs += [blk, vec, vec]
        out_shape += [jax.ShapeDtypeStruct((S_, Dn), BF16), vshape, vshape]
    return pl.pallas_call(
        body, grid=(S_ // tr,), in_specs=in_specs, out_specs=out_specs, out_shape=out_shape,
        compiler_params=_cparams("arbitrary"), name=name,
    )(*args)


def gate_bwd(dx, y, colscale, name, tr=512):
    S_, Dn = dx.shape
    tr = _pick(S_, tr, 8)

    def body(dx_ref, y_ref, cs_ref, d_ref, a_ref, c_ref):
        i = pl.program_id(0)

        @pl.when(i == 0)
        def _():
            a_ref[...] = jnp.zeros_like(a_ref)
            c_ref[...] = jnp.zeros_like(c_ref)

        dxv = dx_ref[...]
        d_ref[...] = (dxv * cs_ref[...]).astype(d_ref.dtype)
        a_ref[...] += jnp.sum(dxv * y_ref[...].astype(F32), axis=0, keepdims=True)
        c_ref[...] += jnp.sum(dxv, axis=0, keepdims=True)

    blk = pl.BlockSpec((tr, Dn), lambda i: (i, 0))
    vec = pl.BlockSpec((1, Dn), lambda i: (0, 0))
    vshape = jax.ShapeDtypeStruct((1, Dn), F32)
    return pl.pallas_call(
        body, grid=(S_ // tr,), in_specs=[blk, blk, vec], out_specs=[blk, vec, vec],
        out_shape=[jax.ShapeDtypeStruct((S_, Dn), BF16), vshape, vshape],
        compiler_params=_cparams("arbitrary"), name=name,
    )(dx, y, _row(colscale))


def loss_head(x, g, target, name, tr=512):
    S_, Dn = x.shape
    tr = _pick(S_, tr, 8)

    def body(x_ref, g_ref, t_ref, dx_ref, dg_ref, loss_ref):
        i = pl.program_id(0)

        @pl.when(i == 0)
        def _():
            dg_ref[...] = jnp.zeros_like(dg_ref)
            loss_ref[...] = jnp.zeros_like(loss_ref)

        xv = x_ref[...]
        r = lax.rsqrt(jnp.mean(xv * xv, axis=-1, keepdims=True) + EPS)
        n = xv * r
        e = n * g_ref[...] - t_ref[...]
        loss_ref[...] += 0.5 * jnp.sum(jnp.mean(e * e, axis=-1, keepdims=True), axis=0, keepdims=True)
        dy = e * (1.0 / Dn)
        dg_ref[...] += jnp.sum(dy * n, axis=0, keepdims=True)
        dn = dy * g_ref[...]
        dx_ref[...] = r * (dn - n * jnp.mean(dn * n, axis=-1, keepdims=True))

    blk = pl.BlockSpec((tr, Dn), lambda i: (i, 0))
    vec = pl.BlockSpec((1, Dn), lambda i: (0, 0))
    one = pl.BlockSpec((1, 1), lambda i: (0, 0))
    return pl.pallas_call(
        body, grid=(S_ // tr,), in_specs=[blk, vec, blk], out_specs=[blk, vec, one],
        out_shape=[jax.ShapeDtypeStruct((S_, Dn), F32), jax.ShapeDtypeStruct((1, Dn), F32),
                   jax.ShapeDtypeStruct((1, 1), F32)],
        compiler_params=_cparams("arbitrary"), name=name,
    )(x, _row(g), target)


POOL_HALO = 16
POOL_CHUNK = 512


def _rows(ref, lo, hi, n_rows):
    parts = []
    if lo < 0:
        parts.append(jnp.zeros((-lo, ref.shape[1]), F32))
    parts.append(ref[max(lo, 0):min(hi, n_rows), :].astype(F32))
    if hi > n_rows:
        parts.append(jnp.zeros((hi - n_rows, ref.shape[1]), F32))
    return parts[0] if len(parts) == 1 else jnp.concatenate(parts, axis=0)


def _window_sum(e, w, back):
    n = e.shape[0]
    s, width = e, 1
    while width < w:
        s = s + pltpu.roll(s, width if back else n - width, 0)
        width *= 2
    return s


def _pool_call(h, out_dtype, name, backward):
    S_, Dn = h.shape
    C = Dn // N_GROUPS
    ch = _pick(S_, POOL_CHUNK, 8)

    def body(h_ref, o_ref):
        g = pl.program_id(0)
        for gi, w in enumerate(POOL_WINDOWS):
            @pl.when(g == gi)
            def _(w=w):
                for r0 in range(0, S_, ch):
                    t = (r0 + lax.broadcasted_iota(jnp.int32, (ch, C), 0)).astype(F32)
                    cnt = jnp.minimum(t + 1.0, float(w))
                    if not backward:
                        ext = _rows(h_ref, r0 - POOL_HALO, r0 + ch, S_)
                        cur = ext[POOL_HALO:]
                        mean = _window_sum(ext, w, True)[POOL_HALO:] / cnt
                        o_ref[r0:r0 + ch, :] = (mean - cur).astype(o_ref.dtype)
                    else:
                        ext = _rows(h_ref, r0, r0 + ch + POOL_HALO, S_)
                        text = (r0 + lax.broadcasted_iota(jnp.int32, (ch + POOL_HALO, C), 0)).astype(F32)
                        e = ext / jnp.minimum(text + 1.0, float(w))
                        o_ref[r0:r0 + ch, :] = (_window_sum(e, w, False)[:ch] - ext[:ch]).astype(o_ref.dtype)

    blk = pl.BlockSpec((S_, C), lambda g: (0, g))
    return pl.pallas_call(
        body, grid=(N_GROUPS,), in_specs=[blk], out_specs=blk,
        out_shape=jax.ShapeDtypeStruct((S_, Dn), out_dtype),
        compiler_params=_cparams("parallel"), name=name,
    )(h)


GLU_CHUNK = 512
GLU_HALO = 16
_SQRT_HALF = 0.7071067811865476
_INV_SQRT_2PI = 0.3989422804014327


def _gelu(a):
    return 0.5 * a * (1.0 + lax.erf(a * _SQRT_HALF))


def _gelu_grad(a):
    return 0.5 * (1.0 + lax.erf(a * _SQRT_HALF)) + a * (_INV_SQRT_2PI * jnp.exp(-0.5 * a * a))


def glu_fwd(u, conv_w, conv_b, name):
    S_, F2 = u.shape
    Fh = F2 // 2
    tf = GLU_TILE
    nt = Fh // tf
    ch = _pick(S_, GLU_CHUNK, GLU_HALO)

    def body(a_ref, v_ref, cw_ref, cb_ref, z_ref):
        cw0, cw1, cw2 = cw_ref[0:1, :], cw_ref[1:2, :], cw_ref[2:3, :]
        cb = cb_ref[...]
        for r0 in range(0, S_, ch):
            ext = _rows(a_ref, r0 - GLU_HALO, r0 + ch, S_)
            a0 = ext[GLU_HALO:]
            a1 = pltpu.roll(ext, 1, 0)[GLU_HALO:]
            a2 = pltpu.roll(ext, 2, 0)[GLU_HALO:]
            ac = a2 * cw0 + a1 * cw1 + a0 * cw2 + cb
            z_ref[r0:r0 + ch, :] = (_gelu(ac) * v_ref[r0:r0 + ch, :].astype(F32)).astype(z_ref.dtype)

    return pl.pallas_call(
        body, grid=(nt,),
        in_specs=[pl.BlockSpec((S_, tf), lambda j: (0, j)), pl.BlockSpec((S_, tf), lambda j: (0, j + nt)),
                  pl.BlockSpec((3, tf), lambda j: (0, j)), pl.BlockSpec((1, tf), lambda j: (0, j))],
        out_specs=pl.BlockSpec((S_, tf), lambda j: (0, j)),
        out_shape=jax.ShapeDtypeStruct((S_, Fh), BF16),
        compiler_params=_cparams("parallel"), name=name,
    )(u, u, conv_w, _row(conv_b))


def glu_bwd(u, dz, conv_w, conv_b, name):
    S_, F2 = u.shape
    Fh = F2 // 2
    tf = GLU_TILE
    nt = Fh // tf
    ch = _pick(S_, GLU_CHUNK, GLU_HALO)

    def body(a_ref, v_ref, dz_ref, cw_ref, cb_ref, du_ref, dcw_ref, dcb_ref, da_buf, dv_buf, sems):
        j = pl.program_id(0)
        slot = j % 2

        def writes(step, sl):
            lo = pl.multiple_of(step * tf, tf)
            return (pltpu.make_async_copy(da_buf.at[sl], du_ref.at[:, pl.ds(lo, tf)], sems.at[sl, 0]),
                    pltpu.make_async_copy(dv_buf.at[sl], du_ref.at[:, pl.ds(Fh + lo, tf)], sems.at[sl, 1]))

        @pl.when(j >= 2)
        def _():
            for cp in writes(j - 2, slot):
                cp.wait()

        cw0, cw1, cw2 = cw_ref[0:1, :], cw_ref[1:2, :], cw_ref[2:3, :]
        cb = cb_ref[...]
        acc = [jnp.zeros((1, tf), F32) for _ in range(4)]
        n = ch + GLU_HALO
        for r0 in range(0, S_, ch):
            ext = _rows(a_ref, r0 - GLU_HALO, r0 + n, S_)
            a0 = ext[GLU_HALO:]
            a1 = pltpu.roll(ext, 1, 0)[GLU_HALO:]
            a2 = pltpu.roll(ext, 2, 0)[GLU_HALO:]
            ac = a2 * cw0 + a1 * cw1 + a0 * cw2 + cb
            vv = _rows(v_ref, r0, r0 + n, S_)
            dzv = _rows(dz_ref, r0, r0 + n, S_)
            gl = _gelu(ac)
            dac = dzv * vv * _gelu_grad(ac)
            da = (dac * cw2 + pltpu.roll(dac, n - 1, 0) * cw1 + pltpu.roll(dac, n - 2, 0) * cw0)[:ch]
            da_buf[slot, r0:r0 + ch, :] = da.astype(da_buf.dtype)
            dv_buf[slot, r0:r0 + ch, :] = (dzv[:ch] * gl[:ch]).astype(dv_buf.dtype)
            dc = dac[:ch]
            acc[0] = acc[0] + jnp.sum(dc * a2[:ch], axis=0, keepdims=True)
            acc[1] = acc[1] + jnp.sum(dc * a1[:ch], axis=0, keepdims=True)
            acc[2] = acc[2] + jnp.sum(dc * a0[:ch], axis=0, keepdims=True)
            acc[3] = acc[3] + jnp.sum(dc, axis=0, keepdims=True)
        dcw_ref[0:1, :] = acc[0]
        dcw_ref[1:2, :] = acc[1]
        dcw_ref[2:3, :] = acc[2]
        dcb_ref[...] = acc[3]
        for cp in writes(j, slot):
            cp.start()

        @pl.when(j == nt - 1)
        def _():
            for cp in writes(j, slot):
                cp.wait()
            if nt > 1:
                for cp in writes(j - 1, 1 - slot):
                    cp.wait()

    return pl.pallas_call(
        body, grid=(nt,),
        in_specs=[pl.BlockSpec((S_, tf), lambda j: (0, j)), pl.BlockSpec((S_, tf), lambda j: (0, j + nt)),
                  pl.BlockSpec((S_, tf), lambda j: (0, j)),
                  pl.BlockSpec((3, tf), lambda j: (0, j)), pl.BlockSpec((1, tf), lambda j: (0, j))],
        out_specs=[_ANY, pl.BlockSpec((3, tf), lambda j: (0, j)), pl.BlockSpec((1, tf), lambda j: (0, j))],
        out_shape=[jax.ShapeDtypeStruct((S_, F2), BF16), jax.ShapeDtypeStruct((3, Fh), F32),
                   jax.ShapeDtypeStruct((1, Fh), F32)],
        scratch_shapes=[pltpu.VMEM((2, S_, tf), BF16), pltpu.VMEM((2, S_, tf), BF16), pltpu.SemaphoreType.DMA((2, 2))],
        compiler_params=_cparams("arbitrary"), name=name,
    )(u, u, dz, conv_w, _row(conv_b))


def rope_tables(pos, inv, name, tr=512):
    S_ = pos.shape[0]
    tr = _pick(S_, tr, 8)

    def body(p_ref, inv_ref, c_ref, s1_ref, s2_ref):
        ang = p_ref[...] * inv_ref[...]
        lane = lax.broadcasted_iota(jnp.int32, ang.shape, 1)
        half = QK_ROPE // 2
        cosv, sinv = jnp.cos(ang), jnp.sin(ang)
        c_ref[...] = jnp.where(lane < QK_ROPE, cosv, 0.0)
        s1_ref[...] = jnp.where(lane < half, -sinv, 0.0)
        s2_ref[...] = jnp.where((lane >= half) & (lane < QK_ROPE), sinv, 0.0)

    blk = pl.BlockSpec((tr, LANES), lambda i: (i, 0))
    shp = jax.ShapeDtypeStruct((S_, LANES), F32)
    return pl.pallas_call(
        body, grid=(S_ // tr,),
        in_specs=[pl.BlockSpec((tr, 1), lambda i: (i, 0)), pl.BlockSpec((1, LANES), lambda i: (0, 0))],
        out_specs=[blk, blk, blk], out_shape=[shp, shp, shp],
        compiler_params=_cparams("parallel"), name=name,
    )(pos, inv)


_HALF = QK_ROPE // 2


def _rope(t, c, s1, s2):
    return t * c + pltpu.roll(t, LANES - _HALF, 1) * s1 + pltpu.roll(t, _HALF, 1) * s2


def _rope_t(d, c, s1, s2):
    return d * c + pltpu.roll(d * s1, _HALF, 1) + pltpu.roll(d * s2, LANES - _HALF, 1)


def q_prep(q, tabs, scale, backward, name, tr=512):
    S_, W = q.shape
    tr = _pick(S_, tr, 8)

    def body(q_ref, c_ref, s1_ref, s2_ref, o_ref):
        o_ref[:, 0:LANES] = (q_ref[:, 0:LANES].astype(F32) * scale).astype(o_ref.dtype)
        t = q_ref[:, LANES:2 * LANES].astype(F32)
        fn = _rope_t if backward else _rope
        o_ref[:, LANES:2 * LANES] = (fn(t, c_ref[...], s1_ref[...], s2_ref[...]) * scale).astype(o_ref.dtype)

    blk = pl.BlockSpec((tr, HEAD_PAD), lambda i, h: (i, h))
    tab = pl.BlockSpec((tr, LANES), lambda i, h: (i, 0))
    return pl.pallas_call(
        body, grid=(S_ // tr, W // HEAD_PAD), in_specs=[blk, tab, tab, tab], out_specs=blk,
        out_shape=jax.ShapeDtypeStruct((S_, W), BF16),
        compiler_params=_cparams("parallel", "parallel"), name=name,
    )(q, *tabs)


def k_prep(knv, kv_ext, tabs, name, tr=512):
    S_ = knv.shape[0]
    tr = _pick(S_, tr, 8)

    def body(kn_ref, v_ref, t_ref, c_ref, s1_ref, s2_ref, o_ref, vx_ref):
        o_ref[:, 0:LANES] = kn_ref[...].astype(o_ref.dtype)
        o_ref[:, LANES:2 * LANES] = _rope(t_ref[...], c_ref[...], s1_ref[...], s2_ref[...]).astype(o_ref.dtype)
        vx_ref[:, 0:V_HEAD] = v_ref[...].astype(vx_ref.dtype)
        vx_ref[:, V_HEAD:HEAD_PAD] = jnp.ones((tr, HEAD_PAD - V_HEAD), vx_ref.dtype)

    tab = pl.BlockSpec((tr, LANES), lambda i, h: (i, 0))
    head = pl.BlockSpec((tr, HEAD_PAD), lambda i, h: (i, h))
    shp = jax.ShapeDtypeStruct((S_, N_HEADS * HEAD_PAD), BF16)
    return pl.pallas_call(
        body, grid=(S_ // tr, N_HEADS),
        in_specs=[pl.BlockSpec((tr, LANES), lambda i, h: (i, h)),
                  pl.BlockSpec((tr, V_HEAD), lambda i, h: (i, N_HEADS + h)),
                  pl.BlockSpec((tr, LANES), lambda i, h: (i, KV_RANK // LANES)), tab, tab, tab],
        out_specs=[head, head], out_shape=[shp, shp],
        compiler_params=_cparams("parallel", "parallel"), name=name,
    )(knv, knv, kv_ext, *tabs)


def k_prep_bwd(dk_a, dk_b, dv_a, dv_b, tabs, name, tr=256):
    S_ = dk_a.shape[0]
    tr = _pick(S_, tr, 8)
    HV = N_HEADS * V_HEAD

    def body(ka_ref, kb_ref, va_ref, vb_ref, c_ref, s1_ref, s2_ref, o_ref, t_ref):
        dr = jnp.zeros((tr, LANES), F32)
        for h in range(N_HEADS):
            lo = h * HEAD_PAD
            o_ref[:, h * LANES:(h + 1) * LANES] = (ka_ref[:, lo:lo + LANES] + kb_ref[:, lo:lo + LANES]).astype(o_ref.dtype)
            dr = dr + ka_ref[:, lo + LANES:lo + 2 * LANES] + kb_ref[:, lo + LANES:lo + 2 * LANES]
        o_ref[:, HV:2 * HV] = (va_ref[...] + vb_ref[...]).astype(o_ref.dtype)
        t_ref[...] = _rope_t(dr, c_ref[...], s1_ref[...], s2_ref[...])

    kblk = pl.BlockSpec((tr, N_HEADS * HEAD_PAD), lambda i: (i, 0))
    vblk = pl.BlockSpec((tr, HV), lambda i: (i, 0))
    tab = pl.BlockSpec((tr, LANES), lambda i: (i, 0))
    return pl.pallas_call(
        body, grid=(S_ // tr,), in_specs=[kblk, kblk, vblk, vblk, tab, tab, tab],
        out_specs=[pl.BlockSpec((tr, 2 * HV), lambda i: (i, 0)), tab],
        out_shape=[jax.ShapeDtypeStruct((S_, 2 * HV), BF16), jax.ShapeDtypeStruct((S_, LANES), F32)],
        compiler_params=_cparams("parallel"), name=name,
    )(dk_a, dk_b, dv_a, dv_b, *tabs)


_NEG = -1e30


def attn_fwd(q, k, vx, name):
    S_ = q.shape[0]
    TQ = _pick(S_, ATT_Q_BLOCK, 8)
    TK = _pick(S_, ATT_K_BLOCK, 8)
    assert TQ % TK == 0 or TK % TQ == 0
    HP = ATT_HEADS_PER_STEP
    W = HP * HEAD_PAD

    def body(q_ref, k_ref, v_ref, o_ref, lse_ref):
        i = pl.program_id(1)
        qs = [q_ref[:, h * HEAD_PAD:(h + 1) * HEAD_PAD] for h in range(HP)]

        def step(j, carry, masked):
            start = pl.multiple_of(j * TK, TK)
            out = []
            for h in range(HP):
                m, acc = carry[h]
                cols = slice(h * HEAD_PAD, (h + 1) * HEAD_PAD)
                s = lax.dot_general(qs[h], k_ref[pl.ds(start, TK), cols], _DIMS["nt"], preferred_element_type=F32)
                if masked:
                    rowi = i * TQ + lax.broadcasted_iota(jnp.int32, (TQ, TK), 0)
                    coli = j * TK + lax.broadcasted_iota(jnp.int32, (TQ, TK), 1)
                    s = jnp.where(coli <= rowi, s, _NEG)
                m_new = jnp.maximum(m, jnp.max(s, axis=-1, keepdims=True))
                alpha = jnp.exp(m - m_new)
                p = jnp.exp(s - m_new).astype(BF16)
                acc = alpha * acc + lax.dot_general(p, v_ref[pl.ds(start, TK), cols], _DIMS["nn"],
                                                    preferred_element_type=F32)
                out.append((m_new, acc))
            return tuple(out)

        init = tuple((jnp.full((TQ, 1), _NEG, F32), jnp.zeros((TQ, HEAD_PAD), F32)) for _ in range(HP))
        n_full, n_diag = (i * (TQ // TK), TQ // TK) if TQ >= TK else (i // (TK // TQ), 1)
        carry = lax.fori_loop(0, n_full, functools.partial(step, masked=False), init)
        for d in range(n_diag):
            carry = step(n_full + d, carry, True)
        for h in range(HP):
            m, acc = carry[h]
            l = acc[:, V_HEAD:]
            o_ref[:, h * V_HEAD:(h + 1) * V_HEAD] = (acc[:, :V_HEAD] / l).astype(o_ref.dtype)
            lse_ref[h] = m + jnp.log(jnp.max(l, axis=-1, keepdims=True))

    return pl.pallas_call(
        body, grid=(N_HEADS // HP, S_ // TQ),
        in_specs=[pl.BlockSpec((TQ, W), lambda g, i: (i, g)),
                  pl.BlockSpec((S_, W), lambda g, i: (0, g)),
                  pl.BlockSpec((S_, W), lambda g, i: (0, g))],
        out_specs=[pl.BlockSpec((TQ, HP * V_HEAD), lambda g, i: (i, g)),
                   pl.BlockSpec((HP, TQ, 1), lambda g, i: (g, i, 0))],
        out_shape=[jax.ShapeDtypeStruct((S_, N_HEADS * V_HEAD), BF16), jax.ShapeDtypeStruct((N_HEADS, S_, 1), F32)],
        compiler_params=_cparams("parallel", "parallel"), name=name,
    )(q, k, vx)


def q_proj(cq, w_ext, tabs, scale, name, tm=1024):
    S_, R = cq.shape
    tm = _pick(S_, tm, 16)

    def body(c_ref, w_ref, t_c, t_s1, t_s2, o_ref):
        y = lax.dot_general(c_ref[...].astype(BF16), w_ref[...].astype(BF16), _DIMS["nn"], preferred_element_type=F32)
        o_ref[:, 0:LANES] = (y[:, 0:LANES] * scale).astype(o_ref.dtype)
        o_ref[:, LANES:2 * LANES] = (_rope(y[:, LANES:2 * LANES], t_c[...], t_s1[...], t_s2[...]) * scale).astype(o_ref.dtype)

    tab = pl.BlockSpec((tm, LANES), lambda i, h: (i, 0))
    return pl.pallas_call(
        body, grid=(S_ // tm, N_HEADS),
        in_specs=[pl.BlockSpec((tm, R), lambda i, h: (i, 0)), pl.BlockSpec((R, HEAD_PAD), lambda i, h: (0, h)),
                  tab, tab, tab],
        out_specs=pl.BlockSpec((tm, HEAD_PAD), lambda i, h: (i, h)),
        out_shape=jax.ShapeDtypeStruct((S_, N_HEADS * HEAD_PAD), BF16),
        compiler_params=_cparams("parallel", "parallel"), name=name,
    )(cq, w_ext, *tabs)


def o_proj_bwd(dy, w_o, o, name, tm=512):
    S_, Dn = dy.shape
    HV = w_o.shape[0]
    tm = _pick(S_, tm, 16)

    def body(dy_ref, w_ref, o_ref, do_ref, d_ref):
        do = lax.dot_general(dy_ref[...].astype(BF16), w_ref[...].astype(BF16), _DIMS["nt"], preferred_element_type=F32)
        do_ref[...] = do.astype(do_ref.dtype)
        prod = do * o_ref[...].astype(F32)
        for h in range(N_HEADS):
            d_ref[h] = jnp.sum(prod[:, h * V_HEAD:(h + 1) * V_HEAD], axis=-1, keepdims=True)

    return pl.pallas_call(
        body, grid=(S_ // tm,),
        in_specs=[pl.BlockSpec((tm, Dn), lambda i: (i, 0)), pl.BlockSpec((HV, Dn), lambda i: (0, 0)),
                  pl.BlockSpec((tm, HV), lambda i: (i, 0))],
        out_specs=[pl.BlockSpec((tm, HV), lambda i: (i, 0)), pl.BlockSpec((N_HEADS, tm, 1), lambda i: (0, i, 0))],
        out_shape=[jax.ShapeDtypeStruct((S_, HV), BF16), jax.ShapeDtypeStruct((N_HEADS, S_, 1), F32)],
        compiler_params=_cparams("parallel"), name=name,
    )(dy, w_o, o)


def attn_delta(o, do, name, tr=512):
    S_ = o.shape[0]
    tr = _pick(S_, tr, 8)

    def body(o_ref, do_ref, d_ref):
        d_ref[...] = jnp.sum(o_ref[...].astype(F32) * do_ref[...].astype(F32), axis=-1, keepdims=True)

    blk = pl.BlockSpec((tr, V_HEAD), lambda i, h: (i, h))
    return pl.pallas_call(
        body, grid=(S_ // tr, N_HEADS), in_specs=[blk, blk],
        out_specs=pl.BlockSpec((None, tr, 1), lambda i, h: (h, i, 0)),
        out_shape=jax.ShapeDtypeStruct((N_HEADS, S_, 1), F32),
        compiler_params=_cparams("parallel", "parallel"), name=name,
    )(o, do)


def attn_bwd(q, k, vx, do, lse_row, delta_row, name):
    S_ = q.shape[0]
    TK = _pick(S_, ATT_BWD_K_BLOCK, LANES)
    TQ = _pick(S_, ATT_BWD_Q_BLOCK, TK)
    HP = ATT_HEADS_PER_STEP
    W = HP * HEAD_PAD
    ratio = TQ // TK
    nq = S_ // TQ

    def body(q_ref, do_ref, lse_ref, dl_ref, k_ref, v_ref, dq_ref, dk_ref, dv_ref):
        j = pl.program_id(1)

        @pl.when(j == 0)
        def _():
            dq_ref[...] = jnp.zeros_like(dq_ref)

        ks = [k_ref[:, h * HEAD_PAD:(h + 1) * HEAD_PAD] for h in range(HP)]
        vs = [v_ref[:, h * HEAD_PAD:h * HEAD_PAD + V_HEAD] for h in range(HP)]

        def step(i, carry, masked):
            start = pl.multiple_of(i * TQ, TQ)
            out = []
            for h in range(HP):
                dk, dv = carry[h]
                cols = slice(h * HEAD_PAD, (h + 1) * HEAD_PAD)
                qv = q_ref[pl.ds(start, TQ), cols]
                dov = do_ref[pl.ds(start, TQ), h * V_HEAD:(h + 1) * V_HEAD]
                st = lax.dot_general(ks[h], qv, _DIMS["nt"], preferred_element_type=F32)
                pt = jnp.exp(st - lse_ref[h, :, pl.ds(start, TQ)])
                if masked:
                    keyi = j * TK + lax.broadcasted_iota(jnp.int32, (TK, TQ), 0)
                    qryi = i * TQ + lax.broadcasted_iota(jnp.int32, (TK, TQ), 1)
                    pt = jnp.where(keyi <= qryi, pt, 0.0)
                dpt = lax.dot_general(vs[h], dov, _DIMS["nt"], preferred_element_type=F32)
                dst = (pt * (dpt - dl_ref[h, :, pl.ds(start, TQ)])).astype(BF16)
                dv = dv + lax.dot_general(pt.astype(BF16), dov, _DIMS["nn"], preferred_element_type=F32)
                dk = dk + lax.dot_general(dst, qv, _DIMS["nn"], preferred_element_type=F32)
                dq_ref[pl.ds(start, TQ), cols] += lax.dot_general(dst, ks[h], _DIMS["tn"], preferred_element_type=F32)
                out.append((dk, dv))
            return tuple(out)

        init = tuple((jnp.zeros((TK, HEAD_PAD), F32), jnp.zeros((TK, V_HEAD), F32)) for _ in range(HP))
        first = j // ratio
        carry = lax.fori_loop(first + 1, nq, functools.partial(step, masked=False), step(first, init, True))
        for h in range(HP):
            dk_ref[:, h * HEAD_PAD:(h + 1) * HEAD_PAD] = carry[h][0]
            dv_ref[:, h * V_HEAD:(h + 1) * V_HEAD] = carry[h][1]

    return pl.pallas_call(
        body, grid=(N_HEADS // HP, S_ // TK),
        in_specs=[pl.BlockSpec((S_, W), lambda g, j: (0, g)),
                  pl.BlockSpec((S_, HP * V_HEAD), lambda g, j: (0, g)),
                  pl.BlockSpec((HP, 1, S_), lambda g, j: (g, 0, 0)),
                  pl.BlockSpec((HP, 1, S_), lambda g, j: (g, 0, 0)),
                  pl.BlockSpec((TK, W), lambda g, j: (j, g)),
                  pl.BlockSpec((TK, W), lambda g, j: (j, g))],
        out_specs=[pl.BlockSpec((S_, W), lambda g, j: (0, g)),
                   pl.BlockSpec((TK, W), lambda g, j: (j, g)),
                   pl.BlockSpec((TK, HP * V_HEAD), lambda g, j: (j, g))],
        out_shape=[jax.ShapeDtypeStruct((S_, N_HEADS * HEAD_PAD), F32),
                   jax.ShapeDtypeStruct((S_, N_HEADS * HEAD_PAD), F32),
                   jax.ShapeDtypeStruct((S_, N_HEADS * V_HEAD), F32)],
        compiler_params=_cparams("parallel", "arbitrary"), name=name,
    )(q, do, lse_row, delta_row, k, vx)


def mods_fwd(c_all, mod_w, mod_b, name, tn=512):
    L, Dn, E = mod_w.shape
    R = c_all.shape[0]
    tn = _pick(E, tn, LANES)

    def body(c_ref, w_ref, b_ref, o_ref):
        cv = c_ref[...]
        sc = (cv / (1.0 + jnp.exp(-cv))).astype(BF16)
        o_ref[...] = lax.dot_general(sc, w_ref[...].astype(BF16), _DIMS["nn"], preferred_element_type=F32) + b_ref[...]

    return pl.pallas_call(
        body, grid=(L, E // tn),
        in_specs=[pl.BlockSpec((R, Dn), lambda l, j: (0, 0)), pl.BlockSpec((None, Dn, tn), lambda l, j: (l, 0, j)),
                  pl.BlockSpec((None, 1, tn), lambda l, j: (l, 0, j))],
        out_specs=pl.BlockSpec((None, R, tn), lambda l, j: (l, 0, j)),
        out_shape=jax.ShapeDtypeStruct((L, R, E), F32),
        compiler_params=_cparams("parallel", "parallel"), name=name,
    )(c_all, mod_w, mod_b.reshape(L, 1, E))


def _adam_math(w, g, m, v):
    m = ADAM_B1 * m + (1.0 - ADAM_B1) * g
    v = ADAM_B2 * v + (1.0 - ADAM_B2) * (g * g)
    m_hat = m / (1.0 - ADAM_B1 ** ADAM_STEP)
    v_hat = v / (1.0 - ADAM_B2 ** ADAM_STEP)
    delta = -ADAM_LR * (m_hat / (jnp.sqrt(v_hat) + ADAM_EPS) + ADAM_WD * w)
    return delta, m, v


def _as2d(a):
    return a.reshape(-1, a.shape[-1]) if a.ndim != 2 else a


def adamw(w, g, m, v, name):
    shape = w.shape
    w2, g2, m2, v2 = _as2d(w), _as2d(g), _as2d(m), _as2d(v)
    R, C = w2.shape
    tr = _pick(R, max(8, (1 << 18) // C // 8 * 8), 8)

    def body(w_ref, g_ref, m_ref, v_ref, d_ref, mo_ref, vo_ref):
        d, mn, vn = _adam_math(w_ref[...], g_ref[...], m_ref[...], v_ref[...])
        d_ref[...] = d
        mo_ref[...] = mn
        vo_ref[...] = vn

    blk = pl.BlockSpec((tr, C), lambda i: (i, 0))
    shp = jax.ShapeDtypeStruct((R, C), F32)
    outs = pl.pallas_call(
        body, grid=(R // tr,), in_specs=[blk] * 4, out_specs=[blk] * 3, out_shape=[shp] * 3,
        compiler_params=_cparams("parallel"), name=name,
    )(w2, g2, m2, v2)
    return tuple(o.reshape(shape) for o in outs)


def adamw_sum(parts, w, m, v, name):
    P, R, C = parts.shape

    def body(p_ref, w_ref, m_ref, v_ref, g_ref, d_ref, mo_ref, vo_ref):
        g = p_ref[0]
        for k in range(1, P):
            g = g + p_ref[k]
        d, mn, vn = _adam_math(w_ref[...], g, m_ref[...], v_ref[...])
        g_ref[...] = g
        d_ref[...] = d
        mo_ref[...] = mn
        vo_ref[...] = vn

    shp = jax.ShapeDtypeStruct((R, C), F32)
    return pl.pallas_call(body, out_shape=[shp] * 4, compiler_params=_cparams(), name=name)(parts, w, m, v)


def adamw_modw(c_col, dm, w, m, v, name, tr=256, tn=512):
    L, Dn, E = w.shape
    B = c_col.shape[0]
    tr = _pick(Dn, tr, 8)
    tn = _pick(E, tn, LANES)

    def body(c_ref, dm_ref, w_ref, m_ref, v_ref, g_ref, d_ref, mo_ref, vo_ref):
        g = jnp.zeros((tr, tn), F32)
        for b in range(B):
            cv = c_ref[b]
            g = g + (cv / (1.0 + jnp.exp(-cv))) * dm_ref[b:b + 1, :]
        d, mn, vn = _adam_math(w_ref[...], g, m_ref[...], v_ref[...])
        g_ref[...] = g
        d_ref[...] = d
        mo_ref[...] = mn
        vo_ref[...] = vn

    blk = pl.BlockSpec((None, tr, tn), lambda l, i, j: (l, i, j))
    shp = jax.ShapeDtypeStruct((L, Dn, E), F32)
    return pl.pallas_call(
        body, grid=(L, Dn // tr, E // tn),
        in_specs=[pl.BlockSpec((B, tr, 1), lambda l, i, j: (0, i, 0)),
                  pl.BlockSpec((None, B, tn), lambda l, i, j: (l, 0, j)), blk, blk, blk],
        out_specs=[blk] * 4, out_shape=[shp] * 4,
        compiler_params=_cparams("parallel", "parallel", "parallel"), name=name,
    )(c_col, dm, w, m, v)


def add_round(a, b, name, tr=512):
    R, C = a.shape
    tr = _pick(R, tr, 16)

    def body(a_ref, b_ref, o_ref):
        o_ref[...] = (a_ref[...] + b_ref[...].astype(F32)).astype(BF16)

    blk = pl.BlockSpec((tr, C), lambda i: (i, 0))
    return pl.pallas_call(
        body, grid=(R // tr,), in_specs=[blk, blk], out_specs=blk, out_shape=jax.ShapeDtypeStruct((R, C), BF16),
        compiler_params=_cparams("parallel"), name=name,
    )(a, b)


def sum_parts(parts, name, tr=512):
    P, R, C = parts.shape
    tr = _pick(R, tr, 16)

    def body(p_ref, o_ref):
        s = p_ref[0].astype(F32)
        for k in range(1, P):
            s = s + p_ref[k].astype(F32)
        o_ref[...] = s

    return pl.pallas_call(
        body, grid=(R // tr,), in_specs=[pl.BlockSpec((P, tr, C), lambda i: (0, i, 0))],
        out_specs=pl.BlockSpec((tr, C), lambda i: (i, 0)), out_shape=jax.ShapeDtypeStruct((R, C), F32),
        compiler_params=_cparams("parallel"), name=name,
    )(parts)


_ANY = pl.BlockSpec(memory_space=pl.ANY)


def _place():
    return lax.axis_index("x"), lax.axis_index("y"), lax.axis_index("c")


def _flip(v, bit):
    return 1 - v if bit else v


def chip_gather(buf, name):
    def body(in_ref, out_ref, send_sems, recv_sems):
        x, y, c = _place()
        me = 2 * x + y
        sends = []
        for k in range(1, N_CHIPS):
            px, py = _flip(x, k >> 1), _flip(y, k & 1)
            cp = pltpu.make_async_remote_copy(src_ref=in_ref, dst_ref=out_ref.at[me], send_sem=send_sems.at[k - 1],
                                              recv_sem=recv_sems.at[k - 1], device_id=(px, py, c), device_id_type=MESH)
            cp.start()
            sends.append(cp)
        for k in range(1, N_CHIPS):
            px, py = _flip(x, k >> 1), _flip(y, k & 1)
            pltpu.make_async_remote_copy(src_ref=in_ref, dst_ref=out_ref.at[2 * px + py], send_sem=send_sems.at[k - 1],
                                         recv_sem=recv_sems.at[k - 1], device_id=(px, py, c),
                                         device_id_type=MESH).wait_recv()
        for cp in sends:
            cp.wait_send()

    out = pl.pallas_call(
        body, in_specs=[_ANY], out_specs=_ANY,
        out_shape=jax.ShapeDtypeStruct((N_CHIPS,) + buf.shape, buf.dtype),
        scratch_shapes=[pltpu.SemaphoreType.DMA((N_CHIPS - 1,)), pltpu.SemaphoreType.DMA((N_CHIPS - 1,))],
        name=name,
    )(buf)
    return lax.dynamic_update_index_in_dim(out, buf, 2 * lax.axis_index("x") + lax.axis_index("y"), 0)


def chip_all_to_all(buf, name):
    def body(in_ref, out_ref, send_sems, recv_sems):
        x, y, c = _place()
        me = 2 * x + y
        sends = []
        for k in range(1, N_CHIPS):
            px, py = _flip(x, k >> 1), _flip(y, k & 1)
            cp = pltpu.make_async_remote_copy(src_ref=in_ref.at[2 * px + py], dst_ref=out_ref.at[me],
                                              send_sem=send_sems.at[k - 1], recv_sem=recv_sems.at[k - 1],
                                              device_id=(px, py, c), device_id_type=MESH)
            cp.start()
            sends.append(cp)
        for k in range(1, N_CHIPS):
            px, py = _flip(x, k >> 1), _flip(y, k & 1)
            pltpu.make_async_remote_copy(src_ref=in_ref.at[me], dst_ref=out_ref.at[2 * px + py],
                                         send_sem=send_sems.at[k - 1], recv_sem=recv_sems.at[k - 1],
                                         device_id=(px, py, c), device_id_type=MESH).wait_recv()
        for cp in sends:
            cp.wait_send()

    out = pl.pallas_call(
        body, in_specs=[_ANY], out_specs=_ANY, out_shape=jax.ShapeDtypeStruct(buf.shape, buf.dtype),
        scratch_shapes=[pltpu.SemaphoreType.DMA((N_CHIPS - 1,)), pltpu.SemaphoreType.DMA((N_CHIPS - 1,))],
        name=name,
    )(buf)
    me = 2 * lax.axis_index("x") + lax.axis_index("y")
    return lax.dynamic_update_index_in_dim(out, _index(buf, me), me, 0)


def core_gather(buf, name):
    def body(in_ref, out_ref, send_sem, recv_sem):
        x, y, c = _place()
        cp = pltpu.make_async_remote_copy(src_ref=in_ref, dst_ref=out_ref.at[c], send_sem=send_sem, recv_sem=recv_sem,
                                          device_id=(x, y, 1 - c), device_id_type=MESH)
        cp.start()
        pltpu.make_async_remote_copy(src_ref=in_ref, dst_ref=out_ref.at[1 - c], send_sem=send_sem, recv_sem=recv_sem,
                                     device_id=(x, y, 1 - c), device_id_type=MESH).wait_recv()
        cp.wait_send()

    out = pl.pallas_call(
        body, in_specs=[_ANY], out_specs=_ANY, out_shape=jax.ShapeDtypeStruct((2,) + buf.shape, buf.dtype),
        scratch_shapes=[pltpu.SemaphoreType.DMA, pltpu.SemaphoreType.DMA],
        name=name,
    )(buf)
    return lax.dynamic_update_index_in_dim(out, buf, lax.axis_index("c"), 0)


def core_swap(buf, name):
    def body(in_ref, out_ref, send_sem, recv_sem):
        x, y, c = _place()
        cp = pltpu.make_async_remote_copy(src_ref=in_ref, dst_ref=out_ref, send_sem=send_sem, recv_sem=recv_sem,
                                          device_id=(x, y, 1 - c), device_id_type=MESH)
        cp.start()
        cp.wait()

    return pl.pallas_call(
        body, in_specs=[_ANY], out_specs=_ANY, out_shape=jax.ShapeDtypeStruct(buf.shape, buf.dtype),
        scratch_shapes=[pltpu.SemaphoreType.DMA, pltpu.SemaphoreType.DMA],
        name=name,
    )(buf)


def device_gather(buf, name):
    def body(in_ref, out_ref, send_sems, recv_sems, local_sem):
        x, y, c = _place()
        me = 4 * x + 2 * y + c
        mine = pltpu.make_async_copy(in_ref, out_ref.at[me], local_sem)
        mine.start()
        sends = []
        for k in range(1, N_DEV):
            peer = (_flip(x, (k >> 2) & 1), _flip(y, (k >> 1) & 1), _flip(c, k & 1))
            cp = pltpu.make_async_remote_copy(src_ref=in_ref, dst_ref=out_ref.at[me], send_sem=send_sems.at[k - 1],
                                              recv_sem=recv_sems.at[k - 1], device_id=peer, device_id_type=MESH)
            cp.start()
            sends.append(cp)
        for k in range(1, N_DEV):
            peer = (_flip(x, (k >> 2) & 1), _flip(y, (k >> 1) & 1), _flip(c, k & 1))
            pltpu.make_async_remote_copy(src_ref=in_ref, dst_ref=out_ref.at[4 * peer[0] + 2 * peer[1] + peer[2]],
                                         send_sem=send_sems.at[k - 1], recv_sem=recv_sems.at[k - 1], device_id=peer,
                                         device_id_type=MESH).wait_recv()
        for cp in sends:
            cp.wait_send()
        mine.wait()

    return pl.pallas_call(
        body, in_specs=[_ANY], out_specs=_ANY, out_shape=jax.ShapeDtypeStruct((N_DEV,) + buf.shape, buf.dtype),
        scratch_shapes=[pltpu.SemaphoreType.DMA((N_DEV - 1,)), pltpu.SemaphoreType.DMA((N_DEV - 1,)),
                        pltpu.SemaphoreType.DMA],
        name=name,
    )(buf)


def _region(ref, chip_axis=None, chip=None, chip_size=None, half_axis=None, half=None, half_size=None):
    idx = [slice(None)] * len(ref.shape)
    if chip is not None:
        idx[chip_axis] = pl.ds(chip * chip_size, chip_size)
    if half is not None:
        idx[half_axis] = pl.ds(half * half_size, half_size)
    return ref.at[tuple(idx)]


def gather_weights(shards, axes, name, after=()):
    n = len(shards)

    def full_shape(t):
        shp = list(shards[t].shape)
        shp[axes[t][0]] *= N_CHIPS
        return tuple(shp)

    def body(*refs):
        ins, outs = refs[:n], refs[n + len(after):2 * n + len(after)]
        ici_send, ici_recv, d2d_send, d2d_recv, own_send, own_recv = refs[2 * n + len(after):]
        x, y, c = _place()
        me = 2 * x + y

        def part(t, ref, chip, half):
            ca, ha = axes[t]
            return _region(ref, ca, chip, ins[t].shape[ca], ha, half, ins[t].shape[ha] // 2)

        def own(t):
            return pltpu.make_async_remote_copy(src_ref=ins[t], dst_ref=part(t, outs[t], me, None),
                                                send_sem=own_send.at[t], recv_sem=own_recv.at[t],
                                                device_id=(x, y, 1 - c), device_id_type=MESH)

        started = []
        for t in range(n):
            own(t).start()
            started.append(own(t))
        for t in range(n):
            for k in range(1, N_CHIPS):
                px, py = _flip(x, k >> 1), _flip(y, k & 1)
                cp = pltpu.make_async_remote_copy(src_ref=part(t, ins[t], None, c), dst_ref=part(t, outs[t], me, c),
                                                  send_sem=ici_send.at[t, k - 1], recv_sem=ici_recv.at[t, k - 1],
                                                  device_id=(px, py, c), device_id_type=MESH)
                cp.start()
                started.append(cp)
        for t in range(n):
            for k in range(1, N_CHIPS):
                px, py = _flip(x, k >> 1), _flip(y, k & 1)
                got = part(t, outs[t], 2 * px + py, c)
                pltpu.make_async_remote_copy(src_ref=part(t, ins[t], None, c), dst_ref=got,
                                             send_sem=ici_send.at[t, k - 1], recv_sem=ici_recv.at[t, k - 1],
                                             device_id=(px, py, c), device_id_type=MESH).wait_recv()
                fw = pltpu.make_async_remote_copy(src_ref=got, dst_ref=got, send_sem=d2d_send.at[t, k - 1],
                                                  recv_sem=d2d_recv.at[t, k - 1], device_id=(x, y, 1 - c),
                                                  device_id_type=MESH)
                fw.start()
                started.append(fw)
        for t in range(n):
            for k in range(1, N_CHIPS):
                px, py = _flip(x, k >> 1), _flip(y, k & 1)
                theirs = part(t, outs[t], 2 * px + py, 1 - c)
                pltpu.make_async_remote_copy(src_ref=theirs, dst_ref=theirs, send_sem=d2d_send.at[t, k - 1],
                                             recv_sem=d2d_recv.at[t, k - 1], device_id=(x, y, 1 - c),
                                             device_id_type=MESH).wait_recv()
        for t in range(n):
            own(t).wait_recv()
        for cp in started:
            cp.wait_send()

    sem = pltpu.SemaphoreType.DMA((n, N_CHIPS - 1))
    own_sem = pltpu.SemaphoreType.DMA((n,))
    return pl.pallas_call(
        body, in_specs=[_ANY] * (n + len(after)), out_specs=[_ANY] * n,
        out_shape=[jax.ShapeDtypeStruct(full_shape(t), shards[t].dtype) for t in range(n)],
        scratch_shapes=[sem, sem, sem, sem, own_sem, own_sem], name=name,
    )(*shards, *after)


_HBM = pl.BlockSpec(memory_space=pltpu.HBM)
_SEM = pl.BlockSpec(memory_space=pltpu.SEMAPHORE)
_EFFECT = pltpu.SideEffectType.DATAFLOW_SIDE_EFFECTING
WEIGHT_COPIES = N_CHIPS


def _weight_peer(k, x, y, c):
    return (x, y, 1 - c) if k == 0 else (_flip(x, k >> 1), _flip(y, k & 1), c)


def weights_start(shards, items, name, after=()):
    n_sh, n_it = len(shards), len(items)

    def src_of(refs, i):
        t, layer, _ = items[i]
        return refs[t] if layer is None else refs[t].at[layer]

    def land_shape(i):
        t, layer, ca = items[i]
        shp = list(shards[t].shape if layer is None else shards[t].shape[1:])
        shp[ca] *= N_CHIPS
        return tuple(shp)

    def body(*refs):
        shard_refs, land_refs = refs[:n_sh], refs[n_sh:n_sh + n_it]
        first_out = n_sh + n_it + len(after)
        send_sems = refs[first_out:first_out + n_it]
        recv_sems = refs[first_out + n_it:first_out + 2 * n_it]
        token = refs[-1]
        x, y, c = _place()
        me = 2 * x + y
        for i in range(n_it):
            src = src_of(shard_refs, i)
            ca = items[i][2]
            dst = _region(land_refs[i], ca, me, src.shape[ca])
            for k in range(WEIGHT_COPIES):
                pltpu.make_async_remote_copy(src_ref=src, dst_ref=dst, send_sem=send_sems[i], recv_sem=recv_sems[i],
                                             device_id=_weight_peer(k, x, y, c), device_id_type=MESH).start()
        token[...] = jnp.zeros_like(token)

    lands = [pltpu.with_memory_space_constraint(lax.empty(land_shape(i), shards[0].dtype), pltpu.HBM)
             for i in range(n_it)]
    ins = [pltpu.with_memory_space_constraint(a, pltpu.HBM) for a in shards] + lands
    sems = (pltpu.SemaphoreType.DMA(()),) * (2 * n_it)
    outs = pl.pallas_call(
        body, name=name,
        out_shape=sems + tuple(pltpu.HBM(a.shape, a.dtype) for a in ins) + (jax.ShapeDtypeStruct((8, LANES), F32),),
        in_specs=[_HBM] * len(ins) + [_ANY] * len(after),
        out_specs=(_SEM,) * (2 * n_it) + (_HBM,) * len(ins) + (pl.BlockSpec(memory_space=pltpu.VMEM),),
        input_output_aliases={i: 2 * n_it + i for i in range(len(ins))},
        compiler_params=pltpu.CompilerParams(has_side_effects=_EFFECT),
    )(*ins, *after)
    base = 2 * n_it
    return (list(outs[:n_it]), list(outs[n_it:base]), list(outs[base:base + n_sh]),
            list(outs[base + n_sh:base + n_sh + n_it]), outs[-1])


def weights_wait(send_sems, recv_sems, lands, after, keep, name):
    m = len(lands)

    def body(*refs):
        land_refs, send_refs, recv_refs = refs[:m], refs[m:2 * m], refs[2 * m:3 * m]
        x, y, c = _place()
        for j in range(m):
            cp = pltpu.make_async_remote_copy(src_ref=land_refs[j], dst_ref=land_refs[j], send_sem=send_refs[j],
                                              recv_sem=recv_refs[j], device_id=(x, y, 1 - c),
                                              device_id_type=MESH)
            cp.wait_send()
            cp.wait_recv()

    outs = pl.pallas_call(
        body, name=name,
        out_shape=tuple(pltpu.HBM(a.shape, a.dtype) for a in lands),
        in_specs=[_HBM] * m + [_SEM] * (2 * m) + [_ANY] + [_HBM] * len(keep),
        out_specs=(_HBM,) * m,
        input_output_aliases={j: j for j in range(m)},
        compiler_params=pltpu.CompilerParams(has_side_effects=_EFFECT),
    )(*lands, *send_sems, *recv_sems, after, *keep)
    return list(outs)


def reduce_to_sibling(lo, hi, name):
    n = len(lo)

    def body(*refs):
        los, his, outs = refs[:n], refs[n:2 * n], refs[2 * n:3 * n]
        send_sems, recv_sems = refs[3 * n:]
        x, y, c = _place()

        def copy(u, src):
            return pltpu.make_async_remote_copy(src_ref=src, dst_ref=outs[u], send_sem=send_sems.at[u],
                                                recv_sem=recv_sems.at[u], device_id=(x, y, 1 - c), device_id_type=MESH)

        for u in range(n):
            @pl.when(c == 0)
            def _(u=u):
                copy(u, his[u]).start()

            @pl.when(c == 1)
            def _(u=u):
                copy(u, los[u]).start()
        for u in range(n):
            copy(u, los[u]).wait_recv()
        for u in range(n):
            copy(u, los[u]).wait_send()

    return pl.pallas_call(
        body, in_specs=[_ANY] * (2 * n), out_specs=[_ANY] * n,
        out_shape=[jax.ShapeDtypeStruct(a.shape, a.dtype) for a in lo],
        scratch_shapes=[pltpu.SemaphoreType.DMA((n,)), pltpu.SemaphoreType.DMA((n,))], name=name,
    )(*lo, *hi)


def add_selected(lo, hi, other, name, tile_elems=1 << 19):
    R, C = lo.shape
    tr = _pick(R, max(16, tile_elems // C // 16 * 16), 16)

    def body(lo_ref, hi_ref, o_ref, out_ref):
        mine = jnp.where(lax.axis_index("c") == 0, lo_ref[...].astype(F32), hi_ref[...].astype(F32))
        out_ref[...] = (mine + o_ref[...].astype(F32)).astype(out_ref.dtype)

    blk = pl.BlockSpec((tr, C), lambda i: (i, 0))
    return pl.pallas_call(
        body, grid=(R // tr,), in_specs=[blk, blk, blk], out_specs=blk, out_shape=jax.ShapeDtypeStruct((R, C), BF16),
        compiler_params=_cparams("parallel"), name=name,
    )(lo, hi, other)


def scatter_to_chips(pieces, chip_axes, name):
    n = len(pieces)

    def block_shape(u):
        shp = list(pieces[u].shape)
        shp[chip_axes[u]] //= N_CHIPS
        return tuple(shp)

    def body(*refs):
        ins, outs = refs[:n], refs[n:2 * n]
        send_sems, recv_sems = refs[2 * n:]
        x, y, c = _place()
        me = 2 * x + y
        started = []
        for u in range(n):
            size = block_shape(u)[chip_axes[u]]
            for k in range(1, N_CHIPS):
                px, py = _flip(x, k >> 1), _flip(y, k & 1)
                cp = pltpu.make_async_remote_copy(src_ref=_region(ins[u], chip_axes[u], 2 * px + py, size),
                                                  dst_ref=outs[u].at[me], send_sem=send_sems.at[u, k - 1],
                                                  recv_sem=recv_sems.at[u, k - 1], device_id=(px, py, c),
                                                  device_id_type=MESH)
                cp.start()
                started.append(cp)
        for u in range(n):
            size = block_shape(u)[chip_axes[u]]
            for k in range(1, N_CHIPS):
                px, py = _flip(x, k >> 1), _flip(y, k & 1)
                pltpu.make_async_remote_copy(src_ref=_region(ins[u], chip_axes[u], me, size),
                                             dst_ref=outs[u].at[2 * px + py], send_sem=send_sems.at[u, k - 1],
                                             recv_sem=recv_sems.at[u, k - 1], device_id=(px, py, c),
                                             device_id_type=MESH).wait_recv()
        for cp in started:
            cp.wait_send()

    sem = pltpu.SemaphoreType.DMA((n, N_CHIPS - 1))
    return pl.pallas_call(
        body, in_specs=[_ANY] * n, out_specs=[_ANY] * n,
        out_shape=[jax.ShapeDtypeStruct((N_CHIPS,) + block_shape(u), pieces[u].dtype) for u in range(n)],
        scratch_shapes=[sem, sem], name=name,
    )(*pieces)


def scatter_start(pieces, chip_axes, name, after=()):
    n = len(pieces)

    def block_shape(u):
        shp = list(pieces[u].shape)
        shp[chip_axes[u]] //= N_CHIPS
        return tuple(shp)

    def body(*refs):
        ins, land_refs = refs[:n], refs[n:2 * n]
        first_out = 2 * n + len(after)
        send_sems, recv_sems = refs[first_out:first_out + n], refs[first_out + n:first_out + 2 * n]
        token = refs[-1]
        x, y, c = _place()
        me = 2 * x + y
        for u in range(n):
            size = block_shape(u)[chip_axes[u]]
            for k in range(1, N_CHIPS):
                px, py = _flip(x, k >> 1), _flip(y, k & 1)
                pltpu.make_async_remote_copy(src_ref=_region(ins[u], chip_axes[u], 2 * px + py, size),
                                             dst_ref=land_refs[u].at[me], send_sem=send_sems[u], recv_sem=recv_sems[u],
                                             device_id=(px, py, c), device_id_type=MESH).start()
        token[...] = jnp.zeros_like(token)

    lands = [pltpu.with_memory_space_constraint(lax.empty((N_CHIPS,) + block_shape(u), pieces[u].dtype), pltpu.HBM)
             for u in range(n)]
    ins = [pltpu.with_memory_space_constraint(a, pltpu.HBM) for a in pieces] + lands
    sems = (pltpu.SemaphoreType.DMA(()),) * (2 * n)
    outs = pl.pallas_call(
        body, name=name,
        out_shape=sems + tuple(pltpu.HBM(a.shape, a.dtype) for a in ins) + (jax.ShapeDtypeStruct((8, LANES), F32),),
        in_specs=[_HBM] * len(ins) + [_ANY] * len(after),
        out_specs=(_SEM,) * (2 * n) + (_HBM,) * len(ins) + (pl.BlockSpec(memory_space=pltpu.VMEM),),
        input_output_aliases={i: 2 * n + i for i in range(len(ins))},
        compiler_params=pltpu.CompilerParams(has_side_effects=_EFFECT),
    )(*ins, *after)
    return list(outs[:n]), list(outs[n:2 * n]), list(outs[2 * n:3 * n]), list(outs[3 * n:4 * n]), outs[-1]


def scatter_wait(send_sems, recv_sems, lands, pieces, after, name):
    n = len(lands)

    def body(*refs):
        land_refs, send_refs, recv_refs = refs[:n], refs[n:2 * n], refs[2 * n:3 * n]
        x, y, c = _place()
        for u in range(n):
            three = land_refs[u].at[pl.ds(0, N_CHIPS - 1)]
            cp = pltpu.make_async_remote_copy(src_ref=three, dst_ref=three, send_sem=send_refs[u], recv_sem=recv_refs[u],
                                              device_id=(x, y, 1 - c), device_id_type=MESH)
            cp.wait_send()
            cp.wait_recv()

    outs = pl.pallas_call(
        body, name=name,
        out_shape=tuple(pltpu.HBM(a.shape, a.dtype) for a in lands),
        in_specs=[_HBM] * n + [_SEM] * (2 * n) + [_ANY] + [_HBM] * len(pieces),
        out_specs=(_HBM,) * n,
        input_output_aliases={j: j for j in range(n)},
        compiler_params=pltpu.CompilerParams(has_side_effects=_EFFECT),
    )(*lands, *send_sems, *recv_sems, after, *pieces)
    return list(outs)


def gather_halves(parts, slots, out_shapes, name):
    n = len(parts)

    def body(*refs):
        ins, outs = refs[:n], refs[n:n + len(out_shapes)]
        send_sems, recv_sems = refs[n + len(out_shapes):]
        x, y, c = _place()
        started = []
        for u in range(n):
            t, s = slots[u]
            cp = pltpu.make_async_remote_copy(src_ref=ins[u], dst_ref=outs[t].at[s, c], send_sem=send_sems.at[u],
                                              recv_sem=recv_sems.at[u], device_id=(x, y, 1 - c), device_id_type=MESH)
            cp.start()
            started.append(cp)
        for u in range(n):
            t, s = slots[u]
            pltpu.make_async_remote_copy(src_ref=ins[u], dst_ref=outs[t].at[s, 1 - c], send_sem=send_sems.at[u],
                                         recv_sem=recv_sems.at[u], device_id=(x, y, 1 - c),
                                         device_id_type=MESH).wait_recv()
        for cp in started:
            cp.wait_send()

    return pl.pallas_call(
        body, in_specs=[_ANY] * n, out_specs=[_ANY] * len(out_shapes),
        out_shape=[jax.ShapeDtypeStruct(shp, F32) for shp in out_shapes],
        scratch_shapes=[pltpu.SemaphoreType.DMA((n,)), pltpu.SemaphoreType.DMA((n,))], name=name,
    )(*parts)


WEIGHT_ORDER = ["mod_w", "mod_b", "norm1_g", "norm2_g", "pool_w", "pool_b", "pool_scale", "kv_in_g", "w_dkv",
                "ckv_norm_g", "w_uk", "w_uv", "w_dq", "q_norm_g", "w_uq", "w_o", "w_up", "conv_w", "conv_b", "w_down",
                "final_g"]
EXCHANGED = {"w_up": (2, 0), "w_down": (1, 0), "w_o": (1, 0), "w_uq": (2, 0), "w_dq": (1, 0), "pool_w": (2, 0),
             "w_dkv": (0, 1), "w_uk": (1, 0), "w_uv": (1, 0)}
SMALL_SHARDED = {"conv_w": 2, "pool_b": 1, "pool_scale": 1}
REPLICATED = ["mod_b", "norm1_g", "norm2_g", "kv_in_g", "ckv_norm_g", "q_norm_g", "conv_b", "final_g"]


def _padded(n, align):
    return -(-n // align) * align


def _flat_pad(parts, total):
    flat = jnp.concatenate(parts, axis=-1)
    pad = total - flat.shape[-1]
    if pad:
        flat = jnp.concatenate([flat, jnp.zeros(flat.shape[:-1] + (pad,), flat.dtype)], axis=-1)
    return flat


def _split_shards(full, axis):
    shp = full.shape
    t = full.reshape(shp[:axis] + (N_CHIPS, shp[axis] // N_CHIPS) + shp[axis + 1:])
    return jnp.moveaxis(t, axis, 0).reshape(N_CHIPS, -1)


def _join_shards(rows, shard_shape, axis):
    t = jnp.moveaxis(rows.reshape((N_CHIPS,) + tuple(shard_shape)), 0, axis)
    return t.reshape(tuple(shard_shape[:axis]) + (N_CHIPS * shard_shape[axis],) + tuple(shard_shape[axis + 1:]))


def _index(a, i, axis=0):
    return lax.dynamic_index_in_dim(a, i, axis, keepdims=False)


def kernel(x, c, positions, mod_w, mod_b, norm1_g, norm2_g, pool_w, pool_b, pool_scale, kv_in_g, w_dkv, ckv_norm_g, w_uk, w_uv, w_dq, q_norm_g, w_uq, w_o, w_up, conv_w, conv_b, w_down, final_g, loss_target, m_mod_w, m_mod_b, m_norm1_g, m_norm2_g, m_pool_w, m_pool_b, m_pool_scale, m_kv_in_g, m_w_dkv, m_ckv_norm_g, m_w_uk, m_w_uv, m_w_dq, m_q_norm_g, m_w_uq, m_w_o, m_w_up, m_conv_w, m_conv_b, m_w_down, m_final_g, v_mod_w, v_mod_b, v_norm1_g, v_norm2_g, v_pool_w, v_pool_b, v_pool_scale, v_kv_in_g, v_w_dkv, v_ckv_norm_g, v_w_uk, v_w_uv, v_w_dq, v_q_norm_g, v_w_uq, v_w_o, v_w_up, v_conv_w, v_conv_b, v_w_down, v_final_g):
    given = dict(locals())
    W = {n: given[n] for n in WEIGHT_ORDER}
    M1 = {n: given["m_" + n] for n in WEIGHT_ORDER}
    V2 = {n: given["v_" + n] for n in WEIGHT_ORDER}
    xi, yi, ci = lax.axis_index("x"), lax.axis_index("y"), lax.axis_index("c")
    chip = 2 * xi + yi
    dev = 4 * xi + 2 * yi + ci
    x0 = x[0]
    S_, D = x0.shape
    Fh = conv_b.shape[1]
    E = mod_b.shape[1]
    Es = E // N_CHIPS
    zD = jnp.zeros((D,), F32)

    c_all = device_gather(c, "gather_c").reshape(N_DEV, D)
    c_pad = jnp.concatenate([c_all, jnp.zeros((16 - N_DEV, D), F32)], axis=0)
    mod_b_mine = lax.dynamic_slice_in_dim(mod_b, chip * Es, Es, axis=1)
    mods_part = mods_fwd(c_pad, mod_w, mod_b_mine, "mods_fwd")
    mods_all = chip_gather(mods_part, "gather_mods")
    mods = jnp.swapaxes(_index(mods_all, dev, axis=2), 0, 1).reshape(DEPTH, E)
    mod = [[mods[l, k * D:(k + 1) * D] for k in range(6)] for l in range(DEPTH)]

    full = {}
    ssz = {n: math.prod(W[n].shape) for n in SMALL_SHARDED}
    Tw = _padded(sum(ssz.values()), 8 * PACK_COLS)
    small_rows = chip_gather(_flat_pad([W[n].reshape(-1) for n in SMALL_SHARDED], Tw).reshape(-1, PACK_COLS),
                             "gather_small_w").reshape(N_CHIPS, Tw)
    off = 0
    for n, axis in SMALL_SHARDED.items():
        full[n] = _join_shards(small_rows[:, off:off + ssz[n]], W[n].shape, axis)
        off += ssz[n]

    names = list(EXCHANGED)
    shards = [W[n].astype(BF16) for n in names]
    n_mla = DEPTH - N_A
    first_axes = {"w_up": (1, 0), "w_down": (0, 1), "pool_w": (1, 0)}
    first = gather_weights([shards[names.index(n)][0] for n in first_axes], list(first_axes.values()), "gather_weights0",
                           after=[mods, small_rows])
    for n, arr in zip(first_axes, first):
        full[(n, 0)] = arr
    items, groups = [], []

    def group(entries):
        groups.append(list(range(len(items), len(items) + len(entries))))
        for n, layer in entries:
            ca = EXCHANGED[n][0] - (0 if layer is None else 1)
            items.append((names.index(n), layer, 0 if n == "w_dkv" else ca))

    for l in range(1, N_A):
        group([("w_up", l), ("w_down", l), ("pool_w", l)])
    for j in range(n_mla):
        head = [("w_dkv", None), ("w_uk", None), ("w_uv", None)] if j == 0 else []
        group(head + [("w_dq", j), ("w_uq", j), ("w_o", j), ("w_up", N_A + j), ("w_down", N_A + j)])
    w_send, w_recv, shards_thru, lands, _ = weights_start(shards, items, "weights_start", after=first)

    def weights_ready(g, after):
        keep = shards_thru if g == len(groups) - 1 else []
        got = weights_wait([w_send[i] for i in groups[g]], [w_recv[i] for i in groups[g]], [lands[i] for i in groups[g]],
                           after, keep, f"weights_wait{g}")
        for i, arr in zip(groups[g], got):
            t, layer, _ = items[i]
            full[(names[t], 0 if layer is None else layer)] = arr

    q_rank = W["w_uq"].shape[1]
    kv_w = KV_RANK + QK_ROPE

    def uq_ext(j):
        wq = full[("w_uq", j)].reshape(q_rank, N_HEADS, QK_HEAD)
        return jnp.concatenate([wq, jnp.zeros((q_rank, N_HEADS, HEAD_PAD - QK_HEAD), BF16)],
                               axis=2).reshape(q_rank, N_HEADS * HEAD_PAD)


    half = QK_ROPE // 2
    inv = 1.0 / (ROPE_THETA ** (jnp.arange(0, QK_ROPE, 2, dtype=F32) / QK_ROPE))
    inv_row = jnp.concatenate([inv, inv, jnp.zeros((LANES - 2 * half,), F32)]).reshape(1, LANES)
    tabs = rope_tables(positions[0].astype(F32).reshape(S_, 1), inv_row, "rope_tables")
    att_scale = QK_HEAD ** -0.5

    saved = []
    xcur = x0
    kv_saved = None
    K = VX = knv = None
    for l in range(DEPTH):
        sh1, sc1, g1, sh2, sc2, g2 = mod[l]
        st = {"xin": xcur}
        if l:
            weights_ready(l - 1, xcur)
        if l == N_A:
            w_dkv_ext = jnp.concatenate([full[("w_dkv", 0)], jnp.zeros((D, KV_RANK + LANES - kv_w), BF16)], axis=1)
            w_ukv = jnp.concatenate([full[("w_uk", 0)], full[("w_uv", 0)]], axis=1)
            xn = norm_fwd(xcur, kv_in_g, zD, zD, BF16, "kvin_fwd")
            kv_ext = mm(xn, w_dkv_ext, "nn", F32, "dkv_mm")
            lat = kv_ext[:, :KV_RANK]
            zk = jnp.zeros((KV_RANK,), F32)
            ckv = norm_fwd(lat, ckv_norm_g, zk, zk, BF16, "ckv_fwd")
            knv = mm(ckv, w_ukv, "nn", BF16, "ukv_mm")
            K, VX = k_prep(knv, kv_ext, tabs, "k_prep")
            kv_saved = {"x": xcur, "xn": xn, "lat": lat, "ckv": ckv}
        if l < N_A:
            h1 = norm_fwd(xcur, norm1_g[l], sc1, sh1, F32, f"norm1_fwd{l}")
            st["pooled"] = _pool_call(h1, BF16, f"pool_fwd{l}", False)
            st["cs"] = g1 * full["pool_scale"][l]
            st["ypre"], xmid = gmm(st["pooled"], full[("pool_w", l)], "nn", BF16, f"pool_mm{l}", bias=full["pool_b"][l],
                                   res=xcur, colscale=st["cs"])
        else:
            j = l - N_A
            st["h1"] = norm_fwd(xcur, norm1_g[l], sc1, sh1, BF16, f"norm1_fwd{l}")
            st["ql"] = mm(st["h1"], full[("w_dq", j)], "nn", F32, f"dq_mm{l}")
            st["cq"] = norm_fwd(st["ql"], q_norm_g[j], jnp.zeros_like(q_norm_g[j]), jnp.zeros_like(q_norm_g[j]), BF16,
                                f"qnorm_fwd{l}")
            st["w_uq_ext"] = uq_ext(j)
            st["Q"] = q_proj(st["cq"], st["w_uq_ext"], tabs, att_scale, f"uq_mm{l}")
            st["o"], lse = attn_fwd(st["Q"], K, VX, f"attn_fwd{l}")
            st["lse"] = lse.reshape(N_HEADS, 1, S_)
            st["y"], xmid = mm(st["o"], full[("w_o", j)], "nn", BF16, f"wo_mm{l}", res=xcur, colscale=g1)
        st["xmid"] = xmid
        st["h2"] = norm_fwd(xmid, norm2_g[l], sc2, sh2, BF16, f"norm2_fwd{l}")
        st["u"] = mm(st["h2"], full[("w_up", l)], "nn", BF16, f"up_mm{l}")
        st["z"] = glu_fwd(st["u"], full["conv_w"][l], conv_b[l], f"glu_fwd{l}")
        st["f"], xcur = mm(st["z"], full[("w_down", l)], "nn", BF16, f"down_mm{l}", tk=1408, res=xmid, colscale=g2)
        saved.append(st)

    dx, d_final_g, loss_part = loss_head(xcur, final_g, loss_target[0], "loss_head")
    loss = lax.psum(loss_part[0, 0], ("x", "y", "c"))

    def begin_reduce(tensors, first_slot, tag):
        units = []
        for n in tensors:
            ca = EXCHANGED[n][0]
            if W[n].ndim > 2:
                n_slots = W[n].shape[0] // 2
                for sl in range(first_slot if n_slots > 1 else 0, first_slot + 1 if n_slots > 1 else 1):
                    units.append((n, sl, G[(n, 2 * sl)], G[(n, 2 * sl + 1)], ca - 1))
            elif n == "w_dkv":
                g4 = G[(n, 0)].reshape(N_CHIPS, 2, -1, kv_w)
                units.append((n, 0, g4[:, 0], g4[:, 1], 0))
            else:
                rows_half = W[n].shape[0] // 2
                units.append((n, 0, G[(n, 0)][:rows_half], G[(n, 0)][rows_half:], ca))
        lo = [u[2] for u in units]
        hi = [u[3] for u in units]
        theirs = reduce_to_sibling(lo, hi, f"reduce_cores_{tag}")
        sums = [add_selected(l_.reshape(-1, l_.shape[-1]), h_.reshape(-1, l_.shape[-1]), t_.reshape(-1, l_.shape[-1]),
                             f"reduce_cores_add_{tag}{i}").reshape(l_.shape)
                for i, (l_, h_, t_) in enumerate(zip(lo, hi, theirs))]
        return units, sums, [u[4] for u in units]

    G = {}
    dmods = [None] * DEPTH
    d_norm1 = [None] * DEPTH
    d_norm2 = [None] * DEPTH
    d_conv_b = [None] * DEPTH
    d_qnorm = [None] * n_mla
    dkv_acc = []
    df, a2, _ = gate_bwd(dx, saved[DEPTH - 1]["f"], mod[DEPTH - 1][5], f"gate2_bwd{DEPTH - 1}")
    for l in reversed(range(DEPTH)):
        sh1, sc1, g1, sh2, sc2, g2 = mod[l]
        st = saved[l]
        next_gate = (saved[l - 1]["f"], mod[l - 1][5]) if l else None
        dz = mm(df, full[("w_down", l)], "nt", BF16, f"down_dx{l}")
        G[("w_down", l)] = mm(st["z"], df, "tn", BF16, f"down_dw{l}")
        du, dcw, dcb = glu_bwd(st["u"], dz, full["conv_w"][l], conv_b[l], f"glu_bwd{l}")
        G[("conv_w", l)] = dcw
        d_conv_b[l] = dcb[0]
        dh2 = mm(du, full[("w_up", l)], "nt", BF16, f"up_dx{l}", tk=1408)
        G[("w_up", l)] = mm(st["h2"], du, "tn", BF16, f"up_dw{l}")
        dxmid, s1, s2, dgate, a1, csum = norm_bwd(st["xmid"], norm2_g[l], sc2, dh2, dx, f"norm2_bwd{l}",
                                                  gate=(st["ypre"], st["cs"]) if l < N_A else (st["y"], g1))
        dsh2, dsc2, d_norm2[l] = s1[0], s2[0] * norm2_g[l], s2[0] * (1.0 + sc2)
        if l < N_A:
            dyp = dgate
            dg1 = full["pool_scale"][l] * a1[0]
            G[("pool_scale", l)] = g1 * a1[0]
            G[("pool_b", l)] = st["cs"] * csum[0]
            dpooled = gmm(dyp, full[("pool_w", l)], "nt", F32, f"pool_dx{l}")
            G[("pool_w", l)] = gmm(st["pooled"], dyp, "tn", BF16, f"pool_dw{l}")
            dh1 = _pool_call(dpooled, F32, f"pool_bwd{l}", True)
        else:
            j = l - N_A
            dy = dgate
            dg1 = a1[0]
            do, delta = o_proj_bwd(dy, full[("w_o", j)], st["o"], f"wo_dx{l}")
            delta = delta.reshape(N_HEADS, 1, S_)
            G[("w_o", j)] = mm(st["o"], dy, "tn", BF16, f"wo_dw{l}")
            dQ, dK, dV = attn_bwd(st["Q"], K, VX, do, st["lse"], delta, f"attn_bwd{l}")
            dkv_acc.append((dK, dV))
            dqe = q_prep(dQ, tabs, att_scale, True, f"q_prep_bwd{l}")
            dcq = mm(dqe, st["w_uq_ext"], "nt", F32, f"uq_dx{l}")
            G[("w_uq", j)] = mm(st["cq"], dqe, "tn", BF16, f"uq_dw{l}").reshape(q_rank, N_HEADS, HEAD_PAD)[
                :, :, :QK_HEAD].reshape(q_rank, N_HEADS * QK_HEAD)
            zq = jnp.zeros_like(q_norm_g[j])
            dql, _, s2q = norm_bwd(st["ql"], q_norm_g[j], zq, dcq, None, f"qnorm_bwd{l}")
            d_qnorm[j] = s2q[0]
            dh1 = mm(dql, full[("w_dq", j)], "nt", BF16, f"dq_dx{l}")
            G[("w_dq", j)] = mm(st["h1"], dql, "tn", BF16, f"dq_dw{l}")
        a2_mine = a2
        if l and l != N_A:
            dx, s1, s2, df, a2, _ = norm_bwd(st["xin"], norm1_g[l], sc1, dh1, dxmid, f"norm1_bwd{l}", gate=next_gate)
        else:
            dx, s1, s2 = norm_bwd(st["xin"], norm1_g[l], sc1, dh1, dxmid, f"norm1_bwd{l}")
        dsh1, dsc1, d_norm1[l] = s1[0], s2[0] * norm1_g[l], s2[0] * (1.0 + sc1)
        dmods[l] = jnp.concatenate([dsh1, dsc1, dg1, dsh2, dsc2, a2_mine[0]])
        if l == N_A:
            (dk_a, dv_a), (dk_b, dv_b) = dkv_acc
            dknv, d_tk = k_prep_bwd(dk_a, dk_b, dv_a, dv_b, tabs, "k_prep_bwd")
            dckv = mm(dknv, w_ukv, "nt", F32, "ukv_dx")
            d_ukv = mm(kv_saved["ckv"], dknv, "tn", BF16, "ukv_dw")
            G[("w_uk", 0)], G[("w_uv", 0)] = d_ukv[:, :N_HEADS * QK_NOPE], d_ukv[:, N_HEADS * QK_NOPE:]
            zk = jnp.zeros((KV_RANK,), F32)
            dlat, _, s2c = norm_bwd(kv_saved["lat"], ckv_norm_g, zk, dckv, None, "ckv_bwd")
            d_ckv_g = s2c[0]
            dkv_ext = jnp.concatenate([dlat, d_tk], axis=1)
            dxn = mm(dkv_ext, w_dkv_ext, "nt", BF16, "dkv_dx")
            G[("w_dkv", 0)] = mm(kv_saved["xn"], dkv_ext, "tn", BF16, "dkv_dw")[:, :kv_w]
            dx, _, s2k, df, a2, _ = norm_bwd(kv_saved["x"], kv_in_g, zD, dxn, dx, "kvin_bwd", gate=next_gate)
            d_kvin_g = s2k[0]
            e_units, e_sums, e_axes = begin_reduce([n for n in EXCHANGED if n != "pool_w"], 1, "early")
            e_send, e_recv, e_pieces, e_lands, e_token = scatter_start(e_sums, e_axes, "reduce_chips_start")
            early = (e_units, e_send, e_recv, e_lands, e_pieces, e_axes)
            mod[l - 1][4] = mod[l - 1][4] + e_token[0, 0]

    small = {"mod_b": jnp.stack(dmods), "norm1_g": jnp.stack(d_norm1), "norm2_g": jnp.stack(d_norm2),
             "kv_in_g": d_kvin_g, "ckv_norm_g": d_ckv_g, "q_norm_g": jnp.stack(d_qnorm),
             "conv_b": jnp.stack(d_conv_b), "final_g": d_final_g[0]}
    extra = {n: jnp.stack([G[(n, i)] for i in range(W[n].shape[0])]) for n in SMALL_SHARDED}
    ssizes = {n: math.prod(W[n].shape) for n in REPLICATED}
    esizes = {n: math.prod(extra[n].shape) for n in SMALL_SHARDED}
    Ts = _padded(sum(ssizes.values()) + sum(esizes.values()), 8 * PACK_COLS)

    def pack_small(d, tail=()):
        return _flat_pad([d[n].reshape(-1) for n in REPLICATED] + [t.reshape(-1) for t in tail],
                         Ts).reshape(Ts // PACK_COLS, PACK_COLS)

    parts = device_gather(pack_small(small, [extra[n] for n in SMALL_SHARDED]), "gather_small")

    l_units, l_sums, l_axes = begin_reduce([n for n in EXCHANGED if W[n].ndim > 2 and W[n].shape[0] == DEPTH] + ["pool_w"],
                                           0, "late")
    l_send, l_recv, l_pieces, l_lands, l_token = scatter_start(l_sums, l_axes, "reduce_chips_late_start", after=[parts])
    parts = parts + l_token[0, 0]

    grads, deltas, new_m, new_v = {}, {}, {}, {}
    outs = adamw_sum(parts, pack_small(W), pack_small(M1), pack_small(V2), "adamw_small")
    off = 0
    for n in REPLICATED:
        for dst, o in zip((grads, deltas, new_m, new_v), outs):
            dst[n] = o.reshape(-1)[off:off + ssizes[n]].reshape(W[n].shape)
        off += ssizes[n]
    for n, axis in SMALL_SHARDED.items():
        g_full = outs[0].reshape(-1)[off:off + esizes[n]].reshape(extra[n].shape)
        off += esizes[n]
        size = W[n].shape[axis]
        grads[n] = lax.dynamic_slice_in_dim(g_full, chip * size, size, axis=axis)
        deltas[n], new_m[n], new_v[n] = adamw(W[n], grads[n], M1[n], V2[n], f"adamw_{n}")

    dm_all = parts.reshape(N_DEV, -1)[:, :DEPTH * E].reshape(N_DEV, DEPTH, E)
    dm_mine = jnp.swapaxes(lax.dynamic_slice_in_dim(dm_all, chip * Es, Es, axis=2), 0, 1)
    grads["mod_w"], deltas["mod_w"], new_m["mod_w"], new_v["mod_w"] = adamw_modw(
        c_all.reshape(N_DEV, D, 1), dm_mine, mod_w, m_mod_w, v_mod_w, "adamw_mod_w")

    def finish_reduce(pieces, axes, got, tag):
        out = []
        for i, (sm, ax, g4) in enumerate(zip(pieces, axes, got)):
            size = sm.shape[ax] // N_CHIPS
            g4 = lax.dynamic_update_index_in_dim(g4, lax.dynamic_slice_in_dim(sm, chip * size, size, axis=ax), chip, 0)
            blk = g4.shape[1:]
            out.append(sum_parts(g4.reshape(N_CHIPS, -1, blk[-1]), f"reduce_chips_add_{tag}{i}").reshape(blk))
        return out

    e_units, e_send, e_recv, e_lands, e_pieces, e_axes = early
    early_got = scatter_wait(e_send, e_recv, e_lands, e_pieces, dx, "reduce_chips_wait")
    reduced = finish_reduce(e_pieces, e_axes, early_got, "early")
    late_got = scatter_wait(l_send, l_recv, l_lands, l_pieces, new_v["mod_w"], "reduce_chips_late_wait")
    reduced += finish_reduce(l_pieces, l_axes, late_got, "late")
    units = e_units + l_units
    slots, out_shapes = [], []
    for n in EXCHANGED:
        mine = [i for i, u in enumerate(units) if u[0] == n]
        out_shapes.append((len(mine), 2) + reduced[mine[0]].shape)
        slots += [(len(out_shapes) - 1, units[i][1]) for i in mine]
    order = [i for n in EXCHANGED for i, u in enumerate(units) if u[0] == n]
    halves = gather_halves([reduced[i] for i in order], slots, out_shapes, "reduce_gather")
    for ti, n in enumerate(EXCHANGED):
        g = halves[ti]
        for i, u in enumerate(units):
            if u[0] == n:
                g = lax.dynamic_update_slice(g, reduced[i][None, None], (u[1], ci) + (0,) * reduced[i].ndim)
        grads[n] = g.reshape(W[n].shape)
        deltas[n], new_m[n], new_v[n] = adamw(W[n], grads[n], M1[n], V2[n], f"adamw_{n}")

    return (loss, dx.reshape(x.shape), *[grads[n] for n in WEIGHT_ORDER], *[deltas[n] for n in WEIGHT_ORDER],
            *[new_m[n] for n in WEIGHT_ORDER], *[new_v[n] for n in WEIGHT_ORDER])
```

```python
import functools
import math

import jax
import jax.numpy as jnp
from jax import lax
from jax.experimental import pallas as pl
from jax.experimental.pallas import tpu as pltpu

F32 = jnp.float32
BF16 = jnp.bfloat16
MESH = pl.DeviceIdType.MESH

DEPTH = 4
N_A = 2
POOL_WINDOWS = (2, 4, 8, 16)
N_GROUPS = 4
N_HEADS = 8
QK_NOPE = 128
QK_ROPE = 64
V_HEAD = 128
QK_HEAD = QK_NOPE + QK_ROPE
HEAD_PAD = 256
KV_RANK = 256
ROPE_THETA = 10000.0
EPS = 1e-6
ADAM_LR = 0.001
ADAM_B1 = 0.9
ADAM_B2 = 0.999
ADAM_EPS = 1e-08
ADAM_WD = 0.01
ADAM_STEP = 10

N_CHIPS = 4
N_DEV = 8
LANES = 128
PACK_COLS = 1024
VMEM_LIMIT = 56 * 1024 * 1024
GLU_TILE = 256
ATT_BWD_K_BLOCK = 512
ATT_BWD_Q_BLOCK = 512
ATT_Q_BLOCK = 1024
ATT_K_BLOCK = 512
ATT_HEADS_PER_STEP = 2


def _cparams(*sem):
    return pltpu.CompilerParams(dimension_semantics=sem if sem else None, vmem_limit_bytes=VMEM_LIMIT)


def _pick(n, target, mult):
    best = None
    d = mult
    while d <= min(n, target):
        if n % d == 0:
            best = d
        d += mult
    return n if best is None else best


def _row(v):
    return v.reshape(1, -1).astype(F32)


_DIMS = {"nn": (((1,), (0,)), ((), ())), "nt": (((1,), (1,)), ((), ())), "tn": (((0,), (0,)), ((), ()))}


def _mm_body(mode, nk, has_bias, has_res):
    def body(*refs):
        a_ref, b_ref = refs[0], refs[1]
        pos = 2
        bias_ref = res_ref = cs_ref = None
        if has_bias:
            bias_ref = refs[pos]
            pos += 1
        if has_res:
            res_ref, cs_ref = refs[pos], refs[pos + 1]
            pos += 2
        o_ref = refs[pos]
        pos += 1
        o2_ref = None
        if has_res:
            o2_ref = refs[pos]
            pos += 1
        acc_ref = refs[pos] if nk > 1 else None
        k = pl.program_id(2)
        part = lax.dot_general(a_ref[...].astype(BF16), b_ref[...].astype(BF16), _DIMS[mode],
                               preferred_element_type=F32)

        def finish(y):
            if has_bias:
                y = y + bias_ref[...]
            o_ref[...] = y.astype(o_ref.dtype)
            if has_res:
                o2_ref[...] = res_ref[...] + cs_ref[...] * y

        if nk == 1:
            finish(part)
            return

        @pl.when(k == 0)
        def _():
            acc_ref[...] = part

        @pl.when((k > 0) & (k < nk - 1))
        def _():
            acc_ref[...] += part

        @pl.when(k == nk - 1)
        def _():
            finish(acc_ref[...] + part)

    return body


def mm(a, b, mode, out_dtype, name, *, tm=1408, tn=1408, tk=1024, bias=None, res=None, colscale=None, layer=None):
    bshape = b.shape if layer is None else b.shape[1:]
    if mode == "nn":
        (M, K), N = a.shape, bshape[1]
    elif mode == "nt":
        (M, K), N = a.shape, bshape[0]
    else:
        (K, M), N = a.shape, bshape[1]
    tm = _pick(M, tm, LANES if mode == "tn" else 8)
    tn = _pick(N, tn, LANES)
    tk = _pick(K, tk, LANES) if mode != "tn" else _pick(K, tk, 8)
    nk = K // tk
    a_spec = {"nn": pl.BlockSpec((tm, tk), lambda i, j, k: (i, k)),
              "nt": pl.BlockSpec((tm, tk), lambda i, j, k: (i, k)),
              "tn": pl.BlockSpec((tk, tm), lambda i, j, k: (k, i))}[mode]
    b_blk, b_map = {"nn": ((tk, tn), lambda i, j, k: (k, j)),
                    "nt": ((tn, tk), lambda i, j, k: (j, k)),
                    "tn": ((tk, tn), lambda i, j, k: (k, j))}[mode]
    if layer is None:
        b_spec = pl.BlockSpec(b_blk, b_map)
    else:
        b_spec = pl.BlockSpec((None,) + b_blk, lambda i, j, k: (layer,) + b_map(i, j, k))
    o_spec = pl.BlockSpec((tm, tn), lambda i, j, k: (i, j))
    v_spec = pl.BlockSpec((1, tn), lambda i, j, k: (0, j))
    in_specs, args = [a_spec, b_spec], [a, b]
    if bias is not None:
        in_specs.append(v_spec)
        args.append(_row(bias))
    out_shape = [jax.ShapeDtypeStruct((M, N), out_dtype)]
    out_specs = [o_spec]
    if res is not None:
        in_specs += [o_spec, v_spec]
        args += [res, _row(colscale)]
        out_shape.append(jax.ShapeDtypeStruct((M, N), F32))
        out_specs.append(o_spec)
    outs = pl.pallas_call(
        _mm_body(mode, nk, bias is not None, res is not None),
        grid=(M // tm, N // tn, nk),
        in_specs=in_specs, out_specs=out_specs, out_shape=out_shape,
        scratch_shapes=[pltpu.VMEM((tm, tn), F32)] if nk > 1 else [],
        compiler_params=_cparams("parallel", "parallel", "arbitrary"),
        name=name,
    )(*args)
    return outs if res is not None else outs[0]


def gmm(a, w, mode, out_dtype, name, *, bias=None, res=None, colscale=None, tr=512):
    S_ = a.shape[0]
    G = N_GROUPS
    C = a.shape[1] // G
    tr = _pick(S_, tr, 8)
    nr = S_ // tr
    if mode == "tn":
        def body(a_ref, b_ref, o_ref, acc_ref):
            i = pl.program_id(1)

            @pl.when(i == 0)
            def _():
                acc_ref[...] = jnp.zeros_like(acc_ref)

            acc_ref[...] += lax.dot_general(a_ref[...].astype(BF16), b_ref[...].astype(BF16), _DIMS["tn"],
                                            preferred_element_type=F32)

            @pl.when(i == nr - 1)
            def _():
                o_ref[...] = acc_ref[...].astype(o_ref.dtype)

        blk = pl.BlockSpec((tr, C), lambda g, i: (i, g))
        return pl.pallas_call(
            body, grid=(G, nr), in_specs=[blk, blk],
            out_specs=pl.BlockSpec((None, C, C), lambda g, i: (g, 0, 0)),
            out_shape=jax.ShapeDtypeStruct((G, C, C), out_dtype),
            scratch_shapes=[pltpu.VMEM((C, C), F32)],
            compiler_params=_cparams("parallel", "arbitrary"), name=name,
        )(a, w)

    has_bias, has_res = bias is not None, res is not None

    def body(*refs):
        a_ref, w_ref = refs[0], refs[1]
        pos = 2
        if has_bias:
            bias_ref = refs[pos]
            pos += 1
        if has_res:
            res_ref, cs_ref = refs[pos], refs[pos + 1]
            pos += 2
        o_ref = refs[pos]
        y = lax.dot_general(a_ref[...].astype(BF16), w_ref[...].astype(BF16), _DIMS[mode],
                            preferred_element_type=F32)
        if has_bias:
            y = y + bias_ref[...]
        o_ref[...] = y.astype(o_ref.dtype)
        if has_res:
            refs[pos + 1][...] = res_ref[...] + cs_ref[...] * y

    blk = pl.BlockSpec((tr, C), lambda i, g: (i, g))
    vec = pl.BlockSpec((1, C), lambda i, g: (0, g))
    in_specs = [blk, pl.BlockSpec((None, C, C), lambda i, g: (g, 0, 0))]
    args = [a, w]
    if has_bias:
        in_specs.append(vec)
        args.append(_row(bias))
    out_shape = [jax.ShapeDtypeStruct(a.shape, out_dtype)]
    out_specs = [blk]
    if has_res:
        in_specs += [blk, vec]
        args += [res, _row(colscale)]
        out_shape.append(jax.ShapeDtypeStruct(a.shape, F32))
        out_specs.append(blk)
    outs = pl.pallas_call(
        body, grid=(nr, G), in_specs=in_specs, out_specs=out_specs, out_shape=out_shape,
        compiler_params=_cparams("parallel", "parallel"), name=name,
    )(*args)
    return outs if has_res else outs[0]


def norm_fwd(x, g, sc, sh, out_dtype, name, tr=512):
    S_, Dn = x.shape
    tr = _pick(S_, tr, 8)

    def body(x_ref, g_ref, sc_ref, sh_ref, o_ref):
        xv = x_ref[...]
        r = lax.rsqrt(jnp.mean(xv * xv, axis=-1, keepdims=True) + EPS)
        o_ref[...] = (((xv * r) * g_ref[...]) * (1.0 + sc_ref[...]) + sh_ref[...]).astype(o_ref.dtype)

    blk = pl.BlockSpec((tr, Dn), lambda i: (i, 0))
    vec = pl.BlockSpec((1, Dn), lambda i: (0, 0))
    return pl.pallas_call(
        body, grid=(S_ // tr,), in_specs=[blk, vec, vec, vec], out_specs=blk,
        out_shape=jax.ShapeDtypeStruct((S_, Dn), out_dtype),
        compiler_params=_cparams("parallel"), name=name,
    )(x, _row(g), _row(sc), _row(sh))


def norm_bwd(x, g, sc, dh, dres, name, gate=None, tr=512):
    S_, Dn = x.shape
    tr = _pick(S_, tr, 8)
    has_res = dres is not None
    has_gate = gate is not None

    def body(*refs):
        x_ref, g_ref, sc_ref, dh_ref = refs[:4]
        pos = 4
        if has_res:
            dres_ref = refs[pos]
            pos += 1
        if has_gate:
            y_ref, cs_ref = refs[pos:pos + 2]
            pos += 2
        dx_ref, s1_ref, s2_ref = refs[pos:pos + 3]
        if has_gate:
            d_ref, a_ref, c_ref = refs[pos + 3:pos + 6]
        i = pl.program_id(0)

        @pl.when(i == 0)
        def _():
            s1_ref[...] = jnp.zeros_like(s1_ref)
            s2_ref[...] = jnp.zeros_like(s2_ref)
            if has_gate:
                a_ref[...] = jnp.zeros_like(a_ref)
                c_ref[...] = jnp.zeros_like(c_ref)

        xv = x_ref[...]
        r = lax.rsqrt(jnp.mean(xv * xv, axis=-1, keepdims=True) + EPS)
        n = xv * r
        dhv = dh_ref[...].astype(F32)
        dn = dhv * (g_ref[...] * (1.0 + sc_ref[...]))
        dx = r * (dn - n * jnp.mean(dn * n, axis=-1, keepdims=True))
        if has_res:
            dx = dx + dres_ref[...]
        dx_ref[...] = dx
        s1_ref[...] += jnp.sum(dhv, axis=0, keepdims=True)
        s2_ref[...] += jnp.sum(dhv * n, axis=0, keepdims=True)
        if has_gate:
            d_ref[...] = (dx * cs_ref[...]).astype(d_ref.dtype)
            a_ref[...] += jnp.sum(dx * y_ref[...].astype(F32), axis=0, keepdims=True)
            c_ref[...] += jnp.sum(dx, axis=0, keepdims=True)

    blk = pl.BlockSpec((tr, Dn), lambda i: (i, 0))
    vec = pl.BlockSpec((1, Dn), lambda i: (0, 0))
    in_specs, args = [blk, vec, vec, blk], [x, _row(g), _row(sc), dh]
    if has_res:
        in_specs.append(blk)
        args.append(dres)
    vshape = jax.ShapeDtypeStruct((1, Dn), F32)
    out_specs = [blk, vec, vec]
    out_shape = [jax.ShapeDtypeStruct((S_, Dn), F32), vshape, vshape]
    if has_gate:
        in_specs += [blk, vec]
        args += [gate[0], _row(gate[1])]
        out_specs += [blk, vec, vec]
        out_shape += [jax.ShapeDtypeStruct((S_, Dn), BF16), vshape, vshape]
    return pl.pallas_call(
        body, grid=(S_ // tr,), in_specs=in_specs, out_specs=out_specs, out_shape=out_shape,
        compiler_params=_cparams("arbitrary"), name=name,
    )(*args)


def gate_bwd(dx, y, colscale, name, tr=512):
    S_, Dn = dx.shape
    tr = _pick(S_, tr, 8)

    def body(dx_ref, y_ref, cs_ref, d_ref, a_ref, c_ref):
        i = pl.program_id(0)

        @pl.when(i == 0)
        def _():
            a_ref[...] = jnp.zeros_like(a_ref)
            c_ref[...] = jnp.zeros_like(c_ref)

        dxv = dx_ref[...]
        d_ref[...] = (dxv * cs_ref[...]).astype(d_ref.dtype)
        a_ref[...] += jnp.sum(dxv * y_ref[...].astype(F32), axis=0, keepdims=True)
        c_ref[...] += jnp.sum(dxv, axis=0, keepdims=True)

    blk = pl.BlockSpec((tr, Dn), lambda i: (i, 0))
    vec = pl.BlockSpec((1, Dn), lambda i: (0, 0))
    vshape = jax.ShapeDtypeStruct((1, Dn), F32)
    return pl.pallas_call(
        body, grid=(S_ // tr,), in_specs=[blk, blk, vec], out_specs=[blk, vec, vec],
        out_shape=[jax.ShapeDtypeStruct((S_, Dn), BF16), vshape, vshape],
        compiler_params=_cparams("arbitrary"), name=name,
    )(dx, y, _row(colscale))


def loss_head(x, g, target, name, tr=512):
    S_, Dn = x.shape
    tr = _pick(S_, tr, 8)

    def body(x_ref, g_ref, t_ref, dx_ref, dg_ref, loss_ref):
        i = pl.program_id(0)

        @pl.when(i == 0)
        def _():
            dg_ref[...] = jnp.zeros_like(dg_ref)
            loss_ref[...] = jnp.zeros_like(loss_ref)

        xv = x_ref[...]
        r = lax.rsqrt(jnp.mean(xv * xv, axis=-1, keepdims=True) + EPS)
        n = xv * r
        e = n * g_ref[...] - t_ref[...]
        loss_ref[...] += 0.5 * jnp.sum(jnp.mean(e * e, axis=-1, keepdims=True), axis=0, keepdims=True)
        dy = e * (1.0 / Dn)
        dg_ref[...] += jnp.sum(dy * n, axis=0, keepdims=True)
        dn = dy * g_ref[...]
        dx_ref[...] = r * (dn - n * jnp.mean(dn * n, axis=-1, keepdims=True))

    blk = pl.BlockSpec((tr, Dn), lambda i: (i, 0))
    vec = pl.BlockSpec((1, Dn), lambda i: (0, 0))
    one = pl.BlockSpec((1, 1), lambda i: (0, 0))
    return pl.pallas_call(
        body, grid=(S_ // tr,), in_specs=[blk, vec, blk], out_specs=[blk, vec, one],
        out_shape=[jax.ShapeDtypeStruct((S_, Dn), F32), jax.ShapeDtypeStruct((1, Dn), F32),
                   jax.ShapeDtypeStruct((1, 1), F32)],
        compiler_params=_cparams("arbitrary"), name=name,
    )(x, _row(g), target)


POOL_HALO = 16
POOL_CHUNK = 512


def _rows(ref, lo, hi, n_rows):
    parts = []
    if lo < 0:
        parts.append(jnp.zeros((-lo, ref.shape[1]), F32))
    parts.append(ref[max(lo, 0):min(hi, n_rows), :].astype(F32))
    if hi > n_rows:
        parts.append(jnp.zeros((hi - n_rows, ref.shape[1]), F32))
    return parts[0] if len(parts) == 1 else jnp.concatenate(parts, axis=0)


def _window_sum(e, w, back):
    n = e.shape[0]
    s, width = e, 1
    while width < w:
        s = s + pltpu.roll(s, width if back else n - width, 0)
        width *= 2
    return s


def _pool_call(h, out_dtype, name, backward):
    S_, Dn = h.shape
    C = Dn // N_GROUPS
    ch = _pick(S_, POOL_CHUNK, 8)

    def body(h_ref, o_ref):
        g = pl.program_id(0)
        for gi, w in enumerate(POOL_WINDOWS):
            @pl.when(g == gi)
            def _(w=w):
                for r0 in range(0, S_, ch):
                    t = (r0 + lax.broadcasted_iota(jnp.int32, (ch, C), 0)).astype(F32)
                    cnt = jnp.minimum(t + 1.0, float(w))
                    if not backward:
                        ext = _rows(h_ref, r0 - POOL_HALO, r0 + ch, S_)
                        cur = ext[POOL_HALO:]
                        mean = _window_sum(ext, w, True)[POOL_HALO:] / cnt
                        o_ref[r0:r0 + ch, :] = (mean - cur).astype(o_ref.dtype)
                    else:
                        ext = _rows(h_ref, r0, r0 + ch + POOL_HALO, S_)
                        text = (r0 + lax.broadcasted_iota(jnp.int32, (ch + POOL_HALO, C), 0)).astype(F32)
                        e = ext / jnp.minimum(text + 1.0, float(w))
                        o_ref[r0:r0 + ch, :] = (_window_sum(e, w, False)[:ch] - ext[:ch]).astype(o_ref.dtype)

    blk = pl.BlockSpec((S_, C), lambda g: (0, g))
    return pl.pallas_call(
        body, grid=(N_GROUPS,), in_specs=[blk], out_specs=blk,
        out_shape=jax.ShapeDtypeStruct((S_, Dn), out_dtype),
        compiler_params=_cparams("parallel"), name=name,
    )(h)


GLU_CHUNK = 512
GLU_HALO = 16
_SQRT_HALF = 0.7071067811865476
_INV_SQRT_2PI = 0.3989422804014327


def _gelu(a):
    return 0.5 * a * (1.0 + lax.erf(a * _SQRT_HALF))


def _gelu_grad(a):
    return 0.5 * (1.0 + lax.erf(a * _SQRT_HALF)) + a * (_INV_SQRT_2PI * jnp.exp(-0.5 * a * a))


def glu_fwd(u, conv_w, conv_b, name):
    S_, F2 = u.shape
    Fh = F2 // 2
    tf = GLU_TILE
    nt = Fh // tf
    ch = _pick(S_, GLU_CHUNK, GLU_HALO)

    def body(a_ref, v_ref, cw_ref, cb_ref, z_ref):
        cw0, cw1, cw2 = cw_ref[0:1, :], cw_ref[1:2, :], cw_ref[2:3, :]
        cb = cb_ref[...]
        for r0 in range(0, S_, ch):
            ext = _rows(a_ref, r0 - GLU_HALO, r0 + ch, S_)
            a0 = ext[GLU_HALO:]
            a1 = pltpu.roll(ext, 1, 0)[GLU_HALO:]
            a2 = pltpu.roll(ext, 2, 0)[GLU_HALO:]
            ac = a2 * cw0 + a1 * cw1 + a0 * cw2 + cb
            z_ref[r0:r0 + ch, :] = (_gelu(ac) * v_ref[r0:r0 + ch, :].astype(F32)).astype(z_ref.dtype)

    return pl.pallas_call(
        body, grid=(nt,),
        in_specs=[pl.BlockSpec((S_, tf), lambda j: (0, j)), pl.BlockSpec((S_, tf), lambda j: (0, j + nt)),
                  pl.BlockSpec((3, tf), lambda j: (0, j)), pl.BlockSpec((1, tf), lambda j: (0, j))],
        out_specs=pl.BlockSpec((S_, tf), lambda j: (0, j)),
        out_shape=jax.ShapeDtypeStruct((S_, Fh), BF16),
        compiler_params=_cparams("parallel"), name=name,
    )(u, u, conv_w, _row(conv_b))


def glu_bwd(u, dz, conv_w, conv_b, name):
    S_, F2 = u.shape
    Fh = F2 // 2
    tf = GLU_TILE
    nt = Fh // tf
    ch = _pick(S_, GLU_CHUNK, GLU_HALO)

    def body(a_ref, v_ref, dz_ref, cw_ref, cb_ref, du_ref, dcw_ref, dcb_ref, da_buf, dv_buf, sems):
        j = pl.program_id(0)
        slot = j % 2

        def writes(step, sl):
            lo = pl.multiple_of(step * tf, tf)
            return (pltpu.make_async_copy(da_buf.at[sl], du_ref.at[:, pl.ds(lo, tf)], sems.at[sl, 0]),
                    pltpu.make_async_copy(dv_buf.at[sl], du_ref.at[:, pl.ds(Fh + lo, tf)], sems.at[sl, 1]))

        @pl.when(j >= 2)
        def _():
            for cp in writes(j - 2, slot):
                cp.wait()

        cw0, cw1, cw2 = cw_ref[0:1, :], cw_ref[1:2, :], cw_ref[2:3, :]
        cb = cb_ref[...]
        acc = [jnp.zeros((1, tf), F32) for _ in range(4)]
        n = ch + GLU_HALO
        for r0 in range(0, S_, ch):
            ext = _rows(a_ref, r0 - GLU_HALO, r0 + n, S_)
            a0 = ext[GLU_HALO:]
            a1 = pltpu.roll(ext, 1, 0)[GLU_HALO:]
            a2 = pltpu.roll(ext, 2, 0)[GLU_HALO:]
            ac = a2 * cw0 + a1 * cw1 + a0 * cw2 + cb
            vv = _rows(v_ref, r0, r0 + n, S_)
            dzv = _rows(dz_ref, r0, r0 + n, S_)
            gl = _gelu(ac)
            dac = dzv * vv * _gelu_grad(ac)
            da = (dac * cw2 + pltpu.roll(dac, n - 1, 0) * cw1 + pltpu.roll(dac, n - 2, 0) * cw0)[:ch]
            da_buf[slot, r0:r0 + ch, :] = da.astype(da_buf.dtype)
            dv_buf[slot, r0:r0 + ch, :] = (dzv[:ch] * gl[:ch]).astype(dv_buf.dtype)
            dc = dac[:ch]
            acc[0] = acc[0] + jnp.sum(dc * a2[:ch], axis=0, keepdims=True)
            acc[1] = acc[1] + jnp.sum(dc * a1[:ch], axis=0, keepdims=True)
            acc[2] = acc[2] + jnp.sum(dc * a0[:ch], axis=0, keepdims=True)
            acc[3] = acc[3] + jnp.sum(dc, axis=0, keepdims=True)
        dcw_ref[0:1, :] = acc[0]
        dcw_ref[1:2, :] = acc[1]
        dcw_ref[2:3, :] = acc[2]
        dcb_ref[...] = acc[3]
        for cp in writes(j, slot):
            cp.start()

        @pl.when(j == nt - 1)
        def _():
            for cp in writes(j, slot):
                cp.wait()
            if nt > 1:
                for cp in writes(j - 1, 1 - slot):
                    cp.wait()

    return pl.pallas_call(
        body, grid=(nt,),
        in_specs=[pl.BlockSpec((S_, tf), lambda j: (0, j)), pl.BlockSpec((S_, tf), lambda j: (0, j + nt)),
                  pl.BlockSpec((S_, tf), lambda j: (0, j)),
                  pl.BlockSpec((3, tf), lambda j: (0, j)), pl.BlockSpec((1, tf), lambda j: (0, j))],
        out_specs=[_ANY, pl.BlockSpec((3, tf), lambda j: (0, j)), pl.BlockSpec((1, tf), lambda j: (0, j))],
        out_shape=[jax.ShapeDtypeStruct((S_, F2), BF16), jax.ShapeDtypeStruct((3, Fh), F32),
                   jax.ShapeDtypeStruct((1, Fh), F32)],
        scratch_shapes=[pltpu.VMEM((2, S_, tf), BF16), pltpu.VMEM((2, S_, tf), BF16), pltpu.SemaphoreType.DMA((2, 2))],
        compiler_params=_cparams("arbitrary"), name=name,
    )(u, u, dz, conv_w, _row(conv_b))


def rope_tables(pos, inv, name, tr=512):
    S_ = pos.shape[0]
    tr = _pick(S_, tr, 8)

    def body(p_ref, inv_ref, c_ref, s1_ref, s2_ref):
        ang = p_ref[...] * inv_ref[...]
        lane = lax.broadcasted_iota(jnp.int32, ang.shape, 1)
        half = QK_ROPE // 2
        cosv, sinv = jnp.cos(ang), jnp.sin(ang)
        c_ref[...] = jnp.where(lane < QK_ROPE, cosv, 0.0)
        s1_ref[...] = jnp.where(lane < half, -sinv, 0.0)
        s2_ref[...] = jnp.where((lane >= half) & (lane < QK_ROPE), sinv, 0.0)

    blk = pl.BlockSpec((tr, LANES), lambda i: (i, 0))
    shp = jax.ShapeDtypeStruct((S_, LANES), F32)
    return pl.pallas_call(
        body, grid=(S_ // tr,),
        in_specs=[pl.BlockSpec((tr, 1), lambda i: (i, 0)), pl.BlockSpec((1, LANES), lambda i: (0, 0))],
        out_specs=[blk, blk, blk], out_shape=[shp, shp, shp],
        compiler_params=_cparams("parallel"), name=name,
    )(pos, inv)


_HALF = QK_ROPE // 2


def _rope(t, c, s1, s2):
    return t * c + pltpu.roll(t, LANES - _HALF, 1) * s1 + pltpu.roll(t, _HALF, 1) * s2


def _rope_t(d, c, s1, s2):
    return d * c + pltpu.roll(d * s1, _HALF, 1) + pltpu.roll(d * s2, LANES - _HALF, 1)


def q_prep(q, tabs, scale, backward, name, tr=512):
    S_, W = q.shape
    tr = _pick(S_, tr, 8)

    def body(q_ref, c_ref, s1_ref, s2_ref, o_ref):
        o_ref[:, 0:LANES] = (q_ref[:, 0:LANES].astype(F32) * scale).astype(o_ref.dtype)
        t = q_ref[:, LANES:2 * LANES].astype(F32)
        fn = _rope_t if backward else _rope
        o_ref[:, LANES:2 * LANES] = (fn(t, c_ref[...], s1_ref[...], s2_ref[...]) * scale).astype(o_ref.dtype)

    blk = pl.BlockSpec((tr, HEAD_PAD), lambda i, h: (i, h))
    tab = pl.BlockSpec((tr, LANES), lambda i, h: (i, 0))
    return pl.pallas_call(
        body, grid=(S_ // tr, W // HEAD_PAD), in_specs=[blk, tab, tab, tab], out_specs=blk,
        out_shape=jax.ShapeDtypeStruct((S_, W), BF16),
        compiler_params=_cparams("parallel", "parallel"), name=name,
    )(q, *tabs)


def k_prep(knv, kv_ext, tabs, name, tr=512):
    S_ = knv.shape[0]
    tr = _pick(S_, tr, 8)

    def body(kn_ref, v_ref, t_ref, c_ref, s1_ref, s2_ref, o_ref, vx_ref):
        o_ref[:, 0:LANES] = kn_ref[...].astype(o_ref.dtype)
        o_ref[:, LANES:2 * LANES] = _rope(t_ref[...], c_ref[...], s1_ref[...], s2_ref[...]).astype(o_ref.dtype)
        vx_ref[:, 0:V_HEAD] = v_ref[...].astype(vx_ref.dtype)
        vx_ref[:, V_HEAD:HEAD_PAD] = jnp.ones((tr, HEAD_PAD - V_HEAD), vx_ref.dtype)

    tab = pl.BlockSpec((tr, LANES), lambda i, h: (i, 0))
    head = pl.BlockSpec((tr, HEAD_PAD), lambda i, h: (i, h))
    shp = jax.ShapeDtypeStruct((S_, N_HEADS * HEAD_PAD), BF16)
    return pl.pallas_call(
        body, grid=(S_ // tr, N_HEADS),
        in_specs=[pl.BlockSpec((tr, LANES), lambda i, h: (i, h)),
                  pl.BlockSpec((tr, V_HEAD), lambda i, h: (i, N_HEADS + h)),
                  pl.BlockSpec((tr, LANES), lambda i, h: (i, KV_RANK // LANES)), tab, tab, tab],
        out_specs=[head, head], out_shape=[shp, shp],
        compiler_params=_cparams("parallel", "parallel"), name=name,
    )(knv, knv, kv_ext, *tabs)


def k_prep_bwd(dk_a, dk_b, dv_a, dv_b, tabs, name, tr=256):
    S_ = dk_a.shape[0]
    tr = _pick(S_, tr, 8)
    HV = N_HEADS * V_HEAD

    def body(ka_ref, kb_ref, va_ref, vb_ref, c_ref, s1_ref, s2_ref, o_ref, t_ref):
        dr = jnp.zeros((tr, LANES), F32)
        for h in range(N_HEADS):
            lo = h * HEAD_PAD
            o_ref[:, h * LANES:(h + 1) * LANES] = (ka_ref[:, lo:lo + LANES] + kb_ref[:, lo:lo + LANES]).astype(o_ref.dtype)
            dr = dr + ka_ref[:, lo + LANES:lo + 2 * LANES] + kb_ref[:, lo + LANES:lo + 2 * LANES]
        o_ref[:, HV:2 * HV] = (va_ref[...] + vb_ref[...]).astype(o_ref.dtype)
        t_ref[...] = _rope_t(dr, c_ref[...], s1_ref[...], s2_ref[...])

    kblk = pl.BlockSpec((tr, N_HEADS * HEAD_PAD), lambda i: (i, 0))
    vblk = pl.BlockSpec((tr, HV), lambda i: (i, 0))
    tab = pl.BlockSpec((tr, LANES), lambda i: (i, 0))
    return pl.pallas_call(
        body, grid=(S_ // tr,), in_specs=[kblk, kblk, vblk, vblk, tab, tab, tab],
        out_specs=[pl.BlockSpec((tr, 2 * HV), lambda i: (i, 0)), tab],
        out_shape=[jax.ShapeDtypeStruct((S_, 2 * HV), BF16), jax.ShapeDtypeStruct((S_, LANES), F32)],
        compiler_params=_cparams("parallel"), name=name,
    )(dk_a, dk_b, dv_a, dv_b, *tabs)


_NEG = -1e30


def attn_fwd(q, k, vx, name):
    S_ = q.shape[0]
    TQ = _pick(S_, ATT_Q_BLOCK, 8)
    TK = _pick(S_, ATT_K_BLOCK, 8)
    assert TQ % TK == 0 or TK % TQ == 0
    HP = ATT_HEADS_PER_STEP
    W = HP * HEAD_PAD

    def body(q_ref, k_ref, v_ref, o_ref, lse_ref):
        i = pl.program_id(1)
        qs = [q_ref[:, h * HEAD_PAD:(h + 1) * HEAD_PAD] for h in range(HP)]

        def step(j, carry, masked):
            start = pl.multiple_of(j * TK, TK)
            out = []
            for h in range(HP):
                m, acc = carry[h]
                cols = slice(h * HEAD_PAD, (h + 1) * HEAD_PAD)
                s = lax.dot_general(qs[h], k_ref[pl.ds(start, TK), cols], _DIMS["nt"], preferred_element_type=F32)
                if masked:
                    rowi = i * TQ + lax.broadcasted_iota(jnp.int32, (TQ, TK), 0)
                    coli = j * TK + lax.broadcasted_iota(jnp.int32, (TQ, TK), 1)
                    s = jnp.where(coli <= rowi, s, _NEG)
                m_new = jnp.maximum(m, jnp.max(s, axis=-1, keepdims=True))
                alpha = jnp.exp(m - m_new)
                p = jnp.exp(s - m_new).astype(BF16)
                acc = alpha * acc + lax.dot_general(p, v_ref[pl.ds(start, TK), cols], _DIMS["nn"],
                                                    preferred_element_type=F32)
                out.append((m_new, acc))
            return tuple(out)

        init = tuple((jnp.full((TQ, 1), _NEG, F32), jnp.zeros((TQ, HEAD_PAD), F32)) for _ in range(HP))
        n_full, n_diag = (i * (TQ // TK), TQ // TK) if TQ >= TK else (i // (TK // TQ), 1)
        carry = lax.fori_loop(0, n_full, functools.partial(step, masked=False), init)
        for d in range(n_diag):
            carry = step(n_full + d, carry, True)
        for h in range(HP):
            m, acc = carry[h]
            l = acc[:, V_HEAD:]
            o_ref[:, h * V_HEAD:(h + 1) * V_HEAD] = (acc[:, :V_HEAD] / l).astype(o_ref.dtype)
            lse_ref[h] = m + jnp.log(jnp.max(l, axis=-1, keepdims=True))

    return pl.pallas_call(
        body, grid=(N_HEADS // HP, S_ // TQ),
        in_specs=[pl.BlockSpec((TQ, W), lambda g, i: (i, g)),
                  pl.BlockSpec((S_, W), lambda g, i: (0, g)),
                  pl.BlockSpec((S_, W), lambda g, i: (0, g))],
        out_specs=[pl.BlockSpec((TQ, HP * V_HEAD), lambda g, i: (i, g)),
                   pl.BlockSpec((HP, TQ, 1), lambda g, i: (g, i, 0))],
        out_shape=[jax.ShapeDtypeStruct((S_, N_HEADS * V_HEAD), BF16), jax.ShapeDtypeStruct((N_HEADS, S_, 1), F32)],
        compiler_params=_cparams("parallel", "parallel"), name=name,
    )(q, k, vx)


def q_proj(cq, w_ext, tabs, scale, name, tm=1024):
    S_, R = cq.shape
    tm = _pick(S_, tm, 16)

    def body(c_ref, w_ref, t_c, t_s1, t_s2, o_ref):
        y = lax.dot_general(c_ref[...].astype(BF16), w_ref[...].astype(BF16), _DIMS["nn"], preferred_element_type=F32)
        o_ref[:, 0:LANES] = (y[:, 0:LANES] * scale).astype(o_ref.dtype)
        o_ref[:, LANES:2 * LANES] = (_rope(y[:, LANES:2 * LANES], t_c[...], t_s1[...], t_s2[...]) * scale).astype(o_ref.dtype)

    tab = pl.BlockSpec((tm, LANES), lambda i, h: (i, 0))
    return pl.pallas_call(
        body, grid=(S_ // tm, N_HEADS),
        in_specs=[pl.BlockSpec((tm, R), lambda i, h: (i, 0)), pl.BlockSpec((R, HEAD_PAD), lambda i, h: (0, h)),
                  tab, tab, tab],
        out_specs=pl.BlockSpec((tm, HEAD_PAD), lambda i, h: (i, h)),
        out_shape=jax.ShapeDtypeStruct((S_, N_HEADS * HEAD_PAD), BF16),
        compiler_params=_cparams("parallel", "parallel"), name=name,
    )(cq, w_ext, *tabs)


def q_proj_bwd(dq, cq, w_ext, tabs, scale, name, tm=512):
    S_, R = cq.shape
    tm = _pick(S_, tm, 16)
    nr = S_ // tm

    def body(dq_ref, c_ref, w_ref, t_c, t_s1, t_s2, dc_ref, dw_ref, acc_ref):
        h, i = pl.program_id(0), pl.program_id(1)
        g = jnp.concatenate([dq_ref[:, 0:LANES] * scale,
                             _rope_t(dq_ref[:, LANES:2 * LANES], t_c[...], t_s1[...], t_s2[...]) * scale],
                            axis=1).astype(BF16)
        part = lax.dot_general(g, w_ref[...].astype(BF16), _DIMS["nt"], preferred_element_type=F32)
        rows = pl.ds(pl.multiple_of(i * tm, tm), tm)

        @pl.when(h == 0)
        def _():
            dc_ref[rows, :] = part

        @pl.when(h > 0)
        def _():
            dc_ref[rows, :] += part

        dwp = lax.dot_general(c_ref[...].astype(BF16), g, _DIMS["tn"], preferred_element_type=F32)

        @pl.when(i == 0)
        def _():
            acc_ref[...] = dwp

        @pl.when(i > 0)
        def _():
            acc_ref[...] += dwp

        @pl.when(i == nr - 1)
        def _():
            dw_ref[...] = acc_ref[...].astype(dw_ref.dtype)

    tab = pl.BlockSpec((tm, LANES), lambda h, i: (i, 0))
    return pl.pallas_call(
        body, grid=(N_HEADS, nr),
        in_specs=[pl.BlockSpec((tm, HEAD_PAD), lambda h, i: (i, h)), pl.BlockSpec((tm, R), lambda h, i: (i, 0)),
                  pl.BlockSpec((R, HEAD_PAD), lambda h, i: (0, h)), tab, tab, tab],
        out_specs=[pl.BlockSpec((S_, R), lambda h, i: (0, 0)), pl.BlockSpec((R, HEAD_PAD), lambda h, i: (0, h))],
        out_shape=[jax.ShapeDtypeStruct((S_, R), F32), jax.ShapeDtypeStruct((R, N_HEADS * HEAD_PAD), BF16)],
        scratch_shapes=[pltpu.VMEM((R, HEAD_PAD), F32)],
        compiler_params=_cparams("arbitrary", "arbitrary"), name=name,
    )(dq, cq, w_ext, *tabs)


def kv_proj(ckv, w_ukv, kv_ext, tabs, name, tm=1024):
    S_, R = ckv.shape
    tm = _pick(S_, tm, 16)

    def body(c_ref, wk_ref, wv_ref, t_ref, t_c, t_s1, t_s2, k_ref, vx_ref):
        cv = c_ref[...].astype(BF16)
        k_ref[:, 0:LANES] = lax.dot_general(cv, wk_ref[...].astype(BF16), _DIMS["nn"],
                                            preferred_element_type=F32).astype(k_ref.dtype)
        k_ref[:, LANES:2 * LANES] = _rope(t_ref[...], t_c[...], t_s1[...], t_s2[...]).astype(k_ref.dtype)
        vx_ref[:, 0:V_HEAD] = lax.dot_general(cv, wv_ref[...].astype(BF16), _DIMS["nn"],
                                              preferred_element_type=F32).astype(vx_ref.dtype)
        vx_ref[:, V_HEAD:HEAD_PAD] = jnp.ones((tm, HEAD_PAD - V_HEAD), vx_ref.dtype)

    tab = pl.BlockSpec((tm, LANES), lambda i, h: (i, 0))
    head = pl.BlockSpec((tm, HEAD_PAD), lambda i, h: (i, h))
    shp = jax.ShapeDtypeStruct((S_, N_HEADS * HEAD_PAD), BF16)
    return pl.pallas_call(
        body, grid=(S_ // tm, N_HEADS),
        in_specs=[pl.BlockSpec((tm, R), lambda i, h: (i, 0)), pl.BlockSpec((R, QK_NOPE), lambda i, h: (0, h)),
                  pl.BlockSpec((R, V_HEAD), lambda i, h: (0, N_HEADS + h)),
                  pl.BlockSpec((tm, LANES), lambda i, h: (i, KV_RANK // LANES)), tab, tab, tab],
        out_specs=[head, head], out_shape=[shp, shp],
        compiler_params=_cparams("parallel", "parallel"), name=name,
    )(ckv, w_ukv, w_ukv, kv_ext, *tabs)


def o_proj_bwd(dy, w_o, o, name, tm=512):
    S_, Dn = dy.shape
    HV = w_o.shape[0]
    tm = _pick(S_, tm, 16)

    def body(dy_ref, w_ref, o_ref, do_ref, d_ref):
        do = lax.dot_general(dy_ref[...].astype(BF16), w_ref[...].astype(BF16), _DIMS["nt"], preferred_element_type=F32)
        do_ref[...] = do.astype(do_ref.dtype)
        prod = do * o_ref[...].astype(F32)
        for h in range(N_HEADS):
            d_ref[h] = jnp.sum(prod[:, h * V_HEAD:(h + 1) * V_HEAD], axis=-1, keepdims=True)

    return pl.pallas_call(
        body, grid=(S_ // tm,),
        in_specs=[pl.BlockSpec((tm, Dn), lambda i: (i, 0)), pl.BlockSpec((HV, Dn), lambda i: (0, 0)),
                  pl.BlockSpec((tm, HV), lambda i: (i, 0))],
        out_specs=[pl.BlockSpec((tm, HV), lambda i: (i, 0)), pl.BlockSpec((N_HEADS, tm, 1), lambda i: (0, i, 0))],
        out_shape=[jax.ShapeDtypeStruct((S_, HV), BF16), jax.ShapeDtypeStruct((N_HEADS, S_, 1), F32)],
        compiler_params=_cparams("parallel"), name=name,
    )(dy, w_o, o)


def attn_delta(o, do, name, tr=512):
    S_ = o.shape[0]
    tr = _pick(S_, tr, 8)

    def body(o_ref, do_ref, d_ref):
        d_ref[...] = jnp.sum(o_ref[...].astype(F32) * do_ref[...].astype(F32), axis=-1, keepdims=True)

    blk = pl.BlockSpec((tr, V_HEAD), lambda i, h: (i, h))
    return pl.pallas_call(
        body, grid=(S_ // tr, N_HEADS), in_specs=[blk, blk],
        out_specs=pl.BlockSpec((None, tr, 1), lambda i, h: (h, i, 0)),
        out_shape=jax.ShapeDtypeStruct((N_HEADS, S_, 1), F32),
        compiler_params=_cparams("parallel", "parallel"), name=name,
    )(o, do)


def attn_bwd(q, k, vx, do, lse_row, delta_row, name):
    S_ = q.shape[0]
    TK = _pick(S_, ATT_BWD_K_BLOCK, LANES)
    TQ = _pick(S_, ATT_BWD_Q_BLOCK, TK)
    HP = ATT_HEADS_PER_STEP
    W = HP * HEAD_PAD
    ratio = TQ // TK
    nq = S_ // TQ

    def body(q_ref, do_ref, lse_ref, dl_ref, k_ref, v_ref, dq_ref, dk_ref, dv_ref):
        j = pl.program_id(1)

        @pl.when(j == 0)
        def _():
            dq_ref[...] = jnp.zeros_like(dq_ref)

        ks = [k_ref[:, h * HEAD_PAD:(h + 1) * HEAD_PAD] for h in range(HP)]
        vs = [v_ref[:, h * HEAD_PAD:h * HEAD_PAD + V_HEAD] for h in range(HP)]

        def step(i, carry, masked):
            start = pl.multiple_of(i * TQ, TQ)
            out = []
            for h in range(HP):
                dk, dv = carry[h]
                cols = slice(h * HEAD_PAD, (h + 1) * HEAD_PAD)
                qv = q_ref[pl.ds(start, TQ), cols]
                dov = do_ref[pl.ds(start, TQ), h * V_HEAD:(h + 1) * V_HEAD]
                st = lax.dot_general(ks[h], qv, _DIMS["nt"], preferred_element_type=F32)
                pt = jnp.exp(st - lse_ref[h, :, pl.ds(start, TQ)])
                if masked:
                    keyi = j * TK + lax.broadcasted_iota(jnp.int32, (TK, TQ), 0)
                    qryi = i * TQ + lax.broadcasted_iota(jnp.int32, (TK, TQ), 1)
                    pt = jnp.where(keyi <= qryi, pt, 0.0)
                dpt = lax.dot_general(vs[h], dov, _DIMS["nt"], preferred_element_type=F32)
                dst = (pt * (dpt - dl_ref[h, :, pl.ds(start, TQ)])).astype(BF16)
                dv = dv + lax.dot_general(pt.astype(BF16), dov, _DIMS["nn"], preferred_element_type=F32)
                dk = dk + lax.dot_general(dst, qv, _DIMS["nn"], preferred_element_type=F32)
                dq_ref[pl.ds(start, TQ), cols] += lax.dot_general(dst, ks[h], _DIMS["tn"], preferred_element_type=F32)
                out.append((dk, dv))
            return tuple(out)

        init = tuple((jnp.zeros((TK, HEAD_PAD), F32), jnp.zeros((TK, V_HEAD), F32)) for _ in range(HP))
        first = j // ratio
        carry = lax.fori_loop(first + 1, nq, functools.partial(step, masked=False), step(first, init, True))
        for h in range(HP):
            dk_ref[:, h * HEAD_PAD:(h + 1) * HEAD_PAD] = carry[h][0]
            dv_ref[:, h * V_HEAD:(h + 1) * V_HEAD] = carry[h][1]

    return pl.pallas_call(
        body, grid=(N_HEADS // HP, S_ // TK),
        in_specs=[pl.BlockSpec((S_, W), lambda g, j: (0, g)),
                  pl.BlockSpec((S_, HP * V_HEAD), lambda g, j: (0, g)),
                  pl.BlockSpec((HP, 1, S_), lambda g, j: (g, 0, 0)),
                  pl.BlockSpec((HP, 1, S_), lambda g, j: (g, 0, 0)),
                  pl.BlockSpec((TK, W), lambda g, j: (j, g)),
                  pl.BlockSpec((TK, W), lambda g, j: (j, g))],
        out_specs=[pl.BlockSpec((S_, W), lambda g, j: (0, g)),
                   pl.BlockSpec((TK, W), lambda g, j: (j, g)),
                   pl.BlockSpec((TK, HP * V_HEAD), lambda g, j: (j, g))],
        out_shape=[jax.ShapeDtypeStruct((S_, N_HEADS * HEAD_PAD), F32),
                   jax.ShapeDtypeStruct((S_, N_HEADS * HEAD_PAD), F32),
                   jax.ShapeDtypeStruct((S_, N_HEADS * V_HEAD), F32)],
        compiler_params=_cparams("parallel", "arbitrary"), name=name,
    )(q, do, lse_row, delta_row, k, vx)


def mods_fwd(c_all, mod_w, mod_b, name, tn=512):
    L, Dn, E = mod_w.shape
    R = c_all.shape[0]
    tn = _pick(E, tn, LANES)

    def body(c_ref, w_ref, b_ref, o_ref):
        cv = c_ref[...]
        sc = (cv / (1.0 + jnp.exp(-cv))).astype(BF16)
        o_ref[...] = lax.dot_general(sc, w_ref[...].astype(BF16), _DIMS["nn"], preferred_element_type=F32) + b_ref[...]

    return pl.pallas_call(
        body, grid=(L, E // tn),
        in_specs=[pl.BlockSpec((R, Dn), lambda l, j: (0, 0)), pl.BlockSpec((None, Dn, tn), lambda l, j: (l, 0, j)),
                  pl.BlockSpec((None, 1, tn), lambda l, j: (l, 0, j))],
        out_specs=pl.BlockSpec((None, R, tn), lambda l, j: (l, 0, j)),
        out_shape=jax.ShapeDtypeStruct((L, R, E), F32),
        compiler_params=_cparams("parallel", "parallel"), name=name,
    )(c_all, mod_w, mod_b.reshape(L, 1, E))


def _adam_math(w, g, m, v):
    m = ADAM_B1 * m + (1.0 - ADAM_B1) * g
    v = ADAM_B2 * v + (1.0 - ADAM_B2) * (g * g)
    m_hat = m / (1.0 - ADAM_B1 ** ADAM_STEP)
    v_hat = v / (1.0 - ADAM_B2 ** ADAM_STEP)
    delta = -ADAM_LR * (m_hat / (jnp.sqrt(v_hat) + ADAM_EPS) + ADAM_WD * w)
    return delta, m, v


def _as2d(a):
    return a.reshape(-1, a.shape[-1]) if a.ndim != 2 else a


def adamw(w, g, m, v, name):
    shape = w.shape
    w2, g2, m2, v2 = _as2d(w), _as2d(g), _as2d(m), _as2d(v)
    R, C = w2.shape
    tr = _pick(R, max(8, (1 << 18) // C // 8 * 8), 8)

    def body(w_ref, g_ref, m_ref, v_ref, d_ref, mo_ref, vo_ref):
        d, mn, vn = _adam_math(w_ref[...], g_ref[...], m_ref[...], v_ref[...])
        d_ref[...] = d
        mo_ref[...] = mn
        vo_ref[...] = vn

    blk = pl.BlockSpec((tr, C), lambda i: (i, 0))
    shp = jax.ShapeDtypeStruct((R, C), F32)
    outs = pl.pallas_call(
        body, grid=(R // tr,), in_specs=[blk] * 4, out_specs=[blk] * 3, out_shape=[shp] * 3,
        compiler_params=_cparams("parallel"), name=name,
    )(w2, g2, m2, v2)
    return tuple(o.reshape(shape) for o in outs)


def adamw_sum(parts, w, m, v, name):
    P, R, C = parts.shape

    def body(p_ref, w_ref, m_ref, v_ref, g_ref, d_ref, mo_ref, vo_ref):
        g = p_ref[0]
        for k in range(1, P):
            g = g + p_ref[k]
        d, mn, vn = _adam_math(w_ref[...], g, m_ref[...], v_ref[...])
        g_ref[...] = g
        d_ref[...] = d
        mo_ref[...] = mn
        vo_ref[...] = vn

    shp = jax.ShapeDtypeStruct((R, C), F32)
    return pl.pallas_call(body, out_shape=[shp] * 4, compiler_params=_cparams(), name=name)(parts, w, m, v)


def adamw_modw(c_col, dm, w, m, v, name, tr=256, tn=512):
    L, Dn, E = w.shape
    B = c_col.shape[0]
    tr = _pick(Dn, tr, 8)
    tn = _pick(E, tn, LANES)

    def body(c_ref, dm_ref, w_ref, m_ref, v_ref, g_ref, d_ref, mo_ref, vo_ref):
        g = jnp.zeros((tr, tn), F32)
        for b in range(B):
            cv = c_ref[b]
            g = g + (cv / (1.0 + jnp.exp(-cv))) * dm_ref[b:b + 1, :]
        d, mn, vn = _adam_math(w_ref[...], g, m_ref[...], v_ref[...])
        g_ref[...] = g
        d_ref[...] = d
        mo_ref[...] = mn
        vo_ref[...] = vn

    blk = pl.BlockSpec((None, tr, tn), lambda l, i, j: (l, i, j))
    shp = jax.ShapeDtypeStruct((L, Dn, E), F32)
    return pl.pallas_call(
        body, grid=(L, Dn // tr, E // tn),
        in_specs=[pl.BlockSpec((B, tr, 1), lambda l, i, j: (0, i, 0)),
                  pl.BlockSpec((None, B, tn), lambda l, i, j: (l, 0, j)), blk, blk, blk],
        out_specs=[blk] * 4, out_shape=[shp] * 4,
        compiler_params=_cparams("parallel", "parallel", "parallel"), name=name,
    )(c_col, dm, w, m, v)


def add_round(a, b, name, tr=512):
    R, C = a.shape
    tr = _pick(R, tr, 16)

    def body(a_ref, b_ref, o_ref):
        o_ref[...] = (a_ref[...] + b_ref[...].astype(F32)).astype(BF16)

    blk = pl.BlockSpec((tr, C), lambda i: (i, 0))
    return pl.pallas_call(
        body, grid=(R // tr,), in_specs=[blk, blk], out_specs=blk, out_shape=jax.ShapeDtypeStruct((R, C), BF16),
        compiler_params=_cparams("parallel"), name=name,
    )(a, b)


def sum_parts(parts, name, tr=512):
    P, R, C = parts.shape
    tr = _pick(R, tr, 16)

    def body(p_ref, o_ref):
        s = p_ref[0].astype(F32)
        for k in range(1, P):
            s = s + p_ref[k].astype(F32)
        o_ref[...] = s

    return pl.pallas_call(
        body, grid=(R // tr,), in_specs=[pl.BlockSpec((P, tr, C), lambda i: (0, i, 0))],
        out_specs=pl.BlockSpec((tr, C), lambda i: (i, 0)), out_shape=jax.ShapeDtypeStruct((R, C), F32),
        compiler_params=_cparams("parallel"), name=name,
    )(parts)


_ANY = pl.BlockSpec(memory_space=pl.ANY)


def _place():
    return lax.axis_index("x"), lax.axis_index("y"), lax.axis_index("c")


def _flip(v, bit):
    return 1 - v if bit else v


def chip_gather(buf, name):
    def body(in_ref, out_ref, send_sems, recv_sems):
        x, y, c = _place()
        me = 2 * x + y
        sends = []
        for k in range(1, N_CHIPS):
            px, py = _flip(x, k >> 1), _flip(y, k & 1)
            cp = pltpu.make_async_remote_copy(src_ref=in_ref, dst_ref=out_ref.at[me], send_sem=send_sems.at[k - 1],
                                              recv_sem=recv_sems.at[k - 1], device_id=(px, py, c), device_id_type=MESH)
            cp.start()
            sends.append(cp)
        for k in range(1, N_CHIPS):
            px, py = _flip(x, k >> 1), _flip(y, k & 1)
            pltpu.make_async_remote_copy(src_ref=in_ref, dst_ref=out_ref.at[2 * px + py], send_sem=send_sems.at[k - 1],
                                         recv_sem=recv_sems.at[k - 1], device_id=(px, py, c),
                                         device_id_type=MESH).wait_recv()
        for cp in sends:
            cp.wait_send()

    out = pl.pallas_call(
        body, in_specs=[_ANY], out_specs=_ANY,
        out_shape=jax.ShapeDtypeStruct((N_CHIPS,) + buf.shape, buf.dtype),
        scratch_shapes=[pltpu.SemaphoreType.DMA((N_CHIPS - 1,)), pltpu.SemaphoreType.DMA((N_CHIPS - 1,))],
        name=name,
    )(buf)
    return lax.dynamic_update_index_in_dim(out, buf, 2 * lax.axis_index("x") + lax.axis_index("y"), 0)


def chip_all_to_all(buf, name):
    def body(in_ref, out_ref, send_sems, recv_sems):
        x, y, c = _place()
        me = 2 * x + y
        sends = []
        for k in range(1, N_CHIPS):
            px, py = _flip(x, k >> 1), _flip(y, k & 1)
            cp = pltpu.make_async_remote_copy(src_ref=in_ref.at[2 * px + py], dst_ref=out_ref.at[me],
                                              send_sem=send_sems.at[k - 1], recv_sem=recv_sems.at[k - 1],
                                              device_id=(px, py, c), device_id_type=MESH)
            cp.start()
            sends.append(cp)
        for k in range(1, N_CHIPS):
            px, py = _flip(x, k >> 1), _flip(y, k & 1)
            pltpu.make_async_remote_copy(src_ref=in_ref.at[me], dst_ref=out_ref.at[2 * px + py],
                                         send_sem=send_sems.at[k - 1], recv_sem=recv_sems.at[k - 1],
                                         device_id=(px, py, c), device_id_type=MESH).wait_recv()
        for cp in sends:
            cp.wait_send()

    out = pl.pallas_call(
        body, in_specs=[_ANY], out_specs=_ANY, out_shape=jax.ShapeDtypeStruct(buf.shape, buf.dtype),
        scratch_shapes=[pltpu.SemaphoreType.DMA((N_CHIPS - 1,)), pltpu.SemaphoreType.DMA((N_CHIPS - 1,))],
        name=name,
    )(buf)
    me = 2 * lax.axis_index("x") + lax.axis_index("y")
    return lax.dynamic_update_index_in_dim(out, _index(buf, me), me, 0)


def core_gather(buf, name):
    def body(in_ref, out_ref, send_sem, recv_sem):
        x, y, c = _place()
        cp = pltpu.make_async_remote_copy(src_ref=in_ref, dst_ref=out_ref.at[c], send_sem=send_sem, recv_sem=recv_sem,
                                          device_id=(x, y, 1 - c), device_id_type=MESH)
        cp.start()
        pltpu.make_async_remote_copy(src_ref=in_ref, dst_ref=out_ref.at[1 - c], send_sem=send_sem, recv_sem=recv_sem,
                                     device_id=(x, y, 1 - c), device_id_type=MESH).wait_recv()
        cp.wait_send()

    out = pl.pallas_call(
        body, in_specs=[_ANY], out_specs=_ANY, out_shape=jax.ShapeDtypeStruct((2,) + buf.shape, buf.dtype),
        scratch_shapes=[pltpu.SemaphoreType.DMA, pltpu.SemaphoreType.DMA],
        name=name,
    )(buf)
    return lax.dynamic_update_index_in_dim(out, buf, lax.axis_index("c"), 0)


def core_swap(buf, name):
    def body(in_ref, out_ref, send_sem, recv_sem):
        x, y, c = _place()
        cp = pltpu.make_async_remote_copy(src_ref=in_ref, dst_ref=out_ref, send_sem=send_sem, recv_sem=recv_sem,
                                          device_id=(x, y, 1 - c), device_id_type=MESH)
        cp.start()
        cp.wait()

    return pl.pallas_call(
        body, in_specs=[_ANY], out_specs=_ANY, out_shape=jax.ShapeDtypeStruct(buf.shape, buf.dtype),
        scratch_shapes=[pltpu.SemaphoreType.DMA, pltpu.SemaphoreType.DMA],
        name=name,
    )(buf)


def device_gather(buf, name):
    def body(in_ref, out_ref, send_sems, recv_sems, local_sem):
        x, y, c = _place()
        me = 4 * x + 2 * y + c
        mine = pltpu.make_async_copy(in_ref, out_ref.at[me], local_sem)
        mine.start()
        sends = []
        for k in range(1, N_DEV):
            peer = (_flip(x, (k >> 2) & 1), _flip(y, (k >> 1) & 1), _flip(c, k & 1))
            cp = pltpu.make_async_remote_copy(src_ref=in_ref, dst_ref=out_ref.at[me], send_sem=send_sems.at[k - 1],
                                              recv_sem=recv_sems.at[k - 1], device_id=peer, device_id_type=MESH)
            cp.start()
            sends.append(cp)
        for k in range(1, N_DEV):
            peer = (_flip(x, (k >> 2) & 1), _flip(y, (k >> 1) & 1), _flip(c, k & 1))
            pltpu.make_async_remote_copy(src_ref=in_ref, dst_ref=out_ref.at[4 * peer[0] + 2 * peer[1] + peer[2]],
                                         send_sem=send_sems.at[k - 1], recv_sem=recv_sems.at[k - 1], device_id=peer,
                                         device_id_type=MESH).wait_recv()
        for cp in sends:
            cp.wait_send()
        mine.wait()

    return pl.pallas_call(
        body, in_specs=[_ANY], out_specs=_ANY, out_shape=jax.ShapeDtypeStruct((N_DEV,) + buf.shape, buf.dtype),
        scratch_shapes=[pltpu.SemaphoreType.DMA((N_DEV - 1,)), pltpu.SemaphoreType.DMA((N_DEV - 1,)),
                        pltpu.SemaphoreType.DMA],
        name=name,
    )(buf)


def _region(ref, chip_axis=None, chip=None, chip_size=None, half_axis=None, half=None, half_size=None):
    idx = [slice(None)] * len(ref.shape)
    if chip is not None:
        idx[chip_axis] = pl.ds(chip * chip_size, chip_size)
    if half is not None:
        idx[half_axis] = pl.ds(half * half_size, half_size)
    return ref.at[tuple(idx)]


def gather_weights(shards, axes, name, after=()):
    n = len(shards)

    def full_shape(t):
        shp = list(shards[t].shape)
        shp[axes[t][0]] *= N_CHIPS
        return tuple(shp)

    def body(*refs):
        ins, outs = refs[:n], refs[n + len(after):2 * n + len(after)]
        ici_send, ici_recv, d2d_send, d2d_recv, own_send, own_recv = refs[2 * n + len(after):]
        x, y, c = _place()
        me = 2 * x + y

        def part(t, ref, chip, half):
            ca, ha = axes[t]
            return _region(ref, ca, chip, ins[t].shape[ca], ha, half, ins[t].shape[ha] // 2)

        def own(t):
            return pltpu.make_async_remote_copy(src_ref=ins[t], dst_ref=part(t, outs[t], me, None),
                                                send_sem=own_send.at[t], recv_sem=own_recv.at[t],
                                                device_id=(x, y, 1 - c), device_id_type=MESH)

        started = []
        for t in range(n):
            own(t).start()
            started.append(own(t))
        for t in range(n):
            for k in range(1, N_CHIPS):
                px, py = _flip(x, k >> 1), _flip(y, k & 1)
                cp = pltpu.make_async_remote_copy(src_ref=part(t, ins[t], None, c), dst_ref=part(t, outs[t], me, c),
                                                  send_sem=ici_send.at[t, k - 1], recv_sem=ici_recv.at[t, k - 1],
                                                  device_id=(px, py, c), device_id_type=MESH)
                cp.start()
                started.append(cp)
        for t in range(n):
            for k in range(1, N_CHIPS):
                px, py = _flip(x, k >> 1), _flip(y, k & 1)
                got = part(t, outs[t], 2 * px + py, c)
                pltpu.make_async_remote_copy(src_ref=part(t, ins[t], None, c), dst_ref=got,
                                             send_sem=ici_send.at[t, k - 1], recv_sem=ici_recv.at[t, k - 1],
                                             device_id=(px, py, c), device_id_type=MESH).wait_recv()
                fw = pltpu.make_async_remote_copy(src_ref=got, dst_ref=got, send_sem=d2d_send.at[t, k - 1],
                                                  recv_sem=d2d_recv.at[t, k - 1], device_id=(x, y, 1 - c),
                                                  device_id_type=MESH)
                fw.start()
                started.append(fw)
        for t in range(n):
            for k in range(1, N_CHIPS):
                px, py = _flip(x, k >> 1), _flip(y, k & 1)
                theirs = part(t, outs[t], 2 * px + py, 1 - c)
                pltpu.make_async_remote_copy(src_ref=theirs, dst_ref=theirs, send_sem=d2d_send.at[t, k - 1],
                                             recv_sem=d2d_recv.at[t, k - 1], device_id=(x, y, 1 - c),
                                             device_id_type=MESH).wait_recv()
        for t in range(n):
            own(t).wait_recv()
        for cp in started:
            cp.wait_send()

    sem = pltpu.SemaphoreType.DMA((n, N_CHIPS - 1))
    own_sem = pltpu.SemaphoreType.DMA((n,))
    return pl.pallas_call(
        body, in_specs=[_ANY] * (n + len(after)), out_specs=[_ANY] * n,
        out_shape=[jax.ShapeDtypeStruct(full_shape(t), shards[t].dtype) for t in range(n)],
        scratch_shapes=[sem, sem, sem, sem, own_sem, own_sem], name=name,
    )(*shards, *after)


_HBM = pl.BlockSpec(memory_space=pltpu.HBM)
_SEM = pl.BlockSpec(memory_space=pltpu.SEMAPHORE)
_EFFECT = pltpu.SideEffectType.DATAFLOW_SIDE_EFFECTING
WEIGHT_COPIES = N_CHIPS


def _weight_peer(k, x, y, c):
    return (x, y, 1 - c) if k == 0 else (_flip(x, k >> 1), _flip(y, k & 1), c)


def weights_start(shards, items, name, after=()):
    n_sh, n_it = len(shards), len(items)

    def src_of(refs, i):
        t, layer, _ = items[i]
        return refs[t] if layer is None else refs[t].at[layer]

    def land_shape(i):
        t, layer, ca = items[i]
        shp = list(shards[t].shape if layer is None else shards[t].shape[1:])
        shp[ca] *= N_CHIPS
        return tuple(shp)

    def body(*refs):
        shard_refs, land_refs = refs[:n_sh], refs[n_sh:n_sh + n_it]
        first_out = n_sh + n_it + len(after)
        send_sems = refs[first_out:first_out + n_it]
        recv_sems = refs[first_out + n_it:first_out + 2 * n_it]
        token = refs[-1]
        x, y, c = _place()
        me = 2 * x + y
        for i in range(n_it):
            src = src_of(shard_refs, i)
            ca = items[i][2]
            dst = _region(land_refs[i], ca, me, src.shape[ca])
            for k in range(WEIGHT_COPIES):
                pltpu.make_async_remote_copy(src_ref=src, dst_ref=dst, send_sem=send_sems[i], recv_sem=recv_sems[i],
                                             device_id=_weight_peer(k, x, y, c), device_id_type=MESH).start()
        token[...] = jnp.zeros_like(token)

    lands = [pltpu.with_memory_space_constraint(lax.empty(land_shape(i), shards[0].dtype), pltpu.HBM)
             for i in range(n_it)]
    ins = [pltpu.with_memory_space_constraint(a, pltpu.HBM) for a in shards] + lands
    sems = (pltpu.SemaphoreType.DMA(()),) * (2 * n_it)
    outs = pl.pallas_call(
        body, name=name,
        out_shape=sems + tuple(pltpu.HBM(a.shape, a.dtype) for a in ins) + (jax.ShapeDtypeStruct((8, LANES), F32),),
        in_specs=[_HBM] * len(ins) + [_ANY] * len(after),
        out_specs=(_SEM,) * (2 * n_it) + (_HBM,) * len(ins) + (pl.BlockSpec(memory_space=pltpu.VMEM),),
        input_output_aliases={i: 2 * n_it + i for i in range(len(ins))},
        compiler_params=pltpu.CompilerParams(has_side_effects=_EFFECT),
    )(*ins, *after)
    base = 2 * n_it
    return (list(outs[:n_it]), list(outs[n_it:base]), list(outs[base:base + n_sh]),
            list(outs[base + n_sh:base + n_sh + n_it]), outs[-1])


def weights_wait(send_sems, recv_sems, lands, after, keep, name):
    m = len(lands)

    def body(*refs):
        land_refs, send_refs, recv_refs = refs[:m], refs[m:2 * m], refs[2 * m:3 * m]
        x, y, c = _place()
        for j in range(m):
            cp = pltpu.make_async_remote_copy(src_ref=land_refs[j], dst_ref=land_refs[j], send_sem=send_refs[j],
                                              recv_sem=recv_refs[j], device_id=(x, y, 1 - c),
                                              device_id_type=MESH)
            cp.wait_send()
            cp.wait_recv()

    outs = pl.pallas_call(
        body, name=name,
        out_shape=tuple(pltpu.HBM(a.shape, a.dtype) for a in lands),
        in_specs=[_HBM] * m + [_SEM] * (2 * m) + [_ANY] + [_HBM] * len(keep),
        out_specs=(_HBM,) * m,
        input_output_aliases={j: j for j in range(m)},
        compiler_params=pltpu.CompilerParams(has_side_effects=_EFFECT),
    )(*lands, *send_sems, *recv_sems, after, *keep)
    return list(outs)


def reduce_to_sibling(lo, hi, name):
    n = len(lo)

    def body(*refs):
        los, his, outs = refs[:n], refs[n:2 * n], refs[2 * n:3 * n]
        send_sems, recv_sems = refs[3 * n:]
        x, y, c = _place()

        def copy(u, src):
            return pltpu.make_async_remote_copy(src_ref=src, dst_ref=outs[u], send_sem=send_sems.at[u],
                                                recv_sem=recv_sems.at[u], device_id=(x, y, 1 - c), device_id_type=MESH)

        for u in range(n):
            @pl.when(c == 0)
            def _(u=u):
                copy(u, his[u]).start()

            @pl.when(c == 1)
            def _(u=u):
                copy(u, los[u]).start()
        for u in range(n):
            copy(u, los[u]).wait_recv()
        for u in range(n):
            copy(u, los[u]).wait_send()

    return pl.pallas_call(
        body, in_specs=[_ANY] * (2 * n), out_specs=[_ANY] * n,
        out_shape=[jax.ShapeDtypeStruct(a.shape, a.dtype) for a in lo],
        scratch_shapes=[pltpu.SemaphoreType.DMA((n,)), pltpu.SemaphoreType.DMA((n,))], name=name,
    )(*lo, *hi)


def add_selected(lo, hi, other, name, tile_elems=1 << 19):
    R, C = lo.shape
    tr = _pick(R, max(16, tile_elems // C // 16 * 16), 16)

    def body(lo_ref, hi_ref, o_ref, out_ref):
        mine = jnp.where(lax.axis_index("c") == 0, lo_ref[...].astype(F32), hi_ref[...].astype(F32))
        out_ref[...] = (mine + o_ref[...].astype(F32)).astype(out_ref.dtype)

    blk = pl.BlockSpec((tr, C), lambda i: (i, 0))
    return pl.pallas_call(
        body, grid=(R // tr,), in_specs=[blk, blk, blk], out_specs=blk, out_shape=jax.ShapeDtypeStruct((R, C), BF16),
        compiler_params=_cparams("parallel"), name=name,
    )(lo, hi, other)


def scatter_to_chips(pieces, chip_axes, name):
    n = len(pieces)

    def block_shape(u):
        shp = list(pieces[u].shape)
        shp[chip_axes[u]] //= N_CHIPS
        return tuple(shp)

    def body(*refs):
        ins, outs = refs[:n], refs[n:2 * n]
        send_sems, recv_sems = refs[2 * n:]
        x, y, c = _place()
        me = 2 * x + y
        started = []
        for u in range(n):
            size = block_shape(u)[chip_axes[u]]
            for k in range(1, N_CHIPS):
                px, py = _flip(x, k >> 1), _flip(y, k & 1)
                cp = pltpu.make_async_remote_copy(src_ref=_region(ins[u], chip_axes[u], 2 * px + py, size),
                                                  dst_ref=outs[u].at[me], send_sem=send_sems.at[u, k - 1],
                                                  recv_sem=recv_sems.at[u, k - 1], device_id=(px, py, c),
                                                  device_id_type=MESH)
                cp.start()
                started.append(cp)
        for u in range(n):
            size = block_shape(u)[chip_axes[u]]
            for k in range(1, N_CHIPS):
                px, py = _flip(x, k >> 1), _flip(y, k & 1)
                pltpu.make_async_remote_copy(src_ref=_region(ins[u], chip_axes[u], me, size),
                                             dst_ref=outs[u].at[2 * px + py], send_sem=send_sems.at[u, k - 1],
                                             recv_sem=recv_sems.at[u, k - 1], device_id=(px, py, c),
                                             device_id_type=MESH).wait_recv()
        for cp in started:
            cp.wait_send()

    sem = pltpu.SemaphoreType.DMA((n, N_CHIPS - 1))
    return pl.pallas_call(
        body, in_specs=[_ANY] * n, out_specs=[_ANY] * n,
        out_shape=[jax.ShapeDtypeStruct((N_CHIPS,) + block_shape(u), pieces[u].dtype) for u in range(n)],
        scratch_shapes=[sem, sem], name=name,
    )(*pieces)


def scatter_start(pieces, chip_axes, name, after=()):
    n = len(pieces)

    def block_shape(u):
        shp = list(pieces[u].shape)
        shp[chip_axes[u]] //= N_CHIPS
        return tuple(shp)

    def body(*refs):
        ins, land_refs = refs[:n], refs[n:2 * n]
        first_out = 2 * n + len(after)
        send_sems, recv_sems = refs[first_out:first_out + n], refs[first_out + n:first_out + 2 * n]
        token = refs[-1]
        x, y, c = _place()
        me = 2 * x + y
        for u in range(n):
            size = block_shape(u)[chip_axes[u]]
            for k in range(1, N_CHIPS):
                px, py = _flip(x, k >> 1), _flip(y, k & 1)
                pltpu.make_async_remote_copy(src_ref=_region(ins[u], chip_axes[u], 2 * px + py, size),
                                             dst_ref=land_refs[u].at[me], send_sem=send_sems[u], recv_sem=recv_sems[u],
                                             device_id=(px, py, c), device_id_type=MESH).start()
        token[...] = jnp.zeros_like(token)

    lands = [pltpu.with_memory_space_constraint(lax.empty((N_CHIPS,) + block_shape(u), pieces[u].dtype), pltpu.HBM)
             for u in range(n)]
    ins = [pltpu.with_memory_space_constraint(a, pltpu.HBM) for a in pieces] + lands
    sems = (pltpu.SemaphoreType.DMA(()),) * (2 * n)
    outs = pl.pallas_call(
        body, name=name,
        out_shape=sems + tuple(pltpu.HBM(a.shape, a.dtype) for a in ins) + (jax.ShapeDtypeStruct((8, LANES), F32),),
        in_specs=[_HBM] * len(ins) + [_ANY] * len(after),
        out_specs=(_SEM,) * (2 * n) + (_HBM,) * len(ins) + (pl.BlockSpec(memory_space=pltpu.VMEM),),
        input_output_aliases={i: 2 * n + i for i in range(len(ins))},
        compiler_params=pltpu.CompilerParams(has_side_effects=_EFFECT),
    )(*ins, *after)
    return list(outs[:n]), list(outs[n:2 * n]), list(outs[2 * n:3 * n]), list(outs[3 * n:4 * n]), outs[-1]


def scatter_wait(send_sems, recv_sems, lands, pieces, after, name):
    n = len(lands)

    def body(*refs):
        land_refs, send_refs, recv_refs = refs[:n], refs[n:2 * n], refs[2 * n:3 * n]
        x, y, c = _place()
        for u in range(n):
            three = land_refs[u].at[pl.ds(0, N_CHIPS - 1)]
            cp = pltpu.make_async_remote_copy(src_ref=three, dst_ref=three, send_sem=send_refs[u], recv_sem=recv_refs[u],
                                              device_id=(x, y, 1 - c), device_id_type=MESH)
            cp.wait_send()
            cp.wait_recv()

    outs = pl.pallas_call(
        body, name=name,
        out_shape=tuple(pltpu.HBM(a.shape, a.dtype) for a in lands),
        in_specs=[_HBM] * n + [_SEM] * (2 * n) + [_ANY] + [_HBM] * len(pieces),
        out_specs=(_HBM,) * n,
        input_output_aliases={j: j for j in range(n)},
        compiler_params=pltpu.CompilerParams(has_side_effects=_EFFECT),
    )(*lands, *send_sems, *recv_sems, after, *pieces)
    return list(outs)


def gather_halves(parts, slots, out_shapes, name):
    n = len(parts)

    def body(*refs):
        ins, outs = refs[:n], refs[n:n + len(out_shapes)]
        send_sems, recv_sems = refs[n + len(out_shapes):]
        x, y, c = _place()
        started = []
        for u in range(n):
            t, s = slots[u]
            cp = pltpu.make_async_remote_copy(src_ref=ins[u], dst_ref=outs[t].at[s, c], send_sem=send_sems.at[u],
                                              recv_sem=recv_sems.at[u], device_id=(x, y, 1 - c), device_id_type=MESH)
            cp.start()
            started.append(cp)
        for u in range(n):
            t, s = slots[u]
            pltpu.make_async_remote_copy(src_ref=ins[u], dst_ref=outs[t].at[s, 1 - c], send_sem=send_sems.at[u],
                                         recv_sem=recv_sems.at[u], device_id=(x, y, 1 - c),
                                         device_id_type=MESH).wait_recv()
        for cp in started:
            cp.wait_send()

    return pl.pallas_call(
        body, in_specs=[_ANY] * n, out_specs=[_ANY] * len(out_shapes),
        out_shape=[jax.ShapeDtypeStruct(shp, F32) for shp in out_shapes],
        scratch_shapes=[pltpu.SemaphoreType.DMA((n,)), pltpu.SemaphoreType.DMA((n,))], name=name,
    )(*parts)


WEIGHT_ORDER = ["mod_w", "mod_b", "norm1_g", "norm2_g", "pool_w", "pool_b", "pool_scale", "kv_in_g", "w_dkv",
                "ckv_norm_g", "w_uk", "w_uv", "w_dq", "q_norm_g", "w_uq", "w_o", "w_up", "conv_w", "conv_b", "w_down",
                "final_g"]
EXCHANGED = {"w_up": (2, 0), "w_down": (1, 0), "w_o": (1, 0), "w_uq": (2, 0), "w_dq": (1, 0), "pool_w": (2, 0),
             "w_dkv": (0, 1), "w_uk": (1, 0), "w_uv": (1, 0)}
SMALL_SHARDED = {"conv_w": 2, "pool_b": 1, "pool_scale": 1}
REPLICATED = ["mod_b", "norm1_g", "norm2_g", "kv_in_g", "ckv_norm_g", "q_norm_g", "conv_b", "final_g"]


def _padded(n, align):
    return -(-n // align) * align


def _flat_pad(parts, total):
    flat = jnp.concatenate(parts, axis=-1)
    pad = total - flat.shape[-1]
    if pad:
        flat = jnp.concatenate([flat, jnp.zeros(flat.shape[:-1] + (pad,), flat.dtype)], axis=-1)
    return flat


def _split_shards(full, axis):
    shp = full.shape
    t = full.reshape(shp[:axis] + (N_CHIPS, shp[axis] // N_CHIPS) + shp[axis + 1:])
    return jnp.moveaxis(t, axis, 0).reshape(N_CHIPS, -1)


def _join_shards(rows, shard_shape, axis):
    t = jnp.moveaxis(rows.reshape((N_CHIPS,) + tuple(shard_shape)), 0, axis)
    return t.reshape(tuple(shard_shape[:axis]) + (N_CHIPS * shard_shape[axis],) + tuple(shard_shape[axis + 1:]))


def _index(a, i, axis=0):
    return lax.dynamic_index_in_dim(a, i, axis, keepdims=False)


def kernel(x, c, positions, mod_w, mod_b, norm1_g, norm2_g, pool_w, pool_b, pool_scale, kv_in_g, w_dkv, ckv_norm_g, w_uk, w_uv, w_dq, q_norm_g, w_uq, w_o, w_up, conv_w, conv_b, w_down, final_g, loss_target, m_mod_w, m_mod_b, m_norm1_g, m_norm2_g, m_pool_w, m_pool_b, m_pool_scale, m_kv_in_g, m_w_dkv, m_ckv_norm_g, m_w_uk, m_w_uv, m_w_dq, m_q_norm_g, m_w_uq, m_w_o, m_w_up, m_conv_w, m_conv_b, m_w_down, m_final_g, v_mod_w, v_mod_b, v_norm1_g, v_norm2_g, v_pool_w, v_pool_b, v_pool_scale, v_kv_in_g, v_w_dkv, v_ckv_norm_g, v_w_uk, v_w_uv, v_w_dq, v_q_norm_g, v_w_uq, v_w_o, v_w_up, v_conv_w, v_conv_b, v_w_down, v_final_g):
    given = dict(locals())
    W = {n: given[n] for n in WEIGHT_ORDER}
    M1 = {n: given["m_" + n] for n in WEIGHT_ORDER}
    V2 = {n: given["v_" + n] for n in WEIGHT_ORDER}
    xi, yi, ci = lax.axis_index("x"), lax.axis_index("y"), lax.axis_index("c")
    chip = 2 * xi + yi
    dev = 4 * xi + 2 * yi + ci
    x0 = x[0]
    S_, D = x0.shape
    Fh = conv_b.shape[1]
    E = mod_b.shape[1]
    Es = E // N_CHIPS
    zD = jnp.zeros((D,), F32)

    c_all = device_gather(c, "gather_c").reshape(N_DEV, D)
    c_pad = jnp.concatenate([c_all, jnp.zeros((16 - N_DEV, D), F32)], axis=0)
    mod_b_mine = lax.dynamic_slice_in_dim(mod_b, chip * Es, Es, axis=1)
    mods_part = mods_fwd(c_pad, mod_w, mod_b_mine, "mods_fwd")
    mods_all = chip_gather(mods_part, "gather_mods")
    mods = jnp.swapaxes(_index(mods_all, dev, axis=2), 0, 1).reshape(DEPTH, E)
    mod = [[mods[l, k * D:(k + 1) * D] for k in range(6)] for l in range(DEPTH)]

    full = {}
    ssz = {n: math.prod(W[n].shape) for n in SMALL_SHARDED}
    Tw = _padded(sum(ssz.values()), 8 * PACK_COLS)
    small_rows = chip_gather(_flat_pad([W[n].reshape(-1) for n in SMALL_SHARDED], Tw).reshape(-1, PACK_COLS),
                             "gather_small_w").reshape(N_CHIPS, Tw)
    off = 0
    for n, axis in SMALL_SHARDED.items():
        full[n] = _join_shards(small_rows[:, off:off + ssz[n]], W[n].shape, axis)
        off += ssz[n]

    names = list(EXCHANGED)
    shards = [W[n].astype(BF16) for n in names]
    n_mla = DEPTH - N_A
    first_axes = {"w_up": (1, 0), "w_down": (0, 1), "pool_w": (1, 0)}
    first = gather_weights([shards[names.index(n)][0] for n in first_axes], list(first_axes.values()), "gather_weights0",
                           after=[mods, small_rows])
    for n, arr in zip(first_axes, first):
        full[(n, 0)] = arr
    items, groups = [], []

    def group(entries):
        groups.append(list(range(len(items), len(items) + len(entries))))
        for n, layer in entries:
            ca = EXCHANGED[n][0] - (0 if layer is None else 1)
            items.append((names.index(n), layer, 0 if n == "w_dkv" else ca))

    for l in range(1, N_A):
        group([("w_up", l), ("w_down", l), ("pool_w", l)])
    for j in range(n_mla):
        head = [("w_dkv", None), ("w_uk", None), ("w_uv", None)] if j == 0 else []
        group(head + [("w_dq", j), ("w_uq", j), ("w_o", j), ("w_up", N_A + j), ("w_down", N_A + j)])
    w_send, w_recv, shards_thru, lands, _ = weights_start(shards, items, "weights_start", after=first)

    def weights_ready(g, after):
        keep = shards_thru if g == len(groups) - 1 else []
        got = weights_wait([w_send[i] for i in groups[g]], [w_recv[i] for i in groups[g]], [lands[i] for i in groups[g]],
                           after, keep, f"weights_wait{g}")
        for i, arr in zip(groups[g], got):
            t, layer, _ = items[i]
            full[(names[t], 0 if layer is None else layer)] = arr

    q_rank = W["w_uq"].shape[1]
    kv_w = KV_RANK + QK_ROPE

    def uq_ext(j):
        wq = full[("w_uq", j)].reshape(q_rank, N_HEADS, QK_HEAD)
        return jnp.concatenate([wq, jnp.zeros((q_rank, N_HEADS, HEAD_PAD - QK_HEAD), BF16)],
                               axis=2).reshape(q_rank, N_HEADS * HEAD_PAD)


    half = QK_ROPE // 2
    inv = 1.0 / (ROPE_THETA ** (jnp.arange(0, QK_ROPE, 2, dtype=F32) / QK_ROPE))
    inv_row = jnp.concatenate([inv, inv, jnp.zeros((LANES - 2 * half,), F32)]).reshape(1, LANES)
    tabs = rope_tables(positions[0].astype(F32).reshape(S_, 1), inv_row, "rope_tables")
    att_scale = QK_HEAD ** -0.5

    saved = []
    xcur = x0
    kv_saved = None
    K = VX = knv = None
    for l in range(DEPTH):
        sh1, sc1, g1, sh2, sc2, g2 = mod[l]
        st = {"xin": xcur}
        if l:
            weights_ready(l - 1, xcur)
        if l == N_A:
            w_dkv_ext = jnp.concatenate([full[("w_dkv", 0)], jnp.zeros((D, KV_RANK + LANES - kv_w), BF16)], axis=1)
            w_ukv = jnp.concatenate([full[("w_uk", 0)], full[("w_uv", 0)]], axis=1)
            xn = norm_fwd(xcur, kv_in_g, zD, zD, BF16, "kvin_fwd")
            kv_ext = mm(xn, w_dkv_ext, "nn", F32, "dkv_mm")
            lat = kv_ext[:, :KV_RANK]
            zk = jnp.zeros((KV_RANK,), F32)
            ckv = norm_fwd(lat, ckv_norm_g, zk, zk, BF16, "ckv_fwd")
            K, VX = kv_proj(ckv, w_ukv, kv_ext, tabs, "ukv_mm")
            kv_saved = {"x": xcur, "xn": xn, "lat": lat, "ckv": ckv}
        if l < N_A:
            h1 = norm_fwd(xcur, norm1_g[l], sc1, sh1, F32, f"norm1_fwd{l}")
            st["pooled"] = _pool_call(h1, BF16, f"pool_fwd{l}", False)
            st["cs"] = g1 * full["pool_scale"][l]
            st["ypre"], xmid = gmm(st["pooled"], full[("pool_w", l)], "nn", BF16, f"pool_mm{l}", bias=full["pool_b"][l],
                                   res=xcur, colscale=st["cs"])
        else:
            j = l - N_A
            st["h1"] = norm_fwd(xcur, norm1_g[l], sc1, sh1, BF16, f"norm1_fwd{l}")
            st["ql"] = mm(st["h1"], full[("w_dq", j)], "nn", F32, f"dq_mm{l}")
            st["cq"] = norm_fwd(st["ql"], q_norm_g[j], jnp.zeros_like(q_norm_g[j]), jnp.zeros_like(q_norm_g[j]), BF16,
                                f"qnorm_fwd{l}")
            st["w_uq_ext"] = uq_ext(j)
            st["Q"] = q_proj(st["cq"], st["w_uq_ext"], tabs, att_scale, f"uq_mm{l}")
            st["o"], lse = attn_fwd(st["Q"], K, VX, f"attn_fwd{l}")
            st["lse"] = lse.reshape(N_HEADS, 1, S_)
            st["y"], xmid = mm(st["o"], full[("w_o", j)], "nn", BF16, f"wo_mm{l}", res=xcur, colscale=g1)
        st["xmid"] = xmid
        st["h2"] = norm_fwd(xmid, norm2_g[l], sc2, sh2, BF16, f"norm2_fwd{l}")
        st["u"] = mm(st["h2"], full[("w_up", l)], "nn", BF16, f"up_mm{l}")
        st["z"] = glu_fwd(st["u"], full["conv_w"][l], conv_b[l], f"glu_fwd{l}")
        st["f"], xcur = mm(st["z"], full[("w_down", l)], "nn", BF16, f"down_mm{l}", tk=1408, res=xmid, colscale=g2)
        saved.append(st)

    dx, d_final_g, loss_part = loss_head(xcur, final_g, loss_target[0], "loss_head")
    loss = lax.psum(loss_part[0, 0], ("x", "y", "c"))

    def begin_reduce(tensors, first_slot, tag):
        units = []
        for n in tensors:
            ca = EXCHANGED[n][0]
            if W[n].ndim > 2:
                n_slots = W[n].shape[0] // 2
                for sl in range(first_slot if n_slots > 1 else 0, first_slot + 1 if n_slots > 1 else 1):
                    units.append((n, sl, G[(n, 2 * sl)], G[(n, 2 * sl + 1)], ca - 1))
            elif n == "w_dkv":
                g4 = G[(n, 0)].reshape(N_CHIPS, 2, -1, kv_w)
                units.append((n, 0, g4[:, 0], g4[:, 1], 0))
            else:
                rows_half = W[n].shape[0] // 2
                units.append((n, 0, G[(n, 0)][:rows_half], G[(n, 0)][rows_half:], ca))
        lo = [u[2] for u in units]
        hi = [u[3] for u in units]
        theirs = reduce_to_sibling(lo, hi, f"reduce_cores_{tag}")
        sums = [add_selected(l_.reshape(-1, l_.shape[-1]), h_.reshape(-1, l_.shape[-1]), t_.reshape(-1, l_.shape[-1]),
                             f"reduce_cores_add_{tag}{i}").reshape(l_.shape)
                for i, (l_, h_, t_) in enumerate(zip(lo, hi, theirs))]
        return units, sums, [u[4] for u in units]

    G = {}
    dmods = [None] * DEPTH
    d_norm1 = [None] * DEPTH
    d_norm2 = [None] * DEPTH
    d_conv_b = [None] * DEPTH
    d_qnorm = [None] * n_mla
    dkv_acc = []
    df, a2, _ = gate_bwd(dx, saved[DEPTH - 1]["f"], mod[DEPTH - 1][5], f"gate2_bwd{DEPTH - 1}")
    for l in reversed(range(DEPTH)):
        sh1, sc1, g1, sh2, sc2, g2 = mod[l]
        st = saved[l]
        next_gate = (saved[l - 1]["f"], mod[l - 1][5]) if l else None
        dz = mm(df, full[("w_down", l)], "nt", BF16, f"down_dx{l}")
        G[("w_down", l)] = mm(st["z"], df, "tn", BF16, f"down_dw{l}")
        du, dcw, dcb = glu_bwd(st["u"], dz, full["conv_w"][l], conv_b[l], f"glu_bwd{l}")
        G[("conv_w", l)] = dcw
        d_conv_b[l] = dcb[0]
        dh2 = mm(du, full[("w_up", l)], "nt", BF16, f"up_dx{l}", tk=1408)
        G[("w_up", l)] = mm(st["h2"], du, "tn", BF16, f"up_dw{l}")
        dxmid, s1, s2, dgate, a1, csum = norm_bwd(st["xmid"], norm2_g[l], sc2, dh2, dx, f"norm2_bwd{l}",
                                                  gate=(st["ypre"], st["cs"]) if l < N_A else (st["y"], g1))
        dsh2, dsc2, d_norm2[l] = s1[0], s2[0] * norm2_g[l], s2[0] * (1.0 + sc2)
        if l < N_A:
            dyp = dgate
            dg1 = full["pool_scale"][l] * a1[0]
            G[("pool_scale", l)] = g1 * a1[0]
            G[("pool_b", l)] = st["cs"] * csum[0]
            dpooled = gmm(dyp, full[("pool_w", l)], "nt", F32, f"pool_dx{l}")
            G[("pool_w", l)] = gmm(st["pooled"], dyp, "tn", BF16, f"pool_dw{l}")
            dh1 = _pool_call(dpooled, F32, f"pool_bwd{l}", True)
        else:
            j = l - N_A
            dy = dgate
            dg1 = a1[0]
            do, delta = o_proj_bwd(dy, full[("w_o", j)], st["o"], f"wo_dx{l}")
            delta = delta.reshape(N_HEADS, 1, S_)
            G[("w_o", j)] = mm(st["o"], dy, "tn", BF16, f"wo_dw{l}")
            dQ, dK, dV = attn_bwd(st["Q"], K, VX, do, st["lse"], delta, f"attn_bwd{l}")
            dkv_acc.append((dK, dV))
            dcq, dw_ext = q_proj_bwd(dQ, st["cq"], st["w_uq_ext"], tabs, att_scale, f"uq_bwd{l}")
            G[("w_uq", j)] = dw_ext.reshape(q_rank, N_HEADS, HEAD_PAD)[:, :, :QK_HEAD].reshape(q_rank, N_HEADS * QK_HEAD)
            zq = jnp.zeros_like(q_norm_g[j])
            dql, _, s2q = norm_bwd(st["ql"], q_norm_g[j], zq, dcq, None, f"qnorm_bwd{l}")
            d_qnorm[j] = s2q[0]
            dh1 = mm(dql, full[("w_dq", j)], "nt", BF16, f"dq_dx{l}")
            G[("w_dq", j)] = mm(st["h1"], dql, "tn", BF16, f"dq_dw{l}")
        a2_mine = a2
        if l and l != N_A:
            dx, s1, s2, df, a2, _ = norm_bwd(st["xin"], norm1_g[l], sc1, dh1, dxmid, f"norm1_bwd{l}", gate=next_gate)
        else:
            dx, s1, s2 = norm_bwd(st["xin"], norm1_g[l], sc1, dh1, dxmid, f"norm1_bwd{l}")
        dsh1, dsc1, d_norm1[l] = s1[0], s2[0] * norm1_g[l], s2[0] * (1.0 + sc1)
        dmods[l] = jnp.concatenate([dsh1, dsc1, dg1, dsh2, dsc2, a2_mine[0]])
        if l == N_A:
            (dk_a, dv_a), (dk_b, dv_b) = dkv_acc
            dknv, d_tk = k_prep_bwd(dk_a, dk_b, dv_a, dv_b, tabs, "k_prep_bwd")
            dckv = mm(dknv, w_ukv, "nt", F32, "ukv_dx")
            d_ukv = mm(kv_saved["ckv"], dknv, "tn", BF16, "ukv_dw")
            G[("w_uk", 0)], G[("w_uv", 0)] = d_ukv[:, :N_HEADS * QK_NOPE], d_ukv[:, N_HEADS * QK_NOPE:]
            zk = jnp.zeros((KV_RANK,), F32)
            dlat, _, s2c = norm_bwd(kv_saved["lat"], ckv_norm_g, zk, dckv, None, "ckv_bwd")
            d_ckv_g = s2c[0]
            dkv_ext = jnp.concatenate([dlat, d_tk], axis=1)
            dxn = mm(dkv_ext, w_dkv_ext, "nt", BF16, "dkv_dx")
            G[("w_dkv", 0)] = mm(kv_saved["xn"], dkv_ext, "tn", BF16, "dkv_dw")[:, :kv_w]
            dx, _, s2k, df, a2, _ = norm_bwd(kv_saved["x"], kv_in_g, zD, dxn, dx, "kvin_bwd", gate=next_gate)
            d_kvin_g = s2k[0]
            e_units, e_sums, e_axes = begin_reduce([n for n in EXCHANGED if n != "pool_w"], 1, "early")
            e_send, e_recv, e_pieces, e_lands, e_token = scatter_start(e_sums, e_axes, "reduce_chips_start")
            early = (e_units, e_send, e_recv, e_lands, e_pieces, e_axes)
            mod[l - 1][4] = mod[l - 1][4] + e_token[0, 0]

    small = {"mod_b": jnp.stack(dmods), "norm1_g": jnp.stack(d_norm1), "norm2_g": jnp.stack(d_norm2),
             "kv_in_g": d_kvin_g, "ckv_norm_g": d_ckv_g, "q_norm_g": jnp.stack(d_qnorm),
             "conv_b": jnp.stack(d_conv_b), "final_g": d_final_g[0]}
    extra = {n: jnp.stack([G[(n, i)] for i in range(W[n].shape[0])]) for n in SMALL_SHARDED}
    ssizes = {n: math.prod(W[n].shape) for n in REPLICATED}
    esizes = {n: math.prod(extra[n].shape) for n in SMALL_SHARDED}
    Ts = _padded(sum(ssizes.values()) + sum(esizes.values()), 8 * PACK_COLS)

    def pack_small(d, tail=()):
        return _flat_pad([d[n].reshape(-1) for n in REPLICATED] + [t.reshape(-1) for t in tail],
                         Ts).reshape(Ts // PACK_COLS, PACK_COLS)

    parts = device_gather(pack_small(small, [extra[n] for n in SMALL_SHARDED]), "gather_small")

    l_units, l_sums, l_axes = begin_reduce([n for n in EXCHANGED if W[n].ndim > 2 and W[n].shape[0] == DEPTH] + ["pool_w"],
                                           0, "late")
    l_send, l_recv, l_pieces, l_lands, l_token = scatter_start(l_sums, l_axes, "reduce_chips_late_start", after=[parts])
    parts = parts + l_token[0, 0]

    grads, deltas, new_m, new_v = {}, {}, {}, {}
    outs = adamw_sum(parts, pack_small(W), pack_small(M1), pack_small(V2), "adamw_small")
    off = 0
    for n in REPLICATED:
        for dst, o in zip((grads, deltas, new_m, new_v), outs):
            dst[n] = o.reshape(-1)[off:off + ssizes[n]].reshape(W[n].shape)
        off += ssizes[n]
    for n, axis in SMALL_SHARDED.items():
        g_full = outs[0].reshape(-1)[off:off + esizes[n]].reshape(extra[n].shape)
        off += esizes[n]
        size = W[n].shape[axis]
        grads[n] = lax.dynamic_slice_in_dim(g_full, chip * size, size, axis=axis)
        deltas[n], new_m[n], new_v[n] = adamw(W[n], grads[n], M1[n], V2[n], f"adamw_{n}")

    dm_all = parts.reshape(N_DEV, -1)[:, :DEPTH * E].reshape(N_DEV, DEPTH, E)
    dm_mine = jnp.swapaxes(lax.dynamic_slice_in_dim(dm_all, chip * Es, Es, axis=2), 0, 1)
    grads["mod_w"], deltas["mod_w"], new_m["mod_w"], new_v["mod_w"] = adamw_modw(
        c_all.reshape(N_DEV, D, 1), dm_mine, mod_w, m_mod_w, v_mod_w, "adamw_mod_w")

    def finish_reduce(pieces, axes, got, tag):
        out = []
        for i, (sm, ax, g4) in enumerate(zip(pieces, axes, got)):
            size = sm.shape[ax] // N_CHIPS
            g4 = lax.dynamic_update_index_in_dim(g4, lax.dynamic_slice_in_dim(sm, chip * size, size, axis=ax), chip, 0)
            blk = g4.shape[1:]
            out.append(sum_parts(g4.reshape(N_CHIPS, -1, blk[-1]), f"reduce_chips_add_{tag}{i}").reshape(blk))
        return out

    e_units, e_send, e_recv, e_lands, e_pieces, e_axes = early
    early_got = scatter_wait(e_send, e_recv, e_lands, e_pieces, dx, "reduce_chips_wait")
    reduced = finish_reduce(e_pieces, e_axes, early_got, "early")
    late_got = scatter_wait(l_send, l_recv, l_lands, l_pieces, new_v["mod_w"], "reduce_chips_late_wait")
    reduced += finish_reduce(l_pieces, l_axes, late_got, "late")
    units = e_units + l_units
    slots, out_shapes = [], []
    for n in EXCHANGED:
        mine = [i for i, u in enumerate(units) if u[0] == n]
        out_shapes.append((len(mine), 2) + reduced[mine[0]].shape)
        slots += [(len(out_shapes) - 1, units[i][1]) for i in mine]
    order = [i for n in EXCHANGED for i, u in enumerate(units) if u[0] == n]
    halves = gather_halves([reduced[i] for i in order], slots, out_shapes, "reduce_gather")
    for ti, n in enumerate(EXCHANGED):
        g = halves[ti]
        for i, u in enumerate(units):
            if u[0] == n:
                g = lax.dynamic_update_slice(g, reduced[i][None, None], (u[1], ci) + (0,) * reduced[i].ndim)
        grads[n] = g.reshape(W[n].shape)
        deltas[n], new_m[n], new_v[n] = adamw(W[n], grads[n], M1[n], V2[n], f"adamw_{n}")

    return (loss, dx.reshape(x.shape), *[grads[n] for n in WEIGHT_ORDER], *[deltas[n] for n in WEIGHT_ORDER],
            *[new_m[n] for n in WEIGHT_ORDER], *[new_v[n] for n in WEIGHT_ORDER])
```

```python
import functools
import math

import jax
import jax.numpy as jnp
from jax import lax
from jax.experimental import pallas as pl
from jax.experimental.pallas import tpu as pltpu

F32 = jnp.float32
BF16 = jnp.bfloat16
MESH = pl.DeviceIdType.MESH

DEPTH = 4
N_A = 2
POOL_WINDOWS = (2, 4, 8, 16)
N_GROUPS = 4
N_HEADS = 8
QK_NOPE = 128
QK_ROPE = 64
V_HEAD = 128
QK_HEAD = QK_NOPE + QK_ROPE
HEAD_PAD = 256
KV_RANK = 256
ROPE_THETA = 10000.0
EPS = 1e-6
ADAM_LR = 0.001
ADAM_B1 = 0.9
ADAM_B2 = 0.999
ADAM_EPS = 1e-08
ADAM_WD = 0.01
ADAM_STEP = 10

N_CHIPS = 4
N_DEV = 8
LANES = 128
PACK_COLS = 1024
VMEM_LIMIT = 56 * 1024 * 1024
GLU_TILE = 256
ATT_BWD_K_BLOCK = 512
ATT_BWD_Q_BLOCK = 512
ATT_Q_BLOCK = 1024
ATT_K_BLOCK = 512
ATT_HEADS_PER_STEP = 2


def _cparams(*sem):
    return pltpu.CompilerParams(dimension_semantics=sem if sem else None, vmem_limit_bytes=VMEM_LIMIT)


def _pick(n, target, mult):
    best = None
    d = mult
    while d <= min(n, target):
        if n % d == 0:
            best = d
        d += mult
    return n if best is None else best


def _row(v):
    return v.reshape(1, -1).astype(F32)


_DIMS = {"nn": (((1,), (0,)), ((), ())), "nt": (((1,), (1,)), ((), ())), "tn": (((0,), (0,)), ((), ()))}


def _mm_body(mode, nk, has_bias, has_res):
    def body(*refs):
        a_ref, b_ref = refs[0], refs[1]
        pos = 2
        bias_ref = res_ref = cs_ref = None
        if has_bias:
            bias_ref = refs[pos]
            pos += 1
        if has_res:
            res_ref, cs_ref = refs[pos], refs[pos + 1]
            pos += 2
        o_ref = refs[pos]
        pos += 1
        o2_ref = None
        if has_res:
            o2_ref = refs[pos]
            pos += 1
        acc_ref = refs[pos] if nk > 1 else None
        k = pl.program_id(2)
        part = lax.dot_general(a_ref[...].astype(BF16), b_ref[...].astype(BF16), _DIMS[mode],
                               preferred_element_type=F32)

        def finish(y):
            if has_bias:
                y = y + bias_ref[...]
            o_ref[...] = y.astype(o_ref.dtype)
            if has_res:
                o2_ref[...] = res_ref[...] + cs_ref[...] * y

        if nk == 1:
            finish(part)
            return

        @pl.when(k == 0)
        def _():
            acc_ref[...] = part

        @pl.when((k > 0) & (k < nk - 1))
        def _():
            acc_ref[...] += part

        @pl.when(k == nk - 1)
        def _():
            finish(acc_ref[...] + part)

    return body


def mm(a, b, mode, out_dtype, name, *, tm=1408, tn=1408, tk=1024, bias=None, res=None, colscale=None, layer=None):
    bshape = b.shape if layer is None else b.shape[1:]
    if mode == "nn":
        (M, K), N = a.shape, bshape[1]
    elif mode == "nt":
        (M, K), N = a.shape, bshape[0]
    else:
        (K, M), N = a.shape, bshape[1]
    tm = _pick(M, tm, LANES if mode == "tn" else 8)
    tn = _pick(N, tn, LANES)
    tk = _pick(K, tk, LANES) if mode != "tn" else _pick(K, tk, 8)
    nk = K // tk
    a_spec = {"nn": pl.BlockSpec((tm, tk), lambda i, j, k: (i, k)),
              "nt": pl.BlockSpec((tm, tk), lambda i, j, k: (i, k)),
              "tn": pl.BlockSpec((tk, tm), lambda i, j, k: (k, i))}[mode]
    b_blk, b_map = {"nn": ((tk, tn), lambda i, j, k: (k, j)),
                    "nt": ((tn, tk), lambda i, j, k: (j, k)),
                    "tn": ((tk, tn), lambda i, j, k: (k, j))}[mode]
    if layer is None:
        b_spec = pl.BlockSpec(b_blk, b_map)
    else:
        b_spec = pl.BlockSpec((None,) + b_blk, lambda i, j, k: (layer,) + b_map(i, j, k))
    o_spec = pl.BlockSpec((tm, tn), lambda i, j, k: (i, j))
    v_spec = pl.BlockSpec((1, tn), lambda i, j, k: (0, j))
    in_specs, args = [a_spec, b_spec], [a, b]
    if bias is not None:
        in_specs.append(v_spec)
        args.append(_row(bias))
    out_shape = [jax.ShapeDtypeStruct((M, N), out_dtype)]
    out_specs = [o_spec]
    if res is not None:
        in_specs += [o_spec, v_spec]
        args += [res, _row(colscale)]
        out_shape.append(jax.ShapeDtypeStruct((M, N), F32))
        out_specs.append(o_spec)
    outs = pl.pallas_call(
        _mm_body(mode, nk, bias is not None, res is not None),
        grid=(M // tm, N // tn, nk),
        in_specs=in_specs, out_specs=out_specs, out_shape=out_shape,
        scratch_shapes=[pltpu.VMEM((tm, tn), F32)] if nk > 1 else [],
        compiler_params=_cparams("parallel", "parallel", "arbitrary"),
        name=name,
    )(*args)
    return outs if res is not None else outs[0]


def gmm(a, w, mode, out_dtype, name, *, bias=None, res=None, colscale=None, tr=512):
    S_ = a.shape[0]
    G = N_GROUPS
    C = a.shape[1] // G
    tr = _pick(S_, tr, 8)
    nr = S_ // tr
    if mode == "tn":
        def body(a_ref, b_ref, o_ref, acc_ref):
            i = pl.program_id(1)

            @pl.when(i == 0)
            def _():
                acc_ref[...] = jnp.zeros_like(acc_ref)

            acc_ref[...] += lax.dot_general(a_ref[...].astype(BF16), b_ref[...].astype(BF16), _DIMS["tn"],
                                            preferred_element_type=F32)

            @pl.when(i == nr - 1)
            def _():
                o_ref[...] = acc_ref[...].astype(o_ref.dtype)

        blk = pl.BlockSpec((tr, C), lambda g, i: (i, g))
        return pl.pallas_call(
            body, grid=(G, nr), in_specs=[blk, blk],
            out_specs=pl.BlockSpec((None, C, C), lambda g, i: (g, 0, 0)),
            out_shape=jax.ShapeDtypeStruct((G, C, C), out_dtype),
            scratch_shapes=[pltpu.VMEM((C, C), F32)],
            compiler_params=_cparams("parallel", "arbitrary"), name=name,
        )(a, w)

    has_bias, has_res = bias is not None, res is not None

    def body(*refs):
        a_ref, w_ref = refs[0], refs[1]
        pos = 2
        if has_bias:
            bias_ref = refs[pos]
            pos += 1
        if has_res:
            res_ref, cs_ref = refs[pos], refs[pos + 1]
            pos += 2
        o_ref = refs[pos]
        y = lax.dot_general(a_ref[...].astype(BF16), w_ref[...].astype(BF16), _DIMS[mode],
                            preferred_element_type=F32)
        if has_bias:
            y = y + bias_ref[...]
        o_ref[...] = y.astype(o_ref.dtype)
        if has_res:
            refs[pos + 1][...] = res_ref[...] + cs_ref[...] * y

    blk = pl.BlockSpec((tr, C), lambda i, g: (i, g))
    vec = pl.BlockSpec((1, C), lambda i, g: (0, g))
    in_specs = [blk, pl.BlockSpec((None, C, C), lambda i, g: (g, 0, 0))]
    args = [a, w]
    if has_bias:
        in_specs.append(vec)
        args.append(_row(bias))
    out_shape = [jax.ShapeDtypeStruct(a.shape, out_dtype)]
    out_specs = [blk]
    if has_res:
        in_specs += [blk, vec]
        args += [res, _row(colscale)]
        out_shape.append(jax.ShapeDtypeStruct(a.shape, F32))
        out_specs.append(blk)
    outs = pl.pallas_call(
        body, grid=(nr, G), in_specs=in_specs, out_specs=out_specs, out_shape=out_shape,
        compiler_params=_cparams("parallel", "parallel"), name=name,
    )(*args)
    return outs if has_res else outs[0]


def norm_fwd(x, g, sc, sh, out_dtype, name, tr=512):
    S_, Dn = x.shape
    tr = _pick(S_, tr, 8)

    def body(x_ref, g_ref, sc_ref, sh_ref, o_ref):
        xv = x_ref[...]
        r = lax.rsqrt(jnp.mean(xv * xv, axis=-1, keepdims=True) + EPS)
        o_ref[...] = (((xv * r) * g_ref[...]) * (1.0 + sc_ref[...]) + sh_ref[...]).astype(o_ref.dtype)

    blk = pl.BlockSpec((tr, Dn), lambda i: (i, 0))
    vec = pl.BlockSpec((1, Dn), lambda i: (0, 0))
    return pl.pallas_call(
        body, grid=(S_ // tr,), in_specs=[blk, vec, vec, vec], out_specs=blk,
        out_shape=jax.ShapeDtypeStruct((S_, Dn), out_dtype),
        compiler_params=_cparams("parallel"), name=name,
    )(x, _row(g), _row(sc), _row(sh))


def norm_bwd(x, g, sc, dh, dres, name, gate=None, tr=512):
    S_, Dn = x.shape
    tr = _pick(S_, tr, 8)
    has_res = dres is not None
    has_gate = gate is not None

    def body(*refs):
        x_ref, g_ref, sc_ref, dh_ref = refs[:4]
        pos = 4
        if has_res:
            dres_ref = refs[pos]
            pos += 1
        if has_gate:
            y_ref, cs_ref = refs[pos:pos + 2]
            pos += 2
        dx_ref, s1_ref, s2_ref = refs[pos:pos + 3]
        if has_gate:
            d_ref, a_ref, c_ref = refs[pos + 3:pos + 6]
        i = pl.program_id(0)

        @pl.when(i == 0)
        def _():
            s1_ref[...] = jnp.zeros_like(s1_ref)
            s2_ref[...] = jnp.zeros_like(s2_ref)
            if has_gate:
                a_ref[...] = jnp.zeros_like(a_ref)
                c_ref[...] = jnp.zeros_like(c_ref)

        xv = x_ref[...]
        r = lax.rsqrt(jnp.mean(xv * xv, axis=-1, keepdims=True) + EPS)
        n = xv * r
        dhv = dh_ref[...].astype(F32)
        dn = dhv * (g_ref[...] * (1.0 + sc_ref[...]))
        dx = r * (dn - n * jnp.mean(dn * n, axis=-1, keepdims=True))
        if has_res:
            dx = dx + dres_ref[...]
        dx_ref[...] = dx
        s1_ref[...] += jnp.sum(dhv, axis=0, keepdims=True)
        s2_ref[...] += jnp.sum(dhv * n, axis=0, keepdims=True)
        if has_gate:
            d_ref[...] = (dx * cs_ref[...]).astype(d_ref.dtype)
            a_ref[...] += jnp.sum(dx * y_ref[...].astype(F32), axis=0, keepdims=True)
            c_ref[...] += jnp.sum(dx, axis=0, keepdims=True)

    blk = pl.BlockSpec((tr, Dn), lambda i: (i, 0))
    vec = pl.BlockSpec((1, Dn), lambda i: (0, 0))
    in_specs, args = [blk, vec, vec, blk], [x, _row(g), _row(sc), dh]
    if has_res:
        in_specs.append(blk)
        args.append(dres)
    vshape = jax.ShapeDtypeStruct((1, Dn), F32)
    out_specs = [blk, vec, vec]
    out_shape = [jax.ShapeDtypeStruct((S_, Dn), F32), vshape, vshape]
    if has_gate:
        in_specs += [blk, vec]
        args += [gate[0], _row(gate[1])]
        out_specs += [blk, vec, vec]
        out_shape += [jax.ShapeDtypeStruct((S_, Dn), BF16), vshape, vshape]
    return pl.pallas_call(
        body, grid=(S_ // tr,), in_specs=in_specs, out_specs=out_specs, out_shape=out_shape,
        compiler_params=_cparams("arbitrary"), name=name,
    )(*args)


def gate_bwd(dx, y, colscale, name, tr=512):
    S_, Dn = dx.shape
    tr = _pick(S_, tr, 8)

    def body(dx_ref, y_ref, cs_ref, d_ref, a_ref, c_ref):
        i = pl.program_id(0)

        @pl.when(i == 0)
        def _():
            a_ref[...] = jnp.zeros_like(a_ref)
            c_ref[...] = jnp.zeros_like(c_ref)

        dxv = dx_ref[...]
        d_ref[...] = (dxv * cs_ref[...]).astype(d_ref.dtype)
        a_ref[...] += jnp.sum(dxv * y_ref[...].astype(F32), axis=0, keepdims=True)
        c_ref[...] += jnp.sum(dxv, axis=0, keepdims=True)

    blk = pl.BlockSpec((tr, Dn), lambda i: (i, 0))
    vec = pl.BlockSpec((1, Dn), lambda i: (0, 0))
    vshape = jax.ShapeDtypeStruct((1, Dn), F32)
    return pl.pallas_call(
        body, grid=(S_ // tr,), in_specs=[blk, blk, vec], out_specs=[blk, vec, vec],
        out_shape=[jax.ShapeDtypeStruct((S_, Dn), BF16), vshape, vshape],
        compiler_params=_cparams("arbitrary"), name=name,
    )(dx, y, _row(colscale))


def loss_head(x, g, target, name, tr=512):
    S_, Dn = x.shape
    tr = _pick(S_, tr, 8)

    def body(x_ref, g_ref, t_ref, dx_ref, dg_ref, loss_ref):
        i = pl.program_id(0)

        @pl.when(i == 0)
        def _():
            dg_ref[...] = jnp.zeros_like(dg_ref)
            loss_ref[...] = jnp.zeros_like(loss_ref)

        xv = x_ref[...]
        r = lax.rsqrt(jnp.mean(xv * xv, axis=-1, keepdims=True) + EPS)
        n = xv * r
        e = n * g_ref[...] - t_ref[...]
        loss_ref[...] += 0.5 * jnp.sum(jnp.mean(e * e, axis=-1, keepdims=True), axis=0, keepdims=True)
        dy = e * (1.0 / Dn)
        dg_ref[...] += jnp.sum(dy * n, axis=0, keepdims=True)
        dn = dy * g_ref[...]
        dx_ref[...] = r * (dn - n * jnp.mean(dn * n, axis=-1, keepdims=True))

    blk = pl.BlockSpec((tr, Dn), lambda i: (i, 0))
    vec = pl.BlockSpec((1, Dn), lambda i: (0, 0))
    one = pl.BlockSpec((1, 1), lambda i: (0, 0))
    return pl.pallas_call(
        body, grid=(S_ // tr,), in_specs=[blk, vec, blk], out_specs=[blk, vec, one],
        out_shape=[jax.ShapeDtypeStruct((S_, Dn), F32), jax.ShapeDtypeStruct((1, Dn), F32),
                   jax.ShapeDtypeStruct((1, 1), F32)],
        compiler_params=_cparams("arbitrary"), name=name,
    )(x, _row(g), target)


POOL_HALO = 16
POOL_CHUNK = 512


def _rows(ref, lo, hi, n_rows):
    parts = []
    if lo < 0:
        parts.append(jnp.zeros((-lo, ref.shape[1]), F32))
    parts.append(ref[max(lo, 0):min(hi, n_rows), :].astype(F32))
    if hi > n_rows:
        parts.append(jnp.zeros((hi - n_rows, ref.shape[1]), F32))
    return parts[0] if len(parts) == 1 else jnp.concatenate(parts, axis=0)


def _window_sum(e, w, back):
    n = e.shape[0]
    s, width = e, 1
    while width < w:
        s = s + pltpu.roll(s, width if back else n - width, 0)
        width *= 2
    return s


def _pool_call(h, out_dtype, name, backward):
    S_, Dn = h.shape
    C = Dn // N_GROUPS
    ch = _pick(S_, POOL_CHUNK, 8)

    def body(h_ref, o_ref):
        g = pl.program_id(0)
        for gi, w in enumerate(POOL_WINDOWS):
            @pl.when(g == gi)
            def _(w=w):
                for r0 in range(0, S_, ch):
                    t = (r0 + lax.broadcasted_iota(jnp.int32, (ch, C), 0)).astype(F32)
                    cnt = jnp.minimum(t + 1.0, float(w))
                    if not backward:
                        ext = _rows(h_ref, r0 - POOL_HALO, r0 + ch, S_)
                        cur = ext[POOL_HALO:]
                        mean = _window_sum(ext, w, True)[POOL_HALO:] / cnt
                        o_ref[r0:r0 + ch, :] = (mean - cur).astype(o_ref.dtype)
                    else:
                        ext = _rows(h_ref, r0, r0 + ch + POOL_HALO, S_)
                        text = (r0 + lax.broadcasted_iota(jnp.int32, (ch + POOL_HALO, C), 0)).astype(F32)
                        e = ext / jnp.minimum(text + 1.0, float(w))
                        o_ref[r0:r0 + ch, :] = (_window_sum(e, w, False)[:ch] - ext[:ch]).astype(o_ref.dtype)

    blk = pl.BlockSpec((S_, C), lambda g: (0, g))
    return pl.pallas_call(
        body, grid=(N_GROUPS,), in_specs=[blk], out_specs=blk,
        out_shape=jax.ShapeDtypeStruct((S_, Dn), out_dtype),
        compiler_params=_cparams("parallel"), name=name,
    )(h)


GLU_CHUNK = 512
GLU_HALO = 16
_SQRT_HALF = 0.7071067811865476
_INV_SQRT_2PI = 0.3989422804014327


def _gelu(a):
    return 0.5 * a * (1.0 + lax.erf(a * _SQRT_HALF))


def _gelu_grad(a):
    return 0.5 * (1.0 + lax.erf(a * _SQRT_HALF)) + a * (_INV_SQRT_2PI * jnp.exp(-0.5 * a * a))


def glu_fwd(u, conv_w, conv_b, name):
    S_, F2 = u.shape
    Fh = F2 // 2
    tf = GLU_TILE
    nt = Fh // tf
    ch = _pick(S_, GLU_CHUNK, GLU_HALO)

    def body(a_ref, v_ref, cw_ref, cb_ref, z_ref):
        cw0, cw1, cw2 = cw_ref[0:1, :], cw_ref[1:2, :], cw_ref[2:3, :]
        cb = cb_ref[...]
        for r0 in range(0, S_, ch):
            ext = _rows(a_ref, r0 - GLU_HALO, r0 + ch, S_)
            a0 = ext[GLU_HALO:]
            a1 = pltpu.roll(ext, 1, 0)[GLU_HALO:]
            a2 = pltpu.roll(ext, 2, 0)[GLU_HALO:]
            ac = a2 * cw0 + a1 * cw1 + a0 * cw2 + cb
            z_ref[r0:r0 + ch, :] = (_gelu(ac) * v_ref[r0:r0 + ch, :].astype(F32)).astype(z_ref.dtype)

    return pl.pallas_call(
        body, grid=(nt,),
        in_specs=[pl.BlockSpec((S_, tf), lambda j: (0, j)), pl.BlockSpec((S_, tf), lambda j: (0, j + nt)),
                  pl.BlockSpec((3, tf), lambda j: (0, j)), pl.BlockSpec((1, tf), lambda j: (0, j))],
        out_specs=pl.BlockSpec((S_, tf), lambda j: (0, j)),
        out_shape=jax.ShapeDtypeStruct((S_, Fh), BF16),
        compiler_params=_cparams("parallel"), name=name,
    )(u, u, conv_w, _row(conv_b))


def glu_bwd(u, dz, conv_w, conv_b, name):
    S_, F2 = u.shape
    Fh = F2 // 2
    tf = GLU_TILE
    nt = Fh // tf
    ch = _pick(S_, GLU_CHUNK, GLU_HALO)

    def body(a_ref, v_ref, dz_ref, cw_ref, cb_ref, du_ref, dcw_ref, dcb_ref, da_buf, dv_buf, sems):
        j = pl.program_id(0)
        slot = j % 2

        def writes(step, sl):
            lo = pl.multiple_of(step * tf, tf)
            return (pltpu.make_async_copy(da_buf.at[sl], du_ref.at[:, pl.ds(lo, tf)], sems.at[sl, 0]),
                    pltpu.make_async_copy(dv_buf.at[sl], du_ref.at[:, pl.ds(Fh + lo, tf)], sems.at[sl, 1]))

        @pl.when(j >= 2)
        def _():
            for cp in writes(j - 2, slot):
                cp.wait()

        cw0, cw1, cw2 = cw_ref[0:1, :], cw_ref[1:2, :], cw_ref[2:3, :]
        cb = cb_ref[...]
        acc = [jnp.zeros((1, tf), F32) for _ in range(4)]
        n = ch + GLU_HALO
        for r0 in range(0, S_, ch):
            ext = _rows(a_ref, r0 - GLU_HALO, r0 + n, S_)
            a0 = ext[GLU_HALO:]
            a1 = pltpu.roll(ext, 1, 0)[GLU_HALO:]
            a2 = pltpu.roll(ext, 2, 0)[GLU_HALO:]
            ac = a2 * cw0 + a1 * cw1 + a0 * cw2 + cb
            vv = _rows(v_ref, r0, r0 + n, S_)
            dzv = _rows(dz_ref, r0, r0 + n, S_)
            gl = _gelu(ac)
            dac = dzv * vv * _gelu_grad(ac)
            da = (dac * cw2 + pltpu.roll(dac, n - 1, 0) * cw1 + pltpu.roll(dac, n - 2, 0) * cw0)[:ch]
            da_buf[slot, r0:r0 + ch, :] = da.astype(da_buf.dtype)
            dv_buf[slot, r0:r0 + ch, :] = (dzv[:ch] * gl[:ch]).astype(dv_buf.dtype)
            dc = dac[:ch]
            acc[0] = acc[0] + jnp.sum(dc * a2[:ch], axis=0, keepdims=True)
            acc[1] = acc[1] + jnp.sum(dc * a1[:ch], axis=0, keepdims=True)
            acc[2] = acc[2] + jnp.sum(dc * a0[:ch], axis=0, keepdims=True)
            acc[3] = acc[3] + jnp.sum(dc, axis=0, keepdims=True)
        dcw_ref[0:1, :] = acc[0]
        dcw_ref[1:2, :] = acc[1]
        dcw_ref[2:3, :] = acc[2]
        dcb_ref[...] = acc[3]
        for cp in writes(j, slot):
            cp.start()

        @pl.when(j == nt - 1)
        def _():
            for cp in writes(j, slot):
                cp.wait()
            if nt > 1:
                for cp in writes(j - 1, 1 - slot):
                    cp.wait()

    return pl.pallas_call(
        body, grid=(nt,),
        in_specs=[pl.BlockSpec((S_, tf), lambda j: (0, j)), pl.BlockSpec((S_, tf), lambda j: (0, j + nt)),
                  pl.BlockSpec((S_, tf), lambda j: (0, j)),
                  pl.BlockSpec((3, tf), lambda j: (0, j)), pl.BlockSpec((1, tf), lambda j: (0, j))],
        out_specs=[_ANY, pl.BlockSpec((3, tf), lambda j: (0, j)), pl.BlockSpec((1, tf), lambda j: (0, j))],
        out_shape=[jax.ShapeDtypeStruct((S_, F2), BF16), jax.ShapeDtypeStruct((3, Fh), F32),
                   jax.ShapeDtypeStruct((1, Fh), F32)],
        scratch_shapes=[pltpu.VMEM((2, S_, tf), BF16), pltpu.VMEM((2, S_, tf), BF16), pltpu.SemaphoreType.DMA((2, 2))],
        compiler_params=_cparams("arbitrary"), name=name,
    )(u, u, dz, conv_w, _row(conv_b))


def rope_tables(pos, inv, name, tr=512):
    S_ = pos.shape[0]
    tr = _pick(S_, tr, 8)

    def body(p_ref, inv_ref, c_ref, s1_ref, s2_ref):
        ang = p_ref[...] * inv_ref[...]
        lane = lax.broadcasted_iota(jnp.int32, ang.shape, 1)
        half = QK_ROPE // 2
        cosv, sinv = jnp.cos(ang), jnp.sin(ang)
        c_ref[...] = jnp.where(lane < QK_ROPE, cosv, 0.0)
        s1_ref[...] = jnp.where(lane < half, -sinv, 0.0)
        s2_ref[...] = jnp.where((lane >= half) & (lane < QK_ROPE), sinv, 0.0)

    blk = pl.BlockSpec((tr, LANES), lambda i: (i, 0))
    shp = jax.ShapeDtypeStruct((S_, LANES), F32)
    return pl.pallas_call(
        body, grid=(S_ // tr,),
        in_specs=[pl.BlockSpec((tr, 1), lambda i: (i, 0)), pl.BlockSpec((1, LANES), lambda i: (0, 0))],
        out_specs=[blk, blk, blk], out_shape=[shp, shp, shp],
        compiler_params=_cparams("parallel"), name=name,
    )(pos, inv)


_HALF = QK_ROPE // 2


def _rope(t, c, s1, s2):
    return t * c + pltpu.roll(t, LANES - _HALF, 1) * s1 + pltpu.roll(t, _HALF, 1) * s2


def _rope_t(d, c, s1, s2):
    return d * c + pltpu.roll(d * s1, _HALF, 1) + pltpu.roll(d * s2, LANES - _HALF, 1)


def q_prep(q, tabs, scale, backward, name, tr=512):
    S_, W = q.shape
    tr = _pick(S_, tr, 8)

    def body(q_ref, c_ref, s1_ref, s2_ref, o_ref):
        o_ref[:, 0:LANES] = (q_ref[:, 0:LANES].astype(F32) * scale).astype(o_ref.dtype)
        t = q_ref[:, LANES:2 * LANES].astype(F32)
        fn = _rope_t if backward else _rope
        o_ref[:, LANES:2 * LANES] = (fn(t, c_ref[...], s1_ref[...], s2_ref[...]) * scale).astype(o_ref.dtype)

    blk = pl.BlockSpec((tr, HEAD_PAD), lambda i, h: (i, h))
    tab = pl.BlockSpec((tr, LANES), lambda i, h: (i, 0))
    return pl.pallas_call(
        body, grid=(S_ // tr, W // HEAD_PAD), in_specs=[blk, tab, tab, tab], out_specs=blk,
        out_shape=jax.ShapeDtypeStruct((S_, W), BF16),
        compiler_params=_cparams("parallel", "parallel"), name=name,
    )(q, *tabs)


def k_prep(knv, kv_ext, tabs, name, tr=512):
    S_ = knv.shape[0]
    tr = _pick(S_, tr, 8)

    def body(kn_ref, v_ref, t_ref, c_ref, s1_ref, s2_ref, o_ref, vx_ref):
        o_ref[:, 0:LANES] = kn_ref[...].astype(o_ref.dtype)
        o_ref[:, LANES:2 * LANES] = _rope(t_ref[...], c_ref[...], s1_ref[...], s2_ref[...]).astype(o_ref.dtype)
        vx_ref[:, 0:V_HEAD] = v_ref[...].astype(vx_ref.dtype)
        vx_ref[:, V_HEAD:HEAD_PAD] = jnp.ones((tr, HEAD_PAD - V_HEAD), vx_ref.dtype)

    tab = pl.BlockSpec((tr, LANES), lambda i, h: (i, 0))
    head = pl.BlockSpec((tr, HEAD_PAD), lambda i, h: (i, h))
    shp = jax.ShapeDtypeStruct((S_, N_HEADS * HEAD_PAD), BF16)
    return pl.pallas_call(
        body, grid=(S_ // tr, N_HEADS),
        in_specs=[pl.BlockSpec((tr, LANES), lambda i, h: (i, h)),
                  pl.BlockSpec((tr, V_HEAD), lambda i, h: (i, N_HEADS + h)),
                  pl.BlockSpec((tr, LANES), lambda i, h: (i, KV_RANK // LANES)), tab, tab, tab],
        out_specs=[head, head], out_shape=[shp, shp],
        compiler_params=_cparams("parallel", "parallel"), name=name,
    )(knv, knv, kv_ext, *tabs)


def k_prep_bwd(dk_a, dk_b, dv_a, dv_b, tabs, name, tr=256):
    S_ = dk_a.shape[0]
    tr = _pick(S_, tr, 8)
    HV = N_HEADS * V_HEAD

    def body(ka_ref, kb_ref, va_ref, vb_ref, c_ref, s1_ref, s2_ref, o_ref, t_ref):
        dr = jnp.zeros((tr, LANES), F32)
        for h in range(N_HEADS):
            lo = h * HEAD_PAD
            o_ref[:, h * LANES:(h + 1) * LANES] = (ka_ref[:, lo:lo + LANES] + kb_ref[:, lo:lo + LANES]).astype(o_ref.dtype)
            dr = dr + ka_ref[:, lo + LANES:lo + 2 * LANES] + kb_ref[:, lo + LANES:lo + 2 * LANES]
        o_ref[:, HV:2 * HV] = (va_ref[...] + vb_ref[...]).astype(o_ref.dtype)
        t_ref[...] = _rope_t(dr, c_ref[...], s1_ref[...], s2_ref[...])

    kblk = pl.BlockSpec((tr, N_HEADS * HEAD_PAD), lambda i: (i, 0))
    vblk = pl.BlockSpec((tr, HV), lambda i: (i, 0))
    tab = pl.BlockSpec((tr, LANES), lambda i: (i, 0))
    return pl.pallas_call(
        body, grid=(S_ // tr,), in_specs=[kblk, kblk, vblk, vblk, tab, tab, tab],
        out_specs=[pl.BlockSpec((tr, 2 * HV), lambda i: (i, 0)), tab],
        out_shape=[jax.ShapeDtypeStruct((S_, 2 * HV), BF16), jax.ShapeDtypeStruct((S_, LANES), F32)],
        compiler_params=_cparams("parallel"), name=name,
    )(dk_a, dk_b, dv_a, dv_b, *tabs)


_NEG = -1e30


def attn_fwd(q, k, vx, name):
    S_ = q.shape[0]
    TQ = _pick(S_, ATT_Q_BLOCK, 8)
    TK = _pick(S_, ATT_K_BLOCK, 8)
    assert TQ % TK == 0 or TK % TQ == 0
    HP = ATT_HEADS_PER_STEP
    W = HP * HEAD_PAD

    def body(q_ref, k_ref, v_ref, o_ref, lse_ref):
        i = pl.program_id(1)
        qs = [q_ref[:, h * HEAD_PAD:(h + 1) * HEAD_PAD] for h in range(HP)]

        def step(j, carry, masked):
            start = pl.multiple_of(j * TK, TK)
            out = []
            for h in range(HP):
                m, acc = carry[h]
                cols = slice(h * HEAD_PAD, (h + 1) * HEAD_PAD)
                s = lax.dot_general(qs[h], k_ref[pl.ds(start, TK), cols], _DIMS["nt"], preferred_element_type=F32)
                if masked:
                    rowi = i * TQ + lax.broadcasted_iota(jnp.int32, (TQ, TK), 0)
                    coli = j * TK + lax.broadcasted_iota(jnp.int32, (TQ, TK), 1)
                    s = jnp.where(coli <= rowi, s, _NEG)
                m_new = jnp.maximum(m, jnp.max(s, axis=-1, keepdims=True))
                alpha = jnp.exp(m - m_new)
                p = jnp.exp(s - m_new).astype(BF16)
                acc = alpha * acc + lax.dot_general(p, v_ref[pl.ds(start, TK), cols], _DIMS["nn"],
                                                    preferred_element_type=F32)
                out.append((m_new, acc))
            return tuple(out)

        init = tuple((jnp.full((TQ, 1), _NEG, F32), jnp.zeros((TQ, HEAD_PAD), F32)) for _ in range(HP))
        n_full, n_diag = (i * (TQ // TK), TQ // TK) if TQ >= TK else (i // (TK // TQ), 1)
        carry = lax.fori_loop(0, n_full, functools.partial(step, masked=False), init)
        for d in range(n_diag):
            carry = step(n_full + d, carry, True)
        for h in range(HP):
            m, acc = carry[h]
            l = acc[:, V_HEAD:]
            o_ref[:, h * V_HEAD:(h + 1) * V_HEAD] = (acc[:, :V_HEAD] / l).astype(o_ref.dtype)
            lse_ref[h] = m + jnp.log(jnp.max(l, axis=-1, keepdims=True))

    return pl.pallas_call(
        body, grid=(N_HEADS // HP, S_ // TQ),
        in_specs=[pl.BlockSpec((TQ, W), lambda g, i: (i, g)),
                  pl.BlockSpec((S_, W), lambda g, i: (0, g)),
                  pl.BlockSpec((S_, W), lambda g, i: (0, g))],
        out_specs=[pl.BlockSpec((TQ, HP * V_HEAD), lambda g, i: (i, g)),
                   pl.BlockSpec((HP, TQ, 1), lambda g, i: (g, i, 0))],
        out_shape=[jax.ShapeDtypeStruct((S_, N_HEADS * V_HEAD), BF16), jax.ShapeDtypeStruct((N_HEADS, S_, 1), F32)],
        compiler_params=_cparams("parallel", "parallel"), name=name,
    )(q, k, vx)


def q_proj(cq, w_ext, tabs, scale, name, tm=4096):
    S_, R = cq.shape
    tm = _pick(S_, tm, 16)

    def body(c_ref, w_ref, t_c, t_s1, t_s2, o_ref):
        y = lax.dot_general(c_ref[...].astype(BF16), w_ref[...].astype(BF16), _DIMS["nn"], preferred_element_type=F32)
        o_ref[:, 0:LANES] = (y[:, 0:LANES] * scale).astype(o_ref.dtype)
        o_ref[:, LANES:2 * LANES] = (_rope(y[:, LANES:2 * LANES], t_c[...], t_s1[...], t_s2[...]) * scale).astype(o_ref.dtype)

    tab = pl.BlockSpec((tm, LANES), lambda i, h: (i, 0))
    return pl.pallas_call(
        body, grid=(S_ // tm, N_HEADS),
        in_specs=[pl.BlockSpec((tm, R), lambda i, h: (i, 0)), pl.BlockSpec((R, HEAD_PAD), lambda i, h: (0, h)),
                  tab, tab, tab],
        out_specs=pl.BlockSpec((tm, HEAD_PAD), lambda i, h: (i, h)),
        out_shape=jax.ShapeDtypeStruct((S_, N_HEADS * HEAD_PAD), BF16),
        compiler_params=_cparams("parallel", "parallel"), name=name,
    )(cq, w_ext, *tabs)


def q_proj_bwd(dq, cq, w_ext, tabs, scale, name, tm=4096):
    S_, R = cq.shape
    tm = _pick(S_, tm, 16)
    nr = S_ // tm

    def body(dq_ref, c_ref, w_ref, t_c, t_s1, t_s2, dc_ref, dw_ref, acc_ref):
        h, i = pl.program_id(0), pl.program_id(1)
        g = jnp.concatenate([dq_ref[:, 0:LANES] * scale,
                             _rope_t(dq_ref[:, LANES:2 * LANES], t_c[...], t_s1[...], t_s2[...]) * scale],
                            axis=1).astype(BF16)
        part = lax.dot_general(g, w_ref[...].astype(BF16), _DIMS["nt"], preferred_element_type=F32)
        rows = pl.ds(pl.multiple_of(i * tm, tm), tm)

        @pl.when(h == 0)
        def _():
            dc_ref[rows, :] = part

        @pl.when(h > 0)
        def _():
            dc_ref[rows, :] += part

        dwp = lax.dot_general(c_ref[...].astype(BF16), g, _DIMS["tn"], preferred_element_type=F32)

        @pl.when(i == 0)
        def _():
            acc_ref[...] = dwp

        @pl.when(i > 0)
        def _():
            acc_ref[...] += dwp

        @pl.when(i == nr - 1)
        def _():
            dw_ref[...] = acc_ref[...].astype(dw_ref.dtype)

    tab = pl.BlockSpec((tm, LANES), lambda h, i: (i, 0))
    return pl.pallas_call(
        body, grid=(N_HEADS, nr),
        in_specs=[pl.BlockSpec((tm, HEAD_PAD), lambda h, i: (i, h)), pl.BlockSpec((tm, R), lambda h, i: (i, 0)),
                  pl.BlockSpec((R, HEAD_PAD), lambda h, i: (0, h)), tab, tab, tab],
        out_specs=[pl.BlockSpec((S_, R), lambda h, i: (0, 0)), pl.BlockSpec((R, HEAD_PAD), lambda h, i: (0, h))],
        out_shape=[jax.ShapeDtypeStruct((S_, R), F32), jax.ShapeDtypeStruct((R, N_HEADS * HEAD_PAD), BF16)],
        scratch_shapes=[pltpu.VMEM((R, HEAD_PAD), F32)],
        compiler_params=_cparams("arbitrary", "arbitrary"), name=name,
    )(dq, cq, w_ext, *tabs)


def kv_proj(ckv, w_ukv, kv_ext, tabs, name, tm=2048):
    S_, R = ckv.shape
    tm = _pick(S_, tm, 16)

    def body(c_ref, wk_ref, wv_ref, t_ref, t_c, t_s1, t_s2, k_ref, vx_ref):
        cv = c_ref[...].astype(BF16)
        k_ref[:, 0:LANES] = lax.dot_general(cv, wk_ref[...].astype(BF16), _DIMS["nn"],
                                            preferred_element_type=F32).astype(k_ref.dtype)
        k_ref[:, LANES:2 * LANES] = _rope(t_ref[...], t_c[...], t_s1[...], t_s2[...]).astype(k_ref.dtype)
        vx_ref[:, 0:V_HEAD] = lax.dot_general(cv, wv_ref[...].astype(BF16), _DIMS["nn"],
                                              preferred_element_type=F32).astype(vx_ref.dtype)
        vx_ref[:, V_HEAD:HEAD_PAD] = jnp.ones((tm, HEAD_PAD - V_HEAD), vx_ref.dtype)

    tab = pl.BlockSpec((tm, LANES), lambda i, h: (i, 0))
    head = pl.BlockSpec((tm, HEAD_PAD), lambda i, h: (i, h))
    shp = jax.ShapeDtypeStruct((S_, N_HEADS * HEAD_PAD), BF16)
    return pl.pallas_call(
        body, grid=(S_ // tm, N_HEADS),
        in_specs=[pl.BlockSpec((tm, R), lambda i, h: (i, 0)), pl.BlockSpec((R, QK_NOPE), lambda i, h: (0, h)),
                  pl.BlockSpec((R, V_HEAD), lambda i, h: (0, N_HEADS + h)),
                  pl.BlockSpec((tm, LANES), lambda i, h: (i, KV_RANK // LANES)), tab, tab, tab],
        out_specs=[head, head], out_shape=[shp, shp],
        compiler_params=_cparams("parallel", "parallel"), name=name,
    )(ckv, w_ukv, w_ukv, kv_ext, *tabs)


def o_proj_bwd(dy, w_o, o, name, tm=512):
    S_, Dn = dy.shape
    HV = w_o.shape[0]
    tm = _pick(S_, tm, 16)

    def body(dy_ref, w_ref, o_ref, do_ref, d_ref):
        do = lax.dot_general(dy_ref[...].astype(BF16), w_ref[...].astype(BF16), _DIMS["nt"], preferred_element_type=F32)
        do_ref[...] = do.astype(do_ref.dtype)
        prod = do * o_ref[...].astype(F32)
        for h in range(N_HEADS):
            d_ref[h] = jnp.sum(prod[:, h * V_HEAD:(h + 1) * V_HEAD], axis=-1, keepdims=True)

    return pl.pallas_call(
        body, grid=(S_ // tm,),
        in_specs=[pl.BlockSpec((tm, Dn), lambda i: (i, 0)), pl.BlockSpec((HV, Dn), lambda i: (0, 0)),
                  pl.BlockSpec((tm, HV), lambda i: (i, 0))],
        out_specs=[pl.BlockSpec((tm, HV), lambda i: (i, 0)), pl.BlockSpec((N_HEADS, tm, 1), lambda i: (0, i, 0))],
        out_shape=[jax.ShapeDtypeStruct((S_, HV), BF16), jax.ShapeDtypeStruct((N_HEADS, S_, 1), F32)],
        compiler_params=_cparams("parallel"), name=name,
    )(dy, w_o, o)


def attn_delta(o, do, name, tr=512):
    S_ = o.shape[0]
    tr = _pick(S_, tr, 8)

    def body(o_ref, do_ref, d_ref):
        d_ref[...] = jnp.sum(o_ref[...].astype(F32) * do_ref[...].astype(F32), axis=-1, keepdims=True)

    blk = pl.BlockSpec((tr, V_HEAD), lambda i, h: (i, h))
    return pl.pallas_call(
        body, grid=(S_ // tr, N_HEADS), in_specs=[blk, blk],
        out_specs=pl.BlockSpec((None, tr, 1), lambda i, h: (h, i, 0)),
        out_shape=jax.ShapeDtypeStruct((N_HEADS, S_, 1), F32),
        compiler_params=_cparams("parallel", "parallel"), name=name,
    )(o, do)


def attn_bwd(q, k, vx, do, lse_row, delta_row, name):
    S_ = q.shape[0]
    TK = _pick(S_, ATT_BWD_K_BLOCK, LANES)
    TQ = _pick(S_, ATT_BWD_Q_BLOCK, TK)
    HP = ATT_HEADS_PER_STEP
    W = HP * HEAD_PAD
    ratio = TQ // TK
    nq = S_ // TQ

    def body(q_ref, do_ref, lse_ref, dl_ref, k_ref, v_ref, dq_ref, dk_ref, dv_ref):
        j = pl.program_id(1)

        @pl.when(j == 0)
        def _():
            dq_ref[...] = jnp.zeros_like(dq_ref)

        ks = [k_ref[:, h * HEAD_PAD:(h + 1) * HEAD_PAD] for h in range(HP)]
        vs = [v_ref[:, h * HEAD_PAD:h * HEAD_PAD + V_HEAD] for h in range(HP)]

        def step(i, carry, masked):
            start = pl.multiple_of(i * TQ, TQ)
            out = []
            for h in range(HP):
                dk, dv = carry[h]
                cols = slice(h * HEAD_PAD, (h + 1) * HEAD_PAD)
                qv = q_ref[pl.ds(start, TQ), cols]
                dov = do_ref[pl.ds(start, TQ), h * V_HEAD:(h + 1) * V_HEAD]
                st = lax.dot_general(ks[h], qv, _DIMS["nt"], preferred_element_type=F32)
                pt = jnp.exp(st - lse_ref[h, :, pl.ds(start, TQ)])
                if masked:
                    keyi = j * TK + lax.broadcasted_iota(jnp.int32, (TK, TQ), 0)
                    qryi = i * TQ + lax.broadcasted_iota(jnp.int32, (TK, TQ), 1)
                    pt = jnp.where(keyi <= qryi, pt, 0.0)
                dpt = lax.dot_general(vs[h], dov, _DIMS["nt"], preferred_element_type=F32)
                dst = (pt * (dpt - dl_ref[h, :, pl.ds(start, TQ)])).astype(BF16)
                dv = dv + lax.dot_general(pt.astype(BF16), dov, _DIMS["nn"], preferred_element_type=F32)
                dk = dk + lax.dot_general(dst, qv, _DIMS["nn"], preferred_element_type=F32)
                dq_ref[pl.ds(start, TQ), cols] += lax.dot_general(dst, ks[h], _DIMS["tn"], preferred_element_type=F32)
                out.append((dk, dv))
            return tuple(out)

        init = tuple((jnp.zeros((TK, HEAD_PAD), F32), jnp.zeros((TK, V_HEAD), F32)) for _ in range(HP))
        first = j // ratio
        carry = lax.fori_loop(first + 1, nq, functools.partial(step, masked=False), step(first, init, True))
        for h in range(HP):
            dk_ref[:, h * HEAD_PAD:(h + 1) * HEAD_PAD] = carry[h][0]
            dv_ref[:, h * V_HEAD:(h + 1) * V_HEAD] = carry[h][1]

    return pl.pallas_call(
        body, grid=(N_HEADS // HP, S_ // TK),
        in_specs=[pl.BlockSpec((S_, W), lambda g, j: (0, g)),
                  pl.BlockSpec((S_, HP * V_HEAD), lambda g, j: (0, g)),
                  pl.BlockSpec((HP, 1, S_), lambda g, j: (g, 0, 0)),
                  pl.BlockSpec((HP, 1, S_), lambda g, j: (g, 0, 0)),
                  pl.BlockSpec((TK, W), lambda g, j: (j, g)),
                  pl.BlockSpec((TK, W), lambda g, j: (j, g))],
        out_specs=[pl.BlockSpec((S_, W), lambda g, j: (0, g)),
                   pl.BlockSpec((TK, W), lambda g, j: (j, g)),
                   pl.BlockSpec((TK, HP * V_HEAD), lambda g, j: (j, g))],
        out_shape=[jax.ShapeDtypeStruct((S_, N_HEADS * HEAD_PAD), F32),
                   jax.ShapeDtypeStruct((S_, N_HEADS * HEAD_PAD), F32),
                   jax.ShapeDtypeStruct((S_, N_HEADS * V_HEAD), F32)],
        compiler_params=_cparams("parallel", "arbitrary"), name=name,
    )(q, do, lse_row, delta_row, k, vx)


def mods_fwd(c_all, mod_w, mod_b, name, tn=512):
    L, Dn, E = mod_w.shape
    R = c_all.shape[0]
    tn = _pick(E, tn, LANES)

    def body(c_ref, w_ref, b_ref, o_ref):
        cv = c_ref[...]
        sc = (cv / (1.0 + jnp.exp(-cv))).astype(BF16)
        o_ref[...] = lax.dot_general(sc, w_ref[...].astype(BF16), _DIMS["nn"], preferred_element_type=F32) + b_ref[...]

    return pl.pallas_call(
        body, grid=(L, E // tn),
        in_specs=[pl.BlockSpec((R, Dn), lambda l, j: (0, 0)), pl.BlockSpec((None, Dn, tn), lambda l, j: (l, 0, j)),
                  pl.BlockSpec((None, 1, tn), lambda l, j: (l, 0, j))],
        out_specs=pl.BlockSpec((None, R, tn), lambda l, j: (l, 0, j)),
        out_shape=jax.ShapeDtypeStruct((L, R, E), F32),
        compiler_params=_cparams("parallel", "parallel"), name=name,
    )(c_all, mod_w, mod_b.reshape(L, 1, E))


def _adam_math(w, g, m, v):
    m = ADAM_B1 * m + (1.0 - ADAM_B1) * g
    v = ADAM_B2 * v + (1.0 - ADAM_B2) * (g * g)
    m_hat = m / (1.0 - ADAM_B1 ** ADAM_STEP)
    v_hat = v / (1.0 - ADAM_B2 ** ADAM_STEP)
    delta = -ADAM_LR * (m_hat / (jnp.sqrt(v_hat) + ADAM_EPS) + ADAM_WD * w)
    return delta, m, v


def _as2d(a):
    return a.reshape(-1, a.shape[-1]) if a.ndim != 2 else a


def adamw(w, g, m, v, name):
    shape = w.shape
    w2, g2, m2, v2 = _as2d(w), _as2d(g), _as2d(m), _as2d(v)
    R, C = w2.shape
    tr = _pick(R, max(8, (1 << 18) // C // 8 * 8), 8)

    def body(w_ref, g_ref, m_ref, v_ref, d_ref, mo_ref, vo_ref):
        d, mn, vn = _adam_math(w_ref[...], g_ref[...], m_ref[...], v_ref[...])
        d_ref[...] = d
        mo_ref[...] = mn
        vo_ref[...] = vn

    blk = pl.BlockSpec((tr, C), lambda i: (i, 0))
    shp = jax.ShapeDtypeStruct((R, C), F32)
    outs = pl.pallas_call(
        body, grid=(R // tr,), in_specs=[blk] * 4, out_specs=[blk] * 3, out_shape=[shp] * 3,
        compiler_params=_cparams("parallel"), name=name,
    )(w2, g2, m2, v2)
    return tuple(o.reshape(shape) for o in outs)


def adamw_sum(parts, w, m, v, name):
    P, R, C = parts.shape

    def body(p_ref, w_ref, m_ref, v_ref, g_ref, d_ref, mo_ref, vo_ref):
        g = p_ref[0]
        for k in range(1, P):
            g = g + p_ref[k]
        d, mn, vn = _adam_math(w_ref[...], g, m_ref[...], v_ref[...])
        g_ref[...] = g
        d_ref[...] = d
        mo_ref[...] = mn
        vo_ref[...] = vn

    shp = jax.ShapeDtypeStruct((R, C), F32)
    return pl.pallas_call(body, out_shape=[shp] * 4, compiler_params=_cparams(), name=name)(parts, w, m, v)


def adamw_modw(c_col, dm, w, m, v, name, tr=256, tn=512):
    L, Dn, E = w.shape
    B = c_col.shape[0]
    tr = _pick(Dn, tr, 8)
    tn = _pick(E, tn, LANES)

    def body(c_ref, dm_ref, w_ref, m_ref, v_ref, g_ref, d_ref, mo_ref, vo_ref):
        g = jnp.zeros((tr, tn), F32)
        for b in range(B):
            cv = c_ref[b]
            g = g + (cv / (1.0 + jnp.exp(-cv))) * dm_ref[b:b + 1, :]
        d, mn, vn = _adam_math(w_ref[...], g, m_ref[...], v_ref[...])
        g_ref[...] = g
        d_ref[...] = d
        mo_ref[...] = mn
        vo_ref[...] = vn

    blk = pl.BlockSpec((None, tr, tn), lambda l, i, j: (l, i, j))
    shp = jax.ShapeDtypeStruct((L, Dn, E), F32)
    return pl.pallas_call(
        body, grid=(L, Dn // tr, E // tn),
        in_specs=[pl.BlockSpec((B, tr, 1), lambda l, i, j: (0, i, 0)),
                  pl.BlockSpec((None, B, tn), lambda l, i, j: (l, 0, j)), blk, blk, blk],
        out_specs=[blk] * 4, out_shape=[shp] * 4,
        compiler_params=_cparams("parallel", "parallel", "parallel"), name=name,
    )(c_col, dm, w, m, v)


def add_round(a, b, name, tr=512):
    R, C = a.shape
    tr = _pick(R, tr, 16)

    def body(a_ref, b_ref, o_ref):
        o_ref[...] = (a_ref[...] + b_ref[...].astype(F32)).astype(BF16)

    blk = pl.BlockSpec((tr, C), lambda i: (i, 0))
    return pl.pallas_call(
        body, grid=(R // tr,), in_specs=[blk, blk], out_specs=blk, out_shape=jax.ShapeDtypeStruct((R, C), BF16),
        compiler_params=_cparams("parallel"), name=name,
    )(a, b)


def sum_parts(parts, name, tr=512):
    P, R, C = parts.shape
    tr = _pick(R, tr, 16)

    def body(p_ref, o_ref):
        s = p_ref[0].astype(F32)
        for k in range(1, P):
            s = s + p_ref[k].astype(F32)
        o_ref[...] = s

    return pl.pallas_call(
        body, grid=(R // tr,), in_specs=[pl.BlockSpec((P, tr, C), lambda i: (0, i, 0))],
        out_specs=pl.BlockSpec((tr, C), lambda i: (i, 0)), out_shape=jax.ShapeDtypeStruct((R, C), F32),
        compiler_params=_cparams("parallel"), name=name,
    )(parts)


_ANY = pl.BlockSpec(memory_space=pl.ANY)


def _place():
    return lax.axis_index("x"), lax.axis_index("y"), lax.axis_index("c")


def _flip(v, bit):
    return 1 - v if bit else v


def chip_gather(buf, name):
    def body(in_ref, out_ref, send_sems, recv_sems):
        x, y, c = _place()
        me = 2 * x + y
        sends = []
        for k in range(1, N_CHIPS):
            px, py = _flip(x, k >> 1), _flip(y, k & 1)
            cp = pltpu.make_async_remote_copy(src_ref=in_ref, dst_ref=out_ref.at[me], send_sem=send_sems.at[k - 1],
                                              recv_sem=recv_sems.at[k - 1], device_id=(px, py, c), device_id_type=MESH)
            cp.start()
            sends.append(cp)
        for k in range(1, N_CHIPS):
            px, py = _flip(x, k >> 1), _flip(y, k & 1)
            pltpu.make_async_remote_copy(src_ref=in_ref, dst_ref=out_ref.at[2 * px + py], send_sem=send_sems.at[k - 1],
                                         recv_sem=recv_sems.at[k - 1], device_id=(px, py, c),
                                         device_id_type=MESH).wait_recv()
        for cp in sends:
            cp.wait_send()

    out = pl.pallas_call(
        body, in_specs=[_ANY], out_specs=_ANY,
        out_shape=jax.ShapeDtypeStruct((N_CHIPS,) + buf.shape, buf.dtype),
        scratch_shapes=[pltpu.SemaphoreType.DMA((N_CHIPS - 1,)), pltpu.SemaphoreType.DMA((N_CHIPS - 1,))],
        name=name,
    )(buf)
    return lax.dynamic_update_index_in_dim(out, buf, 2 * lax.axis_index("x") + lax.axis_index("y"), 0)


def chip_all_to_all(buf, name):
    def body(in_ref, out_ref, send_sems, recv_sems):
        x, y, c = _place()
        me = 2 * x + y
        sends = []
        for k in range(1, N_CHIPS):
            px, py = _flip(x, k >> 1), _flip(y, k & 1)
            cp = pltpu.make_async_remote_copy(src_ref=in_ref.at[2 * px + py], dst_ref=out_ref.at[me],
                                              send_sem=send_sems.at[k - 1], recv_sem=recv_sems.at[k - 1],
                                              device_id=(px, py, c), device_id_type=MESH)
            cp.start()
            sends.append(cp)
        for k in range(1, N_CHIPS):
            px, py = _flip(x, k >> 1), _flip(y, k & 1)
            pltpu.make_async_remote_copy(src_ref=in_ref.at[me], dst_ref=out_ref.at[2 * px + py],
                                         send_sem=send_sems.at[k - 1], recv_sem=recv_sems.at[k - 1],
                                         device_id=(px, py, c), device_id_type=MESH).wait_recv()
        for cp in sends:
            cp.wait_send()

    out = pl.pallas_call(
        body, in_specs=[_ANY], out_specs=_ANY, out_shape=jax.ShapeDtypeStruct(buf.shape, buf.dtype),
        scratch_shapes=[pltpu.SemaphoreType.DMA((N_CHIPS - 1,)), pltpu.SemaphoreType.DMA((N_CHIPS - 1,))],
        name=name,
    )(buf)
    me = 2 * lax.axis_index("x") + lax.axis_index("y")
    return lax.dynamic_update_index_in_dim(out, _index(buf, me), me, 0)


def core_gather(buf, name):
    def body(in_ref, out_ref, send_sem, recv_sem):
        x, y, c = _place()
        cp = pltpu.make_async_remote_copy(src_ref=in_ref, dst_ref=out_ref.at[c], send_sem=send_sem, recv_sem=recv_sem,
                                          device_id=(x, y, 1 - c), device_id_type=MESH)
        cp.start()
        pltpu.make_async_remote_copy(src_ref=in_ref, dst_ref=out_ref.at[1 - c], send_sem=send_sem, recv_sem=recv_sem,
                                     device_id=(x, y, 1 - c), device_id_type=MESH).wait_recv()
        cp.wait_send()

    out = pl.pallas_call(
        body, in_specs=[_ANY], out_specs=_ANY, out_shape=jax.ShapeDtypeStruct((2,) + buf.shape, buf.dtype),
        scratch_shapes=[pltpu.SemaphoreType.DMA, pltpu.SemaphoreType.DMA],
        name=name,
    )(buf)
    return lax.dynamic_update_index_in_dim(out, buf, lax.axis_index("c"), 0)


def core_swap(buf, name):
    def body(in_ref, out_ref, send_sem, recv_sem):
        x, y, c = _place()
        cp = pltpu.make_async_remote_copy(src_ref=in_ref, dst_ref=out_ref, send_sem=send_sem, recv_sem=recv_sem,
                                          device_id=(x, y, 1 - c), device_id_type=MESH)
        cp.start()
        cp.wait()

    return pl.pallas_call(
        body, in_specs=[_ANY], out_specs=_ANY, out_shape=jax.ShapeDtypeStruct(buf.shape, buf.dtype),
        scratch_shapes=[pltpu.SemaphoreType.DMA, pltpu.SemaphoreType.DMA],
        name=name,
    )(buf)


def device_gather(buf, name):
    def body(in_ref, out_ref, send_sems, recv_sems, local_sem):
        x, y, c = _place()
        me = 4 * x + 2 * y + c
        mine = pltpu.make_async_copy(in_ref, out_ref.at[me], local_sem)
        mine.start()
        sends = []
        for k in range(1, N_DEV):
            peer = (_flip(x, (k >> 2) & 1), _flip(y, (k >> 1) & 1), _flip(c, k & 1))
            cp = pltpu.make_async_remote_copy(src_ref=in_ref, dst_ref=out_ref.at[me], send_sem=send_sems.at[k - 1],
                                              recv_sem=recv_sems.at[k - 1], device_id=peer, device_id_type=MESH)
            cp.start()
            sends.append(cp)
        for k in range(1, N_DEV):
            peer = (_flip(x, (k >> 2) & 1), _flip(y, (k >> 1) & 1), _flip(c, k & 1))
            pltpu.make_async_remote_copy(src_ref=in_ref, dst_ref=out_ref.at[4 * peer[0] + 2 * peer[1] + peer[2]],
                                         send_sem=send_sems.at[k - 1], recv_sem=recv_sems.at[k - 1], device_id=peer,
                                         device_id_type=MESH).wait_recv()
        for cp in sends:
            cp.wait_send()
        mine.wait()

    return pl.pallas_call(
        body, in_specs=[_ANY], out_specs=_ANY, out_shape=jax.ShapeDtypeStruct((N_DEV,) + buf.shape, buf.dtype),
        scratch_shapes=[pltpu.SemaphoreType.DMA((N_DEV - 1,)), pltpu.SemaphoreType.DMA((N_DEV - 1,)),
                        pltpu.SemaphoreType.DMA],
        name=name,
    )(buf)


def _region(ref, chip_axis=None, chip=None, chip_size=None, half_axis=None, half=None, half_size=None):
    idx = [slice(None)] * len(ref.shape)
    if chip is not None:
        idx[chip_axis] = pl.ds(chip * chip_size, chip_size)
    if half is not None:
        idx[half_axis] = pl.ds(half * half_size, half_size)
    return ref.at[tuple(idx)]


def gather_weights(shards, axes, name, after=()):
    n = len(shards)

    def full_shape(t):
        shp = list(shards[t].shape)
        shp[axes[t][0]] *= N_CHIPS
        return tuple(shp)

    def body(*refs):
        ins, outs = refs[:n], refs[n + len(after):2 * n + len(after)]
        ici_send, ici_recv, d2d_send, d2d_recv, own_send, own_recv = refs[2 * n + len(after):]
        x, y, c = _place()
        me = 2 * x + y

        def part(t, ref, chip, half):
            ca, ha = axes[t]
            return _region(ref, ca, chip, ins[t].shape[ca], ha, half, ins[t].shape[ha] // 2)

        def own(t):
            return pltpu.make_async_remote_copy(src_ref=ins[t], dst_ref=part(t, outs[t], me, None),
                                                send_sem=own_send.at[t], recv_sem=own_recv.at[t],
                                                device_id=(x, y, 1 - c), device_id_type=MESH)

        started = []
        for t in range(n):
            own(t).start()
            started.append(own(t))
        for t in range(n):
            for k in range(1, N_CHIPS):
                px, py = _flip(x, k >> 1), _flip(y, k & 1)
                cp = pltpu.make_async_remote_copy(src_ref=part(t, ins[t], None, c), dst_ref=part(t, outs[t], me, c),
                                                  send_sem=ici_send.at[t, k - 1], recv_sem=ici_recv.at[t, k - 1],
                                                  device_id=(px, py, c), device_id_type=MESH)
                cp.start()
                started.append(cp)
        for t in range(n):
            for k in range(1, N_CHIPS):
                px, py = _flip(x, k >> 1), _flip(y, k & 1)
                got = part(t, outs[t], 2 * px + py, c)
                pltpu.make_async_remote_copy(src_ref=part(t, ins[t], None, c), dst_ref=got,
                                             send_sem=ici_send.at[t, k - 1], recv_sem=ici_recv.at[t, k - 1],
                                             device_id=(px, py, c), device_id_type=MESH).wait_recv()
                fw = pltpu.make_async_remote_copy(src_ref=got, dst_ref=got, send_sem=d2d_send.at[t, k - 1],
                                                  recv_sem=d2d_recv.at[t, k - 1], device_id=(x, y, 1 - c),
                                                  device_id_type=MESH)
                fw.start()
                started.append(fw)
        for t in range(n):
            for k in range(1, N_CHIPS):
                px, py = _flip(x, k >> 1), _flip(y, k & 1)
                theirs = part(t, outs[t], 2 * px + py, 1 - c)
                pltpu.make_async_remote_copy(src_ref=theirs, dst_ref=theirs, send_sem=d2d_send.at[t, k - 1],
                                             recv_sem=d2d_recv.at[t, k - 1], device_id=(x, y, 1 - c),
                                             device_id_type=MESH).wait_recv()
        for t in range(n):
            own(t).wait_recv()
        for cp in started:
            cp.wait_send()

    sem = pltpu.SemaphoreType.DMA((n, N_CHIPS - 1))
    own_sem = pltpu.SemaphoreType.DMA((n,))
    return pl.pallas_call(
        body, in_specs=[_ANY] * (n + len(after)), out_specs=[_ANY] * n,
        out_shape=[jax.ShapeDtypeStruct(full_shape(t), shards[t].dtype) for t in range(n)],
        scratch_shapes=[sem, sem, sem, sem, own_sem, own_sem], name=name,
    )(*shards, *after)


_HBM = pl.BlockSpec(memory_space=pltpu.HBM)
_SEM = pl.BlockSpec(memory_space=pltpu.SEMAPHORE)
_EFFECT = pltpu.SideEffectType.DATAFLOW_SIDE_EFFECTING
WEIGHT_COPIES = N_CHIPS


def _weight_peer(k, x, y, c):
    return (x, y, 1 - c) if k == 0 else (_flip(x, k >> 1), _flip(y, k & 1), c)


def weights_start(shards, items, name, after=()):
    n_sh, n_it = len(shards), len(items)

    def src_of(refs, i):
        t, layer, _ = items[i]
        return refs[t] if layer is None else refs[t].at[layer]

    def land_shape(i):
        t, layer, ca = items[i]
        shp = list(shards[t].shape if layer is None else shards[t].shape[1:])
        shp[ca] *= N_CHIPS
        return tuple(shp)

    def body(*refs):
        shard_refs, land_refs = refs[:n_sh], refs[n_sh:n_sh + n_it]
        first_out = n_sh + n_it + len(after)
        send_sems = refs[first_out:first_out + n_it]
        recv_sems = refs[first_out + n_it:first_out + 2 * n_it]
        token = refs[-1]
        x, y, c = _place()
        me = 2 * x + y
        for i in range(n_it):
            src = src_of(shard_refs, i)
            ca = items[i][2]
            dst = _region(land_refs[i], ca, me, src.shape[ca])
            for k in range(WEIGHT_COPIES):
                pltpu.make_async_remote_copy(src_ref=src, dst_ref=dst, send_sem=send_sems[i], recv_sem=recv_sems[i],
                                             device_id=_weight_peer(k, x, y, c), device_id_type=MESH).start()
        token[...] = jnp.zeros_like(token)

    lands = [pltpu.with_memory_space_constraint(lax.empty(land_shape(i), shards[0].dtype), pltpu.HBM)
             for i in range(n_it)]
    ins = [pltpu.with_memory_space_constraint(a, pltpu.HBM) for a in shards] + lands
    sems = (pltpu.SemaphoreType.DMA(()),) * (2 * n_it)
    outs = pl.pallas_call(
        body, name=name,
        out_shape=sems + tuple(pltpu.HBM(a.shape, a.dtype) for a in ins) + (jax.ShapeDtypeStruct((8, LANES), F32),),
        in_specs=[_HBM] * len(ins) + [_ANY] * len(after),
        out_specs=(_SEM,) * (2 * n_it) + (_HBM,) * len(ins) + (pl.BlockSpec(memory_space=pltpu.VMEM),),
        input_output_aliases={i: 2 * n_it + i for i in range(len(ins))},
        compiler_params=pltpu.CompilerParams(has_side_effects=_EFFECT),
    )(*ins, *after)
    base = 2 * n_it
    return (list(outs[:n_it]), list(outs[n_it:base]), list(outs[base:base + n_sh]),
            list(outs[base + n_sh:base + n_sh + n_it]), outs[-1])


def weights_wait(send_sems, recv_sems, lands, after, keep, name):
    m = len(lands)

    def body(*refs):
        land_refs, send_refs, recv_refs = refs[:m], refs[m:2 * m], refs[2 * m:3 * m]
        x, y, c = _place()
        for j in range(m):
            cp = pltpu.make_async_remote_copy(src_ref=land_refs[j], dst_ref=land_refs[j], send_sem=send_refs[j],
                                              recv_sem=recv_refs[j], device_id=(x, y, 1 - c),
                                              device_id_type=MESH)
            cp.wait_send()
            cp.wait_recv()

    outs = pl.pallas_call(
        body, name=name,
        out_shape=tuple(pltpu.HBM(a.shape, a.dtype) for a in lands),
        in_specs=[_HBM] * m + [_SEM] * (2 * m) + [_ANY] + [_HBM] * len(keep),
        out_specs=(_HBM,) * m,
        input_output_aliases={j: j for j in range(m)},
        compiler_params=pltpu.CompilerParams(has_side_effects=_EFFECT),
    )(*lands, *send_sems, *recv_sems, after, *keep)
    return list(outs)


def reduce_to_sibling(lo, hi, name):
    n = len(lo)

    def body(*refs):
        los, his, outs = refs[:n], refs[n:2 * n], refs[2 * n:3 * n]
        send_sems, recv_sems = refs[3 * n:]
        x, y, c = _place()

        def copy(u, src):
            return pltpu.make_async_remote_copy(src_ref=src, dst_ref=outs[u], send_sem=send_sems.at[u],
                                                recv_sem=recv_sems.at[u], device_id=(x, y, 1 - c), device_id_type=MESH)

        for u in range(n):
            @pl.when(c == 0)
            def _(u=u):
                copy(u, his[u]).start()

            @pl.when(c == 1)
            def _(u=u):
                copy(u, los[u]).start()
        for u in range(n):
            copy(u, los[u]).wait_recv()
        for u in range(n):
            copy(u, los[u]).wait_send()

    return pl.pallas_call(
        body, in_specs=[_ANY] * (2 * n), out_specs=[_ANY] * n,
        out_shape=[jax.ShapeDtypeStruct(a.shape, a.dtype) for a in lo],
        scratch_shapes=[pltpu.SemaphoreType.DMA((n,)), pltpu.SemaphoreType.DMA((n,))], name=name,
    )(*lo, *hi)


def add_selected(lo, hi, other, name, tile_elems=1 << 19):
    R, C = lo.shape
    tr = _pick(R, max(16, tile_elems // C // 16 * 16), 16)

    def body(lo_ref, hi_ref, o_ref, out_ref):
        mine = jnp.where(lax.axis_index("c") == 0, lo_ref[...].astype(F32), hi_ref[...].astype(F32))
        out_ref[...] = (mine + o_ref[...].astype(F32)).astype(out_ref.dtype)

    blk = pl.BlockSpec((tr, C), lambda i: (i, 0))
    return pl.pallas_call(
        body, grid=(R // tr,), in_specs=[blk, blk, blk], out_specs=blk, out_shape=jax.ShapeDtypeStruct((R, C), BF16),
        compiler_params=_cparams("parallel"), name=name,
    )(lo, hi, other)


def scatter_to_chips(pieces, chip_axes, name):
    n = len(pieces)

    def block_shape(u):
        shp = list(pieces[u].shape)
        shp[chip_axes[u]] //= N_CHIPS
        return tuple(shp)

    def body(*refs):
        ins, outs = refs[:n], refs[n:2 * n]
        send_sems, recv_sems = refs[2 * n:]
        x, y, c = _place()
        me = 2 * x + y
        started = []
        for u in range(n):
            size = block_shape(u)[chip_axes[u]]
            for k in range(1, N_CHIPS):
                px, py = _flip(x, k >> 1), _flip(y, k & 1)
                cp = pltpu.make_async_remote_copy(src_ref=_region(ins[u], chip_axes[u], 2 * px + py, size),
                                                  dst_ref=outs[u].at[me], send_sem=send_sems.at[u, k - 1],
                                                  recv_sem=recv_sems.at[u, k - 1], device_id=(px, py, c),
                                                  device_id_type=MESH)
                cp.start()
                started.append(cp)
        for u in range(n):
            size = block_shape(u)[chip_axes[u]]
            for k in range(1, N_CHIPS):
                px, py = _flip(x, k >> 1), _flip(y, k & 1)
                pltpu.make_async_remote_copy(src_ref=_region(ins[u], chip_axes[u], me, size),
                                             dst_ref=outs[u].at[2 * px + py], send_sem=send_sems.at[u, k - 1],
                                             recv_sem=recv_sems.at[u, k - 1], device_id=(px, py, c),
                                             device_id_type=MESH).wait_recv()
        for cp in started:
            cp.wait_send()

    sem = pltpu.SemaphoreType.DMA((n, N_CHIPS - 1))
    return pl.pallas_call(
        body, in_specs=[_ANY] * n, out_specs=[_ANY] * n,
        out_shape=[jax.ShapeDtypeStruct((N_CHIPS,) + block_shape(u), pieces[u].dtype) for u in range(n)],
        scratch_shapes=[sem, sem], name=name,
    )(*pieces)


def scatter_start(pieces, chip_axes, name, after=()):
    n = len(pieces)

    def block_shape(u):
        shp = list(pieces[u].shape)
        shp[chip_axes[u]] //= N_CHIPS
        return tuple(shp)

    def body(*refs):
        ins, land_refs = refs[:n], refs[n:2 * n]
        first_out = 2 * n + len(after)
        send_sems, recv_sems = refs[first_out:first_out + n], refs[first_out + n:first_out + 2 * n]
        token = refs[-1]
        x, y, c = _place()
        me = 2 * x + y
        for u in range(n):
            size = block_shape(u)[chip_axes[u]]
            for k in range(1, N_CHIPS):
                px, py = _flip(x, k >> 1), _flip(y, k & 1)
                pltpu.make_async_remote_copy(src_ref=_region(ins[u], chip_axes[u], 2 * px + py, size),
                                             dst_ref=land_refs[u].at[me], send_sem=send_sems[u], recv_sem=recv_sems[u],
                                             device_id=(px, py, c), device_id_type=MESH).start()
        token[...] = jnp.zeros_like(token)

    lands = [pltpu.with_memory_space_constraint(lax.empty((N_CHIPS,) + block_shape(u), pieces[u].dtype), pltpu.HBM)
             for u in range(n)]
    ins = [pltpu.with_memory_space_constraint(a, pltpu.HBM) for a in pieces] + lands
    sems = (pltpu.SemaphoreType.DMA(()),) * (2 * n)
    outs = pl.pallas_call(
        body, name=name,
        out_shape=sems + tuple(pltpu.HBM(a.shape, a.dtype) for a in ins) + (jax.ShapeDtypeStruct((8, LANES), F32),),
        in_specs=[_HBM] * len(ins) + [_ANY] * len(after),
        out_specs=(_SEM,) * (2 * n) + (_HBM,) * len(ins) + (pl.BlockSpec(memory_space=pltpu.VMEM),),
        input_output_aliases={i: 2 * n + i for i in range(len(ins))},
        compiler_params=pltpu.CompilerParams(has_side_effects=_EFFECT),
    )(*ins, *after)
    return list(outs[:n]), list(outs[n:2 * n]), list(outs[2 * n:3 * n]), list(outs[3 * n:4 * n]), outs[-1]


def scatter_wait(send_sems, recv_sems, lands, pieces, after, name):
    n = len(lands)

    def body(*refs):
        land_refs, send_refs, recv_refs = refs[:n], refs[n:2 * n], refs[2 * n:3 * n]
        x, y, c = _place()
        for u in range(n):
            three = land_refs[u].at[pl.ds(0, N_CHIPS - 1)]
            cp = pltpu.make_async_remote_copy(src_ref=three, dst_ref=three, send_sem=send_refs[u], recv_sem=recv_refs[u],
                                              device_id=(x, y, 1 - c), device_id_type=MESH)
            cp.wait_send()
            cp.wait_recv()

    outs = pl.pallas_call(
        body, name=name,
        out_shape=tuple(pltpu.HBM(a.shape, a.dtype) for a in lands),
        in_specs=[_HBM] * n + [_SEM] * (2 * n) + [_ANY] + [_HBM] * len(pieces),
        out_specs=(_HBM,) * n,
        input_output_aliases={j: j for j in range(n)},
        compiler_params=pltpu.CompilerParams(has_side_effects=_EFFECT),
    )(*lands, *send_sems, *recv_sems, after, *pieces)
    return list(outs)


def gather_halves(parts, slots, out_shapes, name):
    n = len(parts)

    def body(*refs):
        ins, outs = refs[:n], refs[n:n + len(out_shapes)]
        send_sems, recv_sems = refs[n + len(out_shapes):]
        x, y, c = _place()
        started = []
        for u in range(n):
            t, s = slots[u]
            cp = pltpu.make_async_remote_copy(src_ref=ins[u], dst_ref=outs[t].at[s, c], send_sem=send_sems.at[u],
                                              recv_sem=recv_sems.at[u], device_id=(x, y, 1 - c), device_id_type=MESH)
            cp.start()
            started.append(cp)
        for u in range(n):
            t, s = slots[u]
            pltpu.make_async_remote_copy(src_ref=ins[u], dst_ref=outs[t].at[s, 1 - c], send_sem=send_sems.at[u],
                                         recv_sem=recv_sems.at[u], device_id=(x, y, 1 - c),
                                         device_id_type=MESH).wait_recv()
        for cp in started:
            cp.wait_send()

    return pl.pallas_call(
        body, in_specs=[_ANY] * n, out_specs=[_ANY] * len(out_shapes),
        out_shape=[jax.ShapeDtypeStruct(shp, F32) for shp in out_shapes],
        scratch_shapes=[pltpu.SemaphoreType.DMA((n,)), pltpu.SemaphoreType.DMA((n,))], name=name,
    )(*parts)


WEIGHT_ORDER = ["mod_w", "mod_b", "norm1_g", "norm2_g", "pool_w", "pool_b", "pool_scale", "kv_in_g", "w_dkv",
                "ckv_norm_g", "w_uk", "w_uv", "w_dq", "q_norm_g", "w_uq", "w_o", "w_up", "conv_w", "conv_b", "w_down",
                "final_g"]
EXCHANGED = {"w_up": (2, 0), "w_down": (1, 0), "w_o": (1, 0), "w_uq": (2, 0), "w_dq": (1, 0), "pool_w": (2, 0),
             "w_dkv": (0, 1), "w_uk": (1, 0), "w_uv": (1, 0)}
SMALL_SHARDED = {"conv_w": 2, "pool_b": 1, "pool_scale": 1}
REPLICATED = ["mod_b", "norm1_g", "norm2_g", "kv_in_g", "ckv_norm_g", "q_norm_g", "conv_b", "final_g"]


def _padded(n, align):
    return -(-n // align) * align


def _flat_pad(parts, total):
    flat = jnp.concatenate(parts, axis=-1)
    pad = total - flat.shape[-1]
    if pad:
        flat = jnp.concatenate([flat, jnp.zeros(flat.shape[:-1] + (pad,), flat.dtype)], axis=-1)
    return flat


def _split_shards(full, axis):
    shp = full.shape
    t = full.reshape(shp[:axis] + (N_CHIPS, shp[axis] // N_CHIPS) + shp[axis + 1:])
    return jnp.moveaxis(t, axis, 0).reshape(N_CHIPS, -1)


def _join_shards(rows, shard_shape, axis):
    t = jnp.moveaxis(rows.reshape((N_CHIPS,) + tuple(shard_shape)), 0, axis)
    return t.reshape(tuple(shard_shape[:axis]) + (N_CHIPS * shard_shape[axis],) + tuple(shard_shape[axis + 1:]))


def _index(a, i, axis=0):
    return lax.dynamic_index_in_dim(a, i, axis, keepdims=False)


def kernel(x, c, positions, mod_w, mod_b, norm1_g, norm2_g, pool_w, pool_b, pool_scale, kv_in_g, w_dkv, ckv_norm_g, w_uk, w_uv, w_dq, q_norm_g, w_uq, w_o, w_up, conv_w, conv_b, w_down, final_g, loss_target, m_mod_w, m_mod_b, m_norm1_g, m_norm2_g, m_pool_w, m_pool_b, m_pool_scale, m_kv_in_g, m_w_dkv, m_ckv_norm_g, m_w_uk, m_w_uv, m_w_dq, m_q_norm_g, m_w_uq, m_w_o, m_w_up, m_conv_w, m_conv_b, m_w_down, m_final_g, v_mod_w, v_mod_b, v_norm1_g, v_norm2_g, v_pool_w, v_pool_b, v_pool_scale, v_kv_in_g, v_w_dkv, v_ckv_norm_g, v_w_uk, v_w_uv, v_w_dq, v_q_norm_g, v_w_uq, v_w_o, v_w_up, v_conv_w, v_conv_b, v_w_down, v_final_g):
    given = dict(locals())
    W = {n: given[n] for n in WEIGHT_ORDER}
    M1 = {n: given["m_" + n] for n in WEIGHT_ORDER}
    V2 = {n: given["v_" + n] for n in WEIGHT_ORDER}
    xi, yi, ci = lax.axis_index("x"), lax.axis_index("y"), lax.axis_index("c")
    chip = 2 * xi + yi
    dev = 4 * xi + 2 * yi + ci
    x0 = x[0]
    S_, D = x0.shape
    Fh = conv_b.shape[1]
    E = mod_b.shape[1]
    Es = E // N_CHIPS
    zD = jnp.zeros((D,), F32)

    c_all = device_gather(c, "gather_c").reshape(N_DEV, D)
    c_pad = jnp.concatenate([c_all, jnp.zeros((16 - N_DEV, D), F32)], axis=0)
    mod_b_mine = lax.dynamic_slice_in_dim(mod_b, chip * Es, Es, axis=1)
    mods_part = mods_fwd(c_pad, mod_w, mod_b_mine, "mods_fwd")
    mods_all = chip_gather(mods_part, "gather_mods")
    mods = jnp.swapaxes(_index(mods_all, dev, axis=2), 0, 1).reshape(DEPTH, E)
    mod = [[mods[l, k * D:(k + 1) * D] for k in range(6)] for l in range(DEPTH)]

    full = {}
    ssz = {n: math.prod(W[n].shape) for n in SMALL_SHARDED}
    Tw = _padded(sum(ssz.values()), 8 * PACK_COLS)
    small_rows = chip_gather(_flat_pad([W[n].reshape(-1) for n in SMALL_SHARDED], Tw).reshape(-1, PACK_COLS),
                             "gather_small_w").reshape(N_CHIPS, Tw)
    off = 0
    for n, axis in SMALL_SHARDED.items():
        full[n] = _join_shards(small_rows[:, off:off + ssz[n]], W[n].shape, axis)
        off += ssz[n]

    names = list(EXCHANGED)
    shards = [W[n].astype(BF16) for n in names]
    n_mla = DEPTH - N_A
    first_axes = {"w_up": (1, 0), "w_down": (0, 1), "pool_w": (1, 0)}
    first = gather_weights([shards[names.index(n)][0] for n in first_axes], list(first_axes.values()), "gather_weights0",
                           after=[mods, small_rows])
    for n, arr in zip(first_axes, first):
        full[(n, 0)] = arr
    items, groups = [], []

    def group(entries):
        groups.append(list(range(len(items), len(items) + len(entries))))
        for n, layer in entries:
            ca = EXCHANGED[n][0] - (0 if layer is None else 1)
            items.append((names.index(n), layer, 0 if n == "w_dkv" else ca))

    for l in range(1, N_A):
        group([("w_up", l), ("w_down", l), ("pool_w", l)])
    for j in range(n_mla):
        head = [("w_dkv", None), ("w_uk", None), ("w_uv", None)] if j == 0 else []
        group(head + [("w_dq", j), ("w_uq", j), ("w_o", j), ("w_up", N_A + j), ("w_down", N_A + j)])
    w_send, w_recv, shards_thru, lands, _ = weights_start(shards, items, "weights_start", after=first)

    def weights_ready(g, after):
        keep = shards_thru if g == len(groups) - 1 else []
        got = weights_wait([w_send[i] for i in groups[g]], [w_recv[i] for i in groups[g]], [lands[i] for i in groups[g]],
                           after, keep, f"weights_wait{g}")
        for i, arr in zip(groups[g], got):
            t, layer, _ = items[i]
            full[(names[t], 0 if layer is None else layer)] = arr

    q_rank = W["w_uq"].shape[1]
    kv_w = KV_RANK + QK_ROPE

    def uq_ext(j):
        wq = full[("w_uq", j)].reshape(q_rank, N_HEADS, QK_HEAD)
        return jnp.concatenate([wq, jnp.zeros((q_rank, N_HEADS, HEAD_PAD - QK_HEAD), BF16)],
                               axis=2).reshape(q_rank, N_HEADS * HEAD_PAD)


    half = QK_ROPE // 2
    inv = 1.0 / (ROPE_THETA ** (jnp.arange(0, QK_ROPE, 2, dtype=F32) / QK_ROPE))
    inv_row = jnp.concatenate([inv, inv, jnp.zeros((LANES - 2 * half,), F32)]).reshape(1, LANES)
    tabs = rope_tables(positions[0].astype(F32).reshape(S_, 1), inv_row, "rope_tables")
    att_scale = QK_HEAD ** -0.5

    saved = []
    xcur = x0
    kv_saved = None
    K = VX = knv = None
    for l in range(DEPTH):
        sh1, sc1, g1, sh2, sc2, g2 = mod[l]
        st = {"xin": xcur}
        if l:
            weights_ready(l - 1, xcur)
        if l == N_A:
            w_dkv_ext = jnp.concatenate([full[("w_dkv", 0)], jnp.zeros((D, KV_RANK + LANES - kv_w), BF16)], axis=1)
            w_ukv = jnp.concatenate([full[("w_uk", 0)], full[("w_uv", 0)]], axis=1)
            xn = norm_fwd(xcur, kv_in_g, zD, zD, BF16, "kvin_fwd")
            kv_ext = mm(xn, w_dkv_ext, "nn", F32, "dkv_mm")
            lat = kv_ext[:, :KV_RANK]
            zk = jnp.zeros((KV_RANK,), F32)
            ckv = norm_fwd(lat, ckv_norm_g, zk, zk, BF16, "ckv_fwd")
            K, VX = kv_proj(ckv, w_ukv, kv_ext, tabs, "ukv_mm")
            kv_saved = {"x": xcur, "xn": xn, "lat": lat, "ckv": ckv}
        if l < N_A:
            h1 = norm_fwd(xcur, norm1_g[l], sc1, sh1, F32, f"norm1_fwd{l}")
            st["pooled"] = _pool_call(h1, BF16, f"pool_fwd{l}", False)
            st["cs"] = g1 * full["pool_scale"][l]
            st["ypre"], xmid = gmm(st["pooled"], full[("pool_w", l)], "nn", BF16, f"pool_mm{l}", bias=full["pool_b"][l],
                                   res=xcur, colscale=st["cs"])
        else:
            j = l - N_A
            st["h1"] = norm_fwd(xcur, norm1_g[l], sc1, sh1, BF16, f"norm1_fwd{l}")
            st["ql"] = mm(st["h1"], full[("w_dq", j)], "nn", F32, f"dq_mm{l}")
            st["cq"] = norm_fwd(st["ql"], q_norm_g[j], jnp.zeros_like(q_norm_g[j]), jnp.zeros_like(q_norm_g[j]), BF16,
                                f"qnorm_fwd{l}")
            st["w_uq_ext"] = uq_ext(j)
            st["Q"] = q_proj(st["cq"], st["w_uq_ext"], tabs, att_scale, f"uq_mm{l}")
            st["o"], lse = attn_fwd(st["Q"], K, VX, f"attn_fwd{l}")
            st["lse"] = lse.reshape(N_HEADS, 1, S_)
            st["y"], xmid = mm(st["o"], full[("w_o", j)], "nn", BF16, f"wo_mm{l}", res=xcur, colscale=g1)
        st["xmid"] = xmid
        st["h2"] = norm_fwd(xmid, norm2_g[l], sc2, sh2, BF16, f"norm2_fwd{l}")
        st["u"] = mm(st["h2"], full[("w_up", l)], "nn", BF16, f"up_mm{l}")
        st["z"] = glu_fwd(st["u"], full["conv_w"][l], conv_b[l], f"glu_fwd{l}")
        st["f"], xcur = mm(st["z"], full[("w_down", l)], "nn", BF16, f"down_mm{l}", tk=1408, res=xmid, colscale=g2)
        saved.append(st)

    dx, d_final_g, loss_part = loss_head(xcur, final_g, loss_target[0], "loss_head")
    loss = lax.psum(loss_part[0, 0], ("x", "y", "c"))

    def begin_reduce(tensors, first_slot, tag):
        units = []
        for n in tensors:
            ca = EXCHANGED[n][0]
            if W[n].ndim > 2:
                n_slots = W[n].shape[0] // 2
                for sl in range(first_slot if n_slots > 1 else 0, first_slot + 1 if n_slots > 1 else 1):
                    units.append((n, sl, G[(n, 2 * sl)], G[(n, 2 * sl + 1)], ca - 1))
            elif n == "w_dkv":
                g4 = G[(n, 0)].reshape(N_CHIPS, 2, -1, kv_w)
                units.append((n, 0, g4[:, 0], g4[:, 1], 0))
            else:
                rows_half = W[n].shape[0] // 2
                units.append((n, 0, G[(n, 0)][:rows_half], G[(n, 0)][rows_half:], ca))
        lo = [u[2] for u in units]
        hi = [u[3] for u in units]
        theirs = reduce_to_sibling(lo, hi, f"reduce_cores_{tag}")
        sums = [add_selected(l_.reshape(-1, l_.shape[-1]), h_.reshape(-1, l_.shape[-1]), t_.reshape(-1, l_.shape[-1]),
                             f"reduce_cores_add_{tag}{i}").reshape(l_.shape)
                for i, (l_, h_, t_) in enumerate(zip(lo, hi, theirs))]
        return units, sums, [u[4] for u in units]

    G = {}
    dmods = [None] * DEPTH
    d_norm1 = [None] * DEPTH
    d_norm2 = [None] * DEPTH
    d_conv_b = [None] * DEPTH
    d_qnorm = [None] * n_mla
    dkv_acc = []
    df, a2, _ = gate_bwd(dx, saved[DEPTH - 1]["f"], mod[DEPTH - 1][5], f"gate2_bwd{DEPTH - 1}")
    for l in reversed(range(DEPTH)):
        sh1, sc1, g1, sh2, sc2, g2 = mod[l]
        st = saved[l]
        next_gate = (saved[l - 1]["f"], mod[l - 1][5]) if l else None
        dz = mm(df, full[("w_down", l)], "nt", BF16, f"down_dx{l}")
        G[("w_down", l)] = mm(st["z"], df, "tn", BF16, f"down_dw{l}")
        du, dcw, dcb = glu_bwd(st["u"], dz, full["conv_w"][l], conv_b[l], f"glu_bwd{l}")
        G[("conv_w", l)] = dcw
        d_conv_b[l] = dcb[0]
        dh2 = mm(du, full[("w_up", l)], "nt", BF16, f"up_dx{l}", tk=1408)
        G[("w_up", l)] = mm(st["h2"], du, "tn", BF16, f"up_dw{l}")
        dxmid, s1, s2, dgate, a1, csum = norm_bwd(st["xmid"], norm2_g[l], sc2, dh2, dx, f"norm2_bwd{l}",
                                                  gate=(st["ypre"], st["cs"]) if l < N_A else (st["y"], g1))
        dsh2, dsc2, d_norm2[l] = s1[0], s2[0] * norm2_g[l], s2[0] * (1.0 + sc2)
        if l < N_A:
            dyp = dgate
            dg1 = full["pool_scale"][l] * a1[0]
            G[("pool_scale", l)] = g1 * a1[0]
            G[("pool_b", l)] = st["cs"] * csum[0]
            dpooled = gmm(dyp, full[("pool_w", l)], "nt", F32, f"pool_dx{l}")
            G[("pool_w", l)] = gmm(st["pooled"], dyp, "tn", BF16, f"pool_dw{l}")
            dh1 = _pool_call(dpooled, F32, f"pool_bwd{l}", True)
        else:
            j = l - N_A
            dy = dgate
            dg1 = a1[0]
            do, delta = o_proj_bwd(dy, full[("w_o", j)], st["o"], f"wo_dx{l}")
            delta = delta.reshape(N_HEADS, 1, S_)
            G[("w_o", j)] = mm(st["o"], dy, "tn", BF16, f"wo_dw{l}")
            dQ, dK, dV = attn_bwd(st["Q"], K, VX, do, st["lse"], delta, f"attn_bwd{l}")
            dkv_acc.append((dK, dV))
            dcq, dw_ext = q_proj_bwd(dQ, st["cq"], st["w_uq_ext"], tabs, att_scale, f"uq_bwd{l}")
            G[("w_uq", j)] = dw_ext.reshape(q_rank, N_HEADS, HEAD_PAD)[:, :, :QK_HEAD].reshape(q_rank, N_HEADS * QK_HEAD)
            zq = jnp.zeros_like(q_norm_g[j])
            dql, _, s2q = norm_bwd(st["ql"], q_norm_g[j], zq, dcq, None, f"qnorm_bwd{l}")
            d_qnorm[j] = s2q[0]
            dh1 = mm(dql, full[("w_dq", j)], "nt", BF16, f"dq_dx{l}")
            G[("w_dq", j)] = mm(st["h1"], dql, "tn", BF16, f"dq_dw{l}")
        a2_mine = a2
        if l and l != N_A:
            dx, s1, s2, df, a2, _ = norm_bwd(st["xin"], norm1_g[l], sc1, dh1, dxmid, f"norm1_bwd{l}", gate=next_gate)
        else:
            dx, s1, s2 = norm_bwd(st["xin"], norm1_g[l], sc1, dh1, dxmid, f"norm1_bwd{l}")
        dsh1, dsc1, d_norm1[l] = s1[0], s2[0] * norm1_g[l], s2[0] * (1.0 + sc1)
        dmods[l] = jnp.concatenate([dsh1, dsc1, dg1, dsh2, dsc2, a2_mine[0]])
        if l == N_A:
            (dk_a, dv_a), (dk_b, dv_b) = dkv_acc
            dknv, d_tk = k_prep_bwd(dk_a, dk_b, dv_a, dv_b, tabs, "k_prep_bwd")
            dckv = mm(dknv, w_ukv, "nt", F32, "ukv_dx")
            d_ukv = mm(kv_saved["ckv"], dknv, "tn", BF16, "ukv_dw")
            G[("w_uk", 0)], G[("w_uv", 0)] = d_ukv[:, :N_HEADS * QK_NOPE], d_ukv[:, N_HEADS * QK_NOPE:]
            zk = jnp.zeros((KV_RANK,), F32)
            dlat, _, s2c = norm_bwd(kv_saved["lat"], ckv_norm_g, zk, dckv, None, "ckv_bwd")
            d_ckv_g = s2c[0]
            dkv_ext = jnp.concatenate([dlat, d_tk], axis=1)
            dxn = mm(dkv_ext, w_dkv_ext, "nt", BF16, "dkv_dx")
            G[("w_dkv", 0)] = mm(kv_saved["xn"], dkv_ext, "tn", BF16, "dkv_dw")[:, :kv_w]
            dx, _, s2k, df, a2, _ = norm_bwd(kv_saved["x"], kv_in_g, zD, dxn, dx, "kvin_bwd", gate=next_gate)
            d_kvin_g = s2k[0]
            e_units, e_sums, e_axes = begin_reduce([n for n in EXCHANGED if n != "pool_w"], 1, "early")
            e_send, e_recv, e_pieces, e_lands, e_token = scatter_start(e_sums, e_axes, "reduce_chips_start")
            early = (e_units, e_send, e_recv, e_lands, e_pieces, e_axes)
            mod[l - 1][4] = mod[l - 1][4] + e_token[0, 0]

    small = {"mod_b": jnp.stack(dmods), "norm1_g": jnp.stack(d_norm1), "norm2_g": jnp.stack(d_norm2),
             "kv_in_g": d_kvin_g, "ckv_norm_g": d_ckv_g, "q_norm_g": jnp.stack(d_qnorm),
             "conv_b": jnp.stack(d_conv_b), "final_g": d_final_g[0]}
    extra = {n: jnp.stack([G[(n, i)] for i in range(W[n].shape[0])]) for n in SMALL_SHARDED}
    ssizes = {n: math.prod(W[n].shape) for n in REPLICATED}
    esizes = {n: math.prod(extra[n].shape) for n in SMALL_SHARDED}
    Ts = _padded(sum(ssizes.values()) + sum(esizes.values()), 8 * PACK_COLS)

    def pack_small(d, tail=()):
        return _flat_pad([d[n].reshape(-1) for n in REPLICATED] + [t.reshape(-1) for t in tail],
                         Ts).reshape(Ts // PACK_COLS, PACK_COLS)

    parts = device_gather(pack_small(small, [extra[n] for n in SMALL_SHARDED]), "gather_small")

    l_units, l_sums, l_axes = begin_reduce([n for n in EXCHANGED if W[n].ndim > 2 and W[n].shape[0] == DEPTH] + ["pool_w"],
                                           0, "late")
    l_send, l_recv, l_pieces, l_lands, l_token = scatter_start(l_sums, l_axes, "reduce_chips_late_start", after=[parts])
    parts = parts + l_token[0, 0]

    grads, deltas, new_m, new_v = {}, {}, {}, {}
    outs = adamw_sum(parts, pack_small(W), pack_small(M1), pack_small(V2), "adamw_small")
    off = 0
    for n in REPLICATED:
        for dst, o in zip((grads, deltas, new_m, new_v), outs):
            dst[n] = o.reshape(-1)[off:off + ssizes[n]].reshape(W[n].shape)
        off += ssizes[n]
    for n, axis in SMALL_SHARDED.items():
        g_full = outs[0].reshape(-1)[off:off + esizes[n]].reshape(extra[n].shape)
        off += esizes[n]
        size = W[n].shape[axis]
        grads[n] = lax.dynamic_slice_in_dim(g_full, chip * size, size, axis=axis)
        deltas[n], new_m[n], new_v[n] = adamw(W[n], grads[n], M1[n], V2[n], f"adamw_{n}")

    dm_all = parts.reshape(N_DEV, -1)[:, :DEPTH * E].reshape(N_DEV, DEPTH, E)
    dm_mine = jnp.swapaxes(lax.dynamic_slice_in_dim(dm_all, chip * Es, Es, axis=2), 0, 1)
    grads["mod_w"], deltas["mod_w"], new_m["mod_w"], new_v["mod_w"] = adamw_modw(
        c_all.reshape(N_DEV, D, 1), dm_mine, mod_w, m_mod_w, v_mod_w, "adamw_mod_w")

    def finish_reduce(pieces, axes, got, tag):
        out = []
        for i, (sm, ax, g4) in enumerate(zip(pieces, axes, got)):
            size = sm.shape[ax] // N_CHIPS
            g4 = lax.dynamic_update_index_in_dim(g4, lax.dynamic_slice_in_dim(sm, chip * size, size, axis=ax), chip, 0)
            blk = g4.shape[1:]
            out.append(sum_parts(g4.reshape(N_CHIPS, -1, blk[-1]), f"reduce_chips_add_{tag}{i}").reshape(blk))
        return out

    e_units, e_send, e_recv, e_lands, e_pieces, e_axes = early
    early_got = scatter_wait(e_send, e_recv, e_lands, e_pieces, dx, "reduce_chips_wait")
    reduced = finish_reduce(e_pieces, e_axes, early_got, "early")
    late_got = scatter_wait(l_send, l_recv, l_lands, l_pieces, new_v["mod_w"], "reduce_chips_late_wait")
    reduced += finish_reduce(l_pieces, l_axes, late_got, "late")
    units = e_units + l_units
    slots, out_shapes = [], []
    for n in EXCHANGED:
        mine = [i for i, u in enumerate(units) if u[0] == n]
        out_shapes.append((len(mine), 2) + reduced[mine[0]].shape)
        slots += [(len(out_shapes) - 1, units[i][1]) for i in mine]
    order = [i for n in EXCHANGED for i, u in enumerate(units) if u[0] == n]
    halves = gather_halves([reduced[i] for i in order], slots, out_shapes, "reduce_gather")
    for ti, n in enumerate(EXCHANGED):
        g = halves[ti]
        for i, u in enumerate(units):
            if u[0] == n:
                g = lax.dynamic_update_slice(g, reduced[i][None, None], (u[1], ci) + (0,) * reduced[i].ndim)
        grads[n] = g.reshape(W[n].shape)
        deltas[n], new_m[n], new_v[n] = adamw(W[n], grads[n], M1[n], V2[n], f"adamw_{n}")

    return (loss, dx.reshape(x.shape), *[grads[n] for n in WEIGHT_ORDER], *[deltas[n] for n in WEIGHT_ORDER],
            *[new_m[n] for n in WEIGHT_ORDER], *[new_v[n] for n in WEIGHT_ORDER])
```

```python
import functools
import math

import jax
import jax.numpy as jnp
from jax import lax
from jax.experimental import pallas as pl
from jax.experimental.pallas import tpu as pltpu

F32 = jnp.float32
BF16 = jnp.bfloat16
MESH = pl.DeviceIdType.MESH

DEPTH = 4
N_A = 2
POOL_WINDOWS = (2, 4, 8, 16)
N_GROUPS = 4
N_HEADS = 8
QK_NOPE = 128
QK_ROPE = 64
V_HEAD = 128
QK_HEAD = QK_NOPE + QK_ROPE
HEAD_PAD = 256
KV_RANK = 256
ROPE_THETA = 10000.0
EPS = 1e-6
ADAM_LR = 0.001
ADAM_B1 = 0.9
ADAM_B2 = 0.999
ADAM_EPS = 1e-08
ADAM_WD = 0.01
ADAM_STEP = 10

N_CHIPS = 4
N_DEV = 8
LANES = 128
PACK_COLS = 1024
VMEM_LIMIT = 56 * 1024 * 1024
GLU_TILE = 256
ATT_BWD_K_BLOCK = 512
ATT_BWD_Q_BLOCK = 512
ATT_Q_BLOCK = 1024
ATT_K_BLOCK = 512
ATT_HEADS_PER_STEP = 2


def _cparams(*sem):
    return pltpu.CompilerParams(dimension_semantics=sem if sem else None, vmem_limit_bytes=VMEM_LIMIT)


def _pick(n, target, mult):
    best = None
    d = mult
    while d <= min(n, target):
        if n % d == 0:
            best = d
        d += mult
    return n if best is None else best


def _row(v):
    return v.reshape(1, -1).astype(F32)


_DIMS = {"nn": (((1,), (0,)), ((), ())), "nt": (((1,), (1,)), ((), ())), "tn": (((0,), (0,)), ((), ()))}


def _mm_body(mode, nk, has_bias, has_res):
    def body(*refs):
        a_ref, b_ref = refs[0], refs[1]
        pos = 2
        bias_ref = res_ref = cs_ref = None
        if has_bias:
            bias_ref = refs[pos]
            pos += 1
        if has_res:
            res_ref, cs_ref = refs[pos], refs[pos + 1]
            pos += 2
        o_ref = refs[pos]
        pos += 1
        o2_ref = None
        if has_res:
            o2_ref = refs[pos]
            pos += 1
        acc_ref = refs[pos] if nk > 1 else None
        k = pl.program_id(2)
        part = lax.dot_general(a_ref[...].astype(BF16), b_ref[...].astype(BF16), _DIMS[mode],
                               preferred_element_type=F32)

        def finish(y):
            if has_bias:
                y = y + bias_ref[...]
            o_ref[...] = y.astype(o_ref.dtype)
            if has_res:
                o2_ref[...] = res_ref[...] + cs_ref[...] * y

        if nk == 1:
            finish(part)
            return

        @pl.when(k == 0)
        def _():
            acc_ref[...] = part

        @pl.when((k > 0) & (k < nk - 1))
        def _():
            acc_ref[...] += part

        @pl.when(k == nk - 1)
        def _():
            finish(acc_ref[...] + part)

    return body


def mm(a, b, mode, out_dtype, name, *, tm=1408, tn=1408, tk=1024, bias=None, res=None, colscale=None, layer=None):
    bshape = b.shape if layer is None else b.shape[1:]
    if mode == "nn":
        (M, K), N = a.shape, bshape[1]
    elif mode == "nt":
        (M, K), N = a.shape, bshape[0]
    else:
        (K, M), N = a.shape, bshape[1]
    tm = _pick(M, tm, LANES if mode == "tn" else 8)
    tn = _pick(N, tn, LANES)
    tk = _pick(K, tk, LANES) if mode != "tn" else _pick(K, tk, 8)
    nk = K // tk
    a_spec = {"nn": pl.BlockSpec((tm, tk), lambda i, j, k: (i, k)),
              "nt": pl.BlockSpec((tm, tk), lambda i, j, k: (i, k)),
              "tn": pl.BlockSpec((tk, tm), lambda i, j, k: (k, i))}[mode]
    b_blk, b_map = {"nn": ((tk, tn), lambda i, j, k: (k, j)),
                    "nt": ((tn, tk), lambda i, j, k: (j, k)),
                    "tn": ((tk, tn), lambda i, j, k: (k, j))}[mode]
    if layer is None:
        b_spec = pl.BlockSpec(b_blk, b_map)
    else:
        b_spec = pl.BlockSpec((None,) + b_blk, lambda i, j, k: (layer,) + b_map(i, j, k))
    o_spec = pl.BlockSpec((tm, tn), lambda i, j, k: (i, j))
    v_spec = pl.BlockSpec((1, tn), lambda i, j, k: (0, j))
    in_specs, args = [a_spec, b_spec], [a, b]
    if bias is not None:
        in_specs.append(v_spec)
        args.append(_row(bias))
    out_shape = [jax.ShapeDtypeStruct((M, N), out_dtype)]
    out_specs = [o_spec]
    if res is not None:
        in_specs += [o_spec, v_spec]
        args += [res, _row(colscale)]
        out_shape.append(jax.ShapeDtypeStruct((M, N), F32))
        out_specs.append(o_spec)
    outs = pl.pallas_call(
        _mm_body(mode, nk, bias is not None, res is not None),
        grid=(M // tm, N // tn, nk),
        in_specs=in_specs, out_specs=out_specs, out_shape=out_shape,
        scratch_shapes=[pltpu.VMEM((tm, tn), F32)] if nk > 1 else [],
        compiler_params=_cparams("parallel", "parallel", "arbitrary"),
        name=name,
    )(*args)
    return outs if res is not None else outs[0]


def gmm(a, w, mode, out_dtype, name, *, bias=None, res=None, colscale=None, tr=2048):
    S_ = a.shape[0]
    G = N_GROUPS
    C = a.shape[1] // G
    tr = _pick(S_, tr, 8)
    nr = S_ // tr
    if mode == "tn":
        def body(a_ref, b_ref, o_ref, acc_ref):
            i = pl.program_id(1)

            @pl.when(i == 0)
            def _():
                acc_ref[...] = jnp.zeros_like(acc_ref)

            acc_ref[...] += lax.dot_general(a_ref[...].astype(BF16), b_ref[...].astype(BF16), _DIMS["tn"],
                                            preferred_element_type=F32)

            @pl.when(i == nr - 1)
            def _():
                o_ref[...] = acc_ref[...].astype(o_ref.dtype)

        blk = pl.BlockSpec((tr, C), lambda g, i: (i, g))
        return pl.pallas_call(
            body, grid=(G, nr), in_specs=[blk, blk],
            out_specs=pl.BlockSpec((None, C, C), lambda g, i: (g, 0, 0)),
            out_shape=jax.ShapeDtypeStruct((G, C, C), out_dtype),
            scratch_shapes=[pltpu.VMEM((C, C), F32)],
            compiler_params=_cparams("parallel", "arbitrary"), name=name,
        )(a, w)

    has_bias, has_res = bias is not None, res is not None

    def body(*refs):
        a_ref, w_ref = refs[0], refs[1]
        pos = 2
        if has_bias:
            bias_ref = refs[pos]
            pos += 1
        if has_res:
            res_ref, cs_ref = refs[pos], refs[pos + 1]
            pos += 2
        o_ref = refs[pos]
        y = lax.dot_general(a_ref[...].astype(BF16), w_ref[...].astype(BF16), _DIMS[mode],
                            preferred_element_type=F32)
        if has_bias:
            y = y + bias_ref[...]
        o_ref[...] = y.astype(o_ref.dtype)
        if has_res:
            refs[pos + 1][...] = res_ref[...] + cs_ref[...] * y

    blk = pl.BlockSpec((tr, C), lambda i, g: (i, g))
    vec = pl.BlockSpec((1, C), lambda i, g: (0, g))
    in_specs = [blk, pl.BlockSpec((None, C, C), lambda i, g: (g, 0, 0))]
    args = [a, w]
    if has_bias:
        in_specs.append(vec)
        args.append(_row(bias))
    out_shape = [jax.ShapeDtypeStruct(a.shape, out_dtype)]
    out_specs = [blk]
    if has_res:
        in_specs += [blk, vec]
        args += [res, _row(colscale)]
        out_shape.append(jax.ShapeDtypeStruct(a.shape, F32))
        out_specs.append(blk)
    outs = pl.pallas_call(
        body, grid=(nr, G), in_specs=in_specs, out_specs=out_specs, out_shape=out_shape,
        compiler_params=_cparams("parallel", "parallel"), name=name,
    )(*args)
    return outs if has_res else outs[0]


def norm_fwd(x, g, sc, sh, out_dtype, name, tr=512):
    S_, Dn = x.shape
    tr = _pick(S_, tr, 8)

    def body(x_ref, g_ref, sc_ref, sh_ref, o_ref):
        xv = x_ref[...]
        r = lax.rsqrt(jnp.mean(xv * xv, axis=-1, keepdims=True) + EPS)
        o_ref[...] = (((xv * r) * g_ref[...]) * (1.0 + sc_ref[...]) + sh_ref[...]).astype(o_ref.dtype)

    blk = pl.BlockSpec((tr, Dn), lambda i: (i, 0))
    vec = pl.BlockSpec((1, Dn), lambda i: (0, 0))
    return pl.pallas_call(
        body, grid=(S_ // tr,), in_specs=[blk, vec, vec, vec], out_specs=blk,
        out_shape=jax.ShapeDtypeStruct((S_, Dn), out_dtype),
        compiler_params=_cparams("parallel"), name=name,
    )(x, _row(g), _row(sc), _row(sh))


def norm_bwd(x, g, sc, dh, dres, name, gate=None, tr=512):
    S_, Dn = x.shape
    tr = _pick(S_, tr, 8)
    has_res = dres is not None
    has_gate = gate is not None

    def body(*refs):
        x_ref, g_ref, sc_ref, dh_ref = refs[:4]
        pos = 4
        if has_res:
            dres_ref = refs[pos]
            pos += 1
        if has_gate:
            y_ref, cs_ref = refs[pos:pos + 2]
            pos += 2
        dx_ref, s1_ref, s2_ref = refs[pos:pos + 3]
        if has_gate:
            d_ref, a_ref, c_ref = refs[pos + 3:pos + 6]
        i = pl.program_id(0)

        @pl.when(i == 0)
        def _():
            s1_ref[...] = jnp.zeros_like(s1_ref)
            s2_ref[...] = jnp.zeros_like(s2_ref)
            if has_gate:
                a_ref[...] = jnp.zeros_like(a_ref)
                c_ref[...] = jnp.zeros_like(c_ref)

        xv = x_ref[...]
        r = lax.rsqrt(jnp.mean(xv * xv, axis=-1, keepdims=True) + EPS)
        n = xv * r
        dhv = dh_ref[...].astype(F32)
        dn = dhv * (g_ref[...] * (1.0 + sc_ref[...]))
        dx = r * (dn - n * jnp.mean(dn * n, axis=-1, keepdims=True))
        if has_res:
            dx = dx + dres_ref[...]
        dx_ref[...] = dx
        s1_ref[...] += jnp.sum(dhv, axis=0, keepdims=True)
        s2_ref[...] += jnp.sum(dhv * n, axis=0, keepdims=True)
        if has_gate:
            d_ref[...] = (dx * cs_ref[...]).astype(d_ref.dtype)
            a_ref[...] += jnp.sum(dx * y_ref[...].astype(F32), axis=0, keepdims=True)
            c_ref[...] += jnp.sum(dx, axis=0, keepdims=True)

    blk = pl.BlockSpec((tr, Dn), lambda i: (i, 0))
    vec = pl.BlockSpec((1, Dn), lambda i: (0, 0))
    in_specs, args = [blk, vec, vec, blk], [x, _row(g), _row(sc), dh]
    if has_res:
        in_specs.append(blk)
        args.append(dres)
    vshape = jax.ShapeDtypeStruct((1, Dn), F32)
    out_specs = [blk, vec, vec]
    out_shape = [jax.ShapeDtypeStruct((S_, Dn), F32), vshape, vshape]
    if has_gate:
        in_specs += [blk, vec]
        args += [gate[0], _row(gate[1])]
        out_specs += [blk, vec, vec]
        out_shape += [jax.ShapeDtypeStruct((S_, Dn), BF16), vshape, vshape]
    return pl.pallas_call(
        body, grid=(S_ // tr,), in_specs=in_specs, out_specs=out_specs, out_shape=out_shape,
        compiler_params=_cparams("arbitrary"), name=name,
    )(*args)


def gate_bwd(dx, y, colscale, name, tr=512):
    S_, Dn = dx.shape
    tr = _pick(S_, tr, 8)

    def body(dx_ref, y_ref, cs_ref, d_ref, a_ref, c_ref):
        i = pl.program_id(0)

        @pl.when(i == 0)
        def _():
            a_ref[...] = jnp.zeros_like(a_ref)
            c_ref[...] = jnp.zeros_like(c_ref)

        dxv = dx_ref[...]
        d_ref[...] = (dxv * cs_ref[...]).astype(d_ref.dtype)
        a_ref[...] += jnp.sum(dxv * y_ref[...].astype(F32), axis=0, keepdims=True)
        c_ref[...] += jnp.sum(dxv, axis=0, keepdims=True)

    blk = pl.BlockSpec((tr, Dn), lambda i: (i, 0))
    vec = pl.BlockSpec((1, Dn), lambda i: (0, 0))
    vshape = jax.ShapeDtypeStruct((1, Dn), F32)
    return pl.pallas_call(
        body, grid=(S_ // tr,), in_specs=[blk, blk, vec], out_specs=[blk, vec, vec],
        out_shape=[jax.ShapeDtypeStruct((S_, Dn), BF16), vshape, vshape],
        compiler_params=_cparams("arbitrary"), name=name,
    )(dx, y, _row(colscale))


def loss_head(x, g, target, name, tr=512):
    S_, Dn = x.shape
    tr = _pick(S_, tr, 8)

    def body(x_ref, g_ref, t_ref, dx_ref, dg_ref, loss_ref):
        i = pl.program_id(0)

        @pl.when(i == 0)
        def _():
            dg_ref[...] = jnp.zeros_like(dg_ref)
            loss_ref[...] = jnp.zeros_like(loss_ref)

        xv = x_ref[...]
        r = lax.rsqrt(jnp.mean(xv * xv, axis=-1, keepdims=True) + EPS)
        n = xv * r
        e = n * g_ref[...] - t_ref[...]
        loss_ref[...] += 0.5 * jnp.sum(jnp.mean(e * e, axis=-1, keepdims=True), axis=0, keepdims=True)
        dy = e * (1.0 / Dn)
        dg_ref[...] += jnp.sum(dy * n, axis=0, keepdims=True)
        dn = dy * g_ref[...]
        dx_ref[...] = r * (dn - n * jnp.mean(dn * n, axis=-1, keepdims=True))

    blk = pl.BlockSpec((tr, Dn), lambda i: (i, 0))
    vec = pl.BlockSpec((1, Dn), lambda i: (0, 0))
    one = pl.BlockSpec((1, 1), lambda i: (0, 0))
    return pl.pallas_call(
        body, grid=(S_ // tr,), in_specs=[blk, vec, blk], out_specs=[blk, vec, one],
        out_shape=[jax.ShapeDtypeStruct((S_, Dn), F32), jax.ShapeDtypeStruct((1, Dn), F32),
                   jax.ShapeDtypeStruct((1, 1), F32)],
        compiler_params=_cparams("arbitrary"), name=name,
    )(x, _row(g), target)


POOL_HALO = 16
POOL_CHUNK = 512


def _rows(ref, lo, hi, n_rows):
    parts = []
    if lo < 0:
        parts.append(jnp.zeros((-lo, ref.shape[1]), F32))
    parts.append(ref[max(lo, 0):min(hi, n_rows), :].astype(F32))
    if hi > n_rows:
        parts.append(jnp.zeros((hi - n_rows, ref.shape[1]), F32))
    return parts[0] if len(parts) == 1 else jnp.concatenate(parts, axis=0)


def _window_sum(e, w, back):
    n = e.shape[0]
    s, width = e, 1
    while width < w:
        s = s + pltpu.roll(s, width if back else n - width, 0)
        width *= 2
    return s


def _pool_call(h, out_dtype, name, backward):
    S_, Dn = h.shape
    C = Dn // N_GROUPS
    ch = _pick(S_, POOL_CHUNK, 8)

    def body(h_ref, o_ref):
        g = pl.program_id(0)
        for gi, w in enumerate(POOL_WINDOWS):
            @pl.when(g == gi)
            def _(w=w):
                for r0 in range(0, S_, ch):
                    t = (r0 + lax.broadcasted_iota(jnp.int32, (ch, C), 0)).astype(F32)
                    cnt = jnp.minimum(t + 1.0, float(w))
                    if not backward:
                        ext = _rows(h_ref, r0 - POOL_HALO, r0 + ch, S_)
                        cur = ext[POOL_HALO:]
                        mean = _window_sum(ext, w, True)[POOL_HALO:] / cnt
                        o_ref[r0:r0 + ch, :] = (mean - cur).astype(o_ref.dtype)
                    else:
                        ext = _rows(h_ref, r0, r0 + ch + POOL_HALO, S_)
                        text = (r0 + lax.broadcasted_iota(jnp.int32, (ch + POOL_HALO, C), 0)).astype(F32)
                        e = ext / jnp.minimum(text + 1.0, float(w))
                        o_ref[r0:r0 + ch, :] = (_window_sum(e, w, False)[:ch] - ext[:ch]).astype(o_ref.dtype)

    blk = pl.BlockSpec((S_, C), lambda g: (0, g))
    return pl.pallas_call(
        body, grid=(N_GROUPS,), in_specs=[blk], out_specs=blk,
        out_shape=jax.ShapeDtypeStruct((S_, Dn), out_dtype),
        compiler_params=_cparams("parallel"), name=name,
    )(h)


GLU_CHUNK = 512
GLU_HALO = 16
_SQRT_HALF = 0.7071067811865476
_INV_SQRT_2PI = 0.3989422804014327


def _gelu(a):
    return 0.5 * a * (1.0 + lax.erf(a * _SQRT_HALF))


def _gelu_grad(a):
    return 0.5 * (1.0 + lax.erf(a * _SQRT_HALF)) + a * (_INV_SQRT_2PI * jnp.exp(-0.5 * a * a))


def glu_fwd(u, conv_w, conv_b, name):
    S_, F2 = u.shape
    Fh = F2 // 2
    tf = GLU_TILE
    nt = Fh // tf
    ch = _pick(S_, GLU_CHUNK, GLU_HALO)

    def body(a_ref, v_ref, cw_ref, cb_ref, z_ref):
        cw0, cw1, cw2 = cw_ref[0:1, :], cw_ref[1:2, :], cw_ref[2:3, :]
        cb = cb_ref[...]
        for r0 in range(0, S_, ch):
            ext = _rows(a_ref, r0 - GLU_HALO, r0 + ch, S_)
            a0 = ext[GLU_HALO:]
            a1 = pltpu.roll(ext, 1, 0)[GLU_HALO:]
            a2 = pltpu.roll(ext, 2, 0)[GLU_HALO:]
            ac = a2 * cw0 + a1 * cw1 + a0 * cw2 + cb
            z_ref[r0:r0 + ch, :] = (_gelu(ac) * v_ref[r0:r0 + ch, :].astype(F32)).astype(z_ref.dtype)

    return pl.pallas_call(
        body, grid=(nt,),
        in_specs=[pl.BlockSpec((S_, tf), lambda j: (0, j)), pl.BlockSpec((S_, tf), lambda j: (0, j + nt)),
                  pl.BlockSpec((3, tf), lambda j: (0, j)), pl.BlockSpec((1, tf), lambda j: (0, j))],
        out_specs=pl.BlockSpec((S_, tf), lambda j: (0, j)),
        out_shape=jax.ShapeDtypeStruct((S_, Fh), BF16),
        compiler_params=_cparams("parallel"), name=name,
    )(u, u, conv_w, _row(conv_b))


def glu_bwd(u, dz, conv_w, conv_b, name):
    S_, F2 = u.shape
    Fh = F2 // 2
    tf = GLU_TILE
    nt = Fh // tf
    ch = _pick(S_, GLU_CHUNK, GLU_HALO)

    def body(a_ref, v_ref, dz_ref, cw_ref, cb_ref, du_ref, dcw_ref, dcb_ref, da_buf, dv_buf, sems):
        j = pl.program_id(0)
        slot = j % 2

        def writes(step, sl):
            lo = pl.multiple_of(step * tf, tf)
            return (pltpu.make_async_copy(da_buf.at[sl], du_ref.at[:, pl.ds(lo, tf)], sems.at[sl, 0]),
                    pltpu.make_async_copy(dv_buf.at[sl], du_ref.at[:, pl.ds(Fh + lo, tf)], sems.at[sl, 1]))

        @pl.when(j >= 2)
        def _():
            for cp in writes(j - 2, slot):
                cp.wait()

        cw0, cw1, cw2 = cw_ref[0:1, :], cw_ref[1:2, :], cw_ref[2:3, :]
        cb = cb_ref[...]
        acc = [jnp.zeros((1, tf), F32) for _ in range(4)]
        n = ch + GLU_HALO
        for r0 in range(0, S_, ch):
            ext = _rows(a_ref, r0 - GLU_HALO, r0 + n, S_)
            a0 = ext[GLU_HALO:]
            a1 = pltpu.roll(ext, 1, 0)[GLU_HALO:]
            a2 = pltpu.roll(ext, 2, 0)[GLU_HALO:]
            ac = a2 * cw0 + a1 * cw1 + a0 * cw2 + cb
            vv = _rows(v_ref, r0, r0 + n, S_)
            dzv = _rows(dz_ref, r0, r0 + n, S_)
            gl = _gelu(ac)
            dac = dzv * vv * _gelu_grad(ac)
            da = (dac * cw2 + pltpu.roll(dac, n - 1, 0) * cw1 + pltpu.roll(dac, n - 2, 0) * cw0)[:ch]
            da_buf[slot, r0:r0 + ch, :] = da.astype(da_buf.dtype)
            dv_buf[slot, r0:r0 + ch, :] = (dzv[:ch] * gl[:ch]).astype(dv_buf.dtype)
            dc = dac[:ch]
            acc[0] = acc[0] + jnp.sum(dc * a2[:ch], axis=0, keepdims=True)
            acc[1] = acc[1] + jnp.sum(dc * a1[:ch], axis=0, keepdims=True)
            acc[2] = acc[2] + jnp.sum(dc * a0[:ch], axis=0, keepdims=True)
            acc[3] = acc[3] + jnp.sum(dc, axis=0, keepdims=True)
        dcw_ref[0:1, :] = acc[0]
        dcw_ref[1:2, :] = acc[1]
        dcw_ref[2:3, :] = acc[2]
        dcb_ref[...] = acc[3]
        for cp in writes(j, slot):
            cp.start()

        @pl.when(j == nt - 1)
        def _():
            for cp in writes(j, slot):
                cp.wait()
            if nt > 1:
                for cp in writes(j - 1, 1 - slot):
                    cp.wait()

    return pl.pallas_call(
        body, grid=(nt,),
        in_specs=[pl.BlockSpec((S_, tf), lambda j: (0, j)), pl.BlockSpec((S_, tf), lambda j: (0, j + nt)),
                  pl.BlockSpec((S_, tf), lambda j: (0, j)),
                  pl.BlockSpec((3, tf), lambda j: (0, j)), pl.BlockSpec((1, tf), lambda j: (0, j))],
        out_specs=[_ANY, pl.BlockSpec((3, tf), lambda j: (0, j)), pl.BlockSpec((1, tf), lambda j: (0, j))],
        out_shape=[jax.ShapeDtypeStruct((S_, F2), BF16), jax.ShapeDtypeStruct((3, Fh), F32),
                   jax.ShapeDtypeStruct((1, Fh), F32)],
        scratch_shapes=[pltpu.VMEM((2, S_, tf), BF16), pltpu.VMEM((2, S_, tf), BF16), pltpu.SemaphoreType.DMA((2, 2))],
        compiler_params=_cparams("arbitrary"), name=name,
    )(u, u, dz, conv_w, _row(conv_b))


def rope_tables(pos, inv, name, tr=512):
    S_ = pos.shape[0]
    tr = _pick(S_, tr, 8)

    def body(p_ref, inv_ref, c_ref, s1_ref, s2_ref):
        ang = p_ref[...] * inv_ref[...]
        lane = lax.broadcasted_iota(jnp.int32, ang.shape, 1)
        half = QK_ROPE // 2
        cosv, sinv = jnp.cos(ang), jnp.sin(ang)
        c_ref[...] = jnp.where(lane < QK_ROPE, cosv, 0.0)
        s1_ref[...] = jnp.where(lane < half, -sinv, 0.0)
        s2_ref[...] = jnp.where((lane >= half) & (lane < QK_ROPE), sinv, 0.0)

    blk = pl.BlockSpec((tr, LANES), lambda i: (i, 0))
    shp = jax.ShapeDtypeStruct((S_, LANES), F32)
    return pl.pallas_call(
        body, grid=(S_ // tr,),
        in_specs=[pl.BlockSpec((tr, 1), lambda i: (i, 0)), pl.BlockSpec((1, LANES), lambda i: (0, 0))],
        out_specs=[blk, blk, blk], out_shape=[shp, shp, shp],
        compiler_params=_cparams("parallel"), name=name,
    )(pos, inv)


_HALF = QK_ROPE // 2


def _rope(t, c, s1, s2):
    return t * c + pltpu.roll(t, LANES - _HALF, 1) * s1 + pltpu.roll(t, _HALF, 1) * s2


def _rope_t(d, c, s1, s2):
    return d * c + pltpu.roll(d * s1, _HALF, 1) + pltpu.roll(d * s2, LANES - _HALF, 1)


def q_prep(q, tabs, scale, backward, name, tr=512):
    S_, W = q.shape
    tr = _pick(S_, tr, 8)

    def body(q_ref, c_ref, s1_ref, s2_ref, o_ref):
        o_ref[:, 0:LANES] = (q_ref[:, 0:LANES].astype(F32) * scale).astype(o_ref.dtype)
        t = q_ref[:, LANES:2 * LANES].astype(F32)
        fn = _rope_t if backward else _rope
        o_ref[:, LANES:2 * LANES] = (fn(t, c_ref[...], s1_ref[...], s2_ref[...]) * scale).astype(o_ref.dtype)

    blk = pl.BlockSpec((tr, HEAD_PAD), lambda i, h: (i, h))
    tab = pl.BlockSpec((tr, LANES), lambda i, h: (i, 0))
    return pl.pallas_call(
        body, grid=(S_ // tr, W // HEAD_PAD), in_specs=[blk, tab, tab, tab], out_specs=blk,
        out_shape=jax.ShapeDtypeStruct((S_, W), BF16),
        compiler_params=_cparams("parallel", "parallel"), name=name,
    )(q, *tabs)


def k_prep(knv, kv_ext, tabs, name, tr=512):
    S_ = knv.shape[0]
    tr = _pick(S_, tr, 8)

    def body(kn_ref, v_ref, t_ref, c_ref, s1_ref, s2_ref, o_ref, vx_ref):
        o_ref[:, 0:LANES] = kn_ref[...].astype(o_ref.dtype)
        o_ref[:, LANES:2 * LANES] = _rope(t_ref[...], c_ref[...], s1_ref[...], s2_ref[...]).astype(o_ref.dtype)
        vx_ref[:, 0:V_HEAD] = v_ref[...].astype(vx_ref.dtype)
        vx_ref[:, V_HEAD:HEAD_PAD] = jnp.ones((tr, HEAD_PAD - V_HEAD), vx_ref.dtype)

    tab = pl.BlockSpec((tr, LANES), lambda i, h: (i, 0))
    head = pl.BlockSpec((tr, HEAD_PAD), lambda i, h: (i, h))
    shp = jax.ShapeDtypeStruct((S_, N_HEADS * HEAD_PAD), BF16)
    return pl.pallas_call(
        body, grid=(S_ // tr, N_HEADS),
        in_specs=[pl.BlockSpec((tr, LANES), lambda i, h: (i, h)),
                  pl.BlockSpec((tr, V_HEAD), lambda i, h: (i, N_HEADS + h)),
                  pl.BlockSpec((tr, LANES), lambda i, h: (i, KV_RANK // LANES)), tab, tab, tab],
        out_specs=[head, head], out_shape=[shp, shp],
        compiler_params=_cparams("parallel", "parallel"), name=name,
    )(knv, knv, kv_ext, *tabs)


def k_prep_bwd(dk_a, dk_b, dv_a, dv_b, tabs, name, tr=512):
    S_ = dk_a.shape[0]
    tr = _pick(S_, tr, 8)
    HV = N_HEADS * V_HEAD

    def body(ka_ref, kb_ref, va_ref, vb_ref, c_ref, s1_ref, s2_ref, o_ref, t_ref):
        dr = jnp.zeros((tr, LANES), F32)
        for h in range(N_HEADS):
            lo = h * HEAD_PAD
            o_ref[:, h * LANES:(h + 1) * LANES] = (ka_ref[:, lo:lo + LANES] + kb_ref[:, lo:lo + LANES]).astype(o_ref.dtype)
            dr = dr + ka_ref[:, lo + LANES:lo + 2 * LANES] + kb_ref[:, lo + LANES:lo + 2 * LANES]
        o_ref[:, HV:2 * HV] = (va_ref[...] + vb_ref[...]).astype(o_ref.dtype)
        t_ref[...] = _rope_t(dr, c_ref[...], s1_ref[...], s2_ref[...])

    kblk = pl.BlockSpec((tr, N_HEADS * HEAD_PAD), lambda i: (i, 0))
    vblk = pl.BlockSpec((tr, HV), lambda i: (i, 0))
    tab = pl.BlockSpec((tr, LANES), lambda i: (i, 0))
    return pl.pallas_call(
        body, grid=(S_ // tr,), in_specs=[kblk, kblk, vblk, vblk, tab, tab, tab],
        out_specs=[pl.BlockSpec((tr, 2 * HV), lambda i: (i, 0)), tab],
        out_shape=[jax.ShapeDtypeStruct((S_, 2 * HV), BF16), jax.ShapeDtypeStruct((S_, LANES), F32)],
        compiler_params=_cparams("parallel"), name=name,
    )(dk_a, dk_b, dv_a, dv_b, *tabs)


_NEG = -1e30


def attn_fwd(q, k, vx, name):
    S_ = q.shape[0]
    TQ = _pick(S_, ATT_Q_BLOCK, 8)
    TK = _pick(S_, ATT_K_BLOCK, 8)
    assert TQ % TK == 0 or TK % TQ == 0
    HP = ATT_HEADS_PER_STEP
    W = HP * HEAD_PAD

    def body(q_ref, k_ref, v_ref, o_ref, lse_ref):
        i = pl.program_id(1)
        qs = [q_ref[:, h * HEAD_PAD:(h + 1) * HEAD_PAD] for h in range(HP)]

        def step(j, carry, masked):
            start = pl.multiple_of(j * TK, TK)
            out = []
            for h in range(HP):
                m, acc = carry[h]
                cols = slice(h * HEAD_PAD, (h + 1) * HEAD_PAD)
                s = lax.dot_general(qs[h], k_ref[pl.ds(start, TK), cols], _DIMS["nt"], preferred_element_type=F32)
                if masked:
                    rowi = i * TQ + lax.broadcasted_iota(jnp.int32, (TQ, TK), 0)
                    coli = j * TK + lax.broadcasted_iota(jnp.int32, (TQ, TK), 1)
                    s = jnp.where(coli <= rowi, s, _NEG)
                m_new = jnp.maximum(m, jnp.max(s, axis=-1, keepdims=True))
                alpha = jnp.exp(m - m_new)
                p = jnp.exp(s - m_new).astype(BF16)
                acc = alpha * acc + lax.dot_general(p, v_ref[pl.ds(start, TK), cols], _DIMS["nn"],
                                                    preferred_element_type=F32)
                out.append((m_new, acc))
            return tuple(out)

        init = tuple((jnp.full((TQ, 1), _NEG, F32), jnp.zeros((TQ, HEAD_PAD), F32)) for _ in range(HP))
        n_full, n_diag = (i * (TQ // TK), TQ // TK) if TQ >= TK else (i // (TK // TQ), 1)
        carry = lax.fori_loop(0, n_full, functools.partial(step, masked=False), init)
        for d in range(n_diag):
            carry = step(n_full + d, carry, True)
        for h in range(HP):
            m, acc = carry[h]
            l = acc[:, V_HEAD:]
            o_ref[:, h * V_HEAD:(h + 1) * V_HEAD] = (acc[:, :V_HEAD] / l).astype(o_ref.dtype)
            lse_ref[h] = m + jnp.log(jnp.max(l, axis=-1, keepdims=True))

    return pl.pallas_call(
        body, grid=(N_HEADS // HP, S_ // TQ),
        in_specs=[pl.BlockSpec((TQ, W), lambda g, i: (i, g)),
                  pl.BlockSpec((S_, W), lambda g, i: (0, g)),
                  pl.BlockSpec((S_, W), lambda g, i: (0, g))],
        out_specs=[pl.BlockSpec((TQ, HP * V_HEAD), lambda g, i: (i, g)),
                   pl.BlockSpec((HP, TQ, 1), lambda g, i: (g, i, 0))],
        out_shape=[jax.ShapeDtypeStruct((S_, N_HEADS * V_HEAD), BF16), jax.ShapeDtypeStruct((N_HEADS, S_, 1), F32)],
        compiler_params=_cparams("parallel", "parallel"), name=name,
    )(q, k, vx)


def q_proj(cq, w_ext, tabs, scale, name, tm=4096):
    S_, R = cq.shape
    tm = _pick(S_, tm, 16)

    def body(c_ref, w_ref, t_c, t_s1, t_s2, o_ref):
        y = lax.dot_general(c_ref[...].astype(BF16), w_ref[...].astype(BF16), _DIMS["nn"], preferred_element_type=F32)
        o_ref[:, 0:LANES] = (y[:, 0:LANES] * scale).astype(o_ref.dtype)
        o_ref[:, LANES:2 * LANES] = (_rope(y[:, LANES:2 * LANES], t_c[...], t_s1[...], t_s2[...]) * scale).astype(o_ref.dtype)

    tab = pl.BlockSpec((tm, LANES), lambda i, h: (i, 0))
    return pl.pallas_call(
        body, grid=(S_ // tm, N_HEADS),
        in_specs=[pl.BlockSpec((tm, R), lambda i, h: (i, 0)), pl.BlockSpec((R, HEAD_PAD), lambda i, h: (0, h)),
                  tab, tab, tab],
        out_specs=pl.BlockSpec((tm, HEAD_PAD), lambda i, h: (i, h)),
        out_shape=jax.ShapeDtypeStruct((S_, N_HEADS * HEAD_PAD), BF16),
        compiler_params=_cparams("parallel", "parallel"), name=name,
    )(cq, w_ext, *tabs)


def q_proj_bwd(dq, cq, w_ext, tabs, scale, name, tm=4096):
    S_, R = cq.shape
    tm = _pick(S_, tm, 16)
    nr = S_ // tm

    def body(dq_ref, c_ref, w_ref, t_c, t_s1, t_s2, dc_ref, dw_ref, acc_ref):
        h, i = pl.program_id(0), pl.program_id(1)
        g = jnp.concatenate([dq_ref[:, 0:LANES] * scale,
                             _rope_t(dq_ref[:, LANES:2 * LANES], t_c[...], t_s1[...], t_s2[...]) * scale],
                            axis=1).astype(BF16)
        part = lax.dot_general(g, w_ref[...].astype(BF16), _DIMS["nt"], preferred_element_type=F32)
        rows = pl.ds(pl.multiple_of(i * tm, tm), tm)

        @pl.when(h == 0)
        def _():
            dc_ref[rows, :] = part

        @pl.when(h > 0)
        def _():
            dc_ref[rows, :] += part

        dwp = lax.dot_general(c_ref[...].astype(BF16), g, _DIMS["tn"], preferred_element_type=F32)

        @pl.when(i == 0)
        def _():
            acc_ref[...] = dwp

        @pl.when(i > 0)
        def _():
            acc_ref[...] += dwp

        @pl.when(i == nr - 1)
        def _():
            dw_ref[...] = acc_ref[...].astype(dw_ref.dtype)

    tab = pl.BlockSpec((tm, LANES), lambda h, i: (i, 0))
    return pl.pallas_call(
        body, grid=(N_HEADS, nr),
        in_specs=[pl.BlockSpec((tm, HEAD_PAD), lambda h, i: (i, h)), pl.BlockSpec((tm, R), lambda h, i: (i, 0)),
                  pl.BlockSpec((R, HEAD_PAD), lambda h, i: (0, h)), tab, tab, tab],
        out_specs=[pl.BlockSpec((S_, R), lambda h, i: (0, 0)), pl.BlockSpec((R, HEAD_PAD), lambda h, i: (0, h))],
        out_shape=[jax.ShapeDtypeStruct((S_, R), F32), jax.ShapeDtypeStruct((R, N_HEADS * HEAD_PAD), BF16)],
        scratch_shapes=[pltpu.VMEM((R, HEAD_PAD), F32)],
        compiler_params=_cparams("arbitrary", "arbitrary"), name=name,
    )(dq, cq, w_ext, *tabs)


def kv_proj(ckv, w_ukv, kv_ext, tabs, name, tm=2048):
    S_, R = ckv.shape
    tm = _pick(S_, tm, 16)

    def body(c_ref, wk_ref, wv_ref, t_ref, t_c, t_s1, t_s2, k_ref, vx_ref):
        cv = c_ref[...].astype(BF16)
        k_ref[:, 0:LANES] = lax.dot_general(cv, wk_ref[...].astype(BF16), _DIMS["nn"],
                                            preferred_element_type=F32).astype(k_ref.dtype)
        k_ref[:, LANES:2 * LANES] = _rope(t_ref[...], t_c[...], t_s1[...], t_s2[...]).astype(k_ref.dtype)
        vx_ref[:, 0:V_HEAD] = lax.dot_general(cv, wv_ref[...].astype(BF16), _DIMS["nn"],
                                              preferred_element_type=F32).astype(vx_ref.dtype)
        vx_ref[:, V_HEAD:HEAD_PAD] = jnp.ones((tm, HEAD_PAD - V_HEAD), vx_ref.dtype)

    tab = pl.BlockSpec((tm, LANES), lambda i, h: (i, 0))
    head = pl.BlockSpec((tm, HEAD_PAD), lambda i, h: (i, h))
    shp = jax.ShapeDtypeStruct((S_, N_HEADS * HEAD_PAD), BF16)
    return pl.pallas_call(
        body, grid=(S_ // tm, N_HEADS),
        in_specs=[pl.BlockSpec((tm, R), lambda i, h: (i, 0)), pl.BlockSpec((R, QK_NOPE), lambda i, h: (0, h)),
                  pl.BlockSpec((R, V_HEAD), lambda i, h: (0, N_HEADS + h)),
                  pl.BlockSpec((tm, LANES), lambda i, h: (i, KV_RANK // LANES)), tab, tab, tab],
        out_specs=[head, head], out_shape=[shp, shp],
        compiler_params=_cparams("parallel", "parallel"), name=name,
    )(ckv, w_ukv, w_ukv, kv_ext, *tabs)


def o_proj_bwd(dy, w_o, o, name, tm=2048):
    S_, Dn = dy.shape
    HV = w_o.shape[0]
    tm = _pick(S_, tm, 16)

    def body(dy_ref, w_ref, o_ref, do_ref, d_ref):
        do = lax.dot_general(dy_ref[...].astype(BF16), w_ref[...].astype(BF16), _DIMS["nt"], preferred_element_type=F32)
        do_ref[...] = do.astype(do_ref.dtype)
        prod = do * o_ref[...].astype(F32)
        for h in range(N_HEADS):
            d_ref[h] = jnp.sum(prod[:, h * V_HEAD:(h + 1) * V_HEAD], axis=-1, keepdims=True)

    return pl.pallas_call(
        body, grid=(S_ // tm,),
        in_specs=[pl.BlockSpec((tm, Dn), lambda i: (i, 0)), pl.BlockSpec((HV, Dn), lambda i: (0, 0)),
                  pl.BlockSpec((tm, HV), lambda i: (i, 0))],
        out_specs=[pl.BlockSpec((tm, HV), lambda i: (i, 0)), pl.BlockSpec((N_HEADS, tm, 1), lambda i: (0, i, 0))],
        out_shape=[jax.ShapeDtypeStruct((S_, HV), BF16), jax.ShapeDtypeStruct((N_HEADS, S_, 1), F32)],
        compiler_params=_cparams("parallel"), name=name,
    )(dy, w_o, o)


def attn_delta(o, do, name, tr=512):
    S_ = o.shape[0]
    tr = _pick(S_, tr, 8)

    def body(o_ref, do_ref, d_ref):
        d_ref[...] = jnp.sum(o_ref[...].astype(F32) * do_ref[...].astype(F32), axis=-1, keepdims=True)

    blk = pl.BlockSpec((tr, V_HEAD), lambda i, h: (i, h))
    return pl.pallas_call(
        body, grid=(S_ // tr, N_HEADS), in_specs=[blk, blk],
        out_specs=pl.BlockSpec((None, tr, 1), lambda i, h: (h, i, 0)),
        out_shape=jax.ShapeDtypeStruct((N_HEADS, S_, 1), F32),
        compiler_params=_cparams("parallel", "parallel"), name=name,
    )(o, do)


def attn_bwd(q, k, vx, do, lse_row, delta_row, name):
    S_ = q.shape[0]
    TK = _pick(S_, ATT_BWD_K_BLOCK, LANES)
    TQ = _pick(S_, ATT_BWD_Q_BLOCK, TK)
    HP = ATT_HEADS_PER_STEP
    W = HP * HEAD_PAD
    ratio = TQ // TK
    nq = S_ // TQ

    def body(q_ref, do_ref, lse_ref, dl_ref, k_ref, v_ref, dq_ref, dk_ref, dv_ref):
        j = pl.program_id(1)

        @pl.when(j == 0)
        def _():
            dq_ref[...] = jnp.zeros_like(dq_ref)

        ks = [k_ref[:, h * HEAD_PAD:(h + 1) * HEAD_PAD] for h in range(HP)]
        vs = [v_ref[:, h * HEAD_PAD:h * HEAD_PAD + V_HEAD] for h in range(HP)]

        def step(i, carry, masked):
            start = pl.multiple_of(i * TQ, TQ)
            out = []
            for h in range(HP):
                dk, dv = carry[h]
                cols = slice(h * HEAD_PAD, (h + 1) * HEAD_PAD)
                qv = q_ref[pl.ds(start, TQ), cols]
                dov = do_ref[pl.ds(start, TQ), h * V_HEAD:(h + 1) * V_HEAD]
                st = lax.dot_general(ks[h], qv, _DIMS["nt"], preferred_element_type=F32)
                pt = jnp.exp(st - lse_ref[h, :, pl.ds(start, TQ)])
                if masked:
                    keyi = j * TK + lax.broadcasted_iota(jnp.int32, (TK, TQ), 0)
                    qryi = i * TQ + lax.broadcasted_iota(jnp.int32, (TK, TQ), 1)
                    pt = jnp.where(keyi <= qryi, pt, 0.0)
                dpt = lax.dot_general(vs[h], dov, _DIMS["nt"], preferred_element_type=F32)
                dst = (pt * (dpt - dl_ref[h, :, pl.ds(start, TQ)])).astype(BF16)
                dv = dv + lax.dot_general(pt.astype(BF16), dov, _DIMS["nn"], preferred_element_type=F32)
                dk = dk + lax.dot_general(dst, qv, _DIMS["nn"], preferred_element_type=F32)
                dq_ref[pl.ds(start, TQ), cols] += lax.dot_general(dst, ks[h], _DIMS["tn"], preferred_element_type=F32)
                out.append((dk, dv))
            return tuple(out)

        init = tuple((jnp.zeros((TK, HEAD_PAD), F32), jnp.zeros((TK, V_HEAD), F32)) for _ in range(HP))
        first = j // ratio
        carry = lax.fori_loop(first + 1, nq, functools.partial(step, masked=False), step(first, init, True))
        for h in range(HP):
            dk_ref[:, h * HEAD_PAD:(h + 1) * HEAD_PAD] = carry[h][0]
            dv_ref[:, h * V_HEAD:(h + 1) * V_HEAD] = carry[h][1]

    return pl.pallas_call(
        body, grid=(N_HEADS // HP, S_ // TK),
        in_specs=[pl.BlockSpec((S_, W), lambda g, j: (0, g)),
                  pl.BlockSpec((S_, HP * V_HEAD), lambda g, j: (0, g)),
                  pl.BlockSpec((HP, 1, S_), lambda g, j: (g, 0, 0)),
                  pl.BlockSpec((HP, 1, S_), lambda g, j: (g, 0, 0)),
                  pl.BlockSpec((TK, W), lambda g, j: (j, g)),
                  pl.BlockSpec((TK, W), lambda g, j: (j, g))],
        out_specs=[pl.BlockSpec((S_, W), lambda g, j: (0, g)),
                   pl.BlockSpec((TK, W), lambda g, j: (j, g)),
                   pl.BlockSpec((TK, HP * V_HEAD), lambda g, j: (j, g))],
        out_shape=[jax.ShapeDtypeStruct((S_, N_HEADS * HEAD_PAD), F32),
                   jax.ShapeDtypeStruct((S_, N_HEADS * HEAD_PAD), F32),
                   jax.ShapeDtypeStruct((S_, N_HEADS * V_HEAD), F32)],
        compiler_params=_cparams("parallel", "arbitrary"), name=name,
    )(q, do, lse_row, delta_row, k, vx)


def mods_fwd(c_all, mod_w, mod_b, name, tn=512):
    L, Dn, E = mod_w.shape
    R = c_all.shape[0]
    tn = _pick(E, tn, LANES)

    def body(c_ref, w_ref, b_ref, o_ref):
        cv = c_ref[...]
        sc = (cv / (1.0 + jnp.exp(-cv))).astype(BF16)
        o_ref[...] = lax.dot_general(sc, w_ref[...].astype(BF16), _DIMS["nn"], preferred_element_type=F32) + b_ref[...]

    return pl.pallas_call(
        body, grid=(L, E // tn),
        in_specs=[pl.BlockSpec((R, Dn), lambda l, j: (0, 0)), pl.BlockSpec((None, Dn, tn), lambda l, j: (l, 0, j)),
                  pl.BlockSpec((None, 1, tn), lambda l, j: (l, 0, j))],
        out_specs=pl.BlockSpec((None, R, tn), lambda l, j: (l, 0, j)),
        out_shape=jax.ShapeDtypeStruct((L, R, E), F32),
        compiler_params=_cparams("parallel", "parallel"), name=name,
    )(c_all, mod_w, mod_b.reshape(L, 1, E))


def _adam_math(w, g, m, v):
    m = ADAM_B1 * m + (1.0 - ADAM_B1) * g
    v = ADAM_B2 * v + (1.0 - ADAM_B2) * (g * g)
    m_hat = m / (1.0 - ADAM_B1 ** ADAM_STEP)
    v_hat = v / (1.0 - ADAM_B2 ** ADAM_STEP)
    delta = -ADAM_LR * (m_hat / (jnp.sqrt(v_hat) + ADAM_EPS) + ADAM_WD * w)
    return delta, m, v


def _as2d(a):
    return a.reshape(-1, a.shape[-1]) if a.ndim != 2 else a


def adamw(w, g, m, v, name):
    shape = w.shape
    w2, g2, m2, v2 = _as2d(w), _as2d(g), _as2d(m), _as2d(v)
    R, C = w2.shape
    tr = _pick(R, max(8, (1 << 19) // C // 8 * 8), 8)

    def body(w_ref, g_ref, m_ref, v_ref, d_ref, mo_ref, vo_ref):
        d, mn, vn = _adam_math(w_ref[...], g_ref[...], m_ref[...], v_ref[...])
        d_ref[...] = d
        mo_ref[...] = mn
        vo_ref[...] = vn

    blk = pl.BlockSpec((tr, C), lambda i: (i, 0))
    shp = jax.ShapeDtypeStruct((R, C), F32)
    outs = pl.pallas_call(
        body, grid=(R // tr,), in_specs=[blk] * 4, out_specs=[blk] * 3, out_shape=[shp] * 3,
        compiler_params=_cparams("parallel"), name=name,
    )(w2, g2, m2, v2)
    return tuple(o.reshape(shape) for o in outs)


def adamw_sum(parts, w, m, v, name):
    P, R, C = parts.shape

    def body(p_ref, w_ref, m_ref, v_ref, g_ref, d_ref, mo_ref, vo_ref):
        g = p_ref[0]
        for k in range(1, P):
            g = g + p_ref[k]
        d, mn, vn = _adam_math(w_ref[...], g, m_ref[...], v_ref[...])
        g_ref[...] = g
        d_ref[...] = d
        mo_ref[...] = mn
        vo_ref[...] = vn

    shp = jax.ShapeDtypeStruct((R, C), F32)
    return pl.pallas_call(body, out_shape=[shp] * 4, compiler_params=_cparams(), name=name)(parts, w, m, v)


def adamw_modw(c_col, dm, w, m, v, name, tr=256, tn=1536):
    L, Dn, E = w.shape
    B = c_col.shape[0]
    tr = _pick(Dn, tr, 8)
    tn = _pick(E, tn, LANES)

    def body(c_ref, dm_ref, w_ref, m_ref, v_ref, g_ref, d_ref, mo_ref, vo_ref):
        g = jnp.zeros((tr, tn), F32)
        for b in range(B):
            cv = c_ref[b]
            g = g + (cv / (1.0 + jnp.exp(-cv))) * dm_ref[b:b + 1, :]
        d, mn, vn = _adam_math(w_ref[...], g, m_ref[...], v_ref[...])
        g_ref[...] = g
        d_ref[...] = d
        mo_ref[...] = mn
        vo_ref[...] = vn

    blk = pl.BlockSpec((None, tr, tn), lambda l, i, j: (l, i, j))
    shp = jax.ShapeDtypeStruct((L, Dn, E), F32)
    return pl.pallas_call(
        body, grid=(L, Dn // tr, E // tn),
        in_specs=[pl.BlockSpec((B, tr, 1), lambda l, i, j: (0, i, 0)),
                  pl.BlockSpec((None, B, tn), lambda l, i, j: (l, 0, j)), blk, blk, blk],
        out_specs=[blk] * 4, out_shape=[shp] * 4,
        compiler_params=_cparams("parallel", "parallel", "parallel"), name=name,
    )(c_col, dm, w, m, v)


def add_round(a, b, name, tr=512):
    R, C = a.shape
    tr = _pick(R, tr, 16)

    def body(a_ref, b_ref, o_ref):
        o_ref[...] = (a_ref[...] + b_ref[...].astype(F32)).astype(BF16)

    blk = pl.BlockSpec((tr, C), lambda i: (i, 0))
    return pl.pallas_call(
        body, grid=(R // tr,), in_specs=[blk, blk], out_specs=blk, out_shape=jax.ShapeDtypeStruct((R, C), BF16),
        compiler_params=_cparams("parallel"), name=name,
    )(a, b)


def sum_parts(parts, name, tr=512):
    P, R, C = parts.shape
    tr = _pick(R, tr, 16)

    def body(p_ref, o_ref):
        s = p_ref[0].astype(F32)
        for k in range(1, P):
            s = s + p_ref[k].astype(F32)
        o_ref[...] = s

    return pl.pallas_call(
        body, grid=(R // tr,), in_specs=[pl.BlockSpec((P, tr, C), lambda i: (0, i, 0))],
        out_specs=pl.BlockSpec((tr, C), lambda i: (i, 0)), out_shape=jax.ShapeDtypeStruct((R, C), F32),
        compiler_params=_cparams("parallel"), name=name,
    )(parts)


_ANY = pl.BlockSpec(memory_space=pl.ANY)


def _place():
    return lax.axis_index("x"), lax.axis_index("y"), lax.axis_index("c")


def _flip(v, bit):
    return 1 - v if bit else v


def chip_gather(buf, name):
    def body(in_ref, out_ref, send_sems, recv_sems):
        x, y, c = _place()
        me = 2 * x + y
        sends = []
        for k in range(1, N_CHIPS):
            px, py = _flip(x, k >> 1), _flip(y, k & 1)
            cp = pltpu.make_async_remote_copy(src_ref=in_ref, dst_ref=out_ref.at[me], send_sem=send_sems.at[k - 1],
                                              recv_sem=recv_sems.at[k - 1], device_id=(px, py, c), device_id_type=MESH)
            cp.start()
            sends.append(cp)
        for k in range(1, N_CHIPS):
            px, py = _flip(x, k >> 1), _flip(y, k & 1)
            pltpu.make_async_remote_copy(src_ref=in_ref, dst_ref=out_ref.at[2 * px + py], send_sem=send_sems.at[k - 1],
                                         recv_sem=recv_sems.at[k - 1], device_id=(px, py, c),
                                         device_id_type=MESH).wait_recv()
        for cp in sends:
            cp.wait_send()

    out = pl.pallas_call(
        body, in_specs=[_ANY], out_specs=_ANY,
        out_shape=jax.ShapeDtypeStruct((N_CHIPS,) + buf.shape, buf.dtype),
        scratch_shapes=[pltpu.SemaphoreType.DMA((N_CHIPS - 1,)), pltpu.SemaphoreType.DMA((N_CHIPS - 1,))],
        name=name,
    )(buf)
    return lax.dynamic_update_index_in_dim(out, buf, 2 * lax.axis_index("x") + lax.axis_index("y"), 0)


def chip_all_to_all(buf, name):
    def body(in_ref, out_ref, send_sems, recv_sems):
        x, y, c = _place()
        me = 2 * x + y
        sends = []
        for k in range(1, N_CHIPS):
            px, py = _flip(x, k >> 1), _flip(y, k & 1)
            cp = pltpu.make_async_remote_copy(src_ref=in_ref.at[2 * px + py], dst_ref=out_ref.at[me],
                                              send_sem=send_sems.at[k - 1], recv_sem=recv_sems.at[k - 1],
                                              device_id=(px, py, c), device_id_type=MESH)
            cp.start()
            sends.append(cp)
        for k in range(1, N_CHIPS):
            px, py = _flip(x, k >> 1), _flip(y, k & 1)
            pltpu.make_async_remote_copy(src_ref=in_ref.at[me], dst_ref=out_ref.at[2 * px + py],
                                         send_sem=send_sems.at[k - 1], recv_sem=recv_sems.at[k - 1],
                                         device_id=(px, py, c), device_id_type=MESH).wait_recv()
        for cp in sends:
            cp.wait_send()

    out = pl.pallas_call(
        body, in_specs=[_ANY], out_specs=_ANY, out_shape=jax.ShapeDtypeStruct(buf.shape, buf.dtype),
        scratch_shapes=[pltpu.SemaphoreType.DMA((N_CHIPS - 1,)), pltpu.SemaphoreType.DMA((N_CHIPS - 1,))],
        name=name,
    )(buf)
    me = 2 * lax.axis_index("x") + lax.axis_index("y")
    return lax.dynamic_update_index_in_dim(out, _index(buf, me), me, 0)


def core_gather(buf, name):
    def body(in_ref, out_ref, send_sem, recv_sem):
        x, y, c = _place()
        cp = pltpu.make_async_remote_copy(src_ref=in_ref, dst_ref=out_ref.at[c], send_sem=send_sem, recv_sem=recv_sem,
                                          device_id=(x, y, 1 - c), device_id_type=MESH)
        cp.start()
        pltpu.make_async_remote_copy(src_ref=in_ref, dst_ref=out_ref.at[1 - c], send_sem=send_sem, recv_sem=recv_sem,
                                     device_id=(x, y, 1 - c), device_id_type=MESH).wait_recv()
        cp.wait_send()

    out = pl.pallas_call(
        body, in_specs=[_ANY], out_specs=_ANY, out_shape=jax.ShapeDtypeStruct((2,) + buf.shape, buf.dtype),
        scratch_shapes=[pltpu.SemaphoreType.DMA, pltpu.SemaphoreType.DMA],
        name=name,
    )(buf)
    return lax.dynamic_update_index_in_dim(out, buf, lax.axis_index("c"), 0)


def core_swap(buf, name):
    def body(in_ref, out_ref, send_sem, recv_sem):
        x, y, c = _place()
        cp = pltpu.make_async_remote_copy(src_ref=in_ref, dst_ref=out_ref, send_sem=send_sem, recv_sem=recv_sem,
                                          device_id=(x, y, 1 - c), device_id_type=MESH)
        cp.start()
        cp.wait()

    return pl.pallas_call(
        body, in_specs=[_ANY], out_specs=_ANY, out_shape=jax.ShapeDtypeStruct(buf.shape, buf.dtype),
        scratch_shapes=[pltpu.SemaphoreType.DMA, pltpu.SemaphoreType.DMA],
        name=name,
    )(buf)


def device_gather(buf, name):
    def body(in_ref, out_ref, send_sems, recv_sems, local_sem):
        x, y, c = _place()
        me = 4 * x + 2 * y + c
        mine = pltpu.make_async_copy(in_ref, out_ref.at[me], local_sem)
        mine.start()
        sends = []
        for k in range(1, N_DEV):
            peer = (_flip(x, (k >> 2) & 1), _flip(y, (k >> 1) & 1), _flip(c, k & 1))
            cp = pltpu.make_async_remote_copy(src_ref=in_ref, dst_ref=out_ref.at[me], send_sem=send_sems.at[k - 1],
                                              recv_sem=recv_sems.at[k - 1], device_id=peer, device_id_type=MESH)
            cp.start()
            sends.append(cp)
        for k in range(1, N_DEV):
            peer = (_flip(x, (k >> 2) & 1), _flip(y, (k >> 1) & 1), _flip(c, k & 1))
            pltpu.make_async_remote_copy(src_ref=in_ref, dst_ref=out_ref.at[4 * peer[0] + 2 * peer[1] + peer[2]],
                                         send_sem=send_sems.at[k - 1], recv_sem=recv_sems.at[k - 1], device_id=peer,
                                         device_id_type=MESH).wait_recv()
        for cp in sends:
            cp.wait_send()
        mine.wait()

    return pl.pallas_call(
        body, in_specs=[_ANY], out_specs=_ANY, out_shape=jax.ShapeDtypeStruct((N_DEV,) + buf.shape, buf.dtype),
        scratch_shapes=[pltpu.SemaphoreType.DMA((N_DEV - 1,)), pltpu.SemaphoreType.DMA((N_DEV - 1,)),
                        pltpu.SemaphoreType.DMA],
        name=name,
    )(buf)


def _region(ref, chip_axis=None, chip=None, chip_size=None, half_axis=None, half=None, half_size=None):
    idx = [slice(None)] * len(ref.shape)
    if chip is not None:
        idx[chip_axis] = pl.ds(chip * chip_size, chip_size)
    if half is not None:
        idx[half_axis] = pl.ds(half * half_size, half_size)
    return ref.at[tuple(idx)]


def gather_weights(shards, axes, name, after=()):
    n = len(shards)

    def full_shape(t):
        shp = list(shards[t].shape)
        shp[axes[t][0]] *= N_CHIPS
        return tuple(shp)

    def body(*refs):
        ins, outs = refs[:n], refs[n + len(after):2 * n + len(after)]
        ici_send, ici_recv, d2d_send, d2d_recv, own_send, own_recv = refs[2 * n + len(after):]
        x, y, c = _place()
        me = 2 * x + y

        def part(t, ref, chip, half):
            ca, ha = axes[t]
            return _region(ref, ca, chip, ins[t].shape[ca], ha, half, ins[t].shape[ha] // 2)

        def own(t):
            return pltpu.make_async_remote_copy(src_ref=ins[t], dst_ref=part(t, outs[t], me, None),
                                                send_sem=own_send.at[t], recv_sem=own_recv.at[t],
                                                device_id=(x, y, 1 - c), device_id_type=MESH)

        started = []
        for t in range(n):
            own(t).start()
            started.append(own(t))
        for t in range(n):
            for k in range(1, N_CHIPS):
                px, py = _flip(x, k >> 1), _flip(y, k & 1)
                cp = pltpu.make_async_remote_copy(src_ref=part(t, ins[t], None, c), dst_ref=part(t, outs[t], me, c),
                                                  send_sem=ici_send.at[t, k - 1], recv_sem=ici_recv.at[t, k - 1],
                                                  device_id=(px, py, c), device_id_type=MESH)
                cp.start()
                started.append(cp)
        for t in range(n):
            for k in range(1, N_CHIPS):
                px, py = _flip(x, k >> 1), _flip(y, k & 1)
                got = part(t, outs[t], 2 * px + py, c)
                pltpu.make_async_remote_copy(src_ref=part(t, ins[t], None, c), dst_ref=got,
                                             send_sem=ici_send.at[t, k - 1], recv_sem=ici_recv.at[t, k - 1],
                                             device_id=(px, py, c), device_id_type=MESH).wait_recv()
                fw = pltpu.make_async_remote_copy(src_ref=got, dst_ref=got, send_sem=d2d_send.at[t, k - 1],
                                                  recv_sem=d2d_recv.at[t, k - 1], device_id=(x, y, 1 - c),
                                                  device_id_type=MESH)
                fw.start()
                started.append(fw)
        for t in range(n):
            for k in range(1, N_CHIPS):
                px, py = _flip(x, k >> 1), _flip(y, k & 1)
                theirs = part(t, outs[t], 2 * px + py, 1 - c)
                pltpu.make_async_remote_copy(src_ref=theirs, dst_ref=theirs, send_sem=d2d_send.at[t, k - 1],
                                             recv_sem=d2d_recv.at[t, k - 1], device_id=(x, y, 1 - c),
                                             device_id_type=MESH).wait_recv()
        for t in range(n):
            own(t).wait_recv()
        for cp in started:
            cp.wait_send()

    sem = pltpu.SemaphoreType.DMA((n, N_CHIPS - 1))
    own_sem = pltpu.SemaphoreType.DMA((n,))
    return pl.pallas_call(
        body, in_specs=[_ANY] * (n + len(after)), out_specs=[_ANY] * n,
        out_shape=[jax.ShapeDtypeStruct(full_shape(t), shards[t].dtype) for t in range(n)],
        scratch_shapes=[sem, sem, sem, sem, own_sem, own_sem], name=name,
    )(*shards, *after)


_HBM = pl.BlockSpec(memory_space=pltpu.HBM)
_SEM = pl.BlockSpec(memory_space=pltpu.SEMAPHORE)
_EFFECT = pltpu.SideEffectType.DATAFLOW_SIDE_EFFECTING
WEIGHT_COPIES = N_CHIPS


def _weight_peer(k, x, y, c):
    return (x, y, 1 - c) if k == 0 else (_flip(x, k >> 1), _flip(y, k & 1), c)


def weights_start(shards, items, name, after=()):
    n_sh, n_it = len(shards), len(items)

    def src_of(refs, i):
        t, layer, _ = items[i]
        return refs[t] if layer is None else refs[t].at[layer]

    def land_shape(i):
        t, layer, ca = items[i]
        shp = list(shards[t].shape if layer is None else shards[t].shape[1:])
        shp[ca] *= N_CHIPS
        return tuple(shp)

    def body(*refs):
        shard_refs, land_refs = refs[:n_sh], refs[n_sh:n_sh + n_it]
        first_out = n_sh + n_it + len(after)
        send_sems = refs[first_out:first_out + n_it]
        recv_sems = refs[first_out + n_it:first_out + 2 * n_it]
        token = refs[-1]
        x, y, c = _place()
        me = 2 * x + y
        for i in range(n_it):
            src = src_of(shard_refs, i)
            ca = items[i][2]
            dst = _region(land_refs[i], ca, me, src.shape[ca])
            for k in range(WEIGHT_COPIES):
                pltpu.make_async_remote_copy(src_ref=src, dst_ref=dst, send_sem=send_sems[i], recv_sem=recv_sems[i],
                                             device_id=_weight_peer(k, x, y, c), device_id_type=MESH).start()
        token[...] = jnp.zeros_like(token)

    lands = [pltpu.with_memory_space_constraint(lax.empty(land_shape(i), shards[0].dtype), pltpu.HBM)
             for i in range(n_it)]
    ins = [pltpu.with_memory_space_constraint(a, pltpu.HBM) for a in shards] + lands
    sems = (pltpu.SemaphoreType.DMA(()),) * (2 * n_it)
    outs = pl.pallas_call(
        body, name=name,
        out_shape=sems + tuple(pltpu.HBM(a.shape, a.dtype) for a in ins) + (jax.ShapeDtypeStruct((8, LANES), F32),),
        in_specs=[_HBM] * len(ins) + [_ANY] * len(after),
        out_specs=(_SEM,) * (2 * n_it) + (_HBM,) * len(ins) + (pl.BlockSpec(memory_space=pltpu.VMEM),),
        input_output_aliases={i: 2 * n_it + i for i in range(len(ins))},
        compiler_params=pltpu.CompilerParams(has_side_effects=_EFFECT),
    )(*ins, *after)
    base = 2 * n_it
    return (list(outs[:n_it]), list(outs[n_it:base]), list(outs[base:base + n_sh]),
            list(outs[base + n_sh:base + n_sh + n_it]), outs[-1])


def weights_wait(send_sems, recv_sems, lands, after, keep, name):
    m = len(lands)

    def body(*refs):
        land_refs, send_refs, recv_refs = refs[:m], refs[m:2 * m], refs[2 * m:3 * m]
        x, y, c = _place()
        for j in range(m):
            cp = pltpu.make_async_remote_copy(src_ref=land_refs[j], dst_ref=land_refs[j], send_sem=send_refs[j],
                                              recv_sem=recv_refs[j], device_id=(x, y, 1 - c),
                                              device_id_type=MESH)
            cp.wait_send()
            cp.wait_recv()

    outs = pl.pallas_call(
        body, name=name,
        out_shape=tuple(pltpu.HBM(a.shape, a.dtype) for a in lands),
        in_specs=[_HBM] * m + [_SEM] * (2 * m) + [_ANY] + [_HBM] * len(keep),
        out_specs=(_HBM,) * m,
        input_output_aliases={j: j for j in range(m)},
        compiler_params=pltpu.CompilerParams(has_side_effects=_EFFECT),
    )(*lands, *send_sems, *recv_sems, after, *keep)
    return list(outs)


def reduce_to_sibling(lo, hi, name):
    n = len(lo)

    def body(*refs):
        los, his, outs = refs[:n], refs[n:2 * n], refs[2 * n:3 * n]
        send_sems, recv_sems = refs[3 * n:]
        x, y, c = _place()

        def copy(u, src):
            return pltpu.make_async_remote_copy(src_ref=src, dst_ref=outs[u], send_sem=send_sems.at[u],
                                                recv_sem=recv_sems.at[u], device_id=(x, y, 1 - c), device_id_type=MESH)

        for u in range(n):
            @pl.when(c == 0)
            def _(u=u):
                copy(u, his[u]).start()

            @pl.when(c == 1)
            def _(u=u):
                copy(u, los[u]).start()
        for u in range(n):
            copy(u, los[u]).wait_recv()
        for u in range(n):
            copy(u, los[u]).wait_send()

    return pl.pallas_call(
        body, in_specs=[_ANY] * (2 * n), out_specs=[_ANY] * n,
        out_shape=[jax.ShapeDtypeStruct(a.shape, a.dtype) for a in lo],
        scratch_shapes=[pltpu.SemaphoreType.DMA((n,)), pltpu.SemaphoreType.DMA((n,))], name=name,
    )(*lo, *hi)


def add_selected(lo, hi, other, name, tile_elems=1 << 19):
    R, C = lo.shape
    tr = _pick(R, max(16, tile_elems // C // 16 * 16), 16)

    def body(lo_ref, hi_ref, o_ref, out_ref):
        mine = jnp.where(lax.axis_index("c") == 0, lo_ref[...].astype(F32), hi_ref[...].astype(F32))
        out_ref[...] = (mine + o_ref[...].astype(F32)).astype(out_ref.dtype)

    blk = pl.BlockSpec((tr, C), lambda i: (i, 0))
    return pl.pallas_call(
        body, grid=(R // tr,), in_specs=[blk, blk, blk], out_specs=blk, out_shape=jax.ShapeDtypeStruct((R, C), BF16),
        compiler_params=_cparams("parallel"), name=name,
    )(lo, hi, other)


def scatter_to_chips(pieces, chip_axes, name):
    n = len(pieces)

    def block_shape(u):
        shp = list(pieces[u].shape)
        shp[chip_axes[u]] //= N_CHIPS
        return tuple(shp)

    def body(*refs):
        ins, outs = refs[:n], refs[n:2 * n]
        send_sems, recv_sems = refs[2 * n:]
        x, y, c = _place()
        me = 2 * x + y
        started = []
        for u in range(n):
            size = block_shape(u)[chip_axes[u]]
            for k in range(1, N_CHIPS):
                px, py = _flip(x, k >> 1), _flip(y, k & 1)
                cp = pltpu.make_async_remote_copy(src_ref=_region(ins[u], chip_axes[u], 2 * px + py, size),
                                                  dst_ref=outs[u].at[me], send_sem=send_sems.at[u, k - 1],
                                                  recv_sem=recv_sems.at[u, k - 1], device_id=(px, py, c),
                                                  device_id_type=MESH)
                cp.start()
                started.append(cp)
        for u in range(n):
            size = block_shape(u)[chip_axes[u]]
            for k in range(1, N_CHIPS):
                px, py = _flip(x, k >> 1), _flip(y, k & 1)
                pltpu.make_async_remote_copy(src_ref=_region(ins[u], chip_axes[u], me, size),
                                             dst_ref=outs[u].at[2 * px + py], send_sem=send_sems.at[u, k - 1],
                                             recv_sem=recv_sems.at[u, k - 1], device_id=(px, py, c),
                                             device_id_type=MESH).wait_recv()
        for cp in started:
            cp.wait_send()

    sem = pltpu.SemaphoreType.DMA((n, N_CHIPS - 1))
    return pl.pallas_call(
        body, in_specs=[_ANY] * n, out_specs=[_ANY] * n,
        out_shape=[jax.ShapeDtypeStruct((N_CHIPS,) + block_shape(u), pieces[u].dtype) for u in range(n)],
        scratch_shapes=[sem, sem], name=name,
    )(*pieces)


def scatter_start(pieces, chip_axes, name, after=()):
    n = len(pieces)

    def block_shape(u):
        shp = list(pieces[u].shape)
        shp[chip_axes[u]] //= N_CHIPS
        return tuple(shp)

    def body(*refs):
        ins, land_refs = refs[:n], refs[n:2 * n]
        first_out = 2 * n + len(after)
        send_sems, recv_sems = refs[first_out:first_out + n], refs[first_out + n:first_out + 2 * n]
        token = refs[-1]
        x, y, c = _place()
        me = 2 * x + y
        for u in range(n):
            size = block_shape(u)[chip_axes[u]]
            for k in range(1, N_CHIPS):
                px, py = _flip(x, k >> 1), _flip(y, k & 1)
                pltpu.make_async_remote_copy(src_ref=_region(ins[u], chip_axes[u], 2 * px + py, size),
                                             dst_ref=land_refs[u].at[me], send_sem=send_sems[u], recv_sem=recv_sems[u],
                                             device_id=(px, py, c), device_id_type=MESH).start()
        token[...] = jnp.zeros_like(token)

    lands = [pltpu.with_memory_space_constraint(lax.empty((N_CHIPS,) + block_shape(u), pieces[u].dtype), pltpu.HBM)
             for u in range(n)]
    ins = [pltpu.with_memory_space_constraint(a, pltpu.HBM) for a in pieces] + lands
    sems = (pltpu.SemaphoreType.DMA(()),) * (2 * n)
    outs = pl.pallas_call(
        body, name=name,
        out_shape=sems + tuple(pltpu.HBM(a.shape, a.dtype) for a in ins) + (jax.ShapeDtypeStruct((8, LANES), F32),),
        in_specs=[_HBM] * len(ins) + [_ANY] * len(after),
        out_specs=(_SEM,) * (2 * n) + (_HBM,) * len(ins) + (pl.BlockSpec(memory_space=pltpu.VMEM),),
        input_output_aliases={i: 2 * n + i for i in range(len(ins))},
        compiler_params=pltpu.CompilerParams(has_side_effects=_EFFECT),
    )(*ins, *after)
    return list(outs[:n]), list(outs[n:2 * n]), list(outs[2 * n:3 * n]), list(outs[3 * n:4 * n]), outs[-1]


def scatter_wait(send_sems, recv_sems, lands, pieces, after, name):
    n = len(lands)

    def body(*refs):
        land_refs, send_refs, recv_refs = refs[:n], refs[n:2 * n], refs[2 * n:3 * n]
        x, y, c = _place()
        for u in range(n):
            three = land_refs[u].at[pl.ds(0, N_CHIPS - 1)]
            cp = pltpu.make_async_remote_copy(src_ref=three, dst_ref=three, send_sem=send_refs[u], recv_sem=recv_refs[u],
                                              device_id=(x, y, 1 - c), device_id_type=MESH)
            cp.wait_send()
            cp.wait_recv()

    outs = pl.pallas_call(
        body, name=name,
        out_shape=tuple(pltpu.HBM(a.shape, a.dtype) for a in lands),
        in_specs=[_HBM] * n + [_SEM] * (2 * n) + [_ANY] + [_HBM] * len(pieces),
        out_specs=(_HBM,) * n,
        input_output_aliases={j: j for j in range(n)},
        compiler_params=pltpu.CompilerParams(has_side_effects=_EFFECT),
    )(*lands, *send_sems, *recv_sems, after, *pieces)
    return list(outs)


def gather_halves(parts, slots, out_shapes, name):
    n = len(parts)

    def body(*refs):
        ins, outs = refs[:n], refs[n:n + len(out_shapes)]
        send_sems, recv_sems = refs[n + len(out_shapes):]
        x, y, c = _place()
        started = []
        for u in range(n):
            t, s = slots[u]
            cp = pltpu.make_async_remote_copy(src_ref=ins[u], dst_ref=outs[t].at[s, c], send_sem=send_sems.at[u],
                                              recv_sem=recv_sems.at[u], device_id=(x, y, 1 - c), device_id_type=MESH)
            cp.start()
            started.append(cp)
        for u in range(n):
            t, s = slots[u]
            pltpu.make_async_remote_copy(src_ref=ins[u], dst_ref=outs[t].at[s, 1 - c], send_sem=send_sems.at[u],
                                         recv_sem=recv_sems.at[u], device_id=(x, y, 1 - c),
                                         device_id_type=MESH).wait_recv()
        for cp in started:
            cp.wait_send()

    return pl.pallas_call(
        body, in_specs=[_ANY] * n, out_specs=[_ANY] * len(out_shapes),
        out_shape=[jax.ShapeDtypeStruct(shp, F32) for shp in out_shapes],
        scratch_shapes=[pltpu.SemaphoreType.DMA((n,)), pltpu.SemaphoreType.DMA((n,))], name=name,
    )(*parts)


WEIGHT_ORDER = ["mod_w", "mod_b", "norm1_g", "norm2_g", "pool_w", "pool_b", "pool_scale", "kv_in_g", "w_dkv",
                "ckv_norm_g", "w_uk", "w_uv", "w_dq", "q_norm_g", "w_uq", "w_o", "w_up", "conv_w", "conv_b", "w_down",
                "final_g"]
EXCHANGED = {"w_up": (2, 0), "w_down": (1, 0), "w_o": (1, 0), "w_uq": (2, 0), "w_dq": (1, 0), "pool_w": (2, 0),
             "w_dkv": (0, 1), "w_uk": (1, 0), "w_uv": (1, 0)}
SMALL_SHARDED = {"conv_w": 2, "pool_b": 1, "pool_scale": 1}
REPLICATED = ["mod_b", "norm1_g", "norm2_g", "kv_in_g", "ckv_norm_g", "q_norm_g", "conv_b", "final_g"]


def _padded(n, align):
    return -(-n // align) * align


def _flat_pad(parts, total):
    flat = jnp.concatenate(parts, axis=-1)
    pad = total - flat.shape[-1]
    if pad:
        flat = jnp.concatenate([flat, jnp.zeros(flat.shape[:-1] + (pad,), flat.dtype)], axis=-1)
    return flat


def _split_shards(full, axis):
    shp = full.shape
    t = full.reshape(shp[:axis] + (N_CHIPS, shp[axis] // N_CHIPS) + shp[axis + 1:])
    return jnp.moveaxis(t, axis, 0).reshape(N_CHIPS, -1)


def _join_shards(rows, shard_shape, axis):
    t = jnp.moveaxis(rows.reshape((N_CHIPS,) + tuple(shard_shape)), 0, axis)
    return t.reshape(tuple(shard_shape[:axis]) + (N_CHIPS * shard_shape[axis],) + tuple(shard_shape[axis + 1:]))


def _index(a, i, axis=0):
    return lax.dynamic_index_in_dim(a, i, axis, keepdims=False)


def kernel(x, c, positions, mod_w, mod_b, norm1_g, norm2_g, pool_w, pool_b, pool_scale, kv_in_g, w_dkv, ckv_norm_g, w_uk, w_uv, w_dq, q_norm_g, w_uq, w_o, w_up, conv_w, conv_b, w_down, final_g, loss_target, m_mod_w, m_mod_b, m_norm1_g, m_norm2_g, m_pool_w, m_pool_b, m_pool_scale, m_kv_in_g, m_w_dkv, m_ckv_norm_g, m_w_uk, m_w_uv, m_w_dq, m_q_norm_g, m_w_uq, m_w_o, m_w_up, m_conv_w, m_conv_b, m_w_down, m_final_g, v_mod_w, v_mod_b, v_norm1_g, v_norm2_g, v_pool_w, v_pool_b, v_pool_scale, v_kv_in_g, v_w_dkv, v_ckv_norm_g, v_w_uk, v_w_uv, v_w_dq, v_q_norm_g, v_w_uq, v_w_o, v_w_up, v_conv_w, v_conv_b, v_w_down, v_final_g):
    given = dict(locals())
    W = {n: given[n] for n in WEIGHT_ORDER}
    M1 = {n: given["m_" + n] for n in WEIGHT_ORDER}
    V2 = {n: given["v_" + n] for n in WEIGHT_ORDER}
    xi, yi, ci = lax.axis_index("x"), lax.axis_index("y"), lax.axis_index("c")
    chip = 2 * xi + yi
    dev = 4 * xi + 2 * yi + ci
    x0 = x[0]
    S_, D = x0.shape
    Fh = conv_b.shape[1]
    E = mod_b.shape[1]
    Es = E // N_CHIPS
    zD = jnp.zeros((D,), F32)

    c_all = device_gather(c, "gather_c").reshape(N_DEV, D)
    c_pad = jnp.concatenate([c_all, jnp.zeros((16 - N_DEV, D), F32)], axis=0)
    mod_b_mine = lax.dynamic_slice_in_dim(mod_b, chip * Es, Es, axis=1)
    mods_part = mods_fwd(c_pad, mod_w, mod_b_mine, "mods_fwd")
    mods_all = chip_gather(mods_part, "gather_mods")
    mods = jnp.swapaxes(_index(mods_all, dev, axis=2), 0, 1).reshape(DEPTH, E)
    mod = [[mods[l, k * D:(k + 1) * D] for k in range(6)] for l in range(DEPTH)]

    full = {}
    ssz = {n: math.prod(W[n].shape) for n in SMALL_SHARDED}
    Tw = _padded(sum(ssz.values()), 8 * PACK_COLS)
    small_rows = chip_gather(_flat_pad([W[n].reshape(-1) for n in SMALL_SHARDED], Tw).reshape(-1, PACK_COLS),
                             "gather_small_w").reshape(N_CHIPS, Tw)
    off = 0
    for n, axis in SMALL_SHARDED.items():
        full[n] = _join_shards(small_rows[:, off:off + ssz[n]], W[n].shape, axis)
        off += ssz[n]

    names = list(EXCHANGED)
    shards = [W[n].astype(BF16) for n in names]
    n_mla = DEPTH - N_A
    first_axes = {"w_up": (1, 0), "w_down": (0, 1), "pool_w": (1, 0)}
    first = gather_weights([shards[names.index(n)][0] for n in first_axes], list(first_axes.values()), "gather_weights0",
                           after=[mods, small_rows])
    for n, arr in zip(first_axes, first):
        full[(n, 0)] = arr
    items, groups = [], []

    def group(entries):
        groups.append(list(range(len(items), len(items) + len(entries))))
        for n, layer in entries:
            ca = EXCHANGED[n][0] - (0 if layer is None else 1)
            items.append((names.index(n), layer, 0 if n == "w_dkv" else ca))

    for l in range(1, N_A):
        group([("w_up", l), ("w_down", l), ("pool_w", l)])
    for j in range(n_mla):
        head = [("w_dkv", None), ("w_uk", None), ("w_uv", None)] if j == 0 else []
        group(head + [("w_dq", j), ("w_uq", j), ("w_o", j), ("w_up", N_A + j), ("w_down", N_A + j)])
    w_send, w_recv, shards_thru, lands, _ = weights_start(shards, items, "weights_start", after=first)

    def weights_ready(g, after):
        keep = shards_thru if g == len(groups) - 1 else []
        got = weights_wait([w_send[i] for i in groups[g]], [w_recv[i] for i in groups[g]], [lands[i] for i in groups[g]],
                           after, keep, f"weights_wait{g}")
        for i, arr in zip(groups[g], got):
            t, layer, _ = items[i]
            full[(names[t], 0 if layer is None else layer)] = arr

    q_rank = W["w_uq"].shape[1]
    kv_w = KV_RANK + QK_ROPE

    def uq_ext(j):
        wq = full[("w_uq", j)].reshape(q_rank, N_HEADS, QK_HEAD)
        return jnp.concatenate([wq, jnp.zeros((q_rank, N_HEADS, HEAD_PAD - QK_HEAD), BF16)],
                               axis=2).reshape(q_rank, N_HEADS * HEAD_PAD)


    half = QK_ROPE // 2
    inv = 1.0 / (ROPE_THETA ** (jnp.arange(0, QK_ROPE, 2, dtype=F32) / QK_ROPE))
    inv_row = jnp.concatenate([inv, inv, jnp.zeros((LANES - 2 * half,), F32)]).reshape(1, LANES)
    tabs = rope_tables(positions[0].astype(F32).reshape(S_, 1), inv_row, "rope_tables")
    att_scale = QK_HEAD ** -0.5

    saved = []
    xcur = x0
    kv_saved = None
    K = VX = knv = None
    for l in range(DEPTH):
        sh1, sc1, g1, sh2, sc2, g2 = mod[l]
        st = {"xin": xcur}
        if l:
            weights_ready(l - 1, xcur)
        if l == N_A:
            w_dkv_ext = jnp.concatenate([full[("w_dkv", 0)], jnp.zeros((D, KV_RANK + LANES - kv_w), BF16)], axis=1)
            w_ukv = jnp.concatenate([full[("w_uk", 0)], full[("w_uv", 0)]], axis=1)
            xn = norm_fwd(xcur, kv_in_g, zD, zD, BF16, "kvin_fwd")
            kv_ext = mm(xn, w_dkv_ext, "nn", F32, "dkv_mm")
            lat = kv_ext[:, :KV_RANK]
            zk = jnp.zeros((KV_RANK,), F32)
            ckv = norm_fwd(lat, ckv_norm_g, zk, zk, BF16, "ckv_fwd")
            K, VX = kv_proj(ckv, w_ukv, kv_ext, tabs, "ukv_mm")
            kv_saved = {"x": xcur, "xn": xn, "lat": lat, "ckv": ckv}
        if l < N_A:
            h1 = norm_fwd(xcur, norm1_g[l], sc1, sh1, F32, f"norm1_fwd{l}")
            st["pooled"] = _pool_call(h1, BF16, f"pool_fwd{l}", False)
            st["cs"] = g1 * full["pool_scale"][l]
            st["ypre"], xmid = gmm(st["pooled"], full[("pool_w", l)], "nn", BF16, f"pool_mm{l}", bias=full["pool_b"][l],
                                   res=xcur, colscale=st["cs"])
        else:
            j = l - N_A
            st["h1"] = norm_fwd(xcur, norm1_g[l], sc1, sh1, BF16, f"norm1_fwd{l}")
            st["ql"] = mm(st["h1"], full[("w_dq", j)], "nn", F32, f"dq_mm{l}")
            st["cq"] = norm_fwd(st["ql"], q_norm_g[j], jnp.zeros_like(q_norm_g[j]), jnp.zeros_like(q_norm_g[j]), BF16,
                                f"qnorm_fwd{l}")
            st["w_uq_ext"] = uq_ext(j)
            st["Q"] = q_proj(st["cq"], st["w_uq_ext"], tabs, att_scale, f"uq_mm{l}")
            st["o"], lse = attn_fwd(st["Q"], K, VX, f"attn_fwd{l}")
            st["lse"] = lse.reshape(N_HEADS, 1, S_)
            st["y"], xmid = mm(st["o"], full[("w_o", j)], "nn", BF16, f"wo_mm{l}", res=xcur, colscale=g1)
        st["xmid"] = xmid
        st["h2"] = norm_fwd(xmid, norm2_g[l], sc2, sh2, BF16, f"norm2_fwd{l}")
        st["u"] = mm(st["h2"], full[("w_up", l)], "nn", BF16, f"up_mm{l}")
        st["z"] = glu_fwd(st["u"], full["conv_w"][l], conv_b[l], f"glu_fwd{l}")
        st["f"], xcur = mm(st["z"], full[("w_down", l)], "nn", BF16, f"down_mm{l}", tk=1408, res=xmid, colscale=g2)
        saved.append(st)

    dx, d_final_g, loss_part = loss_head(xcur, final_g, loss_target[0], "loss_head")
    loss = lax.psum(loss_part[0, 0], ("x", "y", "c"))

    def begin_reduce(tensors, first_slot, tag):
        units = []
        for n in tensors:
            ca = EXCHANGED[n][0]
            if W[n].ndim > 2:
                n_slots = W[n].shape[0] // 2
                for sl in range(first_slot if n_slots > 1 else 0, first_slot + 1 if n_slots > 1 else 1):
                    units.append((n, sl, G[(n, 2 * sl)], G[(n, 2 * sl + 1)], ca - 1))
            elif n == "w_dkv":
                g4 = G[(n, 0)].reshape(N_CHIPS, 2, -1, kv_w)
                units.append((n, 0, g4[:, 0], g4[:, 1], 0))
            else:
                rows_half = W[n].shape[0] // 2
                units.append((n, 0, G[(n, 0)][:rows_half], G[(n, 0)][rows_half:], ca))
        lo = [u[2] for u in units]
        hi = [u[3] for u in units]
        theirs = reduce_to_sibling(lo, hi, f"reduce_cores_{tag}")
        sums = [add_selected(l_.reshape(-1, l_.shape[-1]), h_.reshape(-1, l_.shape[-1]), t_.reshape(-1, l_.shape[-1]),
                             f"reduce_cores_add_{tag}{i}").reshape(l_.shape)
                for i, (l_, h_, t_) in enumerate(zip(lo, hi, theirs))]
        return units, sums, [u[4] for u in units]

    G = {}
    dmods = [None] * DEPTH
    d_norm1 = [None] * DEPTH
    d_norm2 = [None] * DEPTH
    d_conv_b = [None] * DEPTH
    d_qnorm = [None] * n_mla
    dkv_acc = []
    df, a2, _ = gate_bwd(dx, saved[DEPTH - 1]["f"], mod[DEPTH - 1][5], f"gate2_bwd{DEPTH - 1}")
    for l in reversed(range(DEPTH)):
        sh1, sc1, g1, sh2, sc2, g2 = mod[l]
        st = saved[l]
        next_gate = (saved[l - 1]["f"], mod[l - 1][5]) if l else None
        dz = mm(df, full[("w_down", l)], "nt", BF16, f"down_dx{l}")
        G[("w_down", l)] = mm(st["z"], df, "tn", BF16, f"down_dw{l}")
        du, dcw, dcb = glu_bwd(st["u"], dz, full["conv_w"][l], conv_b[l], f"glu_bwd{l}")
        G[("conv_w", l)] = dcw
        d_conv_b[l] = dcb[0]
        dh2 = mm(du, full[("w_up", l)], "nt", BF16, f"up_dx{l}", tk=1408)
        G[("w_up", l)] = mm(st["h2"], du, "tn", BF16, f"up_dw{l}")
        dxmid, s1, s2, dgate, a1, csum = norm_bwd(st["xmid"], norm2_g[l], sc2, dh2, dx, f"norm2_bwd{l}",
                                                  gate=(st["ypre"], st["cs"]) if l < N_A else (st["y"], g1))
        dsh2, dsc2, d_norm2[l] = s1[0], s2[0] * norm2_g[l], s2[0] * (1.0 + sc2)
        if l < N_A:
            dyp = dgate
            dg1 = full["pool_scale"][l] * a1[0]
            G[("pool_scale", l)] = g1 * a1[0]
            G[("pool_b", l)] = st["cs"] * csum[0]
            dpooled = gmm(dyp, full[("pool_w", l)], "nt", F32, f"pool_dx{l}")
            G[("pool_w", l)] = gmm(st["pooled"], dyp, "tn", BF16, f"pool_dw{l}")
            dh1 = _pool_call(dpooled, F32, f"pool_bwd{l}", True)
        else:
            j = l - N_A
            dy = dgate
            dg1 = a1[0]
            do, delta = o_proj_bwd(dy, full[("w_o", j)], st["o"], f"wo_dx{l}")
            delta = delta.reshape(N_HEADS, 1, S_)
            G[("w_o", j)] = mm(st["o"], dy, "tn", BF16, f"wo_dw{l}")
            dQ, dK, dV = attn_bwd(st["Q"], K, VX, do, st["lse"], delta, f"attn_bwd{l}")
            dkv_acc.append((dK, dV))
            dcq, dw_ext = q_proj_bwd(dQ, st["cq"], st["w_uq_ext"], tabs, att_scale, f"uq_bwd{l}")
            G[("w_uq", j)] = dw_ext.reshape(q_rank, N_HEADS, HEAD_PAD)[:, :, :QK_HEAD].reshape(q_rank, N_HEADS * QK_HEAD)
            zq = jnp.zeros_like(q_norm_g[j])
            dql, _, s2q = norm_bwd(st["ql"], q_norm_g[j], zq, dcq, None, f"qnorm_bwd{l}")
            d_qnorm[j] = s2q[0]
            dh1 = mm(dql, full[("w_dq", j)], "nt", BF16, f"dq_dx{l}")
            G[("w_dq", j)] = mm(st["h1"], dql, "tn", BF16, f"dq_dw{l}")
        a2_mine = a2
        if l and l != N_A:
            dx, s1, s2, df, a2, _ = norm_bwd(st["xin"], norm1_g[l], sc1, dh1, dxmid, f"norm1_bwd{l}", gate=next_gate)
        else:
            dx, s1, s2 = norm_bwd(st["xin"], norm1_g[l], sc1, dh1, dxmid, f"norm1_bwd{l}")
        dsh1, dsc1, d_norm1[l] = s1[0], s2[0] * norm1_g[l], s2[0] * (1.0 + sc1)
        dmods[l] = jnp.concatenate([dsh1, dsc1, dg1, dsh2, dsc2, a2_mine[0]])
        if l == N_A:
            (dk_a, dv_a), (dk_b, dv_b) = dkv_acc
            dknv, d_tk = k_prep_bwd(dk_a, dk_b, dv_a, dv_b, tabs, "k_prep_bwd")
            dckv = mm(dknv, w_ukv, "nt", F32, "ukv_dx")
            d_ukv = mm(kv_saved["ckv"], dknv, "tn", BF16, "ukv_dw")
            G[("w_uk", 0)], G[("w_uv", 0)] = d_ukv[:, :N_HEADS * QK_NOPE], d_ukv[:, N_HEADS * QK_NOPE:]
            zk = jnp.zeros((KV_RANK,), F32)
            dlat, _, s2c = norm_bwd(kv_saved["lat"], ckv_norm_g, zk, dckv, None, "ckv_bwd")
            d_ckv_g = s2c[0]
            dkv_ext = jnp.concatenate([dlat, d_tk], axis=1)
            dxn = mm(dkv_ext, w_dkv_ext, "nt", BF16, "dkv_dx")
            G[("w_dkv", 0)] = mm(kv_saved["xn"], dkv_ext, "tn", BF16, "dkv_dw")[:, :kv_w]
            dx, _, s2k, df, a2, _ = norm_bwd(kv_saved["x"], kv_in_g, zD, dxn, dx, "kvin_bwd", gate=next_gate)
            d_kvin_g = s2k[0]
            e_units, e_sums, e_axes = begin_reduce([n for n in EXCHANGED if n != "pool_w"], 1, "early")
            e_send, e_recv, e_pieces, e_lands, e_token = scatter_start(e_sums, e_axes, "reduce_chips_start")
            early = (e_units, e_send, e_recv, e_lands, e_pieces, e_axes)
            mod[l - 1][4] = mod[l - 1][4] + e_token[0, 0]

    small = {"mod_b": jnp.stack(dmods), "norm1_g": jnp.stack(d_norm1), "norm2_g": jnp.stack(d_norm2),
             "kv_in_g": d_kvin_g, "ckv_norm_g": d_ckv_g, "q_norm_g": jnp.stack(d_qnorm),
             "conv_b": jnp.stack(d_conv_b), "final_g": d_final_g[0]}
    extra = {n: jnp.stack([G[(n, i)] for i in range(W[n].shape[0])]) for n in SMALL_SHARDED}
    ssizes = {n: math.prod(W[n].shape) for n in REPLICATED}
    esizes = {n: math.prod(extra[n].shape) for n in SMALL_SHARDED}
    Ts = _padded(sum(ssizes.values()) + sum(esizes.values()), 8 * PACK_COLS)

    def pack_small(d, tail=()):
        return _flat_pad([d[n].reshape(-1) for n in REPLICATED] + [t.reshape(-1) for t in tail],
                         Ts).reshape(Ts // PACK_COLS, PACK_COLS)

    parts = device_gather(pack_small(small, [extra[n] for n in SMALL_SHARDED]), "gather_small")

    l_units, l_sums, l_axes = begin_reduce([n for n in EXCHANGED if W[n].ndim > 2 and W[n].shape[0] == DEPTH] + ["pool_w"],
                                           0, "late")
    l_send, l_recv, l_pieces, l_lands, l_token = scatter_start(l_sums, l_axes, "reduce_chips_late_start", after=[parts])
    parts = parts + l_token[0, 0]

    grads, deltas, new_m, new_v = {}, {}, {}, {}
    outs = adamw_sum(parts, pack_small(W), pack_small(M1), pack_small(V2), "adamw_small")
    off = 0
    for n in REPLICATED:
        for dst, o in zip((grads, deltas, new_m, new_v), outs):
            dst[n] = o.reshape(-1)[off:off + ssizes[n]].reshape(W[n].shape)
        off += ssizes[n]
    for n, axis in SMALL_SHARDED.items():
        g_full = outs[0].reshape(-1)[off:off + esizes[n]].reshape(extra[n].shape)
        off += esizes[n]
        size = W[n].shape[axis]
        grads[n] = lax.dynamic_slice_in_dim(g_full, chip * size, size, axis=axis)
        deltas[n], new_m[n], new_v[n] = adamw(W[n], grads[n], M1[n], V2[n], f"adamw_{n}")

    dm_all = parts.reshape(N_DEV, -1)[:, :DEPTH * E].reshape(N_DEV, DEPTH, E)
    dm_mine = jnp.swapaxes(lax.dynamic_slice_in_dim(dm_all, chip * Es, Es, axis=2), 0, 1)
    grads["mod_w"], deltas["mod_w"], new_m["mod_w"], new_v["mod_w"] = adamw_modw(
        c_all.reshape(N_DEV, D, 1), dm_mine, mod_w, m_mod_w, v_mod_w, "adamw_mod_w")

    def finish_reduce(pieces, axes, got, tag):
        out = []
        for i, (sm, ax, g4) in enumerate(zip(pieces, axes, got)):
            size = sm.shape[ax] // N_CHIPS
            g4 = lax.dynamic_update_index_in_dim(g4, lax.dynamic_slice_in_dim(sm, chip * size, size, axis=ax), chip, 0)
            blk = g4.shape[1:]
            out.append(sum_parts(g4.reshape(N_CHIPS, -1, blk[-1]), f"reduce_chips_add_{tag}{i}").reshape(blk))
        return out

    e_units, e_send, e_recv, e_lands, e_pieces, e_axes = early
    early_got = scatter_wait(e_send, e_recv, e_lands, e_pieces, dx, "reduce_chips_wait")
    reduced = finish_reduce(e_pieces, e_axes, early_got, "early")
    late_got = scatter_wait(l_send, l_recv, l_lands, l_pieces, new_v["mod_w"], "reduce_chips_late_wait")
    reduced += finish_reduce(l_pieces, l_axes, late_got, "late")
    units = e_units + l_units
    slots, out_shapes = [], []
    for n in EXCHANGED:
        mine = [i for i, u in enumerate(units) if u[0] == n]
        out_shapes.append((len(mine), 2) + reduced[mine[0]].shape)
        slots += [(len(out_shapes) - 1, units[i][1]) for i in mine]
    order = [i for n in EXCHANGED for i, u in enumerate(units) if u[0] == n]
    halves = gather_halves([reduced[i] for i in order], slots, out_shapes, "reduce_gather")
    for ti, n in enumerate(EXCHANGED):
        g = halves[ti]
        for i, u in enumerate(units):
            if u[0] == n:
                g = lax.dynamic_update_slice(g, reduced[i][None, None], (u[1], ci) + (0,) * reduced[i].ndim)
        grads[n] = g.reshape(W[n].shape)
        deltas[n], new_m[n], new_v[n] = adamw(W[n], grads[n], M1[n], V2[n], f"adamw_{n}")

    return (loss, dx.reshape(x.shape), *[grads[n] for n in WEIGHT_ORDER], *[deltas[n] for n in WEIGHT_ORDER],
            *[new_m[n] for n in WEIGHT_ORDER], *[new_v[n] for n in WEIGHT_ORDER])
```

```python
import functools
import math

import jax
import jax.numpy as jnp
from jax import lax
from jax.experimental import pallas as pl
from jax.experimental.pallas import tpu as pltpu

F32 = jnp.float32
BF16 = jnp.bfloat16
MESH = pl.DeviceIdType.MESH

DEPTH = 4
N_A = 2
POOL_WINDOWS = (2, 4, 8, 16)
N_GROUPS = 4
N_HEADS = 8
QK_NOPE = 128
QK_ROPE = 64
V_HEAD = 128
QK_HEAD = QK_NOPE + QK_ROPE
HEAD_PAD = 256
KV_RANK = 256
ROPE_THETA = 10000.0
EPS = 1e-6
ADAM_LR = 0.001
ADAM_B1 = 0.9
ADAM_B2 = 0.999
ADAM_EPS = 1e-08
ADAM_WD = 0.01
ADAM_STEP = 10

N_CHIPS = 4
N_DEV = 8
LANES = 128
PACK_COLS = 1024
VMEM_LIMIT = 56 * 1024 * 1024
GLU_TILE = 256
ATT_BWD_K_BLOCK = 512
ATT_BWD_Q_BLOCK = 512
ATT_Q_BLOCK = 1024
ATT_K_BLOCK = 512
ATT_HEADS_PER_STEP = 2


def _cparams(*sem):
    return pltpu.CompilerParams(dimension_semantics=sem if sem else None, vmem_limit_bytes=VMEM_LIMIT)


def _pick(n, target, mult):
    best = None
    d = mult
    while d <= min(n, target):
        if n % d == 0:
            best = d
        d += mult
    return n if best is None else best


def _row(v):
    return v.reshape(1, -1).astype(F32)


_DIMS = {"nn": (((1,), (0,)), ((), ())), "nt": (((1,), (1,)), ((), ())), "tn": (((0,), (0,)), ((), ()))}


def _mm_body(mode, nk, has_bias, has_res):
    def body(*refs):
        a_ref, b_ref = refs[0], refs[1]
        pos = 2
        bias_ref = res_ref = cs_ref = None
        if has_bias:
            bias_ref = refs[pos]
            pos += 1
        if has_res:
            res_ref, cs_ref = refs[pos], refs[pos + 1]
            pos += 2
        o_ref = refs[pos]
        pos += 1
        o2_ref = None
        if has_res:
            o2_ref = refs[pos]
            pos += 1
        acc_ref = refs[pos] if nk > 1 else None
        k = pl.program_id(2)
        part = lax.dot_general(a_ref[...].astype(BF16), b_ref[...].astype(BF16), _DIMS[mode],
                               preferred_element_type=F32)

        def finish(y):
            if has_bias:
                y = y + bias_ref[...]
            o_ref[...] = y.astype(o_ref.dtype)
            if has_res:
                o2_ref[...] = res_ref[...] + cs_ref[...] * y

        if nk == 1:
            finish(part)
            return

        @pl.when(k == 0)
        def _():
            acc_ref[...] = part

        @pl.when((k > 0) & (k < nk - 1))
        def _():
            acc_ref[...] += part

        @pl.when(k == nk - 1)
        def _():
            finish(acc_ref[...] + part)

    return body


def mm(a, b, mode, out_dtype, name, *, tm=1408, tn=1408, tk=1024, bias=None, res=None, colscale=None, layer=None):
    bshape = b.shape if layer is None else b.shape[1:]
    if mode == "nn":
        (M, K), N = a.shape, bshape[1]
    elif mode == "nt":
        (M, K), N = a.shape, bshape[0]
    else:
        (K, M), N = a.shape, bshape[1]
    tm = _pick(M, tm, LANES if mode == "tn" else 8)
    tn = _pick(N, tn, LANES)
    tk = _pick(K, tk, LANES) if mode != "tn" else _pick(K, tk, 8)
    nk = K // tk
    a_spec = {"nn": pl.BlockSpec((tm, tk), lambda i, j, k: (i, k)),
              "nt": pl.BlockSpec((tm, tk), lambda i, j, k: (i, k)),
              "tn": pl.BlockSpec((tk, tm), lambda i, j, k: (k, i))}[mode]
    b_blk, b_map = {"nn": ((tk, tn), lambda i, j, k: (k, j)),
                    "nt": ((tn, tk), lambda i, j, k: (j, k)),
                    "tn": ((tk, tn), lambda i, j, k: (k, j))}[mode]
    if layer is None:
        b_spec = pl.BlockSpec(b_blk, b_map)
    else:
        b_spec = pl.BlockSpec((None,) + b_blk, lambda i, j, k: (layer,) + b_map(i, j, k))
    o_spec = pl.BlockSpec((tm, tn), lambda i, j, k: (i, j))
    v_spec = pl.BlockSpec((1, tn), lambda i, j, k: (0, j))
    in_specs, args = [a_spec, b_spec], [a, b]
    if bias is not None:
        in_specs.append(v_spec)
        args.append(_row(bias))
    out_shape = [jax.ShapeDtypeStruct((M, N), out_dtype)]
    out_specs = [o_spec]
    if res is not None:
        in_specs += [o_spec, v_spec]
        args += [res, _row(colscale)]
        out_shape.append(jax.ShapeDtypeStruct((M, N), F32))
        out_specs.append(o_spec)
    outs = pl.pallas_call(
        _mm_body(mode, nk, bias is not None, res is not None),
        grid=(M // tm, N // tn, nk),
        in_specs=in_specs, out_specs=out_specs, out_shape=out_shape,
        scratch_shapes=[pltpu.VMEM((tm, tn), F32)] if nk > 1 else [],
        compiler_params=_cparams("parallel", "parallel", "arbitrary"),
        name=name,
    )(*args)
    return outs if res is not None else outs[0]


def gmm(a, w, mode, out_dtype, name, *, bias=None, res=None, colscale=None, tr=2048):
    S_ = a.shape[0]
    G = N_GROUPS
    C = a.shape[1] // G
    tr = _pick(S_, tr, 8)
    nr = S_ // tr
    if mode == "tn":
        def body(a_ref, b_ref, o_ref, acc_ref):
            i = pl.program_id(1)

            @pl.when(i == 0)
            def _():
                acc_ref[...] = jnp.zeros_like(acc_ref)

            acc_ref[...] += lax.dot_general(a_ref[...].astype(BF16), b_ref[...].astype(BF16), _DIMS["tn"],
                                            preferred_element_type=F32)

            @pl.when(i == nr - 1)
            def _():
                o_ref[...] = acc_ref[...].astype(o_ref.dtype)

        blk = pl.BlockSpec((tr, C), lambda g, i: (i, g))
        return pl.pallas_call(
            body, grid=(G, nr), in_specs=[blk, blk],
            out_specs=pl.BlockSpec((None, C, C), lambda g, i: (g, 0, 0)),
            out_shape=jax.ShapeDtypeStruct((G, C, C), out_dtype),
            scratch_shapes=[pltpu.VMEM((C, C), F32)],
            compiler_params=_cparams("parallel", "arbitrary"), name=name,
        )(a, w)

    has_bias, has_res = bias is not None, res is not None

    def body(*refs):
        a_ref, w_ref = refs[0], refs[1]
        pos = 2
        if has_bias:
            bias_ref = refs[pos]
            pos += 1
        if has_res:
            res_ref, cs_ref = refs[pos], refs[pos + 1]
            pos += 2
        o_ref = refs[pos]
        y = lax.dot_general(a_ref[...].astype(BF16), w_ref[...].astype(BF16), _DIMS[mode],
                            preferred_element_type=F32)
        if has_bias:
            y = y + bias_ref[...]
        o_ref[...] = y.astype(o_ref.dtype)
        if has_res:
            refs[pos + 1][...] = res_ref[...] + cs_ref[...] * y

    blk = pl.BlockSpec((tr, C), lambda i, g: (i, g))
    vec = pl.BlockSpec((1, C), lambda i, g: (0, g))
    in_specs = [blk, pl.BlockSpec((None, C, C), lambda i, g: (g, 0, 0))]
    args = [a, w]
    if has_bias:
        in_specs.append(vec)
        args.append(_row(bias))
    out_shape = [jax.ShapeDtypeStruct(a.shape, out_dtype)]
    out_specs = [blk]
    if has_res:
        in_specs += [blk, vec]
        args += [res, _row(colscale)]
        out_shape.append(jax.ShapeDtypeStruct(a.shape, F32))
        out_specs.append(blk)
    outs = pl.pallas_call(
        body, grid=(nr, G), in_specs=in_specs, out_specs=out_specs, out_shape=out_shape,
        compiler_params=_cparams("parallel", "parallel"), name=name,
    )(*args)
    return outs if has_res else outs[0]


def norm_fwd(x, g, sc, sh, out_dtype, name, tr=1024):
    S_, Dn = x.shape
    tr = _pick(S_, tr, 8)

    def body(x_ref, g_ref, sc_ref, sh_ref, o_ref):
        xv = x_ref[...]
        r = lax.rsqrt(jnp.mean(xv * xv, axis=-1, keepdims=True) + EPS)
        o_ref[...] = (((xv * r) * g_ref[...]) * (1.0 + sc_ref[...]) + sh_ref[...]).astype(o_ref.dtype)

    blk = pl.BlockSpec((tr, Dn), lambda i: (i, 0))
    vec = pl.BlockSpec((1, Dn), lambda i: (0, 0))
    return pl.pallas_call(
        body, grid=(S_ // tr,), in_specs=[blk, vec, vec, vec], out_specs=blk,
        out_shape=jax.ShapeDtypeStruct((S_, Dn), out_dtype),
        compiler_params=_cparams("parallel"), name=name,
    )(x, _row(g), _row(sc), _row(sh))


def norm_bwd(x, g, sc, dh, dres, name, gate=None, tr=1024):
    S_, Dn = x.shape
    tr = _pick(S_, tr, 8)
    has_res = dres is not None
    has_gate = gate is not None

    def body(*refs):
        x_ref, g_ref, sc_ref, dh_ref = refs[:4]
        pos = 4
        if has_res:
            dres_ref = refs[pos]
            pos += 1
        if has_gate:
            y_ref, cs_ref = refs[pos:pos + 2]
            pos += 2
        dx_ref, s1_ref, s2_ref = refs[pos:pos + 3]
        if has_gate:
            d_ref, a_ref, c_ref = refs[pos + 3:pos + 6]
        i = pl.program_id(0)

        @pl.when(i == 0)
        def _():
            s1_ref[...] = jnp.zeros_like(s1_ref)
            s2_ref[...] = jnp.zeros_like(s2_ref)
            if has_gate:
                a_ref[...] = jnp.zeros_like(a_ref)
                c_ref[...] = jnp.zeros_like(c_ref)

        xv = x_ref[...]
        r = lax.rsqrt(jnp.mean(xv * xv, axis=-1, keepdims=True) + EPS)
        n = xv * r
        dhv = dh_ref[...].astype(F32)
        dn = dhv * (g_ref[...] * (1.0 + sc_ref[...]))
        dx = r * (dn - n * jnp.mean(dn * n, axis=-1, keepdims=True))
        if has_res:
            dx = dx + dres_ref[...]
        dx_ref[...] = dx
        s1_ref[...] += jnp.sum(dhv, axis=0, keepdims=True)
        s2_ref[...] += jnp.sum(dhv * n, axis=0, keepdims=True)
        if has_gate:
            d_ref[...] = (dx * cs_ref[...]).astype(d_ref.dtype)
            a_ref[...] += jnp.sum(dx * y_ref[...].astype(F32), axis=0, keepdims=True)
            c_ref[...] += jnp.sum(dx, axis=0, keepdims=True)

    blk = pl.BlockSpec((tr, Dn), lambda i: (i, 0))
    vec = pl.BlockSpec((1, Dn), lambda i: (0, 0))
    in_specs, args = [blk, vec, vec, blk], [x, _row(g), _row(sc), dh]
    if has_res:
        in_specs.append(blk)
        args.append(dres)
    vshape = jax.ShapeDtypeStruct((1, Dn), F32)
    out_specs = [blk, vec, vec]
    out_shape = [jax.ShapeDtypeStruct((S_, Dn), F32), vshape, vshape]
    if has_gate:
        in_specs += [blk, vec]
        args += [gate[0], _row(gate[1])]
        out_specs += [blk, vec, vec]
        out_shape += [jax.ShapeDtypeStruct((S_, Dn), BF16), vshape, vshape]
    return pl.pallas_call(
        body, grid=(S_ // tr,), in_specs=in_specs, out_specs=out_specs, out_shape=out_shape,
        compiler_params=_cparams("arbitrary"), name=name,
    )(*args)


def gate_bwd(dx, y, colscale, name, tr=512):
    S_, Dn = dx.shape
    tr = _pick(S_, tr, 8)

    def body(dx_ref, y_ref, cs_ref, d_ref, a_ref, c_ref):
        i = pl.program_id(0)

        @pl.when(i == 0)
        def _():
            a_ref[...] = jnp.zeros_like(a_ref)
            c_ref[...] = jnp.zeros_like(c_ref)

        dxv = dx_ref[...]
        d_ref[...] = (dxv * cs_ref[...]).astype(d_ref.dtype)
        a_ref[...] += jnp.sum(dxv * y_ref[...].astype(F32), axis=0, keepdims=True)
        c_ref[...] += jnp.sum(dxv, axis=0, keepdims=True)

    blk = pl.BlockSpec((tr, Dn), lambda i: (i, 0))
    vec = pl.BlockSpec((1, Dn), lambda i: (0, 0))
    vshape = jax.ShapeDtypeStruct((1, Dn), F32)
    return pl.pallas_call(
        body, grid=(S_ // tr,), in_specs=[blk, blk, vec], out_specs=[blk, vec, vec],
        out_shape=[jax.ShapeDtypeStruct((S_, Dn), BF16), vshape, vshape],
        compiler_params=_cparams("arbitrary"), name=name,
    )(dx, y, _row(colscale))


def loss_head(x, g, target, name, tr=512):
    S_, Dn = x.shape
    tr = _pick(S_, tr, 8)

    def body(x_ref, g_ref, t_ref, dx_ref, dg_ref, loss_ref):
        i = pl.program_id(0)

        @pl.when(i == 0)
        def _():
            dg_ref[...] = jnp.zeros_like(dg_ref)
            loss_ref[...] = jnp.zeros_like(loss_ref)

        xv = x_ref[...]
        r = lax.rsqrt(jnp.mean(xv * xv, axis=-1, keepdims=True) + EPS)
        n = xv * r
        e = n * g_ref[...] - t_ref[...]
        loss_ref[...] += 0.5 * jnp.sum(jnp.mean(e * e, axis=-1, keepdims=True), axis=0, keepdims=True)
        dy = e * (1.0 / Dn)
        dg_ref[...] += jnp.sum(dy * n, axis=0, keepdims=True)
        dn = dy * g_ref[...]
        dx_ref[...] = r * (dn - n * jnp.mean(dn * n, axis=-1, keepdims=True))

    blk = pl.BlockSpec((tr, Dn), lambda i: (i, 0))
    vec = pl.BlockSpec((1, Dn), lambda i: (0, 0))
    one = pl.BlockSpec((1, 1), lambda i: (0, 0))
    return pl.pallas_call(
        body, grid=(S_ // tr,), in_specs=[blk, vec, blk], out_specs=[blk, vec, one],
        out_shape=[jax.ShapeDtypeStruct((S_, Dn), F32), jax.ShapeDtypeStruct((1, Dn), F32),
                   jax.ShapeDtypeStruct((1, 1), F32)],
        compiler_params=_cparams("arbitrary"), name=name,
    )(x, _row(g), target)


POOL_HALO = 16
POOL_CHUNK = 512


def _rows(ref, lo, hi, n_rows):
    parts = []
    if lo < 0:
        parts.append(jnp.zeros((-lo, ref.shape[1]), F32))
    parts.append(ref[max(lo, 0):min(hi, n_rows), :].astype(F32))
    if hi > n_rows:
        parts.append(jnp.zeros((hi - n_rows, ref.shape[1]), F32))
    return parts[0] if len(parts) == 1 else jnp.concatenate(parts, axis=0)


def _window_sum(e, w, back):
    n = e.shape[0]
    s, width = e, 1
    while width < w:
        s = s + pltpu.roll(s, width if back else n - width, 0)
        width *= 2
    return s


def _pool_call(h, out_dtype, name, backward):
    S_, Dn = h.shape
    C = Dn // N_GROUPS
    ch = _pick(S_, POOL_CHUNK, 8)

    def body(h_ref, o_ref):
        g = pl.program_id(0)
        for gi, w in enumerate(POOL_WINDOWS):
            @pl.when(g == gi)
            def _(w=w):
                for r0 in range(0, S_, ch):
                    t = (r0 + lax.broadcasted_iota(jnp.int32, (ch, C), 0)).astype(F32)
                    cnt = jnp.minimum(t + 1.0, float(w))
                    if not backward:
                        ext = _rows(h_ref, r0 - POOL_HALO, r0 + ch, S_)
                        cur = ext[POOL_HALO:]
                        mean = _window_sum(ext, w, True)[POOL_HALO:] / cnt
                        o_ref[r0:r0 + ch, :] = (mean - cur).astype(o_ref.dtype)
                    else:
                        ext = _rows(h_ref, r0, r0 + ch + POOL_HALO, S_)
                        text = (r0 + lax.broadcasted_iota(jnp.int32, (ch + POOL_HALO, C), 0)).astype(F32)
                        e = ext / jnp.minimum(text + 1.0, float(w))
                        o_ref[r0:r0 + ch, :] = (_window_sum(e, w, False)[:ch] - ext[:ch]).astype(o_ref.dtype)

    blk = pl.BlockSpec((S_, C), lambda g: (0, g))
    return pl.pallas_call(
        body, grid=(N_GROUPS,), in_specs=[blk], out_specs=blk,
        out_shape=jax.ShapeDtypeStruct((S_, Dn), out_dtype),
        compiler_params=_cparams("parallel"), name=name,
    )(h)


GLU_CHUNK = 512
GLU_HALO = 16
_SQRT_HALF = 0.7071067811865476
_INV_SQRT_2PI = 0.3989422804014327


def _gelu(a):
    return 0.5 * a * (1.0 + lax.erf(a * _SQRT_HALF))


def _gelu_grad(a):
    return 0.5 * (1.0 + lax.erf(a * _SQRT_HALF)) + a * (_INV_SQRT_2PI * jnp.exp(-0.5 * a * a))


def glu_fwd(u, conv_w, conv_b, name):
    S_, F2 = u.shape
    Fh = F2 // 2
    tf = GLU_TILE
    nt = Fh // tf
    ch = _pick(S_, GLU_CHUNK, GLU_HALO)

    def body(a_ref, v_ref, cw_ref, cb_ref, z_ref):
        cw0, cw1, cw2 = cw_ref[0:1, :], cw_ref[1:2, :], cw_ref[2:3, :]
        cb = cb_ref[...]
        for r0 in range(0, S_, ch):
            ext = _rows(a_ref, r0 - GLU_HALO, r0 + ch, S_)
            a0 = ext[GLU_HALO:]
            a1 = pltpu.roll(ext, 1, 0)[GLU_HALO:]
            a2 = pltpu.roll(ext, 2, 0)[GLU_HALO:]
            ac = a2 * cw0 + a1 * cw1 + a0 * cw2 + cb
            z_ref[r0:r0 + ch, :] = (_gelu(ac) * v_ref[r0:r0 + ch, :].astype(F32)).astype(z_ref.dtype)

    return pl.pallas_call(
        body, grid=(nt,),
        in_specs=[pl.BlockSpec((S_, tf), lambda j: (0, j)), pl.BlockSpec((S_, tf), lambda j: (0, j + nt)),
                  pl.BlockSpec((3, tf), lambda j: (0, j)), pl.BlockSpec((1, tf), lambda j: (0, j))],
        out_specs=pl.BlockSpec((S_, tf), lambda j: (0, j)),
        out_shape=jax.ShapeDtypeStruct((S_, Fh), BF16),
        compiler_params=_cparams("parallel"), name=name,
    )(u, u, conv_w, _row(conv_b))


def glu_bwd(u, dz, conv_w, conv_b, name):
    S_, F2 = u.shape
    Fh = F2 // 2
    tf = GLU_TILE
    nt = Fh // tf
    ch = _pick(S_, GLU_CHUNK, GLU_HALO)

    def body(a_ref, v_ref, dz_ref, cw_ref, cb_ref, du_ref, dcw_ref, dcb_ref, da_buf, dv_buf, sems):
        j = pl.program_id(0)
        slot = j % 2

        def writes(step, sl):
            lo = pl.multiple_of(step * tf, tf)
            return (pltpu.make_async_copy(da_buf.at[sl], du_ref.at[:, pl.ds(lo, tf)], sems.at[sl, 0]),
                    pltpu.make_async_copy(dv_buf.at[sl], du_ref.at[:, pl.ds(Fh + lo, tf)], sems.at[sl, 1]))

        @pl.when(j >= 2)
        def _():
            for cp in writes(j - 2, slot):
                cp.wait()

        cw0, cw1, cw2 = cw_ref[0:1, :], cw_ref[1:2, :], cw_ref[2:3, :]
        cb = cb_ref[...]
        acc = [jnp.zeros((1, tf), F32) for _ in range(4)]
        n = ch + GLU_HALO
        for r0 in range(0, S_, ch):
            ext = _rows(a_ref, r0 - GLU_HALO, r0 + n, S_)
            a0 = ext[GLU_HALO:]
            a1 = pltpu.roll(ext, 1, 0)[GLU_HALO:]
            a2 = pltpu.roll(ext, 2, 0)[GLU_HALO:]
            ac = a2 * cw0 + a1 * cw1 + a0 * cw2 + cb
            vv = _rows(v_ref, r0, r0 + n, S_)
            dzv = _rows(dz_ref, r0, r0 + n, S_)
            gl = _gelu(ac)
            dac = dzv * vv * _gelu_grad(ac)
            da = (dac * cw2 + pltpu.roll(dac, n - 1, 0) * cw1 + pltpu.roll(dac, n - 2, 0) * cw0)[:ch]
            da_buf[slot, r0:r0 + ch, :] = da.astype(da_buf.dtype)
            dv_buf[slot, r0:r0 + ch, :] = (dzv[:ch] * gl[:ch]).astype(dv_buf.dtype)
            dc = dac[:ch]
            acc[0] = acc[0] + jnp.sum(dc * a2[:ch], axis=0, keepdims=True)
            acc[1] = acc[1] + jnp.sum(dc * a1[:ch], axis=0, keepdims=True)
            acc[2] = acc[2] + jnp.sum(dc * a0[:ch], axis=0, keepdims=True)
            acc[3] = acc[3] + jnp.sum(dc, axis=0, keepdims=True)
        dcw_ref[0:1, :] = acc[0]
        dcw_ref[1:2, :] = acc[1]
        dcw_ref[2:3, :] = acc[2]
        dcb_ref[...] = acc[3]
        for cp in writes(j, slot):
            cp.start()

        @pl.when(j == nt - 1)
        def _():
            for cp in writes(j, slot):
                cp.wait()
            if nt > 1:
                for cp in writes(j - 1, 1 - slot):
                    cp.wait()

    return pl.pallas_call(
        body, grid=(nt,),
        in_specs=[pl.BlockSpec((S_, tf), lambda j: (0, j)), pl.BlockSpec((S_, tf), lambda j: (0, j + nt)),
                  pl.BlockSpec((S_, tf), lambda j: (0, j)),
                  pl.BlockSpec((3, tf), lambda j: (0, j)), pl.BlockSpec((1, tf), lambda j: (0, j))],
        out_specs=[_ANY, pl.BlockSpec((3, tf), lambda j: (0, j)), pl.BlockSpec((1, tf), lambda j: (0, j))],
        out_shape=[jax.ShapeDtypeStruct((S_, F2), BF16), jax.ShapeDtypeStruct((3, Fh), F32),
                   jax.ShapeDtypeStruct((1, Fh), F32)],
        scratch_shapes=[pltpu.VMEM((2, S_, tf), BF16), pltpu.VMEM((2, S_, tf), BF16), pltpu.SemaphoreType.DMA((2, 2))],
        compiler_params=_cparams("arbitrary"), name=name,
    )(u, u, dz, conv_w, _row(conv_b))


def rope_tables(pos, inv, name, tr=512):
    S_ = pos.shape[0]
    tr = _pick(S_, tr, 8)

    def body(p_ref, inv_ref, c_ref, s1_ref, s2_ref):
        ang = p_ref[...] * inv_ref[...]
        lane = lax.broadcasted_iota(jnp.int32, ang.shape, 1)
        half = QK_ROPE // 2
        cosv, sinv = jnp.cos(ang), jnp.sin(ang)
        c_ref[...] = jnp.where(lane < QK_ROPE, cosv, 0.0)
        s1_ref[...] = jnp.where(lane < half, -sinv, 0.0)
        s2_ref[...] = jnp.where((lane >= half) & (lane < QK_ROPE), sinv, 0.0)

    blk = pl.BlockSpec((tr, LANES), lambda i: (i, 0))
    shp = jax.ShapeDtypeStruct((S_, LANES), F32)
    return pl.pallas_call(
        body, grid=(S_ // tr,),
        in_specs=[pl.BlockSpec((tr, 1), lambda i: (i, 0)), pl.BlockSpec((1, LANES), lambda i: (0, 0))],
        out_specs=[blk, blk, blk], out_shape=[shp, shp, shp],
        compiler_params=_cparams("parallel"), name=name,
    )(pos, inv)


_HALF = QK_ROPE // 2


def _rope(t, c, s1, s2):
    return t * c + pltpu.roll(t, LANES - _HALF, 1) * s1 + pltpu.roll(t, _HALF, 1) * s2


def _rope_t(d, c, s1, s2):
    return d * c + pltpu.roll(d * s1, _HALF, 1) + pltpu.roll(d * s2, LANES - _HALF, 1)


def q_prep(q, tabs, scale, backward, name, tr=512):
    S_, W = q.shape
    tr = _pick(S_, tr, 8)

    def body(q_ref, c_ref, s1_ref, s2_ref, o_ref):
        o_ref[:, 0:LANES] = (q_ref[:, 0:LANES].astype(F32) * scale).astype(o_ref.dtype)
        t = q_ref[:, LANES:2 * LANES].astype(F32)
        fn = _rope_t if backward else _rope
        o_ref[:, LANES:2 * LANES] = (fn(t, c_ref[...], s1_ref[...], s2_ref[...]) * scale).astype(o_ref.dtype)

    blk = pl.BlockSpec((tr, HEAD_PAD), lambda i, h: (i, h))
    tab = pl.BlockSpec((tr, LANES), lambda i, h: (i, 0))
    return pl.pallas_call(
        body, grid=(S_ // tr, W // HEAD_PAD), in_specs=[blk, tab, tab, tab], out_specs=blk,
        out_shape=jax.ShapeDtypeStruct((S_, W), BF16),
        compiler_params=_cparams("parallel", "parallel"), name=name,
    )(q, *tabs)


def k_prep(knv, kv_ext, tabs, name, tr=512):
    S_ = knv.shape[0]
    tr = _pick(S_, tr, 8)

    def body(kn_ref, v_ref, t_ref, c_ref, s1_ref, s2_ref, o_ref, vx_ref):
        o_ref[:, 0:LANES] = kn_ref[...].astype(o_ref.dtype)
        o_ref[:, LANES:2 * LANES] = _rope(t_ref[...], c_ref[...], s1_ref[...], s2_ref[...]).astype(o_ref.dtype)
        vx_ref[:, 0:V_HEAD] = v_ref[...].astype(vx_ref.dtype)
        vx_ref[:, V_HEAD:HEAD_PAD] = jnp.ones((tr, HEAD_PAD - V_HEAD), vx_ref.dtype)

    tab = pl.BlockSpec((tr, LANES), lambda i, h: (i, 0))
    head = pl.BlockSpec((tr, HEAD_PAD), lambda i, h: (i, h))
    shp = jax.ShapeDtypeStruct((S_, N_HEADS * HEAD_PAD), BF16)
    return pl.pallas_call(
        body, grid=(S_ // tr, N_HEADS),
        in_specs=[pl.BlockSpec((tr, LANES), lambda i, h: (i, h)),
                  pl.BlockSpec((tr, V_HEAD), lambda i, h: (i, N_HEADS + h)),
                  pl.BlockSpec((tr, LANES), lambda i, h: (i, KV_RANK // LANES)), tab, tab, tab],
        out_specs=[head, head], out_shape=[shp, shp],
        compiler_params=_cparams("parallel", "parallel"), name=name,
    )(knv, knv, kv_ext, *tabs)


def k_prep_bwd(dk_a, dk_b, dv_a, dv_b, tabs, name, tr=512):
    S_ = dk_a.shape[0]
    tr = _pick(S_, tr, 8)
    HV = N_HEADS * V_HEAD

    def body(ka_ref, kb_ref, va_ref, vb_ref, c_ref, s1_ref, s2_ref, o_ref, t_ref):
        dr = jnp.zeros((tr, LANES), F32)
        for h in range(N_HEADS):
            lo = h * HEAD_PAD
            o_ref[:, h * LANES:(h + 1) * LANES] = (ka_ref[:, lo:lo + LANES] + kb_ref[:, lo:lo + LANES]).astype(o_ref.dtype)
            dr = dr + ka_ref[:, lo + LANES:lo + 2 * LANES] + kb_ref[:, lo + LANES:lo + 2 * LANES]
        o_ref[:, HV:2 * HV] = (va_ref[...] + vb_ref[...]).astype(o_ref.dtype)
        t_ref[...] = _rope_t(dr, c_ref[...], s1_ref[...], s2_ref[...])

    kblk = pl.BlockSpec((tr, N_HEADS * HEAD_PAD), lambda i: (i, 0))
    vblk = pl.BlockSpec((tr, HV), lambda i: (i, 0))
    tab = pl.BlockSpec((tr, LANES), lambda i: (i, 0))
    return pl.pallas_call(
        body, grid=(S_ // tr,), in_specs=[kblk, kblk, vblk, vblk, tab, tab, tab],
        out_specs=[pl.BlockSpec((tr, 2 * HV), lambda i: (i, 0)), tab],
        out_shape=[jax.ShapeDtypeStruct((S_, 2 * HV), BF16), jax.ShapeDtypeStruct((S_, LANES), F32)],
        compiler_params=_cparams("parallel"), name=name,
    )(dk_a, dk_b, dv_a, dv_b, *tabs)


_NEG = -1e30


def attn_fwd(q, k, vx, name):
    S_ = q.shape[0]
    TQ = _pick(S_, ATT_Q_BLOCK, 8)
    TK = _pick(S_, ATT_K_BLOCK, 8)
    assert TQ % TK == 0 or TK % TQ == 0
    HP = ATT_HEADS_PER_STEP
    W = HP * HEAD_PAD

    def body(q_ref, k_ref, v_ref, o_ref, lse_ref):
        i = pl.program_id(1)
        qs = [q_ref[:, h * HEAD_PAD:(h + 1) * HEAD_PAD] for h in range(HP)]

        def step(j, carry, masked):
            start = pl.multiple_of(j * TK, TK)
            out = []
            for h in range(HP):
                m, acc = carry[h]
                cols = slice(h * HEAD_PAD, (h + 1) * HEAD_PAD)
                s = lax.dot_general(qs[h], k_ref[pl.ds(start, TK), cols], _DIMS["nt"], preferred_element_type=F32)
                if masked:
                    rowi = i * TQ + lax.broadcasted_iota(jnp.int32, (TQ, TK), 0)
                    coli = j * TK + lax.broadcasted_iota(jnp.int32, (TQ, TK), 1)
                    s = jnp.where(coli <= rowi, s, _NEG)
                m_new = jnp.maximum(m, jnp.max(s, axis=-1, keepdims=True))
                alpha = jnp.exp(m - m_new)
                p = jnp.exp(s - m_new).astype(BF16)
                acc = alpha * acc + lax.dot_general(p, v_ref[pl.ds(start, TK), cols], _DIMS["nn"],
                                                    preferred_element_type=F32)
                out.append((m_new, acc))
            return tuple(out)

        init = tuple((jnp.full((TQ, 1), _NEG, F32), jnp.zeros((TQ, HEAD_PAD), F32)) for _ in range(HP))
        n_full, n_diag = (i * (TQ // TK), TQ // TK) if TQ >= TK else (i // (TK // TQ), 1)
        carry = lax.fori_loop(0, n_full, functools.partial(step, masked=False), init)
        for d in range(n_diag):
            carry = step(n_full + d, carry, True)
        for h in range(HP):
            m, acc = carry[h]
            l = acc[:, V_HEAD:]
            o_ref[:, h * V_HEAD:(h + 1) * V_HEAD] = (acc[:, :V_HEAD] / l).astype(o_ref.dtype)
            lse_ref[h] = m + jnp.log(jnp.max(l, axis=-1, keepdims=True))

    return pl.pallas_call(
        body, grid=(N_HEADS // HP, S_ // TQ),
        in_specs=[pl.BlockSpec((TQ, W), lambda g, i: (i, g)),
                  pl.BlockSpec((S_, W), lambda g, i: (0, g)),
                  pl.BlockSpec((S_, W), lambda g, i: (0, g))],
        out_specs=[pl.BlockSpec((TQ, HP * V_HEAD), lambda g, i: (i, g)),
                   pl.BlockSpec((HP, TQ, 1), lambda g, i: (g, i, 0))],
        out_shape=[jax.ShapeDtypeStruct((S_, N_HEADS * V_HEAD), BF16), jax.ShapeDtypeStruct((N_HEADS, S_, 1), F32)],
        compiler_params=_cparams("parallel", "parallel"), name=name,
    )(q, k, vx)


def q_proj(cq, w_ext, tabs, scale, name, tm=4096):
    S_, R = cq.shape
    tm = _pick(S_, tm, 16)

    def body(c_ref, w_ref, t_c, t_s1, t_s2, o_ref):
        y = lax.dot_general(c_ref[...].astype(BF16), w_ref[...].astype(BF16), _DIMS["nn"], preferred_element_type=F32)
        o_ref[:, 0:LANES] = (y[:, 0:LANES] * scale).astype(o_ref.dtype)
        o_ref[:, LANES:2 * LANES] = (_rope(y[:, LANES:2 * LANES], t_c[...], t_s1[...], t_s2[...]) * scale).astype(o_ref.dtype)

    tab = pl.BlockSpec((tm, LANES), lambda i, h: (i, 0))
    return pl.pallas_call(
        body, grid=(S_ // tm, N_HEADS),
        in_specs=[pl.BlockSpec((tm, R), lambda i, h: (i, 0)), pl.BlockSpec((R, HEAD_PAD), lambda i, h: (0, h)),
                  tab, tab, tab],
        out_specs=pl.BlockSpec((tm, HEAD_PAD), lambda i, h: (i, h)),
        out_shape=jax.ShapeDtypeStruct((S_, N_HEADS * HEAD_PAD), BF16),
        compiler_params=_cparams("parallel", "parallel"), name=name,
    )(cq, w_ext, *tabs)


def q_proj_bwd(dq, cq, w_ext, tabs, scale, name, tm=4096):
    S_, R = cq.shape
    tm = _pick(S_, tm, 16)
    nr = S_ // tm

    def body(dq_ref, c_ref, w_ref, t_c, t_s1, t_s2, dc_ref, dw_ref, acc_ref):
        h, i = pl.program_id(0), pl.program_id(1)
        g = jnp.concatenate([dq_ref[:, 0:LANES] * scale,
                             _rope_t(dq_ref[:, LANES:2 * LANES], t_c[...], t_s1[...], t_s2[...]) * scale],
                            axis=1).astype(BF16)
        part = lax.dot_general(g, w_ref[...].astype(BF16), _DIMS["nt"], preferred_element_type=F32)
        rows = pl.ds(pl.multiple_of(i * tm, tm), tm)

        @pl.when(h == 0)
        def _():
            dc_ref[rows, :] = part

        @pl.when(h > 0)
        def _():
            dc_ref[rows, :] += part

        dwp = lax.dot_general(c_ref[...].astype(BF16), g, _DIMS["tn"], preferred_element_type=F32)

        @pl.when(i == 0)
        def _():
            acc_ref[...] = dwp

        @pl.when(i > 0)
        def _():
            acc_ref[...] += dwp

        @pl.when(i == nr - 1)
        def _():
            dw_ref[...] = acc_ref[...].astype(dw_ref.dtype)

    tab = pl.BlockSpec((tm, LANES), lambda h, i: (i, 0))
    return pl.pallas_call(
        body, grid=(N_HEADS, nr),
        in_specs=[pl.BlockSpec((tm, HEAD_PAD), lambda h, i: (i, h)), pl.BlockSpec((tm, R), lambda h, i: (i, 0)),
                  pl.BlockSpec((R, HEAD_PAD), lambda h, i: (0, h)), tab, tab, tab],
        out_specs=[pl.BlockSpec((S_, R), lambda h, i: (0, 0)), pl.BlockSpec((R, HEAD_PAD), lambda h, i: (0, h))],
        out_shape=[jax.ShapeDtypeStruct((S_, R), F32), jax.ShapeDtypeStruct((R, N_HEADS * HEAD_PAD), BF16)],
        scratch_shapes=[pltpu.VMEM((R, HEAD_PAD), F32)],
        compiler_params=_cparams("arbitrary", "arbitrary"), name=name,
    )(dq, cq, w_ext, *tabs)


def kv_proj(ckv, w_ukv, kv_ext, tabs, name, tm=2048):
    S_, R = ckv.shape
    tm = _pick(S_, tm, 16)

    def body(c_ref, wk_ref, wv_ref, t_ref, t_c, t_s1, t_s2, k_ref, vx_ref):
        cv = c_ref[...].astype(BF16)
        k_ref[:, 0:LANES] = lax.dot_general(cv, wk_ref[...].astype(BF16), _DIMS["nn"],
                                            preferred_element_type=F32).astype(k_ref.dtype)
        k_ref[:, LANES:2 * LANES] = _rope(t_ref[...], t_c[...], t_s1[...], t_s2[...]).astype(k_ref.dtype)
        vx_ref[:, 0:V_HEAD] = lax.dot_general(cv, wv_ref[...].astype(BF16), _DIMS["nn"],
                                              preferred_element_type=F32).astype(vx_ref.dtype)
        vx_ref[:, V_HEAD:HEAD_PAD] = jnp.ones((tm, HEAD_PAD - V_HEAD), vx_ref.dtype)

    tab = pl.BlockSpec((tm, LANES), lambda i, h: (i, 0))
    head = pl.BlockSpec((tm, HEAD_PAD), lambda i, h: (i, h))
    shp = jax.ShapeDtypeStruct((S_, N_HEADS * HEAD_PAD), BF16)
    return pl.pallas_call(
        body, grid=(S_ // tm, N_HEADS),
        in_specs=[pl.BlockSpec((tm, R), lambda i, h: (i, 0)), pl.BlockSpec((R, QK_NOPE), lambda i, h: (0, h)),
                  pl.BlockSpec((R, V_HEAD), lambda i, h: (0, N_HEADS + h)),
                  pl.BlockSpec((tm, LANES), lambda i, h: (i, KV_RANK // LANES)), tab, tab, tab],
        out_specs=[head, head], out_shape=[shp, shp],
        compiler_params=_cparams("parallel", "parallel"), name=name,
    )(ckv, w_ukv, w_ukv, kv_ext, *tabs)


def o_proj_bwd(dy, w_o, o, name, tm=2048):
    S_, Dn = dy.shape
    HV = w_o.shape[0]
    tm = _pick(S_, tm, 16)

    def body(dy_ref, w_ref, o_ref, do_ref, d_ref):
        do = lax.dot_general(dy_ref[...].astype(BF16), w_ref[...].astype(BF16), _DIMS["nt"], preferred_element_type=F32)
        do_ref[...] = do.astype(do_ref.dtype)
        prod = do * o_ref[...].astype(F32)
        for h in range(N_HEADS):
            d_ref[h] = jnp.sum(prod[:, h * V_HEAD:(h + 1) * V_HEAD], axis=-1, keepdims=True)

    return pl.pallas_call(
        body, grid=(S_ // tm,),
        in_specs=[pl.BlockSpec((tm, Dn), lambda i: (i, 0)), pl.BlockSpec((HV, Dn), lambda i: (0, 0)),
                  pl.BlockSpec((tm, HV), lambda i: (i, 0))],
        out_specs=[pl.BlockSpec((tm, HV), lambda i: (i, 0)), pl.BlockSpec((N_HEADS, tm, 1), lambda i: (0, i, 0))],
        out_shape=[jax.ShapeDtypeStruct((S_, HV), BF16), jax.ShapeDtypeStruct((N_HEADS, S_, 1), F32)],
        compiler_params=_cparams("parallel"), name=name,
    )(dy, w_o, o)


def attn_delta(o, do, name, tr=512):
    S_ = o.shape[0]
    tr = _pick(S_, tr, 8)

    def body(o_ref, do_ref, d_ref):
        d_ref[...] = jnp.sum(o_ref[...].astype(F32) * do_ref[...].astype(F32), axis=-1, keepdims=True)

    blk = pl.BlockSpec((tr, V_HEAD), lambda i, h: (i, h))
    return pl.pallas_call(
        body, grid=(S_ // tr, N_HEADS), in_specs=[blk, blk],
        out_specs=pl.BlockSpec((None, tr, 1), lambda i, h: (h, i, 0)),
        out_shape=jax.ShapeDtypeStruct((N_HEADS, S_, 1), F32),
        compiler_params=_cparams("parallel", "parallel"), name=name,
    )(o, do)


def attn_bwd(q, k, vx, do, lse_row, delta_row, name):
    S_ = q.shape[0]
    TK = _pick(S_, ATT_BWD_K_BLOCK, LANES)
    TQ = _pick(S_, ATT_BWD_Q_BLOCK, TK)
    HP = ATT_HEADS_PER_STEP
    W = HP * HEAD_PAD
    ratio = TQ // TK
    nq = S_ // TQ

    def body(q_ref, do_ref, lse_ref, dl_ref, k_ref, v_ref, dq_ref, dk_ref, dv_ref):
        j = pl.program_id(1)

        @pl.when(j == 0)
        def _():
            dq_ref[...] = jnp.zeros_like(dq_ref)

        ks = [k_ref[:, h * HEAD_PAD:(h + 1) * HEAD_PAD] for h in range(HP)]
        vs = [v_ref[:, h * HEAD_PAD:h * HEAD_PAD + V_HEAD] for h in range(HP)]

        def step(i, carry, masked):
            start = pl.multiple_of(i * TQ, TQ)
            out = []
            for h in range(HP):
                dk, dv = carry[h]
                cols = slice(h * HEAD_PAD, (h + 1) * HEAD_PAD)
                qv = q_ref[pl.ds(start, TQ), cols]
                dov = do_ref[pl.ds(start, TQ), h * V_HEAD:(h + 1) * V_HEAD]
                st = lax.dot_general(ks[h], qv, _DIMS["nt"], preferred_element_type=F32)
                pt = jnp.exp(st - lse_ref[h, :, pl.ds(start, TQ)])
                if masked:
                    keyi = j * TK + lax.broadcasted_iota(jnp.int32, (TK, TQ), 0)
                    qryi = i * TQ + lax.broadcasted_iota(jnp.int32, (TK, TQ), 1)
                    pt = jnp.where(keyi <= qryi, pt, 0.0)
                dpt = lax.dot_general(vs[h], dov, _DIMS["nt"], preferred_element_type=F32)
                dst = (pt * (dpt - dl_ref[h, :, pl.ds(start, TQ)])).astype(BF16)
                dv = dv + lax.dot_general(pt.astype(BF16), dov, _DIMS["nn"], preferred_element_type=F32)
                dk = dk + lax.dot_general(dst, qv, _DIMS["nn"], preferred_element_type=F32)
                dq_ref[pl.ds(start, TQ), cols] += lax.dot_general(dst, ks[h], _DIMS["tn"], preferred_element_type=F32)
                out.append((dk, dv))
            return tuple(out)

        init = tuple((jnp.zeros((TK, HEAD_PAD), F32), jnp.zeros((TK, V_HEAD), F32)) for _ in range(HP))
        first = j // ratio
        carry = lax.fori_loop(first + 1, nq, functools.partial(step, masked=False), step(first, init, True))
        for h in range(HP):
            dk_ref[:, h * HEAD_PAD:(h + 1) * HEAD_PAD] = carry[h][0]
            dv_ref[:, h * V_HEAD:(h + 1) * V_HEAD] = carry[h][1]

    return pl.pallas_call(
        body, grid=(N_HEADS // HP, S_ // TK),
        in_specs=[pl.BlockSpec((S_, W), lambda g, j: (0, g)),
                  pl.BlockSpec((S_, HP * V_HEAD), lambda g, j: (0, g)),
                  pl.BlockSpec((HP, 1, S_), lambda g, j: (g, 0, 0)),
                  pl.BlockSpec((HP, 1, S_), lambda g, j: (g, 0, 0)),
                  pl.BlockSpec((TK, W), lambda g, j: (j, g)),
                  pl.BlockSpec((TK, W), lambda g, j: (j, g))],
        out_specs=[pl.BlockSpec((S_, W), lambda g, j: (0, g)),
                   pl.BlockSpec((TK, W), lambda g, j: (j, g)),
                   pl.BlockSpec((TK, HP * V_HEAD), lambda g, j: (j, g))],
        out_shape=[jax.ShapeDtypeStruct((S_, N_HEADS * HEAD_PAD), F32),
                   jax.ShapeDtypeStruct((S_, N_HEADS * HEAD_PAD), F32),
                   jax.ShapeDtypeStruct((S_, N_HEADS * V_HEAD), F32)],
        compiler_params=_cparams("parallel", "arbitrary"), name=name,
    )(q, do, lse_row, delta_row, k, vx)


def mods_fwd(c_all, mod_w, mod_b, name, tn=512):
    L, Dn, E = mod_w.shape
    R = c_all.shape[0]
    tn = _pick(E, tn, LANES)

    def body(c_ref, w_ref, b_ref, o_ref):
        cv = c_ref[...]
        sc = (cv / (1.0 + jnp.exp(-cv))).astype(BF16)
        o_ref[...] = lax.dot_general(sc, w_ref[...].astype(BF16), _DIMS["nn"], preferred_element_type=F32) + b_ref[...]

    return pl.pallas_call(
        body, grid=(L, E // tn),
        in_specs=[pl.BlockSpec((R, Dn), lambda l, j: (0, 0)), pl.BlockSpec((None, Dn, tn), lambda l, j: (l, 0, j)),
                  pl.BlockSpec((None, 1, tn), lambda l, j: (l, 0, j))],
        out_specs=pl.BlockSpec((None, R, tn), lambda l, j: (l, 0, j)),
        out_shape=jax.ShapeDtypeStruct((L, R, E), F32),
        compiler_params=_cparams("parallel", "parallel"), name=name,
    )(c_all, mod_w, mod_b.reshape(L, 1, E))


def _adam_math(w, g, m, v):
    m = ADAM_B1 * m + (1.0 - ADAM_B1) * g
    v = ADAM_B2 * v + (1.0 - ADAM_B2) * (g * g)
    m_hat = m / (1.0 - ADAM_B1 ** ADAM_STEP)
    v_hat = v / (1.0 - ADAM_B2 ** ADAM_STEP)
    delta = -ADAM_LR * (m_hat / (jnp.sqrt(v_hat) + ADAM_EPS) + ADAM_WD * w)
    return delta, m, v


def _as2d(a):
    return a.reshape(-1, a.shape[-1]) if a.ndim != 2 else a


def adamw(w, g, m, v, name):
    shape = w.shape
    w2, g2, m2, v2 = _as2d(w), _as2d(g), _as2d(m), _as2d(v)
    R, C = w2.shape
    tr = _pick(R, max(8, (1 << 19) // C // 8 * 8), 8)

    def body(w_ref, g_ref, m_ref, v_ref, d_ref, mo_ref, vo_ref):
        d, mn, vn = _adam_math(w_ref[...], g_ref[...], m_ref[...], v_ref[...])
        d_ref[...] = d
        mo_ref[...] = mn
        vo_ref[...] = vn

    blk = pl.BlockSpec((tr, C), lambda i: (i, 0))
    shp = jax.ShapeDtypeStruct((R, C), F32)
    outs = pl.pallas_call(
        body, grid=(R // tr,), in_specs=[blk] * 4, out_specs=[blk] * 3, out_shape=[shp] * 3,
        compiler_params=_cparams("parallel"), name=name,
    )(w2, g2, m2, v2)
    return tuple(o.reshape(shape) for o in outs)


def adamw_sum(parts, w, m, v, name):
    P, R, C = parts.shape

    def body(p_ref, w_ref, m_ref, v_ref, g_ref, d_ref, mo_ref, vo_ref):
        g = p_ref[0]
        for k in range(1, P):
            g = g + p_ref[k]
        d, mn, vn = _adam_math(w_ref[...], g, m_ref[...], v_ref[...])
        g_ref[...] = g
        d_ref[...] = d
        mo_ref[...] = mn
        vo_ref[...] = vn

    shp = jax.ShapeDtypeStruct((R, C), F32)
    return pl.pallas_call(body, out_shape=[shp] * 4, compiler_params=_cparams(), name=name)(parts, w, m, v)


def adamw_modw(c_col, dm, w, m, v, name, tr=256, tn=1536):
    L, Dn, E = w.shape
    B = c_col.shape[0]
    tr = _pick(Dn, tr, 8)
    tn = _pick(E, tn, LANES)

    def body(c_ref, dm_ref, w_ref, m_ref, v_ref, g_ref, d_ref, mo_ref, vo_ref):
        g = jnp.zeros((tr, tn), F32)
        for b in range(B):
            cv = c_ref[b]
            g = g + (cv / (1.0 + jnp.exp(-cv))) * dm_ref[b:b + 1, :]
        d, mn, vn = _adam_math(w_ref[...], g, m_ref[...], v_ref[...])
        g_ref[...] = g
        d_ref[...] = d
        mo_ref[...] = mn
        vo_ref[...] = vn

    blk = pl.BlockSpec((None, tr, tn), lambda l, i, j: (l, i, j))
    shp = jax.ShapeDtypeStruct((L, Dn, E), F32)
    return pl.pallas_call(
        body, grid=(L, Dn // tr, E // tn),
        in_specs=[pl.BlockSpec((B, tr, 1), lambda l, i, j: (0, i, 0)),
                  pl.BlockSpec((None, B, tn), lambda l, i, j: (l, 0, j)), blk, blk, blk],
        out_specs=[blk] * 4, out_shape=[shp] * 4,
        compiler_params=_cparams("parallel", "parallel", "parallel"), name=name,
    )(c_col, dm, w, m, v)


def add_round(a, b, name, tr=512):
    R, C = a.shape
    tr = _pick(R, tr, 16)

    def body(a_ref, b_ref, o_ref):
        o_ref[...] = (a_ref[...] + b_ref[...].astype(F32)).astype(BF16)

    blk = pl.BlockSpec((tr, C), lambda i: (i, 0))
    return pl.pallas_call(
        body, grid=(R // tr,), in_specs=[blk, blk], out_specs=blk, out_shape=jax.ShapeDtypeStruct((R, C), BF16),
        compiler_params=_cparams("parallel"), name=name,
    )(a, b)


def sum_parts(parts, name, tr=512):
    P, R, C = parts.shape
    tr = _pick(R, tr, 16)

    def body(p_ref, o_ref):
        s = p_ref[0].astype(F32)
        for k in range(1, P):
            s = s + p_ref[k].astype(F32)
        o_ref[...] = s

    return pl.pallas_call(
        body, grid=(R // tr,), in_specs=[pl.BlockSpec((P, tr, C), lambda i: (0, i, 0))],
        out_specs=pl.BlockSpec((tr, C), lambda i: (i, 0)), out_shape=jax.ShapeDtypeStruct((R, C), F32),
        compiler_params=_cparams("parallel"), name=name,
    )(parts)


_ANY = pl.BlockSpec(memory_space=pl.ANY)


def _place():
    return lax.axis_index("x"), lax.axis_index("y"), lax.axis_index("c")


def _flip(v, bit):
    return 1 - v if bit else v


def chip_gather(buf, name):
    def body(in_ref, out_ref, send_sems, recv_sems):
        x, y, c = _place()
        me = 2 * x + y
        sends = []
        for k in range(1, N_CHIPS):
            px, py = _flip(x, k >> 1), _flip(y, k & 1)
            cp = pltpu.make_async_remote_copy(src_ref=in_ref, dst_ref=out_ref.at[me], send_sem=send_sems.at[k - 1],
                                              recv_sem=recv_sems.at[k - 1], device_id=(px, py, c), device_id_type=MESH)
            cp.start()
            sends.append(cp)
        for k in range(1, N_CHIPS):
            px, py = _flip(x, k >> 1), _flip(y, k & 1)
            pltpu.make_async_remote_copy(src_ref=in_ref, dst_ref=out_ref.at[2 * px + py], send_sem=send_sems.at[k - 1],
                                         recv_sem=recv_sems.at[k - 1], device_id=(px, py, c),
                                         device_id_type=MESH).wait_recv()
        for cp in sends:
            cp.wait_send()

    out = pl.pallas_call(
        body, in_specs=[_ANY], out_specs=_ANY,
        out_shape=jax.ShapeDtypeStruct((N_CHIPS,) + buf.shape, buf.dtype),
        scratch_shapes=[pltpu.SemaphoreType.DMA((N_CHIPS - 1,)), pltpu.SemaphoreType.DMA((N_CHIPS - 1,))],
        name=name,
    )(buf)
    return lax.dynamic_update_index_in_dim(out, buf, 2 * lax.axis_index("x") + lax.axis_index("y"), 0)


def chip_all_to_all(buf, name):
    def body(in_ref, out_ref, send_sems, recv_sems):
        x, y, c = _place()
        me = 2 * x + y
        sends = []
        for k in range(1, N_CHIPS):
            px, py = _flip(x, k >> 1), _flip(y, k & 1)
            cp = pltpu.make_async_remote_copy(src_ref=in_ref.at[2 * px + py], dst_ref=out_ref.at[me],
                                              send_sem=send_sems.at[k - 1], recv_sem=recv_sems.at[k - 1],
                                              device_id=(px, py, c), device_id_type=MESH)
            cp.start()
            sends.append(cp)
        for k in range(1, N_CHIPS):
            px, py = _flip(x, k >> 1), _flip(y, k & 1)
            pltpu.make_async_remote_copy(src_ref=in_ref.at[me], dst_ref=out_ref.at[2 * px + py],
                                         send_sem=send_sems.at[k - 1], recv_sem=recv_sems.at[k - 1],
                                         device_id=(px, py, c), device_id_type=MESH).wait_recv()
        for cp in sends:
            cp.wait_send()

    out = pl.pallas_call(
        body, in_specs=[_ANY], out_specs=_ANY, out_shape=jax.ShapeDtypeStruct(buf.shape, buf.dtype),
        scratch_shapes=[pltpu.SemaphoreType.DMA((N_CHIPS - 1,)), pltpu.SemaphoreType.DMA((N_CHIPS - 1,))],
        name=name,
    )(buf)
    me = 2 * lax.axis_index("x") + lax.axis_index("y")
    return lax.dynamic_update_index_in_dim(out, _index(buf, me), me, 0)


def core_gather(buf, name):
    def body(in_ref, out_ref, send_sem, recv_sem):
        x, y, c = _place()
        cp = pltpu.make_async_remote_copy(src_ref=in_ref, dst_ref=out_ref.at[c], send_sem=send_sem, recv_sem=recv_sem,
                                          device_id=(x, y, 1 - c), device_id_type=MESH)
        cp.start()
        pltpu.make_async_remote_copy(src_ref=in_ref, dst_ref=out_ref.at[1 - c], send_sem=send_sem, recv_sem=recv_sem,
                                     device_id=(x, y, 1 - c), device_id_type=MESH).wait_recv()
        cp.wait_send()

    out = pl.pallas_call(
        body, in_specs=[_ANY], out_specs=_ANY, out_shape=jax.ShapeDtypeStruct((2,) + buf.shape, buf.dtype),
        scratch_shapes=[pltpu.SemaphoreType.DMA, pltpu.SemaphoreType.DMA],
        name=name,
    )(buf)
    return lax.dynamic_update_index_in_dim(out, buf, lax.axis_index("c"), 0)


def core_swap(buf, name):
    def body(in_ref, out_ref, send_sem, recv_sem):
        x, y, c = _place()
        cp = pltpu.make_async_remote_copy(src_ref=in_ref, dst_ref=out_ref, send_sem=send_sem, recv_sem=recv_sem,
                                          device_id=(x, y, 1 - c), device_id_type=MESH)
        cp.start()
        cp.wait()

    return pl.pallas_call(
        body, in_specs=[_ANY], out_specs=_ANY, out_shape=jax.ShapeDtypeStruct(buf.shape, buf.dtype),
        scratch_shapes=[pltpu.SemaphoreType.DMA, pltpu.SemaphoreType.DMA],
        name=name,
    )(buf)


def device_gather(buf, name):
    def body(in_ref, out_ref, send_sems, recv_sems, local_sem):
        x, y, c = _place()
        me = 4 * x + 2 * y + c
        mine = pltpu.make_async_copy(in_ref, out_ref.at[me], local_sem)
        mine.start()
        sends = []
        for k in range(1, N_DEV):
            peer = (_flip(x, (k >> 2) & 1), _flip(y, (k >> 1) & 1), _flip(c, k & 1))
            cp = pltpu.make_async_remote_copy(src_ref=in_ref, dst_ref=out_ref.at[me], send_sem=send_sems.at[k - 1],
                                              recv_sem=recv_sems.at[k - 1], device_id=peer, device_id_type=MESH)
            cp.start()
            sends.append(cp)
        for k in range(1, N_DEV):
            peer = (_flip(x, (k >> 2) & 1), _flip(y, (k >> 1) & 1), _flip(c, k & 1))
            pltpu.make_async_remote_copy(src_ref=in_ref, dst_ref=out_ref.at[4 * peer[0] + 2 * peer[1] + peer[2]],
                                         send_sem=send_sems.at[k - 1], recv_sem=recv_sems.at[k - 1], device_id=peer,
                                         device_id_type=MESH).wait_recv()
        for cp in sends:
            cp.wait_send()
        mine.wait()

    return pl.pallas_call(
        body, in_specs=[_ANY], out_specs=_ANY, out_shape=jax.ShapeDtypeStruct((N_DEV,) + buf.shape, buf.dtype),
        scratch_shapes=[pltpu.SemaphoreType.DMA((N_DEV - 1,)), pltpu.SemaphoreType.DMA((N_DEV - 1,)),
                        pltpu.SemaphoreType.DMA],
        name=name,
    )(buf)


def _region(ref, chip_axis=None, chip=None, chip_size=None, half_axis=None, half=None, half_size=None):
    idx = [slice(None)] * len(ref.shape)
    if chip is not None:
        idx[chip_axis] = pl.ds(chip * chip_size, chip_size)
    if half is not None:
        idx[half_axis] = pl.ds(half * half_size, half_size)
    return ref.at[tuple(idx)]


def gather_weights(shards, axes, name, after=()):
    n = len(shards)

    def full_shape(t):
        shp = list(shards[t].shape)
        shp[axes[t][0]] *= N_CHIPS
        return tuple(shp)

    def body(*refs):
        ins, outs = refs[:n], refs[n + len(after):2 * n + len(after)]
        ici_send, ici_recv, d2d_send, d2d_recv, own_send, own_recv = refs[2 * n + len(after):]
        x, y, c = _place()
        me = 2 * x + y

        def part(t, ref, chip, half):
            ca, ha = axes[t]
            return _region(ref, ca, chip, ins[t].shape[ca], ha, half, ins[t].shape[ha] // 2)

        def own(t):
            return pltpu.make_async_remote_copy(src_ref=ins[t], dst_ref=part(t, outs[t], me, None),
                                                send_sem=own_send.at[t], recv_sem=own_recv.at[t],
                                                device_id=(x, y, 1 - c), device_id_type=MESH)

        started = []
        for t in range(n):
            own(t).start()
            started.append(own(t))
        for t in range(n):
            for k in range(1, N_CHIPS):
                px, py = _flip(x, k >> 1), _flip(y, k & 1)
                cp = pltpu.make_async_remote_copy(src_ref=part(t, ins[t], None, c), dst_ref=part(t, outs[t], me, c),
                                                  send_sem=ici_send.at[t, k - 1], recv_sem=ici_recv.at[t, k - 1],
                                                  device_id=(px, py, c), device_id_type=MESH)
                cp.start()
                started.append(cp)
        for t in range(n):
            for k in range(1, N_CHIPS):
                px, py = _flip(x, k >> 1), _flip(y, k & 1)
                got = part(t, outs[t], 2 * px + py, c)
                pltpu.make_async_remote_copy(src_ref=part(t, ins[t], None, c), dst_ref=got,
                                             send_sem=ici_send.at[t, k - 1], recv_sem=ici_recv.at[t, k - 1],
                                             device_id=(px, py, c), device_id_type=MESH).wait_recv()
                fw = pltpu.make_async_remote_copy(src_ref=got, dst_ref=got, send_sem=d2d_send.at[t, k - 1],
                                                  recv_sem=d2d_recv.at[t, k - 1], device_id=(x, y, 1 - c),
                                                  device_id_type=MESH)
                fw.start()
                started.append(fw)
        for t in range(n):
            for k in range(1, N_CHIPS):
                px, py = _flip(x, k >> 1), _flip(y, k & 1)
                theirs = part(t, outs[t], 2 * px + py, 1 - c)
                pltpu.make_async_remote_copy(src_ref=theirs, dst_ref=theirs, send_sem=d2d_send.at[t, k - 1],
                                             recv_sem=d2d_recv.at[t, k - 1], device_id=(x, y, 1 - c),
                                             device_id_type=MESH).wait_recv()
        for t in range(n):
            own(t).wait_recv()
        for cp in started:
            cp.wait_send()

    sem = pltpu.SemaphoreType.DMA((n, N_CHIPS - 1))
    own_sem = pltpu.SemaphoreType.DMA((n,))
    return pl.pallas_call(
        body, in_specs=[_ANY] * (n + len(after)), out_specs=[_ANY] * n,
        out_shape=[jax.ShapeDtypeStruct(full_shape(t), shards[t].dtype) for t in range(n)],
        scratch_shapes=[sem, sem, sem, sem, own_sem, own_sem], name=name,
    )(*shards, *after)


_HBM = pl.BlockSpec(memory_space=pltpu.HBM)
_SEM = pl.BlockSpec(memory_space=pltpu.SEMAPHORE)
_EFFECT = pltpu.SideEffectType.DATAFLOW_SIDE_EFFECTING
WEIGHT_COPIES = N_CHIPS


def _weight_peer(k, x, y, c):
    return (x, y, 1 - c) if k == 0 else (_flip(x, k >> 1), _flip(y, k & 1), c)


def weights_start(shards, items, name, after=()):
    n_sh, n_it = len(shards), len(items)

    def src_of(refs, i):
        t, layer, _ = items[i]
        return refs[t] if layer is None else refs[t].at[layer]

    def land_shape(i):
        t, layer, ca = items[i]
        shp = list(shards[t].shape if layer is None else shards[t].shape[1:])
        shp[ca] *= N_CHIPS
        return tuple(shp)

    def body(*refs):
        shard_refs, land_refs = refs[:n_sh], refs[n_sh:n_sh + n_it]
        first_out = n_sh + n_it + len(after)
        send_sems = refs[first_out:first_out + n_it]
        recv_sems = refs[first_out + n_it:first_out + 2 * n_it]
        token = refs[-1]
        x, y, c = _place()
        me = 2 * x + y
        for i in range(n_it):
            src = src_of(shard_refs, i)
            ca = items[i][2]
            dst = _region(land_refs[i], ca, me, src.shape[ca])
            for k in range(WEIGHT_COPIES):
                pltpu.make_async_remote_copy(src_ref=src, dst_ref=dst, send_sem=send_sems[i], recv_sem=recv_sems[i],
                                             device_id=_weight_peer(k, x, y, c), device_id_type=MESH).start()
        token[...] = jnp.zeros_like(token)

    lands = [pltpu.with_memory_space_constraint(lax.empty(land_shape(i), shards[0].dtype), pltpu.HBM)
             for i in range(n_it)]
    ins = [pltpu.with_memory_space_constraint(a, pltpu.HBM) for a in shards] + lands
    sems = (pltpu.SemaphoreType.DMA(()),) * (2 * n_it)
    outs = pl.pallas_call(
        body, name=name,
        out_shape=sems + tuple(pltpu.HBM(a.shape, a.dtype) for a in ins) + (jax.ShapeDtypeStruct((8, LANES), F32),),
        in_specs=[_HBM] * len(ins) + [_ANY] * len(after),
        out_specs=(_SEM,) * (2 * n_it) + (_HBM,) * len(ins) + (pl.BlockSpec(memory_space=pltpu.VMEM),),
        input_output_aliases={i: 2 * n_it + i for i in range(len(ins))},
        compiler_params=pltpu.CompilerParams(has_side_effects=_EFFECT),
    )(*ins, *after)
    base = 2 * n_it
    return (list(outs[:n_it]), list(outs[n_it:base]), list(outs[base:base + n_sh]),
            list(outs[base + n_sh:base + n_sh + n_it]), outs[-1])


def weights_wait(send_sems, recv_sems, lands, after, keep, name):
    m = len(lands)

    def body(*refs):
        land_refs, send_refs, recv_refs = refs[:m], refs[m:2 * m], refs[2 * m:3 * m]
        x, y, c = _place()
        for j in range(m):
            cp = pltpu.make_async_remote_copy(src_ref=land_refs[j], dst_ref=land_refs[j], send_sem=send_refs[j],
                                              recv_sem=recv_refs[j], device_id=(x, y, 1 - c),
                                              device_id_type=MESH)
            cp.wait_send()
            cp.wait_recv()

    outs = pl.pallas_call(
        body, name=name,
        out_shape=tuple(pltpu.HBM(a.shape, a.dtype) for a in lands),
        in_specs=[_HBM] * m + [_SEM] * (2 * m) + [_ANY] + [_HBM] * len(keep),
        out_specs=(_HBM,) * m,
        input_output_aliases={j: j for j in range(m)},
        compiler_params=pltpu.CompilerParams(has_side_effects=_EFFECT),
    )(*lands, *send_sems, *recv_sems, after, *keep)
    return list(outs)


def reduce_to_sibling(lo, hi, name):
    n = len(lo)

    def body(*refs):
        los, his, outs = refs[:n], refs[n:2 * n], refs[2 * n:3 * n]
        send_sems, recv_sems = refs[3 * n:]
        x, y, c = _place()

        def copy(u, src):
            return pltpu.make_async_remote_copy(src_ref=src, dst_ref=outs[u], send_sem=send_sems.at[u],
                                                recv_sem=recv_sems.at[u], device_id=(x, y, 1 - c), device_id_type=MESH)

        for u in range(n):
            @pl.when(c == 0)
            def _(u=u):
                copy(u, his[u]).start()

            @pl.when(c == 1)
            def _(u=u):
                copy(u, los[u]).start()
        for u in range(n):
            copy(u, los[u]).wait_recv()
        for u in range(n):
            copy(u, los[u]).wait_send()

    return pl.pallas_call(
        body, in_specs=[_ANY] * (2 * n), out_specs=[_ANY] * n,
        out_shape=[jax.ShapeDtypeStruct(a.shape, a.dtype) for a in lo],
        scratch_shapes=[pltpu.SemaphoreType.DMA((n,)), pltpu.SemaphoreType.DMA((n,))], name=name,
    )(*lo, *hi)


def add_selected(lo, hi, other, name, tile_elems=1 << 19):
    R, C = lo.shape
    tr = _pick(R, max(16, tile_elems // C // 16 * 16), 16)

    def body(lo_ref, hi_ref, o_ref, out_ref):
        mine = jnp.where(lax.axis_index("c") == 0, lo_ref[...].astype(F32), hi_ref[...].astype(F32))
        out_ref[...] = (mine + o_ref[...].astype(F32)).astype(out_ref.dtype)

    blk = pl.BlockSpec((tr, C), lambda i: (i, 0))
    return pl.pallas_call(
        body, grid=(R // tr,), in_specs=[blk, blk, blk], out_specs=blk, out_shape=jax.ShapeDtypeStruct((R, C), BF16),
        compiler_params=_cparams("parallel"), name=name,
    )(lo, hi, other)


def scatter_to_chips(pieces, chip_axes, name):
    n = len(pieces)

    def block_shape(u):
        shp = list(pieces[u].shape)
        shp[chip_axes[u]] //= N_CHIPS
        return tuple(shp)

    def body(*refs):
        ins, outs = refs[:n], refs[n:2 * n]
        send_sems, recv_sems = refs[2 * n:]
        x, y, c = _place()
        me = 2 * x + y
        started = []
        for u in range(n):
            size = block_shape(u)[chip_axes[u]]
            for k in range(1, N_CHIPS):
                px, py = _flip(x, k >> 1), _flip(y, k & 1)
                cp = pltpu.make_async_remote_copy(src_ref=_region(ins[u], chip_axes[u], 2 * px + py, size),
                                                  dst_ref=outs[u].at[me], send_sem=send_sems.at[u, k - 1],
                                                  recv_sem=recv_sems.at[u, k - 1], device_id=(px, py, c),
                                                  device_id_type=MESH)
                cp.start()
                started.append(cp)
        for u in range(n):
            size = block_shape(u)[chip_axes[u]]
            for k in range(1, N_CHIPS):
                px, py = _flip(x, k >> 1), _flip(y, k & 1)
                pltpu.make_async_remote_copy(src_ref=_region(ins[u], chip_axes[u], me, size),
                                             dst_ref=outs[u].at[2 * px + py], send_sem=send_sems.at[u, k - 1],
                                             recv_sem=recv_sems.at[u, k - 1], device_id=(px, py, c),
                                             device_id_type=MESH).wait_recv()
        for cp in started:
            cp.wait_send()

    sem = pltpu.SemaphoreType.DMA((n, N_CHIPS - 1))
    return pl.pallas_call(
        body, in_specs=[_ANY] * n, out_specs=[_ANY] * n,
        out_shape=[jax.ShapeDtypeStruct((N_CHIPS,) + block_shape(u), pieces[u].dtype) for u in range(n)],
        scratch_shapes=[sem, sem], name=name,
    )(*pieces)


def scatter_start(pieces, chip_axes, name, after=()):
    n = len(pieces)

    def block_shape(u):
        shp = list(pieces[u].shape)
        shp[chip_axes[u]] //= N_CHIPS
        return tuple(shp)

    def body(*refs):
        ins, land_refs = refs[:n], refs[n:2 * n]
        first_out = 2 * n + len(after)
        send_sems, recv_sems = refs[first_out:first_out + n], refs[first_out + n:first_out + 2 * n]
        token = refs[-1]
        x, y, c = _place()
        me = 2 * x + y
        for u in range(n):
            size = block_shape(u)[chip_axes[u]]
            for k in range(1, N_CHIPS):
                px, py = _flip(x, k >> 1), _flip(y, k & 1)
                pltpu.make_async_remote_copy(src_ref=_region(ins[u], chip_axes[u], 2 * px + py, size),
                                             dst_ref=land_refs[u].at[me], send_sem=send_sems[u], recv_sem=recv_sems[u],
                                             device_id=(px, py, c), device_id_type=MESH).start()
        token[...] = jnp.zeros_like(token)

    lands = [pltpu.with_memory_space_constraint(lax.empty((N_CHIPS,) + block_shape(u), pieces[u].dtype), pltpu.HBM)
             for u in range(n)]
    ins = [pltpu.with_memory_space_constraint(a, pltpu.HBM) for a in pieces] + lands
    sems = (pltpu.SemaphoreType.DMA(()),) * (2 * n)
    outs = pl.pallas_call(
        body, name=name,
        out_shape=sems + tuple(pltpu.HBM(a.shape, a.dtype) for a in ins) + (jax.ShapeDtypeStruct((8, LANES), F32),),
        in_specs=[_HBM] * len(ins) + [_ANY] * len(after),
        out_specs=(_SEM,) * (2 * n) + (_HBM,) * len(ins) + (pl.BlockSpec(memory_space=pltpu.VMEM),),
        input_output_aliases={i: 2 * n + i for i in range(len(ins))},
        compiler_params=pltpu.CompilerParams(has_side_effects=_EFFECT),
    )(*ins, *after)
    return list(outs[:n]), list(outs[n:2 * n]), list(outs[2 * n:3 * n]), list(outs[3 * n:4 * n]), outs[-1]


def scatter_wait(send_sems, recv_sems, lands, pieces, after, name):
    n = len(lands)

    def body(*refs):
        land_refs, send_refs, recv_refs = refs[:n], refs[n:2 * n], refs[2 * n:3 * n]
        x, y, c = _place()
        for u in range(n):
            three = land_refs[u].at[pl.ds(0, N_CHIPS - 1)]
            cp = pltpu.make_async_remote_copy(src_ref=three, dst_ref=three, send_sem=send_refs[u], recv_sem=recv_refs[u],
                                              device_id=(x, y, 1 - c), device_id_type=MESH)
            cp.wait_send()
            cp.wait_recv()

    outs = pl.pallas_call(
        body, name=name,
        out_shape=tuple(pltpu.HBM(a.shape, a.dtype) for a in lands),
        in_specs=[_HBM] * n + [_SEM] * (2 * n) + [_ANY] + [_HBM] * len(pieces),
        out_specs=(_HBM,) * n,
        input_output_aliases={j: j for j in range(n)},
        compiler_params=pltpu.CompilerParams(has_side_effects=_EFFECT),
    )(*lands, *send_sems, *recv_sems, after, *pieces)
    return list(outs)


def gather_halves(parts, slots, out_shapes, name):
    n = len(parts)

    def body(*refs):
        ins, outs = refs[:n], refs[n:n + len(out_shapes)]
        send_sems, recv_sems = refs[n + len(out_shapes):]
        x, y, c = _place()
        started = []
        for u in range(n):
            t, s = slots[u]
            cp = pltpu.make_async_remote_copy(src_ref=ins[u], dst_ref=outs[t].at[s, c], send_sem=send_sems.at[u],
                                              recv_sem=recv_sems.at[u], device_id=(x, y, 1 - c), device_id_type=MESH)
            cp.start()
            started.append(cp)
        for u in range(n):
            t, s = slots[u]
            pltpu.make_async_remote_copy(src_ref=ins[u], dst_ref=outs[t].at[s, 1 - c], send_sem=send_sems.at[u],
                                         recv_sem=recv_sems.at[u], device_id=(x, y, 1 - c),
                                         device_id_type=MESH).wait_recv()
        for cp in started:
            cp.wait_send()

    return pl.pallas_call(
        body, in_specs=[_ANY] * n, out_specs=[_ANY] * len(out_shapes),
        out_shape=[jax.ShapeDtypeStruct(shp, F32) for shp in out_shapes],
        scratch_shapes=[pltpu.SemaphoreType.DMA((n,)), pltpu.SemaphoreType.DMA((n,))], name=name,
    )(*parts)


WEIGHT_ORDER = ["mod_w", "mod_b", "norm1_g", "norm2_g", "pool_w", "pool_b", "pool_scale", "kv_in_g", "w_dkv",
                "ckv_norm_g", "w_uk", "w_uv", "w_dq", "q_norm_g", "w_uq", "w_o", "w_up", "conv_w", "conv_b", "w_down",
                "final_g"]
EXCHANGED = {"w_up": (2, 0), "w_down": (1, 0), "w_o": (1, 0), "w_uq": (2, 0), "w_dq": (1, 0), "pool_w": (2, 0),
             "w_dkv": (0, 1), "w_uk": (1, 0), "w_uv": (1, 0)}
SMALL_SHARDED = {"conv_w": 2, "pool_b": 1, "pool_scale": 1}
REPLICATED = ["mod_b", "norm1_g", "norm2_g", "kv_in_g", "ckv_norm_g", "q_norm_g", "conv_b", "final_g"]


def _padded(n, align):
    return -(-n // align) * align


def _flat_pad(parts, total):
    flat = jnp.concatenate(parts, axis=-1)
    pad = total - flat.shape[-1]
    if pad:
        flat = jnp.concatenate([flat, jnp.zeros(flat.shape[:-1] + (pad,), flat.dtype)], axis=-1)
    return flat


def _split_shards(full, axis):
    shp = full.shape
    t = full.reshape(shp[:axis] + (N_CHIPS, shp[axis] // N_CHIPS) + shp[axis + 1:])
    return jnp.moveaxis(t, axis, 0).reshape(N_CHIPS, -1)


def _join_shards(rows, shard_shape, axis):
    t = jnp.moveaxis(rows.reshape((N_CHIPS,) + tuple(shard_shape)), 0, axis)
    return t.reshape(tuple(shard_shape[:axis]) + (N_CHIPS * shard_shape[axis],) + tuple(shard_shape[axis + 1:]))


def _index(a, i, axis=0):
    return lax.dynamic_index_in_dim(a, i, axis, keepdims=False)


def kernel(x, c, positions, mod_w, mod_b, norm1_g, norm2_g, pool_w, pool_b, pool_scale, kv_in_g, w_dkv, ckv_norm_g, w_uk, w_uv, w_dq, q_norm_g, w_uq, w_o, w_up, conv_w, conv_b, w_down, final_g, loss_target, m_mod_w, m_mod_b, m_norm1_g, m_norm2_g, m_pool_w, m_pool_b, m_pool_scale, m_kv_in_g, m_w_dkv, m_ckv_norm_g, m_w_uk, m_w_uv, m_w_dq, m_q_norm_g, m_w_uq, m_w_o, m_w_up, m_conv_w, m_conv_b, m_w_down, m_final_g, v_mod_w, v_mod_b, v_norm1_g, v_norm2_g, v_pool_w, v_pool_b, v_pool_scale, v_kv_in_g, v_w_dkv, v_ckv_norm_g, v_w_uk, v_w_uv, v_w_dq, v_q_norm_g, v_w_uq, v_w_o, v_w_up, v_conv_w, v_conv_b, v_w_down, v_final_g):
    given = dict(locals())
    W = {n: given[n] for n in WEIGHT_ORDER}
    M1 = {n: given["m_" + n] for n in WEIGHT_ORDER}
    V2 = {n: given["v_" + n] for n in WEIGHT_ORDER}
    xi, yi, ci = lax.axis_index("x"), lax.axis_index("y"), lax.axis_index("c")
    chip = 2 * xi + yi
    dev = 4 * xi + 2 * yi + ci
    x0 = x[0]
    S_, D = x0.shape
    Fh = conv_b.shape[1]
    E = mod_b.shape[1]
    Es = E // N_CHIPS
    zD = jnp.zeros((D,), F32)

    c_all = device_gather(c, "gather_c").reshape(N_DEV, D)
    c_pad = jnp.concatenate([c_all, jnp.zeros((16 - N_DEV, D), F32)], axis=0)
    mod_b_mine = lax.dynamic_slice_in_dim(mod_b, chip * Es, Es, axis=1)
    mods_part = mods_fwd(c_pad, mod_w, mod_b_mine, "mods_fwd")
    mods_all = chip_gather(mods_part, "gather_mods")
    mods = jnp.swapaxes(_index(mods_all, dev, axis=2), 0, 1).reshape(DEPTH, E)
    mod = [[mods[l, k * D:(k + 1) * D] for k in range(6)] for l in range(DEPTH)]

    full = {}
    ssz = {n: math.prod(W[n].shape) for n in SMALL_SHARDED}
    Tw = _padded(sum(ssz.values()), 8 * PACK_COLS)
    small_rows = chip_gather(_flat_pad([W[n].reshape(-1) for n in SMALL_SHARDED], Tw).reshape(-1, PACK_COLS),
                             "gather_small_w").reshape(N_CHIPS, Tw)
    off = 0
    for n, axis in SMALL_SHARDED.items():
        full[n] = _join_shards(small_rows[:, off:off + ssz[n]], W[n].shape, axis)
        off += ssz[n]

    names = list(EXCHANGED)
    shards = [W[n].astype(BF16) for n in names]
    n_mla = DEPTH - N_A
    first_axes = {"w_up": (1, 0), "w_down": (0, 1), "pool_w": (1, 0)}
    first = gather_weights([shards[names.index(n)][0] for n in first_axes], list(first_axes.values()), "gather_weights0",
                           after=[mods, small_rows])
    for n, arr in zip(first_axes, first):
        full[(n, 0)] = arr
    items, groups = [], []

    def group(entries):
        groups.append(list(range(len(items), len(items) + len(entries))))
        for n, layer in entries:
            ca = EXCHANGED[n][0] - (0 if layer is None else 1)
            items.append((names.index(n), layer, 0 if n == "w_dkv" else ca))

    for l in range(1, N_A):
        group([("w_up", l), ("w_down", l), ("pool_w", l)])
    for j in range(n_mla):
        head = [("w_dkv", None), ("w_uk", None), ("w_uv", None)] if j == 0 else []
        group(head + [("w_dq", j), ("w_uq", j), ("w_o", j), ("w_up", N_A + j), ("w_down", N_A + j)])
    w_send, w_recv, shards_thru, lands, _ = weights_start(shards, items, "weights_start", after=first)

    def weights_ready(g, after):
        keep = shards_thru if g == len(groups) - 1 else []
        got = weights_wait([w_send[i] for i in groups[g]], [w_recv[i] for i in groups[g]], [lands[i] for i in groups[g]],
                           after, keep, f"weights_wait{g}")
        for i, arr in zip(groups[g], got):
            t, layer, _ = items[i]
            full[(names[t], 0 if layer is None else layer)] = arr

    q_rank = W["w_uq"].shape[1]
    kv_w = KV_RANK + QK_ROPE

    def uq_ext(j):
        wq = full[("w_uq", j)].reshape(q_rank, N_HEADS, QK_HEAD)
        return jnp.concatenate([wq, jnp.zeros((q_rank, N_HEADS, HEAD_PAD - QK_HEAD), BF16)],
                               axis=2).reshape(q_rank, N_HEADS * HEAD_PAD)


    half = QK_ROPE // 2
    inv = 1.0 / (ROPE_THETA ** (jnp.arange(0, QK_ROPE, 2, dtype=F32) / QK_ROPE))
    inv_row = jnp.concatenate([inv, inv, jnp.zeros((LANES - 2 * half,), F32)]).reshape(1, LANES)
    tabs = rope_tables(positions[0].astype(F32).reshape(S_, 1), inv_row, "rope_tables")
    att_scale = QK_HEAD ** -0.5

    saved = []
    xcur = x0
    kv_saved = None
    K = VX = knv = None
    for l in range(DEPTH):
        sh1, sc1, g1, sh2, sc2, g2 = mod[l]
        st = {"xin": xcur}
        if l:
            weights_ready(l - 1, xcur)
        if l == N_A:
            w_dkv_ext = jnp.concatenate([full[("w_dkv", 0)], jnp.zeros((D, KV_RANK + LANES - kv_w), BF16)], axis=1)
            w_ukv = jnp.concatenate([full[("w_uk", 0)], full[("w_uv", 0)]], axis=1)
            xn = norm_fwd(xcur, kv_in_g, zD, zD, BF16, "kvin_fwd")
            kv_ext = mm(xn, w_dkv_ext, "nn", F32, "dkv_mm")
            lat = kv_ext[:, :KV_RANK]
            zk = jnp.zeros((KV_RANK,), F32)
            ckv = norm_fwd(lat, ckv_norm_g, zk, zk, BF16, "ckv_fwd")
            K, VX = kv_proj(ckv, w_ukv, kv_ext, tabs, "ukv_mm")
            kv_saved = {"x": xcur, "xn": xn, "lat": lat, "ckv": ckv}
        if l < N_A:
            h1 = norm_fwd(xcur, norm1_g[l], sc1, sh1, F32, f"norm1_fwd{l}")
            st["pooled"] = _pool_call(h1, BF16, f"pool_fwd{l}", False)
            st["cs"] = g1 * full["pool_scale"][l]
            st["ypre"], xmid = gmm(st["pooled"], full[("pool_w", l)], "nn", BF16, f"pool_mm{l}", bias=full["pool_b"][l],
                                   res=xcur, colscale=st["cs"])
        else:
            j = l - N_A
            st["h1"] = norm_fwd(xcur, norm1_g[l], sc1, sh1, BF16, f"norm1_fwd{l}")
            st["ql"] = mm(st["h1"], full[("w_dq", j)], "nn", F32, f"dq_mm{l}")
            st["cq"] = norm_fwd(st["ql"], q_norm_g[j], jnp.zeros_like(q_norm_g[j]), jnp.zeros_like(q_norm_g[j]), BF16,
                                f"qnorm_fwd{l}")
            st["w_uq_ext"] = uq_ext(j)
            st["Q"] = q_proj(st["cq"], st["w_uq_ext"], tabs, att_scale, f"uq_mm{l}")
            st["o"], lse = attn_fwd(st["Q"], K, VX, f"attn_fwd{l}")
            st["lse"] = lse.reshape(N_HEADS, 1, S_)
            st["y"], xmid = mm(st["o"], full[("w_o", j)], "nn", BF16, f"wo_mm{l}", res=xcur, colscale=g1)
        st["xmid"] = xmid
        st["h2"] = norm_fwd(xmid, norm2_g[l], sc2, sh2, BF16, f"norm2_fwd{l}")
        st["u"] = mm(st["h2"], full[("w_up", l)], "nn", BF16, f"up_mm{l}")
        st["z"] = glu_fwd(st["u"], full["conv_w"][l], conv_b[l], f"glu_fwd{l}")
        st["f"], xcur = mm(st["z"], full[("w_down", l)], "nn", BF16, f"down_mm{l}", tk=1408, res=xmid, colscale=g2)
        saved.append(st)

    dx, d_final_g, loss_part = loss_head(xcur, final_g, loss_target[0], "loss_head")
    loss = lax.psum(loss_part[0, 0], ("x", "y", "c"))

    def begin_reduce(tensors, first_slot, tag):
        units = []
        for n in tensors:
            ca = EXCHANGED[n][0]
            if W[n].ndim > 2:
                n_slots = W[n].shape[0] // 2
                for sl in range(first_slot if n_slots > 1 else 0, first_slot + 1 if n_slots > 1 else 1):
                    units.append((n, sl, G[(n, 2 * sl)], G[(n, 2 * sl + 1)], ca - 1))
            elif n == "w_dkv":
                g4 = G[(n, 0)].reshape(N_CHIPS, 2, -1, kv_w)
                units.append((n, 0, g4[:, 0], g4[:, 1], 0))
            else:
                rows_half = W[n].shape[0] // 2
                units.append((n, 0, G[(n, 0)][:rows_half], G[(n, 0)][rows_half:], ca))
        lo = [u[2] for u in units]
        hi = [u[3] for u in units]
        theirs = reduce_to_sibling(lo, hi, f"reduce_cores_{tag}")
        sums = [add_selected(l_.reshape(-1, l_.shape[-1]), h_.reshape(-1, l_.shape[-1]), t_.reshape(-1, l_.shape[-1]),
                             f"reduce_cores_add_{tag}{i}").reshape(l_.shape)
                for i, (l_, h_, t_) in enumerate(zip(lo, hi, theirs))]
        return units, sums, [u[4] for u in units]

    G = {}
    dmods = [None] * DEPTH
    d_norm1 = [None] * DEPTH
    d_norm2 = [None] * DEPTH
    d_conv_b = [None] * DEPTH
    d_qnorm = [None] * n_mla
    dkv_acc = []
    df, a2, _ = gate_bwd(dx, saved[DEPTH - 1]["f"], mod[DEPTH - 1][5], f"gate2_bwd{DEPTH - 1}")
    for l in reversed(range(DEPTH)):
        sh1, sc1, g1, sh2, sc2, g2 = mod[l]
        st = saved[l]
        next_gate = (saved[l - 1]["f"], mod[l - 1][5]) if l else None
        dz = mm(df, full[("w_down", l)], "nt", BF16, f"down_dx{l}")
        G[("w_down", l)] = mm(st["z"], df, "tn", BF16, f"down_dw{l}")
        du, dcw, dcb = glu_bwd(st["u"], dz, full["conv_w"][l], conv_b[l], f"glu_bwd{l}")
        G[("conv_w", l)] = dcw
        d_conv_b[l] = dcb[0]
        dh2 = mm(du, full[("w_up", l)], "nt", BF16, f"up_dx{l}", tk=1408)
        G[("w_up", l)] = mm(st["h2"], du, "tn", BF16, f"up_dw{l}")
        dxmid, s1, s2, dgate, a1, csum = norm_bwd(st["xmid"], norm2_g[l], sc2, dh2, dx, f"norm2_bwd{l}",
                                                  gate=(st["ypre"], st["cs"]) if l < N_A else (st["y"], g1))
        dsh2, dsc2, d_norm2[l] = s1[0], s2[0] * norm2_g[l], s2[0] * (1.0 + sc2)
        if l < N_A:
            dyp = dgate
            dg1 = full["pool_scale"][l] * a1[0]
            G[("pool_scale", l)] = g1 * a1[0]
            G[("pool_b", l)] = st["cs"] * csum[0]
            dpooled = gmm(dyp, full[("pool_w", l)], "nt", F32, f"pool_dx{l}")
            G[("pool_w", l)] = gmm(st["pooled"], dyp, "tn", BF16, f"pool_dw{l}")
            dh1 = _pool_call(dpooled, F32, f"pool_bwd{l}", True)
        else:
            j = l - N_A
            dy = dgate
            dg1 = a1[0]
            do, delta = o_proj_bwd(dy, full[("w_o", j)], st["o"], f"wo_dx{l}")
            delta = delta.reshape(N_HEADS, 1, S_)
            G[("w_o", j)] = mm(st["o"], dy, "tn", BF16, f"wo_dw{l}")
            dQ, dK, dV = attn_bwd(st["Q"], K, VX, do, st["lse"], delta, f"attn_bwd{l}")
            dkv_acc.append((dK, dV))
            dcq, dw_ext = q_proj_bwd(dQ, st["cq"], st["w_uq_ext"], tabs, att_scale, f"uq_bwd{l}")
            G[("w_uq", j)] = dw_ext.reshape(q_rank, N_HEADS, HEAD_PAD)[:, :, :QK_HEAD].reshape(q_rank, N_HEADS * QK_HEAD)
            zq = jnp.zeros_like(q_norm_g[j])
            dql, _, s2q = norm_bwd(st["ql"], q_norm_g[j], zq, dcq, None, f"qnorm_bwd{l}")
            d_qnorm[j] = s2q[0]
            dh1 = mm(dql, full[("w_dq", j)], "nt", BF16, f"dq_dx{l}")
            G[("w_dq", j)] = mm(st["h1"], dql, "tn", BF16, f"dq_dw{l}")
        a2_mine = a2
        if l and l != N_A:
            dx, s1, s2, df, a2, _ = norm_bwd(st["xin"], norm1_g[l], sc1, dh1, dxmid, f"norm1_bwd{l}", gate=next_gate)
        else:
            dx, s1, s2 = norm_bwd(st["xin"], norm1_g[l], sc1, dh1, dxmid, f"norm1_bwd{l}")
        dsh1, dsc1, d_norm1[l] = s1[0], s2[0] * norm1_g[l], s2[0] * (1.0 + sc1)
        dmods[l] = jnp.concatenate([dsh1, dsc1, dg1, dsh2, dsc2, a2_mine[0]])
        if l == N_A:
            (dk_a, dv_a), (dk_b, dv_b) = dkv_acc
            dknv, d_tk = k_prep_bwd(dk_a, dk_b, dv_a, dv_b, tabs, "k_prep_bwd")
            dckv = mm(dknv, w_ukv, "nt", F32, "ukv_dx")
            d_ukv = mm(kv_saved["ckv"], dknv, "tn", BF16, "ukv_dw")
            G[("w_uk", 0)], G[("w_uv", 0)] = d_ukv[:, :N_HEADS * QK_NOPE], d_ukv[:, N_HEADS * QK_NOPE:]
            zk = jnp.zeros((KV_RANK,), F32)
            dlat, _, s2c = norm_bwd(kv_saved["lat"], ckv_norm_g, zk, dckv, None, "ckv_bwd")
            d_ckv_g = s2c[0]
            dkv_ext = jnp.concatenate([dlat, d_tk], axis=1)
            dxn = mm(dkv_ext, w_dkv_ext, "nt", BF16, "dkv_dx")
            G[("w_dkv", 0)] = mm(kv_saved["xn"], dkv_ext, "tn", BF16, "dkv_dw")[:, :kv_w]
            dx, _, s2k, df, a2, _ = norm_bwd(kv_saved["x"], kv_in_g, zD, dxn, dx, "kvin_bwd", gate=next_gate)
            d_kvin_g = s2k[0]
            e_units, e_sums, e_axes = begin_reduce([n for n in EXCHANGED if n != "pool_w"], 1, "early")
            e_send, e_recv, e_pieces, e_lands, e_token = scatter_start(e_sums, e_axes, "reduce_chips_start")
            early = (e_units, e_send, e_recv, e_lands, e_pieces, e_axes)
            mod[l - 1][4] = mod[l - 1][4] + e_token[0, 0]

    small = {"mod_b": jnp.stack(dmods), "norm1_g": jnp.stack(d_norm1), "norm2_g": jnp.stack(d_norm2),
             "kv_in_g": d_kvin_g, "ckv_norm_g": d_ckv_g, "q_norm_g": jnp.stack(d_qnorm),
             "conv_b": jnp.stack(d_conv_b), "final_g": d_final_g[0]}
    extra = {n: jnp.stack([G[(n, i)] for i in range(W[n].shape[0])]) for n in SMALL_SHARDED}
    ssizes = {n: math.prod(W[n].shape) for n in REPLICATED}
    esizes = {n: math.prod(extra[n].shape) for n in SMALL_SHARDED}
    Ts = _padded(sum(ssizes.values()) + sum(esizes.values()), 8 * PACK_COLS)

    def pack_small(d, tail=()):
        return _flat_pad([d[n].reshape(-1) for n in REPLICATED] + [t.reshape(-1) for t in tail],
                         Ts).reshape(Ts // PACK_COLS, PACK_COLS)

    parts = device_gather(pack_small(small, [extra[n] for n in SMALL_SHARDED]), "gather_small")

    l_units, l_sums, l_axes = begin_reduce([n for n in EXCHANGED if W[n].ndim > 2 and W[n].shape[0] == DEPTH] + ["pool_w"],
                                           0, "late")
    l_send, l_recv, l_pieces, l_lands, l_token = scatter_start(l_sums, l_axes, "reduce_chips_late_start", after=[parts])
    parts = parts + l_token[0, 0]

    grads, deltas, new_m, new_v = {}, {}, {}, {}
    outs = adamw_sum(parts, pack_small(W), pack_small(M1), pack_small(V2), "adamw_small")
    off = 0
    for n in REPLICATED:
        for dst, o in zip((grads, deltas, new_m, new_v), outs):
            dst[n] = o.reshape(-1)[off:off + ssizes[n]].reshape(W[n].shape)
        off += ssizes[n]
    for n, axis in SMALL_SHARDED.items():
        g_full = outs[0].reshape(-1)[off:off + esizes[n]].reshape(extra[n].shape)
        off += esizes[n]
        size = W[n].shape[axis]
        grads[n] = lax.dynamic_slice_in_dim(g_full, chip * size, size, axis=axis)
        deltas[n], new_m[n], new_v[n] = adamw(W[n], grads[n], M1[n], V2[n], f"adamw_{n}")

    dm_all = parts.reshape(N_DEV, -1)[:, :DEPTH * E].reshape(N_DEV, DEPTH, E)
    dm_mine = jnp.swapaxes(lax.dynamic_slice_in_dim(dm_all, chip * Es, Es, axis=2), 0, 1)
    grads["mod_w"], deltas["mod_w"], new_m["mod_w"], new_v["mod_w"] = adamw_modw(
        c_all.reshape(N_DEV, D, 1), dm_mine, mod_w, m_mod_w, v_mod_w, "adamw_mod_w")

    def finish_reduce(pieces, axes, got, tag):
        out = []
        for i, (sm, ax, g4) in enumerate(zip(pieces, axes, got)):
            size = sm.shape[ax] // N_CHIPS
            g4 = lax.dynamic_update_index_in_dim(g4, lax.dynamic_slice_in_dim(sm, chip * size, size, axis=ax), chip, 0)
            blk = g4.shape[1:]
            out.append(sum_parts(g4.reshape(N_CHIPS, -1, blk[-1]), f"reduce_chips_add_{tag}{i}").reshape(blk))
        return out

    e_units, e_send, e_recv, e_lands, e_pieces, e_axes = early
    early_got = scatter_wait(e_send, e_recv, e_lands, e_pieces, dx, "reduce_chips_wait")
    reduced = finish_reduce(e_pieces, e_axes, early_got, "early")
    late_got = scatter_wait(l_send, l_recv, l_lands, l_pieces, new_v["mod_w"], "reduce_chips_late_wait")
    reduced += finish_reduce(l_pieces, l_axes, late_got, "late")
    units = e_units + l_units
    slots, out_shapes = [], []
    for n in EXCHANGED:
        mine = [i for i, u in enumerate(units) if u[0] == n]
        out_shapes.append((len(mine), 2) + reduced[mine[0]].shape)
        slots += [(len(out_shapes) - 1, units[i][1]) for i in mine]
    order = [i for n in EXCHANGED for i, u in enumerate(units) if u[0] == n]
    halves = gather_halves([reduced[i] for i in order], slots, out_shapes, "reduce_gather")
    for ti, n in enumerate(EXCHANGED):
        g = halves[ti]
        for i, u in enumerate(units):
            if u[0] == n:
                g = lax.dynamic_update_slice(g, reduced[i][None, None], (u[1], ci) + (0,) * reduced[i].ndim)
        grads[n] = g.reshape(W[n].shape)
        deltas[n], new_m[n], new_v[n] = adamw(W[n], grads[n], M1[n], V2[n], f"adamw_{n}")

    return (loss, dx.reshape(x.shape), *[grads[n] for n in WEIGHT_ORDER], *[deltas[n] for n in WEIGHT_ORDER],
            *[new_m[n] for n in WEIGHT_ORDER], *[new_v[n] for n in WEIGHT_ORDER])
```

```python
import functools
import math

import jax
import jax.numpy as jnp
from jax import lax
from jax.experimental import pallas as pl
from jax.experimental.pallas import tpu as pltpu

F32 = jnp.float32
BF16 = jnp.bfloat16
MESH = pl.DeviceIdType.MESH

DEPTH = 4
N_A = 2
POOL_WINDOWS = (2, 4, 8, 16)
N_GROUPS = 4
N_HEADS = 8
QK_NOPE = 128
QK_ROPE = 64
V_HEAD = 128
QK_HEAD = QK_NOPE + QK_ROPE
HEAD_PAD = 256
KV_RANK = 256
ROPE_THETA = 10000.0
EPS = 1e-6
ADAM_LR = 0.001
ADAM_B1 = 0.9
ADAM_B2 = 0.999
ADAM_EPS = 1e-08
ADAM_WD = 0.01
ADAM_STEP = 10

N_CHIPS = 4
N_DEV = 8
LANES = 128
PACK_COLS = 1024
VMEM_LIMIT = 56 * 1024 * 1024
GLU_TILE = 256
ATT_BWD_K_BLOCK = 512
ATT_BWD_Q_BLOCK = 512
ATT_Q_BLOCK = 1024
ATT_K_BLOCK = 512
ATT_HEADS_PER_STEP = 2


def _cparams(*sem):
    return pltpu.CompilerParams(dimension_semantics=sem if sem else None, vmem_limit_bytes=VMEM_LIMIT)


def _pick(n, target, mult):
    best = None
    d = mult
    while d <= min(n, target):
        if n % d == 0:
            best = d
        d += mult
    return n if best is None else best


def _row(v):
    return v.reshape(1, -1).astype(F32)


_DIMS = {"nn": (((1,), (0,)), ((), ())), "nt": (((1,), (1,)), ((), ())), "tn": (((0,), (0,)), ((), ()))}


def _mm_body(mode, nk, has_bias, has_res):
    def body(*refs):
        a_ref, b_ref = refs[0], refs[1]
        pos = 2
        bias_ref = res_ref = cs_ref = None
        if has_bias:
            bias_ref = refs[pos]
            pos += 1
        if has_res:
            res_ref, cs_ref = refs[pos], refs[pos + 1]
            pos += 2
        o_ref = refs[pos]
        pos += 1
        o2_ref = None
        if has_res:
            o2_ref = refs[pos]
            pos += 1
        acc_ref = refs[pos] if nk > 1 else None
        k = pl.program_id(2)
        part = lax.dot_general(a_ref[...].astype(BF16), b_ref[...].astype(BF16), _DIMS[mode],
                               preferred_element_type=F32)

        def finish(y):
            if has_bias:
                y = y + bias_ref[...]
            o_ref[...] = y.astype(o_ref.dtype)
            if has_res:
                o2_ref[...] = res_ref[...] + cs_ref[...] * y

        if nk == 1:
            finish(part)
            return

        @pl.when(k == 0)
        def _():
            acc_ref[...] = part

        @pl.when((k > 0) & (k < nk - 1))
        def _():
            acc_ref[...] += part

        @pl.when(k == nk - 1)
        def _():
            finish(acc_ref[...] + part)

    return body


def mm(a, b, mode, out_dtype, name, *, tm=1408, tn=1408, tk=1024, bias=None, res=None, colscale=None, layer=None):
    bshape = b.shape if layer is None else b.shape[1:]
    if mode == "nn":
        (M, K), N = a.shape, bshape[1]
    elif mode == "nt":
        (M, K), N = a.shape, bshape[0]
    else:
        (K, M), N = a.shape, bshape[1]
    tm = _pick(M, tm, LANES if mode == "tn" else 8)
    tn = _pick(N, tn, LANES)
    tk = _pick(K, tk, LANES) if mode != "tn" else _pick(K, 2 * tk, 8)
    nk = K // tk
    a_spec = {"nn": pl.BlockSpec((tm, tk), lambda i, j, k: (i, k)),
              "nt": pl.BlockSpec((tm, tk), lambda i, j, k: (i, k)),
              "tn": pl.BlockSpec((tk, tm), lambda i, j, k: (k, i))}[mode]
    b_blk, b_map = {"nn": ((tk, tn), lambda i, j, k: (k, j)),
                    "nt": ((tn, tk), lambda i, j, k: (j, k)),
                    "tn": ((tk, tn), lambda i, j, k: (k, j))}[mode]
    if layer is None:
        b_spec = pl.BlockSpec(b_blk, b_map)
    else:
        b_spec = pl.BlockSpec((None,) + b_blk, lambda i, j, k: (layer,) + b_map(i, j, k))
    o_spec = pl.BlockSpec((tm, tn), lambda i, j, k: (i, j))
    v_spec = pl.BlockSpec((1, tn), lambda i, j, k: (0, j))
    in_specs, args = [a_spec, b_spec], [a, b]
    if bias is not None:
        in_specs.append(v_spec)
        args.append(_row(bias))
    out_shape = [jax.ShapeDtypeStruct((M, N), out_dtype)]
    out_specs = [o_spec]
    if res is not None:
        in_specs += [o_spec, v_spec]
        args += [res, _row(colscale)]
        out_shape.append(jax.ShapeDtypeStruct((M, N), F32))
        out_specs.append(o_spec)
    outs = pl.pallas_call(
        _mm_body(mode, nk, bias is not None, res is not None),
        grid=(M // tm, N // tn, nk),
        in_specs=in_specs, out_specs=out_specs, out_shape=out_shape,
        scratch_shapes=[pltpu.VMEM((tm, tn), F32)] if nk > 1 else [],
        compiler_params=_cparams("parallel", "parallel", "arbitrary"),
        name=name,
    )(*args)
    return outs if res is not None else outs[0]


def gmm(a, w, mode, out_dtype, name, *, bias=None, res=None, colscale=None, tr=2048):
    S_ = a.shape[0]
    G = N_GROUPS
    C = a.shape[1] // G
    tr = _pick(S_, tr, 8)
    nr = S_ // tr
    if mode == "tn":
        def body(a_ref, b_ref, o_ref, acc_ref):
            i = pl.program_id(1)

            @pl.when(i == 0)
            def _():
                acc_ref[...] = jnp.zeros_like(acc_ref)

            acc_ref[...] += lax.dot_general(a_ref[...].astype(BF16), b_ref[...].astype(BF16), _DIMS["tn"],
                                            preferred_element_type=F32)

            @pl.when(i == nr - 1)
            def _():
                o_ref[...] = acc_ref[...].astype(o_ref.dtype)

        blk = pl.BlockSpec((tr, C), lambda g, i: (i, g))
        return pl.pallas_call(
            body, grid=(G, nr), in_specs=[blk, blk],
            out_specs=pl.BlockSpec((None, C, C), lambda g, i: (g, 0, 0)),
            out_shape=jax.ShapeDtypeStruct((G, C, C), out_dtype),
            scratch_shapes=[pltpu.VMEM((C, C), F32)],
            compiler_params=_cparams("parallel", "arbitrary"), name=name,
        )(a, w)

    has_bias, has_res = bias is not None, res is not None

    def body(*refs):
        a_ref, w_ref = refs[0], refs[1]
        pos = 2
        if has_bias:
            bias_ref = refs[pos]
            pos += 1
        if has_res:
            res_ref, cs_ref = refs[pos], refs[pos + 1]
            pos += 2
        o_ref = refs[pos]
        y = lax.dot_general(a_ref[...].astype(BF16), w_ref[...].astype(BF16), _DIMS[mode],
                            preferred_element_type=F32)
        if has_bias:
            y = y + bias_ref[...]
        o_ref[...] = y.astype(o_ref.dtype)
        if has_res:
            refs[pos + 1][...] = res_ref[...] + cs_ref[...] * y

    blk = pl.BlockSpec((tr, C), lambda i, g: (i, g))
    vec = pl.BlockSpec((1, C), lambda i, g: (0, g))
    in_specs = [blk, pl.BlockSpec((None, C, C), lambda i, g: (g, 0, 0))]
    args = [a, w]
    if has_bias:
        in_specs.append(vec)
        args.append(_row(bias))
    out_shape = [jax.ShapeDtypeStruct(a.shape, out_dtype)]
    out_specs = [blk]
    if has_res:
        in_specs += [blk, vec]
        args += [res, _row(colscale)]
        out_shape.append(jax.ShapeDtypeStruct(a.shape, F32))
        out_specs.append(blk)
    outs = pl.pallas_call(
        body, grid=(nr, G), in_specs=in_specs, out_specs=out_specs, out_shape=out_shape,
        compiler_params=_cparams("parallel", "parallel"), name=name,
    )(*args)
    return outs if has_res else outs[0]


def norm_fwd(x, g, sc, sh, out_dtype, name, tr=1024):
    S_, Dn = x.shape
    tr = _pick(S_, tr, 8)

    def body(x_ref, g_ref, sc_ref, sh_ref, o_ref):
        xv = x_ref[...]
        r = lax.rsqrt(jnp.mean(xv * xv, axis=-1, keepdims=True) + EPS)
        o_ref[...] = (((xv * r) * g_ref[...]) * (1.0 + sc_ref[...]) + sh_ref[...]).astype(o_ref.dtype)

    blk = pl.BlockSpec((tr, Dn), lambda i: (i, 0))
    vec = pl.BlockSpec((1, Dn), lambda i: (0, 0))
    return pl.pallas_call(
        body, grid=(S_ // tr,), in_specs=[blk, vec, vec, vec], out_specs=blk,
        out_shape=jax.ShapeDtypeStruct((S_, Dn), out_dtype),
        compiler_params=_cparams("parallel"), name=name,
    )(x, _row(g), _row(sc), _row(sh))


def norm_bwd(x, g, sc, dh, dres, name, gate=None, tr=1024):
    S_, Dn = x.shape
    tr = _pick(S_, tr, 8)
    has_res = dres is not None
    has_gate = gate is not None

    def body(*refs):
        x_ref, g_ref, sc_ref, dh_ref = refs[:4]
        pos = 4
        if has_res:
            dres_ref = refs[pos]
            pos += 1
        if has_gate:
            y_ref, cs_ref = refs[pos:pos + 2]
            pos += 2
        dx_ref, s1_ref, s2_ref = refs[pos:pos + 3]
        if has_gate:
            d_ref, a_ref, c_ref = refs[pos + 3:pos + 6]
        i = pl.program_id(0)

        @pl.when(i == 0)
        def _():
            s1_ref[...] = jnp.zeros_like(s1_ref)
            s2_ref[...] = jnp.zeros_like(s2_ref)
            if has_gate:
                a_ref[...] = jnp.zeros_like(a_ref)
                c_ref[...] = jnp.zeros_like(c_ref)

        xv = x_ref[...]
        r = lax.rsqrt(jnp.mean(xv * xv, axis=-1, keepdims=True) + EPS)
        n = xv * r
        dhv = dh_ref[...].astype(F32)
        dn = dhv * (g_ref[...] * (1.0 + sc_ref[...]))
        dx = r * (dn - n * jnp.mean(dn * n, axis=-1, keepdims=True))
        if has_res:
            dx = dx + dres_ref[...]
        dx_ref[...] = dx
        s1_ref[...] += jnp.sum(dhv, axis=0, keepdims=True)
        s2_ref[...] += jnp.sum(dhv * n, axis=0, keepdims=True)
        if has_gate:
            d_ref[...] = (dx * cs_ref[...]).astype(d_ref.dtype)
            a_ref[...] += jnp.sum(dx * y_ref[...].astype(F32), axis=0, keepdims=True)
            c_ref[...] += jnp.sum(dx, axis=0, keepdims=True)

    blk = pl.BlockSpec((tr, Dn), lambda i: (i, 0))
    vec = pl.BlockSpec((1, Dn), lambda i: (0, 0))
    in_specs, args = [blk, vec, vec, blk], [x, _row(g), _row(sc), dh]
    if has_res:
        in_specs.append(blk)
        args.append(dres)
    vshape = jax.ShapeDtypeStruct((1, Dn), F32)
    out_specs = [blk, vec, vec]
    out_shape = [jax.ShapeDtypeStruct((S_, Dn), F32), vshape, vshape]
    if has_gate:
        in_specs += [blk, vec]
        args += [gate[0], _row(gate[1])]
        out_specs += [blk, vec, vec]
        out_shape += [jax.ShapeDtypeStruct((S_, Dn), BF16), vshape, vshape]
    return pl.pallas_call(
        body, grid=(S_ // tr,), in_specs=in_specs, out_specs=out_specs, out_shape=out_shape,
        compiler_params=_cparams("arbitrary"), name=name,
    )(*args)


def gate_bwd(dx, y, colscale, name, tr=512):
    S_, Dn = dx.shape
    tr = _pick(S_, tr, 8)

    def body(dx_ref, y_ref, cs_ref, d_ref, a_ref, c_ref):
        i = pl.program_id(0)

        @pl.when(i == 0)
        def _():
            a_ref[...] = jnp.zeros_like(a_ref)
            c_ref[...] = jnp.zeros_like(c_ref)

        dxv = dx_ref[...]
        d_ref[...] = (dxv * cs_ref[...]).astype(d_ref.dtype)
        a_ref[...] += jnp.sum(dxv * y_ref[...].astype(F32), axis=0, keepdims=True)
        c_ref[...] += jnp.sum(dxv, axis=0, keepdims=True)

    blk = pl.BlockSpec((tr, Dn), lambda i: (i, 0))
    vec = pl.BlockSpec((1, Dn), lambda i: (0, 0))
    vshape = jax.ShapeDtypeStruct((1, Dn), F32)
    return pl.pallas_call(
        body, grid=(S_ // tr,), in_specs=[blk, blk, vec], out_specs=[blk, vec, vec],
        out_shape=[jax.ShapeDtypeStruct((S_, Dn), BF16), vshape, vshape],
        compiler_params=_cparams("arbitrary"), name=name,
    )(dx, y, _row(colscale))


def loss_head(x, g, target, name, tr=512):
    S_, Dn = x.shape
    tr = _pick(S_, tr, 8)

    def body(x_ref, g_ref, t_ref, dx_ref, dg_ref, loss_ref):
        i = pl.program_id(0)

        @pl.when(i == 0)
        def _():
            dg_ref[...] = jnp.zeros_like(dg_ref)
            loss_ref[...] = jnp.zeros_like(loss_ref)

        xv = x_ref[...]
        r = lax.rsqrt(jnp.mean(xv * xv, axis=-1, keepdims=True) + EPS)
        n = xv * r
        e = n * g_ref[...] - t_ref[...]
        loss_ref[...] += 0.5 * jnp.sum(jnp.mean(e * e, axis=-1, keepdims=True), axis=0, keepdims=True)
        dy = e * (1.0 / Dn)
        dg_ref[...] += jnp.sum(dy * n, axis=0, keepdims=True)
        dn = dy * g_ref[...]
        dx_ref[...] = r * (dn - n * jnp.mean(dn * n, axis=-1, keepdims=True))

    blk = pl.BlockSpec((tr, Dn), lambda i: (i, 0))
    vec = pl.BlockSpec((1, Dn), lambda i: (0, 0))
    one = pl.BlockSpec((1, 1), lambda i: (0, 0))
    return pl.pallas_call(
        body, grid=(S_ // tr,), in_specs=[blk, vec, blk], out_specs=[blk, vec, one],
        out_shape=[jax.ShapeDtypeStruct((S_, Dn), F32), jax.ShapeDtypeStruct((1, Dn), F32),
                   jax.ShapeDtypeStruct((1, 1), F32)],
        compiler_params=_cparams("arbitrary"), name=name,
    )(x, _row(g), target)


POOL_HALO = 16
POOL_CHUNK = 512


def _rows(ref, lo, hi, n_rows):
    parts = []
    if lo < 0:
        parts.append(jnp.zeros((-lo, ref.shape[1]), F32))
    parts.append(ref[max(lo, 0):min(hi, n_rows), :].astype(F32))
    if hi > n_rows:
        parts.append(jnp.zeros((hi - n_rows, ref.shape[1]), F32))
    return parts[0] if len(parts) == 1 else jnp.concatenate(parts, axis=0)


def _window_sum(e, w, back):
    n = e.shape[0]
    s, width = e, 1
    while width < w:
        s = s + pltpu.roll(s, width if back else n - width, 0)
        width *= 2
    return s


def _pool_call(h, out_dtype, name, backward):
    S_, Dn = h.shape
    C = Dn // N_GROUPS
    ch = _pick(S_, POOL_CHUNK, 8)

    def body(h_ref, o_ref):
        g = pl.program_id(0)
        for gi, w in enumerate(POOL_WINDOWS):
            @pl.when(g == gi)
            def _(w=w):
                for r0 in range(0, S_, ch):
                    t = (r0 + lax.broadcasted_iota(jnp.int32, (ch, C), 0)).astype(F32)
                    cnt = jnp.minimum(t + 1.0, float(w))
                    if not backward:
                        ext = _rows(h_ref, r0 - POOL_HALO, r0 + ch, S_)
                        cur = ext[POOL_HALO:]
                        mean = _window_sum(ext, w, True)[POOL_HALO:] / cnt
                        o_ref[r0:r0 + ch, :] = (mean - cur).astype(o_ref.dtype)
                    else:
                        ext = _rows(h_ref, r0, r0 + ch + POOL_HALO, S_)
                        text = (r0 + lax.broadcasted_iota(jnp.int32, (ch + POOL_HALO, C), 0)).astype(F32)
                        e = ext / jnp.minimum(text + 1.0, float(w))
                        o_ref[r0:r0 + ch, :] = (_window_sum(e, w, False)[:ch] - ext[:ch]).astype(o_ref.dtype)

    blk = pl.BlockSpec((S_, C), lambda g: (0, g))
    return pl.pallas_call(
        body, grid=(N_GROUPS,), in_specs=[blk], out_specs=blk,
        out_shape=jax.ShapeDtypeStruct((S_, Dn), out_dtype),
        compiler_params=_cparams("parallel"), name=name,
    )(h)


GLU_CHUNK = 512
GLU_HALO = 16
_SQRT_HALF = 0.7071067811865476
_INV_SQRT_2PI = 0.3989422804014327


def _gelu(a):
    return 0.5 * a * (1.0 + lax.erf(a * _SQRT_HALF))


def _gelu_grad(a):
    return 0.5 * (1.0 + lax.erf(a * _SQRT_HALF)) + a * (_INV_SQRT_2PI * jnp.exp(-0.5 * a * a))


def glu_fwd(u, conv_w, conv_b, name):
    S_, F2 = u.shape
    Fh = F2 // 2
    tf = GLU_TILE
    nt = Fh // tf
    ch = _pick(S_, GLU_CHUNK, GLU_HALO)

    def body(a_ref, v_ref, cw_ref, cb_ref, z_ref):
        cw0, cw1, cw2 = cw_ref[0:1, :], cw_ref[1:2, :], cw_ref[2:3, :]
        cb = cb_ref[...]
        for r0 in range(0, S_, ch):
            ext = _rows(a_ref, r0 - GLU_HALO, r0 + ch, S_)
            a0 = ext[GLU_HALO:]
            a1 = pltpu.roll(ext, 1, 0)[GLU_HALO:]
            a2 = pltpu.roll(ext, 2, 0)[GLU_HALO:]
            ac = a2 * cw0 + a1 * cw1 + a0 * cw2 + cb
            z_ref[r0:r0 + ch, :] = (_gelu(ac) * v_ref[r0:r0 + ch, :].astype(F32)).astype(z_ref.dtype)

    return pl.pallas_call(
        body, grid=(nt,),
        in_specs=[pl.BlockSpec((S_, tf), lambda j: (0, j)), pl.BlockSpec((S_, tf), lambda j: (0, j + nt)),
                  pl.BlockSpec((3, tf), lambda j: (0, j)), pl.BlockSpec((1, tf), lambda j: (0, j))],
        out_specs=pl.BlockSpec((S_, tf), lambda j: (0, j)),
        out_shape=jax.ShapeDtypeStruct((S_, Fh), BF16),
        compiler_params=_cparams("parallel"), name=name,
    )(u, u, conv_w, _row(conv_b))


def glu_bwd(u, dz, conv_w, conv_b, name):
    S_, F2 = u.shape
    Fh = F2 // 2
    tf = GLU_TILE
    nt = Fh // tf
    ch = _pick(S_, GLU_CHUNK, GLU_HALO)

    def body(a_ref, v_ref, dz_ref, cw_ref, cb_ref, du_ref, dcw_ref, dcb_ref, da_buf, dv_buf, sems):
        j = pl.program_id(0)
        slot = j % 2

        def writes(step, sl):
            lo = pl.multiple_of(step * tf, tf)
            return (pltpu.make_async_copy(da_buf.at[sl], du_ref.at[:, pl.ds(lo, tf)], sems.at[sl, 0]),
                    pltpu.make_async_copy(dv_buf.at[sl], du_ref.at[:, pl.ds(Fh + lo, tf)], sems.at[sl, 1]))

        @pl.when(j >= 2)
        def _():
            for cp in writes(j - 2, slot):
                cp.wait()

        cw0, cw1, cw2 = cw_ref[0:1, :], cw_ref[1:2, :], cw_ref[2:3, :]
        cb = cb_ref[...]
        acc = [jnp.zeros((1, tf), F32) for _ in range(4)]
        n = ch + GLU_HALO
        for r0 in range(0, S_, ch):
            ext = _rows(a_ref, r0 - GLU_HALO, r0 + n, S_)
            a0 = ext[GLU_HALO:]
            a1 = pltpu.roll(ext, 1, 0)[GLU_HALO:]
            a2 = pltpu.roll(ext, 2, 0)[GLU_HALO:]
            ac = a2 * cw0 + a1 * cw1 + a0 * cw2 + cb
            vv = _rows(v_ref, r0, r0 + n, S_)
            dzv = _rows(dz_ref, r0, r0 + n, S_)
            gl = _gelu(ac)
            dac = dzv * vv * _gelu_grad(ac)
            da = (dac * cw2 + pltpu.roll(dac, n - 1, 0) * cw1 + pltpu.roll(dac, n - 2, 0) * cw0)[:ch]
            da_buf[slot, r0:r0 + ch, :] = da.astype(da_buf.dtype)
            dv_buf[slot, r0:r0 + ch, :] = (dzv[:ch] * gl[:ch]).astype(dv_buf.dtype)
            dc = dac[:ch]
            acc[0] = acc[0] + jnp.sum(dc * a2[:ch], axis=0, keepdims=True)
            acc[1] = acc[1] + jnp.sum(dc * a1[:ch], axis=0, keepdims=True)
            acc[2] = acc[2] + jnp.sum(dc * a0[:ch], axis=0, keepdims=True)
            acc[3] = acc[3] + jnp.sum(dc, axis=0, keepdims=True)
        dcw_ref[0:1, :] = acc[0]
        dcw_ref[1:2, :] = acc[1]
        dcw_ref[2:3, :] = acc[2]
        dcb_ref[...] = acc[3]
        for cp in writes(j, slot):
            cp.start()

        @pl.when(j == nt - 1)
        def _():
            for cp in writes(j, slot):
                cp.wait()
            if nt > 1:
                for cp in writes(j - 1, 1 - slot):
                    cp.wait()

    return pl.pallas_call(
        body, grid=(nt,),
        in_specs=[pl.BlockSpec((S_, tf), lambda j: (0, j)), pl.BlockSpec((S_, tf), lambda j: (0, j + nt)),
                  pl.BlockSpec((S_, tf), lambda j: (0, j)),
                  pl.BlockSpec((3, tf), lambda j: (0, j)), pl.BlockSpec((1, tf), lambda j: (0, j))],
        out_specs=[_ANY, pl.BlockSpec((3, tf), lambda j: (0, j)), pl.BlockSpec((1, tf), lambda j: (0, j))],
        out_shape=[jax.ShapeDtypeStruct((S_, F2), BF16), jax.ShapeDtypeStruct((3, Fh), F32),
                   jax.ShapeDtypeStruct((1, Fh), F32)],
        scratch_shapes=[pltpu.VMEM((2, S_, tf), BF16), pltpu.VMEM((2, S_, tf), BF16), pltpu.SemaphoreType.DMA((2, 2))],
        compiler_params=_cparams("arbitrary"), name=name,
    )(u, u, dz, conv_w, _row(conv_b))


def rope_tables(pos, inv, name, tr=512):
    S_ = pos.shape[0]
    tr = _pick(S_, tr, 8)

    def body(p_ref, inv_ref, c_ref, s1_ref, s2_ref):
        ang = p_ref[...] * inv_ref[...]
        lane = lax.broadcasted_iota(jnp.int32, ang.shape, 1)
        half = QK_ROPE // 2
        cosv, sinv = jnp.cos(ang), jnp.sin(ang)
        c_ref[...] = jnp.where(lane < QK_ROPE, cosv, 0.0)
        s1_ref[...] = jnp.where(lane < half, -sinv, 0.0)
        s2_ref[...] = jnp.where((lane >= half) & (lane < QK_ROPE), sinv, 0.0)

    blk = pl.BlockSpec((tr, LANES), lambda i: (i, 0))
    shp = jax.ShapeDtypeStruct((S_, LANES), F32)
    return pl.pallas_call(
        body, grid=(S_ // tr,),
        in_specs=[pl.BlockSpec((tr, 1), lambda i: (i, 0)), pl.BlockSpec((1, LANES), lambda i: (0, 0))],
        out_specs=[blk, blk, blk], out_shape=[shp, shp, shp],
        compiler_params=_cparams("parallel"), name=name,
    )(pos, inv)


_HALF = QK_ROPE // 2


def _rope(t, c, s1, s2):
    return t * c + pltpu.roll(t, LANES - _HALF, 1) * s1 + pltpu.roll(t, _HALF, 1) * s2


def _rope_t(d, c, s1, s2):
    return d * c + pltpu.roll(d * s1, _HALF, 1) + pltpu.roll(d * s2, LANES - _HALF, 1)


def q_prep(q, tabs, scale, backward, name, tr=512):
    S_, W = q.shape
    tr = _pick(S_, tr, 8)

    def body(q_ref, c_ref, s1_ref, s2_ref, o_ref):
        o_ref[:, 0:LANES] = (q_ref[:, 0:LANES].astype(F32) * scale).astype(o_ref.dtype)
        t = q_ref[:, LANES:2 * LANES].astype(F32)
        fn = _rope_t if backward else _rope
        o_ref[:, LANES:2 * LANES] = (fn(t, c_ref[...], s1_ref[...], s2_ref[...]) * scale).astype(o_ref.dtype)

    blk = pl.BlockSpec((tr, HEAD_PAD), lambda i, h: (i, h))
    tab = pl.BlockSpec((tr, LANES), lambda i, h: (i, 0))
    return pl.pallas_call(
        body, grid=(S_ // tr, W // HEAD_PAD), in_specs=[blk, tab, tab, tab], out_specs=blk,
        out_shape=jax.ShapeDtypeStruct((S_, W), BF16),
        compiler_params=_cparams("parallel", "parallel"), name=name,
    )(q, *tabs)


def k_prep(knv, kv_ext, tabs, name, tr=512):
    S_ = knv.shape[0]
    tr = _pick(S_, tr, 8)

    def body(kn_ref, v_ref, t_ref, c_ref, s1_ref, s2_ref, o_ref, vx_ref):
        o_ref[:, 0:LANES] = kn_ref[...].astype(o_ref.dtype)
        o_ref[:, LANES:2 * LANES] = _rope(t_ref[...], c_ref[...], s1_ref[...], s2_ref[...]).astype(o_ref.dtype)
        vx_ref[:, 0:V_HEAD] = v_ref[...].astype(vx_ref.dtype)
        vx_ref[:, V_HEAD:HEAD_PAD] = jnp.ones((tr, HEAD_PAD - V_HEAD), vx_ref.dtype)

    tab = pl.BlockSpec((tr, LANES), lambda i, h: (i, 0))
    head = pl.BlockSpec((tr, HEAD_PAD), lambda i, h: (i, h))
    shp = jax.ShapeDtypeStruct((S_, N_HEADS * HEAD_PAD), BF16)
    return pl.pallas_call(
        body, grid=(S_ // tr, N_HEADS),
        in_specs=[pl.BlockSpec((tr, LANES), lambda i, h: (i, h)),
                  pl.BlockSpec((tr, V_HEAD), lambda i, h: (i, N_HEADS + h)),
                  pl.BlockSpec((tr, LANES), lambda i, h: (i, KV_RANK // LANES)), tab, tab, tab],
        out_specs=[head, head], out_shape=[shp, shp],
        compiler_params=_cparams("parallel", "parallel"), name=name,
    )(knv, knv, kv_ext, *tabs)


def k_prep_bwd(dk_a, dk_b, dv_a, dv_b, tabs, name, tr=512):
    S_ = dk_a.shape[0]
    tr = _pick(S_, tr, 8)
    HV = N_HEADS * V_HEAD

    def body(ka_ref, kb_ref, va_ref, vb_ref, c_ref, s1_ref, s2_ref, o_ref, t_ref):
        dr = jnp.zeros((tr, LANES), F32)
        for h in range(N_HEADS):
            lo = h * HEAD_PAD
            o_ref[:, h * LANES:(h + 1) * LANES] = (ka_ref[:, lo:lo + LANES] + kb_ref[:, lo:lo + LANES]).astype(o_ref.dtype)
            dr = dr + ka_ref[:, lo + LANES:lo + 2 * LANES] + kb_ref[:, lo + LANES:lo + 2 * LANES]
        o_ref[:, HV:2 * HV] = (va_ref[...] + vb_ref[...]).astype(o_ref.dtype)
        t_ref[...] = _rope_t(dr, c_ref[...], s1_ref[...], s2_ref[...])

    kblk = pl.BlockSpec((tr, N_HEADS * HEAD_PAD), lambda i: (i, 0))
    vblk = pl.BlockSpec((tr, HV), lambda i: (i, 0))
    tab = pl.BlockSpec((tr, LANES), lambda i: (i, 0))
    return pl.pallas_call(
        body, grid=(S_ // tr,), in_specs=[kblk, kblk, vblk, vblk, tab, tab, tab],
        out_specs=[pl.BlockSpec((tr, 2 * HV), lambda i: (i, 0)), tab],
        out_shape=[jax.ShapeDtypeStruct((S_, 2 * HV), BF16), jax.ShapeDtypeStruct((S_, LANES), F32)],
        compiler_params=_cparams("parallel"), name=name,
    )(dk_a, dk_b, dv_a, dv_b, *tabs)


_NEG = -1e30


def attn_fwd(q, k, vx, name):
    S_ = q.shape[0]
    TQ = _pick(S_, ATT_Q_BLOCK, 8)
    TK = _pick(S_, ATT_K_BLOCK, 8)
    assert TQ % TK == 0 or TK % TQ == 0
    HP = ATT_HEADS_PER_STEP
    W = HP * HEAD_PAD

    def body(q_ref, k_ref, v_ref, o_ref, lse_ref):
        i = pl.program_id(1)
        qs = [q_ref[:, h * HEAD_PAD:(h + 1) * HEAD_PAD] for h in range(HP)]

        def step(j, carry, masked):
            start = pl.multiple_of(j * TK, TK)
            out = []
            for h in range(HP):
                m, acc = carry[h]
                cols = slice(h * HEAD_PAD, (h + 1) * HEAD_PAD)
                s = lax.dot_general(qs[h], k_ref[pl.ds(start, TK), cols], _DIMS["nt"], preferred_element_type=F32)
                if masked:
                    rowi = i * TQ + lax.broadcasted_iota(jnp.int32, (TQ, TK), 0)
                    coli = j * TK + lax.broadcasted_iota(jnp.int32, (TQ, TK), 1)
                    s = jnp.where(coli <= rowi, s, _NEG)
                m_new = jnp.maximum(m, jnp.max(s, axis=-1, keepdims=True))
                alpha = jnp.exp(m - m_new)
                p = jnp.exp(s - m_new).astype(BF16)
                acc = alpha * acc + lax.dot_general(p, v_ref[pl.ds(start, TK), cols], _DIMS["nn"],
                                                    preferred_element_type=F32)
                out.append((m_new, acc))
            return tuple(out)

        init = tuple((jnp.full((TQ, 1), _NEG, F32), jnp.zeros((TQ, HEAD_PAD), F32)) for _ in range(HP))
        n_full, n_diag = (i * (TQ // TK), TQ // TK) if TQ >= TK else (i // (TK // TQ), 1)
        carry = lax.fori_loop(0, n_full, functools.partial(step, masked=False), init)
        for d in range(n_diag):
            carry = step(n_full + d, carry, True)
        for h in range(HP):
            m, acc = carry[h]
            l = acc[:, V_HEAD:]
            o_ref[:, h * V_HEAD:(h + 1) * V_HEAD] = (acc[:, :V_HEAD] / l).astype(o_ref.dtype)
            lse_ref[h] = m + jnp.log(jnp.max(l, axis=-1, keepdims=True))

    return pl.pallas_call(
        body, grid=(N_HEADS // HP, S_ // TQ),
        in_specs=[pl.BlockSpec((TQ, W), lambda g, i: (i, g)),
                  pl.BlockSpec((S_, W), lambda g, i: (0, g)),
                  pl.BlockSpec((S_, W), lambda g, i: (0, g))],
        out_specs=[pl.BlockSpec((TQ, HP * V_HEAD), lambda g, i: (i, g)),
                   pl.BlockSpec((HP, TQ, 1), lambda g, i: (g, i, 0))],
        out_shape=[jax.ShapeDtypeStruct((S_, N_HEADS * V_HEAD), BF16), jax.ShapeDtypeStruct((N_HEADS, S_, 1), F32)],
        compiler_params=_cparams("parallel", "parallel"), name=name,
    )(q, k, vx)


def q_proj(cq, w_ext, tabs, scale, name, tm=4096):
    S_, R = cq.shape
    tm = _pick(S_, tm, 16)

    def body(c_ref, w_ref, t_c, t_s1, t_s2, o_ref):
        y = lax.dot_general(c_ref[...].astype(BF16), w_ref[...].astype(BF16), _DIMS["nn"], preferred_element_type=F32)
        o_ref[:, 0:LANES] = (y[:, 0:LANES] * scale).astype(o_ref.dtype)
        o_ref[:, LANES:2 * LANES] = (_rope(y[:, LANES:2 * LANES], t_c[...], t_s1[...], t_s2[...]) * scale).astype(o_ref.dtype)

    tab = pl.BlockSpec((tm, LANES), lambda i, h: (i, 0))
    return pl.pallas_call(
        body, grid=(S_ // tm, N_HEADS),
        in_specs=[pl.BlockSpec((tm, R), lambda i, h: (i, 0)), pl.BlockSpec((R, HEAD_PAD), lambda i, h: (0, h)),
                  tab, tab, tab],
        out_specs=pl.BlockSpec((tm, HEAD_PAD), lambda i, h: (i, h)),
        out_shape=jax.ShapeDtypeStruct((S_, N_HEADS * HEAD_PAD), BF16),
        compiler_params=_cparams("parallel", "parallel"), name=name,
    )(cq, w_ext, *tabs)


def q_proj_bwd(dq, cq, w_ext, tabs, scale, name, tm=4096):
    S_, R = cq.shape
    tm = _pick(S_, tm, 16)
    nr = S_ // tm

    def body(dq_ref, c_ref, w_ref, t_c, t_s1, t_s2, dc_ref, dw_ref, acc_ref):
        h, i = pl.program_id(0), pl.program_id(1)
        g = jnp.concatenate([dq_ref[:, 0:LANES] * scale,
                             _rope_t(dq_ref[:, LANES:2 * LANES], t_c[...], t_s1[...], t_s2[...]) * scale],
                            axis=1).astype(BF16)
        part = lax.dot_general(g, w_ref[...].astype(BF16), _DIMS["nt"], preferred_element_type=F32)
        rows = pl.ds(pl.multiple_of(i * tm, tm), tm)

        @pl.when(h == 0)
        def _():
            dc_ref[rows, :] = part

        @pl.when(h > 0)
        def _():
            dc_ref[rows, :] += part

        dwp = lax.dot_general(c_ref[...].astype(BF16), g, _DIMS["tn"], preferred_element_type=F32)

        @pl.when(i == 0)
        def _():
            acc_ref[...] = dwp

        @pl.when(i > 0)
        def _():
            acc_ref[...] += dwp

        @pl.when(i == nr - 1)
        def _():
            dw_ref[...] = acc_ref[...].astype(dw_ref.dtype)

    tab = pl.BlockSpec((tm, LANES), lambda h, i: (i, 0))
    return pl.pallas_call(
        body, grid=(N_HEADS, nr),
        in_specs=[pl.BlockSpec((tm, HEAD_PAD), lambda h, i: (i, h)), pl.BlockSpec((tm, R), lambda h, i: (i, 0)),
                  pl.BlockSpec((R, HEAD_PAD), lambda h, i: (0, h)), tab, tab, tab],
        out_specs=[pl.BlockSpec((S_, R), lambda h, i: (0, 0)), pl.BlockSpec((R, HEAD_PAD), lambda h, i: (0, h))],
        out_shape=[jax.ShapeDtypeStruct((S_, R), F32), jax.ShapeDtypeStruct((R, N_HEADS * HEAD_PAD), BF16)],
        scratch_shapes=[pltpu.VMEM((R, HEAD_PAD), F32)],
        compiler_params=_cparams("arbitrary", "arbitrary"), name=name,
    )(dq, cq, w_ext, *tabs)


def kv_proj(ckv, w_ukv, kv_ext, tabs, name, tm=2048):
    S_, R = ckv.shape
    tm = _pick(S_, tm, 16)

    def body(c_ref, wk_ref, wv_ref, t_ref, t_c, t_s1, t_s2, k_ref, vx_ref):
        cv = c_ref[...].astype(BF16)
        k_ref[:, 0:LANES] = lax.dot_general(cv, wk_ref[...].astype(BF16), _DIMS["nn"],
                                            preferred_element_type=F32).astype(k_ref.dtype)
        k_ref[:, LANES:2 * LANES] = _rope(t_ref[...], t_c[...], t_s1[...], t_s2[...]).astype(k_ref.dtype)
        vx_ref[:, 0:V_HEAD] = lax.dot_general(cv, wv_ref[...].astype(BF16), _DIMS["nn"],
                                              preferred_element_type=F32).astype(vx_ref.dtype)
        vx_ref[:, V_HEAD:HEAD_PAD] = jnp.ones((tm, HEAD_PAD - V_HEAD), vx_ref.dtype)

    tab = pl.BlockSpec((tm, LANES), lambda i, h: (i, 0))
    head = pl.BlockSpec((tm, HEAD_PAD), lambda i, h: (i, h))
    shp = jax.ShapeDtypeStruct((S_, N_HEADS * HEAD_PAD), BF16)
    return pl.pallas_call(
        body, grid=(S_ // tm, N_HEADS),
        in_specs=[pl.BlockSpec((tm, R), lambda i, h: (i, 0)), pl.BlockSpec((R, QK_NOPE), lambda i, h: (0, h)),
                  pl.BlockSpec((R, V_HEAD), lambda i, h: (0, N_HEADS + h)),
                  pl.BlockSpec((tm, LANES), lambda i, h: (i, KV_RANK // LANES)), tab, tab, tab],
        out_specs=[head, head], out_shape=[shp, shp],
        compiler_params=_cparams("parallel", "parallel"), name=name,
    )(ckv, w_ukv, w_ukv, kv_ext, *tabs)


def o_proj_bwd(dy, w_o, o, name, tm=2048):
    S_, Dn = dy.shape
    HV = w_o.shape[0]
    tm = _pick(S_, tm, 16)

    def body(dy_ref, w_ref, o_ref, do_ref, d_ref):
        do = lax.dot_general(dy_ref[...].astype(BF16), w_ref[...].astype(BF16), _DIMS["nt"], preferred_element_type=F32)
        do_ref[...] = do.astype(do_ref.dtype)
        prod = do * o_ref[...].astype(F32)
        for h in range(N_HEADS):
            d_ref[h] = jnp.sum(prod[:, h * V_HEAD:(h + 1) * V_HEAD], axis=-1, keepdims=True)

    return pl.pallas_call(
        body, grid=(S_ // tm,),
        in_specs=[pl.BlockSpec((tm, Dn), lambda i: (i, 0)), pl.BlockSpec((HV, Dn), lambda i: (0, 0)),
                  pl.BlockSpec((tm, HV), lambda i: (i, 0))],
        out_specs=[pl.BlockSpec((tm, HV), lambda i: (i, 0)), pl.BlockSpec((N_HEADS, tm, 1), lambda i: (0, i, 0))],
        out_shape=[jax.ShapeDtypeStruct((S_, HV), BF16), jax.ShapeDtypeStruct((N_HEADS, S_, 1), F32)],
        compiler_params=_cparams("parallel"), name=name,
    )(dy, w_o, o)


def attn_delta(o, do, name, tr=512):
    S_ = o.shape[0]
    tr = _pick(S_, tr, 8)

    def body(o_ref, do_ref, d_ref):
        d_ref[...] = jnp.sum(o_ref[...].astype(F32) * do_ref[...].astype(F32), axis=-1, keepdims=True)

    blk = pl.BlockSpec((tr, V_HEAD), lambda i, h: (i, h))
    return pl.pallas_call(
        body, grid=(S_ // tr, N_HEADS), in_specs=[blk, blk],
        out_specs=pl.BlockSpec((None, tr, 1), lambda i, h: (h, i, 0)),
        out_shape=jax.ShapeDtypeStruct((N_HEADS, S_, 1), F32),
        compiler_params=_cparams("parallel", "parallel"), name=name,
    )(o, do)


def attn_bwd(q, k, vx, do, lse_row, delta_row, name):
    S_ = q.shape[0]
    TK = _pick(S_, ATT_BWD_K_BLOCK, LANES)
    TQ = _pick(S_, ATT_BWD_Q_BLOCK, TK)
    HP = ATT_HEADS_PER_STEP
    W = HP * HEAD_PAD
    ratio = TQ // TK
    nq = S_ // TQ

    def body(q_ref, do_ref, lse_ref, dl_ref, k_ref, v_ref, dq_ref, dk_ref, dv_ref):
        j = pl.program_id(1)

        @pl.when(j == 0)
        def _():
            dq_ref[...] = jnp.zeros_like(dq_ref)

        ks = [k_ref[:, h * HEAD_PAD:(h + 1) * HEAD_PAD] for h in range(HP)]
        vs = [v_ref[:, h * HEAD_PAD:h * HEAD_PAD + V_HEAD] for h in range(HP)]

        def step(i, carry, masked):
            start = pl.multiple_of(i * TQ, TQ)
            out = []
            for h in range(HP):
                dk, dv = carry[h]
                cols = slice(h * HEAD_PAD, (h + 1) * HEAD_PAD)
                qv = q_ref[pl.ds(start, TQ), cols]
                dov = do_ref[pl.ds(start, TQ), h * V_HEAD:(h + 1) * V_HEAD]
                st = lax.dot_general(ks[h], qv, _DIMS["nt"], preferred_element_type=F32)
                pt = jnp.exp(st - lse_ref[h, :, pl.ds(start, TQ)])
                if masked:
                    keyi = j * TK + lax.broadcasted_iota(jnp.int32, (TK, TQ), 0)
                    qryi = i * TQ + lax.broadcasted_iota(jnp.int32, (TK, TQ), 1)
                    pt = jnp.where(keyi <= qryi, pt, 0.0)
                dpt = lax.dot_general(vs[h], dov, _DIMS["nt"], preferred_element_type=F32)
                dst = (pt * (dpt - dl_ref[h, :, pl.ds(start, TQ)])).astype(BF16)
                dv = dv + lax.dot_general(pt.astype(BF16), dov, _DIMS["nn"], preferred_element_type=F32)
                dk = dk + lax.dot_general(dst, qv, _DIMS["nn"], preferred_element_type=F32)
                dq_ref[pl.ds(start, TQ), cols] += lax.dot_general(dst, ks[h], _DIMS["tn"], preferred_element_type=F32)
                out.append((dk, dv))
            return tuple(out)

        init = tuple((jnp.zeros((TK, HEAD_PAD), F32), jnp.zeros((TK, V_HEAD), F32)) for _ in range(HP))
        first = j // ratio
        carry = lax.fori_loop(first + 1, nq, functools.partial(step, masked=False), step(first, init, True))
        for h in range(HP):
            dk_ref[:, h * HEAD_PAD:(h + 1) * HEAD_PAD] = carry[h][0]
            dv_ref[:, h * V_HEAD:(h + 1) * V_HEAD] = carry[h][1]

    return pl.pallas_call(
        body, grid=(N_HEADS // HP, S_ // TK),
        in_specs=[pl.BlockSpec((S_, W), lambda g, j: (0, g)),
                  pl.BlockSpec((S_, HP * V_HEAD), lambda g, j: (0, g)),
                  pl.BlockSpec((HP, 1, S_), lambda g, j: (g, 0, 0)),
                  pl.BlockSpec((HP, 1, S_), lambda g, j: (g, 0, 0)),
                  pl.BlockSpec((TK, W), lambda g, j: (j, g)),
                  pl.BlockSpec((TK, W), lambda g, j: (j, g))],
        out_specs=[pl.BlockSpec((S_, W), lambda g, j: (0, g)),
                   pl.BlockSpec((TK, W), lambda g, j: (j, g)),
                   pl.BlockSpec((TK, HP * V_HEAD), lambda g, j: (j, g))],
        out_shape=[jax.ShapeDtypeStruct((S_, N_HEADS * HEAD_PAD), F32),
                   jax.ShapeDtypeStruct((S_, N_HEADS * HEAD_PAD), F32),
                   jax.ShapeDtypeStruct((S_, N_HEADS * V_HEAD), F32)],
        compiler_params=_cparams("parallel", "arbitrary"), name=name,
    )(q, do, lse_row, delta_row, k, vx)


def mods_fwd(c_all, mod_w, mod_b, name, tn=512):
    L, Dn, E = mod_w.shape
    R = c_all.shape[0]
    tn = _pick(E, tn, LANES)

    def body(c_ref, w_ref, b_ref, o_ref):
        cv = c_ref[...]
        sc = (cv / (1.0 + jnp.exp(-cv))).astype(BF16)
        o_ref[...] = lax.dot_general(sc, w_ref[...].astype(BF16), _DIMS["nn"], preferred_element_type=F32) + b_ref[...]

    return pl.pallas_call(
        body, grid=(L, E // tn),
        in_specs=[pl.BlockSpec((R, Dn), lambda l, j: (0, 0)), pl.BlockSpec((None, Dn, tn), lambda l, j: (l, 0, j)),
                  pl.BlockSpec((None, 1, tn), lambda l, j: (l, 0, j))],
        out_specs=pl.BlockSpec((None, R, tn), lambda l, j: (l, 0, j)),
        out_shape=jax.ShapeDtypeStruct((L, R, E), F32),
        compiler_params=_cparams("parallel", "parallel"), name=name,
    )(c_all, mod_w, mod_b.reshape(L, 1, E))


def _adam_math(w, g, m, v):
    m = ADAM_B1 * m + (1.0 - ADAM_B1) * g
    v = ADAM_B2 * v + (1.0 - ADAM_B2) * (g * g)
    m_hat = m / (1.0 - ADAM_B1 ** ADAM_STEP)
    v_hat = v / (1.0 - ADAM_B2 ** ADAM_STEP)
    delta = -ADAM_LR * (m_hat / (jnp.sqrt(v_hat) + ADAM_EPS) + ADAM_WD * w)
    return delta, m, v


def _as2d(a):
    return a.reshape(-1, a.shape[-1]) if a.ndim != 2 else a


def adamw(w, g, m, v, name):
    shape = w.shape
    w2, g2, m2, v2 = _as2d(w), _as2d(g), _as2d(m), _as2d(v)
    R, C = w2.shape
    tr = _pick(R, max(8, (1 << 19) // C // 8 * 8), 8)

    def body(w_ref, g_ref, m_ref, v_ref, d_ref, mo_ref, vo_ref):
        d, mn, vn = _adam_math(w_ref[...], g_ref[...], m_ref[...], v_ref[...])
        d_ref[...] = d
        mo_ref[...] = mn
        vo_ref[...] = vn

    blk = pl.BlockSpec((tr, C), lambda i: (i, 0))
    shp = jax.ShapeDtypeStruct((R, C), F32)
    outs = pl.pallas_call(
        body, grid=(R // tr,), in_specs=[blk] * 4, out_specs=[blk] * 3, out_shape=[shp] * 3,
        compiler_params=_cparams("parallel"), name=name,
    )(w2, g2, m2, v2)
    return tuple(o.reshape(shape) for o in outs)


def adamw_sum(parts, w, m, v, name):
    P, R, C = parts.shape

    def body(p_ref, w_ref, m_ref, v_ref, g_ref, d_ref, mo_ref, vo_ref):
        g = p_ref[0]
        for k in range(1, P):
            g = g + p_ref[k]
        d, mn, vn = _adam_math(w_ref[...], g, m_ref[...], v_ref[...])
        g_ref[...] = g
        d_ref[...] = d
        mo_ref[...] = mn
        vo_ref[...] = vn

    shp = jax.ShapeDtypeStruct((R, C), F32)
    return pl.pallas_call(body, out_shape=[shp] * 4, compiler_params=_cparams(), name=name)(parts, w, m, v)


def adamw_modw(c_col, dm, w, m, v, name, tr=256, tn=1536):
    L, Dn, E = w.shape
    B = c_col.shape[0]
    tr = _pick(Dn, tr, 8)
    tn = _pick(E, tn, LANES)

    def body(c_ref, dm_ref, w_ref, m_ref, v_ref, g_ref, d_ref, mo_ref, vo_ref):
        g = jnp.zeros((tr, tn), F32)
        for b in range(B):
            cv = c_ref[b]
            g = g + (cv / (1.0 + jnp.exp(-cv))) * dm_ref[b:b + 1, :]
        d, mn, vn = _adam_math(w_ref[...], g, m_ref[...], v_ref[...])
        g_ref[...] = g
        d_ref[...] = d
        mo_ref[...] = mn
        vo_ref[...] = vn

    blk = pl.BlockSpec((None, tr, tn), lambda l, i, j: (l, i, j))
    shp = jax.ShapeDtypeStruct((L, Dn, E), F32)
    return pl.pallas_call(
        body, grid=(L, Dn // tr, E // tn),
        in_specs=[pl.BlockSpec((B, tr, 1), lambda l, i, j: (0, i, 0)),
                  pl.BlockSpec((None, B, tn), lambda l, i, j: (l, 0, j)), blk, blk, blk],
        out_specs=[blk] * 4, out_shape=[shp] * 4,
        compiler_params=_cparams("parallel", "parallel", "parallel"), name=name,
    )(c_col, dm, w, m, v)


def add_round(a, b, name, tr=512):
    R, C = a.shape
    tr = _pick(R, tr, 16)

    def body(a_ref, b_ref, o_ref):
        o_ref[...] = (a_ref[...] + b_ref[...].astype(F32)).astype(BF16)

    blk = pl.BlockSpec((tr, C), lambda i: (i, 0))
    return pl.pallas_call(
        body, grid=(R // tr,), in_specs=[blk, blk], out_specs=blk, out_shape=jax.ShapeDtypeStruct((R, C), BF16),
        compiler_params=_cparams("parallel"), name=name,
    )(a, b)


def sum_parts(parts, name, tr=512):
    P, R, C = parts.shape
    tr = _pick(R, tr, 16)

    def body(p_ref, o_ref):
        s = p_ref[0].astype(F32)
        for k in range(1, P):
            s = s + p_ref[k].astype(F32)
        o_ref[...] = s

    return pl.pallas_call(
        body, grid=(R // tr,), in_specs=[pl.BlockSpec((P, tr, C), lambda i: (0, i, 0))],
        out_specs=pl.BlockSpec((tr, C), lambda i: (i, 0)), out_shape=jax.ShapeDtypeStruct((R, C), F32),
        compiler_params=_cparams("parallel"), name=name,
    )(parts)


_ANY = pl.BlockSpec(memory_space=pl.ANY)


def _place():
    return lax.axis_index("x"), lax.axis_index("y"), lax.axis_index("c")


def _flip(v, bit):
    return 1 - v if bit else v


def chip_gather(buf, name):
    def body(in_ref, out_ref, send_sems, recv_sems):
        x, y, c = _place()
        me = 2 * x + y
        sends = []
        for k in range(1, N_CHIPS):
            px, py = _flip(x, k >> 1), _flip(y, k & 1)
            cp = pltpu.make_async_remote_copy(src_ref=in_ref, dst_ref=out_ref.at[me], send_sem=send_sems.at[k - 1],
                                              recv_sem=recv_sems.at[k - 1], device_id=(px, py, c), device_id_type=MESH)
            cp.start()
            sends.append(cp)
        for k in range(1, N_CHIPS):
            px, py = _flip(x, k >> 1), _flip(y, k & 1)
            pltpu.make_async_remote_copy(src_ref=in_ref, dst_ref=out_ref.at[2 * px + py], send_sem=send_sems.at[k - 1],
                                         recv_sem=recv_sems.at[k - 1], device_id=(px, py, c),
                                         device_id_type=MESH).wait_recv()
        for cp in sends:
            cp.wait_send()

    out = pl.pallas_call(
        body, in_specs=[_ANY], out_specs=_ANY,
        out_shape=jax.ShapeDtypeStruct((N_CHIPS,) + buf.shape, buf.dtype),
        scratch_shapes=[pltpu.SemaphoreType.DMA((N_CHIPS - 1,)), pltpu.SemaphoreType.DMA((N_CHIPS - 1,))],
        name=name,
    )(buf)
    return lax.dynamic_update_index_in_dim(out, buf, 2 * lax.axis_index("x") + lax.axis_index("y"), 0)


def chip_all_to_all(buf, name):
    def body(in_ref, out_ref, send_sems, recv_sems):
        x, y, c = _place()
        me = 2 * x + y
        sends = []
        for k in range(1, N_CHIPS):
            px, py = _flip(x, k >> 1), _flip(y, k & 1)
            cp = pltpu.make_async_remote_copy(src_ref=in_ref.at[2 * px + py], dst_ref=out_ref.at[me],
                                              send_sem=send_sems.at[k - 1], recv_sem=recv_sems.at[k - 1],
                                              device_id=(px, py, c), device_id_type=MESH)
            cp.start()
            sends.append(cp)
        for k in range(1, N_CHIPS):
            px, py = _flip(x, k >> 1), _flip(y, k & 1)
            pltpu.make_async_remote_copy(src_ref=in_ref.at[me], dst_ref=out_ref.at[2 * px + py],
                                         send_sem=send_sems.at[k - 1], recv_sem=recv_sems.at[k - 1],
                                         device_id=(px, py, c), device_id_type=MESH).wait_recv()
        for cp in sends:
            cp.wait_send()

    out = pl.pallas_call(
        body, in_specs=[_ANY], out_specs=_ANY, out_shape=jax.ShapeDtypeStruct(buf.shape, buf.dtype),
        scratch_shapes=[pltpu.SemaphoreType.DMA((N_CHIPS - 1,)), pltpu.SemaphoreType.DMA((N_CHIPS - 1,))],
        name=name,
    )(buf)
    me = 2 * lax.axis_index("x") + lax.axis_index("y")
    return lax.dynamic_update_index_in_dim(out, _index(buf, me), me, 0)


def core_gather(buf, name):
    def body(in_ref, out_ref, send_sem, recv_sem):
        x, y, c = _place()
        cp = pltpu.make_async_remote_copy(src_ref=in_ref, dst_ref=out_ref.at[c], send_sem=send_sem, recv_sem=recv_sem,
                                          device_id=(x, y, 1 - c), device_id_type=MESH)
        cp.start()
        pltpu.make_async_remote_copy(src_ref=in_ref, dst_ref=out_ref.at[1 - c], send_sem=send_sem, recv_sem=recv_sem,
                                     device_id=(x, y, 1 - c), device_id_type=MESH).wait_recv()
        cp.wait_send()

    out = pl.pallas_call(
        body, in_specs=[_ANY], out_specs=_ANY, out_shape=jax.ShapeDtypeStruct((2,) + buf.shape, buf.dtype),
        scratch_shapes=[pltpu.SemaphoreType.DMA, pltpu.SemaphoreType.DMA],
        name=name,
    )(buf)
    return lax.dynamic_update_index_in_dim(out, buf, lax.axis_index("c"), 0)


def core_swap(buf, name):
    def body(in_ref, out_ref, send_sem, recv_sem):
        x, y, c = _place()
        cp = pltpu.make_async_remote_copy(src_ref=in_ref, dst_ref=out_ref, send_sem=send_sem, recv_sem=recv_sem,
                                          device_id=(x, y, 1 - c), device_id_type=MESH)
        cp.start()
        cp.wait()

    return pl.pallas_call(
        body, in_specs=[_ANY], out_specs=_ANY, out_shape=jax.ShapeDtypeStruct(buf.shape, buf.dtype),
        scratch_shapes=[pltpu.SemaphoreType.DMA, pltpu.SemaphoreType.DMA],
        name=name,
    )(buf)


def device_gather(buf, name):
    def body(in_ref, out_ref, send_sems, recv_sems, local_sem):
        x, y, c = _place()
        me = 4 * x + 2 * y + c
        mine = pltpu.make_async_copy(in_ref, out_ref.at[me], local_sem)
        mine.start()
        sends = []
        for k in range(1, N_DEV):
            peer = (_flip(x, (k >> 2) & 1), _flip(y, (k >> 1) & 1), _flip(c, k & 1))
            cp = pltpu.make_async_remote_copy(src_ref=in_ref, dst_ref=out_ref.at[me], send_sem=send_sems.at[k - 1],
                                              recv_sem=recv_sems.at[k - 1], device_id=peer, device_id_type=MESH)
            cp.start()
            sends.append(cp)
        for k in range(1, N_DEV):
            peer = (_flip(x, (k >> 2) & 1), _flip(y, (k >> 1) & 1), _flip(c, k & 1))
            pltpu.make_async_remote_copy(src_ref=in_ref, dst_ref=out_ref.at[4 * peer[0] + 2 * peer[1] + peer[2]],
                                         send_sem=send_sems.at[k - 1], recv_sem=recv_sems.at[k - 1], device_id=peer,
                                         device_id_type=MESH).wait_recv()
        for cp in sends:
            cp.wait_send()
        mine.wait()

    return pl.pallas_call(
        body, in_specs=[_ANY], out_specs=_ANY, out_shape=jax.ShapeDtypeStruct((N_DEV,) + buf.shape, buf.dtype),
        scratch_shapes=[pltpu.SemaphoreType.DMA((N_DEV - 1,)), pltpu.SemaphoreType.DMA((N_DEV - 1,)),
                        pltpu.SemaphoreType.DMA],
        name=name,
    )(buf)


def _region(ref, chip_axis=None, chip=None, chip_size=None, half_axis=None, half=None, half_size=None):
    idx = [slice(None)] * len(ref.shape)
    if chip is not None:
        idx[chip_axis] = pl.ds(chip * chip_size, chip_size)
    if half is not None:
        idx[half_axis] = pl.ds(half * half_size, half_size)
    return ref.at[tuple(idx)]


def gather_weights(shards, axes, name, after=()):
    n = len(shards)

    def full_shape(t):
        shp = list(shards[t].shape)
        shp[axes[t][0]] *= N_CHIPS
        return tuple(shp)

    def body(*refs):
        ins, outs = refs[:n], refs[n + len(after):2 * n + len(after)]
        ici_send, ici_recv, d2d_send, d2d_recv, own_send, own_recv = refs[2 * n + len(after):]
        x, y, c = _place()
        me = 2 * x + y

        def part(t, ref, chip, half):
            ca, ha = axes[t]
            return _region(ref, ca, chip, ins[t].shape[ca], ha, half, ins[t].shape[ha] // 2)

        def own(t):
            return pltpu.make_async_remote_copy(src_ref=ins[t], dst_ref=part(t, outs[t], me, None),
                                                send_sem=own_send.at[t], recv_sem=own_recv.at[t],
                                                device_id=(x, y, 1 - c), device_id_type=MESH)

        started = []
        for t in range(n):
            own(t).start()
            started.append(own(t))
        for t in range(n):
            for k in range(1, N_CHIPS):
                px, py = _flip(x, k >> 1), _flip(y, k & 1)
                cp = pltpu.make_async_remote_copy(src_ref=part(t, ins[t], None, c), dst_ref=part(t, outs[t], me, c),
                                                  send_sem=ici_send.at[t, k - 1], recv_sem=ici_recv.at[t, k - 1],
                                                  device_id=(px, py, c), device_id_type=MESH)
                cp.start()
                started.append(cp)
        for t in range(n):
            for k in range(1, N_CHIPS):
                px, py = _flip(x, k >> 1), _flip(y, k & 1)
                got = part(t, outs[t], 2 * px + py, c)
                pltpu.make_async_remote_copy(src_ref=part(t, ins[t], None, c), dst_ref=got,
                                             send_sem=ici_send.at[t, k - 1], recv_sem=ici_recv.at[t, k - 1],
                                             device_id=(px, py, c), device_id_type=MESH).wait_recv()
                fw = pltpu.make_async_remote_copy(src_ref=got, dst_ref=got, send_sem=d2d_send.at[t, k - 1],
                                                  recv_sem=d2d_recv.at[t, k - 1], device_id=(x, y, 1 - c),
                                                  device_id_type=MESH)
                fw.start()
                started.append(fw)
        for t in range(n):
            for k in range(1, N_CHIPS):
                px, py = _flip(x, k >> 1), _flip(y, k & 1)
                theirs = part(t, outs[t], 2 * px + py, 1 - c)
                pltpu.make_async_remote_copy(src_ref=theirs, dst_ref=theirs, send_sem=d2d_send.at[t, k - 1],
                                             recv_sem=d2d_recv.at[t, k - 1], device_id=(x, y, 1 - c),
                                             device_id_type=MESH).wait_recv()
        for t in range(n):
            own(t).wait_recv()
        for cp in started:
            cp.wait_send()

    sem = pltpu.SemaphoreType.DMA((n, N_CHIPS - 1))
    own_sem = pltpu.SemaphoreType.DMA((n,))
    return pl.pallas_call(
        body, in_specs=[_ANY] * (n + len(after)), out_specs=[_ANY] * n,
        out_shape=[jax.ShapeDtypeStruct(full_shape(t), shards[t].dtype) for t in range(n)],
        scratch_shapes=[sem, sem, sem, sem, own_sem, own_sem], name=name,
    )(*shards, *after)


_HBM = pl.BlockSpec(memory_space=pltpu.HBM)
_SEM = pl.BlockSpec(memory_space=pltpu.SEMAPHORE)
_EFFECT = pltpu.SideEffectType.DATAFLOW_SIDE_EFFECTING
WEIGHT_COPIES = N_CHIPS


def _weight_peer(k, x, y, c):
    return (x, y, 1 - c) if k == 0 else (_flip(x, k >> 1), _flip(y, k & 1), c)


def weights_start(shards, items, name, after=()):
    n_sh, n_it = len(shards), len(items)

    def src_of(refs, i):
        t, layer, _ = items[i]
        return refs[t] if layer is None else refs[t].at[layer]

    def land_shape(i):
        t, layer, ca = items[i]
        shp = list(shards[t].shape if layer is None else shards[t].shape[1:])
        shp[ca] *= N_CHIPS
        return tuple(shp)

    def body(*refs):
        shard_refs, land_refs = refs[:n_sh], refs[n_sh:n_sh + n_it]
        first_out = n_sh + n_it + len(after)
        send_sems = refs[first_out:first_out + n_it]
        recv_sems = refs[first_out + n_it:first_out + 2 * n_it]
        token = refs[-1]
        x, y, c = _place()
        me = 2 * x + y
        for i in range(n_it):
            src = src_of(shard_refs, i)
            ca = items[i][2]
            dst = _region(land_refs[i], ca, me, src.shape[ca])
            for k in range(WEIGHT_COPIES):
                pltpu.make_async_remote_copy(src_ref=src, dst_ref=dst, send_sem=send_sems[i], recv_sem=recv_sems[i],
                                             device_id=_weight_peer(k, x, y, c), device_id_type=MESH).start()
        token[...] = jnp.zeros_like(token)

    lands = [pltpu.with_memory_space_constraint(lax.empty(land_shape(i), shards[0].dtype), pltpu.HBM)
             for i in range(n_it)]
    ins = [pltpu.with_memory_space_constraint(a, pltpu.HBM) for a in shards] + lands
    sems = (pltpu.SemaphoreType.DMA(()),) * (2 * n_it)
    outs = pl.pallas_call(
        body, name=name,
        out_shape=sems + tuple(pltpu.HBM(a.shape, a.dtype) for a in ins) + (jax.ShapeDtypeStruct((8, LANES), F32),),
        in_specs=[_HBM] * len(ins) + [_ANY] * len(after),
        out_specs=(_SEM,) * (2 * n_it) + (_HBM,) * len(ins) + (pl.BlockSpec(memory_space=pltpu.VMEM),),
        input_output_aliases={i: 2 * n_it + i for i in range(len(ins))},
        compiler_params=pltpu.CompilerParams(has_side_effects=_EFFECT),
    )(*ins, *after)
    base = 2 * n_it
    return (list(outs[:n_it]), list(outs[n_it:base]), list(outs[base:base + n_sh]),
            list(outs[base + n_sh:base + n_sh + n_it]), outs[-1])


def weights_wait(send_sems, recv_sems, lands, after, keep, name):
    m = len(lands)

    def body(*refs):
        land_refs, send_refs, recv_refs = refs[:m], refs[m:2 * m], refs[2 * m:3 * m]
        x, y, c = _place()
        for j in range(m):
            cp = pltpu.make_async_remote_copy(src_ref=land_refs[j], dst_ref=land_refs[j], send_sem=send_refs[j],
                                              recv_sem=recv_refs[j], device_id=(x, y, 1 - c),
                                              device_id_type=MESH)
            cp.wait_send()
            cp.wait_recv()

    outs = pl.pallas_call(
        body, name=name,
        out_shape=tuple(pltpu.HBM(a.shape, a.dtype) for a in lands),
        in_specs=[_HBM] * m + [_SEM] * (2 * m) + [_ANY] + [_HBM] * len(keep),
        out_specs=(_HBM,) * m,
        input_output_aliases={j: j for j in range(m)},
        compiler_params=pltpu.CompilerParams(has_side_effects=_EFFECT),
    )(*lands, *send_sems, *recv_sems, after, *keep)
    return list(outs)


def reduce_to_sibling(lo, hi, name):
    n = len(lo)

    def body(*refs):
        los, his, outs = refs[:n], refs[n:2 * n], refs[2 * n:3 * n]
        send_sems, recv_sems = refs[3 * n:]
        x, y, c = _place()

        def copy(u, src):
            return pltpu.make_async_remote_copy(src_ref=src, dst_ref=outs[u], send_sem=send_sems.at[u],
                                                recv_sem=recv_sems.at[u], device_id=(x, y, 1 - c), device_id_type=MESH)

        for u in range(n):
            @pl.when(c == 0)
            def _(u=u):
                copy(u, his[u]).start()

            @pl.when(c == 1)
            def _(u=u):
                copy(u, los[u]).start()
        for u in range(n):
            copy(u, los[u]).wait_recv()
        for u in range(n):
            copy(u, los[u]).wait_send()

    return pl.pallas_call(
        body, in_specs=[_ANY] * (2 * n), out_specs=[_ANY] * n,
        out_shape=[jax.ShapeDtypeStruct(a.shape, a.dtype) for a in lo],
        scratch_shapes=[pltpu.SemaphoreType.DMA((n,)), pltpu.SemaphoreType.DMA((n,))], name=name,
    )(*lo, *hi)


def add_selected(lo, hi, other, name, tile_elems=1 << 19):
    R, C = lo.shape
    tr = _pick(R, max(16, tile_elems // C // 16 * 16), 16)

    def body(lo_ref, hi_ref, o_ref, out_ref):
        mine = jnp.where(lax.axis_index("c") == 0, lo_ref[...].astype(F32), hi_ref[...].astype(F32))
        out_ref[...] = (mine + o_ref[...].astype(F32)).astype(out_ref.dtype)

    blk = pl.BlockSpec((tr, C), lambda i: (i, 0))
    return pl.pallas_call(
        body, grid=(R // tr,), in_specs=[blk, blk, blk], out_specs=blk, out_shape=jax.ShapeDtypeStruct((R, C), BF16),
        compiler_params=_cparams("parallel"), name=name,
    )(lo, hi, other)


def scatter_to_chips(pieces, chip_axes, name):
    n = len(pieces)

    def block_shape(u):
        shp = list(pieces[u].shape)
        shp[chip_axes[u]] //= N_CHIPS
        return tuple(shp)

    def body(*refs):
        ins, outs = refs[:n], refs[n:2 * n]
        send_sems, recv_sems = refs[2 * n:]
        x, y, c = _place()
        me = 2 * x + y
        started = []
        for u in range(n):
            size = block_shape(u)[chip_axes[u]]
            for k in range(1, N_CHIPS):
                px, py = _flip(x, k >> 1), _flip(y, k & 1)
                cp = pltpu.make_async_remote_copy(src_ref=_region(ins[u], chip_axes[u], 2 * px + py, size),
                                                  dst_ref=outs[u].at[me], send_sem=send_sems.at[u, k - 1],
                                                  recv_sem=recv_sems.at[u, k - 1], device_id=(px, py, c),
                                                  device_id_type=MESH)
                cp.start()
                started.append(cp)
        for u in range(n):
            size = block_shape(u)[chip_axes[u]]
            for k in range(1, N_CHIPS):
                px, py = _flip(x, k >> 1), _flip(y, k & 1)
                pltpu.make_async_remote_copy(src_ref=_region(ins[u], chip_axes[u], me, size),
                                             dst_ref=outs[u].at[2 * px + py], send_sem=send_sems.at[u, k - 1],
                                             recv_sem=recv_sems.at[u, k - 1], device_id=(px, py, c),
                                             device_id_type=MESH).wait_recv()
        for cp in started:
            cp.wait_send()

    sem = pltpu.SemaphoreType.DMA((n, N_CHIPS - 1))
    return pl.pallas_call(
        body, in_specs=[_ANY] * n, out_specs=[_ANY] * n,
        out_shape=[jax.ShapeDtypeStruct((N_CHIPS,) + block_shape(u), pieces[u].dtype) for u in range(n)],
        scratch_shapes=[sem, sem], name=name,
    )(*pieces)


def scatter_start(pieces, chip_axes, name, after=()):
    n = len(pieces)

    def block_shape(u):
        shp = list(pieces[u].shape)
        shp[chip_axes[u]] //= N_CHIPS
        return tuple(shp)

    def body(*refs):
        ins, land_refs = refs[:n], refs[n:2 * n]
        first_out = 2 * n + len(after)
        send_sems, recv_sems = refs[first_out:first_out + n], refs[first_out + n:first_out + 2 * n]
        token = refs[-1]
        x, y, c = _place()
        me = 2 * x + y
        for u in range(n):
            size = block_shape(u)[chip_axes[u]]
            for k in range(1, N_CHIPS):
                px, py = _flip(x, k >> 1), _flip(y, k & 1)
                pltpu.make_async_remote_copy(src_ref=_region(ins[u], chip_axes[u], 2 * px + py, size),
                                             dst_ref=land_refs[u].at[me], send_sem=send_sems[u], recv_sem=recv_sems[u],
                                             device_id=(px, py, c), device_id_type=MESH).start()
        token[...] = jnp.zeros_like(token)

    lands = [pltpu.with_memory_space_constraint(lax.empty((N_CHIPS,) + block_shape(u), pieces[u].dtype), pltpu.HBM)
             for u in range(n)]
    ins = [pltpu.with_memory_space_constraint(a, pltpu.HBM) for a in pieces] + lands
    sems = (pltpu.SemaphoreType.DMA(()),) * (2 * n)
    outs = pl.pallas_call(
        body, name=name,
        out_shape=sems + tuple(pltpu.HBM(a.shape, a.dtype) for a in ins) + (jax.ShapeDtypeStruct((8, LANES), F32),),
        in_specs=[_HBM] * len(ins) + [_ANY] * len(after),
        out_specs=(_SEM,) * (2 * n) + (_HBM,) * len(ins) + (pl.BlockSpec(memory_space=pltpu.VMEM),),
        input_output_aliases={i: 2 * n + i for i in range(len(ins))},
        compiler_params=pltpu.CompilerParams(has_side_effects=_EFFECT),
    )(*ins, *after)
    return list(outs[:n]), list(outs[n:2 * n]), list(outs[2 * n:3 * n]), list(outs[3 * n:4 * n]), outs[-1]


def scatter_wait(send_sems, recv_sems, lands, pieces, after, name):
    n = len(lands)

    def body(*refs):
        land_refs, send_refs, recv_refs = refs[:n], refs[n:2 * n], refs[2 * n:3 * n]
        x, y, c = _place()
        for u in range(n):
            three = land_refs[u].at[pl.ds(0, N_CHIPS - 1)]
            cp = pltpu.make_async_remote_copy(src_ref=three, dst_ref=three, send_sem=send_refs[u], recv_sem=recv_refs[u],
                                              device_id=(x, y, 1 - c), device_id_type=MESH)
            cp.wait_send()
            cp.wait_recv()

    outs = pl.pallas_call(
        body, name=name,
        out_shape=tuple(pltpu.HBM(a.shape, a.dtype) for a in lands),
        in_specs=[_HBM] * n + [_SEM] * (2 * n) + [_ANY] + [_HBM] * len(pieces),
        out_specs=(_HBM,) * n,
        input_output_aliases={j: j for j in range(n)},
        compiler_params=pltpu.CompilerParams(has_side_effects=_EFFECT),
    )(*lands, *send_sems, *recv_sems, after, *pieces)
    return list(outs)


def gather_halves(parts, slots, out_shapes, name):
    n = len(parts)

    def body(*refs):
        ins, outs = refs[:n], refs[n:n + len(out_shapes)]
        send_sems, recv_sems = refs[n + len(out_shapes):]
        x, y, c = _place()
        started = []
        for u in range(n):
            t, s = slots[u]
            cp = pltpu.make_async_remote_copy(src_ref=ins[u], dst_ref=outs[t].at[s, c], send_sem=send_sems.at[u],
                                              recv_sem=recv_sems.at[u], device_id=(x, y, 1 - c), device_id_type=MESH)
            cp.start()
            started.append(cp)
        for u in range(n):
            t, s = slots[u]
            pltpu.make_async_remote_copy(src_ref=ins[u], dst_ref=outs[t].at[s, 1 - c], send_sem=send_sems.at[u],
                                         recv_sem=recv_sems.at[u], device_id=(x, y, 1 - c),
                                         device_id_type=MESH).wait_recv()
        for cp in started:
            cp.wait_send()

    return pl.pallas_call(
        body, in_specs=[_ANY] * n, out_specs=[_ANY] * len(out_shapes),
        out_shape=[jax.ShapeDtypeStruct(shp, F32) for shp in out_shapes],
        scratch_shapes=[pltpu.SemaphoreType.DMA((n,)), pltpu.SemaphoreType.DMA((n,))], name=name,
    )(*parts)


WEIGHT_ORDER = ["mod_w", "mod_b", "norm1_g", "norm2_g", "pool_w", "pool_b", "pool_scale", "kv_in_g", "w_dkv",
                "ckv_norm_g", "w_uk", "w_uv", "w_dq", "q_norm_g", "w_uq", "w_o", "w_up", "conv_w", "conv_b", "w_down",
                "final_g"]
EXCHANGED = {"w_up": (2, 0), "w_down": (1, 0), "w_o": (1, 0), "w_uq": (2, 0), "w_dq": (1, 0), "pool_w": (2, 0),
             "w_dkv": (0, 1), "w_uk": (1, 0), "w_uv": (1, 0)}
SMALL_SHARDED = {"conv_w": 2, "pool_b": 1, "pool_scale": 1}
REPLICATED = ["mod_b", "norm1_g", "norm2_g", "kv_in_g", "ckv_norm_g", "q_norm_g", "conv_b", "final_g"]


def _padded(n, align):
    return -(-n // align) * align


def _flat_pad(parts, total):
    flat = jnp.concatenate(parts, axis=-1)
    pad = total - flat.shape[-1]
    if pad:
        flat = jnp.concatenate([flat, jnp.zeros(flat.shape[:-1] + (pad,), flat.dtype)], axis=-1)
    return flat


def _split_shards(full, axis):
    shp = full.shape
    t = full.reshape(shp[:axis] + (N_CHIPS, shp[axis] // N_CHIPS) + shp[axis + 1:])
    return jnp.moveaxis(t, axis, 0).reshape(N_CHIPS, -1)


def _join_shards(rows, shard_shape, axis):
    t = jnp.moveaxis(rows.reshape((N_CHIPS,) + tuple(shard_shape)), 0, axis)
    return t.reshape(tuple(shard_shape[:axis]) + (N_CHIPS * shard_shape[axis],) + tuple(shard_shape[axis + 1:]))


def _index(a, i, axis=0):
    return lax.dynamic_index_in_dim(a, i, axis, keepdims=False)


def kernel(x, c, positions, mod_w, mod_b, norm1_g, norm2_g, pool_w, pool_b, pool_scale, kv_in_g, w_dkv, ckv_norm_g, w_uk, w_uv, w_dq, q_norm_g, w_uq, w_o, w_up, conv_w, conv_b, w_down, final_g, loss_target, m_mod_w, m_mod_b, m_norm1_g, m_norm2_g, m_pool_w, m_pool_b, m_pool_scale, m_kv_in_g, m_w_dkv, m_ckv_norm_g, m_w_uk, m_w_uv, m_w_dq, m_q_norm_g, m_w_uq, m_w_o, m_w_up, m_conv_w, m_conv_b, m_w_down, m_final_g, v_mod_w, v_mod_b, v_norm1_g, v_norm2_g, v_pool_w, v_pool_b, v_pool_scale, v_kv_in_g, v_w_dkv, v_ckv_norm_g, v_w_uk, v_w_uv, v_w_dq, v_q_norm_g, v_w_uq, v_w_o, v_w_up, v_conv_w, v_conv_b, v_w_down, v_final_g):
    given = dict(locals())
    W = {n: given[n] for n in WEIGHT_ORDER}
    M1 = {n: given["m_" + n] for n in WEIGHT_ORDER}
    V2 = {n: given["v_" + n] for n in WEIGHT_ORDER}
    xi, yi, ci = lax.axis_index("x"), lax.axis_index("y"), lax.axis_index("c")
    chip = 2 * xi + yi
    dev = 4 * xi + 2 * yi + ci
    x0 = x[0]
    S_, D = x0.shape
    Fh = conv_b.shape[1]
    E = mod_b.shape[1]
    Es = E // N_CHIPS
    zD = jnp.zeros((D,), F32)

    c_all = device_gather(c, "gather_c").reshape(N_DEV, D)
    c_pad = jnp.concatenate([c_all, jnp.zeros((16 - N_DEV, D), F32)], axis=0)
    mod_b_mine = lax.dynamic_slice_in_dim(mod_b, chip * Es, Es, axis=1)
    mods_part = mods_fwd(c_pad, mod_w, mod_b_mine, "mods_fwd")
    mods_all = chip_gather(mods_part, "gather_mods")
    mods = jnp.swapaxes(_index(mods_all, dev, axis=2), 0, 1).reshape(DEPTH, E)
    mod = [[mods[l, k * D:(k + 1) * D] for k in range(6)] for l in range(DEPTH)]

    full = {}
    ssz = {n: math.prod(W[n].shape) for n in SMALL_SHARDED}
    Tw = _padded(sum(ssz.values()), 8 * PACK_COLS)
    small_rows = chip_gather(_flat_pad([W[n].reshape(-1) for n in SMALL_SHARDED], Tw).reshape(-1, PACK_COLS),
                             "gather_small_w").reshape(N_CHIPS, Tw)
    off = 0
    for n, axis in SMALL_SHARDED.items():
        full[n] = _join_shards(small_rows[:, off:off + ssz[n]], W[n].shape, axis)
        off += ssz[n]

    names = list(EXCHANGED)
    shards = [W[n].astype(BF16) for n in names]
    n_mla = DEPTH - N_A
    first_axes = {"w_up": (1, 0), "w_down": (0, 1), "pool_w": (1, 0)}
    first = gather_weights([shards[names.index(n)][0] for n in first_axes], list(first_axes.values()), "gather_weights0",
                           after=[mods, small_rows])
    for n, arr in zip(first_axes, first):
        full[(n, 0)] = arr
    items, groups = [], []

    def group(entries):
        groups.append(list(range(len(items), len(items) + len(entries))))
        for n, layer in entries:
            ca = EXCHANGED[n][0] - (0 if layer is None else 1)
            items.append((names.index(n), layer, 0 if n == "w_dkv" else ca))

    for l in range(1, N_A):
        group([("w_up", l), ("w_down", l), ("pool_w", l)])
    for j in range(n_mla):
        head = [("w_dkv", None), ("w_uk", None), ("w_uv", None)] if j == 0 else []
        group(head + [("w_dq", j), ("w_uq", j), ("w_o", j), ("w_up", N_A + j), ("w_down", N_A + j)])
    w_send, w_recv, shards_thru, lands, _ = weights_start(shards, items, "weights_start", after=first)

    def weights_ready(g, after):
        keep = shards_thru if g == len(groups) - 1 else []
        got = weights_wait([w_send[i] for i in groups[g]], [w_recv[i] for i in groups[g]], [lands[i] for i in groups[g]],
                           after, keep, f"weights_wait{g}")
        for i, arr in zip(groups[g], got):
            t, layer, _ = items[i]
            full[(names[t], 0 if layer is None else layer)] = arr

    q_rank = W["w_uq"].shape[1]
    kv_w = KV_RANK + QK_ROPE

    def uq_ext(j):
        wq = full[("w_uq", j)].reshape(q_rank, N_HEADS, QK_HEAD)
        return jnp.concatenate([wq, jnp.zeros((q_rank, N_HEADS, HEAD_PAD - QK_HEAD), BF16)],
                               axis=2).reshape(q_rank, N_HEADS * HEAD_PAD)


    half = QK_ROPE // 2
    inv = 1.0 / (ROPE_THETA ** (jnp.arange(0, QK_ROPE, 2, dtype=F32) / QK_ROPE))
    inv_row = jnp.concatenate([inv, inv, jnp.zeros((LANES - 2 * half,), F32)]).reshape(1, LANES)
    tabs = rope_tables(positions[0].astype(F32).reshape(S_, 1), inv_row, "rope_tables")
    att_scale = QK_HEAD ** -0.5

    saved = []
    xcur = x0
    kv_saved = None
    K = VX = knv = None
    for l in range(DEPTH):
        sh1, sc1, g1, sh2, sc2, g2 = mod[l]
        st = {"xin": xcur}
        if l:
            weights_ready(l - 1, xcur)
        if l == N_A:
            w_dkv_ext = jnp.concatenate([full[("w_dkv", 0)], jnp.zeros((D, KV_RANK + LANES - kv_w), BF16)], axis=1)
            w_ukv = jnp.concatenate([full[("w_uk", 0)], full[("w_uv", 0)]], axis=1)
            xn = norm_fwd(xcur, kv_in_g, zD, zD, BF16, "kvin_fwd")
            kv_ext = mm(xn, w_dkv_ext, "nn", F32, "dkv_mm")
            lat = kv_ext[:, :KV_RANK]
            zk = jnp.zeros((KV_RANK,), F32)
            ckv = norm_fwd(lat, ckv_norm_g, zk, zk, BF16, "ckv_fwd")
            K, VX = kv_proj(ckv, w_ukv, kv_ext, tabs, "ukv_mm")
            kv_saved = {"x": xcur, "xn": xn, "lat": lat, "ckv": ckv}
        if l < N_A:
            h1 = norm_fwd(xcur, norm1_g[l], sc1, sh1, F32, f"norm1_fwd{l}")
            st["pooled"] = _pool_call(h1, BF16, f"pool_fwd{l}", False)
            st["cs"] = g1 * full["pool_scale"][l]
            st["ypre"], xmid = gmm(st["pooled"], full[("pool_w", l)], "nn", BF16, f"pool_mm{l}", bias=full["pool_b"][l],
                                   res=xcur, colscale=st["cs"])
        else:
            j = l - N_A
            st["h1"] = norm_fwd(xcur, norm1_g[l], sc1, sh1, BF16, f"norm1_fwd{l}")
            st["ql"] = mm(st["h1"], full[("w_dq", j)], "nn", F32, f"dq_mm{l}")
            st["cq"] = norm_fwd(st["ql"], q_norm_g[j], jnp.zeros_like(q_norm_g[j]), jnp.zeros_like(q_norm_g[j]), BF16,
                                f"qnorm_fwd{l}")
            st["w_uq_ext"] = uq_ext(j)
            st["Q"] = q_proj(st["cq"], st["w_uq_ext"], tabs, att_scale, f"uq_mm{l}")
            st["o"], lse = attn_fwd(st["Q"], K, VX, f"attn_fwd{l}")
            st["lse"] = lse.reshape(N_HEADS, 1, S_)
            st["y"], xmid = mm(st["o"], full[("w_o", j)], "nn", BF16, f"wo_mm{l}", res=xcur, colscale=g1)
        st["xmid"] = xmid
        st["h2"] = norm_fwd(xmid, norm2_g[l], sc2, sh2, BF16, f"norm2_fwd{l}")
        st["u"] = mm(st["h2"], full[("w_up", l)], "nn", BF16, f"up_mm{l}")
        st["z"] = glu_fwd(st["u"], full["conv_w"][l], conv_b[l], f"glu_fwd{l}")
        st["f"], xcur = mm(st["z"], full[("w_down", l)], "nn", BF16, f"down_mm{l}", tk=1408, res=xmid, colscale=g2)
        saved.append(st)

    dx, d_final_g, loss_part = loss_head(xcur, final_g, loss_target[0], "loss_head")
    loss = lax.psum(loss_part[0, 0], ("x", "y", "c"))

    def begin_reduce(tensors, first_slot, tag):
        units = []
        for n in tensors:
            ca = EXCHANGED[n][0]
            if W[n].ndim > 2:
                n_slots = W[n].shape[0] // 2
                for sl in range(first_slot if n_slots > 1 else 0, first_slot + 1 if n_slots > 1 else 1):
                    units.append((n, sl, G[(n, 2 * sl)], G[(n, 2 * sl + 1)], ca - 1))
            elif n == "w_dkv":
                g4 = G[(n, 0)].reshape(N_CHIPS, 2, -1, kv_w)
                units.append((n, 0, g4[:, 0], g4[:, 1], 0))
            else:
                rows_half = W[n].shape[0] // 2
                units.append((n, 0, G[(n, 0)][:rows_half], G[(n, 0)][rows_half:], ca))
        lo = [u[2] for u in units]
        hi = [u[3] for u in units]
        theirs = reduce_to_sibling(lo, hi, f"reduce_cores_{tag}")
        sums = [add_selected(l_.reshape(-1, l_.shape[-1]), h_.reshape(-1, l_.shape[-1]), t_.reshape(-1, l_.shape[-1]),
                             f"reduce_cores_add_{tag}{i}").reshape(l_.shape)
                for i, (l_, h_, t_) in enumerate(zip(lo, hi, theirs))]
        return units, sums, [u[4] for u in units]

    G = {}
    dmods = [None] * DEPTH
    d_norm1 = [None] * DEPTH
    d_norm2 = [None] * DEPTH
    d_conv_b = [None] * DEPTH
    d_qnorm = [None] * n_mla
    dkv_acc = []
    df, a2, _ = gate_bwd(dx, saved[DEPTH - 1]["f"], mod[DEPTH - 1][5], f"gate2_bwd{DEPTH - 1}")
    for l in reversed(range(DEPTH)):
        sh1, sc1, g1, sh2, sc2, g2 = mod[l]
        st = saved[l]
        next_gate = (saved[l - 1]["f"], mod[l - 1][5]) if l else None
        dz = mm(df, full[("w_down", l)], "nt", BF16, f"down_dx{l}")
        G[("w_down", l)] = mm(st["z"], df, "tn", BF16, f"down_dw{l}")
        du, dcw, dcb = glu_bwd(st["u"], dz, full["conv_w"][l], conv_b[l], f"glu_bwd{l}")
        G[("conv_w", l)] = dcw
        d_conv_b[l] = dcb[0]
        dh2 = mm(du, full[("w_up", l)], "nt", BF16, f"up_dx{l}", tk=1408)
        G[("w_up", l)] = mm(st["h2"], du, "tn", BF16, f"up_dw{l}")
        dxmid, s1, s2, dgate, a1, csum = norm_bwd(st["xmid"], norm2_g[l], sc2, dh2, dx, f"norm2_bwd{l}",
                                                  gate=(st["ypre"], st["cs"]) if l < N_A else (st["y"], g1))
        dsh2, dsc2, d_norm2[l] = s1[0], s2[0] * norm2_g[l], s2[0] * (1.0 + sc2)
        if l < N_A:
            dyp = dgate
            dg1 = full["pool_scale"][l] * a1[0]
            G[("pool_scale", l)] = g1 * a1[0]
            G[("pool_b", l)] = st["cs"] * csum[0]
            dpooled = gmm(dyp, full[("pool_w", l)], "nt", F32, f"pool_dx{l}")
            G[("pool_w", l)] = gmm(st["pooled"], dyp, "tn", BF16, f"pool_dw{l}")
            dh1 = _pool_call(dpooled, F32, f"pool_bwd{l}", True)
        else:
            j = l - N_A
            dy = dgate
            dg1 = a1[0]
            do, delta = o_proj_bwd(dy, full[("w_o", j)], st["o"], f"wo_dx{l}")
            delta = delta.reshape(N_HEADS, 1, S_)
            G[("w_o", j)] = mm(st["o"], dy, "tn", BF16, f"wo_dw{l}")
            dQ, dK, dV = attn_bwd(st["Q"], K, VX, do, st["lse"], delta, f"attn_bwd{l}")
            dkv_acc.append((dK, dV))
            dcq, dw_ext = q_proj_bwd(dQ, st["cq"], st["w_uq_ext"], tabs, att_scale, f"uq_bwd{l}")
            G[("w_uq", j)] = dw_ext.reshape(q_rank, N_HEADS, HEAD_PAD)[:, :, :QK_HEAD].reshape(q_rank, N_HEADS * QK_HEAD)
            zq = jnp.zeros_like(q_norm_g[j])
            dql, _, s2q = norm_bwd(st["ql"], q_norm_g[j], zq, dcq, None, f"qnorm_bwd{l}")
            d_qnorm[j] = s2q[0]
            dh1 = mm(dql, full[("w_dq", j)], "nt", BF16, f"dq_dx{l}")
            G[("w_dq", j)] = mm(st["h1"], dql, "tn", BF16, f"dq_dw{l}")
        a2_mine = a2
        if l and l != N_A:
            dx, s1, s2, df, a2, _ = norm_bwd(st["xin"], norm1_g[l], sc1, dh1, dxmid, f"norm1_bwd{l}", gate=next_gate)
        else:
            dx, s1, s2 = norm_bwd(st["xin"], norm1_g[l], sc1, dh1, dxmid, f"norm1_bwd{l}")
        dsh1, dsc1, d_norm1[l] = s1[0], s2[0] * norm1_g[l], s2[0] * (1.0 + sc1)
        dmods[l] = jnp.concatenate([dsh1, dsc1, dg1, dsh2, dsc2, a2_mine[0]])
        if l == N_A:
            (dk_a, dv_a), (dk_b, dv_b) = dkv_acc
            dknv, d_tk = k_prep_bwd(dk_a, dk_b, dv_a, dv_b, tabs, "k_prep_bwd")
            dckv = mm(dknv, w_ukv, "nt", F32, "ukv_dx")
            d_ukv = mm(kv_saved["ckv"], dknv, "tn", BF16, "ukv_dw")
            G[("w_uk", 0)], G[("w_uv", 0)] = d_ukv[:, :N_HEADS * QK_NOPE], d_ukv[:, N_HEADS * QK_NOPE:]
            zk = jnp.zeros((KV_RANK,), F32)
            dlat, _, s2c = norm_bwd(kv_saved["lat"], ckv_norm_g, zk, dckv, None, "ckv_bwd")
            d_ckv_g = s2c[0]
            dkv_ext = jnp.concatenate([dlat, d_tk], axis=1)
            dxn = mm(dkv_ext, w_dkv_ext, "nt", BF16, "dkv_dx")
            G[("w_dkv", 0)] = mm(kv_saved["xn"], dkv_ext, "tn", BF16, "dkv_dw")[:, :kv_w]
            dx, _, s2k, df, a2, _ = norm_bwd(kv_saved["x"], kv_in_g, zD, dxn, dx, "kvin_bwd", gate=next_gate)
            d_kvin_g = s2k[0]
            e_units, e_sums, e_axes = begin_reduce([n for n in EXCHANGED if n != "pool_w"], 1, "early")
            e_send, e_recv, e_pieces, e_lands, e_token = scatter_start(e_sums, e_axes, "reduce_chips_start")
            early = (e_units, e_send, e_recv, e_lands, e_pieces, e_axes)
            mod[l - 1][4] = mod[l - 1][4] + e_token[0, 0]

    small = {"mod_b": jnp.stack(dmods), "norm1_g": jnp.stack(d_norm1), "norm2_g": jnp.stack(d_norm2),
             "kv_in_g": d_kvin_g, "ckv_norm_g": d_ckv_g, "q_norm_g": jnp.stack(d_qnorm),
             "conv_b": jnp.stack(d_conv_b), "final_g": d_final_g[0]}
    extra = {n: jnp.stack([G[(n, i)] for i in range(W[n].shape[0])]) for n in SMALL_SHARDED}
    ssizes = {n: math.prod(W[n].shape) for n in REPLICATED}
    esizes = {n: math.prod(extra[n].shape) for n in SMALL_SHARDED}
    Ts = _padded(sum(ssizes.values()) + sum(esizes.values()), 8 * PACK_COLS)

    def pack_small(d, tail=()):
        return _flat_pad([d[n].reshape(-1) for n in REPLICATED] + [t.reshape(-1) for t in tail],
                         Ts).reshape(Ts // PACK_COLS, PACK_COLS)

    parts = device_gather(pack_small(small, [extra[n] for n in SMALL_SHARDED]), "gather_small")

    l_units, l_sums, l_axes = begin_reduce([n for n in EXCHANGED if W[n].ndim > 2 and W[n].shape[0] == DEPTH] + ["pool_w"],
                                           0, "late")
    l_send, l_recv, l_pieces, l_lands, l_token = scatter_start(l_sums, l_axes, "reduce_chips_late_start", after=[parts])
    parts = parts + l_token[0, 0]

    grads, deltas, new_m, new_v = {}, {}, {}, {}
    outs = adamw_sum(parts, pack_small(W), pack_small(M1), pack_small(V2), "adamw_small")
    off = 0
    for n in REPLICATED:
        for dst, o in zip((grads, deltas, new_m, new_v), outs):
            dst[n] = o.reshape(-1)[off:off + ssizes[n]].reshape(W[n].shape)
        off += ssizes[n]
    for n, axis in SMALL_SHARDED.items():
        g_full = outs[0].reshape(-1)[off:off + esizes[n]].reshape(extra[n].shape)
        off += esizes[n]
        size = W[n].shape[axis]
        grads[n] = lax.dynamic_slice_in_dim(g_full, chip * size, size, axis=axis)
        deltas[n], new_m[n], new_v[n] = adamw(W[n], grads[n], M1[n], V2[n], f"adamw_{n}")

    dm_all = parts.reshape(N_DEV, -1)[:, :DEPTH * E].reshape(N_DEV, DEPTH, E)
    dm_mine = jnp.swapaxes(lax.dynamic_slice_in_dim(dm_all, chip * Es, Es, axis=2), 0, 1)
    grads["mod_w"], deltas["mod_w"], new_m["mod_w"], new_v["mod_w"] = adamw_modw(
        c_all.reshape(N_DEV, D, 1), dm_mine, mod_w, m_mod_w, v_mod_w, "adamw_mod_w")

    def finish_reduce(pieces, axes, got, tag):
        out = []
        for i, (sm, ax, g4) in enumerate(zip(pieces, axes, got)):
            size = sm.shape[ax] // N_CHIPS
            g4 = lax.dynamic_update_index_in_dim(g4, lax.dynamic_slice_in_dim(sm, chip * size, size, axis=ax), chip, 0)
            blk = g4.shape[1:]
            out.append(sum_parts(g4.reshape(N_CHIPS, -1, blk[-1]), f"reduce_chips_add_{tag}{i}").reshape(blk))
        return out

    e_units, e_send, e_recv, e_lands, e_pieces, e_axes = early
    early_got = scatter_wait(e_send, e_recv, e_lands, e_pieces, dx, "reduce_chips_wait")
    reduced = finish_reduce(e_pieces, e_axes, early_got, "early")
    late_got = scatter_wait(l_send, l_recv, l_lands, l_pieces, new_v["mod_w"], "reduce_chips_late_wait")
    reduced += finish_reduce(l_pieces, l_axes, late_got, "late")
    units = e_units + l_units
    slots, out_shapes = [], []
    for n in EXCHANGED:
        mine = [i for i, u in enumerate(units) if u[0] == n]
        out_shapes.append((len(mine), 2) + reduced[mine[0]].shape)
        slots += [(len(out_shapes) - 1, units[i][1]) for i in mine]
    order = [i for n in EXCHANGED for i, u in enumerate(units) if u[0] == n]
    halves = gather_halves([reduced[i] for i in order], slots, out_shapes, "reduce_gather")
    for ti, n in enumerate(EXCHANGED):
        g = halves[ti]
        for i, u in enumerate(units):
            if u[0] == n:
                g = lax.dynamic_update_slice(g, reduced[i][None, None], (u[1], ci) + (0,) * reduced[i].ndim)
        grads[n] = g.reshape(W[n].shape)
        deltas[n], new_m[n], new_v[n] = adamw(W[n], grads[n], M1[n], V2[n], f"adamw_{n}")

    return (loss, dx.reshape(x.shape), *[grads[n] for n in WEIGHT_ORDER], *[deltas[n] for n in WEIGHT_ORDER],
            *[new_m[n] for n in WEIGHT_ORDER], *[new_v[n] for n in WEIGHT_ORDER])
```
